```python
import jax, jax.numpy as jnp
from jax import lax
import numpy as np

D_MODEL = 1024
BATCH = 8
SEQ = 4096
DEPTH = 4

HEAD_DIM = 64
HEADS_PER_GROUP = 8
DILATION_GROUPS = ((128, 1), (512, 4), (2048, 16))
N_GROUPS = len(DILATION_GROUPS)
N_ATTN_HEADS = N_GROUPS * HEADS_PER_GROUP
QKV_WIDTH = N_ATTN_HEADS * HEAD_DIM
ATTN_OUT = HEADS_PER_GROUP * HEAD_DIM
CONV_WIDTH = D_MODEL
CONV_K = 3
D_FF = ((8 * D_MODEL // 3) + 127) // 128 * 128
IN_WIDTH = 3 * QKV_WIDTH + 3 * CONV_WIDTH + 2 * D_MODEL
NUM_BUCKETS = 32
MAX_DISTANCE = 2048
BLOCK = 128
N_SUB = 3
EPS = 1e-6
NEG_INF = -1e30

kernel_name = "hybrid_macaron_conv_dilated_attn"


def _t5_bucket(dist):
    exact = NUM_BUCKETS // 2
    d = np.maximum(dist, 1).astype(np.float32)
    large = exact + (np.log(d / exact) / np.log(MAX_DISTANCE / exact) * (NUM_BUCKETS - exact)).astype(np.int32)
    large = np.minimum(large, NUM_BUCKETS - 1)
    return np.where(dist < exact, dist, large).astype(np.int32)


def _rmsnorm(x, g):
    xf = x.astype(jnp.float32)
    y = xf * lax.rsqrt(jnp.mean(xf * xf, axis=-1, keepdims=True) + EPS) * g.astype(jnp.float32)
    return y.astype(x.dtype)


def _swiglu(h, w_gate, w_up, w_down):
    return (jax.nn.silu(h @ w_gate) * (h @ w_up)) @ w_down


def _causal_dwconv(u, w):
    rhs = w.astype(u.dtype).reshape(CONV_K, 1, u.shape[-1])
    return lax.conv_general_dilated(u, rhs, window_strides=(1,), padding=[(CONV_K - 1, 0)],
                                    dimension_numbers=("NWC", "WIO", "NWC"),
                                    feature_group_count=u.shape[-1])


def _dilated_window_attention(q, k, v, bias_tab, window, dilation):
    B, S, H, E = q.shape
    span = window // dilation
    L = S // dilation
    nb = -(-L // BLOCK)
    Lp = nb * BLOCK

    def to_sub(t):
        t = t.reshape(B, L, dilation, H, E)
        return jnp.pad(t, ((0, 0), (0, Lp - L), (0, 0), (0, 0), (0, 0)))

    qs, ks, vs = to_sub(q), to_sub(k), to_sub(v)
    qb = qs.reshape(B, nb, BLOCK, dilation, H, E)

    def key_blocks(t):
        tp = jnp.pad(t, ((0, 0), (BLOCK, 0), (0, 0), (0, 0), (0, 0)))
        prev = tp[:, :Lp].reshape(B, nb, BLOCK, dilation, H, E)
        cur = t.reshape(B, nb, BLOCK, dilation, H, E)
        return jnp.concatenate([prev, cur], axis=2)

    kb, vb = key_blocks(ks), key_blocks(vs)

    i = np.arange(BLOCK)[:, None]
    j = np.arange(2 * BLOCK)[None, :]
    rel = i - j + BLOCK
    band = (rel >= 0) & (rel <= span)
    first_ok = (np.arange(nb)[:, None, None] > 0) | (j[None] >= BLOCK)
    mask = jnp.asarray(band[None] & first_ok)[None, :, None, None]
    bucket = jnp.asarray(_t5_bucket(np.maximum(rel, 0) * dilation))
    bias = jnp.transpose(bias_tab[bucket].astype(jnp.float32), (2, 0, 1))

    logits = jnp.einsum("bnqrhe,bnkrhe->bnrhqk", qb, kb).astype(jnp.float32) * (HEAD_DIM ** -0.5)
    logits = jnp.where(mask, logits + bias, NEG_INF)
    m = jnp.max(logits, axis=-1, keepdims=True)
    p = jnp.exp(logits - m)
    s = jnp.sum(p, axis=-1)
    o = jnp.einsum("bnrhqk,bnkrhe->bnqrhe", p, vb.astype(jnp.float32))
    s_q = jnp.transpose(s, (0, 1, 4, 2, 3))
    o = o / s_q[..., None]
    lse = jnp.transpose(m[..., 0] + jnp.log(s), (0, 1, 4, 2, 3))
    o = o.reshape(B, Lp, dilation, H, E)[:, :L].reshape(B, S, H, E)
    lse = lse.reshape(B, Lp, dilation, H)[:, :L].reshape(B, S, H)
    return o, lse


def _mixer(h, w_in, conv_w, w_conv_out, w_attn_out, w_o, rel_bias):
    B, S, _ = h.shape
    u = h @ w_in
    splits = np.cumsum([QKV_WIDTH, QKV_WIDTH, QKV_WIDTH, CONV_WIDTH, CONV_WIDTH, CONV_WIDTH, D_MODEL])
    q, k, v, cb, cc, ch, g_conv, g_attn = jnp.split(u, splits, axis=-1)

    y_conv = (cb * _causal_dwconv(cc * ch, conv_w)) @ w_conv_out

    q = q.reshape(B, S, N_GROUPS, HEADS_PER_GROUP, HEAD_DIM)
    k = k.reshape(B, S, N_GROUPS, HEADS_PER_GROUP, HEAD_DIM)
    v = v.reshape(B, S, N_GROUPS, HEADS_PER_GROUP, HEAD_DIM)
    outs, lses = [], []
    for g, (window, dilation) in enumerate(DILATION_GROUPS):
        tab = rel_bias[:, g * HEADS_PER_GROUP:(g + 1) * HEADS_PER_GROUP]
        o_g, lse_g = _dilated_window_attention(q[:, :, g], k[:, :, g], v[:, :, g], tab, window, dilation)
        outs.append(o_g)
        lses.append(lse_g)
    alpha = jax.nn.softmax(jnp.stack(lses, axis=0), axis=0)
    o = jnp.sum(alpha[..., None] * jnp.stack(outs, axis=0), axis=0)
    y_attn = o.reshape(B, S, ATTN_OUT).astype(h.dtype) @ w_attn_out

    merged = jax.nn.sigmoid(g_conv) * y_conv + jax.nn.sigmoid(g_attn) * y_attn
    return merged @ w_o


def _fwd_setup_inputs(seed: int = 0) -> dict:
    key = jax.random.key(seed)
    ks = jax.random.split(key, 16)
    f32 = jnp.float32

    def nrm(k, shape, fan_in):
        return jax.random.normal(k, shape, f32) * (fan_in ** -0.5)

    return {
        "x": jax.random.normal(ks[0], (BATCH, SEQ, D_MODEL), f32),
        "c": jax.random.normal(ks[1], (BATCH, D_MODEL), f32),
        "ada_w": nrm(ks[2], (DEPTH, D_MODEL, N_SUB * 3 * D_MODEL), D_MODEL),
        "ada_b": 0.02 * jax.random.normal(ks[3], (DEPTH, N_SUB * 3 * D_MODEL), f32),
        "norm_g": 1.0 + 0.05 * jax.random.normal(ks[4], (DEPTH, N_SUB, D_MODEL), f32),
        "ffn_w_gate": nrm(ks[5], (DEPTH, 2, D_MODEL, D_FF), D_MODEL),
        "ffn_w_up": nrm(ks[6], (DEPTH, 2, D_MODEL, D_FF), D_MODEL),
        "ffn_w_down": nrm(ks[7], (DEPTH, 2, D_FF, D_MODEL), D_FF),
        "w_in": nrm(ks[8], (DEPTH, D_MODEL, IN_WIDTH), D_MODEL),
        "conv_w": nrm(ks[9], (DEPTH, CONV_K, CONV_WIDTH), CONV_K),
        "w_conv_out": nrm(ks[10], (DEPTH, CONV_WIDTH, D_MODEL), CONV_WIDTH),
        "w_attn_out": nrm(ks[11], (DEPTH, ATTN_OUT, D_MODEL), ATTN_OUT),
        "w_o": nrm(ks[12], (DEPTH, D_MODEL, D_MODEL), D_MODEL),
        "rel_bias": 0.5 * jax.random.normal(ks[13], (NUM_BUCKETS, N_ATTN_HEADS), f32),
        "final_g": 1.0 + 0.05 * jax.random.normal(ks[14], (D_MODEL,), f32),
    }


def _fwd_reference(x, c, ada_w, ada_b, norm_g, ffn_w_gate, ffn_w_up, ffn_w_down, w_in, conv_w,
              w_conv_out, w_attn_out, w_o, rel_bias, final_g):
    cs = jax.nn.silu(c)
    B = c.shape[0]
    for l in range(DEPTH):
        mod = (cs @ ada_w[l] + ada_b[l]).reshape(B, N_SUB, 3, D_MODEL)[:, :, :, None, :]
        h = _rmsnorm(x, norm_g[l, 0]) * (1.0 + mod[:, 0, 1]) + mod[:, 0, 0]
        x = x + 0.5 * mod[:, 0, 2] * _swiglu(h, ffn_w_gate[l, 0], ffn_w_up[l, 0], ffn_w_down[l, 0])
        h = _rmsnorm(x, norm_g[l, 1]) * (1.0 + mod[:, 1, 1]) + mod[:, 1, 0]
        x = x + mod[:, 1, 2] * _mixer(h, w_in[l], conv_w[l], w_conv_out[l], w_attn_out[l], w_o[l], rel_bias)
        h = _rmsnorm(x, norm_g[l, 2]) * (1.0 + mod[:, 2, 1]) + mod[:, 2, 0]
        x = x + 0.5 * mod[:, 2, 2] * _swiglu(h, ffn_w_gate[l, 1], ffn_w_up[l, 1], ffn_w_down[l, 1])
    return _rmsnorm(x, final_g)


import jax as _jax
import jax.numpy as _jnp

TWIN_FORMAT = 'train_step'
FWD_PARAMS = ['x', 'c', 'ada_w', 'ada_b', 'norm_g', 'ffn_w_gate', 'ffn_w_up', 'ffn_w_down', 'w_in', 'conv_w', 'w_conv_out', 'w_attn_out', 'w_o', 'rel_bias', 'final_g']
TWIN_WEIGHTS = ['ada_w', 'ada_b', 'norm_g', 'ffn_w_gate', 'ffn_w_up', 'ffn_w_down', 'w_in', 'conv_w', 'w_conv_out', 'w_attn_out', 'w_o', 'rel_bias', 'final_g']
TWIN_DIFF_INPUT = 'x'
TWIN_INPUTS = ['x', 'c', 'ada_w', 'ada_b', 'norm_g', 'ffn_w_gate', 'ffn_w_up', 'ffn_w_down', 'w_in', 'conv_w', 'w_conv_out', 'w_attn_out', 'w_o', 'rel_bias', 'final_g', 'loss_target', 'm_ada_w', 'm_ada_b', 'm_norm_g', 'm_ffn_w_gate', 'm_ffn_w_up', 'm_ffn_w_down', 'm_w_in', 'm_conv_w', 'm_w_conv_out', 'm_w_attn_out', 'm_w_o', 'm_rel_bias', 'm_final_g', 'v_ada_w', 'v_ada_b', 'v_norm_g', 'v_ffn_w_gate', 'v_ffn_w_up', 'v_ffn_w_down', 'v_w_in', 'v_conv_w', 'v_w_conv_out', 'v_w_attn_out', 'v_w_o', 'v_rel_bias', 'v_final_g']
TWIN_OUTPUTS = ['loss', 'grad_x', 'grad_ada_w', 'grad_ada_b', 'grad_norm_g', 'grad_ffn_w_gate', 'grad_ffn_w_up', 'grad_ffn_w_down', 'grad_w_in', 'grad_conv_w', 'grad_w_conv_out', 'grad_w_attn_out', 'grad_w_o', 'grad_rel_bias', 'grad_final_g', 'delta_ada_w', 'delta_ada_b', 'delta_norm_g', 'delta_ffn_w_gate', 'delta_ffn_w_up', 'delta_ffn_w_down', 'delta_w_in', 'delta_conv_w', 'delta_w_conv_out', 'delta_w_attn_out', 'delta_w_o', 'delta_rel_bias', 'delta_final_g', 'new_m_ada_w', 'new_m_ada_b', 'new_m_norm_g', 'new_m_ffn_w_gate', 'new_m_ffn_w_up', 'new_m_ffn_w_down', 'new_m_w_in', 'new_m_conv_w', 'new_m_w_conv_out', 'new_m_w_attn_out', 'new_m_w_o', 'new_m_rel_bias', 'new_m_final_g', 'new_v_ada_w', 'new_v_ada_b', 'new_v_norm_g', 'new_v_ffn_w_gate', 'new_v_ffn_w_up', 'new_v_ffn_w_down', 'new_v_w_in', 'new_v_conv_w', 'new_v_w_conv_out', 'new_v_w_attn_out', 'new_v_w_o', 'new_v_rel_bias', 'new_v_final_g']
TWIN_LEAF_KINDS = {'loss': 'loss', 'grad_x': 'grad_x', 'grad_ada_w': 'grad_w', 'grad_ada_b': 'grad_w', 'grad_norm_g': 'grad_w', 'grad_ffn_w_gate': 'grad_w', 'grad_ffn_w_up': 'grad_w', 'grad_ffn_w_down': 'grad_w', 'grad_w_in': 'grad_w', 'grad_conv_w': 'grad_w', 'grad_w_conv_out': 'grad_w', 'grad_w_attn_out': 'grad_w', 'grad_w_o': 'grad_w', 'grad_rel_bias': 'grad_w', 'grad_final_g': 'grad_w', 'delta_ada_w': 'delta_w', 'delta_ada_b': 'delta_w', 'delta_norm_g': 'delta_w', 'delta_ffn_w_gate': 'delta_w', 'delta_ffn_w_up': 'delta_w', 'delta_ffn_w_down': 'delta_w', 'delta_w_in': 'delta_w', 'delta_conv_w': 'delta_w', 'delta_w_conv_out': 'delta_w', 'delta_w_attn_out': 'delta_w', 'delta_w_o': 'delta_w', 'delta_rel_bias': 'delta_w', 'delta_final_g': 'delta_w', 'new_m_ada_w': 'new_m', 'new_m_ada_b': 'new_m', 'new_m_norm_g': 'new_m', 'new_m_ffn_w_gate': 'new_m', 'new_m_ffn_w_up': 'new_m', 'new_m_ffn_w_down': 'new_m', 'new_m_w_in': 'new_m', 'new_m_conv_w': 'new_m', 'new_m_w_conv_out': 'new_m', 'new_m_w_attn_out': 'new_m', 'new_m_w_o': 'new_m', 'new_m_rel_bias': 'new_m', 'new_m_final_g': 'new_m', 'new_v_ada_w': 'new_v', 'new_v_ada_b': 'new_v', 'new_v_norm_g': 'new_v', 'new_v_ffn_w_gate': 'new_v', 'new_v_ffn_w_up': 'new_v', 'new_v_ffn_w_down': 'new_v', 'new_v_w_in': 'new_v', 'new_v_conv_w': 'new_v', 'new_v_w_conv_out': 'new_v', 'new_v_w_attn_out': 'new_v', 'new_v_w_o': 'new_v', 'new_v_rel_bias': 'new_v', 'new_v_final_g': 'new_v'}


def _forward(args):
    return _fwd_reference(*[args[k] for k in FWD_PARAMS])


def _output_shape():
    out = _jax.eval_shape(lambda: _forward(_fwd_setup_inputs(0)))
    return out.shape, out.dtype

N_MICROBATCH = 1
ADAM_LR = 0.001
ADAM_B1 = 0.9
ADAM_B2 = 0.999
ADAM_EPS = 1e-08
ADAM_WD = 0.01
ADAM_STEP = 10
PER_EXAMPLE_BATCH_AXIS = {'x': 0, 'c': 0, 'loss_target': 0}
SHARED_INPUTS = []
_WEIGHT_DTYPES = {'ada_w': _jnp.float32, 'ada_b': _jnp.float32, 'norm_g': _jnp.float32, 'ffn_w_gate': _jnp.float32, 'ffn_w_up': _jnp.float32, 'ffn_w_down': _jnp.float32, 'w_in': _jnp.float32, 'conv_w': _jnp.float32, 'w_conv_out': _jnp.float32, 'w_attn_out': _jnp.float32, 'w_o': _jnp.float32, 'rel_bias': _jnp.float32, 'final_g': _jnp.float32}
MOMENT_SCALE = {'ada_w': 8.726596e-02, 'ada_b': 1.467939e-01, 'norm_g': 1.311841e-01, 'ffn_w_gate': 3.853652e-02, 'ffn_w_up': 3.737422e-02, 'ffn_w_down': 6.193263e-02, 'w_in': 7.028382e-02, 'conv_w': 1.217116e-01, 'w_conv_out': 1.185086e-01, 'w_attn_out': 3.103495e-02, 'w_o': 1.209530e-01, 'rel_bias': 3.182652e-02, 'final_g': 3.276277e+01}


def _to_microbatches(a, axis):
    t = _jnp.moveaxis(a, axis, 0)
    t = t.reshape((N_MICROBATCH, t.shape[0] // N_MICROBATCH) + t.shape[1:])
    return _jnp.moveaxis(t, 1, axis + 1)


def setup_inputs(seed: int = 0) -> dict:
    inp = _fwd_setup_inputs(seed)
    key = _jax.random.fold_in(_jax.random.key(seed), 7919)
    shape, _ = _output_shape()
    out = dict(inp)
    out["loss_target"] = _jax.random.normal(_jax.random.fold_in(key, 0), shape, _jnp.float32)
    for i, name in enumerate(TWIN_WEIGHTS):
        w = inp[name].astype(_jnp.float32)
        if MOMENT_SCALE is None:
            s = _jnp.sqrt(_jnp.mean(_jnp.square(w)) + 1e-30)
        else:
            s = MOMENT_SCALE[name]
        km, kv = _jax.random.split(_jax.random.fold_in(key, i + 1))
        out[name] = w
        out["m_" + name] = s * _jax.random.normal(km, w.shape, _jnp.float32)
        out["v_" + name] = (s * s) * _jax.random.uniform(kv, w.shape, _jnp.float32, 0.5, 1.5)
    if N_MICROBATCH > 1:
        for name, axis in PER_EXAMPLE_BATCH_AXIS.items():
            out[name] = _to_microbatches(out[name], axis)
    return {'x': out['x'], 'c': out['c'], 'ada_w': out['ada_w'], 'ada_b': out['ada_b'], 'norm_g': out['norm_g'], 'ffn_w_gate': out['ffn_w_gate'], 'ffn_w_up': out['ffn_w_up'], 'ffn_w_down': out['ffn_w_down'], 'w_in': out['w_in'], 'conv_w': out['conv_w'], 'w_conv_out': out['w_conv_out'], 'w_attn_out': out['w_attn_out'], 'w_o': out['w_o'], 'rel_bias': out['rel_bias'], 'final_g': out['final_g'], 'loss_target': out['loss_target'], 'm_ada_w': out['m_ada_w'], 'm_ada_b': out['m_ada_b'], 'm_norm_g': out['m_norm_g'], 'm_ffn_w_gate': out['m_ffn_w_gate'], 'm_ffn_w_up': out['m_ffn_w_up'], 'm_ffn_w_down': out['m_ffn_w_down'], 'm_w_in': out['m_w_in'], 'm_conv_w': out['m_conv_w'], 'm_w_conv_out': out['m_w_conv_out'], 'm_w_attn_out': out['m_w_attn_out'], 'm_w_o': out['m_w_o'], 'm_rel_bias': out['m_rel_bias'], 'm_final_g': out['m_final_g'], 'v_ada_w': out['v_ada_w'], 'v_ada_b': out['v_ada_b'], 'v_norm_g': out['v_norm_g'], 'v_ffn_w_gate': out['v_ffn_w_gate'], 'v_ffn_w_up': out['v_ffn_w_up'], 'v_ffn_w_down': out['v_ffn_w_down'], 'v_w_in': out['v_w_in'], 'v_conv_w': out['v_conv_w'], 'v_w_conv_out': out['v_w_conv_out'], 'v_w_attn_out': out['v_w_attn_out'], 'v_w_o': out['v_w_o'], 'v_rel_bias': out['v_rel_bias'], 'v_final_g': out['v_final_g']}


def _loss(weights, diff, rest, loss_target):
    with _jax.named_scope("forward"):
        args = {**rest, TWIN_DIFF_INPUT: diff, **{k: w.astype(_WEIGHT_DTYPES[k]) for k, w in weights.items()}}
        y = _forward(args)
    with _jax.named_scope("loss_head"):
        err = _jnp.square(y.astype(_jnp.float32) - loss_target)
        return 0.5 * _jnp.sum(_jnp.mean(err, axis=-1)) if err.ndim else 0.5 * err


def _adamw(w, g, m, v):
    m = ADAM_B1 * m + (1.0 - ADAM_B1) * g
    v = ADAM_B2 * v + (1.0 - ADAM_B2) * _jnp.square(g)
    m_hat = m / (1.0 - ADAM_B1 ** ADAM_STEP)
    v_hat = v / (1.0 - ADAM_B2 ** ADAM_STEP)
    delta = -ADAM_LR * (m_hat / (_jnp.sqrt(v_hat) + ADAM_EPS) + ADAM_WD * w)
    return delta, m, v


def reference(x, c, ada_w, ada_b, norm_g, ffn_w_gate, ffn_w_up, ffn_w_down, w_in, conv_w, w_conv_out, w_attn_out, w_o, rel_bias, final_g, loss_target, m_ada_w, m_ada_b, m_norm_g, m_ffn_w_gate, m_ffn_w_up, m_ffn_w_down, m_w_in, m_conv_w, m_w_conv_out, m_w_attn_out, m_w_o, m_rel_bias, m_final_g, v_ada_w, v_ada_b, v_norm_g, v_ffn_w_gate, v_ffn_w_up, v_ffn_w_down, v_w_in, v_conv_w, v_w_conv_out, v_w_attn_out, v_w_o, v_rel_bias, v_final_g):
    given = dict(x=x, c=c, ada_w=ada_w, ada_b=ada_b, norm_g=norm_g, ffn_w_gate=ffn_w_gate, ffn_w_up=ffn_w_up, ffn_w_down=ffn_w_down, w_in=w_in, conv_w=conv_w, w_conv_out=w_conv_out, w_attn_out=w_attn_out, w_o=w_o, rel_bias=rel_bias, final_g=final_g, loss_target=loss_target, m_ada_w=m_ada_w, m_ada_b=m_ada_b, m_norm_g=m_norm_g, m_ffn_w_gate=m_ffn_w_gate, m_ffn_w_up=m_ffn_w_up, m_ffn_w_down=m_ffn_w_down, m_w_in=m_w_in, m_conv_w=m_conv_w, m_w_conv_out=m_w_conv_out, m_w_attn_out=m_w_attn_out, m_w_o=m_w_o, m_rel_bias=m_rel_bias, m_final_g=m_final_g, v_ada_w=v_ada_w, v_ada_b=v_ada_b, v_norm_g=v_norm_g, v_ffn_w_gate=v_ffn_w_gate, v_ffn_w_up=v_ffn_w_up, v_ffn_w_down=v_ffn_w_down, v_w_in=v_w_in, v_conv_w=v_conv_w, v_w_conv_out=v_w_conv_out, v_w_attn_out=v_w_attn_out, v_w_o=v_w_o, v_rel_bias=v_rel_bias, v_final_g=v_final_g)
    weights = {n: given[n] for n in TWIN_WEIGHTS}
    shared = {n: given[n] for n in SHARED_INPUTS}
    per_example = {n: given[n] for n in ['x', 'c']}
    grad_fn = _jax.value_and_grad(_loss, argnums=(0, 1))

    def one_microbatch(ex, loss_target):
        ex = dict(ex)
        diff = ex.pop(TWIN_DIFF_INPUT)
        return grad_fn(weights, diff, {**shared, **ex}, loss_target)

    if N_MICROBATCH == 1:
        loss, (grad_w, grad_x) = one_microbatch(per_example, given["loss_target"])
    else:
        def body(carry, xs):
            loss_sum, grad_sum = carry
            l_k, (gw_k, gx_k) = one_microbatch(xs[0], xs[1])
            with _jax.named_scope("update"):
                return (loss_sum + l_k, _jax.tree.map(_jnp.add, grad_sum, gw_k)), gx_k

        init = (_jnp.zeros((), _jnp.float32), _jax.tree.map(_jnp.zeros_like, weights))
        (loss, grad_w), grad_x = _jax.lax.scan(body, init, (per_example, given["loss_target"]))
    with _jax.named_scope("update"):
        delta_w, new_m, new_v = {}, {}, {}
        for n in TWIN_WEIGHTS:
            delta_w[n], new_m[n], new_v[n] = _adamw(weights[n], grad_w[n], given["m_" + n], given["v_" + n])
    return (loss, grad_x, *[grad_w[n] for n in TWIN_WEIGHTS], *[delta_w[n] for n in TWIN_WEIGHTS],
            *[new_m[n] for n in TWIN_WEIGHTS], *[new_v[n] for n in TWIN_WEIGHTS])
```

```python
import functools

import numpy as np
import jax
import jax.numpy as jnp
from jax import lax
from jax.experimental import pallas as pl
from jax.experimental.pallas import tpu as pltpu

f32, bf16 = jnp.float32, jnp.bfloat16
SDS = jax.ShapeDtypeStruct
MESH = pl.DeviceIdType.MESH

D = 1024
DEPTH = 4
N_CHIPS = 4
FB = 704
HD = 64
QKV_W = 4608
REST_W = 5120
IN_W = QKV_W + REST_W
WIN_SH = IN_W // N_CHIPS
ADA_SH = 9 * D // N_CHIPS
BLK = 128
DILATIONS = (1, 4, 16)
NUM_BUCKETS, MAX_DISTANCE = 32, 2048
EPS = 1e-6
NEG = -1e30
SCALE = HD ** -0.5
LR, B1, B2, AEPS, WD, STEP = 0.001, 0.9, 0.999, 1e-08, 0.01, 10
BC1 = 1.0 - B1 ** STEP
BC2 = 1.0 - B2 ** STEP
VMEM_LIMIT = 56 * 1024 * 1024
TM = 512
TMX = 256
HALO = 16
CB = 512


def _cp(*sem):
    return pltpu.CompilerParams(dimension_semantics=sem if sem else None, vmem_limit_bytes=VMEM_LIMIT)


def _dot(a, b):
    return jnp.dot(a, b, preferred_element_type=f32)


def _dot_nt(a, b):
    return lax.dot_general(a, b, (((1,), (1,)), ((), ())), preferred_element_type=f32)


def _dot_tn(a, b):
    return lax.dot_general(a, b, (((0,), (0,)), ((), ())), preferred_element_type=f32)


def _sigmoid(x):
    return 1.0 / (1.0 + jnp.exp(-x))


def _norm_fwd(x, g, shift, scale):
    rstd = lax.rsqrt(jnp.mean(x * x, axis=-1, keepdims=True) + EPS)
    xhat = x * rstd
    return xhat * g * (1.0 + scale) + shift, xhat, rstd


def _norm_bwd(dh, xhat, rstd, g, scale):
    dshift = jnp.sum(dh, axis=0, keepdims=True)
    dscale = jnp.sum(dh * xhat * g, axis=0, keepdims=True)
    dg = jnp.sum(dh * xhat * (1.0 + scale), axis=0, keepdims=True)
    dxh = dh * (g * (1.0 + scale))
    dx = rstd * (dxh - xhat * jnp.mean(dxh * xhat, axis=-1, keepdims=True))
    return dx, dshift, dscale, dg


def _allgather_small(xp):
    m_per, n = xp.shape

    def body(x_ref, out_ref, send_sems, recv_sems, local_sem):
        x, y, c = lax.axis_index("x"), lax.axis_index("y"), lax.axis_index("c")
        me, sibling = (x, y, c), (x, y, 1 - c)
        chips = [(1 - x, y), (x, 1 - y), (1 - x, 1 - y)]

        def rows(px, py, pc):
            return out_ref.at[pl.ds((4 * px + 2 * py + pc) * m_per, m_per), :]

        def copy(k, block, to, src=None):
            return pltpu.make_async_remote_copy(
                src_ref=rows(*block) if src is None else src, dst_ref=rows(*block),
                send_sem=send_sems.at[k], recv_sem=recv_sems.at[k], device_id=to, device_id_type=MESH)

        mine = pltpu.make_async_copy(x_ref, rows(*me), local_sem)
        mine.start()
        first = [copy(0, me, sibling, src=x_ref)]
        first += [copy(1 + j, me, (*chip, c), src=x_ref) for j, chip in enumerate(chips)]
        for cp in first:
            cp.start()
        passed = [copy(4 + j, (*chip, c), sibling) for j, chip in enumerate(chips)]
        for j, chip in enumerate(chips):
            copy(1 + j, (*chip, c), me).wait_recv()
            passed[j].start()
        copy(0, sibling, me).wait_recv()
        for j, chip in enumerate(chips):
            copy(4 + j, (*chip, 1 - c), me).wait_recv()
        for cp in first + passed:
            cp.wait_send()
        mine.wait()

    return pl.pallas_call(
        body, name="allgather_small",
        out_shape=SDS((8 * m_per, n), xp.dtype),
        in_specs=[pl.BlockSpec(memory_space=pltpu.VMEM)],
        out_specs=pl.BlockSpec(memory_space=pltpu.VMEM),
        scratch_shapes=[pltpu.SemaphoreType.DMA((7,)), pltpu.SemaphoreType.DMA((7,)), pltpu.SemaphoreType.DMA],
        compiler_params=pltpu.CompilerParams(vmem_limit_bytes=VMEM_LIMIT),
    )(xp)


WCLASSES = (
    ("wg0", "lead", (D, FB)), ("wu0", "lead", (D, FB)), ("wd0", "row", (FB, D)),
    ("wg1", "lead", (D, FB)), ("wu1", "lead", (D, FB)), ("wd1", "row", (FB, D)),
    ("win", "col", (D, WIN_SH)), ("wco", "row", (D // N_CHIPS, D)), ("wao", "col", (512, D // N_CHIPS)),
    ("wo", "row", (D // N_CHIPS, D)),
)
NCLS = len(WCLASSES)


def _full_shape(kind, shp):
    if kind == "lead":
        return (N_CHIPS,) + shp
    if kind == "row":
        return (N_CHIPS * shp[0], shp[1])
    return (shp[0], N_CHIPS * shp[1])


def _shard_view(ref, kind, shp, j):
    if kind == "lead":
        return ref.at[j]
    if kind == "row":
        return ref.at[pl.ds(j * shp[0], shp[0]), :]
    return ref.at[:, pl.ds(j * shp[1], shp[1])]


def _half(ref, shp, h):
    hr = shp[0] // 2
    return ref.at[pl.ds(pl.multiple_of(h * hr, 16), hr), :]


def _gather_weights(shards):
    n = NCLS

    def body(*refs):
        ins, outs = refs[:n], refs[n:2 * n]
        send1, recv1, send2, recv2, lsem = refs[2 * n:]
        x, y, c = lax.axis_index("x"), lax.axis_index("y"), lax.axis_index("c")
        chip = 2 * x + y
        sibling = (x, y, 1 - c)

        for mc in range(N_CHIPS):
            @pl.when(chip == mc)
            def _(mc=mc):
                local = []
                for q, (_, kind, shp) in enumerate(WCLASSES):
                    cp = pltpu.make_async_copy(ins[q], _shard_view(outs[q], kind, shp, mc), lsem.at[q])
                    cp.start()
                    local.append(cp)
                sends = []
                for k in (1, 2, 3):
                    pj = mc ^ k
                    for q, (_, kind, shp) in enumerate(WCLASSES):
                        cp = pltpu.make_async_remote_copy(
                            src_ref=_half(ins[q], shp, c), dst_ref=_half(_shard_view(outs[q], kind, shp, mc), shp, c),
                            send_sem=send1.at[q * 3 + k - 1], recv_sem=recv1.at[q * 3 + k - 1],
                            device_id=(pj >> 1, pj & 1, c), device_id_type=MESH)
                        cp.start()
                        sends.append(cp)
                for k in (1, 2, 3):
                    pj = mc ^ k
                    for q, (_, kind, shp) in enumerate(WCLASSES):
                        landed = _half(_shard_view(outs[q], kind, shp, pj), shp, c)
                        pltpu.make_async_remote_copy(
                            src_ref=landed, dst_ref=landed, send_sem=send1.at[q * 3 + k - 1], recv_sem=recv1.at[q * 3 + k - 1],
                            device_id=(pj >> 1, pj & 1, c), device_id_type=MESH).wait_recv()
                        cp = pltpu.make_async_remote_copy(
                            src_ref=landed, dst_ref=landed, send_sem=send2.at[q * 3 + k - 1], recv_sem=recv2.at[q * 3 + k - 1],
                            device_id=sibling, device_id_type=MESH)
                        cp.start()
                        sends.append(cp)
                for k in (1, 2, 3):
                    pj = mc ^ k
                    for q, (_, kind, shp) in enumerate(WCLASSES):
                        other = _half(_shard_view(outs[q], kind, shp, pj), shp, 1 - c)
                        pltpu.make_async_remote_copy(
                            src_ref=other, dst_ref=other, send_sem=send2.at[q * 3 + k - 1], recv_sem=recv2.at[q * 3 + k - 1],
                            device_id=sibling, device_id_type=MESH).wait_recv()
                for cp in sends:
                    cp.wait_send()
                for cp in local:
                    cp.wait()

    anyspec = pl.BlockSpec(memory_space=pl.ANY)
    return pl.pallas_call(
        body, name="gather_weights",
        out_shape=[SDS(_full_shape(kind, shp), bf16) for _, kind, shp in WCLASSES],
        in_specs=[anyspec] * n, out_specs=[anyspec] * n,
        scratch_shapes=[pltpu.SemaphoreType.DMA((3 * n,)), pltpu.SemaphoreType.DMA((3 * n,)),
                        pltpu.SemaphoreType.DMA((3 * n,)), pltpu.SemaphoreType.DMA((3 * n,)),
                        pltpu.SemaphoreType.DMA((n,))],
    )(*shards)


def _scatter_grads(pieces):
    n = NCLS

    def body(*refs):
        ins, outs = refs[:n], refs[n:2 * n]
        send1, recv1, lsem = refs[2 * n:]
        x, y, c = lax.axis_index("x"), lax.axis_index("y"), lax.axis_index("c")
        chip = 2 * x + y

        for mc in range(N_CHIPS):
            @pl.when(chip == mc)
            def _(mc=mc):
                local, sends = [], []
                for q, (_, kind, shp) in enumerate(WCLASSES):
                    cp = pltpu.make_async_copy(_shard_view(ins[q], kind, shp, mc), outs[q].at[0], lsem.at[q])
                    cp.start()
                    local.append(cp)
                for k in (1, 2, 3):
                    pj = mc ^ k
                    for q, (_, kind, shp) in enumerate(WCLASSES):
                        cp = pltpu.make_async_remote_copy(
                            src_ref=_shard_view(ins[q], kind, shp, pj), dst_ref=outs[q].at[k],
                            send_sem=send1.at[q * 3 + k - 1], recv_sem=recv1.at[q * 3 + k - 1],
                            device_id=(pj >> 1, pj & 1, c), device_id_type=MESH)
                        cp.start()
                        sends.append(cp)
                for cp in sends:
                    cp.wait_recv()
                for cp in sends:
                    cp.wait_send()
                for cp in local:
                    cp.wait()

    anyspec = pl.BlockSpec(memory_space=pl.ANY)
    return pl.pallas_call(
        body, name="scatter_grads",
        out_shape=[SDS((N_CHIPS,) + shp, bf16) for _, _, shp in WCLASSES],
        in_specs=[anyspec] * n, out_specs=[anyspec] * n,
        scratch_shapes=[pltpu.SemaphoreType.DMA((3 * n,)), pltpu.SemaphoreType.DMA((3 * n,)), pltpu.SemaphoreType.DMA((n,))],
    )(*pieces)


def _swap_sibling(ts):
    n = len(ts)

    def body(*refs):
        ins, outs = refs[:n], refs[n:2 * n]
        send, recv = refs[2 * n:]
        x, y, c = lax.axis_index("x"), lax.axis_index("y"), lax.axis_index("c")
        cps = []
        for q in range(n):
            cp = pltpu.make_async_remote_copy(src_ref=ins[q], dst_ref=outs[q], send_sem=send.at[q], recv_sem=recv.at[q],
                                              device_id=(x, y, 1 - c), device_id_type=MESH)
            cp.start()
            cps.append(cp)
        for cp in cps:
            cp.wait_recv()
        for cp in cps:
            cp.wait_send()

    anyspec = pl.BlockSpec(memory_space=pl.ANY)
    return pl.pallas_call(
        body, name="swap_sibling",
        out_shape=[SDS(t.shape, t.dtype) for t in ts],
        in_specs=[anyspec] * n, out_specs=[anyspec] * n,
        scratch_shapes=[pltpu.SemaphoreType.DMA((n,)), pltpu.SemaphoreType.DMA((n,))],
    )(*ts)


def _mod_shards(c_all, ada_w, ada_b_sh):
    tn = ADA_SH // 3

    def body(c_ref, w_ref, b_ref, o_ref, cs_ref):
        cv = c_ref[...]
        cs = cv * _sigmoid(cv)
        cs_ref[...] = cs
        o_ref[...] = _dot(cs.astype(bf16), w_ref[...].astype(bf16)) + b_ref[...]

    return pl.pallas_call(
        body, name="mod_shards", grid=(DEPTH, 3),
        out_shape=[SDS((DEPTH, 8, ADA_SH), f32), SDS((8, D), f32)],
        in_specs=[pl.BlockSpec((8, D), lambda l, t: (0, 0)),
                  pl.BlockSpec((None, D, tn), lambda l, t: (l, 0, t)),
                  pl.BlockSpec((None, 1, tn), lambda l, t: (l, 0, t))],
        out_specs=[pl.BlockSpec((None, 8, tn), lambda l, t: (l, 0, t)), pl.BlockSpec((8, D), lambda l, t: (0, 0))],
        compiler_params=_cp("arbitrary", "arbitrary"),
    )(c_all, ada_w, ada_b_sh.reshape(DEPTH, 1, ADA_SH))


def _t5_bucket(dist):
    exact = NUM_BUCKETS // 2
    dd = np.maximum(dist, 1).astype(np.float32)
    large = exact + (np.log(dd / exact) / np.log(MAX_DISTANCE / exact) * (NUM_BUCKETS - exact)).astype(np.int32)
    large = np.minimum(large, NUM_BUCKETS - 1)
    return np.where(dist < exact, dist, large).astype(np.int32)


def _bucket_table():
    i = np.arange(BLK)[:, None]
    j = np.arange(2 * BLK)[None, :]
    rel = i - j + BLK
    return np.stack([_t5_bucket(np.maximum(rel, 0) * d) for d in DILATIONS]).astype(np.int32)


def _band():
    rel = lax.broadcasted_iota(jnp.int32, (BLK, 2 * BLK), 0) - lax.broadcasted_iota(jnp.int32, (BLK, 2 * BLK), 1) + BLK
    return (rel >= 0) & (rel <= BLK)


def _bias_blocks(rel_bias, buckets):
    def body(tab_ref, bk_ref, o_ref):
        h = pl.program_id(0)
        bk = bk_ref[...]
        acc = jnp.zeros((BLK, 2 * BLK), f32)
        for b in range(NUM_BUCKETS):
            acc = jnp.where(bk == b, tab_ref[b, h], acc)
        o_ref[...] = jnp.where(_band(), acc, NEG)

    return pl.pallas_call(
        body, name="bias_blocks", grid=(24,),
        out_shape=SDS((24, BLK, 2 * BLK), f32),
        in_specs=[pl.BlockSpec(memory_space=pltpu.SMEM), pl.BlockSpec((None, BLK, 2 * BLK), lambda h: (h // 8, 0, 0))],
        out_specs=pl.BlockSpec((None, BLK, 2 * BLK), lambda h: (h, 0, 0)),
        compiler_params=_cp("arbitrary"),
    )(rel_bias, buckets)


def _bias_grad(dsaccs, buckets):
    nl = len(dsaccs)

    def body(*refs):
        bk = refs[nl][...]
        tot = refs[0][...]
        for r in refs[1:nl]:
            tot = tot + r[...]
        lane = lax.broadcasted_iota(jnp.int32, (1, 128), 1)
        row = jnp.zeros((1, 128), f32)
        for b in range(NUM_BUCKETS):
            row = jnp.where(lane == b, jnp.sum(jnp.where(bk == b, tot, 0.0)), row)
        refs[nl + 1][...] = row

    return pl.pallas_call(
        body, name="bias_grad", grid=(24,),
        out_shape=SDS((24, 1, 128), f32),
        in_specs=[pl.BlockSpec((None, BLK, 2 * BLK), lambda h: (h, 0, 0))] * nl
                 + [pl.BlockSpec((None, BLK, 2 * BLK), lambda h: (h // 8, 0, 0))],
        out_specs=pl.BlockSpec((None, 1, 128), lambda h: (h, 0, 0)),
        compiler_params=_cp("arbitrary"),
    )(*dsaccs, buckets)


def _ffn_fwd(x, mod9, g3, wg, wu, wd, sub):
    S = x.shape[0]

    def body(x_ref, mod_ref, g_ref, wg_ref, wu_ref, wd_ref, xo_ref, h_ref, a_ref, u_ref, y_ref, acc):
        j = pl.program_id(1)

        @pl.when(j == 0)
        def _():
            h, _, _ = _norm_fwd(x_ref[...], g_ref[sub:sub + 1, :], mod_ref[3 * sub:3 * sub + 1, :], mod_ref[3 * sub + 1:3 * sub + 2, :])
            h_ref[...] = h.astype(bf16)
            acc[...] = jnp.zeros_like(acc)

        h = h_ref[...]
        a = _dot(h, wg_ref[...])
        u = _dot(h, wu_ref[...])
        a_ref[...] = a.astype(bf16)
        u_ref[...] = u.astype(bf16)
        hid = (a * _sigmoid(a) * u).astype(bf16)
        acc[...] += _dot(hid, wd_ref[...])

        @pl.when(j == N_CHIPS - 1)
        def _():
            y = acc[...]
            y_ref[...] = y.astype(bf16)
            xo_ref[...] = x_ref[...] + 0.5 * mod_ref[3 * sub + 2:3 * sub + 3, :] * y

    row = pl.BlockSpec((TM, D), lambda i, j: (i, 0))
    return pl.pallas_call(
        body, name="ffn_fwd", grid=(S // TM, N_CHIPS),
        out_shape=[SDS((S, D), f32), SDS((S, D), bf16), SDS((N_CHIPS, S, FB), bf16), SDS((N_CHIPS, S, FB), bf16), SDS((S, D), bf16)],
        in_specs=[row, pl.BlockSpec((9, D), lambda i, j: (0, 0)), pl.BlockSpec((3, D), lambda i, j: (0, 0)),
                  pl.BlockSpec((None, D, FB), lambda i, j: (j, 0, 0)), pl.BlockSpec((None, D, FB), lambda i, j: (j, 0, 0)),
                  pl.BlockSpec((FB, D), lambda i, j: (j, 0))],
        out_specs=[row, row, pl.BlockSpec((None, TM, FB), lambda i, j: (j, i, 0)), pl.BlockSpec((None, TM, FB), lambda i, j: (j, i, 0)), row],
        scratch_shapes=[pltpu.VMEM((TM, D), f32)],
        compiler_params=_cp("arbitrary", "arbitrary"),
    )(x, mod9, g3, wg, wu, wd)


def _ffn_bwd1(dxo, x, mod9, g3, y, a, u, wg, wu, wd, sub):
    S = x.shape[0]

    def body(dxo_ref, x_ref, mod_ref, g_ref, y_ref, a_ref, u_ref, wg_ref, wu_ref, wd_ref,
             dxi_ref, da_ref, du_ref, hid_ref, dy_ref, sm_ref, acc):
        i, j = pl.program_id(0), pl.program_id(1)
        gate = mod_ref[3 * sub + 2:3 * sub + 3, :]

        @pl.when((i == 0) & (j == 0))
        def _():
            sm_ref[...] = jnp.zeros_like(sm_ref)

        @pl.when(j == 0)
        def _():
            dxo_v = dxo_ref[...]
            dy_ref[...] = (0.5 * gate * dxo_v).astype(bf16)
            sm_ref[2:3, :] += jnp.sum(0.5 * y_ref[...].astype(f32) * dxo_v, axis=0, keepdims=True)
            acc[...] = jnp.zeros_like(acc)

        av, uv = a_ref[...].astype(f32), u_ref[...].astype(f32)
        sg = _sigmoid(av)
        sil = av * sg
        dhid = _dot_nt(dy_ref[...], wd_ref[...])
        da = (dhid * uv * (sg * (1.0 + av * (1.0 - sg)))).astype(bf16)
        du = (dhid * sil).astype(bf16)
        da_ref[...] = da
        du_ref[...] = du
        hid_ref[...] = (sil * uv).astype(bf16)
        acc[...] += _dot_nt(da, wg_ref[...]) + _dot_nt(du, wu_ref[...])

        @pl.when(j == N_CHIPS - 1)
        def _():
            g = g_ref[sub:sub + 1, :]
            scale = mod_ref[3 * sub + 1:3 * sub + 2, :]
            _, xhat, rstd = _norm_fwd(x_ref[...], g, mod_ref[3 * sub:3 * sub + 1, :], scale)
            dx, dshift, dscale, dg = _norm_bwd(acc[...], xhat, rstd, g, scale)
            dxi_ref[...] = dxo_ref[...] + dx
            sm_ref[0:1, :] += dshift
            sm_ref[1:2, :] += dscale
            sm_ref[3:4, :] += dg

    row = pl.BlockSpec((TM, D), lambda i, j: (i, 0))
    hidb = pl.BlockSpec((None, TM, FB), lambda i, j: (j, i, 0))
    wcol = pl.BlockSpec((None, D, FB), lambda i, j: (j, 0, 0))
    return pl.pallas_call(
        body, name="ffn_bwd1", grid=(S // TM, N_CHIPS),
        out_shape=[SDS((S, D), f32), SDS((N_CHIPS, S, FB), bf16), SDS((N_CHIPS, S, FB), bf16), SDS((N_CHIPS, S, FB), bf16),
                   SDS((S, D), bf16), SDS((8, D), f32)],
        in_specs=[row, row, pl.BlockSpec((9, D), lambda i, j: (0, 0)), pl.BlockSpec((3, D), lambda i, j: (0, 0)), row,
                  hidb, hidb, wcol, wcol, pl.BlockSpec((FB, D), lambda i, j: (j, 0))],
        out_specs=[row, hidb, hidb, hidb, row, pl.BlockSpec((8, D), lambda i, j: (0, 0))],
        scratch_shapes=[pltpu.VMEM((TM, D), f32)],
        compiler_params=_cp("arbitrary", "arbitrary"),
    )(dxo, x, mod9, g3, y, a, u, wg, wu, wd)


def _ffn_bwd2(h, da, du, hid, dy):
    S = h.shape[0]
    ni = S // TM

    def body(h_ref, da_ref, du_ref, hid_ref, dy_ref, dwg_ref, dwu_ref, dwd_ref, ag, au, ad):
        i = pl.program_id(1)

        @pl.when(i == 0)
        def _():
            ag[...] = jnp.zeros_like(ag)
            au[...] = jnp.zeros_like(au)
            ad[...] = jnp.zeros_like(ad)

        hv = h_ref[...]
        ag[...] += _dot_tn(hv, da_ref[...])
        au[...] += _dot_tn(hv, du_ref[...])
        ad[...] += _dot_tn(hid_ref[...], dy_ref[...])

        @pl.when(i == ni - 1)
        def _():
            dwg_ref[...] = ag[...].astype(bf16)
            dwu_ref[...] = au[...].astype(bf16)
            dwd_ref[...] = ad[...].astype(bf16)

    row = pl.BlockSpec((TM, D), lambda j, i: (i, 0))
    hidb = pl.BlockSpec((None, TM, FB), lambda j, i: (j, i, 0))
    wcol = pl.BlockSpec((None, D, FB), lambda j, i: (j, 0, 0))
    return pl.pallas_call(
        body, name="ffn_bwd2", grid=(N_CHIPS, ni),
        out_shape=[SDS((N_CHIPS, D, FB), bf16), SDS((N_CHIPS, D, FB), bf16), SDS((N_CHIPS * FB, D), bf16)],
        in_specs=[row, hidb, hidb, hidb, row],
        out_specs=[wcol, wcol, pl.BlockSpec((FB, D), lambda j, i: (j, 0))],
        scratch_shapes=[pltpu.VMEM((D, FB), f32), pltpu.VMEM((D, FB), f32), pltpu.VMEM((FB, D), f32)],
        compiler_params=_cp("arbitrary", "arbitrary"),
    )(h, da, du, hid, dy)


def _mix_qkv(x, mod9, g3, win):
    S = x.shape[0]

    def body(x_ref, mod_ref, g_ref, w_ref, h_ref, o_ref):
        @pl.when(pl.program_id(1) == 0)
        def _():
            h, _, _ = _norm_fwd(x_ref[...], g_ref[1:2, :], mod_ref[3:4, :], mod_ref[4:5, :])
            h_ref[...] = h.astype(bf16)

        o_ref[...] = _dot(h_ref[...], w_ref[...])

    row = pl.BlockSpec((TM, D), lambda i, j: (i, 0))
    return pl.pallas_call(
        body, name="mix_qkv", grid=(S // TM, QKV_W // CB),
        out_shape=[SDS((S, D), bf16), SDS((S, QKV_W), f32)],
        in_specs=[row, pl.BlockSpec((9, D), lambda i, j: (0, 0)), pl.BlockSpec((3, D), lambda i, j: (0, 0)),
                  pl.BlockSpec((D, CB), lambda i, j: (0, j))],
        out_specs=[row, pl.BlockSpec((TM, CB), lambda i, j: (i, j))],
        compiler_params=_cp("arbitrary", "arbitrary"),
    )(x, mod9, g3, win)


def _mix_rest(h, win):
    S = h.shape[0]
    off = QKV_W // CB

    def body(h_ref, w_ref, o_ref):
        o_ref[...] = _dot(h_ref[...], w_ref[...]).astype(bf16)

    return pl.pallas_call(
        body, name="mix_rest", grid=(S // TM, REST_W // CB),
        out_shape=SDS((S, REST_W), bf16),
        in_specs=[pl.BlockSpec((TM, D), lambda i, j: (i, 0)), pl.BlockSpec((D, CB), lambda i, j: (0, off + j))],
        out_specs=pl.BlockSpec((TM, CB), lambda i, j: (i, j)),
        compiler_params=_cp("arbitrary", "arbitrary"),
    )(h, win)


def _attn_fwd(qkv, bias, g):
    S = qkv.shape[0]
    d = DILATIONS[g]
    R = BLK * d
    nb = S // R
    qb, kb, vb = 4 * g, 12 + 4 * g, 24 + 4 * g

    def body(q_ref, kc_ref, kp_ref, vc_ref, vp_ref, b_ref, o_ref, l_ref):
        n = pl.program_id(1)
        col = lax.broadcasted_iota(jnp.int32, (BLK, 2 * BLK), 1)
        first = jnp.where((col < BLK) & (n == 0), NEG, 0.0)

        def step(r, carry):
            sl = pl.ds(r, BLK, stride=d)
            q, kc, kp, vc, vp = q_ref[sl, :], kc_ref[sl, :], kp_ref[sl, :], vc_ref[sl, :], vp_ref[sl, :]
            os, ls = [], []
            for hh in range(2):
                cs = slice(HD * hh, HD * hh + HD)
                qh = q[:, cs].astype(bf16)
                kh = jnp.concatenate([kp[:, cs], kc[:, cs]], axis=0).astype(bf16)
                vh = jnp.concatenate([vp[:, cs], vc[:, cs]], axis=0).astype(bf16)
                s = _dot_nt(qh, kh) * SCALE + b_ref[hh] + first
                m = jnp.max(s, axis=-1, keepdims=True)
                p = jnp.exp(s - m)
                l = jnp.sum(p, axis=-1, keepdims=True)
                os.append(_dot(p.astype(bf16), vh) / l)
                ls.append(jnp.broadcast_to(m + jnp.log(l), (BLK, HD)))
            o_ref[sl, :] = jnp.concatenate(os, axis=1)
            l_ref[sl, :] = jnp.concatenate(ls, axis=1)
            return carry

        lax.fori_loop(0, d, step, 0)

    def blk(cb, prev):
        if prev:
            return pl.BlockSpec((R, 128), lambda hp, n: (jnp.maximum(n - 1, 0), cb + hp))
        return pl.BlockSpec((R, 128), lambda hp, n: (n, cb + hp))

    outb = pl.BlockSpec((R, 128), lambda hp, n: (n, hp))
    return pl.pallas_call(
        body, name=f"attn_fwd_d{d}", grid=(4, nb),
        out_shape=[SDS((S, 512), f32), SDS((S, 512), f32)],
        in_specs=[blk(qb, False), blk(kb, False), blk(kb, True), blk(vb, False), blk(vb, True),
                  pl.BlockSpec((2, BLK, 2 * BLK), lambda hp, n: (4 * g + hp, 0, 0))],
        out_specs=[outb, outb],
        compiler_params=_cp("arbitrary", "arbitrary"),
    )(qkv, qkv, qkv, qkv, qkv, bias)


def _attn_bwd(qkv, do, o, lse, bias, dq_all, dk_all, dv_all, g):
    S = qkv.shape[0]
    d = DILATIONS[g]
    R = BLK * d
    nb = S // R
    qb, kb, vb = 4 * g, 12 + 4 * g, 24 + 4 * g

    def body(q_ref, kc_ref, kp_ref, vc_ref, vp_ref, do_ref, o_ref, l_ref, b_ref, dqi, dki, dvi,
             dq_ref, dk_ref, dv_ref, ds_ref, ck, cv):
        n = pl.program_id(1)
        col = lax.broadcasted_iota(jnp.int32, (BLK, 2 * BLK), 1)
        first = jnp.where((col < BLK) & (n == 0), NEG, 0.0)

        @pl.when(n == 0)
        def _():
            ck[...] = jnp.zeros_like(ck)
            cv[...] = jnp.zeros_like(cv)
            ds_ref[...] = jnp.zeros_like(ds_ref)

        @pl.when(n < nb)
        def _():
            def step(r, carry):
                sl = pl.ds(r, BLK, stride=d)
                q, kc, kp, vc, vp = q_ref[sl, :], kc_ref[sl, :], kp_ref[sl, :], vc_ref[sl, :], vp_ref[sl, :]
                dov, ov, lv = do_ref[sl, :], o_ref[sl, :], l_ref[sl, :]
                dqs, dks, dvs = [], [], []
                for hh in range(2):
                    cs = slice(HD * hh, HD * hh + HD)
                    qh = q[:, cs].astype(bf16)
                    kh = jnp.concatenate([kp[:, cs], kc[:, cs]], axis=0).astype(bf16)
                    vh = jnp.concatenate([vp[:, cs], vc[:, cs]], axis=0).astype(bf16)
                    doh = dov[:, cs]
                    dsum = jnp.sum(doh * ov[:, cs], axis=-1, keepdims=True)
                    s = _dot_nt(qh, kh) * SCALE + b_ref[hh] + first
                    p = jnp.exp(s - lv[:, HD * hh:HD * hh + 1])
                    dohb = doh.astype(bf16)
                    ds = p * (_dot_nt(dohb, vh) - dsum)
                    ds_ref[hh] += ds
                    dsb = ds.astype(bf16)
                    dqs.append(_dot(dsb, kh) * SCALE)
                    dks.append(_dot_tn(dsb, qh) * SCALE)
                    dvs.append(_dot_tn(p.astype(bf16), dohb))
                dq_ref[sl, :] = jnp.concatenate(dqs, axis=1)
                dk = jnp.concatenate(dks, axis=1)
                dv = jnp.concatenate(dvs, axis=1)
                dk_ref[sl, :] = ck[r] + dk[:BLK]
                dv_ref[sl, :] = cv[r] + dv[:BLK]
                ck[r] = dk[BLK:]
                cv[r] = dv[BLK:]
                return carry

            lax.fori_loop(0, d, step, 0)

        @pl.when(n == nb)
        def _():
            def flush(r, carry):
                sl = pl.ds(r, BLK, stride=d)
                dk_ref[sl, :] = ck[r]
                dv_ref[sl, :] = cv[r]
                return carry

            lax.fori_loop(0, d, flush, 0)

    last = nb - 1

    def blk(cb, prev):
        if prev:
            return pl.BlockSpec((R, 128), lambda hp, n: (jnp.maximum(jnp.minimum(n, last) - 1, 0), cb + hp))
        return pl.BlockSpec((R, 128), lambda hp, n: (jnp.minimum(n, last), cb + hp))

    cur = pl.BlockSpec((R, 128), lambda hp, n: (jnp.minimum(n, last), hp))
    anyspec = pl.BlockSpec(memory_space=pl.ANY)
    dqo = pl.BlockSpec((R, 128), lambda hp, n: (jnp.minimum(n, last), 4 * g + hp))
    dko = pl.BlockSpec((R, 128), lambda hp, n: (jnp.maximum(n - 1, 0), 4 * g + hp))
    return pl.pallas_call(
        body, name=f"attn_bwd_d{d}", grid=(4, nb + 1),
        out_shape=[SDS((S, 1536), f32), SDS((S, 1536), f32), SDS((S, 1536), f32), SDS((8, BLK, 2 * BLK), f32)],
        in_specs=[blk(qb, False), blk(kb, False), blk(kb, True), blk(vb, False), blk(vb, True), cur, cur, cur,
                  pl.BlockSpec((2, BLK, 2 * BLK), lambda hp, n: (4 * g + hp, 0, 0)), anyspec, anyspec, anyspec],
        out_specs=[dqo, dko, dko, pl.BlockSpec((2, BLK, 2 * BLK), lambda hp, n: (hp, 0, 0))],
        scratch_shapes=[pltpu.VMEM((d, BLK, 128), f32), pltpu.VMEM((d, BLK, 128), f32)],
        input_output_aliases={9: 0, 10: 1, 11: 2},
        compiler_params=_cp("arbitrary", "arbitrary"),
    )(qkv, qkv, qkv, qkv, qkv, do, o, lse, bias, dq_all, dk_all, dv_all)


def _conv_z(cc, ch, hc, hh, cw_ref, first):
    halo = jnp.where(first, 0.0, hc.astype(f32) * hh.astype(f32))
    T = jnp.concatenate([halo, cc * ch], axis=0)
    z = cw_ref[2:3, :] * T + cw_ref[1:2, :] * pltpu.roll(T, 1, 0) + cw_ref[0:1, :] * pltpu.roll(T, 2, 0)
    return T, z[HALO:]


def _rest_specs(tm, with_next):
    per = tm // HALO
    specs = [pl.BlockSpec((tm, D), functools.partial(lambda i, k: (i, k), k=k)) for k in range(5)]
    specs += [pl.BlockSpec((HALO, D), functools.partial(lambda i, k: (jnp.maximum(i * per - 1, 0), k), k=k)) for k in (1, 2)]
    return specs


def _mix_out_fwd(x, mod9, rest, ogs, lgs, cw, wco, wao, wo):
    S = x.shape[0]
    tm = TMX

    def body(x_ref, mod_ref, cb_ref, cc_ref, ch_ref, gc_ref, ga_ref, hc_ref, hh_ref,
             o0, o1, o2, l0, l1, l2, cw_ref, wco_ref, wao_ref, wo_ref,
             xo_ref, o_ref, lse_ref, yc_ref, ya_ref, out_ref):
        i = pl.program_id(0)
        lv = [l0[...], l1[...], l2[...]]
        mx = jnp.maximum(jnp.maximum(lv[0], lv[1]), lv[2])
        es = [jnp.exp(l - mx) for l in lv]
        den = es[0] + es[1] + es[2]
        o = (es[0] / den) * o0[...] + (es[1] / den) * o1[...] + (es[2] / den) * o2[...]
        o_ref[...] = o
        lse_ref[...] = mx + jnp.log(den)
        _, z = _conv_z(cc_ref[...].astype(f32), ch_ref[...].astype(f32), hc_ref[...], hh_ref[...], cw_ref, i == 0)
        p = (cb_ref[...].astype(f32) * z).astype(bf16)
        yc = _dot(p, wco_ref[...])
        ya = _dot(o.astype(bf16), wao_ref[...])
        yc_ref[...] = yc.astype(bf16)
        ya_ref[...] = ya.astype(bf16)
        merged = _sigmoid(gc_ref[...].astype(f32)) * yc + _sigmoid(ga_ref[...].astype(f32)) * ya
        out = _dot(merged.astype(bf16), wo_ref[...])
        out_ref[...] = out.astype(bf16)
        xo_ref[...] = x_ref[...] + mod_ref[5:6, :] * out

    row = pl.BlockSpec((tm, D), lambda i: (i, 0))
    att = pl.BlockSpec((tm, 512), lambda i: (i, 0))
    full = lambda shp: pl.BlockSpec(shp, lambda i: (0, 0))
    return pl.pallas_call(
        body, name="mix_out_fwd", grid=(S // tm,),
        out_shape=[SDS((S, D), f32), SDS((S, 512), f32), SDS((S, 512), f32), SDS((S, D), bf16), SDS((S, D), bf16), SDS((S, D), bf16)],
        in_specs=[row, full((9, D))] + _rest_specs(tm, False) + [att] * 6 + [full((3, D)), full((D, D)), full((512, D)), full((D, D))],
        out_specs=[row, att, att, row, row, row],
        compiler_params=_cp("arbitrary"),
    )(x, mod9, *([rest] * 7), *ogs, *lgs, cw, wco, wao, wo)


def _mix_out_bwd(dxo, mod9, outv, yc, ya, rest, o, cw, wco, wao, wo):
    S = dxo.shape[0]
    tm = TMX
    ni = S // tm

    def body(dxo_ref, mod_ref, out_ref, yc_ref, ya_ref, cb_ref, cc_ref, ch_ref, gc_ref, ga_ref, hc_ref, hh_ref,
             o_ref, cw_ref, wco_ref, wao_ref, wo_ref,
             dp_ref, dg2_ref, do_ref, dwco_ref, dwao_ref, dwo_ref, sm_ref, aco, aao, ao):
        i = pl.program_id(0)

        @pl.when(i == 0)
        def _():
            sm_ref[...] = jnp.zeros_like(sm_ref)
            aco[...] = jnp.zeros_like(aco)
            aao[...] = jnp.zeros_like(aao)
            ao[...] = jnp.zeros_like(ao)

        dxo_v = dxo_ref[...]
        sm_ref[2:3, :] += jnp.sum(out_ref[...].astype(f32) * dxo_v, axis=0, keepdims=True)
        dout = (mod_ref[5:6, :] * dxo_v).astype(bf16)
        dmerged = _dot_nt(dout, wo_ref[...])
        sc, sa = _sigmoid(gc_ref[...].astype(f32)), _sigmoid(ga_ref[...].astype(f32))
        ycv, yav = yc_ref[...].astype(f32), ya_ref[...].astype(f32)
        ao[...] += _dot_tn((sc * ycv + sa * yav).astype(bf16), dout)
        dyc = (dmerged * sc).astype(bf16)
        dya = (dmerged * sa).astype(bf16)
        dg2_ref[:, :D] = (dmerged * ycv * sc * (1.0 - sc)).astype(bf16)
        dg2_ref[:, D:] = (dmerged * yav * sa * (1.0 - sa)).astype(bf16)
        dp_ref[...] = _dot_nt(dyc, wco_ref[...]).astype(bf16)
        _, z = _conv_z(cc_ref[...].astype(f32), ch_ref[...].astype(f32), hc_ref[...], hh_ref[...], cw_ref, i == 0)
        aco[...] += _dot_tn((cb_ref[...].astype(f32) * z).astype(bf16), dyc)
        do_ref[...] = _dot_nt(dya, wao_ref[...])
        aao[...] += _dot_tn(o_ref[...].astype(bf16), dya)

        @pl.when(i == ni - 1)
        def _():
            dwco_ref[...] = aco[...].astype(bf16)
            dwao_ref[...] = aao[...].astype(bf16)
            dwo_ref[...] = ao[...].astype(bf16)

    row = pl.BlockSpec((tm, D), lambda i: (i, 0))
    att = pl.BlockSpec((tm, 512), lambda i: (i, 0))
    full = lambda shp: pl.BlockSpec(shp, lambda i: (0, 0))
    return pl.pallas_call(
        body, name="mix_out_bwd", grid=(ni,),
        out_shape=[SDS((S, D), bf16), SDS((S, 2 * D), bf16), SDS((S, 512), f32),
                   SDS((D, D), bf16), SDS((512, D), bf16), SDS((D, D), bf16), SDS((8, D), f32)],
        in_specs=[row, full((9, D)), row, row, row] + _rest_specs(tm, False) + [att, full((3, D)), full((D, D)), full((512, D)), full((D, D))],
        out_specs=[row, pl.BlockSpec((tm, 2 * D), lambda i: (i, 0)), att, full((D, D)), full((512, D)), full((D, D)), full((8, D))],
        scratch_shapes=[pltpu.VMEM((D, D), f32), pltpu.VMEM((512, D), f32), pltpu.VMEM((D, D), f32)],
        compiler_params=_cp("arbitrary"),
    )(dxo, mod9, outv, yc, ya, *([rest] * 7), o, cw, wco, wao, wo)


def _conv_bwd(dp, rest, cw):
    S = dp.shape[0]
    tm = TM
    per = tm // HALO
    nh = S // HALO
    ni = S // tm

    def body(dp_ref, dpn_ref, cb_ref, cbn_ref, cc_ref, ch_ref, hc_ref, hh_ref, cw_ref, d3_ref, sm_ref):
        i = pl.program_id(0)

        @pl.when(i == 0)
        def _():
            sm_ref[...] = jnp.zeros_like(sm_ref)

        cc, ch = cc_ref[...].astype(f32), ch_ref[...].astype(f32)
        T, z = _conv_z(cc, ch, hc_ref[...], hh_ref[...], cw_ref, i == 0)
        dpv = dp_ref[...].astype(f32)
        cbv = cb_ref[...].astype(f32)
        dz = dpv * cbv
        dzn = jnp.where(i == ni - 1, 0.0, dpn_ref[...].astype(f32) * cbn_ref[...].astype(f32))
        E = jnp.concatenate([dz, dzn], axis=0)
        ne = tm + HALO
        dT = cw_ref[2:3, :] * E + cw_ref[1:2, :] * pltpu.roll(E, ne - 1, 0) + cw_ref[0:1, :] * pltpu.roll(E, ne - 2, 0)
        dT = dT[:tm]
        d3_ref[:, :D] = (dpv * z).astype(bf16)
        d3_ref[:, D:2 * D] = (dT * ch).astype(bf16)
        d3_ref[:, 2 * D:] = (dT * cc).astype(bf16)
        sm_ref[2:3, :] += jnp.sum(dz * T[HALO:], axis=0, keepdims=True)
        sm_ref[1:2, :] += jnp.sum(dz * pltpu.roll(T, 1, 0)[HALO:], axis=0, keepdims=True)
        sm_ref[0:1, :] += jnp.sum(dz * pltpu.roll(T, 2, 0)[HALO:], axis=0, keepdims=True)

    row = pl.BlockSpec((tm, D), lambda i: (i, 0))
    nxt = pl.BlockSpec((HALO, D), lambda i: (jnp.minimum((i + 1) * per, nh - 1), 0))
    col = lambda k: pl.BlockSpec((tm, D), lambda i: (i, k))
    prv = lambda k: pl.BlockSpec((HALO, D), lambda i: (jnp.maximum(i * per - 1, 0), k))
    return pl.pallas_call(
        body, name="conv_bwd", grid=(ni,),
        out_shape=[SDS((S, 3 * D), bf16), SDS((8, D), f32)],
        in_specs=[row, nxt, col(0), nxt, col(1), col(2), prv(1), prv(2), pl.BlockSpec((3, D), lambda i: (0, 0))],
        out_specs=[pl.BlockSpec((tm, 3 * D), lambda i: (i, 0)), pl.BlockSpec((8, D), lambda i: (0, 0))],
        compiler_params=_cp("arbitrary"),
    )(dp, dp, rest, rest, rest, rest, rest, rest, cw)


_DU_RANGES = ((0, 3), (3, 6), (6, 9), (9, 15), (15, 19))
N_CBLK = IN_W // CB


def _mix_in_bwd_dh(dxo, x, mod9, g3, dus, win):
    S = x.shape[0]

    def body(dxo_ref, x_ref, mod_ref, g_ref, s0, s1, s2, s3, s4, w_ref, dxi_ref, sm_ref, acc):
        i, kb = pl.program_id(0), pl.program_id(1)

        @pl.when((i == 0) & (kb == 0))
        def _():
            sm_ref[...] = jnp.zeros_like(sm_ref)

        @pl.when(kb == 0)
        def _():
            acc[...] = jnp.zeros_like(acc)

        for src, (lo, hi) in zip((s0, s1, s2, s3, s4), _DU_RANGES):
            @pl.when((kb >= lo) & (kb < hi))
            def _(src=src):
                acc[...] += _dot_nt(src[...].astype(bf16), w_ref[...])

        @pl.when(kb == N_CBLK - 1)
        def _():
            g, scale = g_ref[1:2, :], mod_ref[4:5, :]
            _, xhat, rstd = _norm_fwd(x_ref[...], g, mod_ref[3:4, :], scale)
            dx, dshift, dscale, dg = _norm_bwd(acc[...], xhat, rstd, g, scale)
            dxi_ref[...] = dxo_ref[...] + dx
            sm_ref[0:1, :] += dshift
            sm_ref[1:2, :] += dscale
            sm_ref[3:4, :] += dg

    row = pl.BlockSpec((TM, D), lambda i, kb: (i, 0))

    def src_spec(lo, hi):
        return pl.BlockSpec((TM, CB), lambda i, kb: (i, jnp.clip(kb - lo, 0, hi - lo - 1)))

    return pl.pallas_call(
        body, name="mix_in_bwd_dh", grid=(S // TM, N_CBLK),
        out_shape=[SDS((S, D), f32), SDS((8, D), f32)],
        in_specs=[row, row, pl.BlockSpec((9, D), lambda i, kb: (0, 0)), pl.BlockSpec((3, D), lambda i, kb: (0, 0))]
                 + [src_spec(lo, hi) for lo, hi in _DU_RANGES] + [pl.BlockSpec((D, CB), lambda i, kb: (0, kb))],
        out_specs=[row, pl.BlockSpec((8, D), lambda i, kb: (0, 0))],
        scratch_shapes=[pltpu.VMEM((TM, D), f32)],
        compiler_params=_cp("arbitrary", "arbitrary"),
    )(dxo, x, mod9, g3, *dus, win)


def _mix_in_bwd_dw(h, dus):
    S = h.shape[0]
    ni = S // TM

    def body(h_ref, s0, s1, s2, s3, s4, dw_ref, acc):
        kb, i = pl.program_id(0), pl.program_id(1)

        @pl.when(i == 0)
        def _():
            acc[...] = jnp.zeros_like(acc)

        for src, (lo, hi) in zip((s0, s1, s2, s3, s4), _DU_RANGES):
            @pl.when((kb >= lo) & (kb < hi))
            def _(src=src):
                acc[...] += _dot_tn(h_ref[...], src[...].astype(bf16))

        @pl.when(i == ni - 1)
        def _():
            dw_ref[...] = acc[...].astype(bf16)

    def src_spec(lo, hi):
        def imap(kb, i):
            on = (kb >= lo) & (kb < hi)
            return (jnp.where(on, i, 0), jnp.clip(kb - lo, 0, hi - lo - 1))
        return pl.BlockSpec((TM, CB), imap)

    return pl.pallas_call(
        body, name="mix_in_bwd_dw", grid=(N_CBLK, ni),
        out_shape=SDS((D, IN_W), bf16),
        in_specs=[pl.BlockSpec((TM, D), lambda kb, i: (i, 0))] + [src_spec(lo, hi) for lo, hi in _DU_RANGES],
        out_specs=pl.BlockSpec((D, CB), lambda kb, i: (0, kb)),
        scratch_shapes=[pltpu.VMEM((D, CB), f32)],
        compiler_params=_cp("arbitrary", "arbitrary"),
    )(h, *dus)


def _loss_head(x, fg, tgt):
    S = x.shape[0]

    def body(x_ref, g_ref, t_ref, ls_ref, dx_ref, sm_ref):
        i = pl.program_id(0)

        @pl.when(i == 0)
        def _():
            ls_ref[...] = jnp.zeros_like(ls_ref)
            sm_ref[...] = jnp.zeros_like(sm_ref)

        xv, g = x_ref[...], g_ref[...]
        rstd = lax.rsqrt(jnp.mean(xv * xv, axis=-1, keepdims=True) + EPS)
        xhat = xv * rstd
        e = xhat * g - t_ref[...]
        ls_ref[...] += 0.5 * jnp.sum(jnp.mean(e * e, axis=-1, keepdims=True))
        dy = e * (1.0 / D)
        sm_ref[0:1, :] += jnp.sum(dy * xhat, axis=0, keepdims=True)
        dxh = dy * g
        dx_ref[...] = rstd * (dxh - xhat * jnp.mean(dxh * xhat, axis=-1, keepdims=True))

    row = pl.BlockSpec((TM, D), lambda i: (i, 0))
    return pl.pallas_call(
        body, name="loss_head", grid=(S // TM,),
        out_shape=[SDS((8, 128), f32), SDS((S, D), f32), SDS((8, D), f32)],
        in_specs=[row, pl.BlockSpec((1, D), lambda i: (0, 0)), row],
        out_specs=[pl.BlockSpec((8, 128), lambda i: (0, 0)), row, pl.BlockSpec((8, D), lambda i: (0, 0))],
        compiler_params=_cp("arbitrary"),
    )(x, fg, tgt)


def _adam(w, g, m, v):
    m2 = B1 * m + (1.0 - B1) * g
    v2 = B2 * v + (1.0 - B2) * (g * g)
    delta = -LR * ((m2 / BC1) / (jnp.sqrt(v2 / BC2) + AEPS) + WD * w)
    return delta, m2, v2


def _row_tile(rows, cols):
    for tr in (512, 352, 256, 128, 64):
        if rows % tr == 0 and tr * cols * 4 <= (5 << 18):
            return tr
    raise ValueError((rows, cols))


def _sum_slots(land):
    _, R, C = land.shape
    tr = _row_tile(R, C)

    def body(l_ref, t_ref):
        t = l_ref[0].astype(f32)
        for k in range(1, N_CHIPS):
            t = t + l_ref[k].astype(f32)
        t_ref[...] = t

    return pl.pallas_call(
        body, name="sum_slots", grid=(R // tr,),
        out_shape=SDS((R, C), f32),
        in_specs=[pl.BlockSpec((N_CHIPS, tr, C), lambda i: (0, i, 0))],
        out_specs=pl.BlockSpec((tr, C), lambda i: (i, 0)),
        compiler_params=_cp("arbitrary"),
    )(land)


def _adamw_pair(w2, m2, v2, ta, tb, outs, slot):
    R, C = ta.shape
    tr = _row_tile(R, C)
    nrt = R // tr

    def body(w_ref, m_ref, v_ref, ta_ref, tb_ref, g_in, d_in, m_in, v_in, g_ref, d_ref, mo_ref, vo_ref):
        g = ta_ref[...] + tb_ref[...]
        delta, mn, vn = _adam(w_ref[...], g, m_ref[...], v_ref[...])
        g_ref[...] = g
        d_ref[...] = delta
        mo_ref[...] = mn
        vo_ref[...] = vn

    big = pl.BlockSpec((tr, C), lambda i: (slot * nrt + i, 0))
    loc = pl.BlockSpec((tr, C), lambda i: (i, 0))
    anyspec = pl.BlockSpec(memory_space=pl.ANY)
    return pl.pallas_call(
        body, name="adamw_pair", grid=(nrt,),
        out_shape=[SDS(o.shape, f32) for o in outs],
        in_specs=[big, big, big, loc, loc] + [anyspec] * 4,
        out_specs=[big] * 4,
        input_output_aliases={5: 0, 6: 1, 7: 2, 8: 3},
        compiler_params=_cp("arbitrary"),
    )(w2, m2, v2, ta, tb, *outs)


def _adamw_small(w, g, m, v):
    def body(w_ref, g_ref, m_ref, v_ref, d_ref, mo_ref, vo_ref):
        delta, mn, vn = _adam(w_ref[...], g_ref[...], m_ref[...], v_ref[...])
        d_ref[...] = delta
        mo_ref[...] = mn
        vo_ref[...] = vn

    return pl.pallas_call(body, name="adamw_small", out_shape=[SDS(w.shape, f32)] * 3)(w, g, m, v)


def _ada_w_update(cs_all, dmod_sh, w, m, v):
    tr = 256

    def body(cs_ref, dm_ref, w_ref, m_ref, v_ref, g_ref, d_ref, mo_ref, vo_ref):
        g = _dot_tn(cs_ref[...].astype(bf16), dm_ref[...].astype(bf16))
        delta, mn, vn = _adam(w_ref[...], g, m_ref[...], v_ref[...])
        g_ref[...] = g
        d_ref[...] = delta
        mo_ref[...] = mn
        vo_ref[...] = vn

    blk = pl.BlockSpec((None, tr, ADA_SH), lambda l, i: (l, i, 0))
    return pl.pallas_call(
        body, name="ada_w_update", grid=(DEPTH, D // tr),
        out_shape=[SDS(w.shape, f32)] * 4,
        in_specs=[pl.BlockSpec((8, tr), lambda l, i: (0, i)), pl.BlockSpec((None, 8, ADA_SH), lambda l, i: (l, 0, 0)), blk, blk, blk],
        out_specs=[blk] * 4,
        compiler_params=_cp("arbitrary", "arbitrary"),
    )(cs_all, dmod_sh, w, m, v)


def _sum_devices(gathered):
    _, R, C = gathered.shape

    def body(g_ref, o_ref):
        t = g_ref[0]
        for k in range(1, 8):
            t = t + g_ref[k]
        o_ref[...] = t

    return pl.pallas_call(body, name="sum_devices", out_shape=SDS((R, C), f32))(gathered)


def _layer_fwd(x, mod9, g3, cw, W, bias):
    x1, h1, a1, u1, y1 = _ffn_fwd(x, mod9, g3, W["wg0"], W["wu0"], W["wd0"], 0)
    hm, qkv = _mix_qkv(x1, mod9, g3, W["win"])
    rest = _mix_rest(hm, W["win"])
    ogs, lgs = [], []
    for g in range(3):
        og, lg = _attn_fwd(qkv, bias, g)
        ogs.append(og)
        lgs.append(lg)
    x2, o, lse, yc, ya, outv = _mix_out_fwd(x1, mod9, rest, ogs, lgs, cw, W["wco"], W["wao"], W["wo"])
    x3, h3, a3, u3, y3 = _ffn_fwd(x2, mod9, g3, W["wg1"], W["wu1"], W["wd1"], 2)
    saved = dict(x0=x, x1=x1, x2=x2, h1=h1, a1=a1, u1=u1, y1=y1, hm=hm, qkv=qkv, rest=rest, o=o, lse=lse, yc=yc, ya=ya,
                 outv=outv, h3=h3, a3=a3, u3=u3, y3=y3)
    return x3, saved


def _layer_bwd(dx, sv, mod9, g3, cw, W, bias):
    S = dx.shape[0]
    dw = {}
    dx2, da, du, hid, dy, sm3 = _ffn_bwd1(dx, sv["x2"], mod9, g3, sv["y3"], sv["a3"], sv["u3"], W["wg1"], W["wu1"], W["wd1"], 2)
    dw["wg1"], dw["wu1"], dw["wd1"] = _ffn_bwd2(sv["h3"], da, du, hid, dy)
    dp, dg2, do, dw["wco"], dw["wao"], dw["wo"], smo = _mix_out_bwd(
        dx2, mod9, sv["outv"], sv["yc"], sv["ya"], sv["rest"], sv["o"], cw, W["wco"], W["wao"], W["wo"])
    d3, smc = _conv_bwd(dp, sv["rest"], cw)
    dq = lax.empty((S, 1536), f32)
    dk = lax.empty((S, 1536), f32)
    dv = lax.empty((S, 1536), f32)
    dsaccs = []
    for g in range(3):
        dq, dk, dv, dsg = _attn_bwd(sv["qkv"], do, sv["o"], sv["lse"], bias, dq, dk, dv, g)
        dsaccs.append(dsg)
    dus = (dq, dk, dv, d3, dg2)
    dx1, smm = _mix_in_bwd_dh(dx2, sv["x1"], mod9, g3, dus, W["win"])
    dw["win"] = _mix_in_bwd_dw(sv["hm"], dus)
    dx0, da, du, hid, dy, sm1 = _ffn_bwd1(dx1, sv["x0"], mod9, g3, sv["y1"], sv["a1"], sv["u1"], W["wg0"], W["wu0"], W["wd0"], 0)
    dw["wg0"], dw["wu0"], dw["wd0"] = _ffn_bwd2(sv["h1"], da, du, hid, dy)
    dmod = jnp.concatenate([sm1[0:3], smm[0:2], smo[2:3], sm3[0:3]], axis=0)
    dng = jnp.concatenate([sm1[3:4], smm[3:4], sm3[3:4]], axis=0)
    return dx0, dw, dmod, dng, smc[0:3], jnp.concatenate(dsaccs, axis=0)


def _chip_cols(a, chip, width):
    return lax.dynamic_slice_in_dim(a, chip * width, width, axis=a.ndim - 1)


def kernel(x, c, ada_w, ada_b, norm_g, ffn_w_gate, ffn_w_up, ffn_w_down, w_in, conv_w, w_conv_out, w_attn_out, w_o, rel_bias, final_g, loss_target, m_ada_w, m_ada_b, m_norm_g, m_ffn_w_gate, m_ffn_w_up, m_ffn_w_down, m_w_in, m_conv_w, m_w_conv_out, m_w_attn_out, m_w_o, m_rel_bias, m_final_g, v_ada_w, v_ada_b, v_norm_g, v_ffn_w_gate, v_ffn_w_up, v_ffn_w_down, v_w_in, v_conv_w, v_w_conv_out, v_w_attn_out, v_w_o, v_rel_bias, v_final_g):
    ix, iy, ic = lax.axis_index("x"), lax.axis_index("y"), lax.axis_index("c")
    chip = 2 * ix + iy
    dev = 4 * ix + 2 * iy + ic
    xs = x[0]
    S = xs.shape[0]
    qd = D // N_CHIPS

    pack = jnp.concatenate([c, norm_g.reshape(3, D), conv_w.reshape(3, D), jnp.zeros((1, D), f32)], axis=0)
    g1 = _allgather_small(pack).reshape(8, 8, D)
    c_all = g1[:, 0]
    by_chip = g1[0::2]
    ng_full = jnp.concatenate([by_chip[j, 1:4].reshape(DEPTH, 3, qd) for j in range(N_CHIPS)], axis=-1)
    cw_full = jnp.concatenate([by_chip[j, 4:7].reshape(DEPTH, 3, qd) for j in range(N_CHIPS)], axis=-1)
    mod_sh, cs_all = _mod_shards(c_all, ada_w, _chip_cols(ada_b, chip, ADA_SH))
    g2 = _allgather_small(mod_sh.reshape(DEPTH * 8, ADA_SH)).reshape(8, DEPTH, 8, ADA_SH)
    mine = lax.dynamic_index_in_dim(g2[0::2], dev, axis=2, keepdims=False)
    mod = jnp.transpose(mine, (1, 0, 2)).reshape(DEPTH, 9, D)

    buckets = jnp.asarray(_bucket_table())
    bias = _bias_blocks(rel_bias, buckets)

    Ws, saves = [], []
    xc = xs
    for l in range(DEPTH):
        shards = [ffn_w_gate[l, 0], ffn_w_up[l, 0], ffn_w_down[l, 0], ffn_w_gate[l, 1], ffn_w_up[l, 1], ffn_w_down[l, 1],
                  w_in[l], w_conv_out[l], w_attn_out[l], w_o[l]]
        full = _gather_weights([s.astype(bf16) for s in shards])
        W = {name: f for (name, _, _), f in zip(WCLASSES, full)}
        Ws.append(W)
        xc, sv = _layer_fwd(xc, mod[l], ng_full[l], cw_full[l], W, bias)
        saves.append(sv)

    ls, dx, smf = _loss_head(xc, final_g.reshape(1, D), loss_target[0])
    loss = lax.psum(ls[0, 0], ("x", "y", "c"))

    params = dict(wg=ffn_w_gate, wu=ffn_w_up, wd=ffn_w_down, win=w_in, wco=w_conv_out, wao=w_attn_out, wo=w_o)
    moms = dict(wg=m_ffn_w_gate, wu=m_ffn_w_up, wd=m_ffn_w_down, win=m_w_in, wco=m_w_conv_out, wao=m_w_attn_out, wo=m_w_o)
    vars_ = dict(wg=v_ffn_w_gate, wu=v_ffn_w_up, wd=v_ffn_w_down, win=v_w_in, wco=v_w_conv_out, wao=v_w_attn_out, wo=v_w_o)
    flat = lambda a: a.reshape(-1, a.shape[-1])
    big_out = {k: [lax.empty(flat(p).shape, f32) for _ in range(4)] for k, p in params.items()}
    dmods, dngs, dcws, dsaccs = [None] * DEPTH, [None] * DEPTH, [None] * DEPTH, [None] * DEPTH
    for l in reversed(range(DEPTH)):
        dx, dw, dmods[l], dngs[l], dcws[l], dsaccs[l] = _layer_bwd(dx, saves[l], mod[l], ng_full[l], cw_full[l], Ws[l], bias)
        lands = _scatter_grads([dw[name] for name, _, _ in WCLASSES])
        ts = [_sum_slots(ld) for ld in lands]
        tsib = _swap_sibling(ts)
        for q, (name, _, _) in enumerate(WCLASSES):
            key = name.rstrip("01")
            slot = 2 * l + int(name[-1]) if name[-1] in "01" else l
            big_out[key] = _adamw_pair(flat(params[key]), flat(moms[key]), flat(vars_[key]), ts[q], tsib[q], big_out[key], slot)

    drb = jnp.transpose(_bias_grad(dsaccs, buckets)[:, 0, :NUM_BUCKETS])
    drb_row = jnp.pad(drb.reshape(1, NUM_BUCKETS * 24), ((0, 0), (0, D - NUM_BUCKETS * 24)))
    pack2 = jnp.concatenate([jnp.concatenate(dmods, axis=0), jnp.concatenate(dngs, axis=0), jnp.concatenate(dcws, axis=0),
                             smf[0:1], drb_row, jnp.zeros((2, D), f32)], axis=0)
    g3 = _allgather_small(pack2).reshape(8, 64, D)
    tot = _sum_devices(g3)
    g_ada_b = tot[0:36].reshape(DEPTH, 9 * D)
    g_norm_g = _chip_cols(tot[36:48].reshape(DEPTH, 3, D), chip, qd)
    g_conv_w = _chip_cols(tot[48:60].reshape(DEPTH, 3, D), chip, qd)
    g_final_g = tot[60]
    g_rel_bias = tot[61, :NUM_BUCKETS * 24].reshape(NUM_BUCKETS, 24)
    dmod_sh = jnp.transpose(_chip_cols(g3[:, 0:36].reshape(8, DEPTH, 9 * D), chip, ADA_SH), (1, 0, 2))
    g_ada_w, d_ada_w, nm_ada_w, nv_ada_w = _ada_w_update(cs_all, dmod_sh, ada_w, m_ada_w, v_ada_w)

    def small(w, g, m, v):
        shp = w.shape
        to2 = lambda a: a.reshape(-1, shp[-1])
        return [o.reshape(shp) for o in _adamw_small(to2(w), to2(g), to2(m), to2(v))]

    d_ada_b, nm_ada_b, nv_ada_b = small(ada_b, g_ada_b, m_ada_b, v_ada_b)
    d_norm_g, nm_norm_g, nv_norm_g = small(norm_g, g_norm_g, m_norm_g, v_norm_g)
    d_conv_w, nm_conv_w, nv_conv_w = small(conv_w, g_conv_w, m_conv_w, v_conv_w)
    d_rel_bias, nm_rel_bias, nv_rel_bias = small(rel_bias, g_rel_bias, m_rel_bias, v_rel_bias)
    d_final_g, nm_final_g, nv_final_g = small(final_g, g_final_g, m_final_g, v_final_g)

    def big(key, which):
        return big_out[key][which].reshape(params[key].shape)

    grads = [g_ada_w, g_ada_b, g_norm_g, big("wg", 0), big("wu", 0), big("wd", 0), big("win", 0), g_conv_w, big("wco", 0),
             big("wao", 0), big("wo", 0), g_rel_bias, g_final_g]
    deltas = [d_ada_w, d_ada_b, d_norm_g, big("wg", 1), big("wu", 1), big("wd", 1), big("win", 1), d_conv_w, big("wco", 1),
              big("wao", 1), big("wo", 1), d_rel_bias, d_final_g]
    new_m = [nm_ada_w, nm_ada_b, nm_norm_g, big("wg", 2), big("wu", 2), big("wd", 2), big("win", 2), nm_conv_w, big("wco", 2),
             big("wao", 2), big("wo", 2), nm_rel_bias, nm_final_g]
    new_v = [nv_ada_w, nv_ada_b, nv_norm_g, big("wg", 3), big("wu", 3), big("wd", 3), big("win", 3), nv_conv_w, big("wco", 3),
             big("wao", 3), big("wo", 3), nv_rel_bias, nv_final_g]
    return (loss, dx[None], *grads, *deltas, *new_m, *new_v)
```

```python
import functools

import numpy as np
import jax
import jax.numpy as jnp
from jax import lax
from jax.experimental import pallas as pl
from jax.experimental.pallas import tpu as pltpu

f32, bf16 = jnp.float32, jnp.bfloat16
SDS = jax.ShapeDtypeStruct
MESH = pl.DeviceIdType.MESH

D = 1024
DEPTH = 4
N_CHIPS = 4
FB = 704
HD = 64
QKV_W = 4608
REST_W = 5120
IN_W = QKV_W + REST_W
WIN_SH = IN_W // N_CHIPS
ADA_SH = 9 * D // N_CHIPS
BLK = 128
DILATIONS = (1, 4, 16)
NUM_BUCKETS, MAX_DISTANCE = 32, 2048
EPS = 1e-6
NEG = -1e30
SCALE = HD ** -0.5
LR, B1, B2, AEPS, WD, STEP = 0.001, 0.9, 0.999, 1e-08, 0.01, 10
BC1 = 1.0 - B1 ** STEP
BC2 = 1.0 - B2 ** STEP
VMEM_LIMIT = 56 * 1024 * 1024
TM = 512
TMX = 256
HALO = 16
CB = 512


def _cp(*sem):
    return pltpu.CompilerParams(dimension_semantics=sem if sem else None, vmem_limit_bytes=VMEM_LIMIT)


def _dot(a, b):
    return jnp.dot(a, b, preferred_element_type=f32)


def _dot_nt(a, b):
    return lax.dot_general(a, b, (((1,), (1,)), ((), ())), preferred_element_type=f32)


def _dot_tn(a, b):
    return lax.dot_general(a, b, (((0,), (0,)), ((), ())), preferred_element_type=f32)


def _sigmoid(x):
    return 1.0 / (1.0 + jnp.exp(-x))


def _norm_fwd(x, g, shift, scale):
    rstd = lax.rsqrt(jnp.mean(x * x, axis=-1, keepdims=True) + EPS)
    xhat = x * rstd
    return xhat * g * (1.0 + scale) + shift, xhat, rstd


def _norm_bwd(dh, xhat, rstd, g, scale):
    dshift = jnp.sum(dh, axis=0, keepdims=True)
    dscale = jnp.sum(dh * xhat * g, axis=0, keepdims=True)
    dg = jnp.sum(dh * xhat * (1.0 + scale), axis=0, keepdims=True)
    dxh = dh * (g * (1.0 + scale))
    dx = rstd * (dxh - xhat * jnp.mean(dxh * xhat, axis=-1, keepdims=True))
    return dx, dshift, dscale, dg


def _allgather_small(xp):
    m_per, n = xp.shape

    def body(x_ref, out_ref, send_sems, recv_sems, local_sem):
        x, y, c = lax.axis_index("x"), lax.axis_index("y"), lax.axis_index("c")
        me, sibling = (x, y, c), (x, y, 1 - c)
        chips = [(1 - x, y), (x, 1 - y), (1 - x, 1 - y)]

        def rows(px, py, pc):
            return out_ref.at[pl.ds((4 * px + 2 * py + pc) * m_per, m_per), :]

        def copy(k, block, to, src=None):
            return pltpu.make_async_remote_copy(
                src_ref=rows(*block) if src is None else src, dst_ref=rows(*block),
                send_sem=send_sems.at[k], recv_sem=recv_sems.at[k], device_id=to, device_id_type=MESH)

        mine = pltpu.make_async_copy(x_ref, rows(*me), local_sem)
        mine.start()
        first = [copy(0, me, sibling, src=x_ref)]
        first += [copy(1 + j, me, (*chip, c), src=x_ref) for j, chip in enumerate(chips)]
        for cp in first:
            cp.start()
        passed = [copy(4 + j, (*chip, c), sibling) for j, chip in enumerate(chips)]
        for j, chip in enumerate(chips):
            copy(1 + j, (*chip, c), me).wait_recv()
            passed[j].start()
        copy(0, sibling, me).wait_recv()
        for j, chip in enumerate(chips):
            copy(4 + j, (*chip, 1 - c), me).wait_recv()
        for cp in first + passed:
            cp.wait_send()
        mine.wait()

    return pl.pallas_call(
        body, name="allgather_small",
        out_shape=SDS((8 * m_per, n), xp.dtype),
        in_specs=[pl.BlockSpec(memory_space=pltpu.VMEM)],
        out_specs=pl.BlockSpec(memory_space=pltpu.VMEM),
        scratch_shapes=[pltpu.SemaphoreType.DMA((7,)), pltpu.SemaphoreType.DMA((7,)), pltpu.SemaphoreType.DMA],
        compiler_params=pltpu.CompilerParams(vmem_limit_bytes=VMEM_LIMIT),
    )(xp)


WCLASSES = (
    ("wg0", "lead", (D, FB)), ("wu0", "lead", (D, FB)), ("wd0", "row", (FB, D)),
    ("wg1", "lead", (D, FB)), ("wu1", "lead", (D, FB)), ("wd1", "row", (FB, D)),
    ("win", "col", (D, WIN_SH)), ("wco", "row", (D // N_CHIPS, D)), ("wao", "col", (512, D // N_CHIPS)),
    ("wo", "row", (D // N_CHIPS, D)),
)
NCLS = len(WCLASSES)


def _full_shape(kind, shp):
    if kind == "lead":
        return (N_CHIPS,) + shp
    if kind == "row":
        return (N_CHIPS * shp[0], shp[1])
    return (shp[0], N_CHIPS * shp[1])


def _shard_view(ref, kind, shp, j):
    if kind == "lead":
        return ref.at[j]
    if kind == "row":
        return ref.at[pl.ds(j * shp[0], shp[0]), :]
    return ref.at[:, pl.ds(j * shp[1], shp[1])]


def _half(ref, shp, h):
    hr = shp[0] // 2
    return ref.at[pl.ds(pl.multiple_of(h * hr, 16), hr), :]


def _gather_weights(shards):
    n = NCLS

    def body(*refs):
        ins, outs = refs[:n], refs[n:2 * n]
        send1, recv1, send2, recv2, lsem = refs[2 * n:]
        x, y, c = lax.axis_index("x"), lax.axis_index("y"), lax.axis_index("c")
        chip = 2 * x + y
        sibling = (x, y, 1 - c)

        for mc in range(N_CHIPS):
            @pl.when(chip == mc)
            def _(mc=mc):
                local = []
                for q, (_, kind, shp) in enumerate(WCLASSES):
                    cp = pltpu.make_async_copy(ins[q], _shard_view(outs[q], kind, shp, mc), lsem.at[q])
                    cp.start()
                    local.append(cp)
                sends = []
                for k in (1, 2, 3):
                    pj = mc ^ k
                    for q, (_, kind, shp) in enumerate(WCLASSES):
                        cp = pltpu.make_async_remote_copy(
                            src_ref=_half(ins[q], shp, c), dst_ref=_half(_shard_view(outs[q], kind, shp, mc), shp, c),
                            send_sem=send1.at[q * 3 + k - 1], recv_sem=recv1.at[q * 3 + k - 1],
                            device_id=(pj >> 1, pj & 1, c), device_id_type=MESH)
                        cp.start()
                        sends.append(cp)
                for k in (1, 2, 3):
                    pj = mc ^ k
                    for q, (_, kind, shp) in enumerate(WCLASSES):
                        landed = _half(_shard_view(outs[q], kind, shp, pj), shp, c)
                        pltpu.make_async_remote_copy(
                            src_ref=landed, dst_ref=landed, send_sem=send1.at[q * 3 + k - 1], recv_sem=recv1.at[q * 3 + k - 1],
                            device_id=(pj >> 1, pj & 1, c), device_id_type=MESH).wait_recv()
                        cp = pltpu.make_async_remote_copy(
                            src_ref=landed, dst_ref=landed, send_sem=send2.at[q * 3 + k - 1], recv_sem=recv2.at[q * 3 + k - 1],
                            device_id=sibling, device_id_type=MESH)
                        cp.start()
                        sends.append(cp)
                for k in (1, 2, 3):
                    pj = mc ^ k
                    for q, (_, kind, shp) in enumerate(WCLASSES):
                        other = _half(_shard_view(outs[q], kind, shp, pj), shp, 1 - c)
                        pltpu.make_async_remote_copy(
                            src_ref=other, dst_ref=other, send_sem=send2.at[q * 3 + k - 1], recv_sem=recv2.at[q * 3 + k - 1],
                            device_id=sibling, device_id_type=MESH).wait_recv()
                for cp in sends:
                    cp.wait_send()
                for cp in local:
                    cp.wait()

    anyspec = pl.BlockSpec(memory_space=pl.ANY)
    return pl.pallas_call(
        body, name="gather_weights",
        out_shape=[SDS(_full_shape(kind, shp), bf16) for _, kind, shp in WCLASSES],
        in_specs=[anyspec] * n, out_specs=[anyspec] * n,
        scratch_shapes=[pltpu.SemaphoreType.DMA((3 * n,)), pltpu.SemaphoreType.DMA((3 * n,)),
                        pltpu.SemaphoreType.DMA((3 * n,)), pltpu.SemaphoreType.DMA((3 * n,)),
                        pltpu.SemaphoreType.DMA((n,))],
    )(*shards)


def _scatter_grads(pieces):
    n = NCLS

    def body(*refs):
        ins, outs = refs[:n], refs[n:2 * n]
        send1, recv1, lsem = refs[2 * n:]
        x, y, c = lax.axis_index("x"), lax.axis_index("y"), lax.axis_index("c")
        chip = 2 * x + y

        for mc in range(N_CHIPS):
            @pl.when(chip == mc)
            def _(mc=mc):
                local, sends = [], []
                for q, (_, kind, shp) in enumerate(WCLASSES):
                    cp = pltpu.make_async_copy(_shard_view(ins[q], kind, shp, mc), outs[q].at[0], lsem.at[q])
                    cp.start()
                    local.append(cp)
                for k in (1, 2, 3):
                    pj = mc ^ k
                    for q, (_, kind, shp) in enumerate(WCLASSES):
                        cp = pltpu.make_async_remote_copy(
                            src_ref=_shard_view(ins[q], kind, shp, pj), dst_ref=outs[q].at[k],
                            send_sem=send1.at[q * 3 + k - 1], recv_sem=recv1.at[q * 3 + k - 1],
                            device_id=(pj >> 1, pj & 1, c), device_id_type=MESH)
                        cp.start()
                        sends.append(cp)
                for cp in sends:
                    cp.wait_recv()
                for cp in sends:
                    cp.wait_send()
                for cp in local:
                    cp.wait()

    anyspec = pl.BlockSpec(memory_space=pl.ANY)
    return pl.pallas_call(
        body, name="scatter_grads",
        out_shape=[SDS((N_CHIPS,) + shp, bf16) for _, _, shp in WCLASSES],
        in_specs=[anyspec] * n, out_specs=[anyspec] * n,
        scratch_shapes=[pltpu.SemaphoreType.DMA((3 * n,)), pltpu.SemaphoreType.DMA((3 * n,)), pltpu.SemaphoreType.DMA((n,))],
    )(*pieces)


def _swap_sibling(ts):
    n = len(ts)

    def body(*refs):
        ins, outs = refs[:n], refs[n:2 * n]
        send, recv = refs[2 * n:]
        x, y, c = lax.axis_index("x"), lax.axis_index("y"), lax.axis_index("c")
        cps = []
        for q in range(n):
            cp = pltpu.make_async_remote_copy(src_ref=ins[q], dst_ref=outs[q], send_sem=send.at[q], recv_sem=recv.at[q],
                                              device_id=(x, y, 1 - c), device_id_type=MESH)
            cp.start()
            cps.append(cp)
        for cp in cps:
            cp.wait_recv()
        for cp in cps:
            cp.wait_send()

    anyspec = pl.BlockSpec(memory_space=pl.ANY)
    return pl.pallas_call(
        body, name="swap_sibling",
        out_shape=[SDS(t.shape, t.dtype) for t in ts],
        in_specs=[anyspec] * n, out_specs=[anyspec] * n,
        scratch_shapes=[pltpu.SemaphoreType.DMA((n,)), pltpu.SemaphoreType.DMA((n,))],
    )(*ts)


HBM_SPEC = pl.BlockSpec(memory_space=pltpu.HBM)
SEM_SPEC = pl.BlockSpec(memory_space=pltpu.SEMAPHORE)
ANY_SPEC = pl.BlockSpec(memory_space=pl.ANY)
EFFECT = pltpu.SideEffectType.DATAFLOW_SIDE_EFFECTING
N_COPIES = 3 * NCLS


def _in_hbm(a):
    return pltpu.with_memory_space_constraint(a, pltpu.HBM)


def _chip_index():
    return 2 * lax.axis_index("x") + lax.axis_index("y")


def _place_own(shards):
    n = NCLS

    def body(*refs):
        ins, outs, lsem = refs[:n], refs[n:2 * n], refs[2 * n]
        chip = _chip_index()
        for mc in range(N_CHIPS):
            @pl.when(chip == mc)
            def _(mc=mc):
                cps = [pltpu.make_async_copy(ins[q], _shard_view(outs[q], kind, shp, mc), lsem.at[q])
                       for q, (_, kind, shp) in enumerate(WCLASSES)]
                for cp in cps:
                    cp.start()
                for cp in cps:
                    cp.wait()

    return pl.pallas_call(
        body, name="place_own",
        out_shape=[SDS(_full_shape(kind, shp), bf16) for _, kind, shp in WCLASSES],
        in_specs=[ANY_SPEC] * n, out_specs=[ANY_SPEC] * n,
        scratch_shapes=[pltpu.SemaphoreType.DMA((n,))],
    )(*shards)


def _take_own(pieces):
    n = NCLS

    def body(*refs):
        ins, outs, lsem = refs[:n], refs[n:2 * n], refs[2 * n]
        chip = _chip_index()
        for mc in range(N_CHIPS):
            @pl.when(chip == mc)
            def _(mc=mc):
                cps = [pltpu.make_async_copy(_shard_view(ins[q], kind, shp, mc), outs[q].at[0], lsem.at[q])
                       for q, (_, kind, shp) in enumerate(WCLASSES)]
                for cp in cps:
                    cp.start()
                for cp in cps:
                    cp.wait()

    return pl.pallas_call(
        body, name="take_own",
        out_shape=[SDS((N_CHIPS,) + shp, bf16) for _, _, shp in WCLASSES],
        in_specs=[ANY_SPEC] * n, out_specs=[ANY_SPEC] * n,
        scratch_shapes=[pltpu.SemaphoreType.DMA((n,))],
    )(*pieces)


def _split_start(name, srcs, dsts, after, src_view, dst_view):
    n = NCLS

    def body(*refs):
        src, dst = refs[:n], refs[n:2 * n]
        send, recv = refs[2 * n + 1], refs[2 * n + 2]
        token = refs[-1]
        c = lax.axis_index("c")
        chip = _chip_index()
        for mc in range(N_CHIPS):
            @pl.when(chip == mc)
            def _(mc=mc):
                for k in (1, 2, 3):
                    pj = mc ^ k
                    for q in range(n):
                        pltpu.make_async_remote_copy(
                            src_ref=src_view(src[q], q, mc, pj), dst_ref=dst_view(dst[q], q, mc, k),
                            send_sem=send.at[q * 3 + k - 1], recv_sem=recv.at[q * 3 + k - 1],
                            device_id=(pj >> 1, pj & 1, c), device_id_type=MESH).start()
        token[...] = jnp.zeros_like(token)

    return pl.pallas_call(
        body, name=name,
        out_shape=(pltpu.SemaphoreType.DMA((N_COPIES,)), pltpu.SemaphoreType.DMA((N_COPIES,)),
                   *[pltpu.HBM(a.shape, a.dtype) for a in srcs], *[pltpu.HBM(a.shape, a.dtype) for a in dsts], SDS((8, 128), f32)),
        in_specs=[HBM_SPEC] * (2 * n) + [ANY_SPEC],
        out_specs=(SEM_SPEC, SEM_SPEC, *([HBM_SPEC] * (2 * n)), pl.BlockSpec(memory_space=pltpu.VMEM)),
        input_output_aliases={i: 2 + i for i in range(2 * n)},
        compiler_params=pltpu.CompilerParams(has_side_effects=EFFECT),
    )(*[_in_hbm(a) for a in srcs], *[_in_hbm(a) for a in dsts], after)


def _split_wait(name, started, after, arrival_view):
    n = NCLS
    send, recv = started[0], started[1]
    srcs, dsts = started[2:2 + n], started[2 + n:2 + 2 * n]

    def body(*refs):
        src, dst = refs[:n], refs[n:2 * n]
        send_sem, recv_sem = refs[2 * n], refs[2 * n + 1]
        x, y, c = lax.axis_index("x"), lax.axis_index("y"), lax.axis_index("c")
        for k in (1, 2, 3):
            for q in range(n):
                arrival = arrival_view(dst[q], q, k)
                cp = pltpu.make_async_remote_copy(
                    src_ref=arrival, dst_ref=arrival, send_sem=send_sem.at[q * 3 + k - 1], recv_sem=recv_sem.at[q * 3 + k - 1],
                    device_id=(x, y, 1 - c), device_id_type=MESH)
                cp.wait_send()
                cp.wait_recv()

    out = pl.pallas_call(
        body, name=name,
        out_shape=(*[pltpu.HBM(a.shape, a.dtype) for a in srcs], *[pltpu.HBM(a.shape, a.dtype) for a in dsts]),
        in_specs=[HBM_SPEC] * (2 * n) + [SEM_SPEC, SEM_SPEC, ANY_SPEC],
        out_specs=tuple([HBM_SPEC] * (2 * n)),
        input_output_aliases={i: i for i in range(2 * n)},
        compiler_params=pltpu.CompilerParams(has_side_effects=EFFECT),
    )(*srcs, *dsts, send, recv, after)
    return out[n:]


def _cls(q):
    return WCLASSES[q][1], WCLASSES[q][2]


def _gather_start(shards, after):
    fulls = _place_own(shards)
    return _split_start("gather_start", shards, fulls, after,
                        lambda ref, q, mc, pj: ref,
                        lambda ref, q, mc, k: _shard_view(ref, *_cls(q), mc))


def _gather_wait(started, after):
    return _split_wait("gather_wait", started, after, lambda ref, q, k: _shard_view(ref, *_cls(q), 0))


def _scatter_start(pieces, after):
    lands = _take_own(pieces)
    return _split_start("scatter_start", pieces, lands, after,
                        lambda ref, q, mc, pj: _shard_view(ref, *_cls(q), pj),
                        lambda ref, q, mc, k: ref.at[k])


def _scatter_wait(started, after):
    return _split_wait("scatter_wait", started, after, lambda ref, q, k: ref.at[k])


def _mod_shards(c_all, ada_w, ada_b_sh):
    tn = ADA_SH // 3

    def body(c_ref, w_ref, b_ref, o_ref, cs_ref):
        cv = c_ref[...]
        cs = cv * _sigmoid(cv)
        cs_ref[...] = cs
        o_ref[...] = _dot(cs.astype(bf16), w_ref[...].astype(bf16)) + b_ref[...]

    return pl.pallas_call(
        body, name="mod_shards", grid=(DEPTH, 3),
        out_shape=[SDS((DEPTH, 8, ADA_SH), f32), SDS((8, D), f32)],
        in_specs=[pl.BlockSpec((8, D), lambda l, t: (0, 0)),
                  pl.BlockSpec((None, D, tn), lambda l, t: (l, 0, t)),
                  pl.BlockSpec((None, 1, tn), lambda l, t: (l, 0, t))],
        out_specs=[pl.BlockSpec((None, 8, tn), lambda l, t: (l, 0, t)), pl.BlockSpec((8, D), lambda l, t: (0, 0))],
        compiler_params=_cp("arbitrary", "arbitrary"),
    )(c_all, ada_w, ada_b_sh.reshape(DEPTH, 1, ADA_SH))


def _t5_bucket(dist):
    exact = NUM_BUCKETS // 2
    dd = np.maximum(dist, 1).astype(np.float32)
    large = exact + (np.log(dd / exact) / np.log(MAX_DISTANCE / exact) * (NUM_BUCKETS - exact)).astype(np.int32)
    large = np.minimum(large, NUM_BUCKETS - 1)
    return np.where(dist < exact, dist, large).astype(np.int32)


def _bucket_table():
    i = np.arange(BLK)[:, None]
    j = np.arange(2 * BLK)[None, :]
    rel = i - j + BLK
    return np.stack([_t5_bucket(np.maximum(rel, 0) * d) for d in DILATIONS]).astype(np.int32)


def _band():
    rel = lax.broadcasted_iota(jnp.int32, (BLK, 2 * BLK), 0) - lax.broadcasted_iota(jnp.int32, (BLK, 2 * BLK), 1) + BLK
    return (rel >= 0) & (rel <= BLK)


def _bias_blocks(rel_bias, buckets):
    def body(tab_ref, bk_ref, o_ref):
        h = pl.program_id(0)
        bk = bk_ref[...]
        acc = jnp.zeros((BLK, 2 * BLK), f32)
        for b in range(NUM_BUCKETS):
            acc = jnp.where(bk == b, tab_ref[b, h], acc)
        o_ref[...] = jnp.where(_band(), acc, NEG)

    return pl.pallas_call(
        body, name="bias_blocks", grid=(24,),
        out_shape=SDS((24, BLK, 2 * BLK), f32),
        in_specs=[pl.BlockSpec(memory_space=pltpu.SMEM), pl.BlockSpec((None, BLK, 2 * BLK), lambda h: (h // 8, 0, 0))],
        out_specs=pl.BlockSpec((None, BLK, 2 * BLK), lambda h: (h, 0, 0)),
        compiler_params=_cp("arbitrary"),
    )(rel_bias, buckets)


def _bias_grad(dsaccs, buckets):
    nl = len(dsaccs)

    def body(*refs):
        bk = refs[nl][...]
        tot = refs[0][...]
        for r in refs[1:nl]:
            tot = tot + r[...]
        lane = lax.broadcasted_iota(jnp.int32, (1, 128), 1)
        row = jnp.zeros((1, 128), f32)
        for b in range(NUM_BUCKETS):
            row = jnp.where(lane == b, jnp.sum(jnp.where(bk == b, tot, 0.0)), row)
        refs[nl + 1][...] = row

    return pl.pallas_call(
        body, name="bias_grad", grid=(24,),
        out_shape=SDS((24, 1, 128), f32),
        in_specs=[pl.BlockSpec((None, BLK, 2 * BLK), lambda h: (h, 0, 0))] * nl
                 + [pl.BlockSpec((None, BLK, 2 * BLK), lambda h: (h // 8, 0, 0))],
        out_specs=pl.BlockSpec((None, 1, 128), lambda h: (h, 0, 0)),
        compiler_params=_cp("arbitrary"),
    )(*dsaccs, buckets)


def _ffn_fwd(x, mod9, g3, wg, wu, wd, sub):
    S = x.shape[0]

    def body(x_ref, mod_ref, g_ref, wg_ref, wu_ref, wd_ref, xo_ref, h_ref, a_ref, u_ref, y_ref, acc):
        j = pl.program_id(1)

        @pl.when(j == 0)
        def _():
            h, _, _ = _norm_fwd(x_ref[...], g_ref[sub:sub + 1, :], mod_ref[3 * sub:3 * sub + 1, :], mod_ref[3 * sub + 1:3 * sub + 2, :])
            h_ref[...] = h.astype(bf16)
            acc[...] = jnp.zeros_like(acc)

        h = h_ref[...]
        a = _dot(h, wg_ref[...])
        u = _dot(h, wu_ref[...])
        a_ref[...] = a.astype(bf16)
        u_ref[...] = u.astype(bf16)
        hid = (a * _sigmoid(a) * u).astype(bf16)
        acc[...] += _dot(hid, wd_ref[...])

        @pl.when(j == N_CHIPS - 1)
        def _():
            y = acc[...]
            y_ref[...] = y.astype(bf16)
            xo_ref[...] = x_ref[...] + 0.5 * mod_ref[3 * sub + 2:3 * sub + 3, :] * y

    row = pl.BlockSpec((TM, D), lambda i, j: (i, 0))
    return pl.pallas_call(
        body, name="ffn_fwd", grid=(S // TM, N_CHIPS),
        out_shape=[SDS((S, D), f32), SDS((S, D), bf16), SDS((N_CHIPS, S, FB), bf16), SDS((N_CHIPS, S, FB), bf16), SDS((S, D), bf16)],
        in_specs=[row, pl.BlockSpec((9, D), lambda i, j: (0, 0)), pl.BlockSpec((3, D), lambda i, j: (0, 0)),
                  pl.BlockSpec((None, D, FB), lambda i, j: (j, 0, 0)), pl.BlockSpec((None, D, FB), lambda i, j: (j, 0, 0)),
                  pl.BlockSpec((FB, D), lambda i, j: (j, 0))],
        out_specs=[row, row, pl.BlockSpec((None, TM, FB), lambda i, j: (j, i, 0)), pl.BlockSpec((None, TM, FB), lambda i, j: (j, i, 0)), row],
        scratch_shapes=[pltpu.VMEM((TM, D), f32)],
        compiler_params=_cp("arbitrary", "arbitrary"),
    )(x, mod9, g3, wg, wu, wd)


def _ffn_bwd1(dxo, x, mod9, g3, y, a, u, wg, wu, wd, sub):
    S = x.shape[0]

    def body(dxo_ref, x_ref, mod_ref, g_ref, y_ref, a_ref, u_ref, wg_ref, wu_ref, wd_ref,
             dxi_ref, da_ref, du_ref, hid_ref, dy_ref, sm_ref, acc):
        i, j = pl.program_id(0), pl.program_id(1)
        gate = mod_ref[3 * sub + 2:3 * sub + 3, :]

        @pl.when((i == 0) & (j == 0))
        def _():
            sm_ref[...] = jnp.zeros_like(sm_ref)

        @pl.when(j == 0)
        def _():
            dxo_v = dxo_ref[...]
            dy_ref[...] = (0.5 * gate * dxo_v).astype(bf16)
            sm_ref[2:3, :] += jnp.sum(0.5 * y_ref[...].astype(f32) * dxo_v, axis=0, keepdims=True)
            acc[...] = jnp.zeros_like(acc)

        av, uv = a_ref[...].astype(f32), u_ref[...].astype(f32)
        sg = _sigmoid(av)
        sil = av * sg
        dhid = _dot_nt(dy_ref[...], wd_ref[...])
        da = (dhid * uv * (sg * (1.0 + av * (1.0 - sg)))).astype(bf16)
        du = (dhid * sil).astype(bf16)
        da_ref[...] = da
        du_ref[...] = du
        hid_ref[...] = (sil * uv).astype(bf16)
        acc[...] += _dot_nt(da, wg_ref[...]) + _dot_nt(du, wu_ref[...])

        @pl.when(j == N_CHIPS - 1)
        def _():
            g = g_ref[sub:sub + 1, :]
            scale = mod_ref[3 * sub + 1:3 * sub + 2, :]
            _, xhat, rstd = _norm_fwd(x_ref[...], g, mod_ref[3 * sub:3 * sub + 1, :], scale)
            dx, dshift, dscale, dg = _norm_bwd(acc[...], xhat, rstd, g, scale)
            dxi_ref[...] = dxo_ref[...] + dx
            sm_ref[0:1, :] += dshift
            sm_ref[1:2, :] += dscale
            sm_ref[3:4, :] += dg

    row = pl.BlockSpec((TM, D), lambda i, j: (i, 0))
    hidb = pl.BlockSpec((None, TM, FB), lambda i, j: (j, i, 0))
    wcol = pl.BlockSpec((None, D, FB), lambda i, j: (j, 0, 0))
    return pl.pallas_call(
        body, name="ffn_bwd1", grid=(S // TM, N_CHIPS),
        out_shape=[SDS((S, D), f32), SDS((N_CHIPS, S, FB), bf16), SDS((N_CHIPS, S, FB), bf16), SDS((N_CHIPS, S, FB), bf16),
                   SDS((S, D), bf16), SDS((8, D), f32)],
        in_specs=[row, row, pl.BlockSpec((9, D), lambda i, j: (0, 0)), pl.BlockSpec((3, D), lambda i, j: (0, 0)), row,
                  hidb, hidb, wcol, wcol, pl.BlockSpec((FB, D), lambda i, j: (j, 0))],
        out_specs=[row, hidb, hidb, hidb, row, pl.BlockSpec((8, D), lambda i, j: (0, 0))],
        scratch_shapes=[pltpu.VMEM((TM, D), f32)],
        compiler_params=_cp("arbitrary", "arbitrary"),
    )(dxo, x, mod9, g3, y, a, u, wg, wu, wd)


def _ffn_bwd2(h, da, du, hid, dy):
    S = h.shape[0]
    ni = S // TM

    def body(h_ref, da_ref, du_ref, hid_ref, dy_ref, dwg_ref, dwu_ref, dwd_ref, ag, au, ad):
        i = pl.program_id(1)

        @pl.when(i == 0)
        def _():
            ag[...] = jnp.zeros_like(ag)
            au[...] = jnp.zeros_like(au)
            ad[...] = jnp.zeros_like(ad)

        hv = h_ref[...]
        ag[...] += _dot_tn(hv, da_ref[...])
        au[...] += _dot_tn(hv, du_ref[...])
        ad[...] += _dot_tn(hid_ref[...], dy_ref[...])

        @pl.when(i == ni - 1)
        def _():
            dwg_ref[...] = ag[...].astype(bf16)
            dwu_ref[...] = au[...].astype(bf16)
            dwd_ref[...] = ad[...].astype(bf16)

    row = pl.BlockSpec((TM, D), lambda j, i: (i, 0))
    hidb = pl.BlockSpec((None, TM, FB), lambda j, i: (j, i, 0))
    wcol = pl.BlockSpec((None, D, FB), lambda j, i: (j, 0, 0))
    return pl.pallas_call(
        body, name="ffn_bwd2", grid=(N_CHIPS, ni),
        out_shape=[SDS((N_CHIPS, D, FB), bf16), SDS((N_CHIPS, D, FB), bf16), SDS((N_CHIPS * FB, D), bf16)],
        in_specs=[row, hidb, hidb, hidb, row],
        out_specs=[wcol, wcol, pl.BlockSpec((FB, D), lambda j, i: (j, 0))],
        scratch_shapes=[pltpu.VMEM((D, FB), f32), pltpu.VMEM((D, FB), f32), pltpu.VMEM((FB, D), f32)],
        compiler_params=_cp("arbitrary", "arbitrary"),
    )(h, da, du, hid, dy)


def _mix_qkv(x, mod9, g3, win):
    S = x.shape[0]

    def body(x_ref, mod_ref, g_ref, w_ref, h_ref, o_ref):
        @pl.when(pl.program_id(1) == 0)
        def _():
            h, _, _ = _norm_fwd(x_ref[...], g_ref[1:2, :], mod_ref[3:4, :], mod_ref[4:5, :])
            h_ref[...] = h.astype(bf16)

        o_ref[...] = _dot(h_ref[...], w_ref[...])

    row = pl.BlockSpec((TM, D), lambda i, j: (i, 0))
    return pl.pallas_call(
        body, name="mix_qkv", grid=(S // TM, QKV_W // CB),
        out_shape=[SDS((S, D), bf16), SDS((S, QKV_W), f32)],
        in_specs=[row, pl.BlockSpec((9, D), lambda i, j: (0, 0)), pl.BlockSpec((3, D), lambda i, j: (0, 0)),
                  pl.BlockSpec((D, CB), lambda i, j: (0, j))],
        out_specs=[row, pl.BlockSpec((TM, CB), lambda i, j: (i, j))],
        compiler_params=_cp("arbitrary", "arbitrary"),
    )(x, mod9, g3, win)


def _mix_rest(h, win):
    S = h.shape[0]
    off = QKV_W // CB

    def body(h_ref, w_ref, o_ref):
        o_ref[...] = _dot(h_ref[...], w_ref[...]).astype(bf16)

    return pl.pallas_call(
        body, name="mix_rest", grid=(S // TM, REST_W // CB),
        out_shape=SDS((S, REST_W), bf16),
        in_specs=[pl.BlockSpec((TM, D), lambda i, j: (i, 0)), pl.BlockSpec((D, CB), lambda i, j: (0, off + j))],
        out_specs=pl.BlockSpec((TM, CB), lambda i, j: (i, j)),
        compiler_params=_cp("arbitrary", "arbitrary"),
    )(h, win)


def _attn_fwd(qkv, bias, g):
    S = qkv.shape[0]
    d = DILATIONS[g]
    R = BLK * d
    nb = S // R
    qb, kb, vb = 4 * g, 12 + 4 * g, 24 + 4 * g

    def body(q_ref, kc_ref, kp_ref, vc_ref, vp_ref, b_ref, o_ref, l_ref):
        n = pl.program_id(1)
        col = lax.broadcasted_iota(jnp.int32, (BLK, 2 * BLK), 1)
        first = jnp.where((col < BLK) & (n == 0), NEG, 0.0)

        def step(r, carry):
            sl = pl.ds(r, BLK, stride=d)
            q, kc, kp, vc, vp = q_ref[sl, :], kc_ref[sl, :], kp_ref[sl, :], vc_ref[sl, :], vp_ref[sl, :]
            os, ls = [], []
            for hh in range(2):
                cs = slice(HD * hh, HD * hh + HD)
                qh = q[:, cs].astype(bf16)
                kh = jnp.concatenate([kp[:, cs], kc[:, cs]], axis=0).astype(bf16)
                vh = jnp.concatenate([vp[:, cs], vc[:, cs]], axis=0).astype(bf16)
                s = _dot_nt(qh, kh) * SCALE + b_ref[hh] + first
                m = jnp.max(s, axis=-1, keepdims=True)
                p = jnp.exp(s - m)
                l = jnp.sum(p, axis=-1, keepdims=True)
                os.append(_dot(p.astype(bf16), vh) / l)
                ls.append(jnp.broadcast_to(m + jnp.log(l), (BLK, HD)))
            o_ref[sl, :] = jnp.concatenate(os, axis=1)
            l_ref[sl, :] = jnp.concatenate(ls, axis=1)
            return carry

        lax.fori_loop(0, d, step, 0)

    def blk(cb, prev):
        if prev:
            return pl.BlockSpec((R, 128), lambda hp, n: (jnp.maximum(n - 1, 0), cb + hp))
        return pl.BlockSpec((R, 128), lambda hp, n: (n, cb + hp))

    outb = pl.BlockSpec((R, 128), lambda hp, n: (n, hp))
    return pl.pallas_call(
        body, name=f"attn_fwd_d{d}", grid=(4, nb),
        out_shape=[SDS((S, 512), f32), SDS((S, 512), f32)],
        in_specs=[blk(qb, False), blk(kb, False), blk(kb, True), blk(vb, False), blk(vb, True),
                  pl.BlockSpec((2, BLK, 2 * BLK), lambda hp, n: (4 * g + hp, 0, 0))],
        out_specs=[outb, outb],
        compiler_params=_cp("arbitrary", "arbitrary"),
    )(qkv, qkv, qkv, qkv, qkv, bias)


def _attn_bwd(qkv, do, o, lse, bias, dq_all, dk_all, dv_all, g):
    S = qkv.shape[0]
    d = DILATIONS[g]
    R = BLK * d
    nb = S // R
    qb, kb, vb = 4 * g, 12 + 4 * g, 24 + 4 * g

    def body(q_ref, kc_ref, kp_ref, vc_ref, vp_ref, do_ref, o_ref, l_ref, b_ref, dqi, dki, dvi,
             dq_ref, dk_ref, dv_ref, ds_ref, ck, cv):
        n = pl.program_id(1)
        col = lax.broadcasted_iota(jnp.int32, (BLK, 2 * BLK), 1)
        first = jnp.where((col < BLK) & (n == 0), NEG, 0.0)

        @pl.when(n == 0)
        def _():
            ck[...] = jnp.zeros_like(ck)
            cv[...] = jnp.zeros_like(cv)
            ds_ref[...] = jnp.zeros_like(ds_ref)

        @pl.when(n < nb)
        def _():
            def step(r, carry):
                sl = pl.ds(r, BLK, stride=d)
                q, kc, kp, vc, vp = q_ref[sl, :], kc_ref[sl, :], kp_ref[sl, :], vc_ref[sl, :], vp_ref[sl, :]
                dov, ov, lv = do_ref[sl, :], o_ref[sl, :], l_ref[sl, :]
                dqs, dks, dvs = [], [], []
                for hh in range(2):
                    cs = slice(HD * hh, HD * hh + HD)
                    qh = q[:, cs].astype(bf16)
                    kh = jnp.concatenate([kp[:, cs], kc[:, cs]], axis=0).astype(bf16)
                    vh = jnp.concatenate([vp[:, cs], vc[:, cs]], axis=0).astype(bf16)
                    doh = dov[:, cs]
                    dsum = jnp.sum(doh * ov[:, cs], axis=-1, keepdims=True)
                    s = _dot_nt(qh, kh) * SCALE + b_ref[hh] + first
                    p = jnp.exp(s - lv[:, HD * hh:HD * hh + 1])
                    dohb = doh.astype(bf16)
                    ds = p * (_dot_nt(dohb, vh) - dsum)
                    ds_ref[hh] += ds
                    dsb = ds.astype(bf16)
                    dqs.append(_dot(dsb, kh) * SCALE)
                    dks.append(_dot_tn(dsb, qh) * SCALE)
                    dvs.append(_dot_tn(p.astype(bf16), dohb))
                dq_ref[sl, :] = jnp.concatenate(dqs, axis=1)
                dk = jnp.concatenate(dks, axis=1)
                dv = jnp.concatenate(dvs, axis=1)
                dk_ref[sl, :] = ck[r] + dk[:BLK]
                dv_ref[sl, :] = cv[r] + dv[:BLK]
                ck[r] = dk[BLK:]
                cv[r] = dv[BLK:]
                return carry

            lax.fori_loop(0, d, step, 0)

        @pl.when(n == nb)
        def _():
            def flush(r, carry):
                sl = pl.ds(r, BLK, stride=d)
                dk_ref[sl, :] = ck[r]
                dv_ref[sl, :] = cv[r]
                return carry

            lax.fori_loop(0, d, flush, 0)

    last = nb - 1

    def blk(cb, prev):
        if prev:
            return pl.BlockSpec((R, 128), lambda hp, n: (jnp.maximum(jnp.minimum(n, last) - 1, 0), cb + hp))
        return pl.BlockSpec((R, 128), lambda hp, n: (jnp.minimum(n, last), cb + hp))

    cur = pl.BlockSpec((R, 128), lambda hp, n: (jnp.minimum(n, last), hp))
    anyspec = pl.BlockSpec(memory_space=pl.ANY)
    dqo = pl.BlockSpec((R, 128), lambda hp, n: (jnp.minimum(n, last), 4 * g + hp))
    dko = pl.BlockSpec((R, 128), lambda hp, n: (jnp.maximum(n - 1, 0), 4 * g + hp))
    return pl.pallas_call(
        body, name=f"attn_bwd_d{d}", grid=(4, nb + 1),
        out_shape=[SDS((S, 1536), f32), SDS((S, 1536), f32), SDS((S, 1536), f32), SDS((8, BLK, 2 * BLK), f32)],
        in_specs=[blk(qb, False), blk(kb, False), blk(kb, True), blk(vb, False), blk(vb, True), cur, cur, cur,
                  pl.BlockSpec((2, BLK, 2 * BLK), lambda hp, n: (4 * g + hp, 0, 0)), anyspec, anyspec, anyspec],
        out_specs=[dqo, dko, dko, pl.BlockSpec((2, BLK, 2 * BLK), lambda hp, n: (hp, 0, 0))],
        scratch_shapes=[pltpu.VMEM((d, BLK, 128), f32), pltpu.VMEM((d, BLK, 128), f32)],
        input_output_aliases={9: 0, 10: 1, 11: 2},
        compiler_params=_cp("arbitrary", "arbitrary"),
    )(qkv, qkv, qkv, qkv, qkv, do, o, lse, bias, dq_all, dk_all, dv_all)


def _conv_z(cc, ch, hc, hh, cw_ref, first):
    halo = jnp.where(first, 0.0, hc.astype(f32) * hh.astype(f32))
    T = jnp.concatenate([halo, cc * ch], axis=0)
    z = cw_ref[2:3, :] * T + cw_ref[1:2, :] * pltpu.roll(T, 1, 0) + cw_ref[0:1, :] * pltpu.roll(T, 2, 0)
    return T, z[HALO:]


def _rest_specs(tm, with_next):
    per = tm // HALO
    specs = [pl.BlockSpec((tm, D), functools.partial(lambda i, k: (i, k), k=k)) for k in range(5)]
    specs += [pl.BlockSpec((HALO, D), functools.partial(lambda i, k: (jnp.maximum(i * per - 1, 0), k), k=k)) for k in (1, 2)]
    return specs


def _mix_out_fwd(x, mod9, rest, ogs, lgs, cw, wco, wao, wo):
    S = x.shape[0]
    tm = TMX

    def body(x_ref, mod_ref, cb_ref, cc_ref, ch_ref, gc_ref, ga_ref, hc_ref, hh_ref,
             o0, o1, o2, l0, l1, l2, cw_ref, wco_ref, wao_ref, wo_ref,
             xo_ref, o_ref, lse_ref, yc_ref, ya_ref, out_ref):
        i = pl.program_id(0)
        lv = [l0[...], l1[...], l2[...]]
        mx = jnp.maximum(jnp.maximum(lv[0], lv[1]), lv[2])
        es = [jnp.exp(l - mx) for l in lv]
        den = es[0] + es[1] + es[2]
        o = (es[0] / den) * o0[...] + (es[1] / den) * o1[...] + (es[2] / den) * o2[...]
        o_ref[...] = o
        lse_ref[...] = mx + jnp.log(den)
        _, z = _conv_z(cc_ref[...].astype(f32), ch_ref[...].astype(f32), hc_ref[...], hh_ref[...], cw_ref, i == 0)
        p = (cb_ref[...].astype(f32) * z).astype(bf16)
        yc = _dot(p, wco_ref[...])
        ya = _dot(o.astype(bf16), wao_ref[...])
        yc_ref[...] = yc.astype(bf16)
        ya_ref[...] = ya.astype(bf16)
        merged = _sigmoid(gc_ref[...].astype(f32)) * yc + _sigmoid(ga_ref[...].astype(f32)) * ya
        out = _dot(merged.astype(bf16), wo_ref[...])
        out_ref[...] = out.astype(bf16)
        xo_ref[...] = x_ref[...] + mod_ref[5:6, :] * out

    row = pl.BlockSpec((tm, D), lambda i: (i, 0))
    att = pl.BlockSpec((tm, 512), lambda i: (i, 0))
    full = lambda shp: pl.BlockSpec(shp, lambda i: (0, 0))
    return pl.pallas_call(
        body, name="mix_out_fwd", grid=(S // tm,),
        out_shape=[SDS((S, D), f32), SDS((S, 512), f32), SDS((S, 512), f32), SDS((S, D), bf16), SDS((S, D), bf16), SDS((S, D), bf16)],
        in_specs=[row, full((9, D))] + _rest_specs(tm, False) + [att] * 6 + [full((3, D)), full((D, D)), full((512, D)), full((D, D))],
        out_specs=[row, att, att, row, row, row],
        compiler_params=_cp("arbitrary"),
    )(x, mod9, *([rest] * 7), *ogs, *lgs, cw, wco, wao, wo)


def _mix_out_bwd(dxo, mod9, outv, yc, ya, rest, o, cw, wco, wao, wo):
    S = dxo.shape[0]
    tm = TMX
    ni = S // tm

    def body(dxo_ref, mod_ref, out_ref, yc_ref, ya_ref, cb_ref, cc_ref, ch_ref, gc_ref, ga_ref, hc_ref, hh_ref,
             o_ref, cw_ref, wco_ref, wao_ref, wo_ref,
             dp_ref, dg2_ref, do_ref, dwco_ref, dwao_ref, dwo_ref, sm_ref, aco, aao, ao):
        i = pl.program_id(0)

        @pl.when(i == 0)
        def _():
            sm_ref[...] = jnp.zeros_like(sm_ref)
            aco[...] = jnp.zeros_like(aco)
            aao[...] = jnp.zeros_like(aao)
            ao[...] = jnp.zeros_like(ao)

        dxo_v = dxo_ref[...]
        sm_ref[2:3, :] += jnp.sum(out_ref[...].astype(f32) * dxo_v, axis=0, keepdims=True)
        dout = (mod_ref[5:6, :] * dxo_v).astype(bf16)
        dmerged = _dot_nt(dout, wo_ref[...])
        sc, sa = _sigmoid(gc_ref[...].astype(f32)), _sigmoid(ga_ref[...].astype(f32))
        ycv, yav = yc_ref[...].astype(f32), ya_ref[...].astype(f32)
        ao[...] += _dot_tn((sc * ycv + sa * yav).astype(bf16), dout)
        dyc = (dmerged * sc).astype(bf16)
        dya = (dmerged * sa).astype(bf16)
        dg2_ref[:, :D] = (dmerged * ycv * sc * (1.0 - sc)).astype(bf16)
        dg2_ref[:, D:] = (dmerged * yav * sa * (1.0 - sa)).astype(bf16)
        dp_ref[...] = _dot_nt(dyc, wco_ref[...]).astype(bf16)
        _, z = _conv_z(cc_ref[...].astype(f32), ch_ref[...].astype(f32), hc_ref[...], hh_ref[...], cw_ref, i == 0)
        aco[...] += _dot_tn((cb_ref[...].astype(f32) * z).astype(bf16), dyc)
        do_ref[...] = _dot_nt(dya, wao_ref[...])
        aao[...] += _dot_tn(o_ref[...].astype(bf16), dya)

        @pl.when(i == ni - 1)
        def _():
            dwco_ref[...] = aco[...].astype(bf16)
            dwao_ref[...] = aao[...].astype(bf16)
            dwo_ref[...] = ao[...].astype(bf16)

    row = pl.BlockSpec((tm, D), lambda i: (i, 0))
    att = pl.BlockSpec((tm, 512), lambda i: (i, 0))
    full = lambda shp: pl.BlockSpec(shp, lambda i: (0, 0))
    return pl.pallas_call(
        body, name="mix_out_bwd", grid=(ni,),
        out_shape=[SDS((S, D), bf16), SDS((S, 2 * D), bf16), SDS((S, 512), f32),
                   SDS((D, D), bf16), SDS((512, D), bf16), SDS((D, D), bf16), SDS((8, D), f32)],
        in_specs=[row, full((9, D)), row, row, row] + _rest_specs(tm, False) + [att, full((3, D)), full((D, D)), full((512, D)), full((D, D))],
        out_specs=[row, pl.BlockSpec((tm, 2 * D), lambda i: (i, 0)), att, full((D, D)), full((512, D)), full((D, D)), full((8, D))],
        scratch_shapes=[pltpu.VMEM((D, D), f32), pltpu.VMEM((512, D), f32), pltpu.VMEM((D, D), f32)],
        compiler_params=_cp("arbitrary"),
    )(dxo, mod9, outv, yc, ya, *([rest] * 7), o, cw, wco, wao, wo)


def _conv_bwd(dp, rest, cw):
    S = dp.shape[0]
    tm = TM
    per = tm // HALO
    nh = S // HALO
    ni = S // tm

    def body(dp_ref, dpn_ref, cb_ref, cbn_ref, cc_ref, ch_ref, hc_ref, hh_ref, cw_ref, d3_ref, sm_ref):
        i = pl.program_id(0)

        @pl.when(i == 0)
        def _():
            sm_ref[...] = jnp.zeros_like(sm_ref)

        cc, ch = cc_ref[...].astype(f32), ch_ref[...].astype(f32)
        T, z = _conv_z(cc, ch, hc_ref[...], hh_ref[...], cw_ref, i == 0)
        dpv = dp_ref[...].astype(f32)
        cbv = cb_ref[...].astype(f32)
        dz = dpv * cbv
        dzn = jnp.where(i == ni - 1, 0.0, dpn_ref[...].astype(f32) * cbn_ref[...].astype(f32))
        E = jnp.concatenate([dz, dzn], axis=0)
        ne = tm + HALO
        dT = cw_ref[2:3, :] * E + cw_ref[1:2, :] * pltpu.roll(E, ne - 1, 0) + cw_ref[0:1, :] * pltpu.roll(E, ne - 2, 0)
        dT = dT[:tm]
        d3_ref[:, :D] = (dpv * z).astype(bf16)
        d3_ref[:, D:2 * D] = (dT * ch).astype(bf16)
        d3_ref[:, 2 * D:] = (dT * cc).astype(bf16)
        sm_ref[2:3, :] += jnp.sum(dz * T[HALO:], axis=0, keepdims=True)
        sm_ref[1:2, :] += jnp.sum(dz * pltpu.roll(T, 1, 0)[HALO:], axis=0, keepdims=True)
        sm_ref[0:1, :] += jnp.sum(dz * pltpu.roll(T, 2, 0)[HALO:], axis=0, keepdims=True)

    row = pl.BlockSpec((tm, D), lambda i: (i, 0))
    nxt = pl.BlockSpec((HALO, D), lambda i: (jnp.minimum((i + 1) * per, nh - 1), 0))
    col = lambda k: pl.BlockSpec((tm, D), lambda i: (i, k))
    prv = lambda k: pl.BlockSpec((HALO, D), lambda i: (jnp.maximum(i * per - 1, 0), k))
    return pl.pallas_call(
        body, name="conv_bwd", grid=(ni,),
        out_shape=[SDS((S, 3 * D), bf16), SDS((8, D), f32)],
        in_specs=[row, nxt, col(0), nxt, col(1), col(2), prv(1), prv(2), pl.BlockSpec((3, D), lambda i: (0, 0))],
        out_specs=[pl.BlockSpec((tm, 3 * D), lambda i: (i, 0)), pl.BlockSpec((8, D), lambda i: (0, 0))],
        compiler_params=_cp("arbitrary"),
    )(dp, dp, rest, rest, rest, rest, rest, rest, cw)


_DU_RANGES = ((0, 3), (3, 6), (6, 9), (9, 15), (15, 19))
N_CBLK = IN_W // CB


def _mix_in_bwd_dh(dxo, x, mod9, g3, dus, win):
    S = x.shape[0]

    def body(dxo_ref, x_ref, mod_ref, g_ref, s0, s1, s2, s3, s4, w_ref, dxi_ref, sm_ref, acc):
        i, kb = pl.program_id(0), pl.program_id(1)

        @pl.when((i == 0) & (kb == 0))
        def _():
            sm_ref[...] = jnp.zeros_like(sm_ref)

        @pl.when(kb == 0)
        def _():
            acc[...] = jnp.zeros_like(acc)

        for src, (lo, hi) in zip((s0, s1, s2, s3, s4), _DU_RANGES):
            @pl.when((kb >= lo) & (kb < hi))
            def _(src=src):
                acc[...] += _dot_nt(src[...].astype(bf16), w_ref[...])

        @pl.when(kb == N_CBLK - 1)
        def _():
            g, scale = g_ref[1:2, :], mod_ref[4:5, :]
            _, xhat, rstd = _norm_fwd(x_ref[...], g, mod_ref[3:4, :], scale)
            dx, dshift, dscale, dg = _norm_bwd(acc[...], xhat, rstd, g, scale)
            dxi_ref[...] = dxo_ref[...] + dx
            sm_ref[0:1, :] += dshift
            sm_ref[1:2, :] += dscale
            sm_ref[3:4, :] += dg

    row = pl.BlockSpec((TM, D), lambda i, kb: (i, 0))

    def src_spec(lo, hi):
        return pl.BlockSpec((TM, CB), lambda i, kb: (i, jnp.clip(kb - lo, 0, hi - lo - 1)))

    return pl.pallas_call(
        body, name="mix_in_bwd_dh", grid=(S // TM, N_CBLK),
        out_shape=[SDS((S, D), f32), SDS((8, D), f32)],
        in_specs=[row, row, pl.BlockSpec((9, D), lambda i, kb: (0, 0)), pl.BlockSpec((3, D), lambda i, kb: (0, 0))]
                 + [src_spec(lo, hi) for lo, hi in _DU_RANGES] + [pl.BlockSpec((D, CB), lambda i, kb: (0, kb))],
        out_specs=[row, pl.BlockSpec((8, D), lambda i, kb: (0, 0))],
        scratch_shapes=[pltpu.VMEM((TM, D), f32)],
        compiler_params=_cp("arbitrary", "arbitrary"),
    )(dxo, x, mod9, g3, *dus, win)


def _mix_in_bwd_dw(h, dus):
    S = h.shape[0]
    ni = S // TM

    def body(h_ref, s0, s1, s2, s3, s4, dw_ref, acc):
        kb, i = pl.program_id(0), pl.program_id(1)

        @pl.when(i == 0)
        def _():
            acc[...] = jnp.zeros_like(acc)

        for src, (lo, hi) in zip((s0, s1, s2, s3, s4), _DU_RANGES):
            @pl.when((kb >= lo) & (kb < hi))
            def _(src=src):
                acc[...] += _dot_tn(h_ref[...], src[...].astype(bf16))

        @pl.when(i == ni - 1)
        def _():
            dw_ref[...] = acc[...].astype(bf16)

    def src_spec(lo, hi):
        def imap(kb, i):
            on = (kb >= lo) & (kb < hi)
            return (jnp.where(on, i, 0), jnp.clip(kb - lo, 0, hi - lo - 1))
        return pl.BlockSpec((TM, CB), imap)

    return pl.pallas_call(
        body, name="mix_in_bwd_dw", grid=(N_CBLK, ni),
        out_shape=SDS((D, IN_W), bf16),
        in_specs=[pl.BlockSpec((TM, D), lambda kb, i: (i, 0))] + [src_spec(lo, hi) for lo, hi in _DU_RANGES],
        out_specs=pl.BlockSpec((D, CB), lambda kb, i: (0, kb)),
        scratch_shapes=[pltpu.VMEM((D, CB), f32)],
        compiler_params=_cp("arbitrary", "arbitrary"),
    )(h, *dus)


def _loss_head(x, fg, tgt):
    S = x.shape[0]

    def body(x_ref, g_ref, t_ref, ls_ref, dx_ref, sm_ref):
        i = pl.program_id(0)

        @pl.when(i == 0)
        def _():
            ls_ref[...] = jnp.zeros_like(ls_ref)
            sm_ref[...] = jnp.zeros_like(sm_ref)

        xv, g = x_ref[...], g_ref[...]
        rstd = lax.rsqrt(jnp.mean(xv * xv, axis=-1, keepdims=True) + EPS)
        xhat = xv * rstd
        e = xhat * g - t_ref[...]
        ls_ref[...] += 0.5 * jnp.sum(jnp.mean(e * e, axis=-1, keepdims=True))
        dy = e * (1.0 / D)
        sm_ref[0:1, :] += jnp.sum(dy * xhat, axis=0, keepdims=True)
        dxh = dy * g
        dx_ref[...] = rstd * (dxh - xhat * jnp.mean(dxh * xhat, axis=-1, keepdims=True))

    row = pl.BlockSpec((TM, D), lambda i: (i, 0))
    return pl.pallas_call(
        body, name="loss_head", grid=(S // TM,),
        out_shape=[SDS((8, 128), f32), SDS((S, D), f32), SDS((8, D), f32)],
        in_specs=[row, pl.BlockSpec((1, D), lambda i: (0, 0)), row],
        out_specs=[pl.BlockSpec((8, 128), lambda i: (0, 0)), row, pl.BlockSpec((8, D), lambda i: (0, 0))],
        compiler_params=_cp("arbitrary"),
    )(x, fg, tgt)


def _adam(w, g, m, v):
    m2 = B1 * m + (1.0 - B1) * g
    v2 = B2 * v + (1.0 - B2) * (g * g)
    delta = -LR * ((m2 / BC1) / (jnp.sqrt(v2 / BC2) + AEPS) + WD * w)
    return delta, m2, v2


def _row_tile(rows, cols):
    for tr in (512, 352, 256, 128, 64):
        if rows % tr == 0 and tr * cols * 4 <= (5 << 18):
            return tr
    raise ValueError((rows, cols))


def _sum_slots(land):
    _, R, C = land.shape
    tr = _row_tile(R, C)

    def body(l_ref, t_ref):
        t = l_ref[0].astype(f32)
        for k in range(1, N_CHIPS):
            t = t + l_ref[k].astype(f32)
        t_ref[...] = t

    return pl.pallas_call(
        body, name="sum_slots", grid=(R // tr,),
        out_shape=SDS((R, C), f32),
        in_specs=[pl.BlockSpec((N_CHIPS, tr, C), lambda i: (0, i, 0))],
        out_specs=pl.BlockSpec((tr, C), lambda i: (i, 0)),
        compiler_params=_cp("arbitrary"),
    )(land)


def _adamw_pair(w2, m2, v2, ta, tb, outs, slot):
    R, C = ta.shape
    tr = _row_tile(R, C)
    nrt = R // tr

    def body(w_ref, m_ref, v_ref, ta_ref, tb_ref, g_in, d_in, m_in, v_in, g_ref, d_ref, mo_ref, vo_ref):
        g = ta_ref[...] + tb_ref[...]
        delta, mn, vn = _adam(w_ref[...], g, m_ref[...], v_ref[...])
        g_ref[...] = g
        d_ref[...] = delta
        mo_ref[...] = mn
        vo_ref[...] = vn

    big = pl.BlockSpec((tr, C), lambda i: (slot * nrt + i, 0))
    loc = pl.BlockSpec((tr, C), lambda i: (i, 0))
    anyspec = pl.BlockSpec(memory_space=pl.ANY)
    return pl.pallas_call(
        body, name="adamw_pair", grid=(nrt,),
        out_shape=[SDS(o.shape, f32) for o in outs],
        in_specs=[big, big, big, loc, loc] + [anyspec] * 4,
        out_specs=[big] * 4,
        input_output_aliases={5: 0, 6: 1, 7: 2, 8: 3},
        compiler_params=_cp("arbitrary"),
    )(w2, m2, v2, ta, tb, *outs)


def _adamw_small(w, g, m, v):
    def body(w_ref, g_ref, m_ref, v_ref, d_ref, mo_ref, vo_ref):
        delta, mn, vn = _adam(w_ref[...], g_ref[...], m_ref[...], v_ref[...])
        d_ref[...] = delta
        mo_ref[...] = mn
        vo_ref[...] = vn

    return pl.pallas_call(body, name="adamw_small", out_shape=[SDS(w.shape, f32)] * 3)(w, g, m, v)


def _ada_w_update(cs_all, dmod_sh, w, m, v):
    tr = 256

    def body(cs_ref, dm_ref, w_ref, m_ref, v_ref, g_ref, d_ref, mo_ref, vo_ref):
        g = _dot_tn(cs_ref[...].astype(bf16), dm_ref[...].astype(bf16))
        delta, mn, vn = _adam(w_ref[...], g, m_ref[...], v_ref[...])
        g_ref[...] = g
        d_ref[...] = delta
        mo_ref[...] = mn
        vo_ref[...] = vn

    blk = pl.BlockSpec((None, tr, ADA_SH), lambda l, i: (l, i, 0))
    return pl.pallas_call(
        body, name="ada_w_update", grid=(DEPTH, D // tr),
        out_shape=[SDS(w.shape, f32)] * 4,
        in_specs=[pl.BlockSpec((8, tr), lambda l, i: (0, i)), pl.BlockSpec((None, 8, ADA_SH), lambda l, i: (l, 0, 0)), blk, blk, blk],
        out_specs=[blk] * 4,
        compiler_params=_cp("arbitrary", "arbitrary"),
    )(cs_all, dmod_sh, w, m, v)


def _sum_devices(gathered):
    _, R, C = gathered.shape

    def body(g_ref, o_ref):
        t = g_ref[0]
        for k in range(1, 8):
            t = t + g_ref[k]
        o_ref[...] = t

    return pl.pallas_call(body, name="sum_devices", out_shape=SDS((R, C), f32))(gathered)


def _layer_fwd(x, mod9, g3, cw, W, bias):
    x1, h1, a1, u1, y1 = _ffn_fwd(x, mod9, g3, W["wg0"], W["wu0"], W["wd0"], 0)
    hm, qkv = _mix_qkv(x1, mod9, g3, W["win"])
    rest = _mix_rest(hm, W["win"])
    ogs, lgs = [], []
    for g in range(3):
        og, lg = _attn_fwd(qkv, bias, g)
        ogs.append(og)
        lgs.append(lg)
    x2, o, lse, yc, ya, outv = _mix_out_fwd(x1, mod9, rest, ogs, lgs, cw, W["wco"], W["wao"], W["wo"])
    x3, h3, a3, u3, y3 = _ffn_fwd(x2, mod9, g3, W["wg1"], W["wu1"], W["wd1"], 2)
    saved = dict(x0=x, x1=x1, x2=x2, h1=h1, a1=a1, u1=u1, y1=y1, hm=hm, qkv=qkv, rest=rest, o=o, lse=lse, yc=yc, ya=ya,
                 outv=outv, h3=h3, a3=a3, u3=u3, y3=y3)
    return x3, saved


def _layer_bwd(dx, sv, mod9, g3, cw, W, bias):
    S = dx.shape[0]
    dw = {}
    dx2, da, du, hid, dy, sm3 = _ffn_bwd1(dx, sv["x2"], mod9, g3, sv["y3"], sv["a3"], sv["u3"], W["wg1"], W["wu1"], W["wd1"], 2)
    dw["wg1"], dw["wu1"], dw["wd1"] = _ffn_bwd2(sv["h3"], da, du, hid, dy)
    dp, dg2, do, dw["wco"], dw["wao"], dw["wo"], smo = _mix_out_bwd(
        dx2, mod9, sv["outv"], sv["yc"], sv["ya"], sv["rest"], sv["o"], cw, W["wco"], W["wao"], W["wo"])
    d3, smc = _conv_bwd(dp, sv["rest"], cw)
    dq = lax.empty((S, 1536), f32)
    dk = lax.empty((S, 1536), f32)
    dv = lax.empty((S, 1536), f32)
    dsaccs = []
    for g in range(3):
        dq, dk, dv, dsg = _attn_bwd(sv["qkv"], do, sv["o"], sv["lse"], bias, dq, dk, dv, g)
        dsaccs.append(dsg)
    dus = (dq, dk, dv, d3, dg2)
    dx1, smm = _mix_in_bwd_dh(dx2, sv["x1"], mod9, g3, dus, W["win"])
    dw["win"] = _mix_in_bwd_dw(sv["hm"], dus)
    dx0, da, du, hid, dy, sm1 = _ffn_bwd1(dx1, sv["x0"], mod9, g3, sv["y1"], sv["a1"], sv["u1"], W["wg0"], W["wu0"], W["wd0"], 0)
    dw["wg0"], dw["wu0"], dw["wd0"] = _ffn_bwd2(sv["h1"], da, du, hid, dy)
    dmod = jnp.concatenate([sm1[0:3], smm[0:2], smo[2:3], sm3[0:3]], axis=0)
    dng = jnp.concatenate([sm1[3:4], smm[3:4], sm3[3:4]], axis=0)
    return dx0, dw, dmod, dng, smc[0:3], jnp.concatenate(dsaccs, axis=0)


def _chip_cols(a, chip, width):
    return lax.dynamic_slice_in_dim(a, chip * width, width, axis=a.ndim - 1)


def kernel(x, c, ada_w, ada_b, norm_g, ffn_w_gate, ffn_w_up, ffn_w_down, w_in, conv_w, w_conv_out, w_attn_out, w_o, rel_bias, final_g, loss_target, m_ada_w, m_ada_b, m_norm_g, m_ffn_w_gate, m_ffn_w_up, m_ffn_w_down, m_w_in, m_conv_w, m_w_conv_out, m_w_attn_out, m_w_o, m_rel_bias, m_final_g, v_ada_w, v_ada_b, v_norm_g, v_ffn_w_gate, v_ffn_w_up, v_ffn_w_down, v_w_in, v_conv_w, v_w_conv_out, v_w_attn_out, v_w_o, v_rel_bias, v_final_g):
    ix, iy, ic = lax.axis_index("x"), lax.axis_index("y"), lax.axis_index("c")
    chip = 2 * ix + iy
    dev = 4 * ix + 2 * iy + ic
    xs = x[0]
    S = xs.shape[0]
    qd = D // N_CHIPS

    pad8 = lambda a: jnp.pad(a, ((0, -a.shape[0] % 8), (0, 0)))
    pack = jnp.concatenate([pad8(c), pad8(norm_g.reshape(3, D)), pad8(conv_w.reshape(3, D))], axis=0)
    g1 = _allgather_small(pack).reshape(8, 24, D)
    c_all = g1[:, 0]
    by_chip = g1[0::2]
    ng_full = jnp.concatenate([by_chip[j, 8:11].reshape(DEPTH, 3, qd) for j in range(N_CHIPS)], axis=-1)
    cw_full = jnp.concatenate([by_chip[j, 16:19].reshape(DEPTH, 3, qd) for j in range(N_CHIPS)], axis=-1)
    mod_sh, cs_all = _mod_shards(c_all, ada_w, _chip_cols(ada_b, chip, ADA_SH))
    g2 = _allgather_small(mod_sh.reshape(DEPTH * 8, ADA_SH)).reshape(8, DEPTH, 8, ADA_SH)
    mine = lax.dynamic_index_in_dim(g2[0::2], dev, axis=2, keepdims=False)
    mod = jnp.transpose(mine, (1, 0, 2)).reshape(DEPTH, 9, D)

    buckets = jnp.asarray(_bucket_table())
    bias = _bias_blocks(rel_bias, buckets)

    def layer_shards(l):
        shards = [ffn_w_gate[l, 0], ffn_w_up[l, 0], ffn_w_down[l, 0], ffn_w_gate[l, 1], ffn_w_up[l, 1], ffn_w_down[l, 1],
                  w_in[l], w_conv_out[l], w_attn_out[l], w_o[l]]
        return [s.astype(bf16) for s in shards]

    Ws, saves, mods = [], [], []
    xc = xs
    started = _gather_start(layer_shards(0), jnp.zeros((8, 128), f32))
    for l in range(DEPTH):
        full = _gather_wait(started, xc)
        modl = mod[l]
        if l + 1 < DEPTH:
            started = _gather_start(layer_shards(l + 1), full[0])
            modl = modl + started[-1][0, 0]
        W = {name: f for (name, _, _), f in zip(WCLASSES, full)}
        Ws.append(W)
        mods.append(modl)
        xc, sv = _layer_fwd(xc, modl, ng_full[l], cw_full[l], W, bias)
        saves.append(sv)

    ls, dx, smf = _loss_head(xc, final_g.reshape(1, D), loss_target[0])
    loss = lax.psum(ls[0, 0], ("x", "y", "c"))

    params = dict(wg=ffn_w_gate, wu=ffn_w_up, wd=ffn_w_down, win=w_in, wco=w_conv_out, wao=w_attn_out, wo=w_o)
    moms = dict(wg=m_ffn_w_gate, wu=m_ffn_w_up, wd=m_ffn_w_down, win=m_w_in, wco=m_w_conv_out, wao=m_w_attn_out, wo=m_w_o)
    vars_ = dict(wg=v_ffn_w_gate, wu=v_ffn_w_up, wd=v_ffn_w_down, win=v_w_in, wco=v_w_conv_out, wao=v_w_attn_out, wo=v_w_o)
    flat = lambda a: a.reshape(-1, a.shape[-1])
    big_out = {k: [lax.empty(flat(p).shape, f32) for _ in range(4)] for k, p in params.items()}
    dmods, dngs, dcws, dsaccs = [None] * DEPTH, [None] * DEPTH, [None] * DEPTH, [None] * DEPTH
    def finish(l, started, after):
        lands = _scatter_wait(started, after)
        ts = [_sum_slots(ld) for ld in lands]
        tsib = _swap_sibling(ts)
        for q, (name, _, _) in enumerate(WCLASSES):
            key = name.rstrip("01")
            slot = 2 * l + int(name[-1]) if name[-1] in "01" else l
            big_out[key] = _adamw_pair(flat(params[key]), flat(moms[key]), flat(vars_[key]), ts[q], tsib[q], big_out[key], slot)

    pending = None
    for l in reversed(range(DEPTH)):
        modl = mods[l]
        if pending is not None:
            modl = modl + pending[1][-1][0, 0]
        dx, dw, dmods[l], dngs[l], dcws[l], dsaccs[l] = _layer_bwd(dx, saves[l], modl, ng_full[l], cw_full[l], Ws[l], bias)
        started = _scatter_start([dw[name] for name, _, _ in WCLASSES], dx)
        if pending is not None:
            finish(pending[0], pending[1], started[-1])
        pending = (l, started)
    finish(pending[0], pending[1], dx)

    drb = jnp.transpose(_bias_grad(dsaccs, buckets)[:, 0, :NUM_BUCKETS])
    drb_row = jnp.pad(drb.reshape(1, NUM_BUCKETS * 24), ((0, 0), (0, D - NUM_BUCKETS * 24)))
    pack2 = jnp.concatenate([pad8(a) for a in dmods] + [pad8(a) for a in dngs] + [pad8(a) for a in dcws] + [smf, pad8(drb_row)], axis=0)
    n_rows = pack2.shape[0]
    g3 = _allgather_small(pack2).reshape(8, n_rows, D)
    tot = _sum_devices(g3)
    o_ng, o_cw, o_fg, o_rb = 16 * DEPTH, 24 * DEPTH, 32 * DEPTH, 32 * DEPTH + 8
    g_ada_b = jnp.stack([tot[16 * l:16 * l + 9] for l in range(DEPTH)]).reshape(DEPTH, 9 * D)
    g_norm_g = _chip_cols(jnp.stack([tot[o_ng + 8 * l:o_ng + 8 * l + 3] for l in range(DEPTH)]), chip, qd)
    g_conv_w = _chip_cols(jnp.stack([tot[o_cw + 8 * l:o_cw + 8 * l + 3] for l in range(DEPTH)]), chip, qd)
    g_final_g = tot[o_fg]
    g_rel_bias = tot[o_rb, :NUM_BUCKETS * 24].reshape(NUM_BUCKETS, 24)
    dmod_all = jnp.stack([g3[:, 16 * l:16 * l + 9].reshape(8, 9 * D) for l in range(DEPTH)])
    dmod_sh = _chip_cols(dmod_all, chip, ADA_SH)
    g_ada_w, d_ada_w, nm_ada_w, nv_ada_w = _ada_w_update(cs_all, dmod_sh, ada_w, m_ada_w, v_ada_w)

    def small(w, g, m, v):
        shp = w.shape
        to2 = lambda a: a.reshape(-1, shp[-1])
        return [o.reshape(shp) for o in _adamw_small(to2(w), to2(g), to2(m), to2(v))]

    d_ada_b, nm_ada_b, nv_ada_b = small(ada_b, g_ada_b, m_ada_b, v_ada_b)
    d_norm_g, nm_norm_g, nv_norm_g = small(norm_g, g_norm_g, m_norm_g, v_norm_g)
    d_conv_w, nm_conv_w, nv_conv_w = small(conv_w, g_conv_w, m_conv_w, v_conv_w)
    d_rel_bias, nm_rel_bias, nv_rel_bias = small(rel_bias, g_rel_bias, m_rel_bias, v_rel_bias)
    d_final_g, nm_final_g, nv_final_g = small(final_g, g_final_g, m_final_g, v_final_g)

    def big(key, which):
        return big_out[key][which].reshape(params[key].shape)

    grads = [g_ada_w, g_ada_b, g_norm_g, big("wg", 0), big("wu", 0), big("wd", 0), big("win", 0), g_conv_w, big("wco", 0),
             big("wao", 0), big("wo", 0), g_rel_bias, g_final_g]
    deltas = [d_ada_w, d_ada_b, d_norm_g, big("wg", 1), big("wu", 1), big("wd", 1), big("win", 1), d_conv_w, big("wco", 1),
              big("wao", 1), big("wo", 1), d_rel_bias, d_final_g]
    new_m = [nm_ada_w, nm_ada_b, nm_norm_g, big("wg", 2), big("wu", 2), big("wd", 2), big("win", 2), nm_conv_w, big("wco", 2),
             big("wao", 2), big("wo", 2), nm_rel_bias, nm_final_g]
    new_v = [nv_ada_w, nv_ada_b, nv_norm_g, big("wg", 3), big("wu", 3), big("wd", 3), big("win", 3), nv_conv_w, big("wco", 3),
             big("wao", 3), big("wo", 3), nv_rel_bias, nv_final_g]
    return (loss, dx[None], *grads, *deltas, *new_m, *new_v)
```

```python
import functools

import numpy as np
import jax
import jax.numpy as jnp
from jax import lax
from jax.experimental import pallas as pl
from jax.experimental.pallas import tpu as pltpu

f32, bf16 = jnp.float32, jnp.bfloat16
SDS = jax.ShapeDtypeStruct
MESH = pl.DeviceIdType.MESH

D = 1024
DEPTH = 4
N_CHIPS = 4
FB = 704
HD = 64
QKV_W = 4608
REST_W = 5120
IN_W = QKV_W + REST_W
WIN_SH = IN_W // N_CHIPS
ADA_SH = 9 * D // N_CHIPS
BLK = 128
DILATIONS = (1, 4, 16)
NUM_BUCKETS, MAX_DISTANCE = 32, 2048
EPS = 1e-6
NEG = -1e30
SCALE = HD ** -0.5
LR, B1, B2, AEPS, WD, STEP = 0.001, 0.9, 0.999, 1e-08, 0.01, 10
BC1 = 1.0 - B1 ** STEP
BC2 = 1.0 - B2 ** STEP
VMEM_LIMIT = 56 * 1024 * 1024
TM = 512
TMX = 256
HALO = 16
CB = 512


def _cp(*sem):
    return pltpu.CompilerParams(dimension_semantics=sem if sem else None, vmem_limit_bytes=VMEM_LIMIT)


def _dot(a, b):
    return jnp.dot(a, b, preferred_element_type=f32)


def _dot_nt(a, b):
    return lax.dot_general(a, b, (((1,), (1,)), ((), ())), preferred_element_type=f32)


def _dot_tn(a, b):
    return lax.dot_general(a, b, (((0,), (0,)), ((), ())), preferred_element_type=f32)


def _sigmoid(x):
    return 1.0 / (1.0 + jnp.exp(-x))


def _norm_fwd(x, g, shift, scale):
    rstd = lax.rsqrt(jnp.mean(x * x, axis=-1, keepdims=True) + EPS)
    xhat = x * rstd
    return xhat * g * (1.0 + scale) + shift, xhat, rstd


def _norm_bwd(dh, xhat, rstd, g, scale):
    dshift = jnp.sum(dh, axis=0, keepdims=True)
    dscale = jnp.sum(dh * xhat * g, axis=0, keepdims=True)
    dg = jnp.sum(dh * xhat * (1.0 + scale), axis=0, keepdims=True)
    dxh = dh * (g * (1.0 + scale))
    dx = rstd * (dxh - xhat * jnp.mean(dxh * xhat, axis=-1, keepdims=True))
    return dx, dshift, dscale, dg


def _allgather_small(xp):
    m_per, n = xp.shape

    def body(x_ref, out_ref, send_sems, recv_sems, local_sem):
        x, y, c = lax.axis_index("x"), lax.axis_index("y"), lax.axis_index("c")
        me, sibling = (x, y, c), (x, y, 1 - c)
        chips = [(1 - x, y), (x, 1 - y), (1 - x, 1 - y)]

        def rows(px, py, pc):
            return out_ref.at[pl.ds((4 * px + 2 * py + pc) * m_per, m_per), :]

        def copy(k, block, to, src=None):
            return pltpu.make_async_remote_copy(
                src_ref=rows(*block) if src is None else src, dst_ref=rows(*block),
                send_sem=send_sems.at[k], recv_sem=recv_sems.at[k], device_id=to, device_id_type=MESH)

        mine = pltpu.make_async_copy(x_ref, rows(*me), local_sem)
        mine.start()
        first = [copy(0, me, sibling, src=x_ref)]
        first += [copy(1 + j, me, (*chip, c), src=x_ref) for j, chip in enumerate(chips)]
        for cp in first:
            cp.start()
        passed = [copy(4 + j, (*chip, c), sibling) for j, chip in enumerate(chips)]
        for j, chip in enumerate(chips):
            copy(1 + j, (*chip, c), me).wait_recv()
            passed[j].start()
        copy(0, sibling, me).wait_recv()
        for j, chip in enumerate(chips):
            copy(4 + j, (*chip, 1 - c), me).wait_recv()
        for cp in first + passed:
            cp.wait_send()
        mine.wait()

    return pl.pallas_call(
        body, name="allgather_small",
        out_shape=SDS((8 * m_per, n), xp.dtype),
        in_specs=[pl.BlockSpec(memory_space=pltpu.VMEM)],
        out_specs=pl.BlockSpec(memory_space=pltpu.VMEM),
        scratch_shapes=[pltpu.SemaphoreType.DMA((7,)), pltpu.SemaphoreType.DMA((7,)), pltpu.SemaphoreType.DMA],
        compiler_params=pltpu.CompilerParams(vmem_limit_bytes=VMEM_LIMIT),
    )(xp)


WCLASSES = (
    ("wg0", "lead", (D, FB)), ("wu0", "lead", (D, FB)), ("wd0", "row", (FB, D)),
    ("wg1", "lead", (D, FB)), ("wu1", "lead", (D, FB)), ("wd1", "row", (FB, D)),
    ("win", "col", (D, WIN_SH)), ("wco", "row", (D // N_CHIPS, D)), ("wao", "col", (512, D // N_CHIPS)),
    ("wo", "row", (D // N_CHIPS, D)),
)
NCLS = len(WCLASSES)


def _full_shape(kind, shp):
    if kind == "lead":
        return (N_CHIPS,) + shp
    if kind == "row":
        return (N_CHIPS * shp[0], shp[1])
    return (shp[0], N_CHIPS * shp[1])


def _shard_view(ref, kind, shp, j):
    if kind == "lead":
        return ref.at[j]
    if kind == "row":
        return ref.at[pl.ds(j * shp[0], shp[0]), :]
    return ref.at[:, pl.ds(j * shp[1], shp[1])]


def _half(ref, shp, h):
    hr = shp[0] // 2
    return ref.at[pl.ds(pl.multiple_of(h * hr, 16), hr), :]


def _gather_weights(shards):
    n = NCLS

    def body(*refs):
        ins, outs = refs[:n], refs[n:2 * n]
        send1, recv1, send2, recv2, lsem = refs[2 * n:]
        x, y, c = lax.axis_index("x"), lax.axis_index("y"), lax.axis_index("c")
        chip = 2 * x + y
        sibling = (x, y, 1 - c)

        for mc in range(N_CHIPS):
            @pl.when(chip == mc)
            def _(mc=mc):
                local = []
                for q, (_, kind, shp) in enumerate(WCLASSES):
                    cp = pltpu.make_async_copy(ins[q], _shard_view(outs[q], kind, shp, mc), lsem.at[q])
                    cp.start()
                    local.append(cp)
                sends = []
                for k in (1, 2, 3):
                    pj = mc ^ k
                    for q, (_, kind, shp) in enumerate(WCLASSES):
                        cp = pltpu.make_async_remote_copy(
                            src_ref=_half(ins[q], shp, c), dst_ref=_half(_shard_view(outs[q], kind, shp, mc), shp, c),
                            send_sem=send1.at[q * 3 + k - 1], recv_sem=recv1.at[q * 3 + k - 1],
                            device_id=(pj >> 1, pj & 1, c), device_id_type=MESH)
                        cp.start()
                        sends.append(cp)
                for k in (1, 2, 3):
                    pj = mc ^ k
                    for q, (_, kind, shp) in enumerate(WCLASSES):
                        landed = _half(_shard_view(outs[q], kind, shp, pj), shp, c)
                        pltpu.make_async_remote_copy(
                            src_ref=landed, dst_ref=landed, send_sem=send1.at[q * 3 + k - 1], recv_sem=recv1.at[q * 3 + k - 1],
                            device_id=(pj >> 1, pj & 1, c), device_id_type=MESH).wait_recv()
                        cp = pltpu.make_async_remote_copy(
                            src_ref=landed, dst_ref=landed, send_sem=send2.at[q * 3 + k - 1], recv_sem=recv2.at[q * 3 + k - 1],
                            device_id=sibling, device_id_type=MESH)
                        cp.start()
                        sends.append(cp)
                for k in (1, 2, 3):
                    pj = mc ^ k
                    for q, (_, kind, shp) in enumerate(WCLASSES):
                        other = _half(_shard_view(outs[q], kind, shp, pj), shp, 1 - c)
                        pltpu.make_async_remote_copy(
                            src_ref=other, dst_ref=other, send_sem=send2.at[q * 3 + k - 1], recv_sem=recv2.at[q * 3 + k - 1],
                            device_id=sibling, device_id_type=MESH).wait_recv()
                for cp in sends:
                    cp.wait_send()
                for cp in local:
                    cp.wait()

    anyspec = pl.BlockSpec(memory_space=pl.ANY)
    return pl.pallas_call(
        body, name="gather_weights",
        out_shape=[SDS(_full_shape(kind, shp), bf16) for _, kind, shp in WCLASSES],
        in_specs=[anyspec] * n, out_specs=[anyspec] * n,
        scratch_shapes=[pltpu.SemaphoreType.DMA((3 * n,)), pltpu.SemaphoreType.DMA((3 * n,)),
                        pltpu.SemaphoreType.DMA((3 * n,)), pltpu.SemaphoreType.DMA((3 * n,)),
                        pltpu.SemaphoreType.DMA((n,))],
    )(*shards)


def _scatter_grads(pieces):
    n = NCLS

    def body(*refs):
        ins, outs = refs[:n], refs[n:2 * n]
        send1, recv1, lsem = refs[2 * n:]
        x, y, c = lax.axis_index("x"), lax.axis_index("y"), lax.axis_index("c")
        chip = 2 * x + y

        for mc in range(N_CHIPS):
            @pl.when(chip == mc)
            def _(mc=mc):
                local, sends = [], []
                for q, (_, kind, shp) in enumerate(WCLASSES):
                    cp = pltpu.make_async_copy(_shard_view(ins[q], kind, shp, mc), outs[q].at[0], lsem.at[q])
                    cp.start()
                    local.append(cp)
                for k in (1, 2, 3):
                    pj = mc ^ k
                    for q, (_, kind, shp) in enumerate(WCLASSES):
                        cp = pltpu.make_async_remote_copy(
                            src_ref=_shard_view(ins[q], kind, shp, pj), dst_ref=outs[q].at[k],
                            send_sem=send1.at[q * 3 + k - 1], recv_sem=recv1.at[q * 3 + k - 1],
                            device_id=(pj >> 1, pj & 1, c), device_id_type=MESH)
                        cp.start()
                        sends.append(cp)
                for cp in sends:
                    cp.wait_recv()
                for cp in sends:
                    cp.wait_send()
                for cp in local:
                    cp.wait()

    anyspec = pl.BlockSpec(memory_space=pl.ANY)
    return pl.pallas_call(
        body, name="scatter_grads",
        out_shape=[SDS((N_CHIPS,) + shp, bf16) for _, _, shp in WCLASSES],
        in_specs=[anyspec] * n, out_specs=[anyspec] * n,
        scratch_shapes=[pltpu.SemaphoreType.DMA((3 * n,)), pltpu.SemaphoreType.DMA((3 * n,)), pltpu.SemaphoreType.DMA((n,))],
    )(*pieces)


def _swap_sibling(ts):
    n = len(ts)

    def body(*refs):
        ins, outs = refs[:n], refs[n:2 * n]
        send, recv = refs[2 * n:]
        x, y, c = lax.axis_index("x"), lax.axis_index("y"), lax.axis_index("c")
        cps = []
        for q in range(n):
            cp = pltpu.make_async_remote_copy(src_ref=ins[q], dst_ref=outs[q], send_sem=send.at[q], recv_sem=recv.at[q],
                                              device_id=(x, y, 1 - c), device_id_type=MESH)
            cp.start()
            cps.append(cp)
        for cp in cps:
            cp.wait_recv()
        for cp in cps:
            cp.wait_send()

    anyspec = pl.BlockSpec(memory_space=pl.ANY)
    return pl.pallas_call(
        body, name="swap_sibling",
        out_shape=[SDS(t.shape, t.dtype) for t in ts],
        in_specs=[anyspec] * n, out_specs=[anyspec] * n,
        scratch_shapes=[pltpu.SemaphoreType.DMA((n,)), pltpu.SemaphoreType.DMA((n,))],
    )(*ts)


HBM_SPEC = pl.BlockSpec(memory_space=pltpu.HBM)
SEM_SPEC = pl.BlockSpec(memory_space=pltpu.SEMAPHORE)
ANY_SPEC = pl.BlockSpec(memory_space=pl.ANY)
EFFECT = pltpu.SideEffectType.DATAFLOW_SIDE_EFFECTING
N_COPIES = 3 * NCLS


def _in_hbm(a):
    return pltpu.with_memory_space_constraint(a, pltpu.HBM)


def _chip_index():
    return 2 * lax.axis_index("x") + lax.axis_index("y")


def _place_own(shards):
    n = NCLS

    def body(*refs):
        ins, outs, lsem = refs[:n], refs[n:2 * n], refs[2 * n]
        chip = _chip_index()
        for mc in range(N_CHIPS):
            @pl.when(chip == mc)
            def _(mc=mc):
                cps = [pltpu.make_async_copy(ins[q], _shard_view(outs[q], kind, shp, mc), lsem.at[q])
                       for q, (_, kind, shp) in enumerate(WCLASSES)]
                for cp in cps:
                    cp.start()
                for cp in cps:
                    cp.wait()

    return pl.pallas_call(
        body, name="place_own",
        out_shape=[SDS(_full_shape(kind, shp), bf16) for _, kind, shp in WCLASSES],
        in_specs=[ANY_SPEC] * n, out_specs=[ANY_SPEC] * n,
        scratch_shapes=[pltpu.SemaphoreType.DMA((n,))],
    )(*shards)


def _take_own(pieces):
    n = NCLS

    def body(*refs):
        ins, outs, lsem = refs[:n], refs[n:2 * n], refs[2 * n]
        chip = _chip_index()
        for mc in range(N_CHIPS):
            @pl.when(chip == mc)
            def _(mc=mc):
                cps = [pltpu.make_async_copy(_shard_view(ins[q], kind, shp, mc), outs[q].at[0], lsem.at[q])
                       for q, (_, kind, shp) in enumerate(WCLASSES)]
                for cp in cps:
                    cp.start()
                for cp in cps:
                    cp.wait()

    return pl.pallas_call(
        body, name="take_own",
        out_shape=[SDS((N_CHIPS,) + shp, bf16) for _, _, shp in WCLASSES],
        in_specs=[ANY_SPEC] * n, out_specs=[ANY_SPEC] * n,
        scratch_shapes=[pltpu.SemaphoreType.DMA((n,))],
    )(*pieces)


def _split_start(name, srcs, dsts, after, src_view, dst_view):
    n = NCLS

    def body(*refs):
        src, dst = refs[:n], refs[n:2 * n]
        send, recv = refs[2 * n + 1], refs[2 * n + 2]
        token = refs[-1]
        c = lax.axis_index("c")
        chip = _chip_index()
        for mc in range(N_CHIPS):
            @pl.when(chip == mc)
            def _(mc=mc):
                for k in (1, 2, 3):
                    pj = mc ^ k
                    for q in range(n):
                        pltpu.make_async_remote_copy(
                            src_ref=src_view(src[q], q, mc, pj), dst_ref=dst_view(dst[q], q, mc, k),
                            send_sem=send.at[q * 3 + k - 1], recv_sem=recv.at[q * 3 + k - 1],
                            device_id=(pj >> 1, pj & 1, c), device_id_type=MESH).start()
        token[...] = jnp.zeros_like(token)

    return pl.pallas_call(
        body, name=name,
        out_shape=(pltpu.SemaphoreType.DMA((N_COPIES,)), pltpu.SemaphoreType.DMA((N_COPIES,)),
                   *[pltpu.HBM(a.shape, a.dtype) for a in srcs], *[pltpu.HBM(a.shape, a.dtype) for a in dsts], SDS((8, 128), f32)),
        in_specs=[HBM_SPEC] * (2 * n) + [ANY_SPEC],
        out_specs=(SEM_SPEC, SEM_SPEC, *([HBM_SPEC] * (2 * n)), pl.BlockSpec(memory_space=pltpu.VMEM)),
        input_output_aliases={i: 2 + i for i in range(2 * n)},
        compiler_params=pltpu.CompilerParams(has_side_effects=EFFECT),
    )(*[_in_hbm(a) for a in srcs], *[_in_hbm(a) for a in dsts], after)


def _split_wait(name, started, after, arrival_view):
    n = NCLS
    send, recv = started[0], started[1]
    srcs, dsts = started[2:2 + n], started[2 + n:2 + 2 * n]

    def body(*refs):
        src, dst = refs[:n], refs[n:2 * n]
        send_sem, recv_sem = refs[2 * n], refs[2 * n + 1]
        x, y, c = lax.axis_index("x"), lax.axis_index("y"), lax.axis_index("c")
        for k in (1, 2, 3):
            for q in range(n):
                arrival = arrival_view(dst[q], q, k)
                cp = pltpu.make_async_remote_copy(
                    src_ref=arrival, dst_ref=arrival, send_sem=send_sem.at[q * 3 + k - 1], recv_sem=recv_sem.at[q * 3 + k - 1],
                    device_id=(x, y, 1 - c), device_id_type=MESH)
                cp.wait_send()
                cp.wait_recv()

    out = pl.pallas_call(
        body, name=name,
        out_shape=(*[pltpu.HBM(a.shape, a.dtype) for a in srcs], *[pltpu.HBM(a.shape, a.dtype) for a in dsts]),
        in_specs=[HBM_SPEC] * (2 * n) + [SEM_SPEC, SEM_SPEC, ANY_SPEC],
        out_specs=tuple([HBM_SPEC] * (2 * n)),
        input_output_aliases={i: i for i in range(2 * n)},
        compiler_params=pltpu.CompilerParams(has_side_effects=EFFECT),
    )(*srcs, *dsts, send, recv, after)
    return out[n:]


def _cls(q):
    return WCLASSES[q][1], WCLASSES[q][2]


def _gather_start(shards, after):
    fulls = _place_own(shards)
    return _split_start("gather_start", shards, fulls, after,
                        lambda ref, q, mc, pj: ref,
                        lambda ref, q, mc, k: _shard_view(ref, *_cls(q), mc))


def _gather_wait(started, after):
    return _split_wait("gather_wait", started, after, lambda ref, q, k: _shard_view(ref, *_cls(q), 0))


def _scatter_start(pieces, after):
    lands = _take_own(pieces)
    return _split_start("scatter_start", pieces, lands, after,
                        lambda ref, q, mc, pj: _shard_view(ref, *_cls(q), pj),
                        lambda ref, q, mc, k: ref.at[k])


def _scatter_wait(started, after):
    return _split_wait("scatter_wait", started, after, lambda ref, q, k: ref.at[k])


GROUPS = {"A": (0, 1, 2), "B": (6,), "C": (7, 8, 9), "D": (3, 4, 5)}


def _own_spec(kind, shp, tr):
    R, C = shp
    if kind == "lead":
        return pl.BlockSpec((None, tr, C), lambda i, chip: (chip[0], i, 0))
    if kind == "row":
        return pl.BlockSpec((tr, C), lambda i, chip: (chip[0] * (R // tr) + i, 0))
    return pl.BlockSpec((tr, C), lambda i, chip: (i, chip[0]))


def _cast_place(shards, kind, shp, chip_arr):
    n = len(shards)
    R, C = shp
    tr = _row_tile(R, C)

    def body(chip_ref, *refs):
        for q in range(n):
            refs[n + q][...] = refs[q][...].astype(bf16)

    return pl.pallas_call(
        body, name="cast_place",
        grid_spec=pltpu.PrefetchScalarGridSpec(
            num_scalar_prefetch=1, grid=(R // tr,),
            in_specs=[pl.BlockSpec((tr, C), lambda i, chip: (i, 0))] * n,
            out_specs=[_own_spec(kind, shp, tr)] * n),
        out_shape=[SDS(_full_shape(kind, shp), bf16)] * n,
        compiler_params=_cp("arbitrary"),
    )(chip_arr, *shards)


def _sum_own_slots(piece, land, kind, shp, chip_arr):
    R, C = shp
    tr = _row_tile(R, C)

    def body(chip_ref, p_ref, l_ref, t_ref):
        t = p_ref[...].astype(f32)
        for k in range(N_CHIPS - 1):
            t = t + l_ref[k].astype(f32)
        t_ref[...] = t

    return pl.pallas_call(
        body, name="sum_own_slots",
        grid_spec=pltpu.PrefetchScalarGridSpec(
            num_scalar_prefetch=1, grid=(R // tr,),
            in_specs=[_own_spec(kind, shp, tr), pl.BlockSpec((N_CHIPS - 1, tr, C), lambda i, chip: (0, i, 0))],
            out_specs=pl.BlockSpec((tr, C), lambda i, chip: (i, 0))),
        out_shape=SDS((R, C), f32),
        compiler_params=_cp("arbitrary"),
    )(chip_arr, piece, land)


def _xfer_start(name, arrays, ng, after, src_view, dst_view):
    na = len(arrays)

    def body(*refs):
        arr = refs[:na]
        send, recv, token = refs[na + 1], refs[na + 2], refs[-1]
        c = lax.axis_index("c")
        chip = _chip_index()
        for mc in range(N_CHIPS):
            @pl.when(chip == mc)
            def _(mc=mc):
                for k in (1, 2, 3):
                    pj = mc ^ k
                    for i in range(ng):
                        pltpu.make_async_remote_copy(
                            src_ref=src_view(arr, i, mc, pj), dst_ref=dst_view(arr, i, mc, k),
                            send_sem=send.at[i * 3 + k - 1], recv_sem=recv.at[i * 3 + k - 1],
                            device_id=(pj >> 1, pj & 1, c), device_id_type=MESH).start()
        token[...] = jnp.zeros_like(token)

    return pl.pallas_call(
        body, name=name,
        out_shape=(pltpu.SemaphoreType.DMA((3 * ng,)), pltpu.SemaphoreType.DMA((3 * ng,)),
                   *[pltpu.HBM(a.shape, a.dtype) for a in arrays], SDS((8, 128), f32)),
        in_specs=[HBM_SPEC] * na + [ANY_SPEC],
        out_specs=(SEM_SPEC, SEM_SPEC, *([HBM_SPEC] * na), pl.BlockSpec(memory_space=pltpu.VMEM)),
        input_output_aliases={i: 2 + i for i in range(na)},
        compiler_params=pltpu.CompilerParams(has_side_effects=EFFECT),
    )(*[_in_hbm(a) for a in arrays], after)


def _xfer_wait(name, started, ng, after, arrival_view):
    send, recv = started[0], started[1]
    arrays = started[2:-1]
    na = len(arrays)

    def body(*refs):
        arr = refs[:na]
        send_sem, recv_sem = refs[na], refs[na + 1]
        x, y, c = lax.axis_index("x"), lax.axis_index("y"), lax.axis_index("c")
        for k in (1, 2, 3):
            for i in range(ng):
                arrival = arrival_view(arr, i)
                cp = pltpu.make_async_remote_copy(
                    src_ref=arrival, dst_ref=arrival, send_sem=send_sem.at[i * 3 + k - 1], recv_sem=recv_sem.at[i * 3 + k - 1],
                    device_id=(x, y, 1 - c), device_id_type=MESH)
                cp.wait_send()
                cp.wait_recv()

    return pl.pallas_call(
        body, name=name,
        out_shape=tuple(pltpu.HBM(a.shape, a.dtype) for a in arrays),
        in_specs=[HBM_SPEC] * na + [SEM_SPEC, SEM_SPEC, ANY_SPEC],
        out_specs=tuple([HBM_SPEC] * na),
        input_output_aliases={i: i for i in range(na)},
        compiler_params=pltpu.CompilerParams(has_side_effects=EFFECT),
    )(*arrays, send, recv, after)


def _gather_group_start(tag, group, shards_f32, chip_arr, after):
    fulls = [None] * len(group)
    by_shape = {}
    for i, q in enumerate(group):
        by_shape.setdefault(_cls(q), []).append(i)
    for (kind, shp), idx in by_shape.items():
        for i, f in zip(idx, _cast_place([shards_f32[i] for i in idx], kind, shp, chip_arr)):
            fulls[i] = f
    view = lambda arr, i, mc, _: _shard_view(arr[i], *_cls(group[i]), mc)
    return _xfer_start("gather_start_" + tag, fulls, len(group), after, view, view)


def _gather_group_wait(tag, group, started, after):
    return _xfer_wait("gather_wait_" + tag, started, len(group), after, lambda arr, i: _shard_view(arr[i], *_cls(group[i]), 0))


def _scatter_group_start(tag, group, pieces, after):
    ng = len(group)
    lands = [lax.empty((N_CHIPS - 1,) + _cls(q)[1], bf16) for q in group]
    return _xfer_start("scatter_start_" + tag, list(pieces) + lands, ng, after,
                       lambda arr, i, mc, pj: _shard_view(arr[i], *_cls(group[i]), pj),
                       lambda arr, i, mc, k: arr[ng + i].at[k - 1])


def _scatter_group_wait(tag, group, started, after):
    ng = len(group)
    out = _xfer_wait("scatter_wait_" + tag, started, ng, after, lambda arr, i: arr[ng + i].at[0])
    return out[:ng], out[ng:]


def _mod_shards(c_all, ada_w, ada_b_sh):
    tn = ADA_SH // 3

    def body(c_ref, w_ref, b_ref, o_ref, cs_ref):
        cv = c_ref[...]
        cs = cv * _sigmoid(cv)
        cs_ref[...] = cs
        o_ref[...] = _dot(cs.astype(bf16), w_ref[...].astype(bf16)) + b_ref[...]

    return pl.pallas_call(
        body, name="mod_shards", grid=(DEPTH, 3),
        out_shape=[SDS((DEPTH, 8, ADA_SH), f32), SDS((8, D), f32)],
        in_specs=[pl.BlockSpec((8, D), lambda l, t: (0, 0)),
                  pl.BlockSpec((None, D, tn), lambda l, t: (l, 0, t)),
                  pl.BlockSpec((None, 1, tn), lambda l, t: (l, 0, t))],
        out_specs=[pl.BlockSpec((None, 8, tn), lambda l, t: (l, 0, t)), pl.BlockSpec((8, D), lambda l, t: (0, 0))],
        compiler_params=_cp("arbitrary", "arbitrary"),
    )(c_all, ada_w, ada_b_sh.reshape(DEPTH, 1, ADA_SH))


def _t5_bucket(dist):
    exact = NUM_BUCKETS // 2
    dd = np.maximum(dist, 1).astype(np.float32)
    large = exact + (np.log(dd / exact) / np.log(MAX_DISTANCE / exact) * (NUM_BUCKETS - exact)).astype(np.int32)
    large = np.minimum(large, NUM_BUCKETS - 1)
    return np.where(dist < exact, dist, large).astype(np.int32)


def _bucket_table():
    i = np.arange(BLK)[:, None]
    j = np.arange(2 * BLK)[None, :]
    rel = i - j + BLK
    return np.stack([_t5_bucket(np.maximum(rel, 0) * d) for d in DILATIONS]).astype(np.int32)


def _band():
    rel = lax.broadcasted_iota(jnp.int32, (BLK, 2 * BLK), 0) - lax.broadcasted_iota(jnp.int32, (BLK, 2 * BLK), 1) + BLK
    return (rel >= 0) & (rel <= BLK)


def _bias_blocks(rel_bias, buckets):
    def body(tab_ref, bk_ref, o_ref):
        h = pl.program_id(0)
        bk = bk_ref[...]
        acc = jnp.zeros((BLK, 2 * BLK), f32)
        for b in range(NUM_BUCKETS):
            acc = jnp.where(bk == b, tab_ref[b, h], acc)
        o_ref[...] = jnp.where(_band(), acc, NEG)

    return pl.pallas_call(
        body, name="bias_blocks", grid=(24,),
        out_shape=SDS((24, BLK, 2 * BLK), f32),
        in_specs=[pl.BlockSpec(memory_space=pltpu.SMEM), pl.BlockSpec((None, BLK, 2 * BLK), lambda h: (h // 8, 0, 0))],
        out_specs=pl.BlockSpec((None, BLK, 2 * BLK), lambda h: (h, 0, 0)),
        compiler_params=_cp("arbitrary"),
    )(rel_bias, buckets)


def _bias_grad(dsaccs, buckets):
    nl = len(dsaccs)

    def body(*refs):
        bk = refs[nl][...]
        tot = refs[0][...]
        for r in refs[1:nl]:
            tot = tot + r[...]
        lane = lax.broadcasted_iota(jnp.int32, (1, 128), 1)
        row = jnp.zeros((1, 128), f32)
        for b in range(NUM_BUCKETS):
            row = jnp.where(lane == b, jnp.sum(jnp.where(bk == b, tot, 0.0)), row)
        refs[nl + 1][...] = row

    return pl.pallas_call(
        body, name="bias_grad", grid=(24,),
        out_shape=SDS((24, 1, 128), f32),
        in_specs=[pl.BlockSpec((None, BLK, 2 * BLK), lambda h: (h, 0, 0))] * nl
                 + [pl.BlockSpec((None, BLK, 2 * BLK), lambda h: (h // 8, 0, 0))],
        out_specs=pl.BlockSpec((None, 1, 128), lambda h: (h, 0, 0)),
        compiler_params=_cp("arbitrary"),
    )(*dsaccs, buckets)


def _ffn_fwd(x, mod9, g3, wg, wu, wd, sub):
    S = x.shape[0]

    def body(x_ref, mod_ref, g_ref, wg_ref, wu_ref, wd_ref, xo_ref, h_ref, a_ref, u_ref, y_ref, acc):
        j = pl.program_id(1)

        @pl.when(j == 0)
        def _():
            h, _, _ = _norm_fwd(x_ref[...], g_ref[sub:sub + 1, :], mod_ref[3 * sub:3 * sub + 1, :], mod_ref[3 * sub + 1:3 * sub + 2, :])
            h_ref[...] = h.astype(bf16)
            acc[...] = jnp.zeros_like(acc)

        h = h_ref[...]
        a = _dot(h, wg_ref[...])
        u = _dot(h, wu_ref[...])
        a_ref[...] = a.astype(bf16)
        u_ref[...] = u.astype(bf16)
        hid = (a * _sigmoid(a) * u).astype(bf16)
        acc[...] += _dot(hid, wd_ref[...])

        @pl.when(j == N_CHIPS - 1)
        def _():
            y = acc[...]
            y_ref[...] = y.astype(bf16)
            xo_ref[...] = x_ref[...] + 0.5 * mod_ref[3 * sub + 2:3 * sub + 3, :] * y

    row = pl.BlockSpec((TM, D), lambda i, j: (i, 0))
    return pl.pallas_call(
        body, name="ffn_fwd", grid=(S // TM, N_CHIPS),
        out_shape=[SDS((S, D), f32), SDS((S, D), bf16), SDS((N_CHIPS, S, FB), bf16), SDS((N_CHIPS, S, FB), bf16), SDS((S, D), bf16)],
        in_specs=[row, pl.BlockSpec((9, D), lambda i, j: (0, 0)), pl.BlockSpec((3, D), lambda i, j: (0, 0)),
                  pl.BlockSpec((None, D, FB), lambda i, j: (j, 0, 0)), pl.BlockSpec((None, D, FB), lambda i, j: (j, 0, 0)),
                  pl.BlockSpec((FB, D), lambda i, j: (j, 0))],
        out_specs=[row, row, pl.BlockSpec((None, TM, FB), lambda i, j: (j, i, 0)), pl.BlockSpec((None, TM, FB), lambda i, j: (j, i, 0)), row],
        scratch_shapes=[pltpu.VMEM((TM, D), f32)],
        compiler_params=_cp("arbitrary", "arbitrary"),
    )(x, mod9, g3, wg, wu, wd)


def _ffn_bwd1(dxo, x, mod9, g3, y, a, u, wg, wu, wd, sub):
    S = x.shape[0]

    def body(dxo_ref, x_ref, mod_ref, g_ref, y_ref, a_ref, u_ref, wg_ref, wu_ref, wd_ref,
             dxi_ref, da_ref, du_ref, hid_ref, dy_ref, sm_ref, acc):
        i, j = pl.program_id(0), pl.program_id(1)
        gate = mod_ref[3 * sub + 2:3 * sub + 3, :]

        @pl.when((i == 0) & (j == 0))
        def _():
            sm_ref[...] = jnp.zeros_like(sm_ref)

        @pl.when(j == 0)
        def _():
            dxo_v = dxo_ref[...]
            dy_ref[...] = (0.5 * gate * dxo_v).astype(bf16)
            sm_ref[2:3, :] += jnp.sum(0.5 * y_ref[...].astype(f32) * dxo_v, axis=0, keepdims=True)
            acc[...] = jnp.zeros_like(acc)

        av, uv = a_ref[...].astype(f32), u_ref[...].astype(f32)
        sg = _sigmoid(av)
        sil = av * sg
        dhid = _dot_nt(dy_ref[...], wd_ref[...])
        da = (dhid * uv * (sg * (1.0 + av * (1.0 - sg)))).astype(bf16)
        du = (dhid * sil).astype(bf16)
        da_ref[...] = da
        du_ref[...] = du
        hid_ref[...] = (sil * uv).astype(bf16)
        acc[...] += _dot_nt(da, wg_ref[...]) + _dot_nt(du, wu_ref[...])

        @pl.when(j == N_CHIPS - 1)
        def _():
            g = g_ref[sub:sub + 1, :]
            scale = mod_ref[3 * sub + 1:3 * sub + 2, :]
            _, xhat, rstd = _norm_fwd(x_ref[...], g, mod_ref[3 * sub:3 * sub + 1, :], scale)
            dx, dshift, dscale, dg = _norm_bwd(acc[...], xhat, rstd, g, scale)
            dxi_ref[...] = dxo_ref[...] + dx
            sm_ref[0:1, :] += dshift
            sm_ref[1:2, :] += dscale
            sm_ref[3:4, :] += dg

    row = pl.BlockSpec((TM, D), lambda i, j: (i, 0))
    hidb = pl.BlockSpec((None, TM, FB), lambda i, j: (j, i, 0))
    wcol = pl.BlockSpec((None, D, FB), lambda i, j: (j, 0, 0))
    return pl.pallas_call(
        body, name="ffn_bwd1", grid=(S // TM, N_CHIPS),
        out_shape=[SDS((S, D), f32), SDS((N_CHIPS, S, FB), bf16), SDS((N_CHIPS, S, FB), bf16), SDS((N_CHIPS, S, FB), bf16),
                   SDS((S, D), bf16), SDS((8, D), f32)],
        in_specs=[row, row, pl.BlockSpec((9, D), lambda i, j: (0, 0)), pl.BlockSpec((3, D), lambda i, j: (0, 0)), row,
                  hidb, hidb, wcol, wcol, pl.BlockSpec((FB, D), lambda i, j: (j, 0))],
        out_specs=[row, hidb, hidb, hidb, row, pl.BlockSpec((8, D), lambda i, j: (0, 0))],
        scratch_shapes=[pltpu.VMEM((TM, D), f32)],
        compiler_params=_cp("arbitrary", "arbitrary"),
    )(dxo, x, mod9, g3, y, a, u, wg, wu, wd)


def _ffn_bwd2(h, da, du, hid, dy):
    S = h.shape[0]
    ni = S // TM

    def body(h_ref, da_ref, du_ref, hid_ref, dy_ref, dwg_ref, dwu_ref, dwd_ref, ag, au, ad):
        i = pl.program_id(1)

        @pl.when(i == 0)
        def _():
            ag[...] = jnp.zeros_like(ag)
            au[...] = jnp.zeros_like(au)
            ad[...] = jnp.zeros_like(ad)

        hv = h_ref[...]
        ag[...] += _dot_tn(hv, da_ref[...])
        au[...] += _dot_tn(hv, du_ref[...])
        ad[...] += _dot_tn(hid_ref[...], dy_ref[...])

        @pl.when(i == ni - 1)
        def _():
            dwg_ref[...] = ag[...].astype(bf16)
            dwu_ref[...] = au[...].astype(bf16)
            dwd_ref[...] = ad[...].astype(bf16)

    row = pl.BlockSpec((TM, D), lambda j, i: (i, 0))
    hidb = pl.BlockSpec((None, TM, FB), lambda j, i: (j, i, 0))
    wcol = pl.BlockSpec((None, D, FB), lambda j, i: (j, 0, 0))
    return pl.pallas_call(
        body, name="ffn_bwd2", grid=(N_CHIPS, ni),
        out_shape=[SDS((N_CHIPS, D, FB), bf16), SDS((N_CHIPS, D, FB), bf16), SDS((N_CHIPS * FB, D), bf16)],
        in_specs=[row, hidb, hidb, hidb, row],
        out_specs=[wcol, wcol, pl.BlockSpec((FB, D), lambda j, i: (j, 0))],
        scratch_shapes=[pltpu.VMEM((D, FB), f32), pltpu.VMEM((D, FB), f32), pltpu.VMEM((FB, D), f32)],
        compiler_params=_cp("arbitrary", "arbitrary"),
    )(h, da, du, hid, dy)


def _mix_qkv(x, mod9, g3, win):
    S = x.shape[0]

    def body(x_ref, mod_ref, g_ref, w_ref, h_ref, o_ref):
        @pl.when(pl.program_id(1) == 0)
        def _():
            h, _, _ = _norm_fwd(x_ref[...], g_ref[1:2, :], mod_ref[3:4, :], mod_ref[4:5, :])
            h_ref[...] = h.astype(bf16)

        o_ref[...] = _dot(h_ref[...], w_ref[...])

    row = pl.BlockSpec((TM, D), lambda i, j: (i, 0))
    return pl.pallas_call(
        body, name="mix_qkv", grid=(S // TM, QKV_W // CB),
        out_shape=[SDS((S, D), bf16), SDS((S, QKV_W), f32)],
        in_specs=[row, pl.BlockSpec((9, D), lambda i, j: (0, 0)), pl.BlockSpec((3, D), lambda i, j: (0, 0)),
                  pl.BlockSpec((D, CB), lambda i, j: (0, j))],
        out_specs=[row, pl.BlockSpec((TM, CB), lambda i, j: (i, j))],
        compiler_params=_cp("arbitrary", "arbitrary"),
    )(x, mod9, g3, win)


def _mix_rest(h, win):
    S = h.shape[0]
    off = QKV_W // CB

    def body(h_ref, w_ref, o_ref):
        o_ref[...] = _dot(h_ref[...], w_ref[...]).astype(bf16)

    return pl.pallas_call(
        body, name="mix_rest", grid=(S // TM, REST_W // CB),
        out_shape=SDS((S, REST_W), bf16),
        in_specs=[pl.BlockSpec((TM, D), lambda i, j: (i, 0)), pl.BlockSpec((D, CB), lambda i, j: (0, off + j))],
        out_specs=pl.BlockSpec((TM, CB), lambda i, j: (i, j)),
        compiler_params=_cp("arbitrary", "arbitrary"),
    )(h, win)


def _attn_fwd(qkv, bias, g):
    S = qkv.shape[0]
    d = DILATIONS[g]
    R = BLK * d
    nb = S // R
    qb, kb, vb = 4 * g, 12 + 4 * g, 24 + 4 * g

    def body(q_ref, kc_ref, kp_ref, vc_ref, vp_ref, b_ref, o_ref, l_ref):
        n = pl.program_id(1)
        col = lax.broadcasted_iota(jnp.int32, (BLK, 2 * BLK), 1)
        first = jnp.where((col < BLK) & (n == 0), NEG, 0.0)

        def step(r, carry):
            sl = pl.ds(r, BLK, stride=d)
            q, kc, kp, vc, vp = q_ref[sl, :], kc_ref[sl, :], kp_ref[sl, :], vc_ref[sl, :], vp_ref[sl, :]
            os, ls = [], []
            for hh in range(2):
                cs = slice(HD * hh, HD * hh + HD)
                qh = q[:, cs].astype(bf16)
                kh = jnp.concatenate([kp[:, cs], kc[:, cs]], axis=0).astype(bf16)
                vh = jnp.concatenate([vp[:, cs], vc[:, cs]], axis=0).astype(bf16)
                s = _dot_nt(qh, kh) * SCALE + b_ref[hh] + first
                m = jnp.max(s, axis=-1, keepdims=True)
                p = jnp.exp(s - m)
                l = jnp.sum(p, axis=-1, keepdims=True)
                os.append(_dot(p.astype(bf16), vh) / l)
                ls.append(jnp.broadcast_to(m + jnp.log(l), (BLK, HD)))
            o_ref[sl, :] = jnp.concatenate(os, axis=1)
            l_ref[sl, :] = jnp.concatenate(ls, axis=1)
            return carry

        lax.fori_loop(0, d, step, 0)

    def blk(cb, prev):
        if prev:
            return pl.BlockSpec((R, 128), lambda hp, n: (jnp.maximum(n - 1, 0), cb + hp))
        return pl.BlockSpec((R, 128), lambda hp, n: (n, cb + hp))

    outb = pl.BlockSpec((R, 128), lambda hp, n: (n, hp))
    return pl.pallas_call(
        body, name=f"attn_fwd_d{d}", grid=(4, nb),
        out_shape=[SDS((S, 512), f32), SDS((S, 512), f32)],
        in_specs=[blk(qb, False), blk(kb, False), blk(kb, True), blk(vb, False), blk(vb, True),
                  pl.BlockSpec((2, BLK, 2 * BLK), lambda hp, n: (4 * g + hp, 0, 0))],
        out_specs=[outb, outb],
        compiler_params=_cp("arbitrary", "arbitrary"),
    )(qkv, qkv, qkv, qkv, qkv, bias)


def _attn_bwd(qkv, do, o, lse, bias, dq_all, dk_all, dv_all, g):
    S = qkv.shape[0]
    d = DILATIONS[g]
    R = BLK * d
    nb = S // R
    qb, kb, vb = 4 * g, 12 + 4 * g, 24 + 4 * g

    def body(q_ref, kc_ref, kp_ref, vc_ref, vp_ref, do_ref, o_ref, l_ref, b_ref, dqi, dki, dvi,
             dq_ref, dk_ref, dv_ref, ds_ref, ck, cv):
        n = pl.program_id(1)
        col = lax.broadcasted_iota(jnp.int32, (BLK, 2 * BLK), 1)
        first = jnp.where((col < BLK) & (n == 0), NEG, 0.0)

        @pl.when(n == 0)
        def _():
            ck[...] = jnp.zeros_like(ck)
            cv[...] = jnp.zeros_like(cv)
            ds_ref[...] = jnp.zeros_like(ds_ref)

        @pl.when(n < nb)
        def _():
            def step(r, carry):
                sl = pl.ds(r, BLK, stride=d)
                q, kc, kp, vc, vp = q_ref[sl, :], kc_ref[sl, :], kp_ref[sl, :], vc_ref[sl, :], vp_ref[sl, :]
                dov, ov, lv = do_ref[sl, :], o_ref[sl, :], l_ref[sl, :]
                dqs, dks, dvs = [], [], []
                for hh in range(2):
                    cs = slice(HD * hh, HD * hh + HD)
                    qh = q[:, cs].astype(bf16)
                    kh = jnp.concatenate([kp[:, cs], kc[:, cs]], axis=0).astype(bf16)
                    vh = jnp.concatenate([vp[:, cs], vc[:, cs]], axis=0).astype(bf16)
                    doh = dov[:, cs]
                    dsum = jnp.sum(doh * ov[:, cs], axis=-1, keepdims=True)
                    s = _dot_nt(qh, kh) * SCALE + b_ref[hh] + first
                    p = jnp.exp(s - lv[:, HD * hh:HD * hh + 1])
                    dohb = doh.astype(bf16)
                    ds = p * (_dot_nt(dohb, vh) - dsum)
                    ds_ref[hh] += ds
                    dsb = ds.astype(bf16)
                    dqs.append(_dot(dsb, kh) * SCALE)
                    dks.append(_dot_tn(dsb, qh) * SCALE)
                    dvs.append(_dot_tn(p.astype(bf16), dohb))
                dq_ref[sl, :] = jnp.concatenate(dqs, axis=1)
                dk = jnp.concatenate(dks, axis=1)
                dv = jnp.concatenate(dvs, axis=1)
                dk_ref[sl, :] = ck[r] + dk[:BLK]
                dv_ref[sl, :] = cv[r] + dv[:BLK]
                ck[r] = dk[BLK:]
                cv[r] = dv[BLK:]
                return carry

            lax.fori_loop(0, d, step, 0)

        @pl.when(n == nb)
        def _():
            def flush(r, carry):
                sl = pl.ds(r, BLK, stride=d)
                dk_ref[sl, :] = ck[r]
                dv_ref[sl, :] = cv[r]
                return carry

            lax.fori_loop(0, d, flush, 0)

    last = nb - 1

    def blk(cb, prev):
        if prev:
            return pl.BlockSpec((R, 128), lambda hp, n: (jnp.maximum(jnp.minimum(n, last) - 1, 0), cb + hp))
        return pl.BlockSpec((R, 128), lambda hp, n: (jnp.minimum(n, last), cb + hp))

    cur = pl.BlockSpec((R, 128), lambda hp, n: (jnp.minimum(n, last), hp))
    anyspec = pl.BlockSpec(memory_space=pl.ANY)
    dqo = pl.BlockSpec((R, 128), lambda hp, n: (jnp.minimum(n, last), 4 * g + hp))
    dko = pl.BlockSpec((R, 128), lambda hp, n: (jnp.maximum(n - 1, 0), 4 * g + hp))
    return pl.pallas_call(
        body, name=f"attn_bwd_d{d}", grid=(4, nb + 1),
        out_shape=[SDS((S, 1536), f32), SDS((S, 1536), f32), SDS((S, 1536), f32), SDS((8, BLK, 2 * BLK), f32)],
        in_specs=[blk(qb, False), blk(kb, False), blk(kb, True), blk(vb, False), blk(vb, True), cur, cur, cur,
                  pl.BlockSpec((2, BLK, 2 * BLK), lambda hp, n: (4 * g + hp, 0, 0)), anyspec, anyspec, anyspec],
        out_specs=[dqo, dko, dko, pl.BlockSpec((2, BLK, 2 * BLK), lambda hp, n: (hp, 0, 0))],
        scratch_shapes=[pltpu.VMEM((d, BLK, 128), f32), pltpu.VMEM((d, BLK, 128), f32)],
        input_output_aliases={9: 0, 10: 1, 11: 2},
        compiler_params=_cp("arbitrary", "arbitrary"),
    )(qkv, qkv, qkv, qkv, qkv, do, o, lse, bias, dq_all, dk_all, dv_all)


def _conv_z(cc, ch, hc, hh, cw_ref, first):
    halo = jnp.where(first, 0.0, hc.astype(f32) * hh.astype(f32))
    T = jnp.concatenate([halo, cc * ch], axis=0)
    z = cw_ref[2:3, :] * T + cw_ref[1:2, :] * pltpu.roll(T, 1, 0) + cw_ref[0:1, :] * pltpu.roll(T, 2, 0)
    return T, z[HALO:]


def _rest_specs(tm, with_next):
    per = tm // HALO
    specs = [pl.BlockSpec((tm, D), functools.partial(lambda i, k: (i, k), k=k)) for k in range(5)]
    specs += [pl.BlockSpec((HALO, D), functools.partial(lambda i, k: (jnp.maximum(i * per - 1, 0), k), k=k)) for k in (1, 2)]
    return specs


def _mix_out_fwd(x, mod9, rest, ogs, lgs, cw, wco, wao, wo):
    S = x.shape[0]
    tm = TMX

    def body(x_ref, mod_ref, cb_ref, cc_ref, ch_ref, gc_ref, ga_ref, hc_ref, hh_ref,
             o0, o1, o2, l0, l1, l2, cw_ref, wco_ref, wao_ref, wo_ref,
             xo_ref, o_ref, lse_ref, yc_ref, ya_ref, out_ref):
        i = pl.program_id(0)
        lv = [l0[...], l1[...], l2[...]]
        mx = jnp.maximum(jnp.maximum(lv[0], lv[1]), lv[2])
        es = [jnp.exp(l - mx) for l in lv]
        den = es[0] + es[1] + es[2]
        o = (es[0] / den) * o0[...] + (es[1] / den) * o1[...] + (es[2] / den) * o2[...]
        o_ref[...] = o
        lse_ref[...] = mx + jnp.log(den)
        _, z = _conv_z(cc_ref[...].astype(f32), ch_ref[...].astype(f32), hc_ref[...], hh_ref[...], cw_ref, i == 0)
        p = (cb_ref[...].astype(f32) * z).astype(bf16)
        yc = _dot(p, wco_ref[...])
        ya = _dot(o.astype(bf16), wao_ref[...])
        yc_ref[...] = yc.astype(bf16)
        ya_ref[...] = ya.astype(bf16)
        merged = _sigmoid(gc_ref[...].astype(f32)) * yc + _sigmoid(ga_ref[...].astype(f32)) * ya
        out = _dot(merged.astype(bf16), wo_ref[...])
        out_ref[...] = out.astype(bf16)
        xo_ref[...] = x_ref[...] + mod_ref[5:6, :] * out

    row = pl.BlockSpec((tm, D), lambda i: (i, 0))
    att = pl.BlockSpec((tm, 512), lambda i: (i, 0))
    full = lambda shp: pl.BlockSpec(shp, lambda i: (0, 0))
    return pl.pallas_call(
        body, name="mix_out_fwd", grid=(S // tm,),
        out_shape=[SDS((S, D), f32), SDS((S, 512), f32), SDS((S, 512), f32), SDS((S, D), bf16), SDS((S, D), bf16), SDS((S, D), bf16)],
        in_specs=[row, full((9, D))] + _rest_specs(tm, False) + [att] * 6 + [full((3, D)), full((D, D)), full((512, D)), full((D, D))],
        out_specs=[row, att, att, row, row, row],
        compiler_params=_cp("arbitrary"),
    )(x, mod9, *([rest] * 7), *ogs, *lgs, cw, wco, wao, wo)


def _mix_out_bwd(dxo, mod9, outv, yc, ya, rest, o, cw, wco, wao, wo):
    S = dxo.shape[0]
    tm = TMX
    ni = S // tm

    def body(dxo_ref, mod_ref, out_ref, yc_ref, ya_ref, cb_ref, cc_ref, ch_ref, gc_ref, ga_ref, hc_ref, hh_ref,
             o_ref, cw_ref, wco_ref, wao_ref, wo_ref,
             dp_ref, dg2_ref, do_ref, dwco_ref, dwao_ref, dwo_ref, sm_ref, aco, aao, ao):
        i = pl.program_id(0)

        @pl.when(i == 0)
        def _():
            sm_ref[...] = jnp.zeros_like(sm_ref)
            aco[...] = jnp.zeros_like(aco)
            aao[...] = jnp.zeros_like(aao)
            ao[...] = jnp.zeros_like(ao)

        dxo_v = dxo_ref[...]
        sm_ref[2:3, :] += jnp.sum(out_ref[...].astype(f32) * dxo_v, axis=0, keepdims=True)
        dout = (mod_ref[5:6, :] * dxo_v).astype(bf16)
        dmerged = _dot_nt(dout, wo_ref[...])
        sc, sa = _sigmoid(gc_ref[...].astype(f32)), _sigmoid(ga_ref[...].astype(f32))
        ycv, yav = yc_ref[...].astype(f32), ya_ref[...].astype(f32)
        ao[...] += _dot_tn((sc * ycv + sa * yav).astype(bf16), dout)
        dyc = (dmerged * sc).astype(bf16)
        dya = (dmerged * sa).astype(bf16)
        dg2_ref[:, :D] = (dmerged * ycv * sc * (1.0 - sc)).astype(bf16)
        dg2_ref[:, D:] = (dmerged * yav * sa * (1.0 - sa)).astype(bf16)
        dp_ref[...] = _dot_nt(dyc, wco_ref[...]).astype(bf16)
        _, z = _conv_z(cc_ref[...].astype(f32), ch_ref[...].astype(f32), hc_ref[...], hh_ref[...], cw_ref, i == 0)
        aco[...] += _dot_tn((cb_ref[...].astype(f32) * z).astype(bf16), dyc)
        do_ref[...] = _dot_nt(dya, wao_ref[...])
        aao[...] += _dot_tn(o_ref[...].astype(bf16), dya)

        @pl.when(i == ni - 1)
        def _():
            dwco_ref[...] = aco[...].astype(bf16)
            dwao_ref[...] = aao[...].astype(bf16)
            dwo_ref[...] = ao[...].astype(bf16)

    row = pl.BlockSpec((tm, D), lambda i: (i, 0))
    att = pl.BlockSpec((tm, 512), lambda i: (i, 0))
    full = lambda shp: pl.BlockSpec(shp, lambda i: (0, 0))
    return pl.pallas_call(
        body, name="mix_out_bwd", grid=(ni,),
        out_shape=[SDS((S, D), bf16), SDS((S, 2 * D), bf16), SDS((S, 512), f32),
                   SDS((D, D), bf16), SDS((512, D), bf16), SDS((D, D), bf16), SDS((8, D), f32)],
        in_specs=[row, full((9, D)), row, row, row] + _rest_specs(tm, False) + [att, full((3, D)), full((D, D)), full((512, D)), full((D, D))],
        out_specs=[row, pl.BlockSpec((tm, 2 * D), lambda i: (i, 0)), att, full((D, D)), full((512, D)), full((D, D)), full((8, D))],
        scratch_shapes=[pltpu.VMEM((D, D), f32), pltpu.VMEM((512, D), f32), pltpu.VMEM((D, D), f32)],
        compiler_params=_cp("arbitrary"),
    )(dxo, mod9, outv, yc, ya, *([rest] * 7), o, cw, wco, wao, wo)


def _conv_bwd(dp, rest, cw):
    S = dp.shape[0]
    tm = TM
    per = tm // HALO
    nh = S // HALO
    ni = S // tm

    def body(dp_ref, dpn_ref, cb_ref, cbn_ref, cc_ref, ch_ref, hc_ref, hh_ref, cw_ref, d3_ref, sm_ref):
        i = pl.program_id(0)

        @pl.when(i == 0)
        def _():
            sm_ref[...] = jnp.zeros_like(sm_ref)

        cc, ch = cc_ref[...].astype(f32), ch_ref[...].astype(f32)
        T, z = _conv_z(cc, ch, hc_ref[...], hh_ref[...], cw_ref, i == 0)
        dpv = dp_ref[...].astype(f32)
        cbv = cb_ref[...].astype(f32)
        dz = dpv * cbv
        dzn = jnp.where(i == ni - 1, 0.0, dpn_ref[...].astype(f32) * cbn_ref[...].astype(f32))
        E = jnp.concatenate([dz, dzn], axis=0)
        ne = tm + HALO
        dT = cw_ref[2:3, :] * E + cw_ref[1:2, :] * pltpu.roll(E, ne - 1, 0) + cw_ref[0:1, :] * pltpu.roll(E, ne - 2, 0)
        dT = dT[:tm]
        d3_ref[:, :D] = (dpv * z).astype(bf16)
        d3_ref[:, D:2 * D] = (dT * ch).astype(bf16)
        d3_ref[:, 2 * D:] = (dT * cc).astype(bf16)
        sm_ref[2:3, :] += jnp.sum(dz * T[HALO:], axis=0, keepdims=True)
        sm_ref[1:2, :] += jnp.sum(dz * pltpu.roll(T, 1, 0)[HALO:], axis=0, keepdims=True)
        sm_ref[0:1, :] += jnp.sum(dz * pltpu.roll(T, 2, 0)[HALO:], axis=0, keepdims=True)

    row = pl.BlockSpec((tm, D), lambda i: (i, 0))
    nxt = pl.BlockSpec((HALO, D), lambda i: (jnp.minimum((i + 1) * per, nh - 1), 0))
    col = lambda k: pl.BlockSpec((tm, D), lambda i: (i, k))
    prv = lambda k: pl.BlockSpec((HALO, D), lambda i: (jnp.maximum(i * per - 1, 0), k))
    return pl.pallas_call(
        body, name="conv_bwd", grid=(ni,),
        out_shape=[SDS((S, 3 * D), bf16), SDS((8, D), f32)],
        in_specs=[row, nxt, col(0), nxt, col(1), col(2), prv(1), prv(2), pl.BlockSpec((3, D), lambda i: (0, 0))],
        out_specs=[pl.BlockSpec((tm, 3 * D), lambda i: (i, 0)), pl.BlockSpec((8, D), lambda i: (0, 0))],
        compiler_params=_cp("arbitrary"),
    )(dp, dp, rest, rest, rest, rest, rest, rest, cw)


_DU_RANGES = ((0, 3), (3, 6), (6, 9), (9, 15), (15, 19))
N_CBLK = IN_W // CB


def _mix_in_bwd_dh(dxo, x, mod9, g3, dus, win):
    S = x.shape[0]

    def body(dxo_ref, x_ref, mod_ref, g_ref, s0, s1, s2, s3, s4, w_ref, dxi_ref, sm_ref, acc):
        i, kb = pl.program_id(0), pl.program_id(1)

        @pl.when((i == 0) & (kb == 0))
        def _():
            sm_ref[...] = jnp.zeros_like(sm_ref)

        @pl.when(kb == 0)
        def _():
            acc[...] = jnp.zeros_like(acc)

        for src, (lo, hi) in zip((s0, s1, s2, s3, s4), _DU_RANGES):
            @pl.when((kb >= lo) & (kb < hi))
            def _(src=src):
                acc[...] += _dot_nt(src[...].astype(bf16), w_ref[...])

        @pl.when(kb == N_CBLK - 1)
        def _():
            g, scale = g_ref[1:2, :], mod_ref[4:5, :]
            _, xhat, rstd = _norm_fwd(x_ref[...], g, mod_ref[3:4, :], scale)
            dx, dshift, dscale, dg = _norm_bwd(acc[...], xhat, rstd, g, scale)
            dxi_ref[...] = dxo_ref[...] + dx
            sm_ref[0:1, :] += dshift
            sm_ref[1:2, :] += dscale
            sm_ref[3:4, :] += dg

    row = pl.BlockSpec((TM, D), lambda i, kb: (i, 0))

    def src_spec(lo, hi):
        return pl.BlockSpec((TM, CB), lambda i, kb: (i, jnp.clip(kb - lo, 0, hi - lo - 1)))

    return pl.pallas_call(
        body, name="mix_in_bwd_dh", grid=(S // TM, N_CBLK),
        out_shape=[SDS((S, D), f32), SDS((8, D), f32)],
        in_specs=[row, row, pl.BlockSpec((9, D), lambda i, kb: (0, 0)), pl.BlockSpec((3, D), lambda i, kb: (0, 0))]
                 + [src_spec(lo, hi) for lo, hi in _DU_RANGES] + [pl.BlockSpec((D, CB), lambda i, kb: (0, kb))],
        out_specs=[row, pl.BlockSpec((8, D), lambda i, kb: (0, 0))],
        scratch_shapes=[pltpu.VMEM((TM, D), f32)],
        compiler_params=_cp("arbitrary", "arbitrary"),
    )(dxo, x, mod9, g3, *dus, win)


def _mix_in_bwd_dw(h, dus):
    S = h.shape[0]
    ni = S // TM

    def body(h_ref, s0, s1, s2, s3, s4, dw_ref, acc):
        kb, i = pl.program_id(0), pl.program_id(1)

        @pl.when(i == 0)
        def _():
            acc[...] = jnp.zeros_like(acc)

        for src, (lo, hi) in zip((s0, s1, s2, s3, s4), _DU_RANGES):
            @pl.when((kb >= lo) & (kb < hi))
            def _(src=src):
                acc[...] += _dot_tn(h_ref[...], src[...].astype(bf16))

        @pl.when(i == ni - 1)
        def _():
            dw_ref[...] = acc[...].astype(bf16)

    def src_spec(lo, hi):
        def imap(kb, i):
            on = (kb >= lo) & (kb < hi)
            return (jnp.where(on, i, 0), jnp.clip(kb - lo, 0, hi - lo - 1))
        return pl.BlockSpec((TM, CB), imap)

    return pl.pallas_call(
        body, name="mix_in_bwd_dw", grid=(N_CBLK, ni),
        out_shape=SDS((D, IN_W), bf16),
        in_specs=[pl.BlockSpec((TM, D), lambda kb, i: (i, 0))] + [src_spec(lo, hi) for lo, hi in _DU_RANGES],
        out_specs=pl.BlockSpec((D, CB), lambda kb, i: (0, kb)),
        scratch_shapes=[pltpu.VMEM((D, CB), f32)],
        compiler_params=_cp("arbitrary", "arbitrary"),
    )(h, *dus)


def _loss_head(x, fg, tgt):
    S = x.shape[0]

    def body(x_ref, g_ref, t_ref, ls_ref, dx_ref, sm_ref):
        i = pl.program_id(0)

        @pl.when(i == 0)
        def _():
            ls_ref[...] = jnp.zeros_like(ls_ref)
            sm_ref[...] = jnp.zeros_like(sm_ref)

        xv, g = x_ref[...], g_ref[...]
        rstd = lax.rsqrt(jnp.mean(xv * xv, axis=-1, keepdims=True) + EPS)
        xhat = xv * rstd
        e = xhat * g - t_ref[...]
        ls_ref[...] += 0.5 * jnp.sum(jnp.mean(e * e, axis=-1, keepdims=True))
        dy = e * (1.0 / D)
        sm_ref[0:1, :] += jnp.sum(dy * xhat, axis=0, keepdims=True)
        dxh = dy * g
        dx_ref[...] = rstd * (dxh - xhat * jnp.mean(dxh * xhat, axis=-1, keepdims=True))

    row = pl.BlockSpec((TM, D), lambda i: (i, 0))
    return pl.pallas_call(
        body, name="loss_head", grid=(S // TM,),
        out_shape=[SDS((8, 128), f32), SDS((S, D), f32), SDS((8, D), f32)],
        in_specs=[row, pl.BlockSpec((1, D), lambda i: (0, 0)), row],
        out_specs=[pl.BlockSpec((8, 128), lambda i: (0, 0)), row, pl.BlockSpec((8, D), lambda i: (0, 0))],
        compiler_params=_cp("arbitrary"),
    )(x, fg, tgt)


def _adam(w, g, m, v):
    m2 = B1 * m + (1.0 - B1) * g
    v2 = B2 * v + (1.0 - B2) * (g * g)
    delta = -LR * ((m2 / BC1) / (jnp.sqrt(v2 / BC2) + AEPS) + WD * w)
    return delta, m2, v2


def _row_tile(rows, cols):
    for tr in (512, 352, 256, 128, 64):
        if rows % tr == 0 and tr * cols * 4 <= (5 << 18):
            return tr
    raise ValueError((rows, cols))


def _sum_slots(land):
    _, R, C = land.shape
    tr = _row_tile(R, C)

    def body(l_ref, t_ref):
        t = l_ref[0].astype(f32)
        for k in range(1, N_CHIPS):
            t = t + l_ref[k].astype(f32)
        t_ref[...] = t

    return pl.pallas_call(
        body, name="sum_slots", grid=(R // tr,),
        out_shape=SDS((R, C), f32),
        in_specs=[pl.BlockSpec((N_CHIPS, tr, C), lambda i: (0, i, 0))],
        out_specs=pl.BlockSpec((tr, C), lambda i: (i, 0)),
        compiler_params=_cp("arbitrary"),
    )(land)


def _adamw_pair(w2, m2, v2, ta, tb, outs, slot):
    R, C = ta.shape
    tr = _row_tile(R, C)
    nrt = R // tr

    def body(w_ref, m_ref, v_ref, ta_ref, tb_ref, g_in, d_in, m_in, v_in, g_ref, d_ref, mo_ref, vo_ref):
        g = ta_ref[...] + tb_ref[...]
        delta, mn, vn = _adam(w_ref[...], g, m_ref[...], v_ref[...])
        g_ref[...] = g
        d_ref[...] = delta
        mo_ref[...] = mn
        vo_ref[...] = vn

    big = pl.BlockSpec((tr, C), lambda i: (slot * nrt + i, 0))
    loc = pl.BlockSpec((tr, C), lambda i: (i, 0))
    anyspec = pl.BlockSpec(memory_space=pl.ANY)
    return pl.pallas_call(
        body, name="adamw_pair", grid=(nrt,),
        out_shape=[SDS(o.shape, f32) for o in outs],
        in_specs=[big, big, big, loc, loc] + [anyspec] * 4,
        out_specs=[big] * 4,
        input_output_aliases={5: 0, 6: 1, 7: 2, 8: 3},
        compiler_params=_cp("arbitrary"),
    )(w2, m2, v2, ta, tb, *outs)


def _adamw_small(w, g, m, v):
    def body(w_ref, g_ref, m_ref, v_ref, d_ref, mo_ref, vo_ref):
        delta, mn, vn = _adam(w_ref[...], g_ref[...], m_ref[...], v_ref[...])
        d_ref[...] = delta
        mo_ref[...] = mn
        vo_ref[...] = vn

    return pl.pallas_call(body, name="adamw_small", out_shape=[SDS(w.shape, f32)] * 3)(w, g, m, v)


def _ada_w_update(cs_all, dmod_sh, w, m, v):
    tr = 256

    def body(cs_ref, dm_ref, w_ref, m_ref, v_ref, g_ref, d_ref, mo_ref, vo_ref):
        g = _dot_tn(cs_ref[...].astype(bf16), dm_ref[...].astype(bf16))
        delta, mn, vn = _adam(w_ref[...], g, m_ref[...], v_ref[...])
        g_ref[...] = g
        d_ref[...] = delta
        mo_ref[...] = mn
        vo_ref[...] = vn

    blk = pl.BlockSpec((None, tr, ADA_SH), lambda l, i: (l, i, 0))
    return pl.pallas_call(
        body, name="ada_w_update", grid=(DEPTH, D // tr),
        out_shape=[SDS(w.shape, f32)] * 4,
        in_specs=[pl.BlockSpec((8, tr), lambda l, i: (0, i)), pl.BlockSpec((None, 8, ADA_SH), lambda l, i: (l, 0, 0)), blk, blk, blk],
        out_specs=[blk] * 4,
        compiler_params=_cp("arbitrary", "arbitrary"),
    )(cs_all, dmod_sh, w, m, v)


def _sum_devices(gathered):
    _, R, C = gathered.shape

    def body(g_ref, o_ref):
        t = g_ref[0]
        for k in range(1, 8):
            t = t + g_ref[k]
        o_ref[...] = t

    return pl.pallas_call(body, name="sum_devices", out_shape=SDS((R, C), f32))(gathered)


def _layer_fwd(x, mod9, g3, cw, getw, bias):
    W = dict(getw("A", x))
    x1, h1, a1, u1, y1 = _ffn_fwd(x, mod9, g3, W["wg0"], W["wu0"], W["wd0"], 0)
    W.update(getw("B", x1))
    hm, qkv = _mix_qkv(x1, mod9, g3, W["win"])
    rest = _mix_rest(hm, W["win"])
    ogs, lgs = [], []
    for g in range(3):
        og, lg = _attn_fwd(qkv, bias, g)
        ogs.append(og)
        lgs.append(lg)
    W.update(getw("C", ogs[2]))
    x2, o, lse, yc, ya, outv = _mix_out_fwd(x1, mod9, rest, ogs, lgs, cw, W["wco"], W["wao"], W["wo"])
    W.update(getw("D", x2))
    x3, h3, a3, u3, y3 = _ffn_fwd(x2, mod9, g3, W["wg1"], W["wu1"], W["wd1"], 2)
    saved = dict(x0=x, x1=x1, x2=x2, h1=h1, a1=a1, u1=u1, y1=y1, hm=hm, qkv=qkv, rest=rest, o=o, lse=lse, yc=yc, ya=ya,
                 outv=outv, h3=h3, a3=a3, u3=u3, y3=y3)
    return x3, saved, W


def _layer_bwd(dx, sv, mod9, g3, cw, W, bias, emit):
    S = dx.shape[0]
    dw = {}
    dx2, da, du, hid, dy, sm3 = _ffn_bwd1(dx, sv["x2"], mod9, g3, sv["y3"], sv["a3"], sv["u3"], W["wg1"], W["wu1"], W["wd1"], 2)
    dw["wg1"], dw["wu1"], dw["wd1"] = _ffn_bwd2(sv["h3"], da, du, hid, dy)
    emit("D", dw)
    dp, dg2, do, dw["wco"], dw["wao"], dw["wo"], smo = _mix_out_bwd(
        dx2, mod9, sv["outv"], sv["yc"], sv["ya"], sv["rest"], sv["o"], cw, W["wco"], W["wao"], W["wo"])
    emit("C", dw)
    d3, smc = _conv_bwd(dp, sv["rest"], cw)
    dq = lax.empty((S, 1536), f32)
    dk = lax.empty((S, 1536), f32)
    dv = lax.empty((S, 1536), f32)
    dsaccs = []
    for g in range(3):
        dq, dk, dv, dsg = _attn_bwd(sv["qkv"], do, sv["o"], sv["lse"], bias, dq, dk, dv, g)
        dsaccs.append(dsg)
    dus = (dq, dk, dv, d3, dg2)
    dx1, smm = _mix_in_bwd_dh(dx2, sv["x1"], mod9, g3, dus, W["win"])
    dw["win"] = _mix_in_bwd_dw(sv["hm"], dus)
    emit("B", dw)
    dx0, da, du, hid, dy, sm1 = _ffn_bwd1(dx1, sv["x0"], mod9, g3, sv["y1"], sv["a1"], sv["u1"], W["wg0"], W["wu0"], W["wd0"], 0)
    dw["wg0"], dw["wu0"], dw["wd0"] = _ffn_bwd2(sv["h1"], da, du, hid, dy)
    emit("A", dw)
    dmod = jnp.concatenate([sm1[0:3], smm[0:2], smo[2:3], sm3[0:3]], axis=0)
    dng = jnp.concatenate([sm1[3:4], smm[3:4], sm3[3:4]], axis=0)
    return dx0, dmod, dng, smc[0:3], jnp.concatenate(dsaccs, axis=0)


def _chip_cols(a, chip, width):
    return lax.dynamic_slice_in_dim(a, chip * width, width, axis=a.ndim - 1)


def kernel(x, c, ada_w, ada_b, norm_g, ffn_w_gate, ffn_w_up, ffn_w_down, w_in, conv_w, w_conv_out, w_attn_out, w_o, rel_bias, final_g, loss_target, m_ada_w, m_ada_b, m_norm_g, m_ffn_w_gate, m_ffn_w_up, m_ffn_w_down, m_w_in, m_conv_w, m_w_conv_out, m_w_attn_out, m_w_o, m_rel_bias, m_final_g, v_ada_w, v_ada_b, v_norm_g, v_ffn_w_gate, v_ffn_w_up, v_ffn_w_down, v_w_in, v_conv_w, v_w_conv_out, v_w_attn_out, v_w_o, v_rel_bias, v_final_g):
    ix, iy, ic = lax.axis_index("x"), lax.axis_index("y"), lax.axis_index("c")
    chip = 2 * ix + iy
    dev = 4 * ix + 2 * iy + ic
    xs = x[0]
    S = xs.shape[0]
    qd = D // N_CHIPS

    pad8 = lambda a: jnp.pad(a, ((0, -a.shape[0] % 8), (0, 0)))
    pack = jnp.concatenate([pad8(c), pad8(norm_g.reshape(3, D)), pad8(conv_w.reshape(3, D))], axis=0)
    g1 = _allgather_small(pack).reshape(8, 24, D)
    c_all = g1[:, 0]
    by_chip = g1[0::2]
    ng_full = jnp.concatenate([by_chip[j, 8:11].reshape(DEPTH, 3, qd) for j in range(N_CHIPS)], axis=-1)
    cw_full = jnp.concatenate([by_chip[j, 16:19].reshape(DEPTH, 3, qd) for j in range(N_CHIPS)], axis=-1)
    mod_sh, cs_all = _mod_shards(c_all, ada_w, _chip_cols(ada_b, chip, ADA_SH))
    g2 = _allgather_small(mod_sh.reshape(DEPTH * 8, ADA_SH)).reshape(8, DEPTH, 8, ADA_SH)
    mine = lax.dynamic_index_in_dim(g2[0::2], dev, axis=2, keepdims=False)
    mod = jnp.transpose(mine, (1, 0, 2)).reshape(DEPTH, 9, D)

    buckets = jnp.asarray(_bucket_table())
    bias = _bias_blocks(rel_bias, buckets)

    chip_arr = jnp.reshape(chip, (1,)).astype(jnp.int32)
    names = [w[0] for w in WCLASSES]

    def layer_shards(l):
        return [ffn_w_gate[l, 0], ffn_w_up[l, 0], ffn_w_down[l, 0], ffn_w_gate[l, 1], ffn_w_up[l, 1], ffn_w_down[l, 1],
                w_in[l], w_conv_out[l], w_attn_out[l], w_o[l]]

    def start_gathers(l, after):
        shards, sts, tok = layer_shards(l), {}, after
        for gname, group in GROUPS.items():
            sts[gname] = _gather_group_start(f"l{l}{gname}", group, [shards[q] for q in group], chip_arr, tok)
            tok = sts[gname][-1]
        return sts, tok

    def wait_gather(l, gname, started, after):
        full = _gather_group_wait(f"l{l}{gname}", GROUPS[gname], started, after)
        return {names[q]: f for q, f in zip(GROUPS[gname], full)}

    Ws, saves, mods = [], [], []
    xc = xs
    sts, _ = start_gathers(0, mod)
    for l in range(DEPTH):
        first = wait_gather(l, "A", sts["A"], xc)
        modl = mod[l]
        cur = sts
        if l + 1 < DEPTH:
            sts, tok = start_gathers(l + 1, first["wg0"])
            modl = modl + tok[0, 0]

        def getw(gname, after, l=l, cur=cur, first=first):
            return first if gname == "A" else wait_gather(l, gname, cur[gname], after)

        xc, sv, W = _layer_fwd(xc, modl, ng_full[l], cw_full[l], getw, bias)
        Ws.append(W)
        mods.append(modl)
        saves.append(sv)

    ls, dx, smf = _loss_head(xc, final_g.reshape(1, D), loss_target[0])
    loss = lax.psum(ls[0, 0], ("x", "y", "c"))

    params = dict(wg=ffn_w_gate, wu=ffn_w_up, wd=ffn_w_down, win=w_in, wco=w_conv_out, wao=w_attn_out, wo=w_o)
    moms = dict(wg=m_ffn_w_gate, wu=m_ffn_w_up, wd=m_ffn_w_down, win=m_w_in, wco=m_w_conv_out, wao=m_w_attn_out, wo=m_w_o)
    vars_ = dict(wg=v_ffn_w_gate, wu=v_ffn_w_up, wd=v_ffn_w_down, win=v_w_in, wco=v_w_conv_out, wao=v_w_attn_out, wo=v_w_o)
    flat = lambda a: a.reshape(-1, a.shape[-1])
    big_out = {k: [lax.empty(flat(p).shape, f32) for _ in range(4)] for k, p in params.items()}
    dmods, dngs, dcws, dsaccs = [None] * DEPTH, [None] * DEPTH, [None] * DEPTH, [None] * DEPTH

    def finish(l, gname, started, after):
        group = GROUPS[gname]
        pieces, lands = _scatter_group_wait(f"l{l}{gname}", group, started, after)
        ts = [_sum_own_slots(pieces[i], lands[i], *_cls(q), chip_arr) for i, q in enumerate(group)]
        tsib = _swap_sibling(ts)
        for i, q in enumerate(group):
            name = names[q]
            key = name.rstrip("01")
            slot = 2 * l + int(name[-1]) if name[-1] in "01" else l
            big_out[key] = _adamw_pair(flat(params[key]), flat(moms[key]), flat(vars_[key]), ts[i], tsib[i], big_out[key], slot)

    pending, tok = [], None
    for l in reversed(range(DEPTH)):
        modl = mods[l] if tok is None else mods[l] + tok[0, 0]
        mine = []

        def emit(gname, dw, l=l, mine=mine):
            prev = mine[-1][2][-1] if mine else dx
            mine.append((l, gname, _scatter_group_start(f"l{l}{gname}", GROUPS[gname], [dw[names[q]] for q in GROUPS[gname]], prev)))

        dx, dmods[l], dngs[l], dcws[l], dsaccs[l] = _layer_bwd(dx, saves[l], modl, ng_full[l], cw_full[l], Ws[l], bias, emit)
        for pl_, pg, pst in pending:
            finish(pl_, pg, pst, dx)
        pending, tok = mine, mine[-1][2][-1]
    for pl_, pg, pst in pending:
        finish(pl_, pg, pst, dx)

    drb = jnp.transpose(_bias_grad(dsaccs, buckets)[:, 0, :NUM_BUCKETS])
    drb_row = jnp.pad(drb.reshape(1, NUM_BUCKETS * 24), ((0, 0), (0, D - NUM_BUCKETS * 24)))
    pack2 = jnp.concatenate([pad8(a) for a in dmods] + [pad8(a) for a in dngs] + [pad8(a) for a in dcws] + [smf, pad8(drb_row)], axis=0)
    n_rows = pack2.shape[0]
    g3 = _allgather_small(pack2).reshape(8, n_rows, D)
    tot = _sum_devices(g3)
    o_ng, o_cw, o_fg, o_rb = 16 * DEPTH, 24 * DEPTH, 32 * DEPTH, 32 * DEPTH + 8
    g_ada_b = jnp.stack([tot[16 * l:16 * l + 9] for l in range(DEPTH)]).reshape(DEPTH, 9 * D)
    g_norm_g = _chip_cols(jnp.stack([tot[o_ng + 8 * l:o_ng + 8 * l + 3] for l in range(DEPTH)]), chip, qd)
    g_conv_w = _chip_cols(jnp.stack([tot[o_cw + 8 * l:o_cw + 8 * l + 3] for l in range(DEPTH)]), chip, qd)
    g_final_g = tot[o_fg]
    g_rel_bias = tot[o_rb, :NUM_BUCKETS * 24].reshape(NUM_BUCKETS, 24)
    dmod_all = jnp.stack([g3[:, 16 * l:16 * l + 9].reshape(8, 9 * D) for l in range(DEPTH)])
    dmod_sh = _chip_cols(dmod_all, chip, ADA_SH)
    g_ada_w, d_ada_w, nm_ada_w, nv_ada_w = _ada_w_update(cs_all, dmod_sh, ada_w, m_ada_w, v_ada_w)

    def small(w, g, m, v):
        shp = w.shape
        to2 = lambda a: a.reshape(-1, shp[-1])
        return [o.reshape(shp) for o in _adamw_small(to2(w), to2(g), to2(m), to2(v))]

    d_ada_b, nm_ada_b, nv_ada_b = small(ada_b, g_ada_b, m_ada_b, v_ada_b)
    d_norm_g, nm_norm_g, nv_norm_g = small(norm_g, g_norm_g, m_norm_g, v_norm_g)
    d_conv_w, nm_conv_w, nv_conv_w = small(conv_w, g_conv_w, m_conv_w, v_conv_w)
    d_rel_bias, nm_rel_bias, nv_rel_bias = small(rel_bias, g_rel_bias, m_rel_bias, v_rel_bias)
    d_final_g, nm_final_g, nv_final_g = small(final_g, g_final_g, m_final_g, v_final_g)

    def big(key, which):
        return big_out[key][which].reshape(params[key].shape)

    grads = [g_ada_w, g_ada_b, g_norm_g, big("wg", 0), big("wu", 0), big("wd", 0), big("win", 0), g_conv_w, big("wco", 0),
             big("wao", 0), big("wo", 0), g_rel_bias, g_final_g]
    deltas = [d_ada_w, d_ada_b, d_norm_g, big("wg", 1), big("wu", 1), big("wd", 1), big("win", 1), d_conv_w, big("wco", 1),
              big("wao", 1), big("wo", 1), d_rel_bias, d_final_g]
    new_m = [nm_ada_w, nm_ada_b, nm_norm_g, big("wg", 2), big("wu", 2), big("wd", 2), big("win", 2), nm_conv_w, big("wco", 2),
             big("wao", 2), big("wo", 2), nm_rel_bias, nm_final_g]
    new_v = [nv_ada_w, nv_ada_b, nv_norm_g, big("wg", 3), big("wu", 3), big("wd", 3), big("win", 3), nv_conv_w, big("wco", 3),
             big("wao", 3), big("wo", 3), nv_rel_bias, nv_final_g]
    return (loss, dx[None], *grads, *deltas, *new_m, *new_v)
```

```python
import functools

import numpy as np
import jax
import jax.numpy as jnp
from jax import lax
from jax.experimental import pallas as pl
from jax.experimental.pallas import tpu as pltpu

f32, bf16 = jnp.float32, jnp.bfloat16
SDS = jax.ShapeDtypeStruct
MESH = pl.DeviceIdType.MESH

D = 1024
DEPTH = 4
N_CHIPS = 4
FB = 704
HD = 64
QKV_W = 4608
REST_W = 5120
IN_W = QKV_W + REST_W
WIN_SH = IN_W // N_CHIPS
ADA_SH = 9 * D // N_CHIPS
BLK = 128
DILATIONS = (1, 4, 16)
Q_BLOCKS = (4, 1, 1)
NUM_BUCKETS, MAX_DISTANCE = 32, 2048
EPS = 1e-6
NEG = -1e30
SCALE = HD ** -0.5
LR, B1, B2, AEPS, WD, STEP = 0.001, 0.9, 0.999, 1e-08, 0.01, 10
BC1 = 1.0 - B1 ** STEP
BC2 = 1.0 - B2 ** STEP
VMEM_LIMIT = 56 * 1024 * 1024
TM = 512
TMW = 1024
TMX = 256
HALO = 16
CB = 512


def _cp(*sem):
    return pltpu.CompilerParams(dimension_semantics=sem if sem else None, vmem_limit_bytes=VMEM_LIMIT)


def _dot(a, b):
    return jnp.dot(a, b, preferred_element_type=f32)


def _dot_nt(a, b):
    return lax.dot_general(a, b, (((1,), (1,)), ((), ())), preferred_element_type=f32)


def _dot_tn(a, b):
    return lax.dot_general(a, b, (((0,), (0,)), ((), ())), preferred_element_type=f32)


def _sigmoid(x):
    return 0.5 * jnp.tanh(0.5 * x) + 0.5


def _norm_fwd(x, g, shift, scale):
    rstd = lax.rsqrt(jnp.mean(x * x, axis=-1, keepdims=True) + EPS)
    xhat = x * rstd
    return xhat * g * (1.0 + scale) + shift, xhat, rstd


def _norm_bwd(dh, xhat, rstd, g, scale):
    dshift = jnp.sum(dh, axis=0, keepdims=True)
    dscale = jnp.sum(dh * xhat * g, axis=0, keepdims=True)
    dg = jnp.sum(dh * xhat * (1.0 + scale), axis=0, keepdims=True)
    dxh = dh * (g * (1.0 + scale))
    dx = rstd * (dxh - xhat * jnp.mean(dxh * xhat, axis=-1, keepdims=True))
    return dx, dshift, dscale, dg


def _allgather_small(xp):
    m_per, n = xp.shape

    def body(x_ref, out_ref, send_sems, recv_sems, local_sem):
        x, y, c = lax.axis_index("x"), lax.axis_index("y"), lax.axis_index("c")
        me, sibling = (x, y, c), (x, y, 1 - c)
        chips = [(1 - x, y), (x, 1 - y), (1 - x, 1 - y)]

        def rows(px, py, pc):
            return out_ref.at[pl.ds((4 * px + 2 * py + pc) * m_per, m_per), :]

        def copy(k, block, to, src=None):
            return pltpu.make_async_remote_copy(
                src_ref=rows(*block) if src is None else src, dst_ref=rows(*block),
                send_sem=send_sems.at[k], recv_sem=recv_sems.at[k], device_id=to, device_id_type=MESH)

        mine = pltpu.make_async_copy(x_ref, rows(*me), local_sem)
        mine.start()
        first = [copy(0, me, sibling, src=x_ref)]
        first += [copy(1 + j, me, (*chip, c), src=x_ref) for j, chip in enumerate(chips)]
        for cp in first:
            cp.start()
        passed = [copy(4 + j, (*chip, c), sibling) for j, chip in enumerate(chips)]
        for j, chip in enumerate(chips):
            copy(1 + j, (*chip, c), me).wait_recv()
            passed[j].start()
        copy(0, sibling, me).wait_recv()
        for j, chip in enumerate(chips):
            copy(4 + j, (*chip, 1 - c), me).wait_recv()
        for cp in first + passed:
            cp.wait_send()
        mine.wait()

    return pl.pallas_call(
        body, name="allgather_small",
        out_shape=SDS((8 * m_per, n), xp.dtype),
        in_specs=[pl.BlockSpec(memory_space=pltpu.VMEM)],
        out_specs=pl.BlockSpec(memory_space=pltpu.VMEM),
        scratch_shapes=[pltpu.SemaphoreType.DMA((7,)), pltpu.SemaphoreType.DMA((7,)), pltpu.SemaphoreType.DMA],
        compiler_params=pltpu.CompilerParams(vmem_limit_bytes=VMEM_LIMIT),
    )(xp)


WCLASSES = (
    ("wg0", "lead", (D, FB)), ("wu0", "lead", (D, FB)), ("wd0", "row", (FB, D)),
    ("wg1", "lead", (D, FB)), ("wu1", "lead", (D, FB)), ("wd1", "row", (FB, D)),
    ("win", "col", (D, WIN_SH)), ("wco", "row", (D // N_CHIPS, D)), ("wao", "col", (512, D // N_CHIPS)),
    ("wo", "row", (D // N_CHIPS, D)),
)
NCLS = len(WCLASSES)


def _full_shape(kind, shp):
    if kind == "lead":
        return (N_CHIPS,) + shp
    if kind == "row":
        return (N_CHIPS * shp[0], shp[1])
    return (shp[0], N_CHIPS * shp[1])


def _shard_view(ref, kind, shp, j):
    if kind == "lead":
        return ref.at[j]
    if kind == "row":
        return ref.at[pl.ds(j * shp[0], shp[0]), :]
    return ref.at[:, pl.ds(j * shp[1], shp[1])]


def _half(ref, shp, h):
    hr = shp[0] // 2
    return ref.at[pl.ds(pl.multiple_of(h * hr, 16), hr), :]


def _gather_weights(shards):
    n = NCLS

    def body(*refs):
        ins, outs = refs[:n], refs[n:2 * n]
        send1, recv1, send2, recv2, lsem = refs[2 * n:]
        x, y, c = lax.axis_index("x"), lax.axis_index("y"), lax.axis_index("c")
        chip = 2 * x + y
        sibling = (x, y, 1 - c)

        for mc in range(N_CHIPS):
            @pl.when(chip == mc)
            def _(mc=mc):
                local = []
                for q, (_, kind, shp) in enumerate(WCLASSES):
                    cp = pltpu.make_async_copy(ins[q], _shard_view(outs[q], kind, shp, mc), lsem.at[q])
                    cp.start()
                    local.append(cp)
                sends = []
                for k in (1, 2, 3):
                    pj = mc ^ k
                    for q, (_, kind, shp) in enumerate(WCLASSES):
                        cp = pltpu.make_async_remote_copy(
                            src_ref=_half(ins[q], shp, c), dst_ref=_half(_shard_view(outs[q], kind, shp, mc), shp, c),
                            send_sem=send1.at[q * 3 + k - 1], recv_sem=recv1.at[q * 3 + k - 1],
                            device_id=(pj >> 1, pj & 1, c), device_id_type=MESH)
                        cp.start()
                        sends.append(cp)
                for k in (1, 2, 3):
                    pj = mc ^ k
                    for q, (_, kind, shp) in enumerate(WCLASSES):
                        landed = _half(_shard_view(outs[q], kind, shp, pj), shp, c)
                        pltpu.make_async_remote_copy(
                            src_ref=landed, dst_ref=landed, send_sem=send1.at[q * 3 + k - 1], recv_sem=recv1.at[q * 3 + k - 1],
                            device_id=(pj >> 1, pj & 1, c), device_id_type=MESH).wait_recv()
                        cp = pltpu.make_async_remote_copy(
                            src_ref=landed, dst_ref=landed, send_sem=send2.at[q * 3 + k - 1], recv_sem=recv2.at[q * 3 + k - 1],
                            device_id=sibling, device_id_type=MESH)
                        cp.start()
                        sends.append(cp)
                for k in (1, 2, 3):
                    pj = mc ^ k
                    for q, (_, kind, shp) in enumerate(WCLASSES):
                        other = _half(_shard_view(outs[q], kind, shp, pj), shp, 1 - c)
                        pltpu.make_async_remote_copy(
                            src_ref=other, dst_ref=other, send_sem=send2.at[q * 3 + k - 1], recv_sem=recv2.at[q * 3 + k - 1],
                            device_id=sibling, device_id_type=MESH).wait_recv()
                for cp in sends:
                    cp.wait_send()
                for cp in local:
                    cp.wait()

    anyspec = pl.BlockSpec(memory_space=pl.ANY)
    return pl.pallas_call(
        body, name="gather_weights",
        out_shape=[SDS(_full_shape(kind, shp), bf16) for _, kind, shp in WCLASSES],
        in_specs=[anyspec] * n, out_specs=[anyspec] * n,
        scratch_shapes=[pltpu.SemaphoreType.DMA((3 * n,)), pltpu.SemaphoreType.DMA((3 * n,)),
                        pltpu.SemaphoreType.DMA((3 * n,)), pltpu.SemaphoreType.DMA((3 * n,)),
                        pltpu.SemaphoreType.DMA((n,))],
    )(*shards)


def _scatter_grads(pieces):
    n = NCLS

    def body(*refs):
        ins, outs = refs[:n], refs[n:2 * n]
        send1, recv1, lsem = refs[2 * n:]
        x, y, c = lax.axis_index("x"), lax.axis_index("y"), lax.axis_index("c")
        chip = 2 * x + y

        for mc in range(N_CHIPS):
            @pl.when(chip == mc)
            def _(mc=mc):
                local, sends = [], []
                for q, (_, kind, shp) in enumerate(WCLASSES):
                    cp = pltpu.make_async_copy(_shard_view(ins[q], kind, shp, mc), outs[q].at[0], lsem.at[q])
                    cp.start()
                    local.append(cp)
                for k in (1, 2, 3):
                    pj = mc ^ k
                    for q, (_, kind, shp) in enumerate(WCLASSES):
                        cp = pltpu.make_async_remote_copy(
                            src_ref=_shard_view(ins[q], kind, shp, pj), dst_ref=outs[q].at[k],
                            send_sem=send1.at[q * 3 + k - 1], recv_sem=recv1.at[q * 3 + k - 1],
                            device_id=(pj >> 1, pj & 1, c), device_id_type=MESH)
                        cp.start()
                        sends.append(cp)
                for cp in sends:
                    cp.wait_recv()
                for cp in sends:
                    cp.wait_send()
                for cp in local:
                    cp.wait()

    anyspec = pl.BlockSpec(memory_space=pl.ANY)
    return pl.pallas_call(
        body, name="scatter_grads",
        out_shape=[SDS((N_CHIPS,) + shp, bf16) for _, _, shp in WCLASSES],
        in_specs=[anyspec] * n, out_specs=[anyspec] * n,
        scratch_shapes=[pltpu.SemaphoreType.DMA((3 * n,)), pltpu.SemaphoreType.DMA((3 * n,)), pltpu.SemaphoreType.DMA((n,))],
    )(*pieces)


def _swap_sibling(ts):
    n = len(ts)

    def body(*refs):
        ins, outs = refs[:n], refs[n:2 * n]
        send, recv = refs[2 * n:]
        x, y, c = lax.axis_index("x"), lax.axis_index("y"), lax.axis_index("c")
        cps = []
        for q in range(n):
            cp = pltpu.make_async_remote_copy(src_ref=ins[q], dst_ref=outs[q], send_sem=send.at[q], recv_sem=recv.at[q],
                                              device_id=(x, y, 1 - c), device_id_type=MESH)
            cp.start()
            cps.append(cp)
        for cp in cps:
            cp.wait_recv()
        for cp in cps:
            cp.wait_send()

    anyspec = pl.BlockSpec(memory_space=pl.ANY)
    return pl.pallas_call(
        body, name="swap_sibling",
        out_shape=[SDS(t.shape, t.dtype) for t in ts],
        in_specs=[anyspec] * n, out_specs=[anyspec] * n,
        scratch_shapes=[pltpu.SemaphoreType.DMA((n,)), pltpu.SemaphoreType.DMA((n,))],
    )(*ts)


HBM_SPEC = pl.BlockSpec(memory_space=pltpu.HBM)
SEM_SPEC = pl.BlockSpec(memory_space=pltpu.SEMAPHORE)
ANY_SPEC = pl.BlockSpec(memory_space=pl.ANY)
EFFECT = pltpu.SideEffectType.DATAFLOW_SIDE_EFFECTING
N_COPIES = 3 * NCLS


def _in_hbm(a):
    return pltpu.with_memory_space_constraint(a, pltpu.HBM)


def _chip_index():
    return 2 * lax.axis_index("x") + lax.axis_index("y")


def _place_own(shards):
    n = NCLS

    def body(*refs):
        ins, outs, lsem = refs[:n], refs[n:2 * n], refs[2 * n]
        chip = _chip_index()
        for mc in range(N_CHIPS):
            @pl.when(chip == mc)
            def _(mc=mc):
                cps = [pltpu.make_async_copy(ins[q], _shard_view(outs[q], kind, shp, mc), lsem.at[q])
                       for q, (_, kind, shp) in enumerate(WCLASSES)]
                for cp in cps:
                    cp.start()
                for cp in cps:
                    cp.wait()

    return pl.pallas_call(
        body, name="place_own",
        out_shape=[SDS(_full_shape(kind, shp), bf16) for _, kind, shp in WCLASSES],
        in_specs=[ANY_SPEC] * n, out_specs=[ANY_SPEC] * n,
        scratch_shapes=[pltpu.SemaphoreType.DMA((n,))],
    )(*shards)


def _take_own(pieces):
    n = NCLS

    def body(*refs):
        ins, outs, lsem = refs[:n], refs[n:2 * n], refs[2 * n]
        chip = _chip_index()
        for mc in range(N_CHIPS):
            @pl.when(chip == mc)
            def _(mc=mc):
                cps = [pltpu.make_async_copy(_shard_view(ins[q], kind, shp, mc), outs[q].at[0], lsem.at[q])
                       for q, (_, kind, shp) in enumerate(WCLASSES)]
                for cp in cps:
                    cp.start()
                for cp in cps:
                    cp.wait()

    return pl.pallas_call(
        body, name="take_own",
        out_shape=[SDS((N_CHIPS,) + shp, bf16) for _, _, shp in WCLASSES],
        in_specs=[ANY_SPEC] * n, out_specs=[ANY_SPEC] * n,
        scratch_shapes=[pltpu.SemaphoreType.DMA((n,))],
    )(*pieces)


def _split_start(name, srcs, dsts, after, src_view, dst_view):
    n = NCLS

    def body(*refs):
        src, dst = refs[:n], refs[n:2 * n]
        send, recv = refs[2 * n + 1], refs[2 * n + 2]
        token = refs[-1]
        c = lax.axis_index("c")
        chip = _chip_index()
        for mc in range(N_CHIPS):
            @pl.when(chip == mc)
            def _(mc=mc):
                for k in (1, 2, 3):
                    pj = mc ^ k
                    for q in range(n):
                        pltpu.make_async_remote_copy(
                            src_ref=src_view(src[q], q, mc, pj), dst_ref=dst_view(dst[q], q, mc, k),
                            send_sem=send.at[q * 3 + k - 1], recv_sem=recv.at[q * 3 + k - 1],
                            device_id=(pj >> 1, pj & 1, c), device_id_type=MESH).start()
        token[...] = jnp.zeros_like(token)

    return pl.pallas_call(
        body, name=name,
        out_shape=(pltpu.SemaphoreType.DMA((N_COPIES,)), pltpu.SemaphoreType.DMA((N_COPIES,)),
                   *[pltpu.HBM(a.shape, a.dtype) for a in srcs], *[pltpu.HBM(a.shape, a.dtype) for a in dsts], SDS((8, 128), f32)),
        in_specs=[HBM_SPEC] * (2 * n) + [ANY_SPEC],
        out_specs=(SEM_SPEC, SEM_SPEC, *([HBM_SPEC] * (2 * n)), pl.BlockSpec(memory_space=pltpu.VMEM)),
        input_output_aliases={i: 2 + i for i in range(2 * n)},
        compiler_params=pltpu.CompilerParams(has_side_effects=EFFECT),
    )(*[_in_hbm(a) for a in srcs], *[_in_hbm(a) for a in dsts], after)


def _split_wait(name, started, after, arrival_view):
    n = NCLS
    send, recv = started[0], started[1]
    srcs, dsts = started[2:2 + n], started[2 + n:2 + 2 * n]

    def body(*refs):
        src, dst = refs[:n], refs[n:2 * n]
        send_sem, recv_sem = refs[2 * n], refs[2 * n + 1]
        x, y, c = lax.axis_index("x"), lax.axis_index("y"), lax.axis_index("c")
        for k in (1, 2, 3):
            for q in range(n):
                arrival = arrival_view(dst[q], q, k)
                cp = pltpu.make_async_remote_copy(
                    src_ref=arrival, dst_ref=arrival, send_sem=send_sem.at[q * 3 + k - 1], recv_sem=recv_sem.at[q * 3 + k - 1],
                    device_id=(x, y, 1 - c), device_id_type=MESH)
                cp.wait_send()
                cp.wait_recv()

    out = pl.pallas_call(
        body, name=name,
        out_shape=(*[pltpu.HBM(a.shape, a.dtype) for a in srcs], *[pltpu.HBM(a.shape, a.dtype) for a in dsts]),
        in_specs=[HBM_SPEC] * (2 * n) + [SEM_SPEC, SEM_SPEC, ANY_SPEC],
        out_specs=tuple([HBM_SPEC] * (2 * n)),
        input_output_aliases={i: i for i in range(2 * n)},
        compiler_params=pltpu.CompilerParams(has_side_effects=EFFECT),
    )(*srcs, *dsts, send, recv, after)
    return out[n:]


def _cls(q):
    return WCLASSES[q][1], WCLASSES[q][2]


def _gather_start(shards, after):
    fulls = _place_own(shards)
    return _split_start("gather_start", shards, fulls, after,
                        lambda ref, q, mc, pj: ref,
                        lambda ref, q, mc, k: _shard_view(ref, *_cls(q), mc))


def _gather_wait(started, after):
    return _split_wait("gather_wait", started, after, lambda ref, q, k: _shard_view(ref, *_cls(q), 0))


def _scatter_start(pieces, after):
    lands = _take_own(pieces)
    return _split_start("scatter_start", pieces, lands, after,
                        lambda ref, q, mc, pj: _shard_view(ref, *_cls(q), pj),
                        lambda ref, q, mc, k: ref.at[k])


def _scatter_wait(started, after):
    return _split_wait("scatter_wait", started, after, lambda ref, q, k: ref.at[k])


GROUPS = {"A": (0, 1, 2), "B": (6,), "C": (7, 8, 9), "D": (3, 4, 5)}


def _own_spec(kind, shp, tr):
    R, C = shp
    if kind == "lead":
        return pl.BlockSpec((None, tr, C), lambda i, chip: (chip[0], i, 0))
    if kind == "row":
        return pl.BlockSpec((tr, C), lambda i, chip: (chip[0] * (R // tr) + i, 0))
    return pl.BlockSpec((tr, C), lambda i, chip: (i, chip[0]))


def _cast_place(shards, kind, shp, chip_arr):
    n = len(shards)
    R, C = shp
    tr = _row_tile(R, C)

    def body(chip_ref, *refs):
        for q in range(n):
            refs[n + q][...] = refs[q][...].astype(bf16)

    return pl.pallas_call(
        body, name="cast_place",
        grid_spec=pltpu.PrefetchScalarGridSpec(
            num_scalar_prefetch=1, grid=(R // tr,),
            in_specs=[pl.BlockSpec((tr, C), lambda i, chip: (i, 0))] * n,
            out_specs=[_own_spec(kind, shp, tr)] * n),
        out_shape=[SDS(_full_shape(kind, shp), bf16)] * n,
        compiler_params=_cp("arbitrary"),
    )(chip_arr, *shards)


def _sum_own_slots(piece, land, kind, shp, chip_arr):
    R, C = shp
    tr = _row_tile(R, C)

    def body(chip_ref, p_ref, l_ref, t_ref):
        t = p_ref[...].astype(f32)
        for k in range(N_CHIPS - 1):
            t = t + l_ref[k].astype(f32)
        t_ref[...] = t

    return pl.pallas_call(
        body, name="sum_own_slots",
        grid_spec=pltpu.PrefetchScalarGridSpec(
            num_scalar_prefetch=1, grid=(R // tr,),
            in_specs=[_own_spec(kind, shp, tr), pl.BlockSpec((N_CHIPS - 1, tr, C), lambda i, chip: (0, i, 0))],
            out_specs=pl.BlockSpec((tr, C), lambda i, chip: (i, 0))),
        out_shape=SDS((R, C), f32),
        compiler_params=_cp("arbitrary"),
    )(chip_arr, piece, land)


def _xfer_start(name, arrays, ng, after, src_view, dst_view):
    na = len(arrays)

    def body(*refs):
        arr = refs[:na]
        send, recv, token = refs[na + 1], refs[na + 2], refs[-1]
        c = lax.axis_index("c")
        chip = _chip_index()
        for mc in range(N_CHIPS):
            @pl.when(chip == mc)
            def _(mc=mc):
                for k in (1, 2, 3):
                    pj = mc ^ k
                    for i in range(ng):
                        pltpu.make_async_remote_copy(
                            src_ref=src_view(arr, i, mc, pj), dst_ref=dst_view(arr, i, mc, k),
                            send_sem=send.at[i * 3 + k - 1], recv_sem=recv.at[i * 3 + k - 1],
                            device_id=(pj >> 1, pj & 1, c), device_id_type=MESH).start()
        token[...] = jnp.zeros_like(token)

    return pl.pallas_call(
        body, name=name,
        out_shape=(pltpu.SemaphoreType.DMA((3 * ng,)), pltpu.SemaphoreType.DMA((3 * ng,)),
                   *[pltpu.HBM(a.shape, a.dtype) for a in arrays], SDS((8, 128), f32)),
        in_specs=[HBM_SPEC] * na + [ANY_SPEC],
        out_specs=(SEM_SPEC, SEM_SPEC, *([HBM_SPEC] * na), pl.BlockSpec(memory_space=pltpu.VMEM)),
        input_output_aliases={i: 2 + i for i in range(na)},
        compiler_params=pltpu.CompilerParams(has_side_effects=EFFECT),
    )(*[_in_hbm(a) for a in arrays], after)


def _xfer_wait(name, started, ng, after, arrival_view):
    send, recv = started[0], started[1]
    arrays = started[2:-1]
    na = len(arrays)

    def body(*refs):
        arr = refs[:na]
        send_sem, recv_sem = refs[na], refs[na + 1]
        x, y, c = lax.axis_index("x"), lax.axis_index("y"), lax.axis_index("c")
        for k in (1, 2, 3):
            for i in range(ng):
                arrival = arrival_view(arr, i)
                cp = pltpu.make_async_remote_copy(
                    src_ref=arrival, dst_ref=arrival, send_sem=send_sem.at[i * 3 + k - 1], recv_sem=recv_sem.at[i * 3 + k - 1],
                    device_id=(x, y, 1 - c), device_id_type=MESH)
                cp.wait_send()
                cp.wait_recv()

    return pl.pallas_call(
        body, name=name,
        out_shape=tuple(pltpu.HBM(a.shape, a.dtype) for a in arrays),
        in_specs=[HBM_SPEC] * na + [SEM_SPEC, SEM_SPEC, ANY_SPEC],
        out_specs=tuple([HBM_SPEC] * na),
        input_output_aliases={i: i for i in range(na)},
        compiler_params=pltpu.CompilerParams(has_side_effects=EFFECT),
    )(*arrays, send, recv, after)


def _gather_group_start(tag, group, shards_f32, chip_arr, after):
    fulls = [None] * len(group)
    by_shape = {}
    for i, q in enumerate(group):
        by_shape.setdefault(_cls(q), []).append(i)
    for (kind, shp), idx in by_shape.items():
        for i, f in zip(idx, _cast_place([shards_f32[i] for i in idx], kind, shp, chip_arr)):
            fulls[i] = f
    view = lambda arr, i, mc, _: _shard_view(arr[i], *_cls(group[i]), mc)
    return _xfer_start("gather_start_" + tag, fulls, len(group), after, view, view)


def _gather_group_wait(tag, group, started, after):
    return _xfer_wait("gather_wait_" + tag, started, len(group), after, lambda arr, i: _shard_view(arr[i], *_cls(group[i]), 0))


def _scatter_group_start(tag, group, pieces, after):
    ng = len(group)
    lands = [lax.empty((N_CHIPS - 1,) + _cls(q)[1], bf16) for q in group]
    return _xfer_start("scatter_start_" + tag, list(pieces) + lands, ng, after,
                       lambda arr, i, mc, pj: _shard_view(arr[i], *_cls(group[i]), pj),
                       lambda arr, i, mc, k: arr[ng + i].at[k - 1])


def _scatter_group_wait(tag, group, started, after):
    ng = len(group)
    out = _xfer_wait("scatter_wait_" + tag, started, ng, after, lambda arr, i: arr[ng + i].at[0])
    return out[:ng], out[ng:]


def _mod_shards(c_all, ada_w, ada_b_sh):
    tn = ADA_SH // 3

    def body(c_ref, w_ref, b_ref, o_ref, cs_ref):
        cv = c_ref[...]
        cs = cv * _sigmoid(cv)
        cs_ref[...] = cs
        o_ref[...] = _dot(cs.astype(bf16), w_ref[...].astype(bf16)) + b_ref[...]

    return pl.pallas_call(
        body, name="mod_shards", grid=(DEPTH, 3),
        out_shape=[SDS((DEPTH, 8, ADA_SH), f32), SDS((8, D), f32)],
        in_specs=[pl.BlockSpec((8, D), lambda l, t: (0, 0)),
                  pl.BlockSpec((None, D, tn), lambda l, t: (l, 0, t)),
                  pl.BlockSpec((None, 1, tn), lambda l, t: (l, 0, t))],
        out_specs=[pl.BlockSpec((None, 8, tn), lambda l, t: (l, 0, t)), pl.BlockSpec((8, D), lambda l, t: (0, 0))],
        compiler_params=_cp("arbitrary", "arbitrary"),
    )(c_all, ada_w, ada_b_sh.reshape(DEPTH, 1, ADA_SH))


def _t5_bucket(dist):
    exact = NUM_BUCKETS // 2
    dd = np.maximum(dist, 1).astype(np.float32)
    large = exact + (np.log(dd / exact) / np.log(MAX_DISTANCE / exact) * (NUM_BUCKETS - exact)).astype(np.int32)
    large = np.minimum(large, NUM_BUCKETS - 1)
    return np.where(dist < exact, dist, large).astype(np.int32)


def _bucket_table():
    i = np.arange(BLK)[:, None]
    j = np.arange(2 * BLK)[None, :]
    rel = i - j + BLK
    return np.stack([_t5_bucket(np.maximum(rel, 0) * d) for d in DILATIONS]).astype(np.int32)


def _band():
    rel = lax.broadcasted_iota(jnp.int32, (BLK, 2 * BLK), 0) - lax.broadcasted_iota(jnp.int32, (BLK, 2 * BLK), 1) + BLK
    return (rel >= 0) & (rel <= BLK)


def _bias_blocks(rel_bias, buckets):
    def body(tab_ref, bk_ref, o_ref):
        h = pl.program_id(0)
        bk = bk_ref[...]
        acc = jnp.zeros((BLK, 2 * BLK), f32)
        for b in range(NUM_BUCKETS):
            acc = jnp.where(bk == b, tab_ref[b, h], acc)
        o_ref[...] = jnp.where(_band(), acc, NEG)

    return pl.pallas_call(
        body, name="bias_blocks", grid=(24,),
        out_shape=SDS((24, BLK, 2 * BLK), f32),
        in_specs=[pl.BlockSpec(memory_space=pltpu.SMEM), pl.BlockSpec((None, BLK, 2 * BLK), lambda h: (h // 8, 0, 0))],
        out_specs=pl.BlockSpec((None, BLK, 2 * BLK), lambda h: (h, 0, 0)),
        compiler_params=_cp("arbitrary"),
    )(rel_bias, buckets)


def _bias_grad(dsaccs, buckets):
    nl = len(dsaccs)

    def body(*refs):
        bk = refs[nl][...]
        tot = refs[0][...]
        for r in refs[1:nl]:
            tot = tot + r[...]
        lane = lax.broadcasted_iota(jnp.int32, (1, 128), 1)
        row = jnp.zeros((1, 128), f32)
        for b in range(NUM_BUCKETS):
            row = jnp.where(lane == b, jnp.sum(jnp.where(bk == b, tot, 0.0)), row)
        refs[nl + 1][...] = row

    return pl.pallas_call(
        body, name="bias_grad", grid=(24,),
        out_shape=SDS((24, 1, 128), f32),
        in_specs=[pl.BlockSpec((None, BLK, 2 * BLK), lambda h: (h, 0, 0))] * nl
                 + [pl.BlockSpec((None, BLK, 2 * BLK), lambda h: (h // 8, 0, 0))],
        out_specs=pl.BlockSpec((None, 1, 128), lambda h: (h, 0, 0)),
        compiler_params=_cp("arbitrary"),
    )(*dsaccs, buckets)


def _ffn_fwd(x, mod9, g3, wg, wu, wd, sub):
    S = x.shape[0]

    def body(x_ref, mod_ref, g_ref, wg_ref, wu_ref, wd_ref, xo_ref, h_ref, a_ref, u_ref, y_ref, acc):
        j = pl.program_id(1)

        @pl.when(j == 0)
        def _():
            h, _, _ = _norm_fwd(x_ref[...], g_ref[sub:sub + 1, :], mod_ref[3 * sub:3 * sub + 1, :], mod_ref[3 * sub + 1:3 * sub + 2, :])
            h_ref[...] = h.astype(bf16)
            acc[...] = jnp.zeros_like(acc)

        h = h_ref[...]
        a = _dot(h, wg_ref[...])
        u = _dot(h, wu_ref[...])
        a_ref[...] = a.astype(bf16)
        u_ref[...] = u.astype(bf16)
        hid = (a * _sigmoid(a) * u).astype(bf16)
        acc[...] += _dot(hid, wd_ref[...])

        @pl.when(j == N_CHIPS - 1)
        def _():
            y = acc[...]
            y_ref[...] = y.astype(bf16)
            xo_ref[...] = x_ref[...] + 0.5 * mod_ref[3 * sub + 2:3 * sub + 3, :] * y

    row = pl.BlockSpec((TM, D), lambda i, j: (i, 0))
    return pl.pallas_call(
        body, name="ffn_fwd", grid=(S // TM, N_CHIPS),
        out_shape=[SDS((S, D), f32), SDS((S, D), bf16), SDS((N_CHIPS, S, FB), bf16), SDS((N_CHIPS, S, FB), bf16), SDS((S, D), bf16)],
        in_specs=[row, pl.BlockSpec((9, D), lambda i, j: (0, 0)), pl.BlockSpec((3, D), lambda i, j: (0, 0)),
                  pl.BlockSpec((None, D, FB), lambda i, j: (j, 0, 0)), pl.BlockSpec((None, D, FB), lambda i, j: (j, 0, 0)),
                  pl.BlockSpec((FB, D), lambda i, j: (j, 0))],
        out_specs=[row, row, pl.BlockSpec((None, TM, FB), lambda i, j: (j, i, 0)), pl.BlockSpec((None, TM, FB), lambda i, j: (j, i, 0)), row],
        scratch_shapes=[pltpu.VMEM((TM, D), f32)],
        compiler_params=_cp("arbitrary", "arbitrary"),
    )(x, mod9, g3, wg, wu, wd)


def _ffn_bwd1(dxo, x, mod9, g3, y, a, u, wg, wu, wd, sub):
    S = x.shape[0]

    def body(dxo_ref, x_ref, mod_ref, g_ref, y_ref, a_ref, u_ref, wg_ref, wu_ref, wd_ref,
             dxi_ref, da_ref, du_ref, hid_ref, dy_ref, sm_ref, acc):
        i, j = pl.program_id(0), pl.program_id(1)
        gate = mod_ref[3 * sub + 2:3 * sub + 3, :]

        @pl.when((i == 0) & (j == 0))
        def _():
            sm_ref[...] = jnp.zeros_like(sm_ref)

        @pl.when(j == 0)
        def _():
            dxo_v = dxo_ref[...]
            dy_ref[...] = (0.5 * gate * dxo_v).astype(bf16)
            sm_ref[2:3, :] += jnp.sum(0.5 * y_ref[...].astype(f32) * dxo_v, axis=0, keepdims=True)
            acc[...] = jnp.zeros_like(acc)

        av, uv = a_ref[...].astype(f32), u_ref[...].astype(f32)
        sg = _sigmoid(av)
        sil = av * sg
        dhid = _dot_nt(dy_ref[...], wd_ref[...])
        da = (dhid * uv * (sg * (1.0 + av * (1.0 - sg)))).astype(bf16)
        du = (dhid * sil).astype(bf16)
        da_ref[...] = da
        du_ref[...] = du
        hid_ref[...] = (sil * uv).astype(bf16)
        acc[...] += _dot_nt(da, wg_ref[...]) + _dot_nt(du, wu_ref[...])

        @pl.when(j == N_CHIPS - 1)
        def _():
            g = g_ref[sub:sub + 1, :]
            scale = mod_ref[3 * sub + 1:3 * sub + 2, :]
            _, xhat, rstd = _norm_fwd(x_ref[...], g, mod_ref[3 * sub:3 * sub + 1, :], scale)
            dx, dshift, dscale, dg = _norm_bwd(acc[...], xhat, rstd, g, scale)
            dxi_ref[...] = dxo_ref[...] + dx
            sm_ref[0:1, :] += dshift
            sm_ref[1:2, :] += dscale
            sm_ref[3:4, :] += dg

    row = pl.BlockSpec((TM, D), lambda i, j: (i, 0))
    hidb = pl.BlockSpec((None, TM, FB), lambda i, j: (j, i, 0))
    wcol = pl.BlockSpec((None, D, FB), lambda i, j: (j, 0, 0))
    return pl.pallas_call(
        body, name="ffn_bwd1", grid=(S // TM, N_CHIPS),
        out_shape=[SDS((S, D), f32), SDS((N_CHIPS, S, FB), bf16), SDS((N_CHIPS, S, FB), bf16), SDS((N_CHIPS, S, FB), bf16),
                   SDS((S, D), bf16), SDS((8, D), f32)],
        in_specs=[row, row, pl.BlockSpec((9, D), lambda i, j: (0, 0)), pl.BlockSpec((3, D), lambda i, j: (0, 0)), row,
                  hidb, hidb, wcol, wcol, pl.BlockSpec((FB, D), lambda i, j: (j, 0))],
        out_specs=[row, hidb, hidb, hidb, row, pl.BlockSpec((8, D), lambda i, j: (0, 0))],
        scratch_shapes=[pltpu.VMEM((TM, D), f32)],
        compiler_params=_cp("arbitrary", "arbitrary"),
    )(dxo, x, mod9, g3, y, a, u, wg, wu, wd)


def _ffn_bwd2(h, da, du, hid, dy):
    S = h.shape[0]
    ni = S // TMW

    def body(h_ref, da_ref, du_ref, hid_ref, dy_ref, dwg_ref, dwu_ref, dwd_ref, ag, au, ad):
        i = pl.program_id(1)

        @pl.when(i == 0)
        def _():
            ag[...] = jnp.zeros_like(ag)
            au[...] = jnp.zeros_like(au)
            ad[...] = jnp.zeros_like(ad)

        hv = h_ref[...]
        ag[...] += _dot_tn(hv, da_ref[...])
        au[...] += _dot_tn(hv, du_ref[...])
        ad[...] += _dot_tn(hid_ref[...], dy_ref[...])

        @pl.when(i == ni - 1)
        def _():
            dwg_ref[...] = ag[...].astype(bf16)
            dwu_ref[...] = au[...].astype(bf16)
            dwd_ref[...] = ad[...].astype(bf16)

    row = pl.BlockSpec((TMW, D), lambda j, i: (i, 0))
    hidb = pl.BlockSpec((None, TMW, FB), lambda j, i: (j, i, 0))
    wcol = pl.BlockSpec((None, D, FB), lambda j, i: (j, 0, 0))
    return pl.pallas_call(
        body, name="ffn_bwd2", grid=(N_CHIPS, ni),
        out_shape=[SDS((N_CHIPS, D, FB), bf16), SDS((N_CHIPS, D, FB), bf16), SDS((N_CHIPS * FB, D), bf16)],
        in_specs=[row, hidb, hidb, hidb, row],
        out_specs=[wcol, wcol, pl.BlockSpec((FB, D), lambda j, i: (j, 0))],
        scratch_shapes=[pltpu.VMEM((D, FB), f32), pltpu.VMEM((D, FB), f32), pltpu.VMEM((FB, D), f32)],
        compiler_params=_cp("arbitrary", "arbitrary"),
    )(h, da, du, hid, dy)


def _mix_qkv(x, mod9, g3, win):
    S = x.shape[0]

    def body(x_ref, mod_ref, g_ref, w_ref, h_ref, o_ref):
        @pl.when(pl.program_id(1) == 0)
        def _():
            h, _, _ = _norm_fwd(x_ref[...], g_ref[1:2, :], mod_ref[3:4, :], mod_ref[4:5, :])
            h_ref[...] = h.astype(bf16)

        o_ref[...] = _dot(h_ref[...], w_ref[...])

    row = pl.BlockSpec((TM, D), lambda i, j: (i, 0))
    return pl.pallas_call(
        body, name="mix_qkv", grid=(S // TM, QKV_W // CB),
        out_shape=[SDS((S, D), bf16), SDS((S, QKV_W), f32)],
        in_specs=[row, pl.BlockSpec((9, D), lambda i, j: (0, 0)), pl.BlockSpec((3, D), lambda i, j: (0, 0)),
                  pl.BlockSpec((D, CB), lambda i, j: (0, j))],
        out_specs=[row, pl.BlockSpec((TM, CB), lambda i, j: (i, j))],
        compiler_params=_cp("arbitrary", "arbitrary"),
    )(x, mod9, g3, win)


def _mix_rest(h, win):
    S = h.shape[0]
    off = QKV_W // CB

    def body(h_ref, w_ref, o_ref):
        o_ref[...] = _dot(h_ref[...], w_ref[...]).astype(bf16)

    return pl.pallas_call(
        body, name="mix_rest", grid=(S // TM, REST_W // CB),
        out_shape=SDS((S, REST_W), bf16),
        in_specs=[pl.BlockSpec((TM, D), lambda i, j: (i, 0)), pl.BlockSpec((D, CB), lambda i, j: (0, off + j))],
        out_specs=pl.BlockSpec((TM, CB), lambda i, j: (i, j)),
        compiler_params=_cp("arbitrary", "arbitrary"),
    )(h, win)


def _attn_fwd(qkv, bias, g):
    S = qkv.shape[0]
    d = DILATIONS[g]
    nq = Q_BLOCKS[g]
    Rb = BLK * d
    R = Rb * nq
    nb = S // R
    qb, kb, vb = 4 * g, 12 + 4 * g, 24 + 4 * g

    def body(q_ref, kc_ref, kp_ref, vc_ref, vp_ref, b_ref, o_ref, l_ref):
        n = pl.program_id(1)
        col = lax.broadcasted_iota(jnp.int32, (BLK, 2 * BLK), 1)
        first = jnp.where((col < BLK) & (n == 0), NEG, 0.0)
        head0 = lax.broadcasted_iota(jnp.int32, (1, 2 * HD), 1) < HD

        def one(b, r):
            sl = pl.ds(b * Rb + r, BLK, stride=d)
            q = q_ref[sl, :]
            if b == 0:
                kp, vp = kp_ref[pl.ds(r, BLK, stride=d), :], vp_ref[pl.ds(r, BLK, stride=d), :]
            else:
                before = pl.ds((b - 1) * Rb + r, BLK, stride=d)
                kp, vp = kc_ref[before, :], vc_ref[before, :]
            kk = jnp.concatenate([kp, kc_ref[sl, :]], axis=0).astype(bf16)
            vv = jnp.concatenate([vp, vc_ref[sl, :]], axis=0).astype(bf16)
            os, ls = [], []
            for hh in range(2):
                qm = jnp.where(head0 if hh == 0 else ~head0, q, 0.0).astype(bf16)
                s = _dot_nt(qm, kk) * SCALE + b_ref[hh]
                if b == 0:
                    s = s + first
                m = jnp.max(s, axis=-1, keepdims=True)
                p = jnp.exp(s - m)
                l = jnp.sum(p, axis=-1, keepdims=True)
                os.append(_dot(p.astype(bf16), vv) / l)
                ls.append(m + jnp.log(l))
            o_ref[sl, :] = jnp.where(head0, os[0], os[1])
            l_ref[sl, :] = jnp.where(head0, ls[0], ls[1])

        for b in range(nq):
            if d == 1:
                one(b, 0)
            else:
                lax.fori_loop(0, d, lambda r, carry, b=b: (one(b, r), carry)[1], 0, unroll=4)

    def blk(cb, prev):
        if prev:
            return pl.BlockSpec((Rb, 128), lambda hp, n: (jnp.maximum(n * nq - 1, 0), cb + hp))
        return pl.BlockSpec((R, 128), lambda hp, n: (n, cb + hp))

    outb = pl.BlockSpec((R, 128), lambda hp, n: (n, hp))
    return pl.pallas_call(
        body, name=f"attn_fwd_d{d}", grid=(4, nb),
        out_shape=[SDS((S, 512), f32), SDS((S, 512), f32)],
        in_specs=[blk(qb, False), blk(kb, False), blk(kb, True), blk(vb, False), blk(vb, True),
                  pl.BlockSpec((2, BLK, 2 * BLK), lambda hp, n: (4 * g + hp, 0, 0))],
        out_specs=[outb, outb],
        compiler_params=_cp("arbitrary", "arbitrary"),
    )(qkv, qkv, qkv, qkv, qkv, bias)


def _attn_bwd(qkv, do, o, lse, bias, dq_all, dk_all, dv_all, g):
    S = qkv.shape[0]
    d = DILATIONS[g]
    nq = Q_BLOCKS[g]
    Rb = BLK * d
    R = Rb * nq
    nb = S // R
    qb, kb, vb = 4 * g, 12 + 4 * g, 24 + 4 * g

    def body(q_ref, kc_ref, kp_ref, vc_ref, vp_ref, do_ref, o_ref, l_ref, b_ref, dqi, dki, dvi,
             dq_ref, dk_ref, dv_ref, ds_ref, ck, cv, tk, tv):
        n = pl.program_id(1)
        col = lax.broadcasted_iota(jnp.int32, (BLK, 2 * BLK), 1)
        first = jnp.where((col < BLK) & (n == 0), NEG, 0.0)

        @pl.when(n == 0)
        def _():
            ck[...] = jnp.zeros_like(ck)
            cv[...] = jnp.zeros_like(cv)
            ds_ref[...] = jnp.zeros_like(ds_ref)

        @pl.when(n < nb)
        def _():
            head0 = lax.broadcasted_iota(jnp.int32, (1, 2 * HD), 1) < HD

            def one(b, r):
                sl = pl.ds(b * Rb + r, BLK, stride=d)
                before = pl.ds((max(b, 1) - 1) * Rb + r, BLK, stride=d)
                q = q_ref[sl, :]
                if b == 0:
                    kp, vp = kp_ref[pl.ds(r, BLK, stride=d), :], vp_ref[pl.ds(r, BLK, stride=d), :]
                else:
                    kp, vp = kc_ref[before, :], vc_ref[before, :]
                kk = jnp.concatenate([kp, kc_ref[sl, :]], axis=0).astype(bf16)
                vv = jnp.concatenate([vp, vc_ref[sl, :]], axis=0).astype(bf16)
                dov, lv = do_ref[sl, :], l_ref[sl, :]
                prod = dov * o_ref[sl, :]
                qb, dob = q.astype(bf16), dov.astype(bf16)
                dqs, dks, dvs = [], [], []
                for hh in range(2):
                    msk = head0 if hh == 0 else ~head0
                    qm = jnp.where(msk, q, 0.0).astype(bf16)
                    dom = jnp.where(msk, dov, 0.0).astype(bf16)
                    dsum = jnp.sum(jnp.where(msk, prod, 0.0), axis=-1, keepdims=True)
                    s = _dot_nt(qm, kk) * SCALE + b_ref[hh]
                    if b == 0:
                        s = s + first
                    p = jnp.exp(s - lv[:, HD * hh:HD * hh + 1])
                    ds = p * (_dot_nt(dom, vv) - dsum)
                    ds_ref[hh] += ds
                    dsb = ds.astype(bf16)
                    dqs.append(_dot(dsb, kk) * SCALE)
                    dks.append(_dot_tn(dsb, qb) * SCALE)
                    dvs.append(_dot_tn(p.astype(bf16), dob))
                dq_ref[sl, :] = jnp.where(head0, dqs[0], dqs[1])
                dk = jnp.where(head0, dks[0], dks[1])
                dv = jnp.where(head0, dvs[0], dvs[1])
                tk[sl, :] = dk[BLK:]
                tv[sl, :] = dv[BLK:]
                if b == 0:
                    prev_rows = pl.ds((nq - 1) * Rb + r, BLK, stride=d)
                    ck[prev_rows, :] += dk[:BLK]
                    cv[prev_rows, :] += dv[:BLK]
                else:
                    tk[before, :] += dk[:BLK]
                    tv[before, :] += dv[:BLK]

            for b in range(nq):
                if d == 1:
                    one(b, 0)
                else:
                    lax.fori_loop(0, d, lambda r, carry, b=b: (one(b, r), carry)[1], 0, unroll=2)
            dk_ref[...] = ck[...]
            dv_ref[...] = cv[...]
            ck[...] = tk[...]
            cv[...] = tv[...]

        @pl.when(n == nb)
        def _():
            dk_ref[...] = ck[...]
            dv_ref[...] = cv[...]

    last = nb - 1

    def blk(cb, prev):
        if prev:
            return pl.BlockSpec((Rb, 128), lambda hp, n: (jnp.maximum(jnp.minimum(n, last) * nq - 1, 0), cb + hp))
        return pl.BlockSpec((R, 128), lambda hp, n: (jnp.minimum(n, last), cb + hp))

    cur = pl.BlockSpec((R, 128), lambda hp, n: (jnp.minimum(n, last), hp))
    anyspec = pl.BlockSpec(memory_space=pl.ANY)
    dqo = pl.BlockSpec((R, 128), lambda hp, n: (jnp.minimum(n, last), 4 * g + hp))
    dko = pl.BlockSpec((R, 128), lambda hp, n: (jnp.maximum(n - 1, 0), 4 * g + hp))
    return pl.pallas_call(
        body, name=f"attn_bwd_d{d}", grid=(4, nb + 1),
        out_shape=[SDS((S, 1536), f32), SDS((S, 1536), f32), SDS((S, 1536), f32), SDS((8, BLK, 2 * BLK), f32)],
        in_specs=[blk(qb, False), blk(kb, False), blk(kb, True), blk(vb, False), blk(vb, True), cur, cur, cur,
                  pl.BlockSpec((2, BLK, 2 * BLK), lambda hp, n: (4 * g + hp, 0, 0)), anyspec, anyspec, anyspec],
        out_specs=[dqo, dko, dko, pl.BlockSpec((2, BLK, 2 * BLK), lambda hp, n: (hp, 0, 0))],
        scratch_shapes=[pltpu.VMEM((R, 128), f32)] * 4,
        input_output_aliases={9: 0, 10: 1, 11: 2},
        compiler_params=_cp("arbitrary", "arbitrary"),
    )(qkv, qkv, qkv, qkv, qkv, do, o, lse, bias, dq_all, dk_all, dv_all)


def _conv_z(cc, ch, hc, hh, cw_ref, first):
    halo = jnp.where(first, 0.0, hc.astype(f32) * hh.astype(f32))
    T = jnp.concatenate([halo, cc * ch], axis=0)
    z = cw_ref[2:3, :] * T + cw_ref[1:2, :] * pltpu.roll(T, 1, 0) + cw_ref[0:1, :] * pltpu.roll(T, 2, 0)
    return T, z[HALO:]


def _rest_specs(tm, with_next):
    per = tm // HALO
    specs = [pl.BlockSpec((tm, D), functools.partial(lambda i, k: (i, k), k=k)) for k in range(5)]
    specs += [pl.BlockSpec((HALO, D), functools.partial(lambda i, k: (jnp.maximum(i * per - 1, 0), k), k=k)) for k in (1, 2)]
    return specs


def _mix_out_fwd(x, mod9, rest, ogs, lgs, cw, wco, wao, wo):
    S = x.shape[0]
    tm = TMX

    def body(x_ref, mod_ref, cb_ref, cc_ref, ch_ref, gc_ref, ga_ref, hc_ref, hh_ref,
             o0, o1, o2, l0, l1, l2, cw_ref, wco_ref, wao_ref, wo_ref,
             xo_ref, o_ref, lse_ref, yc_ref, ya_ref, out_ref):
        i = pl.program_id(0)
        lv = [l0[...], l1[...], l2[...]]
        mx = jnp.maximum(jnp.maximum(lv[0], lv[1]), lv[2])
        es = [jnp.exp(l - mx) for l in lv]
        den = es[0] + es[1] + es[2]
        o = (es[0] / den) * o0[...] + (es[1] / den) * o1[...] + (es[2] / den) * o2[...]
        o_ref[...] = o
        lse_ref[...] = mx + jnp.log(den)
        _, z = _conv_z(cc_ref[...].astype(f32), ch_ref[...].astype(f32), hc_ref[...], hh_ref[...], cw_ref, i == 0)
        p = (cb_ref[...].astype(f32) * z).astype(bf16)
        yc = _dot(p, wco_ref[...])
        ya = _dot(o.astype(bf16), wao_ref[...])
        yc_ref[...] = yc.astype(bf16)
        ya_ref[...] = ya.astype(bf16)
        merged = _sigmoid(gc_ref[...].astype(f32)) * yc + _sigmoid(ga_ref[...].astype(f32)) * ya
        out = _dot(merged.astype(bf16), wo_ref[...])
        out_ref[...] = out.astype(bf16)
        xo_ref[...] = x_ref[...] + mod_ref[5:6, :] * out

    row = pl.BlockSpec((tm, D), lambda i: (i, 0))
    att = pl.BlockSpec((tm, 512), lambda i: (i, 0))
    full = lambda shp: pl.BlockSpec(shp, lambda i: (0, 0))
    return pl.pallas_call(
        body, name="mix_out_fwd", grid=(S // tm,),
        out_shape=[SDS((S, D), f32), SDS((S, 512), f32), SDS((S, 512), f32), SDS((S, D), bf16), SDS((S, D), bf16), SDS((S, D), bf16)],
        in_specs=[row, full((9, D))] + _rest_specs(tm, False) + [att] * 6 + [full((3, D)), full((D, D)), full((512, D)), full((D, D))],
        out_specs=[row, att, att, row, row, row],
        compiler_params=_cp("arbitrary"),
    )(x, mod9, *([rest] * 7), *ogs, *lgs, cw, wco, wao, wo)


def _mix_out_bwd(dxo, mod9, outv, yc, ya, rest, o, cw, wco, wao, wo):
    S = dxo.shape[0]
    tm = TMX
    ni = S // tm

    def body(dxo_ref, mod_ref, out_ref, yc_ref, ya_ref, cb_ref, cc_ref, ch_ref, gc_ref, ga_ref, hc_ref, hh_ref,
             o_ref, cw_ref, wco_ref, wao_ref, wo_ref,
             dp_ref, dg2_ref, do_ref, dwco_ref, dwao_ref, dwo_ref, sm_ref, aco, aao, ao):
        i = pl.program_id(0)

        @pl.when(i == 0)
        def _():
            sm_ref[...] = jnp.zeros_like(sm_ref)
            aco[...] = jnp.zeros_like(aco)
            aao[...] = jnp.zeros_like(aao)
            ao[...] = jnp.zeros_like(ao)

        dxo_v = dxo_ref[...]
        sm_ref[2:3, :] += jnp.sum(out_ref[...].astype(f32) * dxo_v, axis=0, keepdims=True)
        dout = (mod_ref[5:6, :] * dxo_v).astype(bf16)
        dmerged = _dot_nt(dout, wo_ref[...])
        sc, sa = _sigmoid(gc_ref[...].astype(f32)), _sigmoid(ga_ref[...].astype(f32))
        ycv, yav = yc_ref[...].astype(f32), ya_ref[...].astype(f32)
        ao[...] += _dot_tn((sc * ycv + sa * yav).astype(bf16), dout)
        dyc = (dmerged * sc).astype(bf16)
        dya = (dmerged * sa).astype(bf16)
        dg2_ref[:, :D] = (dmerged * ycv * sc * (1.0 - sc)).astype(bf16)
        dg2_ref[:, D:] = (dmerged * yav * sa * (1.0 - sa)).astype(bf16)
        dp_ref[...] = _dot_nt(dyc, wco_ref[...]).astype(bf16)
        _, z = _conv_z(cc_ref[...].astype(f32), ch_ref[...].astype(f32), hc_ref[...], hh_ref[...], cw_ref, i == 0)
        aco[...] += _dot_tn((cb_ref[...].astype(f32) * z).astype(bf16), dyc)
        do_ref[...] = _dot_nt(dya, wao_ref[...])
        aao[...] += _dot_tn(o_ref[...].astype(bf16), dya)

        @pl.when(i == ni - 1)
        def _():
            dwco_ref[...] = aco[...].astype(bf16)
            dwao_ref[...] = aao[...].astype(bf16)
            dwo_ref[...] = ao[...].astype(bf16)

    row = pl.BlockSpec((tm, D), lambda i: (i, 0))
    att = pl.BlockSpec((tm, 512), lambda i: (i, 0))
    full = lambda shp: pl.BlockSpec(shp, lambda i: (0, 0))
    return pl.pallas_call(
        body, name="mix_out_bwd", grid=(ni,),
        out_shape=[SDS((S, D), bf16), SDS((S, 2 * D), bf16), SDS((S, 512), f32),
                   SDS((D, D), bf16), SDS((512, D), bf16), SDS((D, D), bf16), SDS((8, D), f32)],
        in_specs=[row, full((9, D)), row, row, row] + _rest_specs(tm, False) + [att, full((3, D)), full((D, D)), full((512, D)), full((D, D))],
        out_specs=[row, pl.BlockSpec((tm, 2 * D), lambda i: (i, 0)), att, full((D, D)), full((512, D)), full((D, D)), full((8, D))],
        scratch_shapes=[pltpu.VMEM((D, D), f32), pltpu.VMEM((512, D), f32), pltpu.VMEM((D, D), f32)],
        compiler_params=_cp("arbitrary"),
    )(dxo, mod9, outv, yc, ya, *([rest] * 7), o, cw, wco, wao, wo)


def _conv_bwd(dp, rest, cw):
    S = dp.shape[0]
    tm = TM
    per = tm // HALO
    nh = S // HALO
    ni = S // tm

    def body(dp_ref, dpn_ref, cb_ref, cbn_ref, cc_ref, ch_ref, hc_ref, hh_ref, cw_ref, d3_ref, sm_ref):
        i = pl.program_id(0)

        @pl.when(i == 0)
        def _():
            sm_ref[...] = jnp.zeros_like(sm_ref)

        cc, ch = cc_ref[...].astype(f32), ch_ref[...].astype(f32)
        T, z = _conv_z(cc, ch, hc_ref[...], hh_ref[...], cw_ref, i == 0)
        dpv = dp_ref[...].astype(f32)
        cbv = cb_ref[...].astype(f32)
        dz = dpv * cbv
        dzn = jnp.where(i == ni - 1, 0.0, dpn_ref[...].astype(f32) * cbn_ref[...].astype(f32))
        E = jnp.concatenate([dz, dzn], axis=0)
        ne = tm + HALO
        dT = cw_ref[2:3, :] * E + cw_ref[1:2, :] * pltpu.roll(E, ne - 1, 0) + cw_ref[0:1, :] * pltpu.roll(E, ne - 2, 0)
        dT = dT[:tm]
        d3_ref[:, :D] = (dpv * z).astype(bf16)
        d3_ref[:, D:2 * D] = (dT * ch).astype(bf16)
        d3_ref[:, 2 * D:] = (dT * cc).astype(bf16)
        sm_ref[2:3, :] += jnp.sum(dz * T[HALO:], axis=0, keepdims=True)
        sm_ref[1:2, :] += jnp.sum(dz * pltpu.roll(T, 1, 0)[HALO:], axis=0, keepdims=True)
        sm_ref[0:1, :] += jnp.sum(dz * pltpu.roll(T, 2, 0)[HALO:], axis=0, keepdims=True)

    row = pl.BlockSpec((tm, D), lambda i: (i, 0))
    nxt = pl.BlockSpec((HALO, D), lambda i: (jnp.minimum((i + 1) * per, nh - 1), 0))
    col = lambda k: pl.BlockSpec((tm, D), lambda i: (i, k))
    prv = lambda k: pl.BlockSpec((HALO, D), lambda i: (jnp.maximum(i * per - 1, 0), k))
    return pl.pallas_call(
        body, name="conv_bwd", grid=(ni,),
        out_shape=[SDS((S, 3 * D), bf16), SDS((8, D), f32)],
        in_specs=[row, nxt, col(0), nxt, col(1), col(2), prv(1), prv(2), pl.BlockSpec((3, D), lambda i: (0, 0))],
        out_specs=[pl.BlockSpec((tm, 3 * D), lambda i: (i, 0)), pl.BlockSpec((8, D), lambda i: (0, 0))],
        compiler_params=_cp("arbitrary"),
    )(dp, dp, rest, rest, rest, rest, rest, rest, cw)


_DU_RANGES = ((0, 3), (3, 6), (6, 9), (9, 15), (15, 19))
N_CBLK = IN_W // CB


def _mix_in_bwd_dh(dxo, x, mod9, g3, dus, win):
    S = x.shape[0]

    def body(dxo_ref, x_ref, mod_ref, g_ref, s0, s1, s2, s3, s4, w_ref, dxi_ref, sm_ref, acc):
        i, kb = pl.program_id(0), pl.program_id(1)

        @pl.when((i == 0) & (kb == 0))
        def _():
            sm_ref[...] = jnp.zeros_like(sm_ref)

        @pl.when(kb == 0)
        def _():
            acc[...] = jnp.zeros_like(acc)

        for src, (lo, hi) in zip((s0, s1, s2, s3, s4), _DU_RANGES):
            @pl.when((kb >= lo) & (kb < hi))
            def _(src=src):
                acc[...] += _dot_nt(src[...].astype(bf16), w_ref[...])

        @pl.when(kb == N_CBLK - 1)
        def _():
            g, scale = g_ref[1:2, :], mod_ref[4:5, :]
            _, xhat, rstd = _norm_fwd(x_ref[...], g, mod_ref[3:4, :], scale)
            dx, dshift, dscale, dg = _norm_bwd(acc[...], xhat, rstd, g, scale)
            dxi_ref[...] = dxo_ref[...] + dx
            sm_ref[0:1, :] += dshift
            sm_ref[1:2, :] += dscale
            sm_ref[3:4, :] += dg

    row = pl.BlockSpec((TM, D), lambda i, kb: (i, 0))

    def src_spec(lo, hi):
        return pl.BlockSpec((TM, CB), lambda i, kb: (i, jnp.clip(kb - lo, 0, hi - lo - 1)))

    return pl.pallas_call(
        body, name="mix_in_bwd_dh", grid=(S // TM, N_CBLK),
        out_shape=[SDS((S, D), f32), SDS((8, D), f32)],
        in_specs=[row, row, pl.BlockSpec((9, D), lambda i, kb: (0, 0)), pl.BlockSpec((3, D), lambda i, kb: (0, 0))]
                 + [src_spec(lo, hi) for lo, hi in _DU_RANGES] + [pl.BlockSpec((D, CB), lambda i, kb: (0, kb))],
        out_specs=[row, pl.BlockSpec((8, D), lambda i, kb: (0, 0))],
        scratch_shapes=[pltpu.VMEM((TM, D), f32)],
        compiler_params=_cp("arbitrary", "arbitrary"),
    )(dxo, x, mod9, g3, *dus, win)


def _mix_in_bwd_dw(h, dus):
    S = h.shape[0]
    ni = S // TMW

    def body(h_ref, s0, s1, s2, s3, s4, dw_ref, acc):
        kb, i = pl.program_id(0), pl.program_id(1)

        @pl.when(i == 0)
        def _():
            acc[...] = jnp.zeros_like(acc)

        for src, (lo, hi) in zip((s0, s1, s2, s3, s4), _DU_RANGES):
            @pl.when((kb >= lo) & (kb < hi))
            def _(src=src):
                acc[...] += _dot_tn(h_ref[...], src[...].astype(bf16))

        @pl.when(i == ni - 1)
        def _():
            dw_ref[...] = acc[...].astype(bf16)

    def src_spec(lo, hi):
        def imap(kb, i):
            on = (kb >= lo) & (kb < hi)
            return (jnp.where(on, i, 0), jnp.clip(kb - lo, 0, hi - lo - 1))
        return pl.BlockSpec((TMW, CB), imap)

    return pl.pallas_call(
        body, name="mix_in_bwd_dw", grid=(N_CBLK, ni),
        out_shape=SDS((D, IN_W), bf16),
        in_specs=[pl.BlockSpec((TMW, D), lambda kb, i: (i, 0))] + [src_spec(lo, hi) for lo, hi in _DU_RANGES],
        out_specs=pl.BlockSpec((D, CB), lambda kb, i: (0, kb)),
        scratch_shapes=[pltpu.VMEM((D, CB), f32)],
        compiler_params=_cp("arbitrary", "arbitrary"),
    )(h, *dus)


def _loss_head(x, fg, tgt):
    S = x.shape[0]

    def body(x_ref, g_ref, t_ref, ls_ref, dx_ref, sm_ref):
        i = pl.program_id(0)

        @pl.when(i == 0)
        def _():
            ls_ref[...] = jnp.zeros_like(ls_ref)
            sm_ref[...] = jnp.zeros_like(sm_ref)

        xv, g = x_ref[...], g_ref[...]
        rstd = lax.rsqrt(jnp.mean(xv * xv, axis=-1, keepdims=True) + EPS)
        xhat = xv * rstd
        e = xhat * g - t_ref[...]
        ls_ref[...] += 0.5 * jnp.sum(jnp.mean(e * e, axis=-1, keepdims=True))
        dy = e * (1.0 / D)
        sm_ref[0:1, :] += jnp.sum(dy * xhat, axis=0, keepdims=True)
        dxh = dy * g
        dx_ref[...] = rstd * (dxh - xhat * jnp.mean(dxh * xhat, axis=-1, keepdims=True))

    row = pl.BlockSpec((TM, D), lambda i: (i, 0))
    return pl.pallas_call(
        body, name="loss_head", grid=(S // TM,),
        out_shape=[SDS((8, 128), f32), SDS((S, D), f32), SDS((8, D), f32)],
        in_specs=[row, pl.BlockSpec((1, D), lambda i: (0, 0)), row],
        out_specs=[pl.BlockSpec((8, 128), lambda i: (0, 0)), row, pl.BlockSpec((8, D), lambda i: (0, 0))],
        compiler_params=_cp("arbitrary"),
    )(x, fg, tgt)


def _adam(w, g, m, v):
    m2 = B1 * m + (1.0 - B1) * g
    v2 = B2 * v + (1.0 - B2) * (g * g)
    delta = -LR * ((m2 / BC1) / (jnp.sqrt(v2 / BC2) + AEPS) + WD * w)
    return delta, m2, v2


def _row_tile(rows, cols):
    for tr in (512, 352, 256, 128, 64):
        if rows % tr == 0 and tr * cols * 4 <= (5 << 18):
            return tr
    raise ValueError((rows, cols))


def _sum_slots(land):
    _, R, C = land.shape
    tr = _row_tile(R, C)

    def body(l_ref, t_ref):
        t = l_ref[0].astype(f32)
        for k in range(1, N_CHIPS):
            t = t + l_ref[k].astype(f32)
        t_ref[...] = t

    return pl.pallas_call(
        body, name="sum_slots", grid=(R // tr,),
        out_shape=SDS((R, C), f32),
        in_specs=[pl.BlockSpec((N_CHIPS, tr, C), lambda i: (0, i, 0))],
        out_specs=pl.BlockSpec((tr, C), lambda i: (i, 0)),
        compiler_params=_cp("arbitrary"),
    )(land)


def _adamw_pair(w2, m2, v2, ta, tb, outs, slot):
    R, C = ta.shape
    tr = _row_tile(R, C)
    nrt = R // tr

    def body(w_ref, m_ref, v_ref, ta_ref, tb_ref, g_in, d_in, m_in, v_in, g_ref, d_ref, mo_ref, vo_ref):
        g = ta_ref[...] + tb_ref[...]
        delta, mn, vn = _adam(w_ref[...], g, m_ref[...], v_ref[...])
        g_ref[...] = g
        d_ref[...] = delta
        mo_ref[...] = mn
        vo_ref[...] = vn

    big = pl.BlockSpec((tr, C), lambda i: (slot * nrt + i, 0))
    loc = pl.BlockSpec((tr, C), lambda i: (i, 0))
    anyspec = pl.BlockSpec(memory_space=pl.ANY)
    return pl.pallas_call(
        body, name="adamw_pair", grid=(nrt,),
        out_shape=[SDS(o.shape, f32) for o in outs],
        in_specs=[big, big, big, loc, loc] + [anyspec] * 4,
        out_specs=[big] * 4,
        input_output_aliases={5: 0, 6: 1, 7: 2, 8: 3},
        compiler_params=_cp("arbitrary"),
    )(w2, m2, v2, ta, tb, *outs)


def _adamw_small(w, g, m, v):
    def body(w_ref, g_ref, m_ref, v_ref, d_ref, mo_ref, vo_ref):
        delta, mn, vn = _adam(w_ref[...], g_ref[...], m_ref[...], v_ref[...])
        d_ref[...] = delta
        mo_ref[...] = mn
        vo_ref[...] = vn

    return pl.pallas_call(body, name="adamw_small", out_shape=[SDS(w.shape, f32)] * 3)(w, g, m, v)


def _ada_w_update(cs_all, dmod_sh, w, m, v):
    tr = 256

    def body(cs_ref, dm_ref, w_ref, m_ref, v_ref, g_ref, d_ref, mo_ref, vo_ref):
        g = _dot_tn(cs_ref[...].astype(bf16), dm_ref[...].astype(bf16))
        delta, mn, vn = _adam(w_ref[...], g, m_ref[...], v_ref[...])
        g_ref[...] = g
        d_ref[...] = delta
        mo_ref[...] = mn
        vo_ref[...] = vn

    blk = pl.BlockSpec((None, tr, ADA_SH), lambda l, i: (l, i, 0))
    return pl.pallas_call(
        body, name="ada_w_update", grid=(DEPTH, D // tr),
        out_shape=[SDS(w.shape, f32)] * 4,
        in_specs=[pl.BlockSpec((8, tr), lambda l, i: (0, i)), pl.BlockSpec((None, 8, ADA_SH), lambda l, i: (l, 0, 0)), blk, blk, blk],
        out_specs=[blk] * 4,
        compiler_params=_cp("arbitrary", "arbitrary"),
    )(cs_all, dmod_sh, w, m, v)


def _sum_devices(gathered):
    _, R, C = gathered.shape

    def body(g_ref, o_ref):
        t = g_ref[0]
        for k in range(1, 8):
            t = t + g_ref[k]
        o_ref[...] = t

    return pl.pallas_call(body, name="sum_devices", out_shape=SDS((R, C), f32))(gathered)


def _layer_fwd(x, mod9, g3, cw, getw, bias):
    W = {}

    def take(gname, after, mod9):
        w, tok = getw(gname, after)
        W.update(w)
        return mod9 if tok is None else mod9 + tok[0, 0]

    mod9 = take("A", x, mod9)
    x1, h1, a1, u1, y1 = _ffn_fwd(x, mod9, g3, W["wg0"], W["wu0"], W["wd0"], 0)
    mod9 = take("B", x1, mod9)
    hm, qkv = _mix_qkv(x1, mod9, g3, W["win"])
    rest = _mix_rest(hm, W["win"])
    ogs, lgs = [], []
    for g in range(3):
        og, lg = _attn_fwd(qkv, bias, g)
        ogs.append(og)
        lgs.append(lg)
    mod9 = take("C", ogs[2], mod9)
    x2, o, lse, yc, ya, outv = _mix_out_fwd(x1, mod9, rest, ogs, lgs, cw, W["wco"], W["wao"], W["wo"])
    mod9 = take("D", x2, mod9)
    x3, h3, a3, u3, y3 = _ffn_fwd(x2, mod9, g3, W["wg1"], W["wu1"], W["wd1"], 2)
    saved = dict(x0=x, x1=x1, x2=x2, h1=h1, a1=a1, u1=u1, y1=y1, hm=hm, qkv=qkv, rest=rest, o=o, lse=lse, yc=yc, ya=ya,
                 outv=outv, h3=h3, a3=a3, u3=u3, y3=y3)
    return x3, saved, W


def _layer_bwd(dx, sv, mod9, g3, cw, W, bias, emit):
    S = dx.shape[0]
    dw = {}

    def send(gname, mod9):
        tok = emit(gname, dw)
        return mod9 if tok is None else mod9 + tok[0, 0]

    dx2, da, du, hid, dy, sm3 = _ffn_bwd1(dx, sv["x2"], mod9, g3, sv["y3"], sv["a3"], sv["u3"], W["wg1"], W["wu1"], W["wd1"], 2)
    dw["wg1"], dw["wu1"], dw["wd1"] = _ffn_bwd2(sv["h3"], da, du, hid, dy)
    mod9 = send("D", mod9)
    dp, dg2, do, dw["wco"], dw["wao"], dw["wo"], smo = _mix_out_bwd(
        dx2, mod9, sv["outv"], sv["yc"], sv["ya"], sv["rest"], sv["o"], cw, W["wco"], W["wao"], W["wo"])
    mod9_c = send("C", mod9)
    cw = cw + (mod9_c - mod9)[0:1, :]
    mod9 = mod9_c
    d3, smc = _conv_bwd(dp, sv["rest"], cw)
    dq = lax.empty((S, 1536), f32)
    dk = lax.empty((S, 1536), f32)
    dv = lax.empty((S, 1536), f32)
    dsaccs = []
    for g in range(3):
        dq, dk, dv, dsg = _attn_bwd(sv["qkv"], do, sv["o"], sv["lse"], bias, dq, dk, dv, g)
        dsaccs.append(dsg)
    dus = (dq, dk, dv, d3, dg2)
    dx1, smm = _mix_in_bwd_dh(dx2, sv["x1"], mod9, g3, dus, W["win"])
    dw["win"] = _mix_in_bwd_dw(sv["hm"], dus)
    mod9 = send("B", mod9)
    dx0, da, du, hid, dy, sm1 = _ffn_bwd1(dx1, sv["x0"], mod9, g3, sv["y1"], sv["a1"], sv["u1"], W["wg0"], W["wu0"], W["wd0"], 0)
    dw["wg0"], dw["wu0"], dw["wd0"] = _ffn_bwd2(sv["h1"], da, du, hid, dy)
    send("A", mod9)
    dmod = jnp.concatenate([sm1[0:3], smm[0:2], smo[2:3], sm3[0:3]], axis=0)
    dng = jnp.concatenate([sm1[3:4], smm[3:4], sm3[3:4]], axis=0)
    return dx0, dmod, dng, smc[0:3], jnp.concatenate(dsaccs, axis=0)


def _chip_cols(a, chip, width):
    return lax.dynamic_slice_in_dim(a, chip * width, width, axis=a.ndim - 1)


def kernel(x, c, ada_w, ada_b, norm_g, ffn_w_gate, ffn_w_up, ffn_w_down, w_in, conv_w, w_conv_out, w_attn_out, w_o, rel_bias, final_g, loss_target, m_ada_w, m_ada_b, m_norm_g, m_ffn_w_gate, m_ffn_w_up, m_ffn_w_down, m_w_in, m_conv_w, m_w_conv_out, m_w_attn_out, m_w_o, m_rel_bias, m_final_g, v_ada_w, v_ada_b, v_norm_g, v_ffn_w_gate, v_ffn_w_up, v_ffn_w_down, v_w_in, v_conv_w, v_w_conv_out, v_w_attn_out, v_w_o, v_rel_bias, v_final_g):
    ix, iy, ic = lax.axis_index("x"), lax.axis_index("y"), lax.axis_index("c")
    chip = 2 * ix + iy
    dev = 4 * ix + 2 * iy + ic
    xs = x[0]
    S = xs.shape[0]
    qd = D // N_CHIPS

    pad8 = lambda a: jnp.pad(a, ((0, -a.shape[0] % 8), (0, 0)))
    pack = jnp.concatenate([pad8(c), pad8(norm_g.reshape(3, D)), pad8(conv_w.reshape(3, D))], axis=0)
    g1 = _allgather_small(pack).reshape(8, 24, D)
    c_all = g1[:, 0]
    by_chip = g1[0::2]
    ng_full = jnp.concatenate([by_chip[j, 8:11].reshape(DEPTH, 3, qd) for j in range(N_CHIPS)], axis=-1)
    cw_full = jnp.concatenate([by_chip[j, 16:19].reshape(DEPTH, 3, qd) for j in range(N_CHIPS)], axis=-1)
    mod_sh, cs_all = _mod_shards(c_all, ada_w, _chip_cols(ada_b, chip, ADA_SH))
    g2 = _allgather_small(mod_sh.reshape(DEPTH * 8, ADA_SH)).reshape(8, DEPTH, 8, ADA_SH)
    mine = lax.dynamic_index_in_dim(g2[0::2], dev, axis=2, keepdims=False)
    mod = jnp.transpose(mine, (1, 0, 2)).reshape(DEPTH, 9, D)

    buckets = jnp.asarray(_bucket_table())
    bias = _bias_blocks(rel_bias, buckets)

    chip_arr = jnp.reshape(chip, (1,)).astype(jnp.int32)
    names = [w[0] for w in WCLASSES]

    def layer_shards(l):
        return [ffn_w_gate[l, 0], ffn_w_up[l, 0], ffn_w_down[l, 0], ffn_w_gate[l, 1], ffn_w_up[l, 1], ffn_w_down[l, 1],
                w_in[l], w_conv_out[l], w_attn_out[l], w_o[l]]

    started = {}
    extra_starts = {(0, "A"): [(0, "B")], (0, "B"): [(0, "C"), (0, "D"), (1, "A")]}

    def start_gather(l, gname, after):
        shards = layer_shards(l)
        started[(l, gname)] = _gather_group_start(f"l{l}{gname}", GROUPS[gname], [shards[q] for q in GROUPS[gname]], chip_arr, after)
        return started[(l, gname)][-1]

    def make_getw(l):
        def getw(gname, after):
            full = _gather_group_wait(f"l{l}{gname}", GROUPS[gname], started[(l, gname)], after)
            tok = None
            for nl, ng in extra_starts.get((l, gname), []) + [(l + 1, gname)]:
                if nl < DEPTH and (nl, ng) not in started:
                    tok = start_gather(nl, ng, full[0] if tok is None else tok)
            return {names[q]: f for q, f in zip(GROUPS[gname], full)}, tok
        return getw

    Ws, saves = [], []
    xc = xs
    start_gather(0, "A", mod)
    for l in range(DEPTH):
        xc, sv, W = _layer_fwd(xc, mod[l], ng_full[l], cw_full[l], make_getw(l), bias)
        Ws.append(W)
        saves.append(sv)

    ls, dx, smf = _loss_head(xc, final_g.reshape(1, D), loss_target[0])
    loss = lax.psum(ls[0, 0], ("x", "y", "c"))

    params = dict(wg=ffn_w_gate, wu=ffn_w_up, wd=ffn_w_down, win=w_in, wco=w_conv_out, wao=w_attn_out, wo=w_o)
    moms = dict(wg=m_ffn_w_gate, wu=m_ffn_w_up, wd=m_ffn_w_down, win=m_w_in, wco=m_w_conv_out, wao=m_w_attn_out, wo=m_w_o)
    vars_ = dict(wg=v_ffn_w_gate, wu=v_ffn_w_up, wd=v_ffn_w_down, win=v_w_in, wco=v_w_conv_out, wao=v_w_attn_out, wo=v_w_o)
    flat = lambda a: a.reshape(-1, a.shape[-1])
    big_out = {k: [lax.empty(flat(p).shape, f32) for _ in range(4)] for k, p in params.items()}
    dmods, dngs, dcws, dsaccs = [None] * DEPTH, [None] * DEPTH, [None] * DEPTH, [None] * DEPTH

    def finish(l, gname, started, after):
        group = GROUPS[gname]
        pieces, lands = _scatter_group_wait(f"l{l}{gname}", group, started, after)
        ts = [_sum_own_slots(pieces[i], lands[i], *_cls(q), chip_arr) for i, q in enumerate(group)]
        tsib = _swap_sibling(ts)
        for i, q in enumerate(group):
            name = names[q]
            key = name.rstrip("01")
            slot = 2 * l + int(name[-1]) if name[-1] in "01" else l
            big_out[key] = _adamw_pair(flat(params[key]), flat(moms[key]), flat(vars_[key]), ts[i], tsib[i], big_out[key], slot)

    pending, tok = [], None
    for l in reversed(range(DEPTH)):
        modl = mod[l] if tok is None else mod[l] + tok[0, 0]
        mine = []

        def emit(gname, dw, l=l, mine=mine):
            prev = mine[-1][2][-1] if mine else dx
            mine.append((l, gname, _scatter_group_start(f"l{l}{gname}", GROUPS[gname], [dw[names[q]] for q in GROUPS[gname]], prev)))
            return mine[-1][2][-1]

        dx, dmods[l], dngs[l], dcws[l], dsaccs[l] = _layer_bwd(dx, saves[l], modl, ng_full[l], cw_full[l], Ws[l], bias, emit)
        for pl_, pg, pst in pending:
            finish(pl_, pg, pst, dx)
        pending, tok = mine, mine[-1][2][-1]
    for pl_, pg, pst in pending:
        finish(pl_, pg, pst, dx)

    drb = jnp.transpose(_bias_grad(dsaccs, buckets)[:, 0, :NUM_BUCKETS])
    drb_row = jnp.pad(drb.reshape(1, NUM_BUCKETS * 24), ((0, 0), (0, D - NUM_BUCKETS * 24)))
    pack2 = jnp.concatenate([pad8(a) for a in dmods] + [pad8(a) for a in dngs] + [pad8(a) for a in dcws] + [smf, pad8(drb_row)], axis=0)
    n_rows = pack2.shape[0]
    g3 = _allgather_small(pack2).reshape(8, n_rows, D)
    tot = _sum_devices(g3)
    o_ng, o_cw, o_fg, o_rb = 16 * DEPTH, 24 * DEPTH, 32 * DEPTH, 32 * DEPTH + 8
    g_ada_b = jnp.stack([tot[16 * l:16 * l + 9] for l in range(DEPTH)]).reshape(DEPTH, 9 * D)
    g_norm_g = _chip_cols(jnp.stack([tot[o_ng + 8 * l:o_ng + 8 * l + 3] for l in range(DEPTH)]), chip, qd)
    g_conv_w = _chip_cols(jnp.stack([tot[o_cw + 8 * l:o_cw + 8 * l + 3] for l in range(DEPTH)]), chip, qd)
    g_final_g = tot[o_fg]
    g_rel_bias = tot[o_rb, :NUM_BUCKETS * 24].reshape(NUM_BUCKETS, 24)
    dmod_all = jnp.stack([g3[:, 16 * l:16 * l + 9].reshape(8, 9 * D) for l in range(DEPTH)])
    dmod_sh = _chip_cols(dmod_all, chip, ADA_SH)
    g_ada_w, d_ada_w, nm_ada_w, nv_ada_w = _ada_w_update(cs_all, dmod_sh, ada_w, m_ada_w, v_ada_w)

    def small(w, g, m, v):
        shp = w.shape
        to2 = lambda a: a.reshape(-1, shp[-1])
        return [o.reshape(shp) for o in _adamw_small(to2(w), to2(g), to2(m), to2(v))]

    d_ada_b, nm_ada_b, nv_ada_b = small(ada_b, g_ada_b, m_ada_b, v_ada_b)
    d_norm_g, nm_norm_g, nv_norm_g = small(norm_g, g_norm_g, m_norm_g, v_norm_g)
    d_conv_w, nm_conv_w, nv_conv_w = small(conv_w, g_conv_w, m_conv_w, v_conv_w)
    d_rel_bias, nm_rel_bias, nv_rel_bias = small(rel_bias, g_rel_bias, m_rel_bias, v_rel_bias)
    d_final_g, nm_final_g, nv_final_g = small(final_g, g_final_g, m_final_g, v_final_g)

    def big(key, which):
        return big_out[key][which].reshape(params[key].shape)

    grads = [g_ada_w, g_ada_b, g_norm_g, big("wg", 0), big("wu", 0), big("wd", 0), big("win", 0), g_conv_w, big("wco", 0),
             big("wao", 0), big("wo", 0), g_rel_bias, g_final_g]
    deltas = [d_ada_w, d_ada_b, d_norm_g, big("wg", 1), big("wu", 1), big("wd", 1), big("win", 1), d_conv_w, big("wco", 1),
              big("wao", 1), big("wo", 1), d_rel_bias, d_final_g]
    new_m = [nm_ada_w, nm_ada_b, nm_norm_g, big("wg", 2), big("wu", 2), big("wd", 2), big("win", 2), nm_conv_w, big("wco", 2),
             big("wao", 2), big("wo", 2), nm_rel_bias, nm_final_g]
    new_v = [nv_ada_w, nv_ada_b, nv_norm_g, big("wg", 3), big("wu", 3), big("wd", 3), big("win", 3), nv_conv_w, big("wco", 3),
             big("wao", 3), big("wo", 3), nv_rel_bias, nv_final_g]
    return (loss, dx[None], *grads, *deltas, *new_m, *new_v)
```

```python
import functools

import numpy as np
import jax
import jax.numpy as jnp
from jax import lax
from jax.experimental import pallas as pl
from jax.experimental.pallas import tpu as pltpu

f32, bf16 = jnp.float32, jnp.bfloat16
SDS = jax.ShapeDtypeStruct
MESH = pl.DeviceIdType.MESH

D = 1024
DEPTH = 4
N_CHIPS = 4
FB = 704
HD = 64
QKV_W = 4608
REST_W = 5120
IN_W = QKV_W + REST_W
WIN_SH = IN_W // N_CHIPS
ADA_SH = 9 * D // N_CHIPS
BLK = 128
DILATIONS = (1, 4, 16)
Q_BLOCKS = (4, 1, 1)
NUM_BUCKETS, MAX_DISTANCE = 32, 2048
EPS = 1e-6
NEG = -1e30
SCALE = HD ** -0.5
LR, B1, B2, AEPS, WD, STEP = 0.001, 0.9, 0.999, 1e-08, 0.01, 10
BC1 = 1.0 - B1 ** STEP
BC2 = 1.0 - B2 ** STEP
VMEM_LIMIT = 56 * 1024 * 1024
TM = 512
TMW = 1024
TMP = 1024
ROWS_EW = 16
TMX = 256
HALO = 16
CB = 512


def _cp(*sem):
    return pltpu.CompilerParams(dimension_semantics=sem if sem else None, vmem_limit_bytes=VMEM_LIMIT)


def _dot(a, b):
    return jnp.dot(a, b, preferred_element_type=f32)


def _dot_nt(a, b):
    return lax.dot_general(a, b, (((1,), (1,)), ((), ())), preferred_element_type=f32)


def _dot_tn(a, b):
    return lax.dot_general(a, b, (((0,), (0,)), ((), ())), preferred_element_type=f32)


def _sigmoid(x):
    return 0.5 * jnp.tanh(0.5 * x) + 0.5


def _norm_fwd(x, g, shift, scale):
    rstd = lax.rsqrt(jnp.mean(x * x, axis=-1, keepdims=True) + EPS)
    xhat = x * rstd
    return xhat * g * (1.0 + scale) + shift, xhat, rstd


def _norm_bwd(dh, xhat, rstd, g, scale):
    dshift = jnp.sum(dh, axis=0, keepdims=True)
    dscale = jnp.sum(dh * xhat * g, axis=0, keepdims=True)
    dg = jnp.sum(dh * xhat * (1.0 + scale), axis=0, keepdims=True)
    dxh = dh * (g * (1.0 + scale))
    dx = rstd * (dxh - xhat * jnp.mean(dxh * xhat, axis=-1, keepdims=True))
    return dx, dshift, dscale, dg


def _allgather_small(xp):
    m_per, n = xp.shape

    def body(x_ref, out_ref, send_sems, recv_sems, local_sem):
        x, y, c = lax.axis_index("x"), lax.axis_index("y"), lax.axis_index("c")
        me, sibling = (x, y, c), (x, y, 1 - c)
        chips = [(1 - x, y), (x, 1 - y), (1 - x, 1 - y)]

        def rows(px, py, pc):
            return out_ref.at[pl.ds((4 * px + 2 * py + pc) * m_per, m_per), :]

        def copy(k, block, to, src=None):
            return pltpu.make_async_remote_copy(
                src_ref=rows(*block) if src is None else src, dst_ref=rows(*block),
                send_sem=send_sems.at[k], recv_sem=recv_sems.at[k], device_id=to, device_id_type=MESH)

        mine = pltpu.make_async_copy(x_ref, rows(*me), local_sem)
        mine.start()
        first = [copy(0, me, sibling, src=x_ref)]
        first += [copy(1 + j, me, (*chip, c), src=x_ref) for j, chip in enumerate(chips)]
        for cp in first:
            cp.start()
        passed = [copy(4 + j, (*chip, c), sibling) for j, chip in enumerate(chips)]
        for j, chip in enumerate(chips):
            copy(1 + j, (*chip, c), me).wait_recv()
            passed[j].start()
        copy(0, sibling, me).wait_recv()
        for j, chip in enumerate(chips):
            copy(4 + j, (*chip, 1 - c), me).wait_recv()
        for cp in first + passed:
            cp.wait_send()
        mine.wait()

    return pl.pallas_call(
        body, name="allgather_small",
        out_shape=SDS((8 * m_per, n), xp.dtype),
        in_specs=[pl.BlockSpec(memory_space=pltpu.VMEM)],
        out_specs=pl.BlockSpec(memory_space=pltpu.VMEM),
        scratch_shapes=[pltpu.SemaphoreType.DMA((7,)), pltpu.SemaphoreType.DMA((7,)), pltpu.SemaphoreType.DMA],
        compiler_params=pltpu.CompilerParams(vmem_limit_bytes=VMEM_LIMIT),
    )(xp)


WCLASSES = (
    ("wg0", "lead", (D, FB)), ("wu0", "lead", (D, FB)), ("wd0", "row", (FB, D)),
    ("wg1", "lead", (D, FB)), ("wu1", "lead", (D, FB)), ("wd1", "row", (FB, D)),
    ("win", "col", (D, WIN_SH)), ("wco", "row", (D // N_CHIPS, D)), ("wao", "col", (512, D // N_CHIPS)),
    ("wo", "row", (D // N_CHIPS, D)),
)
NCLS = len(WCLASSES)


def _full_shape(kind, shp):
    if kind == "lead":
        return (N_CHIPS,) + shp
    if kind == "row":
        return (N_CHIPS * shp[0], shp[1])
    return (shp[0], N_CHIPS * shp[1])


def _shard_view(ref, kind, shp, j):
    if kind == "lead":
        return ref.at[j]
    if kind == "row":
        return ref.at[pl.ds(j * shp[0], shp[0]), :]
    return ref.at[:, pl.ds(j * shp[1], shp[1])]


def _half(ref, shp, h):
    hr = shp[0] // 2
    return ref.at[pl.ds(pl.multiple_of(h * hr, 16), hr), :]


def _gather_weights(shards):
    n = NCLS

    def body(*refs):
        ins, outs = refs[:n], refs[n:2 * n]
        send1, recv1, send2, recv2, lsem = refs[2 * n:]
        x, y, c = lax.axis_index("x"), lax.axis_index("y"), lax.axis_index("c")
        chip = 2 * x + y
        sibling = (x, y, 1 - c)

        for mc in range(N_CHIPS):
            @pl.when(chip == mc)
            def _(mc=mc):
                local = []
                for q, (_, kind, shp) in enumerate(WCLASSES):
                    cp = pltpu.make_async_copy(ins[q], _shard_view(outs[q], kind, shp, mc), lsem.at[q])
                    cp.start()
                    local.append(cp)
                sends = []
                for k in (1, 2, 3):
                    pj = mc ^ k
                    for q, (_, kind, shp) in enumerate(WCLASSES):
                        cp = pltpu.make_async_remote_copy(
                            src_ref=_half(ins[q], shp, c), dst_ref=_half(_shard_view(outs[q], kind, shp, mc), shp, c),
                            send_sem=send1.at[q * 3 + k - 1], recv_sem=recv1.at[q * 3 + k - 1],
                            device_id=(pj >> 1, pj & 1, c), device_id_type=MESH)
                        cp.start()
                        sends.append(cp)
                for k in (1, 2, 3):
                    pj = mc ^ k
                    for q, (_, kind, shp) in enumerate(WCLASSES):
                        landed = _half(_shard_view(outs[q], kind, shp, pj), shp, c)
                        pltpu.make_async_remote_copy(
                            src_ref=landed, dst_ref=landed, send_sem=send1.at[q * 3 + k - 1], recv_sem=recv1.at[q * 3 + k - 1],
                            device_id=(pj >> 1, pj & 1, c), device_id_type=MESH).wait_recv()
                        cp = pltpu.make_async_remote_copy(
                            src_ref=landed, dst_ref=landed, send_sem=send2.at[q * 3 + k - 1], recv_sem=recv2.at[q * 3 + k - 1],
                            device_id=sibling, device_id_type=MESH)
                        cp.start()
                        sends.append(cp)
                for k in (1, 2, 3):
                    pj = mc ^ k
                    for q, (_, kind, shp) in enumerate(WCLASSES):
                        other = _half(_shard_view(outs[q], kind, shp, pj), shp, 1 - c)
                        pltpu.make_async_remote_copy(
                            src_ref=other, dst_ref=other, send_sem=send2.at[q * 3 + k - 1], recv_sem=recv2.at[q * 3 + k - 1],
                            device_id=sibling, device_id_type=MESH).wait_recv()
                for cp in sends:
                    cp.wait_send()
                for cp in local:
                    cp.wait()

    anyspec = pl.BlockSpec(memory_space=pl.ANY)
    return pl.pallas_call(
        body, name="gather_weights",
        out_shape=[SDS(_full_shape(kind, shp), bf16) for _, kind, shp in WCLASSES],
        in_specs=[anyspec] * n, out_specs=[anyspec] * n,
        scratch_shapes=[pltpu.SemaphoreType.DMA((3 * n,)), pltpu.SemaphoreType.DMA((3 * n,)),
                        pltpu.SemaphoreType.DMA((3 * n,)), pltpu.SemaphoreType.DMA((3 * n,)),
                        pltpu.SemaphoreType.DMA((n,))],
    )(*shards)


def _scatter_grads(pieces):
    n = NCLS

    def body(*refs):
        ins, outs = refs[:n], refs[n:2 * n]
        send1, recv1, lsem = refs[2 * n:]
        x, y, c = lax.axis_index("x"), lax.axis_index("y"), lax.axis_index("c")
        chip = 2 * x + y

        for mc in range(N_CHIPS):
            @pl.when(chip == mc)
            def _(mc=mc):
                local, sends = [], []
                for q, (_, kind, shp) in enumerate(WCLASSES):
                    cp = pltpu.make_async_copy(_shard_view(ins[q], kind, shp, mc), outs[q].at[0], lsem.at[q])
                    cp.start()
                    local.append(cp)
                for k in (1, 2, 3):
                    pj = mc ^ k
                    for q, (_, kind, shp) in enumerate(WCLASSES):
                        cp = pltpu.make_async_remote_copy(
                            src_ref=_shard_view(ins[q], kind, shp, pj), dst_ref=outs[q].at[k],
                            send_sem=send1.at[q * 3 + k - 1], recv_sem=recv1.at[q * 3 + k - 1],
                            device_id=(pj >> 1, pj & 1, c), device_id_type=MESH)
                        cp.start()
                        sends.append(cp)
                for cp in sends:
                    cp.wait_recv()
                for cp in sends:
                    cp.wait_send()
                for cp in local:
                    cp.wait()

    anyspec = pl.BlockSpec(memory_space=pl.ANY)
    return pl.pallas_call(
        body, name="scatter_grads",
        out_shape=[SDS((N_CHIPS,) + shp, bf16) for _, _, shp in WCLASSES],
        in_specs=[anyspec] * n, out_specs=[anyspec] * n,
        scratch_shapes=[pltpu.SemaphoreType.DMA((3 * n,)), pltpu.SemaphoreType.DMA((3 * n,)), pltpu.SemaphoreType.DMA((n,))],
    )(*pieces)


def _swap_sibling(ts):
    n = len(ts)

    def body(*refs):
        ins, outs = refs[:n], refs[n:2 * n]
        send, recv = refs[2 * n:]
        x, y, c = lax.axis_index("x"), lax.axis_index("y"), lax.axis_index("c")
        cps = []
        for q in range(n):
            cp = pltpu.make_async_remote_copy(src_ref=ins[q], dst_ref=outs[q], send_sem=send.at[q], recv_sem=recv.at[q],
                                              device_id=(x, y, 1 - c), device_id_type=MESH)
            cp.start()
            cps.append(cp)
        for cp in cps:
            cp.wait_recv()
        for cp in cps:
            cp.wait_send()

    anyspec = pl.BlockSpec(memory_space=pl.ANY)
    return pl.pallas_call(
        body, name="swap_sibling",
        out_shape=[SDS(t.shape, t.dtype) for t in ts],
        in_specs=[anyspec] * n, out_specs=[anyspec] * n,
        scratch_shapes=[pltpu.SemaphoreType.DMA((n,)), pltpu.SemaphoreType.DMA((n,))],
    )(*ts)


HBM_SPEC = pl.BlockSpec(memory_space=pltpu.HBM)
SEM_SPEC = pl.BlockSpec(memory_space=pltpu.SEMAPHORE)
ANY_SPEC = pl.BlockSpec(memory_space=pl.ANY)
EFFECT = pltpu.SideEffectType.DATAFLOW_SIDE_EFFECTING
N_COPIES = 3 * NCLS


def _in_hbm(a):
    return pltpu.with_memory_space_constraint(a, pltpu.HBM)


def _chip_index():
    return 2 * lax.axis_index("x") + lax.axis_index("y")


def _place_own(shards):
    n = NCLS

    def body(*refs):
        ins, outs, lsem = refs[:n], refs[n:2 * n], refs[2 * n]
        chip = _chip_index()
        for mc in range(N_CHIPS):
            @pl.when(chip == mc)
            def _(mc=mc):
                cps = [pltpu.make_async_copy(ins[q], _shard_view(outs[q], kind, shp, mc), lsem.at[q])
                       for q, (_, kind, shp) in enumerate(WCLASSES)]
                for cp in cps:
                    cp.start()
                for cp in cps:
                    cp.wait()

    return pl.pallas_call(
        body, name="place_own",
        out_shape=[SDS(_full_shape(kind, shp), bf16) for _, kind, shp in WCLASSES],
        in_specs=[ANY_SPEC] * n, out_specs=[ANY_SPEC] * n,
        scratch_shapes=[pltpu.SemaphoreType.DMA((n,))],
    )(*shards)


def _take_own(pieces):
    n = NCLS

    def body(*refs):
        ins, outs, lsem = refs[:n], refs[n:2 * n], refs[2 * n]
        chip = _chip_index()
        for mc in range(N_CHIPS):
            @pl.when(chip == mc)
            def _(mc=mc):
                cps = [pltpu.make_async_copy(_shard_view(ins[q], kind, shp, mc), outs[q].at[0], lsem.at[q])
                       for q, (_, kind, shp) in enumerate(WCLASSES)]
                for cp in cps:
                    cp.start()
                for cp in cps:
                    cp.wait()

    return pl.pallas_call(
        body, name="take_own",
        out_shape=[SDS((N_CHIPS,) + shp, bf16) for _, _, shp in WCLASSES],
        in_specs=[ANY_SPEC] * n, out_specs=[ANY_SPEC] * n,
        scratch_shapes=[pltpu.SemaphoreType.DMA((n,))],
    )(*pieces)


def _split_start(name, srcs, dsts, after, src_view, dst_view):
    n = NCLS

    def body(*refs):
        src, dst = refs[:n], refs[n:2 * n]
        send, recv = refs[2 * n + 1], refs[2 * n + 2]
        token = refs[-1]
        c = lax.axis_index("c")
        chip = _chip_index()
        for mc in range(N_CHIPS):
            @pl.when(chip == mc)
            def _(mc=mc):
                for k in (1, 2, 3):
                    pj = mc ^ k
                    for q in range(n):
                        pltpu.make_async_remote_copy(
                            src_ref=src_view(src[q], q, mc, pj), dst_ref=dst_view(dst[q], q, mc, k),
                            send_sem=send.at[q * 3 + k - 1], recv_sem=recv.at[q * 3 + k - 1],
                            device_id=(pj >> 1, pj & 1, c), device_id_type=MESH).start()
        token[...] = jnp.zeros_like(token)

    return pl.pallas_call(
        body, name=name,
        out_shape=(pltpu.SemaphoreType.DMA((N_COPIES,)), pltpu.SemaphoreType.DMA((N_COPIES,)),
                   *[pltpu.HBM(a.shape, a.dtype) for a in srcs], *[pltpu.HBM(a.shape, a.dtype) for a in dsts], SDS((8, 128), f32)),
        in_specs=[HBM_SPEC] * (2 * n) + [ANY_SPEC],
        out_specs=(SEM_SPEC, SEM_SPEC, *([HBM_SPEC] * (2 * n)), pl.BlockSpec(memory_space=pltpu.VMEM)),
        input_output_aliases={i: 2 + i for i in range(2 * n)},
        compiler_params=pltpu.CompilerParams(has_side_effects=EFFECT),
    )(*[_in_hbm(a) for a in srcs], *[_in_hbm(a) for a in dsts], after)


def _split_wait(name, started, after, arrival_view):
    n = NCLS
    send, recv = started[0], started[1]
    srcs, dsts = started[2:2 + n], started[2 + n:2 + 2 * n]

    def body(*refs):
        src, dst = refs[:n], refs[n:2 * n]
        send_sem, recv_sem = refs[2 * n], refs[2 * n + 1]
        x, y, c = lax.axis_index("x"), lax.axis_index("y"), lax.axis_index("c")
        for k in (1, 2, 3):
            for q in range(n):
                arrival = arrival_view(dst[q], q, k)
                cp = pltpu.make_async_remote_copy(
                    src_ref=arrival, dst_ref=arrival, send_sem=send_sem.at[q * 3 + k - 1], recv_sem=recv_sem.at[q * 3 + k - 1],
                    device_id=(x, y, 1 - c), device_id_type=MESH)
                cp.wait_send()
                cp.wait_recv()

    out = pl.pallas_call(
        body, name=name,
        out_shape=(*[pltpu.HBM(a.shape, a.dtype) for a in srcs], *[pltpu.HBM(a.shape, a.dtype) for a in dsts]),
        in_specs=[HBM_SPEC] * (2 * n) + [SEM_SPEC, SEM_SPEC, ANY_SPEC],
        out_specs=tuple([HBM_SPEC] * (2 * n)),
        input_output_aliases={i: i for i in range(2 * n)},
        compiler_params=pltpu.CompilerParams(has_side_effects=EFFECT),
    )(*srcs, *dsts, send, recv, after)
    return out[n:]


def _cls(q):
    return WCLASSES[q][1], WCLASSES[q][2]


def _gather_start(shards, after):
    fulls = _place_own(shards)
    return _split_start("gather_start", shards, fulls, after,
                        lambda ref, q, mc, pj: ref,
                        lambda ref, q, mc, k: _shard_view(ref, *_cls(q), mc))


def _gather_wait(started, after):
    return _split_wait("gather_wait", started, after, lambda ref, q, k: _shard_view(ref, *_cls(q), 0))


def _scatter_start(pieces, after):
    lands = _take_own(pieces)
    return _split_start("scatter_start", pieces, lands, after,
                        lambda ref, q, mc, pj: _shard_view(ref, *_cls(q), pj),
                        lambda ref, q, mc, k: ref.at[k])


def _scatter_wait(started, after):
    return _split_wait("scatter_wait", started, after, lambda ref, q, k: ref.at[k])


GROUPS = {"A": (0, 1, 2), "B": (6,), "C": (7, 8, 9), "D": (3, 4, 5)}


def _own_spec(kind, shp, tr):
    R, C = shp
    if kind == "lead":
        return pl.BlockSpec((None, tr, C), lambda i, chip: (chip[0], i, 0))
    if kind == "row":
        return pl.BlockSpec((tr, C), lambda i, chip: (chip[0] * (R // tr) + i, 0))
    return pl.BlockSpec((tr, C), lambda i, chip: (i, chip[0]))


def _cast_place(shards, kind, shp, chip_arr):
    n = len(shards)
    R, C = shp
    tr = _row_tile(R, C)

    def body(chip_ref, *refs):
        for q in range(n):
            refs[n + q][...] = refs[q][...].astype(bf16)

    def in_spec(lead):
        return pl.BlockSpec((None,) * len(lead) + (tr, C), lambda i, chip: (*lead, i, 0))

    return pl.pallas_call(
        body, name="cast_place",
        grid_spec=pltpu.PrefetchScalarGridSpec(
            num_scalar_prefetch=1, grid=(R // tr,),
            in_specs=[in_spec(lead) for _, lead in shards],
            out_specs=[_own_spec(kind, shp, tr)] * n),
        out_shape=[SDS(_full_shape(kind, shp), bf16)] * n,
        compiler_params=_cp("arbitrary"),
    )(chip_arr, *[a for a, _ in shards])


def _sum_own_slots(piece, land, kind, shp, chip_arr):
    R, C = shp
    tr = _row_tile(R, C)

    def body(chip_ref, p_ref, l_ref, t_ref):
        t = p_ref[...].astype(f32)
        for k in range(N_CHIPS - 1):
            t = t + l_ref[k].astype(f32)
        t_ref[...] = t

    return pl.pallas_call(
        body, name="sum_own_slots",
        grid_spec=pltpu.PrefetchScalarGridSpec(
            num_scalar_prefetch=1, grid=(R // tr,),
            in_specs=[_own_spec(kind, shp, tr), pl.BlockSpec((N_CHIPS - 1, tr, C), lambda i, chip: (0, i, 0))],
            out_specs=pl.BlockSpec((tr, C), lambda i, chip: (i, 0))),
        out_shape=SDS((R, C), f32),
        compiler_params=_cp("arbitrary"),
    )(chip_arr, piece, land)


def _xfer_start(name, arrays, ng, after, src_view, dst_view):
    na = len(arrays)

    def body(*refs):
        arr = refs[:na]
        send, recv, token = refs[na + 1], refs[na + 2], refs[-1]
        c = lax.axis_index("c")
        chip = _chip_index()
        for mc in range(N_CHIPS):
            @pl.when(chip == mc)
            def _(mc=mc):
                for k in (1, 2, 3):
                    pj = mc ^ k
                    for i in range(ng):
                        pltpu.make_async_remote_copy(
                            src_ref=src_view(arr, i, mc, pj), dst_ref=dst_view(arr, i, mc, k),
                            send_sem=send.at[i * 3 + k - 1], recv_sem=recv.at[i * 3 + k - 1],
                            device_id=(pj >> 1, pj & 1, c), device_id_type=MESH).start()
        token[...] = jnp.zeros_like(token)

    return pl.pallas_call(
        body, name=name,
        out_shape=(pltpu.SemaphoreType.DMA((3 * ng,)), pltpu.SemaphoreType.DMA((3 * ng,)),
                   *[pltpu.HBM(a.shape, a.dtype) for a in arrays], SDS((8, 128), f32)),
        in_specs=[HBM_SPEC] * na + [ANY_SPEC],
        out_specs=(SEM_SPEC, SEM_SPEC, *([HBM_SPEC] * na), pl.BlockSpec(memory_space=pltpu.VMEM)),
        input_output_aliases={i: 2 + i for i in range(na)},
        compiler_params=pltpu.CompilerParams(has_side_effects=EFFECT),
    )(*[_in_hbm(a) for a in arrays], after)


def _xfer_wait(name, started, ng, after, arrival_view):
    send, recv = started[0], started[1]
    arrays = started[2:-1]
    na = len(arrays)

    def body(*refs):
        arr = refs[:na]
        send_sem, recv_sem = refs[na], refs[na + 1]
        x, y, c = lax.axis_index("x"), lax.axis_index("y"), lax.axis_index("c")
        for k in (1, 2, 3):
            for i in range(ng):
                arrival = arrival_view(arr, i)
                cp = pltpu.make_async_remote_copy(
                    src_ref=arrival, dst_ref=arrival, send_sem=send_sem.at[i * 3 + k - 1], recv_sem=recv_sem.at[i * 3 + k - 1],
                    device_id=(x, y, 1 - c), device_id_type=MESH)
                cp.wait_send()
                cp.wait_recv()

    return pl.pallas_call(
        body, name=name,
        out_shape=tuple(pltpu.HBM(a.shape, a.dtype) for a in arrays),
        in_specs=[HBM_SPEC] * na + [SEM_SPEC, SEM_SPEC, ANY_SPEC],
        out_specs=tuple([HBM_SPEC] * na),
        input_output_aliases={i: i for i in range(na)},
        compiler_params=pltpu.CompilerParams(has_side_effects=EFFECT),
    )(*arrays, send, recv, after)


def _gather_group_start(tag, group, shards_f32, chip_arr, after):
    fulls = [None] * len(group)
    by_shape = {}
    for i, q in enumerate(group):
        by_shape.setdefault(_cls(q), []).append(i)
    for (kind, shp), idx in by_shape.items():
        for i, f in zip(idx, _cast_place([shards_f32[i] for i in idx], kind, shp, chip_arr)):
            fulls[i] = f
    view = lambda arr, i, mc, _: _shard_view(arr[i], *_cls(group[i]), mc)
    return _xfer_start("gather_start_" + tag, fulls, len(group), after, view, view)


def _gather_group_wait(tag, group, started, after):
    return _xfer_wait("gather_wait_" + tag, started, len(group), after, lambda arr, i: _shard_view(arr[i], *_cls(group[i]), 0))


def _scatter_group_start(tag, group, pieces, after):
    ng = len(group)
    lands = [lax.empty((N_CHIPS - 1,) + _cls(q)[1], bf16) for q in group]
    return _xfer_start("scatter_start_" + tag, list(pieces) + lands, ng, after,
                       lambda arr, i, mc, pj: _shard_view(arr[i], *_cls(group[i]), pj),
                       lambda arr, i, mc, k: arr[ng + i].at[k - 1])


def _scatter_group_wait(tag, group, started, after):
    ng = len(group)
    out = _xfer_wait("scatter_wait_" + tag, started, ng, after, lambda arr, i: arr[ng + i].at[0])
    return out[:ng], out[ng:]


def _mod_shards(c_all, ada_w, ada_b_sh):
    tn = ADA_SH // 3

    def body(c_ref, w_ref, b_ref, o_ref, cs_ref):
        cv = c_ref[...]
        cs = cv * _sigmoid(cv)
        cs_ref[...] = cs
        o_ref[...] = _dot(cs.astype(bf16), w_ref[...].astype(bf16)) + b_ref[...]

    return pl.pallas_call(
        body, name="mod_shards", grid=(DEPTH, 3),
        out_shape=[SDS((DEPTH, 8, ADA_SH), f32), SDS((8, D), f32)],
        in_specs=[pl.BlockSpec((8, D), lambda l, t: (0, 0)),
                  pl.BlockSpec((None, D, tn), lambda l, t: (l, 0, t)),
                  pl.BlockSpec((None, 1, tn), lambda l, t: (l, 0, t))],
        out_specs=[pl.BlockSpec((None, 8, tn), lambda l, t: (l, 0, t)), pl.BlockSpec((8, D), lambda l, t: (0, 0))],
        compiler_params=_cp("arbitrary", "arbitrary"),
    )(c_all, ada_w, ada_b_sh.reshape(DEPTH, 1, ADA_SH))


def _t5_bucket(dist):
    exact = NUM_BUCKETS // 2
    dd = np.maximum(dist, 1).astype(np.float32)
    large = exact + (np.log(dd / exact) / np.log(MAX_DISTANCE / exact) * (NUM_BUCKETS - exact)).astype(np.int32)
    large = np.minimum(large, NUM_BUCKETS - 1)
    return np.where(dist < exact, dist, large).astype(np.int32)


def _bucket_table():
    i = np.arange(BLK)[:, None]
    j = np.arange(2 * BLK)[None, :]
    rel = i - j + BLK
    return np.stack([_t5_bucket(np.maximum(rel, 0) * d) for d in DILATIONS]).astype(np.int32)


def _band():
    rel = lax.broadcasted_iota(jnp.int32, (BLK, 2 * BLK), 0) - lax.broadcasted_iota(jnp.int32, (BLK, 2 * BLK), 1) + BLK
    return (rel >= 0) & (rel <= BLK)


def _bias_blocks(rel_bias, buckets):
    def body(tab_ref, bk_ref, o_ref):
        h = pl.program_id(0)
        bk = bk_ref[...]
        acc = jnp.zeros((BLK, 2 * BLK), f32)
        for b in range(NUM_BUCKETS):
            acc = jnp.where(bk == b, tab_ref[b, h], acc)
        o_ref[...] = jnp.where(_band(), acc, NEG)

    return pl.pallas_call(
        body, name="bias_blocks", grid=(24,),
        out_shape=SDS((24, BLK, 2 * BLK), f32),
        in_specs=[pl.BlockSpec(memory_space=pltpu.SMEM), pl.BlockSpec((None, BLK, 2 * BLK), lambda h: (h // 8, 0, 0))],
        out_specs=pl.BlockSpec((None, BLK, 2 * BLK), lambda h: (h, 0, 0)),
        compiler_params=_cp("arbitrary"),
    )(rel_bias, buckets)


def _bias_grad(dsaccs, buckets):
    nl = len(dsaccs)

    def body(*refs):
        bk = refs[nl][...]
        tot = refs[0][...]
        for r in refs[1:nl]:
            tot = tot + r[...]
        lane = lax.broadcasted_iota(jnp.int32, (1, 128), 1)
        row = jnp.zeros((1, 128), f32)
        for b in range(NUM_BUCKETS):
            row = jnp.where(lane == b, jnp.sum(jnp.where(bk == b, tot, 0.0)), row)
        refs[nl + 1][...] = row

    return pl.pallas_call(
        body, name="bias_grad", grid=(24,),
        out_shape=SDS((24, 1, 128), f32),
        in_specs=[pl.BlockSpec((None, BLK, 2 * BLK), lambda h: (h, 0, 0))] * nl
                 + [pl.BlockSpec((None, BLK, 2 * BLK), lambda h: (h // 8, 0, 0))],
        out_specs=pl.BlockSpec((None, 1, 128), lambda h: (h, 0, 0)),
        compiler_params=_cp("arbitrary"),
    )(*dsaccs, buckets)


def _ffn_fwd(x, mod9, g3, wg, wu, wd, sub):
    S = x.shape[0]

    def body(x_ref, mod_ref, g_ref, wg_ref, wu_ref, wd_ref, xo_ref, h_ref, a_ref, u_ref, y_ref, acc):
        j = pl.program_id(1)

        @pl.when(j == 0)
        def _():
            h, _, _ = _norm_fwd(x_ref[...], g_ref[sub:sub + 1, :], mod_ref[3 * sub:3 * sub + 1, :], mod_ref[3 * sub + 1:3 * sub + 2, :])
            h_ref[...] = h.astype(bf16)
            acc[...] = jnp.zeros_like(acc)

        h = h_ref[...]
        a = _dot(h, wg_ref[...])
        u = _dot(h, wu_ref[...])
        a_ref[...] = a.astype(bf16)
        u_ref[...] = u.astype(bf16)
        hid = (a * _sigmoid(a) * u).astype(bf16)
        acc[...] += _dot(hid, wd_ref[...])

        @pl.when(j == N_CHIPS - 1)
        def _():
            y = acc[...]
            y_ref[...] = y.astype(bf16)
            xo_ref[...] = x_ref[...] + 0.5 * mod_ref[3 * sub + 2:3 * sub + 3, :] * y

    row = pl.BlockSpec((TM, D), lambda i, j: (i, 0))
    return pl.pallas_call(
        body, name="ffn_fwd", grid=(S // TM, N_CHIPS),
        out_shape=[SDS((S, D), f32), SDS((S, D), bf16), SDS((N_CHIPS, S, FB), bf16), SDS((N_CHIPS, S, FB), bf16), SDS((S, D), bf16)],
        in_specs=[row, pl.BlockSpec((9, D), lambda i, j: (0, 0)), pl.BlockSpec((3, D), lambda i, j: (0, 0)),
                  pl.BlockSpec((None, D, FB), lambda i, j: (j, 0, 0)), pl.BlockSpec((None, D, FB), lambda i, j: (j, 0, 0)),
                  pl.BlockSpec((FB, D), lambda i, j: (j, 0))],
        out_specs=[row, row, pl.BlockSpec((None, TM, FB), lambda i, j: (j, i, 0)), pl.BlockSpec((None, TM, FB), lambda i, j: (j, i, 0)), row],
        scratch_shapes=[pltpu.VMEM((TM, D), f32)],
        compiler_params=_cp("arbitrary", "arbitrary"),
    )(x, mod9, g3, wg, wu, wd)


def _ffn_bwd1(dxo, x, mod9, g3, y, a, u, wg, wu, wd, sub):
    S = x.shape[0]

    def body(dxo_ref, x_ref, mod_ref, g_ref, y_ref, a_ref, u_ref, wg_ref, wu_ref, wd_ref,
             dxi_ref, da_ref, du_ref, hid_ref, dy_ref, sm_ref, acc, dh_s):
        i, j = pl.program_id(0), pl.program_id(1)
        gate = mod_ref[3 * sub + 2:3 * sub + 3, :]

        @pl.when((i == 0) & (j == 0))
        def _():
            sm_ref[...] = jnp.zeros_like(sm_ref)

        @pl.when(j == 0)
        def _():
            dxo_v = dxo_ref[...]
            dy_ref[...] = (0.5 * gate * dxo_v).astype(bf16)
            sm_ref[2:3, :] += jnp.sum(0.5 * y_ref[...].astype(f32) * dxo_v, axis=0, keepdims=True)
            acc[...] = jnp.zeros_like(acc)

        dh_s[...] = _dot_nt(dy_ref[...], wd_ref[...])

        def chunk(ci, carry):
            rows = pl.ds(pl.multiple_of(ci * ROWS_EW, ROWS_EW), ROWS_EW)
            av, uv, dhid = a_ref[rows, :].astype(f32), u_ref[rows, :].astype(f32), dh_s[rows, :]
            sg = _sigmoid(av)
            sil = av * sg
            da_ref[rows, :] = (dhid * uv * (sg * (1.0 + av * (1.0 - sg)))).astype(bf16)
            du_ref[rows, :] = (dhid * sil).astype(bf16)
            hid_ref[rows, :] = (sil * uv).astype(bf16)
            return carry

        lax.fori_loop(0, TM // ROWS_EW, chunk, 0, unroll=2)
        acc[...] += _dot_nt(da_ref[...], wg_ref[...]) + _dot_nt(du_ref[...], wu_ref[...])

        @pl.when(j == N_CHIPS - 1)
        def _():
            g = g_ref[sub:sub + 1, :]
            scale = mod_ref[3 * sub + 1:3 * sub + 2, :]
            _, xhat, rstd = _norm_fwd(x_ref[...], g, mod_ref[3 * sub:3 * sub + 1, :], scale)
            dx, dshift, dscale, dg = _norm_bwd(acc[...], xhat, rstd, g, scale)
            dxi_ref[...] = dxo_ref[...] + dx
            sm_ref[0:1, :] += dshift
            sm_ref[1:2, :] += dscale
            sm_ref[3:4, :] += dg

    row = pl.BlockSpec((TM, D), lambda i, j: (i, 0))
    hidb = pl.BlockSpec((None, TM, FB), lambda i, j: (j, i, 0))
    wcol = pl.BlockSpec((None, D, FB), lambda i, j: (j, 0, 0))
    return pl.pallas_call(
        body, name="ffn_bwd1", grid=(S // TM, N_CHIPS),
        out_shape=[SDS((S, D), f32), SDS((N_CHIPS, S, FB), bf16), SDS((N_CHIPS, S, FB), bf16), SDS((N_CHIPS, S, FB), bf16),
                   SDS((S, D), bf16), SDS((8, D), f32)],
        in_specs=[row, row, pl.BlockSpec((9, D), lambda i, j: (0, 0)), pl.BlockSpec((3, D), lambda i, j: (0, 0)), row,
                  hidb, hidb, wcol, wcol, pl.BlockSpec((FB, D), lambda i, j: (j, 0))],
        out_specs=[row, hidb, hidb, hidb, row, pl.BlockSpec((8, D), lambda i, j: (0, 0))],
        scratch_shapes=[pltpu.VMEM((TM, D), f32), pltpu.VMEM((TM, FB), f32)],
        compiler_params=_cp("arbitrary", "arbitrary"),
    )(dxo, x, mod9, g3, y, a, u, wg, wu, wd)


def _ffn_bwd2(h, da, du, hid, dy):
    S = h.shape[0]
    ni = S // TMW

    def body(h_ref, da_ref, du_ref, hid_ref, dy_ref, dwg_ref, dwu_ref, dwd_ref, ag, au, ad):
        i = pl.program_id(1)

        @pl.when(i == 0)
        def _():
            ag[...] = jnp.zeros_like(ag)
            au[...] = jnp.zeros_like(au)
            ad[...] = jnp.zeros_like(ad)

        hv = h_ref[...]
        ag[...] += _dot_tn(hv, da_ref[...])
        au[...] += _dot_tn(hv, du_ref[...])
        ad[...] += _dot_tn(hid_ref[...], dy_ref[...])

        @pl.when(i == ni - 1)
        def _():
            dwg_ref[...] = ag[...].astype(bf16)
            dwu_ref[...] = au[...].astype(bf16)
            dwd_ref[...] = ad[...].astype(bf16)

    row = pl.BlockSpec((TMW, D), lambda j, i: (i, 0))
    hidb = pl.BlockSpec((None, TMW, FB), lambda j, i: (j, i, 0))
    wcol = pl.BlockSpec((None, D, FB), lambda j, i: (j, 0, 0))
    return pl.pallas_call(
        body, name="ffn_bwd2", grid=(N_CHIPS, ni),
        out_shape=[SDS((N_CHIPS, D, FB), bf16), SDS((N_CHIPS, D, FB), bf16), SDS((N_CHIPS * FB, D), bf16)],
        in_specs=[row, hidb, hidb, hidb, row],
        out_specs=[wcol, wcol, pl.BlockSpec((FB, D), lambda j, i: (j, 0))],
        scratch_shapes=[pltpu.VMEM((D, FB), f32), pltpu.VMEM((D, FB), f32), pltpu.VMEM((FB, D), f32)],
        compiler_params=_cp("arbitrary", "arbitrary"),
    )(h, da, du, hid, dy)


def _mix_qkv(x, mod9, g3, win):
    S = x.shape[0]

    def body(x_ref, mod_ref, g_ref, w_ref, h_ref, o_ref):
        @pl.when(pl.program_id(1) == 0)
        def _():
            h, _, _ = _norm_fwd(x_ref[...], g_ref[1:2, :], mod_ref[3:4, :], mod_ref[4:5, :])
            h_ref[...] = h.astype(bf16)

        o_ref[...] = _dot(h_ref[...], w_ref[...])

    row = pl.BlockSpec((TMP, D), lambda i, j: (i, 0))
    return pl.pallas_call(
        body, name="mix_qkv", grid=(S // TMP, QKV_W // CB),
        out_shape=[SDS((S, D), bf16), SDS((S, QKV_W), f32)],
        in_specs=[row, pl.BlockSpec((9, D), lambda i, j: (0, 0)), pl.BlockSpec((3, D), lambda i, j: (0, 0)),
                  pl.BlockSpec((D, CB), lambda i, j: (0, j))],
        out_specs=[row, pl.BlockSpec((TMP, CB), lambda i, j: (i, j))],
        compiler_params=_cp("arbitrary", "arbitrary"),
    )(x, mod9, g3, win)


def _mix_rest(h, win):
    S = h.shape[0]
    off = QKV_W // CB

    def body(h_ref, w_ref, o_ref):
        o_ref[...] = _dot(h_ref[...], w_ref[...]).astype(bf16)

    return pl.pallas_call(
        body, name="mix_rest", grid=(S // TMP, REST_W // CB),
        out_shape=SDS((S, REST_W), bf16),
        in_specs=[pl.BlockSpec((TMP, D), lambda i, j: (i, 0)), pl.BlockSpec((D, CB), lambda i, j: (0, off + j))],
        out_specs=pl.BlockSpec((TMP, CB), lambda i, j: (i, j)),
        compiler_params=_cp("arbitrary", "arbitrary"),
    )(h, win)


def _attn_fwd(qkv, bias, g):
    S = qkv.shape[0]
    d = DILATIONS[g]
    nq = Q_BLOCKS[g]
    Rb = BLK * d
    R = Rb * nq
    nb = S // R
    qb, kb, vb = 4 * g, 12 + 4 * g, 24 + 4 * g

    def body(q_ref, kc_ref, kp_ref, vc_ref, vp_ref, b_ref, o_ref, l_ref):
        n = pl.program_id(1)
        col = lax.broadcasted_iota(jnp.int32, (BLK, 2 * BLK), 1)
        first = jnp.where((col < BLK) & (n == 0), NEG, 0.0)
        head0 = lax.broadcasted_iota(jnp.int32, (1, 2 * HD), 1) < HD

        def one(b, r):
            sl = pl.ds(b * Rb + r, BLK, stride=d)
            q = q_ref[sl, :]
            if b == 0:
                kp, vp = kp_ref[pl.ds(r, BLK, stride=d), :], vp_ref[pl.ds(r, BLK, stride=d), :]
            else:
                before = pl.ds((b - 1) * Rb + r, BLK, stride=d)
                kp, vp = kc_ref[before, :], vc_ref[before, :]
            kk = jnp.concatenate([kp, kc_ref[sl, :]], axis=0).astype(bf16)
            vv = jnp.concatenate([vp, vc_ref[sl, :]], axis=0).astype(bf16)
            os, ls = [], []
            for hh in range(2):
                qm = jnp.where(head0 if hh == 0 else ~head0, q, 0.0).astype(bf16)
                s = _dot_nt(qm, kk) * SCALE + b_ref[hh]
                if b == 0:
                    s = s + first
                m = jnp.max(s, axis=-1, keepdims=True)
                p = jnp.exp(s - m)
                l = jnp.sum(p, axis=-1, keepdims=True)
                os.append(_dot(p.astype(bf16), vv) / l)
                ls.append(m + jnp.log(l))
            o_ref[sl, :] = jnp.where(head0, os[0], os[1])
            l_ref[sl, :] = jnp.where(head0, ls[0], ls[1])

        for b in range(nq):
            if d == 1:
                one(b, 0)
            else:
                lax.fori_loop(0, d, lambda r, carry, b=b: (one(b, r), carry)[1], 0, unroll=4)

    def blk(cb, prev):
        if prev:
            return pl.BlockSpec((Rb, 128), lambda hp, n: (jnp.maximum(n * nq - 1, 0), cb + hp))
        return pl.BlockSpec((R, 128), lambda hp, n: (n, cb + hp))

    outb = pl.BlockSpec((R, 128), lambda hp, n: (n, hp))
    return pl.pallas_call(
        body, name=f"attn_fwd_d{d}", grid=(4, nb),
        out_shape=[SDS((S, 512), f32), SDS((S, 512), f32)],
        in_specs=[blk(qb, False), blk(kb, False), blk(kb, True), blk(vb, False), blk(vb, True),
                  pl.BlockSpec((2, BLK, 2 * BLK), lambda hp, n: (4 * g + hp, 0, 0))],
        out_specs=[outb, outb],
        compiler_params=_cp("arbitrary", "arbitrary"),
    )(qkv, qkv, qkv, qkv, qkv, bias)


def _attn_bwd(qkv, do, o, lse, bias, dq_all, dk_all, dv_all, g):
    S = qkv.shape[0]
    d = DILATIONS[g]
    nq = Q_BLOCKS[g]
    Rb = BLK * d
    R = Rb * nq
    nb = S // R
    qb, kb, vb = 4 * g, 12 + 4 * g, 24 + 4 * g

    def body(q_ref, kc_ref, kp_ref, vc_ref, vp_ref, do_ref, o_ref, l_ref, b_ref, dqi, dki, dvi,
             dq_ref, dk_ref, dv_ref, ds_ref, ck, cv, tk, tv):
        n = pl.program_id(1)
        col = lax.broadcasted_iota(jnp.int32, (BLK, 2 * BLK), 1)
        first = jnp.where((col < BLK) & (n == 0), NEG, 0.0)

        @pl.when(n == 0)
        def _():
            ck[...] = jnp.zeros_like(ck)
            cv[...] = jnp.zeros_like(cv)
            ds_ref[...] = jnp.zeros_like(ds_ref)

        @pl.when(n < nb)
        def _():
            head0 = lax.broadcasted_iota(jnp.int32, (1, 2 * HD), 1) < HD

            def one(b, r):
                sl = pl.ds(b * Rb + r, BLK, stride=d)
                before = pl.ds((max(b, 1) - 1) * Rb + r, BLK, stride=d)
                q = q_ref[sl, :]
                if b == 0:
                    kp, vp = kp_ref[pl.ds(r, BLK, stride=d), :], vp_ref[pl.ds(r, BLK, stride=d), :]
                else:
                    kp, vp = kc_ref[before, :], vc_ref[before, :]
                kk = jnp.concatenate([kp, kc_ref[sl, :]], axis=0).astype(bf16)
                vv = jnp.concatenate([vp, vc_ref[sl, :]], axis=0).astype(bf16)
                dov, lv = do_ref[sl, :], l_ref[sl, :]
                prod = dov * o_ref[sl, :]
                qb, dob = q.astype(bf16), dov.astype(bf16)
                dqs, dks, dvs = [], [], []
                for hh in range(2):
                    msk = head0 if hh == 0 else ~head0
                    qm = jnp.where(msk, q, 0.0).astype(bf16)
                    dom = jnp.where(msk, dov, 0.0).astype(bf16)
                    dsum = jnp.sum(jnp.where(msk, prod, 0.0), axis=-1, keepdims=True)
                    s = _dot_nt(qm, kk) * SCALE + b_ref[hh]
                    if b == 0:
                        s = s + first
                    p = jnp.exp(s - lv[:, HD * hh:HD * hh + 1])
                    ds = p * (_dot_nt(dom, vv) - dsum)
                    ds_ref[hh] += ds
                    dsb = ds.astype(bf16)
                    dqs.append(_dot(dsb, kk) * SCALE)
                    dks.append(_dot_tn(dsb, qb) * SCALE)
                    dvs.append(_dot_tn(p.astype(bf16), dob))
                dq_ref[sl, :] = jnp.where(head0, dqs[0], dqs[1])
                dk = jnp.where(head0, dks[0], dks[1])
                dv = jnp.where(head0, dvs[0], dvs[1])
                tk[sl, :] = dk[BLK:]
                tv[sl, :] = dv[BLK:]
                if b == 0:
                    prev_rows = pl.ds((nq - 1) * Rb + r, BLK, stride=d)
                    ck[prev_rows, :] += dk[:BLK]
                    cv[prev_rows, :] += dv[:BLK]
                else:
                    tk[before, :] += dk[:BLK]
                    tv[before, :] += dv[:BLK]

            for b in range(nq):
                if d == 1:
                    one(b, 0)
                else:
                    lax.fori_loop(0, d, lambda r, carry, b=b: (one(b, r), carry)[1], 0, unroll=2)
            dk_ref[...] = ck[...]
            dv_ref[...] = cv[...]
            ck[...] = tk[...]
            cv[...] = tv[...]

        @pl.when(n == nb)
        def _():
            dk_ref[...] = ck[...]
            dv_ref[...] = cv[...]

    last = nb - 1

    def blk(cb, prev):
        if prev:
            return pl.BlockSpec((Rb, 128), lambda hp, n: (jnp.maximum(jnp.minimum(n, last) * nq - 1, 0), cb + hp))
        return pl.BlockSpec((R, 128), lambda hp, n: (jnp.minimum(n, last), cb + hp))

    cur = pl.BlockSpec((R, 128), lambda hp, n: (jnp.minimum(n, last), hp))
    anyspec = pl.BlockSpec(memory_space=pl.ANY)
    dqo = pl.BlockSpec((R, 128), lambda hp, n: (jnp.minimum(n, last), 4 * g + hp))
    dko = pl.BlockSpec((R, 128), lambda hp, n: (jnp.maximum(n - 1, 0), 4 * g + hp))
    return pl.pallas_call(
        body, name=f"attn_bwd_d{d}", grid=(4, nb + 1),
        out_shape=[SDS((S, 1536), f32), SDS((S, 1536), f32), SDS((S, 1536), f32), SDS((8, BLK, 2 * BLK), f32)],
        in_specs=[blk(qb, False), blk(kb, False), blk(kb, True), blk(vb, False), blk(vb, True), cur, cur, cur,
                  pl.BlockSpec((2, BLK, 2 * BLK), lambda hp, n: (4 * g + hp, 0, 0)), anyspec, anyspec, anyspec],
        out_specs=[dqo, dko, dko, pl.BlockSpec((2, BLK, 2 * BLK), lambda hp, n: (hp, 0, 0))],
        scratch_shapes=[pltpu.VMEM((R, 128), f32)] * 4,
        input_output_aliases={9: 0, 10: 1, 11: 2},
        compiler_params=_cp("arbitrary", "arbitrary"),
    )(qkv, qkv, qkv, qkv, qkv, do, o, lse, bias, dq_all, dk_all, dv_all)


def _conv_z(cc, ch, hc, hh, cw_ref, first):
    halo = jnp.where(first, 0.0, hc.astype(f32) * hh.astype(f32))
    T = jnp.concatenate([halo, cc * ch], axis=0)
    z = cw_ref[2:3, :] * T + cw_ref[1:2, :] * pltpu.roll(T, 1, 0) + cw_ref[0:1, :] * pltpu.roll(T, 2, 0)
    return T, z[HALO:]


def _rest_specs(tm, with_next):
    per = tm // HALO
    specs = [pl.BlockSpec((tm, D), functools.partial(lambda i, k: (i, k), k=k)) for k in range(5)]
    specs += [pl.BlockSpec((HALO, D), functools.partial(lambda i, k: (jnp.maximum(i * per - 1, 0), k), k=k)) for k in (1, 2)]
    return specs


def _mix_out_fwd(x, mod9, rest, ogs, lgs, cw, wco, wao, wo):
    S = x.shape[0]
    tm = TMX

    def body(x_ref, mod_ref, cb_ref, cc_ref, ch_ref, gc_ref, ga_ref, hc_ref, hh_ref,
             o0, o1, o2, l0, l1, l2, cw_ref, wco_ref, wao_ref, wo_ref,
             xo_ref, o_ref, lse_ref, yc_ref, ya_ref, out_ref):
        i = pl.program_id(0)
        lv = [l0[...], l1[...], l2[...]]
        mx = jnp.maximum(jnp.maximum(lv[0], lv[1]), lv[2])
        es = [jnp.exp(l - mx) for l in lv]
        den = es[0] + es[1] + es[2]
        o = (es[0] / den) * o0[...] + (es[1] / den) * o1[...] + (es[2] / den) * o2[...]
        o_ref[...] = o
        lse_ref[...] = mx + jnp.log(den)
        _, z = _conv_z(cc_ref[...].astype(f32), ch_ref[...].astype(f32), hc_ref[...], hh_ref[...], cw_ref, i == 0)
        p = (cb_ref[...].astype(f32) * z).astype(bf16)
        yc = _dot(p, wco_ref[...])
        ya = _dot(o.astype(bf16), wao_ref[...])
        yc_ref[...] = yc.astype(bf16)
        ya_ref[...] = ya.astype(bf16)
        merged = _sigmoid(gc_ref[...].astype(f32)) * yc + _sigmoid(ga_ref[...].astype(f32)) * ya
        out = _dot(merged.astype(bf16), wo_ref[...])
        out_ref[...] = out.astype(bf16)
        xo_ref[...] = x_ref[...] + mod_ref[5:6, :] * out

    row = pl.BlockSpec((tm, D), lambda i: (i, 0))
    att = pl.BlockSpec((tm, 512), lambda i: (i, 0))
    full = lambda shp: pl.BlockSpec(shp, lambda i: (0, 0))
    return pl.pallas_call(
        body, name="mix_out_fwd", grid=(S // tm,),
        out_shape=[SDS((S, D), f32), SDS((S, 512), f32), SDS((S, 512), f32), SDS((S, D), bf16), SDS((S, D), bf16), SDS((S, D), bf16)],
        in_specs=[row, full((9, D))] + _rest_specs(tm, False) + [att] * 6 + [full((3, D)), full((D, D)), full((512, D)), full((D, D))],
        out_specs=[row, att, att, row, row, row],
        compiler_params=_cp("arbitrary"),
    )(x, mod9, *([rest] * 7), *ogs, *lgs, cw, wco, wao, wo)


def _mix_out_bwd(dxo, mod9, outv, yc, ya, rest, o, cw, wco, wao, wo):
    S = dxo.shape[0]
    tm = TMX
    ni = S // tm

    def body(dxo_ref, mod_ref, out_ref, yc_ref, ya_ref, cb_ref, cc_ref, ch_ref, gc_ref, ga_ref, hc_ref, hh_ref,
             o_ref, cw_ref, wco_ref, wao_ref, wo_ref,
             dp_ref, dg2_ref, do_ref, dwco_ref, dwao_ref, dwo_ref, sm_ref, aco, aao, ao):
        i = pl.program_id(0)

        @pl.when(i == 0)
        def _():
            sm_ref[...] = jnp.zeros_like(sm_ref)
            aco[...] = jnp.zeros_like(aco)
            aao[...] = jnp.zeros_like(aao)
            ao[...] = jnp.zeros_like(ao)

        dxo_v = dxo_ref[...]
        sm_ref[2:3, :] += jnp.sum(out_ref[...].astype(f32) * dxo_v, axis=0, keepdims=True)
        dout = (mod_ref[5:6, :] * dxo_v).astype(bf16)
        dmerged = _dot_nt(dout, wo_ref[...])
        sc, sa = _sigmoid(gc_ref[...].astype(f32)), _sigmoid(ga_ref[...].astype(f32))
        ycv, yav = yc_ref[...].astype(f32), ya_ref[...].astype(f32)
        ao[...] += _dot_tn((sc * ycv + sa * yav).astype(bf16), dout)
        dyc = (dmerged * sc).astype(bf16)
        dya = (dmerged * sa).astype(bf16)
        dg2_ref[:, :D] = (dmerged * ycv * sc * (1.0 - sc)).astype(bf16)
        dg2_ref[:, D:] = (dmerged * yav * sa * (1.0 - sa)).astype(bf16)
        dp_ref[...] = _dot_nt(dyc, wco_ref[...]).astype(bf16)
        _, z = _conv_z(cc_ref[...].astype(f32), ch_ref[...].astype(f32), hc_ref[...], hh_ref[...], cw_ref, i == 0)
        aco[...] += _dot_tn((cb_ref[...].astype(f32) * z).astype(bf16), dyc)
        do_ref[...] = _dot_nt(dya, wao_ref[...])
        aao[...] += _dot_tn(o_ref[...].astype(bf16), dya)

        @pl.when(i == ni - 1)
        def _():
            dwco_ref[...] = aco[...].astype(bf16)
            dwao_ref[...] = aao[...].astype(bf16)
            dwo_ref[...] = ao[...].astype(bf16)

    row = pl.BlockSpec((tm, D), lambda i: (i, 0))
    att = pl.BlockSpec((tm, 512), lambda i: (i, 0))
    full = lambda shp: pl.BlockSpec(shp, lambda i: (0, 0))
    return pl.pallas_call(
        body, name="mix_out_bwd", grid=(ni,),
        out_shape=[SDS((S, D), bf16), SDS((S, 2 * D), bf16), SDS((S, 512), f32),
                   SDS((D, D), bf16), SDS((512, D), bf16), SDS((D, D), bf16), SDS((8, D), f32)],
        in_specs=[row, full((9, D)), row, row, row] + _rest_specs(tm, False) + [att, full((3, D)), full((D, D)), full((512, D)), full((D, D))],
        out_specs=[row, pl.BlockSpec((tm, 2 * D), lambda i: (i, 0)), att, full((D, D)), full((512, D)), full((D, D)), full((8, D))],
        scratch_shapes=[pltpu.VMEM((D, D), f32), pltpu.VMEM((512, D), f32), pltpu.VMEM((D, D), f32)],
        compiler_params=_cp("arbitrary"),
    )(dxo, mod9, outv, yc, ya, *([rest] * 7), o, cw, wco, wao, wo)


def _conv_bwd(dp, rest, cw):
    S = dp.shape[0]
    tm = TM
    per = tm // HALO
    nh = S // HALO
    ni = S // tm

    def body(dp_ref, dpn_ref, cb_ref, cbn_ref, cc_ref, ch_ref, hc_ref, hh_ref, cw_ref, d3_ref, sm_ref):
        i = pl.program_id(0)

        @pl.when(i == 0)
        def _():
            sm_ref[...] = jnp.zeros_like(sm_ref)

        cc, ch = cc_ref[...].astype(f32), ch_ref[...].astype(f32)
        T, z = _conv_z(cc, ch, hc_ref[...], hh_ref[...], cw_ref, i == 0)
        dpv = dp_ref[...].astype(f32)
        cbv = cb_ref[...].astype(f32)
        dz = dpv * cbv
        dzn = jnp.where(i == ni - 1, 0.0, dpn_ref[...].astype(f32) * cbn_ref[...].astype(f32))
        E = jnp.concatenate([dz, dzn], axis=0)
        ne = tm + HALO
        dT = cw_ref[2:3, :] * E + cw_ref[1:2, :] * pltpu.roll(E, ne - 1, 0) + cw_ref[0:1, :] * pltpu.roll(E, ne - 2, 0)
        dT = dT[:tm]
        d3_ref[:, :D] = (dpv * z).astype(bf16)
        d3_ref[:, D:2 * D] = (dT * ch).astype(bf16)
        d3_ref[:, 2 * D:] = (dT * cc).astype(bf16)
        sm_ref[2:3, :] += jnp.sum(dz * T[HALO:], axis=0, keepdims=True)
        sm_ref[1:2, :] += jnp.sum(dz * pltpu.roll(T, 1, 0)[HALO:], axis=0, keepdims=True)
        sm_ref[0:1, :] += jnp.sum(dz * pltpu.roll(T, 2, 0)[HALO:], axis=0, keepdims=True)

    row = pl.BlockSpec((tm, D), lambda i: (i, 0))
    nxt = pl.BlockSpec((HALO, D), lambda i: (jnp.minimum((i + 1) * per, nh - 1), 0))
    col = lambda k: pl.BlockSpec((tm, D), lambda i: (i, k))
    prv = lambda k: pl.BlockSpec((HALO, D), lambda i: (jnp.maximum(i * per - 1, 0), k))
    return pl.pallas_call(
        body, name="conv_bwd", grid=(ni,),
        out_shape=[SDS((S, 3 * D), bf16), SDS((8, D), f32)],
        in_specs=[row, nxt, col(0), nxt, col(1), col(2), prv(1), prv(2), pl.BlockSpec((3, D), lambda i: (0, 0))],
        out_specs=[pl.BlockSpec((tm, 3 * D), lambda i: (i, 0)), pl.BlockSpec((8, D), lambda i: (0, 0))],
        compiler_params=_cp("arbitrary"),
    )(dp, dp, rest, rest, rest, rest, rest, rest, cw)


_DU_RANGES = ((0, 3), (3, 6), (6, 9), (9, 15), (15, 19))
N_CBLK = IN_W // CB


def _mix_in_bwd_dh(dxo, x, mod9, g3, dus, win):
    S = x.shape[0]

    def body(dxo_ref, x_ref, mod_ref, g_ref, s0, s1, s2, s3, s4, w_ref, dxi_ref, sm_ref, acc):
        i, kb = pl.program_id(0), pl.program_id(1)

        @pl.when((i == 0) & (kb == 0))
        def _():
            sm_ref[...] = jnp.zeros_like(sm_ref)

        @pl.when(kb == 0)
        def _():
            acc[...] = jnp.zeros_like(acc)

        for src, (lo, hi) in zip((s0, s1, s2, s3, s4), _DU_RANGES):
            @pl.when((kb >= lo) & (kb < hi))
            def _(src=src):
                acc[...] += _dot_nt(src[...].astype(bf16), w_ref[...])

        @pl.when(kb == N_CBLK - 1)
        def _():
            g, scale = g_ref[1:2, :], mod_ref[4:5, :]
            _, xhat, rstd = _norm_fwd(x_ref[...], g, mod_ref[3:4, :], scale)
            dx, dshift, dscale, dg = _norm_bwd(acc[...], xhat, rstd, g, scale)
            dxi_ref[...] = dxo_ref[...] + dx
            sm_ref[0:1, :] += dshift
            sm_ref[1:2, :] += dscale
            sm_ref[3:4, :] += dg

    row = pl.BlockSpec((TMP, D), lambda i, kb: (i, 0))

    def src_spec(lo, hi):
        return pl.BlockSpec((TMP, CB), lambda i, kb: (i, jnp.clip(kb - lo, 0, hi - lo - 1)))

    return pl.pallas_call(
        body, name="mix_in_bwd_dh", grid=(S // TMP, N_CBLK),
        out_shape=[SDS((S, D), f32), SDS((8, D), f32)],
        in_specs=[row, row, pl.BlockSpec((9, D), lambda i, kb: (0, 0)), pl.BlockSpec((3, D), lambda i, kb: (0, 0))]
                 + [src_spec(lo, hi) for lo, hi in _DU_RANGES] + [pl.BlockSpec((D, CB), lambda i, kb: (0, kb))],
        out_specs=[row, pl.BlockSpec((8, D), lambda i, kb: (0, 0))],
        scratch_shapes=[pltpu.VMEM((TMP, D), f32)],
        compiler_params=_cp("arbitrary", "arbitrary"),
    )(dxo, x, mod9, g3, *dus, win)


def _mix_in_bwd_dw(h, dus):
    S = h.shape[0]
    ni = S // TMW

    def body(h_ref, s0, s1, s2, s3, s4, dw_ref, acc):
        kb, i = pl.program_id(0), pl.program_id(1)

        @pl.when(i == 0)
        def _():
            acc[...] = jnp.zeros_like(acc)

        for src, (lo, hi) in zip((s0, s1, s2, s3, s4), _DU_RANGES):
            @pl.when((kb >= lo) & (kb < hi))
            def _(src=src):
                acc[...] += _dot_tn(h_ref[...], src[...].astype(bf16))

        @pl.when(i == ni - 1)
        def _():
            dw_ref[...] = acc[...].astype(bf16)

    def src_spec(lo, hi):
        def imap(kb, i):
            on = (kb >= lo) & (kb < hi)
            return (jnp.where(on, i, 0), jnp.clip(kb - lo, 0, hi - lo - 1))
        return pl.BlockSpec((TMW, CB), imap)

    return pl.pallas_call(
        body, name="mix_in_bwd_dw", grid=(N_CBLK, ni),
        out_shape=SDS((D, IN_W), bf16),
        in_specs=[pl.BlockSpec((TMW, D), lambda kb, i: (i, 0))] + [src_spec(lo, hi) for lo, hi in _DU_RANGES],
        out_specs=pl.BlockSpec((D, CB), lambda kb, i: (0, kb)),
        scratch_shapes=[pltpu.VMEM((D, CB), f32)],
        compiler_params=_cp("arbitrary", "arbitrary"),
    )(h, *dus)


def _loss_head(x, fg, tgt):
    S = x.shape[0]

    def body(x_ref, g_ref, t_ref, ls_ref, dx_ref, sm_ref):
        i = pl.program_id(0)

        @pl.when(i == 0)
        def _():
            ls_ref[...] = jnp.zeros_like(ls_ref)
            sm_ref[...] = jnp.zeros_like(sm_ref)

        xv, g = x_ref[...], g_ref[...]
        rstd = lax.rsqrt(jnp.mean(xv * xv, axis=-1, keepdims=True) + EPS)
        xhat = xv * rstd
        e = xhat * g - t_ref[...]
        ls_ref[...] += 0.5 * jnp.sum(jnp.mean(e * e, axis=-1, keepdims=True))
        dy = e * (1.0 / D)
        sm_ref[0:1, :] += jnp.sum(dy * xhat, axis=0, keepdims=True)
        dxh = dy * g
        dx_ref[...] = rstd * (dxh - xhat * jnp.mean(dxh * xhat, axis=-1, keepdims=True))

    row = pl.BlockSpec((TM, D), lambda i: (i, 0))
    return pl.pallas_call(
        body, name="loss_head", grid=(S // TM,),
        out_shape=[SDS((8, 128), f32), SDS((S, D), f32), SDS((8, D), f32)],
        in_specs=[row, pl.BlockSpec((1, D), lambda i: (0, 0)), row],
        out_specs=[pl.BlockSpec((8, 128), lambda i: (0, 0)), row, pl.BlockSpec((8, D), lambda i: (0, 0))],
        compiler_params=_cp("arbitrary"),
    )(x, fg, tgt)


def _adam(w, g, m, v):
    m2 = B1 * m + (1.0 - B1) * g
    v2 = B2 * v + (1.0 - B2) * (g * g)
    delta = -LR * ((m2 / BC1) / (jnp.sqrt(v2 / BC2) + AEPS) + WD * w)
    return delta, m2, v2


def _row_tile(rows, cols):
    for tr in (512, 352, 256, 128, 64):
        if rows % tr == 0 and tr * cols * 4 <= (5 << 18):
            return tr
    raise ValueError((rows, cols))


def _sum_slots(land):
    _, R, C = land.shape
    tr = _row_tile(R, C)

    def body(l_ref, t_ref):
        t = l_ref[0].astype(f32)
        for k in range(1, N_CHIPS):
            t = t + l_ref[k].astype(f32)
        t_ref[...] = t

    return pl.pallas_call(
        body, name="sum_slots", grid=(R // tr,),
        out_shape=SDS((R, C), f32),
        in_specs=[pl.BlockSpec((N_CHIPS, tr, C), lambda i: (0, i, 0))],
        out_specs=pl.BlockSpec((tr, C), lambda i: (i, 0)),
        compiler_params=_cp("arbitrary"),
    )(land)


def _adamw_pair(w2, m2, v2, ta, tb, outs, slot):
    R, C = ta.shape
    tr = _row_tile(R, C)
    nrt = R // tr

    def body(w_ref, m_ref, v_ref, ta_ref, tb_ref, g_in, d_in, m_in, v_in, g_ref, d_ref, mo_ref, vo_ref):
        g = ta_ref[...] + tb_ref[...]
        delta, mn, vn = _adam(w_ref[...], g, m_ref[...], v_ref[...])
        g_ref[...] = g
        d_ref[...] = delta
        mo_ref[...] = mn
        vo_ref[...] = vn

    big = pl.BlockSpec((tr, C), lambda i: (slot * nrt + i, 0))
    loc = pl.BlockSpec((tr, C), lambda i: (i, 0))
    anyspec = pl.BlockSpec(memory_space=pl.ANY)
    return pl.pallas_call(
        body, name="adamw_pair", grid=(nrt,),
        out_shape=[SDS(o.shape, f32) for o in outs],
        in_specs=[big, big, big, loc, loc] + [anyspec] * 4,
        out_specs=[big] * 4,
        input_output_aliases={5: 0, 6: 1, 7: 2, 8: 3},
        compiler_params=_cp("arbitrary"),
    )(w2, m2, v2, ta, tb, *outs)


def _adamw_small(w, g, m, v):
    def body(w_ref, g_ref, m_ref, v_ref, d_ref, mo_ref, vo_ref):
        delta, mn, vn = _adam(w_ref[...], g_ref[...], m_ref[...], v_ref[...])
        d_ref[...] = delta
        mo_ref[...] = mn
        vo_ref[...] = vn

    return pl.pallas_call(body, name="adamw_small", out_shape=[SDS(w.shape, f32)] * 3)(w, g, m, v)


def _ada_w_update(cs_all, dmod_sh, w, m, v):
    tr = 256

    def body(cs_ref, dm_ref, w_ref, m_ref, v_ref, g_ref, d_ref, mo_ref, vo_ref):
        g = _dot_tn(cs_ref[...].astype(bf16), dm_ref[...].astype(bf16))
        delta, mn, vn = _adam(w_ref[...], g, m_ref[...], v_ref[...])
        g_ref[...] = g
        d_ref[...] = delta
        mo_ref[...] = mn
        vo_ref[...] = vn

    blk = pl.BlockSpec((None, tr, ADA_SH), lambda l, i: (l, i, 0))
    return pl.pallas_call(
        body, name="ada_w_update", grid=(DEPTH, D // tr),
        out_shape=[SDS(w.shape, f32)] * 4,
        in_specs=[pl.BlockSpec((8, tr), lambda l, i: (0, i)), pl.BlockSpec((None, 8, ADA_SH), lambda l, i: (l, 0, 0)), blk, blk, blk],
        out_specs=[blk] * 4,
        compiler_params=_cp("arbitrary", "arbitrary"),
    )(cs_all, dmod_sh, w, m, v)


def _sum_devices(gathered):
    _, R, C = gathered.shape

    def body(g_ref, o_ref):
        t = g_ref[0]
        for k in range(1, 8):
            t = t + g_ref[k]
        o_ref[...] = t

    return pl.pallas_call(body, name="sum_devices", out_shape=SDS((R, C), f32))(gathered)


def _layer_fwd(x, mod9, g3, cw, getw, bias):
    W = {}

    def take(gname, after, mod9):
        w, tok = getw(gname, after)
        W.update(w)
        return mod9 if tok is None else mod9 + tok[0, 0]

    mod9 = take("A", x, mod9)
    x1, h1, a1, u1, y1 = _ffn_fwd(x, mod9, g3, W["wg0"], W["wu0"], W["wd0"], 0)
    mod9 = take("B", x1, mod9)
    hm, qkv = _mix_qkv(x1, mod9, g3, W["win"])
    rest = _mix_rest(hm, W["win"])
    ogs, lgs = [], []
    for g in range(3):
        og, lg = _attn_fwd(qkv, bias, g)
        ogs.append(og)
        lgs.append(lg)
    mod9 = take("C", ogs[2], mod9)
    x2, o, lse, yc, ya, outv = _mix_out_fwd(x1, mod9, rest, ogs, lgs, cw, W["wco"], W["wao"], W["wo"])
    mod9 = take("D", x2, mod9)
    x3, h3, a3, u3, y3 = _ffn_fwd(x2, mod9, g3, W["wg1"], W["wu1"], W["wd1"], 2)
    saved = dict(x0=x, x1=x1, x2=x2, h1=h1, a1=a1, u1=u1, y1=y1, hm=hm, qkv=qkv, rest=rest, o=o, lse=lse, yc=yc, ya=ya,
                 outv=outv, h3=h3, a3=a3, u3=u3, y3=y3)
    return x3, saved, W


def _layer_bwd(dx, sv, mod9, g3, cw, W, bias, emit):
    S = dx.shape[0]
    dw = {}

    def send(gname, mod9):
        tok = emit(gname, dw)
        return mod9 if tok is None else mod9 + tok[0, 0]

    dx2, da, du, hid, dy, sm3 = _ffn_bwd1(dx, sv["x2"], mod9, g3, sv["y3"], sv["a3"], sv["u3"], W["wg1"], W["wu1"], W["wd1"], 2)
    dw["wg1"], dw["wu1"], dw["wd1"] = _ffn_bwd2(sv["h3"], da, du, hid, dy)
    mod9 = send("D", mod9)
    dp, dg2, do, dw["wco"], dw["wao"], dw["wo"], smo = _mix_out_bwd(
        dx2, mod9, sv["outv"], sv["yc"], sv["ya"], sv["rest"], sv["o"], cw, W["wco"], W["wao"], W["wo"])
    mod9_c = send("C", mod9)
    cw = cw + (mod9_c - mod9)[0:1, :]
    mod9 = mod9_c
    d3, smc = _conv_bwd(dp, sv["rest"], cw)
    dq = lax.empty((S, 1536), f32)
    dk = lax.empty((S, 1536), f32)
    dv = lax.empty((S, 1536), f32)
    dsaccs = []
    for g in range(3):
        dq, dk, dv, dsg = _attn_bwd(sv["qkv"], do, sv["o"], sv["lse"], bias, dq, dk, dv, g)
        dsaccs.append(dsg)
    dus = (dq, dk, dv, d3, dg2)
    dx1, smm = _mix_in_bwd_dh(dx2, sv["x1"], mod9, g3, dus, W["win"])
    dw["win"] = _mix_in_bwd_dw(sv["hm"], dus)
    mod9 = send("B", mod9)
    dx0, da, du, hid, dy, sm1 = _ffn_bwd1(dx1, sv["x0"], mod9, g3, sv["y1"], sv["a1"], sv["u1"], W["wg0"], W["wu0"], W["wd0"], 0)
    dw["wg0"], dw["wu0"], dw["wd0"] = _ffn_bwd2(sv["h1"], da, du, hid, dy)
    send("A", mod9)
    dmod = jnp.concatenate([sm1[0:3], smm[0:2], smo[2:3], sm3[0:3]], axis=0)
    dng = jnp.concatenate([sm1[3:4], smm[3:4], sm3[3:4]], axis=0)
    return dx0, dmod, dng, smc[0:3], jnp.concatenate(dsaccs, axis=0)


def _chip_cols(a, chip, width):
    return lax.dynamic_slice_in_dim(a, chip * width, width, axis=a.ndim - 1)


def kernel(x, c, ada_w, ada_b, norm_g, ffn_w_gate, ffn_w_up, ffn_w_down, w_in, conv_w, w_conv_out, w_attn_out, w_o, rel_bias, final_g, loss_target, m_ada_w, m_ada_b, m_norm_g, m_ffn_w_gate, m_ffn_w_up, m_ffn_w_down, m_w_in, m_conv_w, m_w_conv_out, m_w_attn_out, m_w_o, m_rel_bias, m_final_g, v_ada_w, v_ada_b, v_norm_g, v_ffn_w_gate, v_ffn_w_up, v_ffn_w_down, v_w_in, v_conv_w, v_w_conv_out, v_w_attn_out, v_w_o, v_rel_bias, v_final_g):
    ix, iy, ic = lax.axis_index("x"), lax.axis_index("y"), lax.axis_index("c")
    chip = 2 * ix + iy
    dev = 4 * ix + 2 * iy + ic
    xs = x[0]
    S = xs.shape[0]
    qd = D // N_CHIPS

    chip_arr = jnp.reshape(chip, (1,)).astype(jnp.int32)
    names = [w[0] for w in WCLASSES]

    def layer_shards(l):
        return [(ffn_w_gate, (l, 0)), (ffn_w_up, (l, 0)), (ffn_w_down, (l, 0)), (ffn_w_gate, (l, 1)), (ffn_w_up, (l, 1)),
                (ffn_w_down, (l, 1)), (w_in, (l,)), (w_conv_out, (l,)), (w_attn_out, (l,)), (w_o, (l,))]

    started = {}
    extra_starts = {(0, "A"): [(0, "B")], (0, "B"): [(0, "C"), (0, "D"), (1, "A")]}

    def start_gather(l, gname, after):
        shards = layer_shards(l)
        started[(l, gname)] = _gather_group_start(f"l{l}{gname}", GROUPS[gname], [shards[q] for q in GROUPS[gname]], chip_arr, after)
        return started[(l, gname)][-1]

    tok0 = start_gather(0, "A", c)

    pad8 = lambda a: jnp.pad(a, ((0, -a.shape[0] % 8), (0, 0)))
    pack = jnp.concatenate([pad8(c + tok0[0:1, 0:1]), pad8(norm_g.reshape(3, D)), pad8(conv_w.reshape(3, D))], axis=0)
    g1 = _allgather_small(pack).reshape(8, 24, D)
    c_all = g1[:, 0]
    by_chip = g1[0::2]
    ng_full = jnp.concatenate([by_chip[j, 8:11].reshape(DEPTH, 3, qd) for j in range(N_CHIPS)], axis=-1)
    cw_full = jnp.concatenate([by_chip[j, 16:19].reshape(DEPTH, 3, qd) for j in range(N_CHIPS)], axis=-1)
    mod_sh, cs_all = _mod_shards(c_all, ada_w, _chip_cols(ada_b, chip, ADA_SH))
    g2 = _allgather_small(mod_sh.reshape(DEPTH * 8, ADA_SH)).reshape(8, DEPTH, 8, ADA_SH)
    mine = lax.dynamic_index_in_dim(g2[0::2], dev, axis=2, keepdims=False)
    mod = jnp.transpose(mine, (1, 0, 2)).reshape(DEPTH, 9, D)

    buckets = jnp.asarray(_bucket_table())
    bias = _bias_blocks(rel_bias, buckets)

    def make_getw(l):
        def getw(gname, after):
            full = _gather_group_wait(f"l{l}{gname}", GROUPS[gname], started[(l, gname)], after)
            tok = None
            for nl, ng in extra_starts.get((l, gname), []) + [(l + 1, gname)]:
                if nl < DEPTH and (nl, ng) not in started:
                    tok = start_gather(nl, ng, full[0] if tok is None else tok)
            return {names[q]: f for q, f in zip(GROUPS[gname], full)}, tok
        return getw

    Ws, saves = [], []
    xc = xs
    for l in range(DEPTH):
        xc, sv, W = _layer_fwd(xc, mod[l], ng_full[l], cw_full[l], make_getw(l), bias)
        Ws.append(W)
        saves.append(sv)

    ls, dx, smf = _loss_head(xc, final_g.reshape(1, D), loss_target[0])
    loss = lax.psum(ls[0, 0], ("x", "y", "c"))

    params = dict(wg=ffn_w_gate, wu=ffn_w_up, wd=ffn_w_down, win=w_in, wco=w_conv_out, wao=w_attn_out, wo=w_o)
    moms = dict(wg=m_ffn_w_gate, wu=m_ffn_w_up, wd=m_ffn_w_down, win=m_w_in, wco=m_w_conv_out, wao=m_w_attn_out, wo=m_w_o)
    vars_ = dict(wg=v_ffn_w_gate, wu=v_ffn_w_up, wd=v_ffn_w_down, win=v_w_in, wco=v_w_conv_out, wao=v_w_attn_out, wo=v_w_o)
    flat = lambda a: a.reshape(-1, a.shape[-1])
    big_out = {k: [lax.empty(flat(p).shape, f32) for _ in range(4)] for k, p in params.items()}
    dmods, dngs, dcws, dsaccs = [None] * DEPTH, [None] * DEPTH, [None] * DEPTH, [None] * DEPTH

    def finish(l, gname, started, after):
        group = GROUPS[gname]
        pieces, lands = _scatter_group_wait(f"l{l}{gname}", group, started, after)
        ts = [_sum_own_slots(pieces[i], lands[i], *_cls(q), chip_arr) for i, q in enumerate(group)]
        tsib = _swap_sibling(ts)
        for i, q in enumerate(group):
            name = names[q]
            key = name.rstrip("01")
            slot = 2 * l + int(name[-1]) if name[-1] in "01" else l
            big_out[key] = _adamw_pair(flat(params[key]), flat(moms[key]), flat(vars_[key]), ts[i], tsib[i], big_out[key], slot)

    pending, tok = [], None
    for l in reversed(range(DEPTH)):
        modl = mod[l] if tok is None else mod[l] + tok[0, 0]
        mine = []

        def emit(gname, dw, l=l, mine=mine):
            prev = mine[-1][2][-1] if mine else dx
            mine.append((l, gname, _scatter_group_start(f"l{l}{gname}", GROUPS[gname], [dw[names[q]] for q in GROUPS[gname]], prev)))
            return mine[-1][2][-1]

        dx, dmods[l], dngs[l], dcws[l], dsaccs[l] = _layer_bwd(dx, saves[l], modl, ng_full[l], cw_full[l], Ws[l], bias, emit)
        for pl_, pg, pst in pending:
            finish(pl_, pg, pst, dx)
        pending, tok = mine, mine[-1][2][-1]
    for pl_, pg, pst in pending:
        finish(pl_, pg, pst, dx)

    drb = jnp.transpose(_bias_grad(dsaccs, buckets)[:, 0, :NUM_BUCKETS])
    drb_row = jnp.pad(drb.reshape(1, NUM_BUCKETS * 24), ((0, 0), (0, D - NUM_BUCKETS * 24)))
    pack2 = jnp.concatenate([pad8(a) for a in dmods] + [pad8(a) for a in dngs] + [pad8(a) for a in dcws] + [smf, pad8(drb_row)], axis=0)
    n_rows = pack2.shape[0]
    g3 = _allgather_small(pack2).reshape(8, n_rows, D)
    tot = _sum_devices(g3)
    o_ng, o_cw, o_fg, o_rb = 16 * DEPTH, 24 * DEPTH, 32 * DEPTH, 32 * DEPTH + 8
    g_ada_b = jnp.stack([tot[16 * l:16 * l + 9] for l in range(DEPTH)]).reshape(DEPTH, 9 * D)
    g_norm_g = _chip_cols(jnp.stack([tot[o_ng + 8 * l:o_ng + 8 * l + 3] for l in range(DEPTH)]), chip, qd)
    g_conv_w = _chip_cols(jnp.stack([tot[o_cw + 8 * l:o_cw + 8 * l + 3] for l in range(DEPTH)]), chip, qd)
    g_final_g = tot[o_fg]
    g_rel_bias = tot[o_rb, :NUM_BUCKETS * 24].reshape(NUM_BUCKETS, 24)
    dmod_all = jnp.stack([g3[:, 16 * l:16 * l + 9].reshape(8, 9 * D) for l in range(DEPTH)])
    dmod_sh = _chip_cols(dmod_all, chip, ADA_SH)
    g_ada_w, d_ada_w, nm_ada_w, nv_ada_w = _ada_w_update(cs_all, dmod_sh, ada_w, m_ada_w, v_ada_w)

    def small(w, g, m, v):
        shp = w.shape
        to2 = lambda a: a.reshape(-1, shp[-1])
        return [o.reshape(shp) for o in _adamw_small(to2(w), to2(g), to2(m), to2(v))]

    d_ada_b, nm_ada_b, nv_ada_b = small(ada_b, g_ada_b, m_ada_b, v_ada_b)
    d_norm_g, nm_norm_g, nv_norm_g = small(norm_g, g_norm_g, m_norm_g, v_norm_g)
    d_conv_w, nm_conv_w, nv_conv_w = small(conv_w, g_conv_w, m_conv_w, v_conv_w)
    d_rel_bias, nm_rel_bias, nv_rel_bias = small(rel_bias, g_rel_bias, m_rel_bias, v_rel_bias)
    d_final_g, nm_final_g, nv_final_g = small(final_g, g_final_g, m_final_g, v_final_g)

    def big(key, which):
        return big_out[key][which].reshape(params[key].shape)

    grads = [g_ada_w, g_ada_b, g_norm_g, big("wg", 0), big("wu", 0), big("wd", 0), big("win", 0), g_conv_w, big("wco", 0),
             big("wao", 0), big("wo", 0), g_rel_bias, g_final_g]
    deltas = [d_ada_w, d_ada_b, d_norm_g, big("wg", 1), big("wu", 1), big("wd", 1), big("win", 1), d_conv_w, big("wco", 1),
              big("wao", 1), big("wo", 1), d_rel_bias, d_final_g]
    new_m = [nm_ada_w, nm_ada_b, nm_norm_g, big("wg", 2), big("wu", 2), big("wd", 2), big("win", 2), nm_conv_w, big("wco", 2),
             big("wao", 2), big("wo", 2), nm_rel_bias, nm_final_g]
    new_v = [nv_ada_w, nv_ada_b, nv_norm_g, big("wg", 3), big("wu", 3), big("wd", 3), big("win", 3), nv_conv_w, big("wco", 3),
             big("wao", 3), big("wo", 3), nv_rel_bias, nv_final_g]
    return (loss, dx[None], *grads, *deltas, *new_m, *new_v)
```

```python
import functools

import numpy as np
import jax
import jax.numpy as jnp
from jax import lax
from jax.experimental import pallas as pl
from jax.experimental.pallas import tpu as pltpu

f32, bf16 = jnp.float32, jnp.bfloat16
SDS = jax.ShapeDtypeStruct
MESH = pl.DeviceIdType.MESH

D = 1024
DEPTH = 4
N_CHIPS = 4
FB = 704
HD = 64
QKV_W = 4608
REST_W = 5120
IN_W = QKV_W + REST_W
WIN_SH = IN_W // N_CHIPS
ADA_SH = 9 * D // N_CHIPS
BLK = 128
DILATIONS = (1, 4, 16)
Q_BLOCKS = (4, 1, 1)
NUM_BUCKETS, MAX_DISTANCE = 32, 2048
EPS = 1e-6
NEG = -1e30
SCALE = HD ** -0.5
LR, B1, B2, AEPS, WD, STEP = 0.001, 0.9, 0.999, 1e-08, 0.01, 10
BC1 = 1.0 - B1 ** STEP
BC2 = 1.0 - B2 ** STEP
VMEM_LIMIT = 56 * 1024 * 1024
TM = 512
TMW = 1024
TMP = 1024
TMF = 1024
TMX = 256
HALO = 16
CB = 512


def _cp(*sem):
    return pltpu.CompilerParams(dimension_semantics=sem if sem else None, vmem_limit_bytes=VMEM_LIMIT)


def _dot(a, b):
    return jnp.dot(a, b, preferred_element_type=f32)


def _dot_nt(a, b):
    return lax.dot_general(a, b, (((1,), (1,)), ((), ())), preferred_element_type=f32)


def _dot_tn(a, b):
    return lax.dot_general(a, b, (((0,), (0,)), ((), ())), preferred_element_type=f32)


def _sigmoid(x):
    return 0.5 * jnp.tanh(0.5 * x) + 0.5


def _norm_fwd(x, g, shift, scale):
    rstd = lax.rsqrt(jnp.mean(x * x, axis=-1, keepdims=True) + EPS)
    xhat = x * rstd
    return xhat * g * (1.0 + scale) + shift, xhat, rstd


def _norm_bwd(dh, xhat, rstd, g, scale):
    dshift = jnp.sum(dh, axis=0, keepdims=True)
    dscale = jnp.sum(dh * xhat * g, axis=0, keepdims=True)
    dg = jnp.sum(dh * xhat * (1.0 + scale), axis=0, keepdims=True)
    dxh = dh * (g * (1.0 + scale))
    dx = rstd * (dxh - xhat * jnp.mean(dxh * xhat, axis=-1, keepdims=True))
    return dx, dshift, dscale, dg


def _allgather_small(xp):
    m_per, n = xp.shape

    def body(x_ref, out_ref, send_sems, recv_sems, local_sem):
        x, y, c = lax.axis_index("x"), lax.axis_index("y"), lax.axis_index("c")
        me, sibling = (x, y, c), (x, y, 1 - c)
        chips = [(1 - x, y), (x, 1 - y), (1 - x, 1 - y)]

        def rows(px, py, pc):
            return out_ref.at[pl.ds((4 * px + 2 * py + pc) * m_per, m_per), :]

        def copy(k, block, to, src=None):
            return pltpu.make_async_remote_copy(
                src_ref=rows(*block) if src is None else src, dst_ref=rows(*block),
                send_sem=send_sems.at[k], recv_sem=recv_sems.at[k], device_id=to, device_id_type=MESH)

        mine = pltpu.make_async_copy(x_ref, rows(*me), local_sem)
        mine.start()
        first = [copy(0, me, sibling, src=x_ref)]
        first += [copy(1 + j, me, (*chip, c), src=x_ref) for j, chip in enumerate(chips)]
        for cp in first:
            cp.start()
        passed = [copy(4 + j, (*chip, c), sibling) for j, chip in enumerate(chips)]
        for j, chip in enumerate(chips):
            copy(1 + j, (*chip, c), me).wait_recv()
            passed[j].start()
        copy(0, sibling, me).wait_recv()
        for j, chip in enumerate(chips):
            copy(4 + j, (*chip, 1 - c), me).wait_recv()
        for cp in first + passed:
            cp.wait_send()
        mine.wait()

    return pl.pallas_call(
        body, name="allgather_small",
        out_shape=SDS((8 * m_per, n), xp.dtype),
        in_specs=[pl.BlockSpec(memory_space=pltpu.VMEM)],
        out_specs=pl.BlockSpec(memory_space=pltpu.VMEM),
        scratch_shapes=[pltpu.SemaphoreType.DMA((7,)), pltpu.SemaphoreType.DMA((7,)), pltpu.SemaphoreType.DMA],
        compiler_params=pltpu.CompilerParams(vmem_limit_bytes=VMEM_LIMIT),
    )(xp)


WCLASSES = (
    ("wg0", "lead", (D, FB)), ("wu0", "lead", (D, FB)), ("wd0", "row", (FB, D)),
    ("wg1", "lead", (D, FB)), ("wu1", "lead", (D, FB)), ("wd1", "row", (FB, D)),
    ("win", "col", (D, WIN_SH)), ("wco", "row", (D // N_CHIPS, D)), ("wao", "col", (512, D // N_CHIPS)),
    ("wo", "row", (D // N_CHIPS, D)),
)
NCLS = len(WCLASSES)


def _full_shape(kind, shp):
    if kind == "lead":
        return (N_CHIPS,) + shp
    if kind == "row":
        return (N_CHIPS * shp[0], shp[1])
    return (shp[0], N_CHIPS * shp[1])


def _shard_view(ref, kind, shp, j):
    if kind == "lead":
        return ref.at[j]
    if kind == "row":
        return ref.at[pl.ds(j * shp[0], shp[0]), :]
    return ref.at[:, pl.ds(j * shp[1], shp[1])]


def _half(ref, shp, h):
    hr = shp[0] // 2
    return ref.at[pl.ds(pl.multiple_of(h * hr, 16), hr), :]


def _gather_weights(shards):
    n = NCLS

    def body(*refs):
        ins, outs = refs[:n], refs[n:2 * n]
        send1, recv1, send2, recv2, lsem = refs[2 * n:]
        x, y, c = lax.axis_index("x"), lax.axis_index("y"), lax.axis_index("c")
        chip = 2 * x + y
        sibling = (x, y, 1 - c)

        for mc in range(N_CHIPS):
            @pl.when(chip == mc)
            def _(mc=mc):
                local = []
                for q, (_, kind, shp) in enumerate(WCLASSES):
                    cp = pltpu.make_async_copy(ins[q], _shard_view(outs[q], kind, shp, mc), lsem.at[q])
                    cp.start()
                    local.append(cp)
                sends = []
                for k in (1, 2, 3):
                    pj = mc ^ k
                    for q, (_, kind, shp) in enumerate(WCLASSES):
                        cp = pltpu.make_async_remote_copy(
                            src_ref=_half(ins[q], shp, c), dst_ref=_half(_shard_view(outs[q], kind, shp, mc), shp, c),
                            send_sem=send1.at[q * 3 + k - 1], recv_sem=recv1.at[q * 3 + k - 1],
                            device_id=(pj >> 1, pj & 1, c), device_id_type=MESH)
                        cp.start()
                        sends.append(cp)
                for k in (1, 2, 3):
                    pj = mc ^ k
                    for q, (_, kind, shp) in enumerate(WCLASSES):
                        landed = _half(_shard_view(outs[q], kind, shp, pj), shp, c)
                        pltpu.make_async_remote_copy(
                            src_ref=landed, dst_ref=landed, send_sem=send1.at[q * 3 + k - 1], recv_sem=recv1.at[q * 3 + k - 1],
                            device_id=(pj >> 1, pj & 1, c), device_id_type=MESH).wait_recv()
                        cp = pltpu.make_async_remote_copy(
                            src_ref=landed, dst_ref=landed, send_sem=send2.at[q * 3 + k - 1], recv_sem=recv2.at[q * 3 + k - 1],
                            device_id=sibling, device_id_type=MESH)
                        cp.start()
                        sends.append(cp)
                for k in (1, 2, 3):
                    pj = mc ^ k
                    for q, (_, kind, shp) in enumerate(WCLASSES):
                        other = _half(_shard_view(outs[q], kind, shp, pj), shp, 1 - c)
                        pltpu.make_async_remote_copy(
                            src_ref=other, dst_ref=other, send_sem=send2.at[q * 3 + k - 1], recv_sem=recv2.at[q * 3 + k - 1],
                            device_id=sibling, device_id_type=MESH).wait_recv()
                for cp in sends:
                    cp.wait_send()
                for cp in local:
                    cp.wait()

    anyspec = pl.BlockSpec(memory_space=pl.ANY)
    return pl.pallas_call(
        body, name="gather_weights",
        out_shape=[SDS(_full_shape(kind, shp), bf16) for _, kind, shp in WCLASSES],
        in_specs=[anyspec] * n, out_specs=[anyspec] * n,
        scratch_shapes=[pltpu.SemaphoreType.DMA((3 * n,)), pltpu.SemaphoreType.DMA((3 * n,)),
                        pltpu.SemaphoreType.DMA((3 * n,)), pltpu.SemaphoreType.DMA((3 * n,)),
                        pltpu.SemaphoreType.DMA((n,))],
    )(*shards)


def _scatter_grads(pieces):
    n = NCLS

    def body(*refs):
        ins, outs = refs[:n], refs[n:2 * n]
        send1, recv1, lsem = refs[2 * n:]
        x, y, c = lax.axis_index("x"), lax.axis_index("y"), lax.axis_index("c")
        chip = 2 * x + y

        for mc in range(N_CHIPS):
            @pl.when(chip == mc)
            def _(mc=mc):
                local, sends = [], []
                for q, (_, kind, shp) in enumerate(WCLASSES):
                    cp = pltpu.make_async_copy(_shard_view(ins[q], kind, shp, mc), outs[q].at[0], lsem.at[q])
                    cp.start()
                    local.append(cp)
                for k in (1, 2, 3):
                    pj = mc ^ k
                    for q, (_, kind, shp) in enumerate(WCLASSES):
                        cp = pltpu.make_async_remote_copy(
                            src_ref=_shard_view(ins[q], kind, shp, pj), dst_ref=outs[q].at[k],
                            send_sem=send1.at[q * 3 + k - 1], recv_sem=recv1.at[q * 3 + k - 1],
                            device_id=(pj >> 1, pj & 1, c), device_id_type=MESH)
                        cp.start()
                        sends.append(cp)
                for cp in sends:
                    cp.wait_recv()
                for cp in sends:
                    cp.wait_send()
                for cp in local:
                    cp.wait()

    anyspec = pl.BlockSpec(memory_space=pl.ANY)
    return pl.pallas_call(
        body, name="scatter_grads",
        out_shape=[SDS((N_CHIPS,) + shp, bf16) for _, _, shp in WCLASSES],
        in_specs=[anyspec] * n, out_specs=[anyspec] * n,
        scratch_shapes=[pltpu.SemaphoreType.DMA((3 * n,)), pltpu.SemaphoreType.DMA((3 * n,)), pltpu.SemaphoreType.DMA((n,))],
    )(*pieces)


def _swap_sibling(ts):
    n = len(ts)

    def body(*refs):
        ins, outs = refs[:n], refs[n:2 * n]
        send, recv = refs[2 * n:]
        x, y, c = lax.axis_index("x"), lax.axis_index("y"), lax.axis_index("c")
        cps = []
        for q in range(n):
            cp = pltpu.make_async_remote_copy(src_ref=ins[q], dst_ref=outs[q], send_sem=send.at[q], recv_sem=recv.at[q],
                                              device_id=(x, y, 1 - c), device_id_type=MESH)
            cp.start()
            cps.append(cp)
        for cp in cps:
            cp.wait_recv()
        for cp in cps:
            cp.wait_send()

    anyspec = pl.BlockSpec(memory_space=pl.ANY)
    return pl.pallas_call(
        body, name="swap_sibling",
        out_shape=[SDS(t.shape, t.dtype) for t in ts],
        in_specs=[anyspec] * n, out_specs=[anyspec] * n,
        scratch_shapes=[pltpu.SemaphoreType.DMA((n,)), pltpu.SemaphoreType.DMA((n,))],
    )(*ts)


HBM_SPEC = pl.BlockSpec(memory_space=pltpu.HBM)
SEM_SPEC = pl.BlockSpec(memory_space=pltpu.SEMAPHORE)
ANY_SPEC = pl.BlockSpec(memory_space=pl.ANY)
EFFECT = pltpu.SideEffectType.DATAFLOW_SIDE_EFFECTING
N_COPIES = 3 * NCLS


def _in_hbm(a):
    return pltpu.with_memory_space_constraint(a, pltpu.HBM)


def _chip_index():
    return 2 * lax.axis_index("x") + lax.axis_index("y")


def _place_own(shards):
    n = NCLS

    def body(*refs):
        ins, outs, lsem = refs[:n], refs[n:2 * n], refs[2 * n]
        chip = _chip_index()
        for mc in range(N_CHIPS):
            @pl.when(chip == mc)
            def _(mc=mc):
                cps = [pltpu.make_async_copy(ins[q], _shard_view(outs[q], kind, shp, mc), lsem.at[q])
                       for q, (_, kind, shp) in enumerate(WCLASSES)]
                for cp in cps:
                    cp.start()
                for cp in cps:
                    cp.wait()

    return pl.pallas_call(
        body, name="place_own",
        out_shape=[SDS(_full_shape(kind, shp), bf16) for _, kind, shp in WCLASSES],
        in_specs=[ANY_SPEC] * n, out_specs=[ANY_SPEC] * n,
        scratch_shapes=[pltpu.SemaphoreType.DMA((n,))],
    )(*shards)


def _take_own(pieces):
    n = NCLS

    def body(*refs):
        ins, outs, lsem = refs[:n], refs[n:2 * n], refs[2 * n]
        chip = _chip_index()
        for mc in range(N_CHIPS):
            @pl.when(chip == mc)
            def _(mc=mc):
                cps = [pltpu.make_async_copy(_shard_view(ins[q], kind, shp, mc), outs[q].at[0], lsem.at[q])
                       for q, (_, kind, shp) in enumerate(WCLASSES)]
                for cp in cps:
                    cp.start()
                for cp in cps:
                    cp.wait()

    return pl.pallas_call(
        body, name="take_own",
        out_shape=[SDS((N_CHIPS,) + shp, bf16) for _, _, shp in WCLASSES],
        in_specs=[ANY_SPEC] * n, out_specs=[ANY_SPEC] * n,
        scratch_shapes=[pltpu.SemaphoreType.DMA((n,))],
    )(*pieces)


def _split_start(name, srcs, dsts, after, src_view, dst_view):
    n = NCLS

    def body(*refs):
        src, dst = refs[:n], refs[n:2 * n]
        send, recv = refs[2 * n + 1], refs[2 * n + 2]
        token = refs[-1]
        c = lax.axis_index("c")
        chip = _chip_index()
        for mc in range(N_CHIPS):
            @pl.when(chip == mc)
            def _(mc=mc):
                for k in (1, 2, 3):
                    pj = mc ^ k
                    for q in range(n):
                        pltpu.make_async_remote_copy(
                            src_ref=src_view(src[q], q, mc, pj), dst_ref=dst_view(dst[q], q, mc, k),
                            send_sem=send.at[q * 3 + k - 1], recv_sem=recv.at[q * 3 + k - 1],
                            device_id=(pj >> 1, pj & 1, c), device_id_type=MESH).start()
        token[...] = jnp.zeros_like(token)

    return pl.pallas_call(
        body, name=name,
        out_shape=(pltpu.SemaphoreType.DMA((N_COPIES,)), pltpu.SemaphoreType.DMA((N_COPIES,)),
                   *[pltpu.HBM(a.shape, a.dtype) for a in srcs], *[pltpu.HBM(a.shape, a.dtype) for a in dsts], SDS((8, 128), f32)),
        in_specs=[HBM_SPEC] * (2 * n) + [ANY_SPEC],
        out_specs=(SEM_SPEC, SEM_SPEC, *([HBM_SPEC] * (2 * n)), pl.BlockSpec(memory_space=pltpu.VMEM)),
        input_output_aliases={i: 2 + i for i in range(2 * n)},
        compiler_params=pltpu.CompilerParams(has_side_effects=EFFECT),
    )(*[_in_hbm(a) for a in srcs], *[_in_hbm(a) for a in dsts], after)


def _split_wait(name, started, after, arrival_view):
    n = NCLS
    send, recv = started[0], started[1]
    srcs, dsts = started[2:2 + n], started[2 + n:2 + 2 * n]

    def body(*refs):
        src, dst = refs[:n], refs[n:2 * n]
        send_sem, recv_sem = refs[2 * n], refs[2 * n + 1]
        x, y, c = lax.axis_index("x"), lax.axis_index("y"), lax.axis_index("c")
        for k in (1, 2, 3):
            for q in range(n):
                arrival = arrival_view(dst[q], q, k)
                cp = pltpu.make_async_remote_copy(
                    src_ref=arrival, dst_ref=arrival, send_sem=send_sem.at[q * 3 + k - 1], recv_sem=recv_sem.at[q * 3 + k - 1],
                    device_id=(x, y, 1 - c), device_id_type=MESH)
                cp.wait_send()
                cp.wait_recv()

    out = pl.pallas_call(
        body, name=name,
        out_shape=(*[pltpu.HBM(a.shape, a.dtype) for a in srcs], *[pltpu.HBM(a.shape, a.dtype) for a in dsts]),
        in_specs=[HBM_SPEC] * (2 * n) + [SEM_SPEC, SEM_SPEC, ANY_SPEC],
        out_specs=tuple([HBM_SPEC] * (2 * n)),
        input_output_aliases={i: i for i in range(2 * n)},
        compiler_params=pltpu.CompilerParams(has_side_effects=EFFECT),
    )(*srcs, *dsts, send, recv, after)
    return out[n:]


def _cls(q):
    return WCLASSES[q][1], WCLASSES[q][2]


def _gather_start(shards, after):
    fulls = _place_own(shards)
    return _split_start("gather_start", shards, fulls, after,
                        lambda ref, q, mc, pj: ref,
                        lambda ref, q, mc, k: _shard_view(ref, *_cls(q), mc))


def _gather_wait(started, after):
    return _split_wait("gather_wait", started, after, lambda ref, q, k: _shard_view(ref, *_cls(q), 0))


def _scatter_start(pieces, after):
    lands = _take_own(pieces)
    return _split_start("scatter_start", pieces, lands, after,
                        lambda ref, q, mc, pj: _shard_view(ref, *_cls(q), pj),
                        lambda ref, q, mc, k: ref.at[k])


def _scatter_wait(started, after):
    return _split_wait("scatter_wait", started, after, lambda ref, q, k: ref.at[k])


GROUPS = {"A": (0, 1, 2), "B": (6,), "C": (7, 8, 9), "D": (3, 4, 5)}


def _own_spec(kind, shp, tr):
    R, C = shp
    if kind == "lead":
        return pl.BlockSpec((None, tr, C), lambda i, chip: (chip[0], i, 0))
    if kind == "row":
        return pl.BlockSpec((tr, C), lambda i, chip: (chip[0] * (R // tr) + i, 0))
    return pl.BlockSpec((tr, C), lambda i, chip: (i, chip[0]))


def _cast_place(shards, kind, shp, chip_arr):
    n = len(shards)
    R, C = shp
    tr = _row_tile(R, C)

    def body(chip_ref, *refs):
        for q in range(n):
            refs[n + q][...] = refs[q][...].astype(bf16)

    def in_spec(lead):
        return pl.BlockSpec((None,) * len(lead) + (tr, C), lambda i, chip: (*lead, i, 0))

    return pl.pallas_call(
        body, name="cast_place",
        grid_spec=pltpu.PrefetchScalarGridSpec(
            num_scalar_prefetch=1, grid=(R // tr,),
            in_specs=[in_spec(lead) for _, lead in shards],
            out_specs=[_own_spec(kind, shp, tr)] * n),
        out_shape=[SDS(_full_shape(kind, shp), bf16)] * n,
        compiler_params=_cp("arbitrary"),
    )(chip_arr, *[a for a, _ in shards])


def _sum_own_slots(piece, land, kind, shp, chip_arr):
    R, C = shp
    tr = _row_tile(R, C)

    def body(chip_ref, p_ref, l_ref, t_ref):
        t = p_ref[...].astype(f32)
        for k in range(N_CHIPS - 1):
            t = t + l_ref[k].astype(f32)
        t_ref[...] = t

    return pl.pallas_call(
        body, name="sum_own_slots",
        grid_spec=pltpu.PrefetchScalarGridSpec(
            num_scalar_prefetch=1, grid=(R // tr,),
            in_specs=[_own_spec(kind, shp, tr), pl.BlockSpec((N_CHIPS - 1, tr, C), lambda i, chip: (0, i, 0))],
            out_specs=pl.BlockSpec((tr, C), lambda i, chip: (i, 0))),
        out_shape=SDS((R, C), f32),
        compiler_params=_cp("arbitrary"),
    )(chip_arr, piece, land)


def _xfer_start(name, arrays, ng, after, src_view, dst_view):
    na = len(arrays)

    def body(*refs):
        arr = refs[:na]
        send, recv, token = refs[na + 1], refs[na + 2], refs[-1]
        c = lax.axis_index("c")
        chip = _chip_index()
        for mc in range(N_CHIPS):
            @pl.when(chip == mc)
            def _(mc=mc):
                for k in (1, 2, 3):
                    pj = mc ^ k
                    for i in range(ng):
                        pltpu.make_async_remote_copy(
                            src_ref=src_view(arr, i, mc, pj), dst_ref=dst_view(arr, i, mc, k),
                            send_sem=send.at[i * 3 + k - 1], recv_sem=recv.at[i * 3 + k - 1],
                            device_id=(pj >> 1, pj & 1, c), device_id_type=MESH).start()
        token[...] = jnp.zeros_like(token)

    return pl.pallas_call(
        body, name=name,
        out_shape=(pltpu.SemaphoreType.DMA((3 * ng,)), pltpu.SemaphoreType.DMA((3 * ng,)),
                   *[pltpu.HBM(a.shape, a.dtype) for a in arrays], SDS((8, 128), f32)),
        in_specs=[HBM_SPEC] * na + [ANY_SPEC],
        out_specs=(SEM_SPEC, SEM_SPEC, *([HBM_SPEC] * na), pl.BlockSpec(memory_space=pltpu.VMEM)),
        input_output_aliases={i: 2 + i for i in range(na)},
        compiler_params=pltpu.CompilerParams(has_side_effects=EFFECT),
    )(*[_in_hbm(a) for a in arrays], after)


def _xfer_wait(name, started, ng, after, arrival_view):
    send, recv = started[0], started[1]
    arrays = started[2:-1]
    na = len(arrays)

    def body(*refs):
        arr = refs[:na]
        send_sem, recv_sem = refs[na], refs[na + 1]
        x, y, c = lax.axis_index("x"), lax.axis_index("y"), lax.axis_index("c")
        for k in (1, 2, 3):
            for i in range(ng):
                arrival = arrival_view(arr, i)
                cp = pltpu.make_async_remote_copy(
                    src_ref=arrival, dst_ref=arrival, send_sem=send_sem.at[i * 3 + k - 1], recv_sem=recv_sem.at[i * 3 + k - 1],
                    device_id=(x, y, 1 - c), device_id_type=MESH)
                cp.wait_send()
                cp.wait_recv()

    return pl.pallas_call(
        body, name=name,
        out_shape=tuple(pltpu.HBM(a.shape, a.dtype) for a in arrays),
        in_specs=[HBM_SPEC] * na + [SEM_SPEC, SEM_SPEC, ANY_SPEC],
        out_specs=tuple([HBM_SPEC] * na),
        input_output_aliases={i: i for i in range(na)},
        compiler_params=pltpu.CompilerParams(has_side_effects=EFFECT),
    )(*arrays, send, recv, after)


def _gather_group_start(tag, group, shards_f32, chip_arr, after):
    fulls = [None] * len(group)
    by_shape = {}
    for i, q in enumerate(group):
        by_shape.setdefault(_cls(q), []).append(i)
    for (kind, shp), idx in by_shape.items():
        for i, f in zip(idx, _cast_place([shards_f32[i] for i in idx], kind, shp, chip_arr)):
            fulls[i] = f
    view = lambda arr, i, mc, _: _shard_view(arr[i], *_cls(group[i]), mc)
    return _xfer_start("gather_start_" + tag, fulls, len(group), after, view, view)


def _gather_group_wait(tag, group, started, after):
    return _xfer_wait("gather_wait_" + tag, started, len(group), after, lambda arr, i: _shard_view(arr[i], *_cls(group[i]), 0))


def _scatter_group_start(tag, group, pieces, after):
    ng = len(group)
    lands = [lax.empty((N_CHIPS - 1,) + _cls(q)[1], bf16) for q in group]
    return _xfer_start("scatter_start_" + tag, list(pieces) + lands, ng, after,
                       lambda arr, i, mc, pj: _shard_view(arr[i], *_cls(group[i]), pj),
                       lambda arr, i, mc, k: arr[ng + i].at[k - 1])


def _scatter_group_wait(tag, group, started, after):
    ng = len(group)
    out = _xfer_wait("scatter_wait_" + tag, started, ng, after, lambda arr, i: arr[ng + i].at[0])
    return out[:ng], out[ng:]


def _mod_shards(c_all, ada_w, ada_b_sh):
    tn = ADA_SH // 3

    def body(c_ref, w_ref, b_ref, o_ref, cs_ref):
        cv = c_ref[...]
        cs = cv * _sigmoid(cv)
        cs_ref[...] = cs
        o_ref[...] = _dot(cs.astype(bf16), w_ref[...].astype(bf16)) + b_ref[...]

    return pl.pallas_call(
        body, name="mod_shards", grid=(DEPTH, 3),
        out_shape=[SDS((DEPTH, 8, ADA_SH), f32), SDS((8, D), f32)],
        in_specs=[pl.BlockSpec((8, D), lambda l, t: (0, 0)),
                  pl.BlockSpec((None, D, tn), lambda l, t: (l, 0, t)),
                  pl.BlockSpec((None, 1, tn), lambda l, t: (l, 0, t))],
        out_specs=[pl.BlockSpec((None, 8, tn), lambda l, t: (l, 0, t)), pl.BlockSpec((8, D), lambda l, t: (0, 0))],
        compiler_params=_cp("arbitrary", "arbitrary"),
    )(c_all, ada_w, ada_b_sh.reshape(DEPTH, 1, ADA_SH))


def _t5_bucket(dist):
    exact = NUM_BUCKETS // 2
    dd = np.maximum(dist, 1).astype(np.float32)
    large = exact + (np.log(dd / exact) / np.log(MAX_DISTANCE / exact) * (NUM_BUCKETS - exact)).astype(np.int32)
    large = np.minimum(large, NUM_BUCKETS - 1)
    return np.where(dist < exact, dist, large).astype(np.int32)


def _bucket_table():
    i = np.arange(BLK)[:, None]
    j = np.arange(2 * BLK)[None, :]
    rel = i - j + BLK
    return np.stack([_t5_bucket(np.maximum(rel, 0) * d) for d in DILATIONS]).astype(np.int32)


def _band():
    rel = lax.broadcasted_iota(jnp.int32, (BLK, 2 * BLK), 0) - lax.broadcasted_iota(jnp.int32, (BLK, 2 * BLK), 1) + BLK
    return (rel >= 0) & (rel <= BLK)


def _bias_blocks(rel_bias, buckets):
    def body(tab_ref, bk_ref, o_ref):
        h = pl.program_id(0)
        bk = bk_ref[...]
        acc = jnp.zeros((BLK, 2 * BLK), f32)
        for b in range(NUM_BUCKETS):
            acc = jnp.where(bk == b, tab_ref[b, h], acc)
        o_ref[...] = jnp.where(_band(), acc, NEG)

    return pl.pallas_call(
        body, name="bias_blocks", grid=(24,),
        out_shape=SDS((24, BLK, 2 * BLK), f32),
        in_specs=[pl.BlockSpec(memory_space=pltpu.SMEM), pl.BlockSpec((None, BLK, 2 * BLK), lambda h: (h // 8, 0, 0))],
        out_specs=pl.BlockSpec((None, BLK, 2 * BLK), lambda h: (h, 0, 0)),
        compiler_params=_cp("arbitrary"),
    )(rel_bias, buckets)


def _bias_grad(dsaccs, buckets):
    nl = len(dsaccs)

    def body(*refs):
        bk = refs[nl][...]
        tot = refs[0][...]
        for r in refs[1:nl]:
            tot = tot + r[...]
        lane = lax.broadcasted_iota(jnp.int32, (1, 128), 1)
        row = jnp.zeros((1, 128), f32)
        for b in range(NUM_BUCKETS):
            row = jnp.where(lane == b, jnp.sum(jnp.where(bk == b, tot, 0.0)), row)
        refs[nl + 1][...] = row

    return pl.pallas_call(
        body, name="bias_grad", grid=(24,),
        out_shape=SDS((24, 1, 128), f32),
        in_specs=[pl.BlockSpec((None, BLK, 2 * BLK), lambda h: (h, 0, 0))] * nl
                 + [pl.BlockSpec((None, BLK, 2 * BLK), lambda h: (h // 8, 0, 0))],
        out_specs=pl.BlockSpec((None, 1, 128), lambda h: (h, 0, 0)),
        compiler_params=_cp("arbitrary"),
    )(*dsaccs, buckets)


def _ffn_fwd(x, mod9, g3, wg, wu, wd, sub):
    S = x.shape[0]

    def body(x_ref, mod_ref, g_ref, wg_ref, wu_ref, wd_ref, xo_ref, h_ref, ga_ref, sa_ref, hid_ref, y_ref, acc):
        j = pl.program_id(1)

        @pl.when(j == 0)
        def _():
            h, _, _ = _norm_fwd(x_ref[...], g_ref[sub:sub + 1, :], mod_ref[3 * sub:3 * sub + 1, :], mod_ref[3 * sub + 1:3 * sub + 2, :])
            h_ref[...] = h.astype(bf16)
            acc[...] = jnp.zeros_like(acc)

        h = h_ref[...]
        a = _dot(h, wg_ref[...])
        u = _dot(h, wu_ref[...])
        sg = _sigmoid(a)
        sil = a * sg
        ga_ref[...] = (u * (sg * (1.0 + a * (1.0 - sg)))).astype(bf16)
        sa_ref[...] = sil.astype(bf16)
        hid_ref[...] = (sil * u).astype(bf16)
        acc[...] += _dot(hid_ref[...], wd_ref[...])

        @pl.when(j == N_CHIPS - 1)
        def _():
            y = acc[...]
            y_ref[...] = y.astype(bf16)
            xo_ref[...] = x_ref[...] + 0.5 * mod_ref[3 * sub + 2:3 * sub + 3, :] * y

    row = pl.BlockSpec((TMF, D), lambda i, j: (i, 0))
    hidb = pl.BlockSpec((None, TMF, FB), lambda i, j: (j, i, 0))
    hids = SDS((N_CHIPS, S, FB), bf16)
    return pl.pallas_call(
        body, name="ffn_fwd", grid=(S // TMF, N_CHIPS),
        out_shape=[SDS((S, D), f32), SDS((S, D), bf16), hids, hids, hids, SDS((S, D), bf16)],
        in_specs=[row, pl.BlockSpec((9, D), lambda i, j: (0, 0)), pl.BlockSpec((3, D), lambda i, j: (0, 0)),
                  pl.BlockSpec((None, D, FB), lambda i, j: (j, 0, 0)), pl.BlockSpec((None, D, FB), lambda i, j: (j, 0, 0)),
                  pl.BlockSpec((FB, D), lambda i, j: (j, 0))],
        out_specs=[row, row, hidb, hidb, hidb, row],
        scratch_shapes=[pltpu.VMEM((TMF, D), f32)],
        compiler_params=_cp("arbitrary", "arbitrary"),
    )(x, mod9, g3, wg, wu, wd)


def _ffn_bwd1(dxo, x, mod9, g3, y, ga, sa, wg, wu, wd, sub):
    S = x.shape[0]

    def body(dxo_ref, x_ref, mod_ref, g_ref, y_ref, ga_ref, sa_ref, wg_ref, wu_ref, wd_ref,
             dxi_ref, da_ref, du_ref, dy_ref, sm_ref, acc):
        i, j = pl.program_id(0), pl.program_id(1)
        gate = mod_ref[3 * sub + 2:3 * sub + 3, :]

        @pl.when((i == 0) & (j == 0))
        def _():
            sm_ref[...] = jnp.zeros_like(sm_ref)

        @pl.when(j == 0)
        def _():
            dxo_v = dxo_ref[...]
            dy_ref[...] = (0.5 * gate * dxo_v).astype(bf16)
            sm_ref[2:3, :] += jnp.sum(0.5 * y_ref[...].astype(f32) * dxo_v, axis=0, keepdims=True)
            acc[...] = jnp.zeros_like(acc)

        dhid = _dot_nt(dy_ref[...], wd_ref[...])
        da = (dhid * ga_ref[...].astype(f32)).astype(bf16)
        du = (dhid * sa_ref[...].astype(f32)).astype(bf16)
        da_ref[...] = da
        du_ref[...] = du
        acc[...] += _dot_nt(da, wg_ref[...]) + _dot_nt(du, wu_ref[...])

        @pl.when(j == N_CHIPS - 1)
        def _():
            g = g_ref[sub:sub + 1, :]
            scale = mod_ref[3 * sub + 1:3 * sub + 2, :]
            _, xhat, rstd = _norm_fwd(x_ref[...], g, mod_ref[3 * sub:3 * sub + 1, :], scale)
            dx, dshift, dscale, dg = _norm_bwd(acc[...], xhat, rstd, g, scale)
            dxi_ref[...] = dxo_ref[...] + dx
            sm_ref[0:1, :] += dshift
            sm_ref[1:2, :] += dscale
            sm_ref[3:4, :] += dg

    row = pl.BlockSpec((TM, D), lambda i, j: (i, 0))
    hidb = pl.BlockSpec((None, TM, FB), lambda i, j: (j, i, 0))
    wcol = pl.BlockSpec((None, D, FB), lambda i, j: (j, 0, 0))
    return pl.pallas_call(
        body, name="ffn_bwd1", grid=(S // TM, N_CHIPS),
        out_shape=[SDS((S, D), f32), SDS((N_CHIPS, S, FB), bf16), SDS((N_CHIPS, S, FB), bf16), SDS((S, D), bf16), SDS((8, D), f32)],
        in_specs=[row, row, pl.BlockSpec((9, D), lambda i, j: (0, 0)), pl.BlockSpec((3, D), lambda i, j: (0, 0)), row,
                  hidb, hidb, wcol, wcol, pl.BlockSpec((FB, D), lambda i, j: (j, 0))],
        out_specs=[row, hidb, hidb, row, pl.BlockSpec((8, D), lambda i, j: (0, 0))],
        scratch_shapes=[pltpu.VMEM((TM, D), f32)],
        compiler_params=_cp("arbitrary", "arbitrary"),
    )(dxo, x, mod9, g3, y, ga, sa, wg, wu, wd)


def _ffn_bwd2(h, da, du, hid, dy):
    S = h.shape[0]
    ni = S // TMW

    def body(h_ref, da_ref, du_ref, hid_ref, dy_ref, dwg_ref, dwu_ref, dwd_ref, ag, au, ad):
        i = pl.program_id(1)

        @pl.when(i == 0)
        def _():
            ag[...] = jnp.zeros_like(ag)
            au[...] = jnp.zeros_like(au)
            ad[...] = jnp.zeros_like(ad)

        hv = h_ref[...]
        ag[...] += _dot_tn(hv, da_ref[...])
        au[...] += _dot_tn(hv, du_ref[...])
        ad[...] += _dot_tn(hid_ref[...], dy_ref[...])

        @pl.when(i == ni - 1)
        def _():
            dwg_ref[...] = ag[...].astype(bf16)
            dwu_ref[...] = au[...].astype(bf16)
            dwd_ref[...] = ad[...].astype(bf16)

    row = pl.BlockSpec((TMW, D), lambda j, i: (i, 0))
    hidb = pl.BlockSpec((None, TMW, FB), lambda j, i: (j, i, 0))
    wcol = pl.BlockSpec((None, D, FB), lambda j, i: (j, 0, 0))
    return pl.pallas_call(
        body, name="ffn_bwd2", grid=(N_CHIPS, ni),
        out_shape=[SDS((N_CHIPS, D, FB), bf16), SDS((N_CHIPS, D, FB), bf16), SDS((N_CHIPS * FB, D), bf16)],
        in_specs=[row, hidb, hidb, hidb, row],
        out_specs=[wcol, wcol, pl.BlockSpec((FB, D), lambda j, i: (j, 0))],
        scratch_shapes=[pltpu.VMEM((D, FB), f32), pltpu.VMEM((D, FB), f32), pltpu.VMEM((FB, D), f32)],
        compiler_params=_cp("arbitrary", "arbitrary"),
    )(h, da, du, hid, dy)


def _mix_qkv(x, mod9, g3, win):
    S = x.shape[0]

    def body(x_ref, mod_ref, g_ref, w_ref, h_ref, o_ref):
        @pl.when(pl.program_id(1) == 0)
        def _():
            h, _, _ = _norm_fwd(x_ref[...], g_ref[1:2, :], mod_ref[3:4, :], mod_ref[4:5, :])
            h_ref[...] = h.astype(bf16)

        o_ref[...] = _dot(h_ref[...], w_ref[...])

    row = pl.BlockSpec((TMP, D), lambda i, j: (i, 0))
    return pl.pallas_call(
        body, name="mix_qkv", grid=(S // TMP, QKV_W // CB),
        out_shape=[SDS((S, D), bf16), SDS((S, QKV_W), f32)],
        in_specs=[row, pl.BlockSpec((9, D), lambda i, j: (0, 0)), pl.BlockSpec((3, D), lambda i, j: (0, 0)),
                  pl.BlockSpec((D, CB), lambda i, j: (0, j))],
        out_specs=[row, pl.BlockSpec((TMP, CB), lambda i, j: (i, j))],
        compiler_params=_cp("arbitrary", "arbitrary"),
    )(x, mod9, g3, win)


def _mix_rest(h, win):
    S = h.shape[0]
    off = QKV_W // CB

    def body(h_ref, w_ref, o_ref):
        o_ref[...] = _dot(h_ref[...], w_ref[...]).astype(bf16)

    return pl.pallas_call(
        body, name="mix_rest", grid=(S // TMP, REST_W // CB),
        out_shape=SDS((S, REST_W), bf16),
        in_specs=[pl.BlockSpec((TMP, D), lambda i, j: (i, 0)), pl.BlockSpec((D, CB), lambda i, j: (0, off + j))],
        out_specs=pl.BlockSpec((TMP, CB), lambda i, j: (i, j)),
        compiler_params=_cp("arbitrary", "arbitrary"),
    )(h, win)


def _attn_fwd(qkv, bias, g):
    S = qkv.shape[0]
    d = DILATIONS[g]
    nq = Q_BLOCKS[g]
    Rb = BLK * d
    R = Rb * nq
    nb = S // R
    qb, kb, vb = 4 * g, 12 + 4 * g, 24 + 4 * g

    def body(q_ref, kc_ref, kp_ref, vc_ref, vp_ref, b_ref, o_ref, l_ref):
        n = pl.program_id(1)
        col = lax.broadcasted_iota(jnp.int32, (BLK, 2 * BLK), 1)
        first = jnp.where((col < BLK) & (n == 0), NEG, 0.0)
        head0 = lax.broadcasted_iota(jnp.int32, (1, 2 * HD), 1) < HD

        def one(b, r):
            sl = pl.ds(b * Rb + r, BLK, stride=d)
            q = q_ref[sl, :]
            if b == 0:
                kp, vp = kp_ref[pl.ds(r, BLK, stride=d), :], vp_ref[pl.ds(r, BLK, stride=d), :]
            else:
                before = pl.ds((b - 1) * Rb + r, BLK, stride=d)
                kp, vp = kc_ref[before, :], vc_ref[before, :]
            kk = jnp.concatenate([kp, kc_ref[sl, :]], axis=0).astype(bf16)
            vv = jnp.concatenate([vp, vc_ref[sl, :]], axis=0).astype(bf16)
            os, ls = [], []
            for hh in range(2):
                qm = jnp.where(head0 if hh == 0 else ~head0, q, 0.0).astype(bf16)
                s = _dot_nt(qm, kk) * SCALE + b_ref[hh]
                if b == 0:
                    s = s + first
                m = jnp.max(s, axis=-1, keepdims=True)
                p = jnp.exp(s - m)
                l = jnp.sum(p, axis=-1, keepdims=True)
                os.append(_dot(p.astype(bf16), vv) / l)
                ls.append(m + jnp.log(l))
            o_ref[sl, :] = jnp.where(head0, os[0], os[1])
            l_ref[sl, :] = jnp.where(head0, ls[0], ls[1])

        for b in range(nq):
            if d == 1:
                one(b, 0)
            else:
                lax.fori_loop(0, d, lambda r, carry, b=b: (one(b, r), carry)[1], 0, unroll=4)

    def blk(cb, prev):
        if prev:
            return pl.BlockSpec((Rb, 128), lambda hp, n: (jnp.maximum(n * nq - 1, 0), cb + hp))
        return pl.BlockSpec((R, 128), lambda hp, n: (n, cb + hp))

    outb = pl.BlockSpec((R, 128), lambda hp, n: (n, hp))
    return pl.pallas_call(
        body, name=f"attn_fwd_d{d}", grid=(4, nb),
        out_shape=[SDS((S, 512), f32), SDS((S, 512), f32)],
        in_specs=[blk(qb, False), blk(kb, False), blk(kb, True), blk(vb, False), blk(vb, True),
                  pl.BlockSpec((2, BLK, 2 * BLK), lambda hp, n: (4 * g + hp, 0, 0))],
        out_specs=[outb, outb],
        compiler_params=_cp("arbitrary", "arbitrary"),
    )(qkv, qkv, qkv, qkv, qkv, bias)


def _attn_bwd(qkv, do, o, lse, bias, dq_all, dk_all, dv_all, g):
    S = qkv.shape[0]
    d = DILATIONS[g]
    nq = Q_BLOCKS[g]
    Rb = BLK * d
    R = Rb * nq
    nb = S // R
    qb, kb, vb = 4 * g, 12 + 4 * g, 24 + 4 * g

    def body(q_ref, kc_ref, kp_ref, vc_ref, vp_ref, do_ref, o_ref, l_ref, b_ref, dqi, dki, dvi,
             dq_ref, dk_ref, dv_ref, ds_ref, ck, cv, tk, tv):
        n = pl.program_id(1)
        col = lax.broadcasted_iota(jnp.int32, (BLK, 2 * BLK), 1)
        first = jnp.where((col < BLK) & (n == 0), NEG, 0.0)

        @pl.when(n == 0)
        def _():
            ck[...] = jnp.zeros_like(ck)
            cv[...] = jnp.zeros_like(cv)
            ds_ref[...] = jnp.zeros_like(ds_ref)

        @pl.when(n < nb)
        def _():
            head0 = lax.broadcasted_iota(jnp.int32, (1, 2 * HD), 1) < HD

            def one(b, r):
                sl = pl.ds(b * Rb + r, BLK, stride=d)
                before = pl.ds((max(b, 1) - 1) * Rb + r, BLK, stride=d)
                q = q_ref[sl, :]
                if b == 0:
                    kp, vp = kp_ref[pl.ds(r, BLK, stride=d), :], vp_ref[pl.ds(r, BLK, stride=d), :]
                else:
                    kp, vp = kc_ref[before, :], vc_ref[before, :]
                kk = jnp.concatenate([kp, kc_ref[sl, :]], axis=0).astype(bf16)
                vv = jnp.concatenate([vp, vc_ref[sl, :]], axis=0).astype(bf16)
                dov, lv = do_ref[sl, :], l_ref[sl, :]
                prod = dov * o_ref[sl, :]
                qb, dob = q.astype(bf16), dov.astype(bf16)
                dqs, dks, dvs = [], [], []
                for hh in range(2):
                    msk = head0 if hh == 0 else ~head0
                    qm = jnp.where(msk, q, 0.0).astype(bf16)
                    dom = jnp.where(msk, dov, 0.0).astype(bf16)
                    dsum = jnp.sum(jnp.where(msk, prod, 0.0), axis=-1, keepdims=True)
                    s = _dot_nt(qm, kk) * SCALE + b_ref[hh]
                    if b == 0:
                        s = s + first
                    p = jnp.exp(s - lv[:, HD * hh:HD * hh + 1])
                    ds = p * (_dot_nt(dom, vv) - dsum)
                    ds_ref[hh] += ds
                    dsb = ds.astype(bf16)
                    dqs.append(_dot(dsb, kk) * SCALE)
                    dks.append(_dot_tn(dsb, qb) * SCALE)
                    dvs.append(_dot_tn(p.astype(bf16), dob))
                dq_ref[sl, :] = jnp.where(head0, dqs[0], dqs[1])
                dk = jnp.where(head0, dks[0], dks[1])
                dv = jnp.where(head0, dvs[0], dvs[1])
                tk[sl, :] = dk[BLK:]
                tv[sl, :] = dv[BLK:]
                if b == 0:
                    prev_rows = pl.ds((nq - 1) * Rb + r, BLK, stride=d)
                    ck[prev_rows, :] += dk[:BLK]
                    cv[prev_rows, :] += dv[:BLK]
                else:
                    tk[before, :] += dk[:BLK]
                    tv[before, :] += dv[:BLK]

            for b in range(nq):
                if d == 1:
                    one(b, 0)
                else:
                    lax.fori_loop(0, d, lambda r, carry, b=b: (one(b, r), carry)[1], 0, unroll=4)
            dk_ref[...] = ck[...]
            dv_ref[...] = cv[...]
            ck[...] = tk[...]
            cv[...] = tv[...]

        @pl.when(n == nb)
        def _():
            dk_ref[...] = ck[...]
            dv_ref[...] = cv[...]

    last = nb - 1

    def blk(cb, prev):
        if prev:
            return pl.BlockSpec((Rb, 128), lambda hp, n: (jnp.maximum(jnp.minimum(n, last) * nq - 1, 0), cb + hp))
        return pl.BlockSpec((R, 128), lambda hp, n: (jnp.minimum(n, last), cb + hp))

    cur = pl.BlockSpec((R, 128), lambda hp, n: (jnp.minimum(n, last), hp))
    anyspec = pl.BlockSpec(memory_space=pl.ANY)
    dqo = pl.BlockSpec((R, 128), lambda hp, n: (jnp.minimum(n, last), 4 * g + hp))
    dko = pl.BlockSpec((R, 128), lambda hp, n: (jnp.maximum(n - 1, 0), 4 * g + hp))
    return pl.pallas_call(
        body, name=f"attn_bwd_d{d}", grid=(4, nb + 1),
        out_shape=[SDS((S, 1536), f32), SDS((S, 1536), f32), SDS((S, 1536), f32), SDS((8, BLK, 2 * BLK), f32)],
        in_specs=[blk(qb, False), blk(kb, False), blk(kb, True), blk(vb, False), blk(vb, True), cur, cur, cur,
                  pl.BlockSpec((2, BLK, 2 * BLK), lambda hp, n: (4 * g + hp, 0, 0)), anyspec, anyspec, anyspec],
        out_specs=[dqo, dko, dko, pl.BlockSpec((2, BLK, 2 * BLK), lambda hp, n: (hp, 0, 0))],
        scratch_shapes=[pltpu.VMEM((R, 128), f32)] * 4,
        input_output_aliases={9: 0, 10: 1, 11: 2},
        compiler_params=_cp("arbitrary", "arbitrary"),
    )(qkv, qkv, qkv, qkv, qkv, do, o, lse, bias, dq_all, dk_all, dv_all)


def _conv_z(cc, ch, hc, hh, cw_ref, first):
    halo = jnp.where(first, 0.0, hc.astype(f32) * hh.astype(f32))
    T = jnp.concatenate([halo, cc * ch], axis=0)
    z = cw_ref[2:3, :] * T + cw_ref[1:2, :] * pltpu.roll(T, 1, 0) + cw_ref[0:1, :] * pltpu.roll(T, 2, 0)
    return T, z[HALO:]


def _rest_specs(tm, with_next):
    per = tm // HALO
    specs = [pl.BlockSpec((tm, D), functools.partial(lambda i, k: (i, k), k=k)) for k in range(5)]
    specs += [pl.BlockSpec((HALO, D), functools.partial(lambda i, k: (jnp.maximum(i * per - 1, 0), k), k=k)) for k in (1, 2)]
    return specs


def _mix_out_fwd(x, mod9, rest, ogs, lgs, cw, wco, wao, wo):
    S = x.shape[0]
    tm = TMX

    def body(x_ref, mod_ref, cb_ref, cc_ref, ch_ref, gc_ref, ga_ref, hc_ref, hh_ref,
             o0, o1, o2, l0, l1, l2, cw_ref, wco_ref, wao_ref, wo_ref,
             xo_ref, o_ref, lse_ref, yc_ref, ya_ref, out_ref):
        i = pl.program_id(0)
        lv = [l0[...], l1[...], l2[...]]
        mx = jnp.maximum(jnp.maximum(lv[0], lv[1]), lv[2])
        es = [jnp.exp(l - mx) for l in lv]
        den = es[0] + es[1] + es[2]
        o = (es[0] / den) * o0[...] + (es[1] / den) * o1[...] + (es[2] / den) * o2[...]
        o_ref[...] = o
        lse_ref[...] = mx + jnp.log(den)
        _, z = _conv_z(cc_ref[...].astype(f32), ch_ref[...].astype(f32), hc_ref[...], hh_ref[...], cw_ref, i == 0)
        p = (cb_ref[...].astype(f32) * z).astype(bf16)
        yc = _dot(p, wco_ref[...])
        ya = _dot(o.astype(bf16), wao_ref[...])
        yc_ref[...] = yc.astype(bf16)
        ya_ref[...] = ya.astype(bf16)
        merged = _sigmoid(gc_ref[...].astype(f32)) * yc + _sigmoid(ga_ref[...].astype(f32)) * ya
        out = _dot(merged.astype(bf16), wo_ref[...])
        out_ref[...] = out.astype(bf16)
        xo_ref[...] = x_ref[...] + mod_ref[5:6, :] * out

    row = pl.BlockSpec((tm, D), lambda i: (i, 0))
    att = pl.BlockSpec((tm, 512), lambda i: (i, 0))
    full = lambda shp: pl.BlockSpec(shp, lambda i: (0, 0))
    return pl.pallas_call(
        body, name="mix_out_fwd", grid=(S // tm,),
        out_shape=[SDS((S, D), f32), SDS((S, 512), f32), SDS((S, 512), f32), SDS((S, D), bf16), SDS((S, D), bf16), SDS((S, D), bf16)],
        in_specs=[row, full((9, D))] + _rest_specs(tm, False) + [att] * 6 + [full((3, D)), full((D, D)), full((512, D)), full((D, D))],
        out_specs=[row, att, att, row, row, row],
        compiler_params=_cp("arbitrary"),
    )(x, mod9, *([rest] * 7), *ogs, *lgs, cw, wco, wao, wo)


def _mix_out_bwd(dxo, mod9, outv, yc, ya, rest, o, cw, wco, wao, wo):
    S = dxo.shape[0]
    tm = TMX
    ni = S // tm

    def body(dxo_ref, mod_ref, out_ref, yc_ref, ya_ref, cb_ref, cc_ref, ch_ref, gc_ref, ga_ref, hc_ref, hh_ref,
             o_ref, cw_ref, wco_ref, wao_ref, wo_ref,
             dp_ref, dg2_ref, do_ref, dwco_ref, dwao_ref, dwo_ref, sm_ref, aco, aao, ao):
        i = pl.program_id(0)

        @pl.when(i == 0)
        def _():
            sm_ref[...] = jnp.zeros_like(sm_ref)
            aco[...] = jnp.zeros_like(aco)
            aao[...] = jnp.zeros_like(aao)
            ao[...] = jnp.zeros_like(ao)

        dxo_v = dxo_ref[...]
        sm_ref[2:3, :] += jnp.sum(out_ref[...].astype(f32) * dxo_v, axis=0, keepdims=True)
        dout = (mod_ref[5:6, :] * dxo_v).astype(bf16)
        dmerged = _dot_nt(dout, wo_ref[...])
        sc, sa = _sigmoid(gc_ref[...].astype(f32)), _sigmoid(ga_ref[...].astype(f32))
        ycv, yav = yc_ref[...].astype(f32), ya_ref[...].astype(f32)
        ao[...] += _dot_tn((sc * ycv + sa * yav).astype(bf16), dout)
        dyc = (dmerged * sc).astype(bf16)
        dya = (dmerged * sa).astype(bf16)
        dg2_ref[:, :D] = (dmerged * ycv * sc * (1.0 - sc)).astype(bf16)
        dg2_ref[:, D:] = (dmerged * yav * sa * (1.0 - sa)).astype(bf16)
        dp_ref[...] = _dot_nt(dyc, wco_ref[...]).astype(bf16)
        _, z = _conv_z(cc_ref[...].astype(f32), ch_ref[...].astype(f32), hc_ref[...], hh_ref[...], cw_ref, i == 0)
        aco[...] += _dot_tn((cb_ref[...].astype(f32) * z).astype(bf16), dyc)
        do_ref[...] = _dot_nt(dya, wao_ref[...])
        aao[...] += _dot_tn(o_ref[...].astype(bf16), dya)

        @pl.when(i == ni - 1)
        def _():
            dwco_ref[...] = aco[...].astype(bf16)
            dwao_ref[...] = aao[...].astype(bf16)
            dwo_ref[...] = ao[...].astype(bf16)

    row = pl.BlockSpec((tm, D), lambda i: (i, 0))
    att = pl.BlockSpec((tm, 512), lambda i: (i, 0))
    full = lambda shp: pl.BlockSpec(shp, lambda i: (0, 0))
    return pl.pallas_call(
        body, name="mix_out_bwd", grid=(ni,),
        out_shape=[SDS((S, D), bf16), SDS((S, 2 * D), bf16), SDS((S, 512), f32),
                   SDS((D, D), bf16), SDS((512, D), bf16), SDS((D, D), bf16), SDS((8, D), f32)],
        in_specs=[row, full((9, D)), row, row, row] + _rest_specs(tm, False) + [att, full((3, D)), full((D, D)), full((512, D)), full((D, D))],
        out_specs=[row, pl.BlockSpec((tm, 2 * D), lambda i: (i, 0)), att, full((D, D)), full((512, D)), full((D, D)), full((8, D))],
        scratch_shapes=[pltpu.VMEM((D, D), f32), pltpu.VMEM((512, D), f32), pltpu.VMEM((D, D), f32)],
        compiler_params=_cp("arbitrary"),
    )(dxo, mod9, outv, yc, ya, *([rest] * 7), o, cw, wco, wao, wo)


def _conv_bwd(dp, rest, cw):
    S = dp.shape[0]
    tm = TM
    per = tm // HALO
    nh = S // HALO
    ni = S // tm

    def body(dp_ref, dpn_ref, cb_ref, cbn_ref, cc_ref, ch_ref, hc_ref, hh_ref, cw_ref, d3_ref, sm_ref):
        i = pl.program_id(0)

        @pl.when(i == 0)
        def _():
            sm_ref[...] = jnp.zeros_like(sm_ref)

        cc, ch = cc_ref[...].astype(f32), ch_ref[...].astype(f32)
        T, z = _conv_z(cc, ch, hc_ref[...], hh_ref[...], cw_ref, i == 0)
        dpv = dp_ref[...].astype(f32)
        cbv = cb_ref[...].astype(f32)
        dz = dpv * cbv
        dzn = jnp.where(i == ni - 1, 0.0, dpn_ref[...].astype(f32) * cbn_ref[...].astype(f32))
        E = jnp.concatenate([dz, dzn], axis=0)
        ne = tm + HALO
        dT = cw_ref[2:3, :] * E + cw_ref[1:2, :] * pltpu.roll(E, ne - 1, 0) + cw_ref[0:1, :] * pltpu.roll(E, ne - 2, 0)
        dT = dT[:tm]
        d3_ref[:, :D] = (dpv * z).astype(bf16)
        d3_ref[:, D:2 * D] = (dT * ch).astype(bf16)
        d3_ref[:, 2 * D:] = (dT * cc).astype(bf16)
        sm_ref[2:3, :] += jnp.sum(dz * T[HALO:], axis=0, keepdims=True)
        sm_ref[1:2, :] += jnp.sum(dz * pltpu.roll(T, 1, 0)[HALO:], axis=0, keepdims=True)
        sm_ref[0:1, :] += jnp.sum(dz * pltpu.roll(T, 2, 0)[HALO:], axis=0, keepdims=True)

    row = pl.BlockSpec((tm, D), lambda i: (i, 0))
    nxt = pl.BlockSpec((HALO, D), lambda i: (jnp.minimum((i + 1) * per, nh - 1), 0))
    col = lambda k: pl.BlockSpec((tm, D), lambda i: (i, k))
    prv = lambda k: pl.BlockSpec((HALO, D), lambda i: (jnp.maximum(i * per - 1, 0), k))
    return pl.pallas_call(
        body, name="conv_bwd", grid=(ni,),
        out_shape=[SDS((S, 3 * D), bf16), SDS((8, D), f32)],
        in_specs=[row, nxt, col(0), nxt, col(1), col(2), prv(1), prv(2), pl.BlockSpec((3, D), lambda i: (0, 0))],
        out_specs=[pl.BlockSpec((tm, 3 * D), lambda i: (i, 0)), pl.BlockSpec((8, D), lambda i: (0, 0))],
        compiler_params=_cp("arbitrary"),
    )(dp, dp, rest, rest, rest, rest, rest, rest, cw)


_DU_RANGES = ((0, 3), (3, 6), (6, 9), (9, 15), (15, 19))
N_CBLK = IN_W // CB


def _mix_in_bwd_dh(dxo, x, mod9, g3, dus, win):
    S = x.shape[0]

    def body(dxo_ref, x_ref, mod_ref, g_ref, s0, s1, s2, s3, s4, w_ref, dxi_ref, sm_ref, acc):
        i, kb = pl.program_id(0), pl.program_id(1)

        @pl.when((i == 0) & (kb == 0))
        def _():
            sm_ref[...] = jnp.zeros_like(sm_ref)

        @pl.when(kb == 0)
        def _():
            acc[...] = jnp.zeros_like(acc)

        for src, (lo, hi) in zip((s0, s1, s2, s3, s4), _DU_RANGES):
            @pl.when((kb >= lo) & (kb < hi))
            def _(src=src):
                acc[...] += _dot_nt(src[...].astype(bf16), w_ref[...])

        @pl.when(kb == N_CBLK - 1)
        def _():
            g, scale = g_ref[1:2, :], mod_ref[4:5, :]
            _, xhat, rstd = _norm_fwd(x_ref[...], g, mod_ref[3:4, :], scale)
            dx, dshift, dscale, dg = _norm_bwd(acc[...], xhat, rstd, g, scale)
            dxi_ref[...] = dxo_ref[...] + dx
            sm_ref[0:1, :] += dshift
            sm_ref[1:2, :] += dscale
            sm_ref[3:4, :] += dg

    row = pl.BlockSpec((TMP, D), lambda i, kb: (i, 0))

    def src_spec(lo, hi):
        return pl.BlockSpec((TMP, CB), lambda i, kb: (i, jnp.clip(kb - lo, 0, hi - lo - 1)))

    return pl.pallas_call(
        body, name="mix_in_bwd_dh", grid=(S // TMP, N_CBLK),
        out_shape=[SDS((S, D), f32), SDS((8, D), f32)],
        in_specs=[row, row, pl.BlockSpec((9, D), lambda i, kb: (0, 0)), pl.BlockSpec((3, D), lambda i, kb: (0, 0))]
                 + [src_spec(lo, hi) for lo, hi in _DU_RANGES] + [pl.BlockSpec((D, CB), lambda i, kb: (0, kb))],
        out_specs=[row, pl.BlockSpec((8, D), lambda i, kb: (0, 0))],
        scratch_shapes=[pltpu.VMEM((TMP, D), f32)],
        compiler_params=_cp("arbitrary", "arbitrary"),
    )(dxo, x, mod9, g3, *dus, win)


def _mix_in_bwd_dw(h, dus):
    S = h.shape[0]
    ni = S // TMW

    def body(h_ref, s0, s1, s2, s3, s4, dw_ref, acc):
        kb, i = pl.program_id(0), pl.program_id(1)

        @pl.when(i == 0)
        def _():
            acc[...] = jnp.zeros_like(acc)

        for src, (lo, hi) in zip((s0, s1, s2, s3, s4), _DU_RANGES):
            @pl.when((kb >= lo) & (kb < hi))
            def _(src=src):
                rows = pl.ds(pl.multiple_of(i * TMW, TMW), TMW)
                acc[...] += _dot_tn(h_ref[rows, :], src[...].astype(bf16))

        @pl.when(i == ni - 1)
        def _():
            dw_ref[...] = acc[...].astype(bf16)

    def src_spec(lo, hi):
        def imap(kb, i):
            on = (kb >= lo) & (kb < hi)
            return (jnp.where(on, i, 0), jnp.clip(kb - lo, 0, hi - lo - 1))
        return pl.BlockSpec((TMW, CB), imap)

    return pl.pallas_call(
        body, name="mix_in_bwd_dw", grid=(N_CBLK, ni),
        out_shape=SDS((D, IN_W), bf16),
        in_specs=[pl.BlockSpec((S, D), lambda kb, i: (0, 0))] + [src_spec(lo, hi) for lo, hi in _DU_RANGES],
        out_specs=pl.BlockSpec((D, CB), lambda kb, i: (0, kb)),
        scratch_shapes=[pltpu.VMEM((D, CB), f32)],
        compiler_params=_cp("arbitrary", "arbitrary"),
    )(h, *dus)


def _loss_head(x, fg, tgt):
    S = x.shape[0]

    def body(x_ref, g_ref, t_ref, ls_ref, dx_ref, sm_ref):
        i = pl.program_id(0)

        @pl.when(i == 0)
        def _():
            ls_ref[...] = jnp.zeros_like(ls_ref)
            sm_ref[...] = jnp.zeros_like(sm_ref)

        xv, g = x_ref[...], g_ref[...]
        rstd = lax.rsqrt(jnp.mean(xv * xv, axis=-1, keepdims=True) + EPS)
        xhat = xv * rstd
        e = xhat * g - t_ref[...]
        ls_ref[...] += 0.5 * jnp.sum(jnp.mean(e * e, axis=-1, keepdims=True))
        dy = e * (1.0 / D)
        sm_ref[0:1, :] += jnp.sum(dy * xhat, axis=0, keepdims=True)
        dxh = dy * g
        dx_ref[...] = rstd * (dxh - xhat * jnp.mean(dxh * xhat, axis=-1, keepdims=True))

    row = pl.BlockSpec((TM, D), lambda i: (i, 0))
    return pl.pallas_call(
        body, name="loss_head", grid=(S // TM,),
        out_shape=[SDS((8, 128), f32), SDS((S, D), f32), SDS((8, D), f32)],
        in_specs=[row, pl.BlockSpec((1, D), lambda i: (0, 0)), row],
        out_specs=[pl.BlockSpec((8, 128), lambda i: (0, 0)), row, pl.BlockSpec((8, D), lambda i: (0, 0))],
        compiler_params=_cp("arbitrary"),
    )(x, fg, tgt)


def _adam(w, g, m, v):
    m2 = B1 * m + (1.0 - B1) * g
    v2 = B2 * v + (1.0 - B2) * (g * g)
    delta = -LR * ((m2 / BC1) / (jnp.sqrt(v2 / BC2) + AEPS) + WD * w)
    return delta, m2, v2


def _row_tile(rows, cols):
    for tr in (512, 352, 256, 128, 64):
        if rows % tr == 0 and tr * cols * 4 <= (5 << 18):
            return tr
    raise ValueError((rows, cols))


def _sum_slots(land):
    _, R, C = land.shape
    tr = _row_tile(R, C)

    def body(l_ref, t_ref):
        t = l_ref[0].astype(f32)
        for k in range(1, N_CHIPS):
            t = t + l_ref[k].astype(f32)
        t_ref[...] = t

    return pl.pallas_call(
        body, name="sum_slots", grid=(R // tr,),
        out_shape=SDS((R, C), f32),
        in_specs=[pl.BlockSpec((N_CHIPS, tr, C), lambda i: (0, i, 0))],
        out_specs=pl.BlockSpec((tr, C), lambda i: (i, 0)),
        compiler_params=_cp("arbitrary"),
    )(land)


def _adamw_pair(w2, m2, v2, ta, tb, outs, slot):
    R, C = ta.shape
    tr = _row_tile(R, C)
    nrt = R // tr

    def body(w_ref, m_ref, v_ref, ta_ref, tb_ref, g_in, d_in, m_in, v_in, g_ref, d_ref, mo_ref, vo_ref):
        g = ta_ref[...] + tb_ref[...]
        delta, mn, vn = _adam(w_ref[...], g, m_ref[...], v_ref[...])
        g_ref[...] = g
        d_ref[...] = delta
        mo_ref[...] = mn
        vo_ref[...] = vn

    big = pl.BlockSpec((tr, C), lambda i: (slot * nrt + i, 0))
    loc = pl.BlockSpec((tr, C), lambda i: (i, 0))
    anyspec = pl.BlockSpec(memory_space=pl.ANY)
    return pl.pallas_call(
        body, name="adamw_pair", grid=(nrt,),
        out_shape=[SDS(o.shape, f32) for o in outs],
        in_specs=[big, big, big, loc, loc] + [anyspec] * 4,
        out_specs=[big] * 4,
        input_output_aliases={5: 0, 6: 1, 7: 2, 8: 3},
        compiler_params=_cp("arbitrary"),
    )(w2, m2, v2, ta, tb, *outs)


def _adamw_small(w, g, m, v):
    def body(w_ref, g_ref, m_ref, v_ref, d_ref, mo_ref, vo_ref):
        delta, mn, vn = _adam(w_ref[...], g_ref[...], m_ref[...], v_ref[...])
        d_ref[...] = delta
        mo_ref[...] = mn
        vo_ref[...] = vn

    return pl.pallas_call(body, name="adamw_small", out_shape=[SDS(w.shape, f32)] * 3)(w, g, m, v)


def _ada_w_update(cs_all, dmod_sh, w, m, v):
    tr = 256

    def body(cs_ref, dm_ref, w_ref, m_ref, v_ref, g_ref, d_ref, mo_ref, vo_ref):
        g = _dot_tn(cs_ref[...].astype(bf16), dm_ref[...].astype(bf16))
        delta, mn, vn = _adam(w_ref[...], g, m_ref[...], v_ref[...])
        g_ref[...] = g
        d_ref[...] = delta
        mo_ref[...] = mn
        vo_ref[...] = vn

    blk = pl.BlockSpec((None, tr, ADA_SH), lambda l, i: (l, i, 0))
    return pl.pallas_call(
        body, name="ada_w_update", grid=(DEPTH, D // tr),
        out_shape=[SDS(w.shape, f32)] * 4,
        in_specs=[pl.BlockSpec((8, tr), lambda l, i: (0, i)), pl.BlockSpec((None, 8, ADA_SH), lambda l, i: (l, 0, 0)), blk, blk, blk],
        out_specs=[blk] * 4,
        compiler_params=_cp("arbitrary", "arbitrary"),
    )(cs_all, dmod_sh, w, m, v)


def _sum_devices(gathered):
    _, R, C = gathered.shape

    def body(g_ref, o_ref):
        t = g_ref[0]
        for k in range(1, 8):
            t = t + g_ref[k]
        o_ref[...] = t

    return pl.pallas_call(body, name="sum_devices", out_shape=SDS((R, C), f32))(gathered)


def _layer_fwd(x, mod9, g3, cw, getw, bias):
    W = {}

    def take(gname, after, mod9):
        w, tok = getw(gname, after)
        W.update(w)
        return mod9 if tok is None else mod9 + tok[0, 0]

    mod9 = take("A", x, mod9)
    x1, h1, a1, u1, hid1, y1 = _ffn_fwd(x, mod9, g3, W["wg0"], W["wu0"], W["wd0"], 0)
    mod9 = take("B", x1, mod9)
    hm, qkv = _mix_qkv(x1, mod9, g3, W["win"])
    rest = _mix_rest(hm, W["win"])
    ogs, lgs = [], []
    for g in range(3):
        og, lg = _attn_fwd(qkv, bias, g)
        ogs.append(og)
        lgs.append(lg)
    mod9 = take("C", ogs[2], mod9)
    x2, o, lse, yc, ya, outv = _mix_out_fwd(x1, mod9, rest, ogs, lgs, cw, W["wco"], W["wao"], W["wo"])
    mod9 = take("D", x2, mod9)
    x3, h3, a3, u3, hid3, y3 = _ffn_fwd(x2, mod9, g3, W["wg1"], W["wu1"], W["wd1"], 2)
    saved = dict(x0=x, x1=x1, x2=x2, h1=h1, a1=a1, u1=u1, hid1=hid1, y1=y1, hm=hm, qkv=qkv, rest=rest, o=o, lse=lse, yc=yc, ya=ya,
                 outv=outv, h3=h3, a3=a3, u3=u3, hid3=hid3, y3=y3)
    return x3, saved, W


def _layer_bwd(dx, sv, mod9, g3, cw, W, bias, emit):
    S = dx.shape[0]
    dw = {}

    def send(gname, mod9):
        tok = emit(gname, dw)
        return mod9 if tok is None else mod9 + tok[0, 0]

    dx2, da, du, dy, sm3 = _ffn_bwd1(dx, sv["x2"], mod9, g3, sv["y3"], sv["a3"], sv["u3"], W["wg1"], W["wu1"], W["wd1"], 2)
    dw["wg1"], dw["wu1"], dw["wd1"] = _ffn_bwd2(sv["h3"], da, du, sv["hid3"], dy)
    mod9 = send("D", mod9)
    dp, dg2, do, dw["wco"], dw["wao"], dw["wo"], smo = _mix_out_bwd(
        dx2, mod9, sv["outv"], sv["yc"], sv["ya"], sv["rest"], sv["o"], cw, W["wco"], W["wao"], W["wo"])
    mod9_c = send("C", mod9)
    cw = cw + (mod9_c - mod9)[0:1, :]
    mod9 = mod9_c
    d3, smc = _conv_bwd(dp, sv["rest"], cw)
    dq = lax.empty((S, 1536), f32)
    dk = lax.empty((S, 1536), f32)
    dv = lax.empty((S, 1536), f32)
    dsaccs = []
    for g in range(3):
        dq, dk, dv, dsg = _attn_bwd(sv["qkv"], do, sv["o"], sv["lse"], bias, dq, dk, dv, g)
        dsaccs.append(dsg)
    dus = (dq, dk, dv, d3, dg2)
    dx1, smm = _mix_in_bwd_dh(dx2, sv["x1"], mod9, g3, dus, W["win"])
    dw["win"] = _mix_in_bwd_dw(sv["hm"], dus)
    mod9 = send("B", mod9)
    dx0, da, du, dy, sm1 = _ffn_bwd1(dx1, sv["x0"], mod9, g3, sv["y1"], sv["a1"], sv["u1"], W["wg0"], W["wu0"], W["wd0"], 0)
    dw["wg0"], dw["wu0"], dw["wd0"] = _ffn_bwd2(sv["h1"], da, du, sv["hid1"], dy)
    send("A", mod9)
    dmod = jnp.concatenate([sm1[0:3], smm[0:2], smo[2:3], sm3[0:3]], axis=0)
    dng = jnp.concatenate([sm1[3:4], smm[3:4], sm3[3:4]], axis=0)
    return dx0, dmod, dng, smc[0:3], jnp.concatenate(dsaccs, axis=0)


def _chip_cols(a, chip, width):
    return lax.dynamic_slice_in_dim(a, chip * width, width, axis=a.ndim - 1)


def kernel(x, c, ada_w, ada_b, norm_g, ffn_w_gate, ffn_w_up, ffn_w_down, w_in, conv_w, w_conv_out, w_attn_out, w_o, rel_bias, final_g, loss_target, m_ada_w, m_ada_b, m_norm_g, m_ffn_w_gate, m_ffn_w_up, m_ffn_w_down, m_w_in, m_conv_w, m_w_conv_out, m_w_attn_out, m_w_o, m_rel_bias, m_final_g, v_ada_w, v_ada_b, v_norm_g, v_ffn_w_gate, v_ffn_w_up, v_ffn_w_down, v_w_in, v_conv_w, v_w_conv_out, v_w_attn_out, v_w_o, v_rel_bias, v_final_g):
    ix, iy, ic = lax.axis_index("x"), lax.axis_index("y"), lax.axis_index("c")
    chip = 2 * ix + iy
    dev = 4 * ix + 2 * iy + ic
    xs = x[0]
    S = xs.shape[0]
    qd = D // N_CHIPS

    chip_arr = jnp.reshape(chip, (1,)).astype(jnp.int32)
    names = [w[0] for w in WCLASSES]

    def layer_shards(l):
        return [(ffn_w_gate, (l, 0)), (ffn_w_up, (l, 0)), (ffn_w_down, (l, 0)), (ffn_w_gate, (l, 1)), (ffn_w_up, (l, 1)),
                (ffn_w_down, (l, 1)), (w_in, (l,)), (w_conv_out, (l,)), (w_attn_out, (l,)), (w_o, (l,))]

    started = {}
    extra_starts = {(0, "A"): [(0, "B")], (0, "B"): [(0, "C"), (0, "D"), (1, "A")]}

    def start_gather(l, gname, after):
        shards = layer_shards(l)
        started[(l, gname)] = _gather_group_start(f"l{l}{gname}", GROUPS[gname], [shards[q] for q in GROUPS[gname]], chip_arr, after)
        return started[(l, gname)][-1]

    tok0 = start_gather(0, "A", c)

    pad8 = lambda a: jnp.pad(a, ((0, -a.shape[0] % 8), (0, 0)))
    pack = jnp.concatenate([pad8(c + tok0[0:1, 0:1]), pad8(norm_g.reshape(3, D)), pad8(conv_w.reshape(3, D))], axis=0)
    g1 = _allgather_small(pack).reshape(8, 24, D)
    c_all = g1[:, 0]
    by_chip = g1[0::2]
    ng_full = jnp.concatenate([by_chip[j, 8:11].reshape(DEPTH, 3, qd) for j in range(N_CHIPS)], axis=-1)
    cw_full = jnp.concatenate([by_chip[j, 16:19].reshape(DEPTH, 3, qd) for j in range(N_CHIPS)], axis=-1)
    mod_sh, cs_all = _mod_shards(c_all, ada_w, _chip_cols(ada_b, chip, ADA_SH))
    g2 = _allgather_small(mod_sh.reshape(DEPTH * 8, ADA_SH)).reshape(8, DEPTH, 8, ADA_SH)
    mine = lax.dynamic_index_in_dim(g2[0::2], dev, axis=2, keepdims=False)
    mod = jnp.transpose(mine, (1, 0, 2)).reshape(DEPTH, 9, D)

    buckets = jnp.asarray(_bucket_table())
    bias = _bias_blocks(rel_bias, buckets)

    def make_getw(l):
        def getw(gname, after):
            full = _gather_group_wait(f"l{l}{gname}", GROUPS[gname], started[(l, gname)], after)
            tok = None
            for nl, ng in extra_starts.get((l, gname), []) + [(l + 1, gname)]:
                if nl < DEPTH and (nl, ng) not in started:
                    tok = start_gather(nl, ng, full[0] if tok is None else tok)
            return {names[q]: f for q, f in zip(GROUPS[gname], full)}, tok
        return getw

    Ws, saves = [], []
    xc = xs
    for l in range(DEPTH):
        xc, sv, W = _layer_fwd(xc, mod[l], ng_full[l], cw_full[l], make_getw(l), bias)
        Ws.append(W)
        saves.append(sv)

    ls, dx, smf = _loss_head(xc, final_g.reshape(1, D), loss_target[0])
    loss = lax.psum(ls[0, 0], ("x", "y", "c"))

    params = dict(wg=ffn_w_gate, wu=ffn_w_up, wd=ffn_w_down, win=w_in, wco=w_conv_out, wao=w_attn_out, wo=w_o)
    moms = dict(wg=m_ffn_w_gate, wu=m_ffn_w_up, wd=m_ffn_w_down, win=m_w_in, wco=m_w_conv_out, wao=m_w_attn_out, wo=m_w_o)
    vars_ = dict(wg=v_ffn_w_gate, wu=v_ffn_w_up, wd=v_ffn_w_down, win=v_w_in, wco=v_w_conv_out, wao=v_w_attn_out, wo=v_w_o)
    flat = lambda a: a.reshape(-1, a.shape[-1])
    big_out = {k: [lax.empty(flat(p).shape, f32) for _ in range(4)] for k, p in params.items()}
    dmods, dngs, dcws, dsaccs = [None] * DEPTH, [None] * DEPTH, [None] * DEPTH, [None] * DEPTH

    def finish(l, gname, started, after):
        group = GROUPS[gname]
        pieces, lands = _scatter_group_wait(f"l{l}{gname}", group, started, after)
        ts = [_sum_own_slots(pieces[i], lands[i], *_cls(q), chip_arr) for i, q in enumerate(group)]
        tsib = _swap_sibling(ts)
        for i, q in enumerate(group):
            name = names[q]
            key = name.rstrip("01")
            slot = 2 * l + int(name[-1]) if name[-1] in "01" else l
            big_out[key] = _adamw_pair(flat(params[key]), flat(moms[key]), flat(vars_[key]), ts[i], tsib[i], big_out[key], slot)

    pending, tok = [], None
    for l in reversed(range(DEPTH)):
        modl = mod[l] if tok is None else mod[l] + tok[0, 0]
        mine = []

        def emit(gname, dw, l=l, mine=mine):
            prev = mine[-1][2][-1] if mine else dx
            mine.append((l, gname, _scatter_group_start(f"l{l}{gname}", GROUPS[gname], [dw[names[q]] for q in GROUPS[gname]], prev)))
            return mine[-1][2][-1]

        dx, dmods[l], dngs[l], dcws[l], dsaccs[l] = _layer_bwd(dx, saves[l], modl, ng_full[l], cw_full[l], Ws[l], bias, emit)
        for pl_, pg, pst in pending:
            finish(pl_, pg, pst, dx)
        pending, tok = mine, mine[-1][2][-1]
    for pl_, pg, pst in pending:
        finish(pl_, pg, pst, dx)

    drb = jnp.transpose(_bias_grad(dsaccs, buckets)[:, 0, :NUM_BUCKETS])
    drb_row = jnp.pad(drb.reshape(1, NUM_BUCKETS * 24), ((0, 0), (0, D - NUM_BUCKETS * 24)))
    pack2 = jnp.concatenate([pad8(a) for a in dmods] + [pad8(a) for a in dngs] + [pad8(a) for a in dcws] + [smf, pad8(drb_row)], axis=0)
    n_rows = pack2.shape[0]
    g3 = _allgather_small(pack2).reshape(8, n_rows, D)
    tot = _sum_devices(g3)
    o_ng, o_cw, o_fg, o_rb = 16 * DEPTH, 24 * DEPTH, 32 * DEPTH, 32 * DEPTH + 8
    g_ada_b = jnp.stack([tot[16 * l:16 * l + 9] for l in range(DEPTH)]).reshape(DEPTH, 9 * D)
    g_norm_g = _chip_cols(jnp.stack([tot[o_ng + 8 * l:o_ng + 8 * l + 3] for l in range(DEPTH)]), chip, qd)
    g_conv_w = _chip_cols(jnp.stack([tot[o_cw + 8 * l:o_cw + 8 * l + 3] for l in range(DEPTH)]), chip, qd)
    g_final_g = tot[o_fg]
    g_rel_bias = tot[o_rb, :NUM_BUCKETS * 24].reshape(NUM_BUCKETS, 24)
    dmod_all = jnp.stack([g3[:, 16 * l:16 * l + 9].reshape(8, 9 * D) for l in range(DEPTH)])
    dmod_sh = _chip_cols(dmod_all, chip, ADA_SH)
    g_ada_w, d_ada_w, nm_ada_w, nv_ada_w = _ada_w_update(cs_all, dmod_sh, ada_w, m_ada_w, v_ada_w)

    def small(w, g, m, v):
        shp = w.shape
        to2 = lambda a: a.reshape(-1, shp[-1])
        return [o.reshape(shp) for o in _adamw_small(to2(w), to2(g), to2(m), to2(v))]

    d_ada_b, nm_ada_b, nv_ada_b = small(ada_b, g_ada_b, m_ada_b, v_ada_b)
    d_norm_g, nm_norm_g, nv_norm_g = small(norm_g, g_norm_g, m_norm_g, v_norm_g)
    d_conv_w, nm_conv_w, nv_conv_w = small(conv_w, g_conv_w, m_conv_w, v_conv_w)
    d_rel_bias, nm_rel_bias, nv_rel_bias = small(rel_bias, g_rel_bias, m_rel_bias, v_rel_bias)
    d_final_g, nm_final_g, nv_final_g = small(final_g, g_final_g, m_final_g, v_final_g)

    def big(key, which):
        return big_out[key][which].reshape(params[key].shape)

    grads = [g_ada_w, g_ada_b, g_norm_g, big("wg", 0), big("wu", 0), big("wd", 0), big("win", 0), g_conv_w, big("wco", 0),
             big("wao", 0), big("wo", 0), g_rel_bias, g_final_g]
    deltas = [d_ada_w, d_ada_b, d_norm_g, big("wg", 1), big("wu", 1), big("wd", 1), big("win", 1), d_conv_w, big("wco", 1),
              big("wao", 1), big("wo", 1), d_rel_bias, d_final_g]
    new_m = [nm_ada_w, nm_ada_b, nm_norm_g, big("wg", 2), big("wu", 2), big("wd", 2), big("win", 2), nm_conv_w, big("wco", 2),
             big("wao", 2), big("wo", 2), nm_rel_bias, nm_final_g]
    new_v = [nv_ada_w, nv_ada_b, nv_norm_g, big("wg", 3), big("wu", 3), big("wd", 3), big("win", 3), nv_conv_w, big("wco", 3),
             big("wao", 3), big("wo", 3), nv_rel_bias, nv_final_g]
    return (loss, dx[None], *grads, *deltas, *new_m, *new_v)
```

```python
import functools

import numpy as np
import jax
import jax.numpy as jnp
from jax import lax
from jax.experimental import pallas as pl
from jax.experimental.pallas import tpu as pltpu

f32, bf16 = jnp.float32, jnp.bfloat16
SDS = jax.ShapeDtypeStruct
MESH = pl.DeviceIdType.MESH

D = 1024
DEPTH = 4
N_CHIPS = 4
FB = 704
HD = 64
QKV_W = 4608
REST_W = 5120
IN_W = QKV_W + REST_W
WIN_SH = IN_W // N_CHIPS
ADA_SH = 9 * D // N_CHIPS
BLK = 128
DILATIONS = (1, 4, 16)
Q_BLOCKS = (4, 1, 1)
NUM_BUCKETS, MAX_DISTANCE = 32, 2048
EPS = 1e-6
NEG = -1e30
SCALE = HD ** -0.5
LR, B1, B2, AEPS, WD, STEP = 0.001, 0.9, 0.999, 1e-08, 0.01, 10
BC1 = 1.0 - B1 ** STEP
BC2 = 1.0 - B2 ** STEP
VMEM_LIMIT = 56 * 1024 * 1024
TM = 512
TMW = 1024
TMP = 1024
TMF = 1024
SH_STEP = 2
TMX = 256
HALO = 16
CB = 512


def _cp(*sem):
    return pltpu.CompilerParams(dimension_semantics=sem if sem else None, vmem_limit_bytes=VMEM_LIMIT)


def _dot(a, b):
    return jnp.dot(a, b, preferred_element_type=f32)


def _dot_nt(a, b):
    return lax.dot_general(a, b, (((1,), (1,)), ((), ())), preferred_element_type=f32)


def _dot_tn(a, b):
    return lax.dot_general(a, b, (((0,), (0,)), ((), ())), preferred_element_type=f32)


def _sigmoid(x):
    return 0.5 * jnp.tanh(0.5 * x) + 0.5


def _norm_fwd(x, g, shift, scale):
    rstd = lax.rsqrt(jnp.mean(x * x, axis=-1, keepdims=True) + EPS)
    xhat = x * rstd
    return xhat * g * (1.0 + scale) + shift, xhat, rstd


def _norm_bwd(dh, xhat, rstd, g, scale):
    dshift = jnp.sum(dh, axis=0, keepdims=True)
    dscale = jnp.sum(dh * xhat * g, axis=0, keepdims=True)
    dg = jnp.sum(dh * xhat * (1.0 + scale), axis=0, keepdims=True)
    dxh = dh * (g * (1.0 + scale))
    dx = rstd * (dxh - xhat * jnp.mean(dxh * xhat, axis=-1, keepdims=True))
    return dx, dshift, dscale, dg


def _allgather_small(xp):
    m_per, n = xp.shape

    def body(x_ref, out_ref, send_sems, recv_sems, local_sem):
        x, y, c = lax.axis_index("x"), lax.axis_index("y"), lax.axis_index("c")
        me, sibling = (x, y, c), (x, y, 1 - c)
        chips = [(1 - x, y), (x, 1 - y), (1 - x, 1 - y)]

        def rows(px, py, pc):
            return out_ref.at[pl.ds((4 * px + 2 * py + pc) * m_per, m_per), :]

        def copy(k, block, to, src=None):
            return pltpu.make_async_remote_copy(
                src_ref=rows(*block) if src is None else src, dst_ref=rows(*block),
                send_sem=send_sems.at[k], recv_sem=recv_sems.at[k], device_id=to, device_id_type=MESH)

        mine = pltpu.make_async_copy(x_ref, rows(*me), local_sem)
        mine.start()
        first = [copy(0, me, sibling, src=x_ref)]
        first += [copy(1 + j, me, (*chip, c), src=x_ref) for j, chip in enumerate(chips)]
        for cp in first:
            cp.start()
        passed = [copy(4 + j, (*chip, c), sibling) for j, chip in enumerate(chips)]
        for j, chip in enumerate(chips):
            copy(1 + j, (*chip, c), me).wait_recv()
            passed[j].start()
        copy(0, sibling, me).wait_recv()
        for j, chip in enumerate(chips):
            copy(4 + j, (*chip, 1 - c), me).wait_recv()
        for cp in first + passed:
            cp.wait_send()
        mine.wait()

    return pl.pallas_call(
        body, name="allgather_small",
        out_shape=SDS((8 * m_per, n), xp.dtype),
        in_specs=[pl.BlockSpec(memory_space=pltpu.VMEM)],
        out_specs=pl.BlockSpec(memory_space=pltpu.VMEM),
        scratch_shapes=[pltpu.SemaphoreType.DMA((7,)), pltpu.SemaphoreType.DMA((7,)), pltpu.SemaphoreType.DMA],
        compiler_params=pltpu.CompilerParams(vmem_limit_bytes=VMEM_LIMIT),
    )(xp)


WCLASSES = (
    ("wg0", "lead", (D, FB)), ("wu0", "lead", (D, FB)), ("wd0", "row", (FB, D)),
    ("wg1", "lead", (D, FB)), ("wu1", "lead", (D, FB)), ("wd1", "row", (FB, D)),
    ("win", "col", (D, WIN_SH)), ("wco", "row", (D // N_CHIPS, D)), ("wao", "col", (512, D // N_CHIPS)),
    ("wo", "row", (D // N_CHIPS, D)),
)
NCLS = len(WCLASSES)


def _full_shape(kind, shp):
    if kind == "lead":
        return (N_CHIPS,) + shp
    if kind == "row":
        return (N_CHIPS * shp[0], shp[1])
    return (shp[0], N_CHIPS * shp[1])


def _shard_view(ref, kind, shp, j):
    if kind == "lead":
        return ref.at[j]
    if kind == "row":
        return ref.at[pl.ds(j * shp[0], shp[0]), :]
    return ref.at[:, pl.ds(j * shp[1], shp[1])]


def _half(ref, shp, h):
    hr = shp[0] // 2
    return ref.at[pl.ds(pl.multiple_of(h * hr, 16), hr), :]


def _gather_weights(shards):
    n = NCLS

    def body(*refs):
        ins, outs = refs[:n], refs[n:2 * n]
        send1, recv1, send2, recv2, lsem = refs[2 * n:]
        x, y, c = lax.axis_index("x"), lax.axis_index("y"), lax.axis_index("c")
        chip = 2 * x + y
        sibling = (x, y, 1 - c)

        for mc in range(N_CHIPS):
            @pl.when(chip == mc)
            def _(mc=mc):
                local = []
                for q, (_, kind, shp) in enumerate(WCLASSES):
                    cp = pltpu.make_async_copy(ins[q], _shard_view(outs[q], kind, shp, mc), lsem.at[q])
                    cp.start()
                    local.append(cp)
                sends = []
                for k in (1, 2, 3):
                    pj = mc ^ k
                    for q, (_, kind, shp) in enumerate(WCLASSES):
                        cp = pltpu.make_async_remote_copy(
                            src_ref=_half(ins[q], shp, c), dst_ref=_half(_shard_view(outs[q], kind, shp, mc), shp, c),
                            send_sem=send1.at[q * 3 + k - 1], recv_sem=recv1.at[q * 3 + k - 1],
                            device_id=(pj >> 1, pj & 1, c), device_id_type=MESH)
                        cp.start()
                        sends.append(cp)
                for k in (1, 2, 3):
                    pj = mc ^ k
                    for q, (_, kind, shp) in enumerate(WCLASSES):
                        landed = _half(_shard_view(outs[q], kind, shp, pj), shp, c)
                        pltpu.make_async_remote_copy(
                            src_ref=landed, dst_ref=landed, send_sem=send1.at[q * 3 + k - 1], recv_sem=recv1.at[q * 3 + k - 1],
                            device_id=(pj >> 1, pj & 1, c), device_id_type=MESH).wait_recv()
                        cp = pltpu.make_async_remote_copy(
                            src_ref=landed, dst_ref=landed, send_sem=send2.at[q * 3 + k - 1], recv_sem=recv2.at[q * 3 + k - 1],
                            device_id=sibling, device_id_type=MESH)
                        cp.start()
                        sends.append(cp)
                for k in (1, 2, 3):
                    pj = mc ^ k
                    for q, (_, kind, shp) in enumerate(WCLASSES):
                        other = _half(_shard_view(outs[q], kind, shp, pj), shp, 1 - c)
                        pltpu.make_async_remote_copy(
                            src_ref=other, dst_ref=other, send_sem=send2.at[q * 3 + k - 1], recv_sem=recv2.at[q * 3 + k - 1],
                            device_id=sibling, device_id_type=MESH).wait_recv()
                for cp in sends:
                    cp.wait_send()
                for cp in local:
                    cp.wait()

    anyspec = pl.BlockSpec(memory_space=pl.ANY)
    return pl.pallas_call(
        body, name="gather_weights",
        out_shape=[SDS(_full_shape(kind, shp), bf16) for _, kind, shp in WCLASSES],
        in_specs=[anyspec] * n, out_specs=[anyspec] * n,
        scratch_shapes=[pltpu.SemaphoreType.DMA((3 * n,)), pltpu.SemaphoreType.DMA((3 * n,)),
                        pltpu.SemaphoreType.DMA((3 * n,)), pltpu.SemaphoreType.DMA((3 * n,)),
                        pltpu.SemaphoreType.DMA((n,))],
    )(*shards)


def _scatter_grads(pieces):
    n = NCLS

    def body(*refs):
        ins, outs = refs[:n], refs[n:2 * n]
        send1, recv1, lsem = refs[2 * n:]
        x, y, c = lax.axis_index("x"), lax.axis_index("y"), lax.axis_index("c")
        chip = 2 * x + y

        for mc in range(N_CHIPS):
            @pl.when(chip == mc)
            def _(mc=mc):
                local, sends = [], []
                for q, (_, kind, shp) in enumerate(WCLASSES):
                    cp = pltpu.make_async_copy(_shard_view(ins[q], kind, shp, mc), outs[q].at[0], lsem.at[q])
                    cp.start()
                    local.append(cp)
                for k in (1, 2, 3):
                    pj = mc ^ k
                    for q, (_, kind, shp) in enumerate(WCLASSES):
                        cp = pltpu.make_async_remote_copy(
                            src_ref=_shard_view(ins[q], kind, shp, pj), dst_ref=outs[q].at[k],
                            send_sem=send1.at[q * 3 + k - 1], recv_sem=recv1.at[q * 3 + k - 1],
                            device_id=(pj >> 1, pj & 1, c), device_id_type=MESH)
                        cp.start()
                        sends.append(cp)
                for cp in sends:
                    cp.wait_recv()
                for cp in sends:
                    cp.wait_send()
                for cp in local:
                    cp.wait()

    anyspec = pl.BlockSpec(memory_space=pl.ANY)
    return pl.pallas_call(
        body, name="scatter_grads",
        out_shape=[SDS((N_CHIPS,) + shp, bf16) for _, _, shp in WCLASSES],
        in_specs=[anyspec] * n, out_specs=[anyspec] * n,
        scratch_shapes=[pltpu.SemaphoreType.DMA((3 * n,)), pltpu.SemaphoreType.DMA((3 * n,)), pltpu.SemaphoreType.DMA((n,))],
    )(*pieces)


def _swap_sibling(ts):
    n = len(ts)

    def body(*refs):
        ins, outs = refs[:n], refs[n:2 * n]
        send, recv = refs[2 * n:]
        x, y, c = lax.axis_index("x"), lax.axis_index("y"), lax.axis_index("c")
        cps = []
        for q in range(n):
            cp = pltpu.make_async_remote_copy(src_ref=ins[q], dst_ref=outs[q], send_sem=send.at[q], recv_sem=recv.at[q],
                                              device_id=(x, y, 1 - c), device_id_type=MESH)
            cp.start()
            cps.append(cp)
        for cp in cps:
            cp.wait_recv()
        for cp in cps:
            cp.wait_send()

    anyspec = pl.BlockSpec(memory_space=pl.ANY)
    return pl.pallas_call(
        body, name="swap_sibling",
        out_shape=[SDS(t.shape, t.dtype) for t in ts],
        in_specs=[anyspec] * n, out_specs=[anyspec] * n,
        scratch_shapes=[pltpu.SemaphoreType.DMA((n,)), pltpu.SemaphoreType.DMA((n,))],
    )(*ts)


HBM_SPEC = pl.BlockSpec(memory_space=pltpu.HBM)
SEM_SPEC = pl.BlockSpec(memory_space=pltpu.SEMAPHORE)
ANY_SPEC = pl.BlockSpec(memory_space=pl.ANY)
EFFECT = pltpu.SideEffectType.DATAFLOW_SIDE_EFFECTING
N_COPIES = 3 * NCLS


def _in_hbm(a):
    return pltpu.with_memory_space_constraint(a, pltpu.HBM)


def _chip_index():
    return 2 * lax.axis_index("x") + lax.axis_index("y")


def _place_own(shards):
    n = NCLS

    def body(*refs):
        ins, outs, lsem = refs[:n], refs[n:2 * n], refs[2 * n]
        chip = _chip_index()
        for mc in range(N_CHIPS):
            @pl.when(chip == mc)
            def _(mc=mc):
                cps = [pltpu.make_async_copy(ins[q], _shard_view(outs[q], kind, shp, mc), lsem.at[q])
                       for q, (_, kind, shp) in enumerate(WCLASSES)]
                for cp in cps:
                    cp.start()
                for cp in cps:
                    cp.wait()

    return pl.pallas_call(
        body, name="place_own",
        out_shape=[SDS(_full_shape(kind, shp), bf16) for _, kind, shp in WCLASSES],
        in_specs=[ANY_SPEC] * n, out_specs=[ANY_SPEC] * n,
        scratch_shapes=[pltpu.SemaphoreType.DMA((n,))],
    )(*shards)


def _take_own(pieces):
    n = NCLS

    def body(*refs):
        ins, outs, lsem = refs[:n], refs[n:2 * n], refs[2 * n]
        chip = _chip_index()
        for mc in range(N_CHIPS):
            @pl.when(chip == mc)
            def _(mc=mc):
                cps = [pltpu.make_async_copy(_shard_view(ins[q], kind, shp, mc), outs[q].at[0], lsem.at[q])
                       for q, (_, kind, shp) in enumerate(WCLASSES)]
                for cp in cps:
                    cp.start()
                for cp in cps:
                    cp.wait()

    return pl.pallas_call(
        body, name="take_own",
        out_shape=[SDS((N_CHIPS,) + shp, bf16) for _, _, shp in WCLASSES],
        in_specs=[ANY_SPEC] * n, out_specs=[ANY_SPEC] * n,
        scratch_shapes=[pltpu.SemaphoreType.DMA((n,))],
    )(*pieces)


def _split_start(name, srcs, dsts, after, src_view, dst_view):
    n = NCLS

    def body(*refs):
        src, dst = refs[:n], refs[n:2 * n]
        send, recv = refs[2 * n + 1], refs[2 * n + 2]
        token = refs[-1]
        c = lax.axis_index("c")
        chip = _chip_index()
        for mc in range(N_CHIPS):
            @pl.when(chip == mc)
            def _(mc=mc):
                for k in (1, 2, 3):
                    pj = mc ^ k
                    for q in range(n):
                        pltpu.make_async_remote_copy(
                            src_ref=src_view(src[q], q, mc, pj), dst_ref=dst_view(dst[q], q, mc, k),
                            send_sem=send.at[q * 3 + k - 1], recv_sem=recv.at[q * 3 + k - 1],
                            device_id=(pj >> 1, pj & 1, c), device_id_type=MESH).start()
        token[...] = jnp.zeros_like(token)

    return pl.pallas_call(
        body, name=name,
        out_shape=(pltpu.SemaphoreType.DMA((N_COPIES,)), pltpu.SemaphoreType.DMA((N_COPIES,)),
                   *[pltpu.HBM(a.shape, a.dtype) for a in srcs], *[pltpu.HBM(a.shape, a.dtype) for a in dsts], SDS((8, 128), f32)),
        in_specs=[HBM_SPEC] * (2 * n) + [ANY_SPEC],
        out_specs=(SEM_SPEC, SEM_SPEC, *([HBM_SPEC] * (2 * n)), pl.BlockSpec(memory_space=pltpu.VMEM)),
        input_output_aliases={i: 2 + i for i in range(2 * n)},
        compiler_params=pltpu.CompilerParams(has_side_effects=EFFECT),
    )(*[_in_hbm(a) for a in srcs], *[_in_hbm(a) for a in dsts], after)


def _split_wait(name, started, after, arrival_view):
    n = NCLS
    send, recv = started[0], started[1]
    srcs, dsts = started[2:2 + n], started[2 + n:2 + 2 * n]

    def body(*refs):
        src, dst = refs[:n], refs[n:2 * n]
        send_sem, recv_sem = refs[2 * n], refs[2 * n + 1]
        x, y, c = lax.axis_index("x"), lax.axis_index("y"), lax.axis_index("c")
        for k in (1, 2, 3):
            for q in range(n):
                arrival = arrival_view(dst[q], q, k)
                cp = pltpu.make_async_remote_copy(
                    src_ref=arrival, dst_ref=arrival, send_sem=send_sem.at[q * 3 + k - 1], recv_sem=recv_sem.at[q * 3 + k - 1],
                    device_id=(x, y, 1 - c), device_id_type=MESH)
                cp.wait_send()
                cp.wait_recv()

    out = pl.pallas_call(
        body, name=name,
        out_shape=(*[pltpu.HBM(a.shape, a.dtype) for a in srcs], *[pltpu.HBM(a.shape, a.dtype) for a in dsts]),
        in_specs=[HBM_SPEC] * (2 * n) + [SEM_SPEC, SEM_SPEC, ANY_SPEC],
        out_specs=tuple([HBM_SPEC] * (2 * n)),
        input_output_aliases={i: i for i in range(2 * n)},
        compiler_params=pltpu.CompilerParams(has_side_effects=EFFECT),
    )(*srcs, *dsts, send, recv, after)
    return out[n:]


def _cls(q):
    return WCLASSES[q][1], WCLASSES[q][2]


def _gather_start(shards, after):
    fulls = _place_own(shards)
    return _split_start("gather_start", shards, fulls, after,
                        lambda ref, q, mc, pj: ref,
                        lambda ref, q, mc, k: _shard_view(ref, *_cls(q), mc))


def _gather_wait(started, after):
    return _split_wait("gather_wait", started, after, lambda ref, q, k: _shard_view(ref, *_cls(q), 0))


def _scatter_start(pieces, after):
    lands = _take_own(pieces)
    return _split_start("scatter_start", pieces, lands, after,
                        lambda ref, q, mc, pj: _shard_view(ref, *_cls(q), pj),
                        lambda ref, q, mc, k: ref.at[k])


def _scatter_wait(started, after):
    return _split_wait("scatter_wait", started, after, lambda ref, q, k: ref.at[k])


GROUPS = {"A": (0, 1, 2), "B": (6,), "C": (7, 8, 9), "D": (3, 4, 5)}


def _own_spec(kind, shp, tr):
    R, C = shp
    if kind == "lead":
        return pl.BlockSpec((None, tr, C), lambda i, chip: (chip[0], i, 0))
    if kind == "row":
        return pl.BlockSpec((tr, C), lambda i, chip: (chip[0] * (R // tr) + i, 0))
    return pl.BlockSpec((tr, C), lambda i, chip: (i, chip[0]))


def _cast_place(shards, kind, shp, chip_arr):
    n = len(shards)
    R, C = shp
    tr = _row_tile(R, C)

    def body(chip_ref, *refs):
        for q in range(n):
            refs[n + q][...] = refs[q][...].astype(bf16)

    def in_spec(lead):
        return pl.BlockSpec((None,) * len(lead) + (tr, C), lambda i, chip: (*lead, i, 0))

    return pl.pallas_call(
        body, name="cast_place",
        grid_spec=pltpu.PrefetchScalarGridSpec(
            num_scalar_prefetch=1, grid=(R // tr,),
            in_specs=[in_spec(lead) for _, lead in shards],
            out_specs=[_own_spec(kind, shp, tr)] * n),
        out_shape=[SDS(_full_shape(kind, shp), bf16)] * n,
        compiler_params=_cp("arbitrary"),
    )(chip_arr, *[a for a, _ in shards])


def _sum_own_slots(piece, land, kind, shp, chip_arr):
    R, C = shp
    tr = _row_tile(R, C)

    def body(chip_ref, p_ref, l_ref, t_ref):
        t = p_ref[...].astype(f32)
        for k in range(N_CHIPS - 1):
            t = t + l_ref[k].astype(f32)
        t_ref[...] = t

    return pl.pallas_call(
        body, name="sum_own_slots",
        grid_spec=pltpu.PrefetchScalarGridSpec(
            num_scalar_prefetch=1, grid=(R // tr,),
            in_specs=[_own_spec(kind, shp, tr), pl.BlockSpec((N_CHIPS - 1, tr, C), lambda i, chip: (0, i, 0))],
            out_specs=pl.BlockSpec((tr, C), lambda i, chip: (i, 0))),
        out_shape=SDS((R, C), f32),
        compiler_params=_cp("arbitrary"),
    )(chip_arr, piece, land)


def _xfer_start(name, arrays, ng, after, src_view, dst_view):
    na = len(arrays)

    def body(*refs):
        arr = refs[:na]
        send, recv, token = refs[na + 1], refs[na + 2], refs[-1]
        c = lax.axis_index("c")
        chip = _chip_index()
        for mc in range(N_CHIPS):
            @pl.when(chip == mc)
            def _(mc=mc):
                for k in (1, 2, 3):
                    pj = mc ^ k
                    for i in range(ng):
                        pltpu.make_async_remote_copy(
                            src_ref=src_view(arr, i, mc, pj), dst_ref=dst_view(arr, i, mc, k),
                            send_sem=send.at[i * 3 + k - 1], recv_sem=recv.at[i * 3 + k - 1],
                            device_id=(pj >> 1, pj & 1, c), device_id_type=MESH).start()
        token[...] = jnp.zeros_like(token)

    return pl.pallas_call(
        body, name=name,
        out_shape=(pltpu.SemaphoreType.DMA((3 * ng,)), pltpu.SemaphoreType.DMA((3 * ng,)),
                   *[pltpu.HBM(a.shape, a.dtype) for a in arrays], SDS((8, 128), f32)),
        in_specs=[HBM_SPEC] * na + [ANY_SPEC],
        out_specs=(SEM_SPEC, SEM_SPEC, *([HBM_SPEC] * na), pl.BlockSpec(memory_space=pltpu.VMEM)),
        input_output_aliases={i: 2 + i for i in range(na)},
        compiler_params=pltpu.CompilerParams(has_side_effects=EFFECT),
    )(*[_in_hbm(a) for a in arrays], after)


def _xfer_wait(name, started, ng, after, arrival_view):
    send, recv = started[0], started[1]
    arrays = started[2:-1]
    na = len(arrays)

    def body(*refs):
        arr = refs[:na]
        send_sem, recv_sem = refs[na], refs[na + 1]
        x, y, c = lax.axis_index("x"), lax.axis_index("y"), lax.axis_index("c")
        for k in (1, 2, 3):
            for i in range(ng):
                arrival = arrival_view(arr, i)
                cp = pltpu.make_async_remote_copy(
                    src_ref=arrival, dst_ref=arrival, send_sem=send_sem.at[i * 3 + k - 1], recv_sem=recv_sem.at[i * 3 + k - 1],
                    device_id=(x, y, 1 - c), device_id_type=MESH)
                cp.wait_send()
                cp.wait_recv()

    return pl.pallas_call(
        body, name=name,
        out_shape=tuple(pltpu.HBM(a.shape, a.dtype) for a in arrays),
        in_specs=[HBM_SPEC] * na + [SEM_SPEC, SEM_SPEC, ANY_SPEC],
        out_specs=tuple([HBM_SPEC] * na),
        input_output_aliases={i: i for i in range(na)},
        compiler_params=pltpu.CompilerParams(has_side_effects=EFFECT),
    )(*arrays, send, recv, after)


def _gather_group_cast(group, shards_f32, chip_arr):
    fulls = [None] * len(group)
    by_shape = {}
    for i, q in enumerate(group):
        by_shape.setdefault(_cls(q), []).append(i)
    for (kind, shp), idx in by_shape.items():
        for i, f in zip(idx, _cast_place([shards_f32[i] for i in idx], kind, shp, chip_arr)):
            fulls[i] = f
    return fulls


def _gather_group_start(tag, group, fulls, after):
    view = lambda arr, i, mc, _: _shard_view(arr[i], *_cls(group[i]), mc)
    return _xfer_start("gather_start_" + tag, fulls, len(group), after, view, view)


def _gather_group_wait(tag, group, started, after):
    return _xfer_wait("gather_wait_" + tag, started, len(group), after, lambda arr, i: _shard_view(arr[i], *_cls(group[i]), 0))


def _scatter_group_start(tag, group, pieces, after):
    ng = len(group)
    lands = [lax.empty((N_CHIPS - 1,) + _cls(q)[1], bf16) for q in group]
    return _xfer_start("scatter_start_" + tag, list(pieces) + lands, ng, after,
                       lambda arr, i, mc, pj: _shard_view(arr[i], *_cls(group[i]), pj),
                       lambda arr, i, mc, k: arr[ng + i].at[k - 1])


def _scatter_group_wait(tag, group, started, after):
    ng = len(group)
    out = _xfer_wait("scatter_wait_" + tag, started, ng, after, lambda arr, i: arr[ng + i].at[0])
    return out[:ng], out[ng:]


def _mod_shards(c_all, ada_w, ada_b_sh):
    tn = ADA_SH // 3

    def body(c_ref, w_ref, b_ref, o_ref, cs_ref):
        cv = c_ref[...]
        cs = cv * _sigmoid(cv)
        cs_ref[...] = cs
        o_ref[...] = _dot(cs.astype(bf16), w_ref[...].astype(bf16)) + b_ref[...]

    return pl.pallas_call(
        body, name="mod_shards", grid=(DEPTH, 3),
        out_shape=[SDS((DEPTH, 8, ADA_SH), f32), SDS((8, D), f32)],
        in_specs=[pl.BlockSpec((8, D), lambda l, t: (0, 0)),
                  pl.BlockSpec((None, D, tn), lambda l, t: (l, 0, t)),
                  pl.BlockSpec((None, 1, tn), lambda l, t: (l, 0, t))],
        out_specs=[pl.BlockSpec((None, 8, tn), lambda l, t: (l, 0, t)), pl.BlockSpec((8, D), lambda l, t: (0, 0))],
        compiler_params=_cp("arbitrary", "arbitrary"),
    )(c_all, ada_w, ada_b_sh.reshape(DEPTH, 1, ADA_SH))


def _t5_bucket(dist):
    exact = NUM_BUCKETS // 2
    dd = np.maximum(dist, 1).astype(np.float32)
    large = exact + (np.log(dd / exact) / np.log(MAX_DISTANCE / exact) * (NUM_BUCKETS - exact)).astype(np.int32)
    large = np.minimum(large, NUM_BUCKETS - 1)
    return np.where(dist < exact, dist, large).astype(np.int32)


def _bucket_table():
    i = np.arange(BLK)[:, None]
    j = np.arange(2 * BLK)[None, :]
    rel = i - j + BLK
    return np.stack([_t5_bucket(np.maximum(rel, 0) * d) for d in DILATIONS]).astype(np.int32)


def _band():
    rel = lax.broadcasted_iota(jnp.int32, (BLK, 2 * BLK), 0) - lax.broadcasted_iota(jnp.int32, (BLK, 2 * BLK), 1) + BLK
    return (rel >= 0) & (rel <= BLK)


def _bias_blocks(rel_bias, buckets):
    def body(tab_ref, bk_ref, o_ref):
        h = pl.program_id(0)
        bk = bk_ref[...]
        acc = jnp.zeros((BLK, 2 * BLK), f32)
        for b in range(NUM_BUCKETS):
            acc = jnp.where(bk == b, tab_ref[b, h], acc)
        o_ref[...] = jnp.where(_band(), acc, NEG)

    return pl.pallas_call(
        body, name="bias_blocks", grid=(24,),
        out_shape=SDS((24, BLK, 2 * BLK), f32),
        in_specs=[pl.BlockSpec(memory_space=pltpu.SMEM), pl.BlockSpec((None, BLK, 2 * BLK), lambda h: (h // 8, 0, 0))],
        out_specs=pl.BlockSpec((None, BLK, 2 * BLK), lambda h: (h, 0, 0)),
        compiler_params=_cp("arbitrary"),
    )(rel_bias, buckets)


def _bias_grad(dsaccs, buckets):
    nl = len(dsaccs)

    def body(*refs):
        bk = refs[nl][...]
        tot = refs[0][...]
        for r in refs[1:nl]:
            tot = tot + r[...]
        lane = lax.broadcasted_iota(jnp.int32, (1, 128), 1)
        row = jnp.zeros((1, 128), f32)
        for b in range(NUM_BUCKETS):
            row = jnp.where(lane == b, jnp.sum(jnp.where(bk == b, tot, 0.0)), row)
        refs[nl + 1][...] = row

    return pl.pallas_call(
        body, name="bias_grad", grid=(24,),
        out_shape=SDS((24, 1, 128), f32),
        in_specs=[pl.BlockSpec((None, BLK, 2 * BLK), lambda h: (h, 0, 0))] * nl
                 + [pl.BlockSpec((None, BLK, 2 * BLK), lambda h: (h // 8, 0, 0))],
        out_specs=pl.BlockSpec((None, 1, 128), lambda h: (h, 0, 0)),
        compiler_params=_cp("arbitrary"),
    )(*dsaccs, buckets)


def _ffn_fwd(x, mod9, g3, wg, wu, wd, sub):
    S = x.shape[0]

    def body(x_ref, mod_ref, g_ref, wg_ref, wu_ref, wd_ref, xo_ref, h_ref, ga_ref, sa_ref, hid_ref, y_ref, acc):
        j = pl.program_id(1)

        @pl.when(j == 0)
        def _():
            h, _, _ = _norm_fwd(x_ref[...], g_ref[sub:sub + 1, :], mod_ref[3 * sub:3 * sub + 1, :], mod_ref[3 * sub + 1:3 * sub + 2, :])
            h_ref[...] = h.astype(bf16)
            acc[...] = jnp.zeros_like(acc)

        h = h_ref[...]
        a = _dot(h, wg_ref[...])
        u = _dot(h, wu_ref[...])
        sg = _sigmoid(a)
        sil = a * sg
        ga_ref[...] = (u * (sg * (1.0 + a * (1.0 - sg)))).astype(bf16)
        sa_ref[...] = sil.astype(bf16)
        hid_ref[...] = (sil * u).astype(bf16)
        acc[...] += _dot(hid_ref[...], wd_ref[...])

        @pl.when(j == N_CHIPS - 1)
        def _():
            y = acc[...]
            y_ref[...] = y.astype(bf16)
            xo_ref[...] = x_ref[...] + 0.5 * mod_ref[3 * sub + 2:3 * sub + 3, :] * y

    row = pl.BlockSpec((TMF, D), lambda i, j: (i, 0))
    hidb = pl.BlockSpec((None, TMF, FB), lambda i, j: (j, i, 0))
    hids = SDS((N_CHIPS, S, FB), bf16)
    return pl.pallas_call(
        body, name="ffn_fwd", grid=(S // TMF, N_CHIPS),
        out_shape=[SDS((S, D), f32), SDS((S, D), bf16), hids, hids, hids, SDS((S, D), bf16)],
        in_specs=[row, pl.BlockSpec((9, D), lambda i, j: (0, 0)), pl.BlockSpec((3, D), lambda i, j: (0, 0)),
                  pl.BlockSpec((None, D, FB), lambda i, j: (j, 0, 0)), pl.BlockSpec((None, D, FB), lambda i, j: (j, 0, 0)),
                  pl.BlockSpec((FB, D), lambda i, j: (j, 0))],
        out_specs=[row, row, hidb, hidb, hidb, row],
        scratch_shapes=[pltpu.VMEM((TMF, D), f32)],
        compiler_params=_cp("arbitrary", "arbitrary"),
    )(x, mod9, g3, wg, wu, wd)


def _ffn_bwd1(dxo, x, mod9, g3, y, ga, sa, wg, wu, wd, sub):
    S = x.shape[0]

    def body(dxo_ref, x_ref, mod_ref, g_ref, y_ref, ga_ref, sa_ref, wg_ref, wu_ref, wd_ref,
             dxi_ref, da_ref, du_ref, dy_ref, sm_ref, acc):
        i, j = pl.program_id(0), pl.program_id(1)
        gate = mod_ref[3 * sub + 2:3 * sub + 3, :]

        @pl.when((i == 0) & (j == 0))
        def _():
            sm_ref[...] = jnp.zeros_like(sm_ref)

        @pl.when(j == 0)
        def _():
            dxo_v = dxo_ref[...]
            dy_ref[...] = (0.5 * gate * dxo_v).astype(bf16)
            sm_ref[2:3, :] += jnp.sum(0.5 * y_ref[...].astype(f32) * dxo_v, axis=0, keepdims=True)
            acc[...] = jnp.zeros_like(acc)

        part = None
        for s in range(SH_STEP):
            dhid = _dot_nt(dy_ref[...], wd_ref[s * FB:(s + 1) * FB, :])
            da = (dhid * ga_ref[s].astype(f32)).astype(bf16)
            du = (dhid * sa_ref[s].astype(f32)).astype(bf16)
            da_ref[s] = da
            du_ref[s] = du
            t = _dot_nt(da, wg_ref[s]) + _dot_nt(du, wu_ref[s])
            part = t if part is None else part + t
        acc[...] += part

        @pl.when(j == N_CHIPS // SH_STEP - 1)
        def _():
            g = g_ref[sub:sub + 1, :]
            scale = mod_ref[3 * sub + 1:3 * sub + 2, :]
            _, xhat, rstd = _norm_fwd(x_ref[...], g, mod_ref[3 * sub:3 * sub + 1, :], scale)
            dx, dshift, dscale, dg = _norm_bwd(acc[...], xhat, rstd, g, scale)
            dxi_ref[...] = dxo_ref[...] + dx
            sm_ref[0:1, :] += dshift
            sm_ref[1:2, :] += dscale
            sm_ref[3:4, :] += dg

    row = pl.BlockSpec((TM, D), lambda i, j: (i, 0))
    hidb = pl.BlockSpec((SH_STEP, TM, FB), lambda i, j: (j, i, 0))
    wcol = pl.BlockSpec((SH_STEP, D, FB), lambda i, j: (j, 0, 0))
    return pl.pallas_call(
        body, name="ffn_bwd1", grid=(S // TM, N_CHIPS // SH_STEP),
        out_shape=[SDS((S, D), f32), SDS((N_CHIPS, S, FB), bf16), SDS((N_CHIPS, S, FB), bf16), SDS((S, D), bf16), SDS((8, D), f32)],
        in_specs=[row, row, pl.BlockSpec((9, D), lambda i, j: (0, 0)), pl.BlockSpec((3, D), lambda i, j: (0, 0)), row,
                  hidb, hidb, wcol, wcol, pl.BlockSpec((SH_STEP * FB, D), lambda i, j: (j, 0))],
        out_specs=[row, hidb, hidb, row, pl.BlockSpec((8, D), lambda i, j: (0, 0))],
        scratch_shapes=[pltpu.VMEM((TM, D), f32)],
        compiler_params=_cp("arbitrary", "arbitrary"),
    )(dxo, x, mod9, g3, y, ga, sa, wg, wu, wd)


def _ffn_bwd2(h, da, du, hid, dy):
    S = h.shape[0]
    ni = S // TMW

    def body(h_ref, da_ref, du_ref, hid_ref, dy_ref, dwg_ref, dwu_ref, dwd_ref, ag, au, ad):
        i = pl.program_id(1)

        @pl.when(i == 0)
        def _():
            ag[...] = jnp.zeros_like(ag)
            au[...] = jnp.zeros_like(au)
            ad[...] = jnp.zeros_like(ad)

        hv = h_ref[...]
        ag[...] += _dot_tn(hv, da_ref[...])
        au[...] += _dot_tn(hv, du_ref[...])
        ad[...] += _dot_tn(hid_ref[...], dy_ref[...])

        @pl.when(i == ni - 1)
        def _():
            dwg_ref[...] = ag[...].astype(bf16)
            dwu_ref[...] = au[...].astype(bf16)
            dwd_ref[...] = ad[...].astype(bf16)

    row = pl.BlockSpec((TMW, D), lambda j, i: (i, 0))
    hidb = pl.BlockSpec((None, TMW, FB), lambda j, i: (j, i, 0))
    wcol = pl.BlockSpec((None, D, FB), lambda j, i: (j, 0, 0))
    return pl.pallas_call(
        body, name="ffn_bwd2", grid=(N_CHIPS, ni),
        out_shape=[SDS((N_CHIPS, D, FB), bf16), SDS((N_CHIPS, D, FB), bf16), SDS((N_CHIPS * FB, D), bf16)],
        in_specs=[row, hidb, hidb, hidb, row],
        out_specs=[wcol, wcol, pl.BlockSpec((FB, D), lambda j, i: (j, 0))],
        scratch_shapes=[pltpu.VMEM((D, FB), f32), pltpu.VMEM((D, FB), f32), pltpu.VMEM((FB, D), f32)],
        compiler_params=_cp("arbitrary", "arbitrary"),
    )(h, da, du, hid, dy)


def _mix_qkv(x, mod9, g3, win):
    S = x.shape[0]

    def body(x_ref, mod_ref, g_ref, w_ref, h_ref, o_ref):
        @pl.when(pl.program_id(1) == 0)
        def _():
            h, _, _ = _norm_fwd(x_ref[...], g_ref[1:2, :], mod_ref[3:4, :], mod_ref[4:5, :])
            h_ref[...] = h.astype(bf16)

        o_ref[...] = _dot(h_ref[...], w_ref[...])

    row = pl.BlockSpec((TMP, D), lambda i, j: (i, 0))
    return pl.pallas_call(
        body, name="mix_qkv", grid=(S // TMP, QKV_W // CB),
        out_shape=[SDS((S, D), bf16), SDS((S, QKV_W), f32)],
        in_specs=[row, pl.BlockSpec((9, D), lambda i, j: (0, 0)), pl.BlockSpec((3, D), lambda i, j: (0, 0)),
                  pl.BlockSpec((D, CB), lambda i, j: (0, j))],
        out_specs=[row, pl.BlockSpec((TMP, CB), lambda i, j: (i, j))],
        compiler_params=_cp("arbitrary", "arbitrary"),
    )(x, mod9, g3, win)


def _mix_rest(h, win):
    S = h.shape[0]
    off = QKV_W // CB

    def body(h_ref, w_ref, o_ref):
        o_ref[...] = _dot(h_ref[...], w_ref[...]).astype(bf16)

    return pl.pallas_call(
        body, name="mix_rest", grid=(S // TMP, REST_W // CB),
        out_shape=SDS((S, REST_W), bf16),
        in_specs=[pl.BlockSpec((TMP, D), lambda i, j: (i, 0)), pl.BlockSpec((D, CB), lambda i, j: (0, off + j))],
        out_specs=pl.BlockSpec((TMP, CB), lambda i, j: (i, j)),
        compiler_params=_cp("arbitrary", "arbitrary"),
    )(h, win)


def _attn_fwd(qkv, bias, g):
    S = qkv.shape[0]
    d = DILATIONS[g]
    nq = Q_BLOCKS[g]
    Rb = BLK * d
    R = Rb * nq
    nb = S // R
    qb, kb, vb = 4 * g, 12 + 4 * g, 24 + 4 * g

    def body(q_ref, kc_ref, kp_ref, vc_ref, vp_ref, b_ref, o_ref, l_ref):
        n = pl.program_id(1)
        col = lax.broadcasted_iota(jnp.int32, (BLK, 2 * BLK), 1)
        first = jnp.where((col < BLK) & (n == 0), NEG, 0.0)
        head0 = lax.broadcasted_iota(jnp.int32, (1, 2 * HD), 1) < HD

        def one(b, r):
            sl = pl.ds(b * Rb + r, BLK, stride=d)
            q = q_ref[sl, :]
            if b == 0:
                kp, vp = kp_ref[pl.ds(r, BLK, stride=d), :], vp_ref[pl.ds(r, BLK, stride=d), :]
            else:
                before = pl.ds((b - 1) * Rb + r, BLK, stride=d)
                kp, vp = kc_ref[before, :], vc_ref[before, :]
            kk = jnp.concatenate([kp, kc_ref[sl, :]], axis=0).astype(bf16)
            vv = jnp.concatenate([vp, vc_ref[sl, :]], axis=0).astype(bf16)
            os, ls = [], []
            for hh in range(2):
                qm = jnp.where(head0 if hh == 0 else ~head0, q, 0.0).astype(bf16)
                s = _dot_nt(qm, kk) * SCALE + b_ref[hh]
                if b == 0:
                    s = s + first
                m = jnp.max(s, axis=-1, keepdims=True)
                p = jnp.exp(s - m)
                l = jnp.sum(p, axis=-1, keepdims=True)
                os.append(_dot(p.astype(bf16), vv) / l)
                ls.append(m + jnp.log(l))
            o_ref[sl, :] = jnp.where(head0, os[0], os[1])
            l_ref[sl, :] = jnp.where(head0, ls[0], ls[1])

        for b in range(nq):
            if d == 1:
                one(b, 0)
            else:
                lax.fori_loop(0, d, lambda r, carry, b=b: (one(b, r), carry)[1], 0, unroll=4)

    def blk(cb, prev):
        if prev:
            return pl.BlockSpec((Rb, 128), lambda hp, n: (jnp.maximum(n * nq - 1, 0), cb + hp))
        return pl.BlockSpec((R, 128), lambda hp, n: (n, cb + hp))

    outb = pl.BlockSpec((R, 128), lambda hp, n: (n, hp))
    return pl.pallas_call(
        body, name=f"attn_fwd_d{d}", grid=(4, nb),
        out_shape=[SDS((S, 512), f32), SDS((S, 512), f32)],
        in_specs=[blk(qb, False), blk(kb, False), blk(kb, True), blk(vb, False), blk(vb, True),
                  pl.BlockSpec((2, BLK, 2 * BLK), lambda hp, n: (4 * g + hp, 0, 0))],
        out_specs=[outb, outb],
        compiler_params=_cp("arbitrary", "arbitrary"),
    )(qkv, qkv, qkv, qkv, qkv, bias)


def _attn_bwd(qkv, do, o, lse, bias, dq_all, dk_all, dv_all, g):
    S = qkv.shape[0]
    d = DILATIONS[g]
    nq = Q_BLOCKS[g]
    Rb = BLK * d
    R = Rb * nq
    nb = S // R
    qb, kb, vb = 4 * g, 12 + 4 * g, 24 + 4 * g

    def body(q_ref, kc_ref, kp_ref, vc_ref, vp_ref, do_ref, o_ref, l_ref, b_ref, dqi, dki, dvi,
             dq_ref, dk_ref, dv_ref, ds_ref, ck, cv, tk, tv):
        n = pl.program_id(1)
        col = lax.broadcasted_iota(jnp.int32, (BLK, 2 * BLK), 1)
        first = jnp.where((col < BLK) & (n == 0), NEG, 0.0)

        @pl.when(n == 0)
        def _():
            ck[...] = jnp.zeros_like(ck)
            cv[...] = jnp.zeros_like(cv)
            ds_ref[...] = jnp.zeros_like(ds_ref)

        @pl.when(n < nb)
        def _():
            head0 = lax.broadcasted_iota(jnp.int32, (1, 2 * HD), 1) < HD

            def one(b, r):
                sl = pl.ds(b * Rb + r, BLK, stride=d)
                before = pl.ds((max(b, 1) - 1) * Rb + r, BLK, stride=d)
                q = q_ref[sl, :]
                if b == 0:
                    kp, vp = kp_ref[pl.ds(r, BLK, stride=d), :], vp_ref[pl.ds(r, BLK, stride=d), :]
                else:
                    kp, vp = kc_ref[before, :], vc_ref[before, :]
                kk = jnp.concatenate([kp, kc_ref[sl, :]], axis=0).astype(bf16)
                vv = jnp.concatenate([vp, vc_ref[sl, :]], axis=0).astype(bf16)
                dov, lv = do_ref[sl, :], l_ref[sl, :]
                prod = dov * o_ref[sl, :]
                qb, dob = q.astype(bf16), dov.astype(bf16)
                dqs, dks, dvs = [], [], []
                for hh in range(2):
                    msk = head0 if hh == 0 else ~head0
                    qm = jnp.where(msk, q, 0.0).astype(bf16)
                    dom = jnp.where(msk, dov, 0.0).astype(bf16)
                    dsum = jnp.sum(jnp.where(msk, prod, 0.0), axis=-1, keepdims=True)
                    s = _dot_nt(qm, kk) * SCALE + b_ref[hh]
                    if b == 0:
                        s = s + first
                    p = jnp.exp(s - lv[:, HD * hh:HD * hh + 1])
                    ds = p * (_dot_nt(dom, vv) - dsum)
                    ds_ref[hh] += ds
                    dsb = ds.astype(bf16)
                    dqs.append(_dot(dsb, kk) * SCALE)
                    dks.append(_dot_tn(dsb, qb) * SCALE)
                    dvs.append(_dot_tn(p.astype(bf16), dob))
                dq_ref[sl, :] = jnp.where(head0, dqs[0], dqs[1])
                dk = jnp.where(head0, dks[0], dks[1])
                dv = jnp.where(head0, dvs[0], dvs[1])
                tk[sl, :] = dk[BLK:]
                tv[sl, :] = dv[BLK:]
                if b == 0:
                    prev_rows = pl.ds((nq - 1) * Rb + r, BLK, stride=d)
                    ck[prev_rows, :] += dk[:BLK]
                    cv[prev_rows, :] += dv[:BLK]
                else:
                    tk[before, :] += dk[:BLK]
                    tv[before, :] += dv[:BLK]

            for b in range(nq):
                if d == 1:
                    one(b, 0)
                else:
                    lax.fori_loop(0, d, lambda r, carry, b=b: (one(b, r), carry)[1], 0, unroll=4)
            dk_ref[...] = ck[...]
            dv_ref[...] = cv[...]
            ck[...] = tk[...]
            cv[...] = tv[...]

        @pl.when(n == nb)
        def _():
            dk_ref[...] = ck[...]
            dv_ref[...] = cv[...]

    last = nb - 1

    def blk(cb, prev):
        if prev:
            return pl.BlockSpec((Rb, 128), lambda hp, n: (jnp.maximum(jnp.minimum(n, last) * nq - 1, 0), cb + hp))
        return pl.BlockSpec((R, 128), lambda hp, n: (jnp.minimum(n, last), cb + hp))

    cur = pl.BlockSpec((R, 128), lambda hp, n: (jnp.minimum(n, last), hp))
    anyspec = pl.BlockSpec(memory_space=pl.ANY)
    dqo = pl.BlockSpec((R, 128), lambda hp, n: (jnp.minimum(n, last), 4 * g + hp))
    dko = pl.BlockSpec((R, 128), lambda hp, n: (jnp.maximum(n - 1, 0), 4 * g + hp))
    return pl.pallas_call(
        body, name=f"attn_bwd_d{d}", grid=(4, nb + 1),
        out_shape=[SDS((S, 1536), f32), SDS((S, 1536), f32), SDS((S, 1536), f32), SDS((8, BLK, 2 * BLK), f32)],
        in_specs=[blk(qb, False), blk(kb, False), blk(kb, True), blk(vb, False), blk(vb, True), cur, cur, cur,
                  pl.BlockSpec((2, BLK, 2 * BLK), lambda hp, n: (4 * g + hp, 0, 0)), anyspec, anyspec, anyspec],
        out_specs=[dqo, dko, dko, pl.BlockSpec((2, BLK, 2 * BLK), lambda hp, n: (hp, 0, 0))],
        scratch_shapes=[pltpu.VMEM((R, 128), f32)] * 4,
        input_output_aliases={9: 0, 10: 1, 11: 2},
        compiler_params=_cp("arbitrary", "arbitrary"),
    )(qkv, qkv, qkv, qkv, qkv, do, o, lse, bias, dq_all, dk_all, dv_all)


def _conv_z(cc, ch, hc, hh, cw_ref, first):
    halo = jnp.where(first, 0.0, hc.astype(f32) * hh.astype(f32))
    T = jnp.concatenate([halo, cc * ch], axis=0)
    z = cw_ref[2:3, :] * T + cw_ref[1:2, :] * pltpu.roll(T, 1, 0) + cw_ref[0:1, :] * pltpu.roll(T, 2, 0)
    return T, z[HALO:]


def _rest_specs(tm, with_next):
    per = tm // HALO
    specs = [pl.BlockSpec((tm, D), functools.partial(lambda i, k: (i, k), k=k)) for k in range(5)]
    specs += [pl.BlockSpec((HALO, D), functools.partial(lambda i, k: (jnp.maximum(i * per - 1, 0), k), k=k)) for k in (1, 2)]
    return specs


def _mix_out_fwd(x, mod9, rest, ogs, lgs, cw, wco, wao, wo):
    S = x.shape[0]
    tm = TMX

    def body(x_ref, mod_ref, cb_ref, cc_ref, ch_ref, gc_ref, ga_ref, hc_ref, hh_ref,
             o0, o1, o2, l0, l1, l2, cw_ref, wco_ref, wao_ref, wo_ref,
             xo_ref, o_ref, lse_ref, yc_ref, ya_ref, out_ref):
        i = pl.program_id(0)
        lv = [l0[...], l1[...], l2[...]]
        mx = jnp.maximum(jnp.maximum(lv[0], lv[1]), lv[2])
        es = [jnp.exp(l - mx) for l in lv]
        den = es[0] + es[1] + es[2]
        o = (es[0] / den) * o0[...] + (es[1] / den) * o1[...] + (es[2] / den) * o2[...]
        o_ref[...] = o
        lse_ref[...] = mx + jnp.log(den)
        _, z = _conv_z(cc_ref[...].astype(f32), ch_ref[...].astype(f32), hc_ref[...], hh_ref[...], cw_ref, i == 0)
        p = (cb_ref[...].astype(f32) * z).astype(bf16)
        yc = _dot(p, wco_ref[...])
        ya = _dot(o.astype(bf16), wao_ref[...])
        yc_ref[...] = yc.astype(bf16)
        ya_ref[...] = ya.astype(bf16)
        merged = _sigmoid(gc_ref[...].astype(f32)) * yc + _sigmoid(ga_ref[...].astype(f32)) * ya
        out = _dot(merged.astype(bf16), wo_ref[...])
        out_ref[...] = out.astype(bf16)
        xo_ref[...] = x_ref[...] + mod_ref[5:6, :] * out

    row = pl.BlockSpec((tm, D), lambda i: (i, 0))
    att = pl.BlockSpec((tm, 512), lambda i: (i, 0))
    full = lambda shp: pl.BlockSpec(shp, lambda i: (0, 0))
    return pl.pallas_call(
        body, name="mix_out_fwd", grid=(S // tm,),
        out_shape=[SDS((S, D), f32), SDS((S, 512), f32), SDS((S, 512), f32), SDS((S, D), bf16), SDS((S, D), bf16), SDS((S, D), bf16)],
        in_specs=[row, full((9, D))] + _rest_specs(tm, False) + [att] * 6 + [full((3, D)), full((D, D)), full((512, D)), full((D, D))],
        out_specs=[row, att, att, row, row, row],
        compiler_params=_cp("arbitrary"),
    )(x, mod9, *([rest] * 7), *ogs, *lgs, cw, wco, wao, wo)


def _mix_out_bwd(dxo, mod9, outv, yc, ya, rest, o, cw, wco, wao, wo):
    S = dxo.shape[0]
    tm = TMX
    ni = S // tm

    def body(dxo_ref, mod_ref, out_ref, yc_ref, ya_ref, cb_ref, cc_ref, ch_ref, gc_ref, ga_ref, hc_ref, hh_ref,
             o_ref, cw_ref, wco_ref, wao_ref, wo_ref,
             dp_ref, dg2_ref, do_ref, dwco_ref, dwao_ref, dwo_ref, sm_ref, aco, aao, ao):
        i = pl.program_id(0)

        @pl.when(i == 0)
        def _():
            sm_ref[...] = jnp.zeros_like(sm_ref)
            aco[...] = jnp.zeros_like(aco)
            aao[...] = jnp.zeros_like(aao)
            ao[...] = jnp.zeros_like(ao)

        dxo_v = dxo_ref[...]
        sm_ref[2:3, :] += jnp.sum(out_ref[...].astype(f32) * dxo_v, axis=0, keepdims=True)
        dout = (mod_ref[5:6, :] * dxo_v).astype(bf16)
        dmerged = _dot_nt(dout, wo_ref[...])
        sc, sa = _sigmoid(gc_ref[...].astype(f32)), _sigmoid(ga_ref[...].astype(f32))
        ycv, yav = yc_ref[...].astype(f32), ya_ref[...].astype(f32)
        ao[...] += _dot_tn((sc * ycv + sa * yav).astype(bf16), dout)
        dyc = (dmerged * sc).astype(bf16)
        dya = (dmerged * sa).astype(bf16)
        dg2_ref[:, :D] = (dmerged * ycv * sc * (1.0 - sc)).astype(bf16)
        dg2_ref[:, D:] = (dmerged * yav * sa * (1.0 - sa)).astype(bf16)
        dp_ref[...] = _dot_nt(dyc, wco_ref[...]).astype(bf16)
        _, z = _conv_z(cc_ref[...].astype(f32), ch_ref[...].astype(f32), hc_ref[...], hh_ref[...], cw_ref, i == 0)
        aco[...] += _dot_tn((cb_ref[...].astype(f32) * z).astype(bf16), dyc)
        do_ref[...] = _dot_nt(dya, wao_ref[...])
        aao[...] += _dot_tn(o_ref[...].astype(bf16), dya)

        @pl.when(i == ni - 1)
        def _():
            dwco_ref[...] = aco[...].astype(bf16)
            dwao_ref[...] = aao[...].astype(bf16)
            dwo_ref[...] = ao[...].astype(bf16)

    row = pl.BlockSpec((tm, D), lambda i: (i, 0))
    att = pl.BlockSpec((tm, 512), lambda i: (i, 0))
    full = lambda shp: pl.BlockSpec(shp, lambda i: (0, 0))
    return pl.pallas_call(
        body, name="mix_out_bwd", grid=(ni,),
        out_shape=[SDS((S, D), bf16), SDS((S, 2 * D), bf16), SDS((S, 512), f32),
                   SDS((D, D), bf16), SDS((512, D), bf16), SDS((D, D), bf16), SDS((8, D), f32)],
        in_specs=[row, full((9, D)), row, row, row] + _rest_specs(tm, False) + [att, full((3, D)), full((D, D)), full((512, D)), full((D, D))],
        out_specs=[row, pl.BlockSpec((tm, 2 * D), lambda i: (i, 0)), att, full((D, D)), full((512, D)), full((D, D)), full((8, D))],
        scratch_shapes=[pltpu.VMEM((D, D), f32), pltpu.VMEM((512, D), f32), pltpu.VMEM((D, D), f32)],
        compiler_params=_cp("arbitrary"),
    )(dxo, mod9, outv, yc, ya, *([rest] * 7), o, cw, wco, wao, wo)


def _conv_bwd(dp, rest, cw):
    S = dp.shape[0]
    tm = TM
    per = tm // HALO
    nh = S // HALO
    ni = S // tm

    def body(dp_ref, dpn_ref, cb_ref, cbn_ref, cc_ref, ch_ref, hc_ref, hh_ref, cw_ref, d3_ref, sm_ref):
        i = pl.program_id(0)

        @pl.when(i == 0)
        def _():
            sm_ref[...] = jnp.zeros_like(sm_ref)

        cc, ch = cc_ref[...].astype(f32), ch_ref[...].astype(f32)
        T, z = _conv_z(cc, ch, hc_ref[...], hh_ref[...], cw_ref, i == 0)
        dpv = dp_ref[...].astype(f32)
        cbv = cb_ref[...].astype(f32)
        dz = dpv * cbv
        dzn = jnp.where(i == ni - 1, 0.0, dpn_ref[...].astype(f32) * cbn_ref[...].astype(f32))
        E = jnp.concatenate([dz, dzn], axis=0)
        ne = tm + HALO
        dT = cw_ref[2:3, :] * E + cw_ref[1:2, :] * pltpu.roll(E, ne - 1, 0) + cw_ref[0:1, :] * pltpu.roll(E, ne - 2, 0)
        dT = dT[:tm]
        d3_ref[:, :D] = (dpv * z).astype(bf16)
        d3_ref[:, D:2 * D] = (dT * ch).astype(bf16)
        d3_ref[:, 2 * D:] = (dT * cc).astype(bf16)
        sm_ref[2:3, :] += jnp.sum(dz * T[HALO:], axis=0, keepdims=True)
        sm_ref[1:2, :] += jnp.sum(dz * pltpu.roll(T, 1, 0)[HALO:], axis=0, keepdims=True)
        sm_ref[0:1, :] += jnp.sum(dz * pltpu.roll(T, 2, 0)[HALO:], axis=0, keepdims=True)

    row = pl.BlockSpec((tm, D), lambda i: (i, 0))
    nxt = pl.BlockSpec((HALO, D), lambda i: (jnp.minimum((i + 1) * per, nh - 1), 0))
    col = lambda k: pl.BlockSpec((tm, D), lambda i: (i, k))
    prv = lambda k: pl.BlockSpec((HALO, D), lambda i: (jnp.maximum(i * per - 1, 0), k))
    return pl.pallas_call(
        body, name="conv_bwd", grid=(ni,),
        out_shape=[SDS((S, 3 * D), bf16), SDS((8, D), f32)],
        in_specs=[row, nxt, col(0), nxt, col(1), col(2), prv(1), prv(2), pl.BlockSpec((3, D), lambda i: (0, 0))],
        out_specs=[pl.BlockSpec((tm, 3 * D), lambda i: (i, 0)), pl.BlockSpec((8, D), lambda i: (0, 0))],
        compiler_params=_cp("arbitrary"),
    )(dp, dp, rest, rest, rest, rest, rest, rest, cw)


_DU_RANGES = ((0, 3), (3, 6), (6, 9), (9, 15), (15, 19))
N_CBLK = IN_W // CB


def _mix_in_bwd_dh(dxo, x, mod9, g3, dus, win):
    S = x.shape[0]

    def body(dxo_ref, x_ref, mod_ref, g_ref, s0, s1, s2, s3, s4, w_ref, dxi_ref, sm_ref, acc):
        i, kb = pl.program_id(0), pl.program_id(1)

        @pl.when((i == 0) & (kb == 0))
        def _():
            sm_ref[...] = jnp.zeros_like(sm_ref)

        @pl.when(kb == 0)
        def _():
            acc[...] = jnp.zeros_like(acc)

        for src, (lo, hi) in zip((s0, s1, s2, s3, s4), _DU_RANGES):
            @pl.when((kb >= lo) & (kb < hi))
            def _(src=src):
                acc[...] += _dot_nt(src[...].astype(bf16), w_ref[...])

        @pl.when(kb == N_CBLK - 1)
        def _():
            g, scale = g_ref[1:2, :], mod_ref[4:5, :]
            _, xhat, rstd = _norm_fwd(x_ref[...], g, mod_ref[3:4, :], scale)
            dx, dshift, dscale, dg = _norm_bwd(acc[...], xhat, rstd, g, scale)
            dxi_ref[...] = dxo_ref[...] + dx
            sm_ref[0:1, :] += dshift
            sm_ref[1:2, :] += dscale
            sm_ref[3:4, :] += dg

    row = pl.BlockSpec((TMP, D), lambda i, kb: (i, 0))

    def src_spec(lo, hi):
        return pl.BlockSpec((TMP, CB), lambda i, kb: (i, jnp.clip(kb - lo, 0, hi - lo - 1)))

    return pl.pallas_call(
        body, name="mix_in_bwd_dh", grid=(S // TMP, N_CBLK),
        out_shape=[SDS((S, D), f32), SDS((8, D), f32)],
        in_specs=[row, row, pl.BlockSpec((9, D), lambda i, kb: (0, 0)), pl.BlockSpec((3, D), lambda i, kb: (0, 0))]
                 + [src_spec(lo, hi) for lo, hi in _DU_RANGES] + [pl.BlockSpec((D, CB), lambda i, kb: (0, kb))],
        out_specs=[row, pl.BlockSpec((8, D), lambda i, kb: (0, 0))],
        scratch_shapes=[pltpu.VMEM((TMP, D), f32)],
        compiler_params=_cp("arbitrary", "arbitrary"),
    )(dxo, x, mod9, g3, *dus, win)


def _mix_in_bwd_dw(h, dus):
    S = h.shape[0]
    ni = S // TMW

    def body(h_ref, s0, s1, s2, s3, s4, dw_ref, acc):
        kb, i = pl.program_id(0), pl.program_id(1)

        @pl.when(i == 0)
        def _():
            acc[...] = jnp.zeros_like(acc)

        for src, (lo, hi) in zip((s0, s1, s2, s3, s4), _DU_RANGES):
            @pl.when((kb >= lo) & (kb < hi))
            def _(src=src):
                rows = pl.ds(pl.multiple_of(i * TMW, TMW), TMW)
                acc[...] += _dot_tn(h_ref[rows, :], src[...].astype(bf16))

        @pl.when(i == ni - 1)
        def _():
            dw_ref[...] = acc[...].astype(bf16)

    def src_spec(lo, hi):
        def imap(kb, i):
            on = (kb >= lo) & (kb < hi)
            return (jnp.where(on, i, 0), jnp.clip(kb - lo, 0, hi - lo - 1))
        return pl.BlockSpec((TMW, CB), imap)

    return pl.pallas_call(
        body, name="mix_in_bwd_dw", grid=(N_CBLK, ni),
        out_shape=SDS((D, IN_W), bf16),
        in_specs=[pl.BlockSpec((S, D), lambda kb, i: (0, 0))] + [src_spec(lo, hi) for lo, hi in _DU_RANGES],
        out_specs=pl.BlockSpec((D, CB), lambda kb, i: (0, kb)),
        scratch_shapes=[pltpu.VMEM((D, CB), f32)],
        compiler_params=_cp("arbitrary", "arbitrary"),
    )(h, *dus)


def _loss_head(x, fg, tgt):
    S = x.shape[0]

    def body(x_ref, g_ref, t_ref, ls_ref, dx_ref, sm_ref):
        i = pl.program_id(0)

        @pl.when(i == 0)
        def _():
            ls_ref[...] = jnp.zeros_like(ls_ref)
            sm_ref[...] = jnp.zeros_like(sm_ref)

        xv, g = x_ref[...], g_ref[...]
        rstd = lax.rsqrt(jnp.mean(xv * xv, axis=-1, keepdims=True) + EPS)
        xhat = xv * rstd
        e = xhat * g - t_ref[...]
        ls_ref[...] += 0.5 * jnp.sum(jnp.mean(e * e, axis=-1, keepdims=True))
        dy = e * (1.0 / D)
        sm_ref[0:1, :] += jnp.sum(dy * xhat, axis=0, keepdims=True)
        dxh = dy * g
        dx_ref[...] = rstd * (dxh - xhat * jnp.mean(dxh * xhat, axis=-1, keepdims=True))

    row = pl.BlockSpec((TM, D), lambda i: (i, 0))
    return pl.pallas_call(
        body, name="loss_head", grid=(S // TM,),
        out_shape=[SDS((8, 128), f32), SDS((S, D), f32), SDS((8, D), f32)],
        in_specs=[row, pl.BlockSpec((1, D), lambda i: (0, 0)), row],
        out_specs=[pl.BlockSpec((8, 128), lambda i: (0, 0)), row, pl.BlockSpec((8, D), lambda i: (0, 0))],
        compiler_params=_cp("arbitrary"),
    )(x, fg, tgt)


def _adam(w, g, m, v):
    m2 = B1 * m + (1.0 - B1) * g
    v2 = B2 * v + (1.0 - B2) * (g * g)
    delta = -LR * ((m2 / BC1) / (jnp.sqrt(v2 / BC2) + AEPS) + WD * w)
    return delta, m2, v2


def _row_tile(rows, cols):
    for tr in (512, 352, 256, 128, 64):
        if rows % tr == 0 and tr * cols * 4 <= (5 << 18):
            return tr
    raise ValueError((rows, cols))


def _sum_slots(land):
    _, R, C = land.shape
    tr = _row_tile(R, C)

    def body(l_ref, t_ref):
        t = l_ref[0].astype(f32)
        for k in range(1, N_CHIPS):
            t = t + l_ref[k].astype(f32)
        t_ref[...] = t

    return pl.pallas_call(
        body, name="sum_slots", grid=(R // tr,),
        out_shape=SDS((R, C), f32),
        in_specs=[pl.BlockSpec((N_CHIPS, tr, C), lambda i: (0, i, 0))],
        out_specs=pl.BlockSpec((tr, C), lambda i: (i, 0)),
        compiler_params=_cp("arbitrary"),
    )(land)


def _adamw_pair(w2, m2, v2, ta, tb, outs, slot):
    R, C = ta.shape
    tr = _row_tile(R, C)
    nrt = R // tr

    def body(w_ref, m_ref, v_ref, ta_ref, tb_ref, g_in, d_in, m_in, v_in, g_ref, d_ref, mo_ref, vo_ref):
        g = ta_ref[...] + tb_ref[...]
        delta, mn, vn = _adam(w_ref[...], g, m_ref[...], v_ref[...])
        g_ref[...] = g
        d_ref[...] = delta
        mo_ref[...] = mn
        vo_ref[...] = vn

    big = pl.BlockSpec((tr, C), lambda i: (slot * nrt + i, 0))
    loc = pl.BlockSpec((tr, C), lambda i: (i, 0))
    anyspec = pl.BlockSpec(memory_space=pl.ANY)
    return pl.pallas_call(
        body, name="adamw_pair", grid=(nrt,),
        out_shape=[SDS(o.shape, f32) for o in outs],
        in_specs=[big, big, big, loc, loc] + [anyspec] * 4,
        out_specs=[big] * 4,
        input_output_aliases={5: 0, 6: 1, 7: 2, 8: 3},
        compiler_params=_cp("arbitrary"),
    )(w2, m2, v2, ta, tb, *outs)


def _adamw_small(w, g, m, v):
    def body(w_ref, g_ref, m_ref, v_ref, d_ref, mo_ref, vo_ref):
        delta, mn, vn = _adam(w_ref[...], g_ref[...], m_ref[...], v_ref[...])
        d_ref[...] = delta
        mo_ref[...] = mn
        vo_ref[...] = vn

    return pl.pallas_call(body, name="adamw_small", out_shape=[SDS(w.shape, f32)] * 3)(w, g, m, v)


def _ada_w_update(cs_all, dmod_sh, w, m, v):
    tr = 256

    def body(cs_ref, dm_ref, w_ref, m_ref, v_ref, g_ref, d_ref, mo_ref, vo_ref):
        g = _dot_tn(cs_ref[...].astype(bf16), dm_ref[...].astype(bf16))
        delta, mn, vn = _adam(w_ref[...], g, m_ref[...], v_ref[...])
        g_ref[...] = g
        d_ref[...] = delta
        mo_ref[...] = mn
        vo_ref[...] = vn

    blk = pl.BlockSpec((None, tr, ADA_SH), lambda l, i: (l, i, 0))
    return pl.pallas_call(
        body, name="ada_w_update", grid=(DEPTH, D // tr),
        out_shape=[SDS(w.shape, f32)] * 4,
        in_specs=[pl.BlockSpec((8, tr), lambda l, i: (0, i)), pl.BlockSpec((None, 8, ADA_SH), lambda l, i: (l, 0, 0)), blk, blk, blk],
        out_specs=[blk] * 4,
        compiler_params=_cp("arbitrary", "arbitrary"),
    )(cs_all, dmod_sh, w, m, v)


def _sum_devices(gathered):
    _, R, C = gathered.shape

    def body(g_ref, o_ref):
        t = g_ref[0]
        for k in range(1, 8):
            t = t + g_ref[k]
        o_ref[...] = t

    return pl.pallas_call(body, name="sum_devices", out_shape=SDS((R, C), f32))(gathered)


def _layer_fwd(x, mod9, g3, cw, getw, bias):
    W = {}

    def take(gname, after, mod9):
        w, tok = getw(gname, after)
        W.update(w)
        return mod9 if tok is None else mod9 + tok[0, 0]

    mod9 = take("A", x, mod9)
    x1, h1, a1, u1, hid1, y1 = _ffn_fwd(x, mod9, g3, W["wg0"], W["wu0"], W["wd0"], 0)
    mod9 = take("B", x1, mod9)
    hm, qkv = _mix_qkv(x1, mod9, g3, W["win"])
    rest = _mix_rest(hm, W["win"])
    ogs, lgs = [], []
    for g in range(3):
        og, lg = _attn_fwd(qkv, bias, g)
        ogs.append(og)
        lgs.append(lg)
    mod9 = take("C", ogs[2], mod9)
    x2, o, lse, yc, ya, outv = _mix_out_fwd(x1, mod9, rest, ogs, lgs, cw, W["wco"], W["wao"], W["wo"])
    mod9 = take("D", x2, mod9)
    x3, h3, a3, u3, hid3, y3 = _ffn_fwd(x2, mod9, g3, W["wg1"], W["wu1"], W["wd1"], 2)
    saved = dict(x0=x, x1=x1, x2=x2, h1=h1, a1=a1, u1=u1, hid1=hid1, y1=y1, hm=hm, qkv=qkv, rest=rest, o=o, lse=lse, yc=yc, ya=ya,
                 outv=outv, h3=h3, a3=a3, u3=u3, hid3=hid3, y3=y3)
    return x3, saved, W


def _layer_bwd(dx, sv, mod9, g3, cw, W, bias, emit):
    S = dx.shape[0]
    dw = {}

    def send(gname, mod9):
        tok = emit(gname, dw)
        return mod9 if tok is None else mod9 + tok[0, 0]

    dx2, da, du, dy, sm3 = _ffn_bwd1(dx, sv["x2"], mod9, g3, sv["y3"], sv["a3"], sv["u3"], W["wg1"], W["wu1"], W["wd1"], 2)
    dw["wg1"], dw["wu1"], dw["wd1"] = _ffn_bwd2(sv["h3"], da, du, sv["hid3"], dy)
    mod9 = send("D", mod9)
    dp, dg2, do, dw["wco"], dw["wao"], dw["wo"], smo = _mix_out_bwd(
        dx2, mod9, sv["outv"], sv["yc"], sv["ya"], sv["rest"], sv["o"], cw, W["wco"], W["wao"], W["wo"])
    mod9_c = send("C", mod9)
    cw = cw + (mod9_c - mod9)[0:1, :]
    mod9 = mod9_c
    d3, smc = _conv_bwd(dp, sv["rest"], cw)
    dq = lax.empty((S, 1536), f32)
    dk = lax.empty((S, 1536), f32)
    dv = lax.empty((S, 1536), f32)
    dsaccs = []
    for g in range(3):
        dq, dk, dv, dsg = _attn_bwd(sv["qkv"], do, sv["o"], sv["lse"], bias, dq, dk, dv, g)
        dsaccs.append(dsg)
    dus = (dq, dk, dv, d3, dg2)
    dx1, smm = _mix_in_bwd_dh(dx2, sv["x1"], mod9, g3, dus, W["win"])
    dw["win"] = _mix_in_bwd_dw(sv["hm"], dus)
    mod9 = send("B", mod9)
    dx0, da, du, dy, sm1 = _ffn_bwd1(dx1, sv["x0"], mod9, g3, sv["y1"], sv["a1"], sv["u1"], W["wg0"], W["wu0"], W["wd0"], 0)
    dw["wg0"], dw["wu0"], dw["wd0"] = _ffn_bwd2(sv["h1"], da, du, sv["hid1"], dy)
    send("A", mod9)
    dmod = jnp.concatenate([sm1[0:3], smm[0:2], smo[2:3], sm3[0:3]], axis=0)
    dng = jnp.concatenate([sm1[3:4], smm[3:4], sm3[3:4]], axis=0)
    return dx0, dmod, dng, smc[0:3], jnp.concatenate(dsaccs, axis=0)


def _chip_cols(a, chip, width):
    return lax.dynamic_slice_in_dim(a, chip * width, width, axis=a.ndim - 1)


def kernel(x, c, ada_w, ada_b, norm_g, ffn_w_gate, ffn_w_up, ffn_w_down, w_in, conv_w, w_conv_out, w_attn_out, w_o, rel_bias, final_g, loss_target, m_ada_w, m_ada_b, m_norm_g, m_ffn_w_gate, m_ffn_w_up, m_ffn_w_down, m_w_in, m_conv_w, m_w_conv_out, m_w_attn_out, m_w_o, m_rel_bias, m_final_g, v_ada_w, v_ada_b, v_norm_g, v_ffn_w_gate, v_ffn_w_up, v_ffn_w_down, v_w_in, v_conv_w, v_w_conv_out, v_w_attn_out, v_w_o, v_rel_bias, v_final_g):
    ix, iy, ic = lax.axis_index("x"), lax.axis_index("y"), lax.axis_index("c")
    chip = 2 * ix + iy
    dev = 4 * ix + 2 * iy + ic
    xs = x[0]
    S = xs.shape[0]
    qd = D // N_CHIPS

    chip_arr = jnp.reshape(chip, (1,)).astype(jnp.int32)
    names = [w[0] for w in WCLASSES]

    def layer_shards(l):
        return [(ffn_w_gate, (l, 0)), (ffn_w_up, (l, 0)), (ffn_w_down, (l, 0)), (ffn_w_gate, (l, 1)), (ffn_w_up, (l, 1)),
                (ffn_w_down, (l, 1)), (w_in, (l,)), (w_conv_out, (l,)), (w_attn_out, (l,)), (w_o, (l,))]

    started = {}
    extra_starts = {(0, "A"): [(0, "B")], (0, "B"): [(0, "C"), (0, "D"), (1, "A")]}

    casts = {}

    def cast_group(l, gname):
        shards = layer_shards(l)
        casts[(l, gname)] = _gather_group_cast(GROUPS[gname], [shards[q] for q in GROUPS[gname]], chip_arr)

    def start_gather(l, gname, after):
        started[(l, gname)] = _gather_group_start(f"l{l}{gname}", GROUPS[gname], casts[(l, gname)], after)
        return started[(l, gname)][-1]

    cast_group(0, "A")
    tok0 = start_gather(0, "A", c)
    for l in range(DEPTH):
        for gname in GROUPS:
            if (l, gname) not in casts:
                cast_group(l, gname)
    last_cast = casts[(DEPTH - 1, "D")][-1]

    pad8 = lambda a: jnp.pad(a, ((0, -a.shape[0] % 8), (0, 0)))
    pack = jnp.concatenate([pad8(c + tok0[0:1, 0:1]), pad8(norm_g.reshape(3, D)), pad8(conv_w.reshape(3, D))], axis=0)
    g1 = _allgather_small(pack).reshape(8, 24, D)
    c_all = g1[:, 0]
    by_chip = g1[0::2]
    ng_full = jnp.concatenate([by_chip[j, 8:11].reshape(DEPTH, 3, qd) for j in range(N_CHIPS)], axis=-1)
    cw_full = jnp.concatenate([by_chip[j, 16:19].reshape(DEPTH, 3, qd) for j in range(N_CHIPS)], axis=-1)
    mod_sh, cs_all = _mod_shards(c_all, ada_w, _chip_cols(ada_b, chip, ADA_SH))
    g2 = _allgather_small(mod_sh.reshape(DEPTH * 8, ADA_SH)).reshape(8, DEPTH, 8, ADA_SH)
    mine = lax.dynamic_index_in_dim(g2[0::2], dev, axis=2, keepdims=False)
    mod = jnp.transpose(mine, (1, 0, 2)).reshape(DEPTH, 9, D)

    buckets = jnp.asarray(_bucket_table())
    bias = _bias_blocks(rel_bias, buckets)

    def make_getw(l):
        def getw(gname, after):
            if (l, gname) == (0, "A"):
                after = last_cast
            full = _gather_group_wait(f"l{l}{gname}", GROUPS[gname], started[(l, gname)], after)
            tok = None
            for nl, ng in extra_starts.get((l, gname), []) + [(l + 1, gname)]:
                if nl < DEPTH and (nl, ng) not in started:
                    tok = start_gather(nl, ng, full[0] if tok is None else tok)
            return {names[q]: f for q, f in zip(GROUPS[gname], full)}, tok
        return getw

    Ws, saves = [], []
    xc = xs
    for l in range(DEPTH):
        xc, sv, W = _layer_fwd(xc, mod[l], ng_full[l], cw_full[l], make_getw(l), bias)
        Ws.append(W)
        saves.append(sv)

    ls, dx, smf = _loss_head(xc, final_g.reshape(1, D), loss_target[0])
    loss = lax.psum(ls[0, 0], ("x", "y", "c"))

    params = dict(wg=ffn_w_gate, wu=ffn_w_up, wd=ffn_w_down, win=w_in, wco=w_conv_out, wao=w_attn_out, wo=w_o)
    moms = dict(wg=m_ffn_w_gate, wu=m_ffn_w_up, wd=m_ffn_w_down, win=m_w_in, wco=m_w_conv_out, wao=m_w_attn_out, wo=m_w_o)
    vars_ = dict(wg=v_ffn_w_gate, wu=v_ffn_w_up, wd=v_ffn_w_down, win=v_w_in, wco=v_w_conv_out, wao=v_w_attn_out, wo=v_w_o)
    flat = lambda a: a.reshape(-1, a.shape[-1])
    big_out = {k: [lax.empty(flat(p).shape, f32) for _ in range(4)] for k, p in params.items()}
    dmods, dngs, dcws, dsaccs = [None] * DEPTH, [None] * DEPTH, [None] * DEPTH, [None] * DEPTH

    def finish(l, gname, started, after):
        group = GROUPS[gname]
        pieces, lands = _scatter_group_wait(f"l{l}{gname}", group, started, after)
        ts = [_sum_own_slots(pieces[i], lands[i], *_cls(q), chip_arr) for i, q in enumerate(group)]
        tsib = _swap_sibling(ts)
        for i, q in enumerate(group):
            name = names[q]
            key = name.rstrip("01")
            slot = 2 * l + int(name[-1]) if name[-1] in "01" else l
            big_out[key] = _adamw_pair(flat(params[key]), flat(moms[key]), flat(vars_[key]), ts[i], tsib[i], big_out[key], slot)

    pending, tok = [], None
    for l in reversed(range(DEPTH)):
        modl = mod[l] if tok is None else mod[l] + tok[0, 0]
        mine = []

        def emit(gname, dw, l=l, mine=mine):
            prev = mine[-1][2][-1] if mine else dx
            mine.append((l, gname, _scatter_group_start(f"l{l}{gname}", GROUPS[gname], [dw[names[q]] for q in GROUPS[gname]], prev)))
            return mine[-1][2][-1]

        dx, dmods[l], dngs[l], dcws[l], dsaccs[l] = _layer_bwd(dx, saves[l], modl, ng_full[l], cw_full[l], Ws[l], bias, emit)
        for pl_, pg, pst in pending:
            finish(pl_, pg, pst, dx)
        pending, tok = mine, mine[-1][2][-1]
    for pl_, pg, pst in pending[:-1]:
        finish(pl_, pg, pst, dx)

    drb = jnp.transpose(_bias_grad(dsaccs, buckets)[:, 0, :NUM_BUCKETS])
    drb_row = jnp.pad(drb.reshape(1, NUM_BUCKETS * 24), ((0, 0), (0, D - NUM_BUCKETS * 24)))
    pack2 = jnp.concatenate([pad8(a) for a in dmods] + [pad8(a) for a in dngs] + [pad8(a) for a in dcws] + [smf, pad8(drb_row)], axis=0)
    n_rows = pack2.shape[0]
    g3 = _allgather_small(pack2).reshape(8, n_rows, D)
    tot = _sum_devices(g3)
    o_ng, o_cw, o_fg, o_rb = 16 * DEPTH, 24 * DEPTH, 32 * DEPTH, 32 * DEPTH + 8
    g_ada_b = jnp.stack([tot[16 * l:16 * l + 9] for l in range(DEPTH)]).reshape(DEPTH, 9 * D)
    g_norm_g = _chip_cols(jnp.stack([tot[o_ng + 8 * l:o_ng + 8 * l + 3] for l in range(DEPTH)]), chip, qd)
    g_conv_w = _chip_cols(jnp.stack([tot[o_cw + 8 * l:o_cw + 8 * l + 3] for l in range(DEPTH)]), chip, qd)
    g_final_g = tot[o_fg]
    g_rel_bias = tot[o_rb, :NUM_BUCKETS * 24].reshape(NUM_BUCKETS, 24)
    dmod_all = jnp.stack([g3[:, 16 * l:16 * l + 9].reshape(8, 9 * D) for l in range(DEPTH)])
    dmod_sh = _chip_cols(dmod_all, chip, ADA_SH)
    g_ada_w, d_ada_w, nm_ada_w, nv_ada_w = _ada_w_update(cs_all, dmod_sh, ada_w, m_ada_w, v_ada_w)

    def small(w, g, m, v):
        shp = w.shape
        to2 = lambda a: a.reshape(-1, shp[-1])
        return [o.reshape(shp) for o in _adamw_small(to2(w), to2(g), to2(m), to2(v))]

    d_ada_b, nm_ada_b, nv_ada_b = small(ada_b, g_ada_b, m_ada_b, v_ada_b)
    d_norm_g, nm_norm_g, nv_norm_g = small(norm_g, g_norm_g, m_norm_g, v_norm_g)
    d_conv_w, nm_conv_w, nv_conv_w = small(conv_w, g_conv_w, m_conv_w, v_conv_w)
    d_rel_bias, nm_rel_bias, nv_rel_bias = small(rel_bias, g_rel_bias, m_rel_bias, v_rel_bias)
    d_final_g, nm_final_g, nv_final_g = small(final_g, g_final_g, m_final_g, v_final_g)

    behind = nv_ada_w[0, 0:8, 0:128] + big_out["win"][3][0:8, 0:128] + big_out["wo"][3][0:8, 0:128] + big_out["wd"][3][0:8, 0:128]
    finish(*pending[-1], behind)

    def big(key, which):
        return big_out[key][which].reshape(params[key].shape)

    grads = [g_ada_w, g_ada_b, g_norm_g, big("wg", 0), big("wu", 0), big("wd", 0), big("win", 0), g_conv_w, big("wco", 0),
             big("wao", 0), big("wo", 0), g_rel_bias, g_final_g]
    deltas = [d_ada_w, d_ada_b, d_norm_g, big("wg", 1), big("wu", 1), big("wd", 1), big("win", 1), d_conv_w, big("wco", 1),
              big("wao", 1), big("wo", 1), d_rel_bias, d_final_g]
    new_m = [nm_ada_w, nm_ada_b, nm_norm_g, big("wg", 2), big("wu", 2), big("wd", 2), big("win", 2), nm_conv_w, big("wco", 2),
             big("wao", 2), big("wo", 2), nm_rel_bias, nm_final_g]
    new_v = [nv_ada_w, nv_ada_b, nv_norm_g, big("wg", 3), big("wu", 3), big("wd", 3), big("win", 3), nv_conv_w, big("wco", 3),
             big("wao", 3), big("wo", 3), nv_rel_bias, nv_final_g]
    return (loss, dx[None], *grads, *deltas, *new_m, *new_v)
```

```python
import functools

import numpy as np
import jax
import jax.numpy as jnp
from jax import lax
from jax.experimental import pallas as pl
from jax.experimental.pallas import tpu as pltpu

f32, bf16 = jnp.float32, jnp.bfloat16
SDS = jax.ShapeDtypeStruct
MESH = pl.DeviceIdType.MESH

D = 1024
DEPTH = 4
N_CHIPS = 4
FB = 704
HD = 64
QKV_W = 4608
REST_W = 5120
IN_W = QKV_W + REST_W
WIN_SH = IN_W // N_CHIPS
ADA_SH = 9 * D // N_CHIPS
BLK = 128
DILATIONS = (1, 4, 16)
Q_BLOCKS = (4, 1, 1)
NUM_BUCKETS, MAX_DISTANCE = 32, 2048
EPS = 1e-6
NEG = -1e30
SCALE = HD ** -0.5
LR, B1, B2, AEPS, WD, STEP = 0.001, 0.9, 0.999, 1e-08, 0.01, 10
BC1 = 1.0 - B1 ** STEP
BC2 = 1.0 - B2 ** STEP
VMEM_LIMIT = 56 * 1024 * 1024
TM = 512
TMW = 1024
TMP = 1024
TMF = 1024
SH_STEP = 2
TMX = 256
HALO = 16
CB = 512


def _cp(*sem):
    return pltpu.CompilerParams(dimension_semantics=sem if sem else None, vmem_limit_bytes=VMEM_LIMIT)


def _dot(a, b):
    return jnp.dot(a, b, preferred_element_type=f32)


def _dot_nt(a, b):
    return lax.dot_general(a, b, (((1,), (1,)), ((), ())), preferred_element_type=f32)


def _dot_tn(a, b):
    return lax.dot_general(a, b, (((0,), (0,)), ((), ())), preferred_element_type=f32)


def _sigmoid(x):
    return 0.5 * jnp.tanh(0.5 * x) + 0.5


def _norm_fwd(x, g, shift, scale):
    rstd = lax.rsqrt(jnp.mean(x * x, axis=-1, keepdims=True) + EPS)
    xhat = x * rstd
    return xhat * g * (1.0 + scale) + shift, xhat, rstd


def _norm_bwd(dh, xhat, rstd, g, scale):
    dshift = jnp.sum(dh, axis=0, keepdims=True)
    dscale = jnp.sum(dh * xhat * g, axis=0, keepdims=True)
    dg = jnp.sum(dh * xhat * (1.0 + scale), axis=0, keepdims=True)
    dxh = dh * (g * (1.0 + scale))
    dx = rstd * (dxh - xhat * jnp.mean(dxh * xhat, axis=-1, keepdims=True))
    return dx, dshift, dscale, dg


def _allgather_small(xp):
    m_per, n = xp.shape

    def body(x_ref, out_ref, send_sems, recv_sems, local_sem):
        x, y, c = lax.axis_index("x"), lax.axis_index("y"), lax.axis_index("c")
        me, sibling = (x, y, c), (x, y, 1 - c)
        chips = [(1 - x, y), (x, 1 - y), (1 - x, 1 - y)]

        def rows(px, py, pc):
            return out_ref.at[pl.ds((4 * px + 2 * py + pc) * m_per, m_per), :]

        def copy(k, block, to, src=None):
            return pltpu.make_async_remote_copy(
                src_ref=rows(*block) if src is None else src, dst_ref=rows(*block),
                send_sem=send_sems.at[k], recv_sem=recv_sems.at[k], device_id=to, device_id_type=MESH)

        mine = pltpu.make_async_copy(x_ref, rows(*me), local_sem)
        mine.start()
        first = [copy(0, me, sibling, src=x_ref)]
        first += [copy(1 + j, me, (*chip, c), src=x_ref) for j, chip in enumerate(chips)]
        for cp in first:
            cp.start()
        passed = [copy(4 + j, (*chip, c), sibling) for j, chip in enumerate(chips)]
        for j, chip in enumerate(chips):
            copy(1 + j, (*chip, c), me).wait_recv()
            passed[j].start()
        copy(0, sibling, me).wait_recv()
        for j, chip in enumerate(chips):
            copy(4 + j, (*chip, 1 - c), me).wait_recv()
        for cp in first + passed:
            cp.wait_send()
        mine.wait()

    return pl.pallas_call(
        body, name="allgather_small",
        out_shape=SDS((8 * m_per, n), xp.dtype),
        in_specs=[pl.BlockSpec(memory_space=pltpu.VMEM)],
        out_specs=pl.BlockSpec(memory_space=pltpu.VMEM),
        scratch_shapes=[pltpu.SemaphoreType.DMA((7,)), pltpu.SemaphoreType.DMA((7,)), pltpu.SemaphoreType.DMA],
        compiler_params=pltpu.CompilerParams(vmem_limit_bytes=VMEM_LIMIT),
    )(xp)


WCLASSES = (
    ("wg0", "lead", (D, FB)), ("wu0", "lead", (D, FB)), ("wd0", "row", (FB, D)),
    ("wg1", "lead", (D, FB)), ("wu1", "lead", (D, FB)), ("wd1", "row", (FB, D)),
    ("win", "col", (D, WIN_SH)), ("wco", "row", (D // N_CHIPS, D)), ("wao", "col", (512, D // N_CHIPS)),
    ("wo", "row", (D // N_CHIPS, D)),
)
NCLS = len(WCLASSES)


def _full_shape(kind, shp):
    if kind == "lead":
        return (N_CHIPS,) + shp
    if kind == "row":
        return (N_CHIPS * shp[0], shp[1])
    return (shp[0], N_CHIPS * shp[1])


def _shard_view(ref, kind, shp, j):
    if kind == "lead":
        return ref.at[j]
    if kind == "row":
        return ref.at[pl.ds(j * shp[0], shp[0]), :]
    return ref.at[:, pl.ds(j * shp[1], shp[1])]


def _half(ref, shp, h):
    hr = shp[0] // 2
    return ref.at[pl.ds(pl.multiple_of(h * hr, 16), hr), :]


def _gather_weights(shards):
    n = NCLS

    def body(*refs):
        ins, outs = refs[:n], refs[n:2 * n]
        send1, recv1, send2, recv2, lsem = refs[2 * n:]
        x, y, c = lax.axis_index("x"), lax.axis_index("y"), lax.axis_index("c")
        chip = 2 * x + y
        sibling = (x, y, 1 - c)

        for mc in range(N_CHIPS):
            @pl.when(chip == mc)
            def _(mc=mc):
                local = []
                for q, (_, kind, shp) in enumerate(WCLASSES):
                    cp = pltpu.make_async_copy(ins[q], _shard_view(outs[q], kind, shp, mc), lsem.at[q])
                    cp.start()
                    local.append(cp)
                sends = []
                for k in (1, 2, 3):
                    pj = mc ^ k
                    for q, (_, kind, shp) in enumerate(WCLASSES):
                        cp = pltpu.make_async_remote_copy(
                            src_ref=_half(ins[q], shp, c), dst_ref=_half(_shard_view(outs[q], kind, shp, mc), shp, c),
                            send_sem=send1.at[q * 3 + k - 1], recv_sem=recv1.at[q * 3 + k - 1],
                            device_id=(pj >> 1, pj & 1, c), device_id_type=MESH)
                        cp.start()
                        sends.append(cp)
                for k in (1, 2, 3):
                    pj = mc ^ k
                    for q, (_, kind, shp) in enumerate(WCLASSES):
                        landed = _half(_shard_view(outs[q], kind, shp, pj), shp, c)
                        pltpu.make_async_remote_copy(
                            src_ref=landed, dst_ref=landed, send_sem=send1.at[q * 3 + k - 1], recv_sem=recv1.at[q * 3 + k - 1],
                            device_id=(pj >> 1, pj & 1, c), device_id_type=MESH).wait_recv()
                        cp = pltpu.make_async_remote_copy(
                            src_ref=landed, dst_ref=landed, send_sem=send2.at[q * 3 + k - 1], recv_sem=recv2.at[q * 3 + k - 1],
                            device_id=sibling, device_id_type=MESH)
                        cp.start()
                        sends.append(cp)
                for k in (1, 2, 3):
                    pj = mc ^ k
                    for q, (_, kind, shp) in enumerate(WCLASSES):
                        other = _half(_shard_view(outs[q], kind, shp, pj), shp, 1 - c)
                        pltpu.make_async_remote_copy(
                            src_ref=other, dst_ref=other, send_sem=send2.at[q * 3 + k - 1], recv_sem=recv2.at[q * 3 + k - 1],
                            device_id=sibling, device_id_type=MESH).wait_recv()
                for cp in sends:
                    cp.wait_send()
                for cp in local:
                    cp.wait()

    anyspec = pl.BlockSpec(memory_space=pl.ANY)
    return pl.pallas_call(
        body, name="gather_weights",
        out_shape=[SDS(_full_shape(kind, shp), bf16) for _, kind, shp in WCLASSES],
        in_specs=[anyspec] * n, out_specs=[anyspec] * n,
        scratch_shapes=[pltpu.SemaphoreType.DMA((3 * n,)), pltpu.SemaphoreType.DMA((3 * n,)),
                        pltpu.SemaphoreType.DMA((3 * n,)), pltpu.SemaphoreType.DMA((3 * n,)),
                        pltpu.SemaphoreType.DMA((n,))],
    )(*shards)


def _scatter_grads(pieces):
    n = NCLS

    def body(*refs):
        ins, outs = refs[:n], refs[n:2 * n]
        send1, recv1, lsem = refs[2 * n:]
        x, y, c = lax.axis_index("x"), lax.axis_index("y"), lax.axis_index("c")
        chip = 2 * x + y

        for mc in range(N_CHIPS):
            @pl.when(chip == mc)
            def _(mc=mc):
                local, sends = [], []
                for q, (_, kind, shp) in enumerate(WCLASSES):
                    cp = pltpu.make_async_copy(_shard_view(ins[q], kind, shp, mc), outs[q].at[0], lsem.at[q])
                    cp.start()
                    local.append(cp)
                for k in (1, 2, 3):
                    pj = mc ^ k
                    for q, (_, kind, shp) in enumerate(WCLASSES):
                        cp = pltpu.make_async_remote_copy(
                            src_ref=_shard_view(ins[q], kind, shp, pj), dst_ref=outs[q].at[k],
                            send_sem=send1.at[q * 3 + k - 1], recv_sem=recv1.at[q * 3 + k - 1],
                            device_id=(pj >> 1, pj & 1, c), device_id_type=MESH)
                        cp.start()
                        sends.append(cp)
                for cp in sends:
                    cp.wait_recv()
                for cp in sends:
                    cp.wait_send()
                for cp in local:
                    cp.wait()

    anyspec = pl.BlockSpec(memory_space=pl.ANY)
    return pl.pallas_call(
        body, name="scatter_grads",
        out_shape=[SDS((N_CHIPS,) + shp, bf16) for _, _, shp in WCLASSES],
        in_specs=[anyspec] * n, out_specs=[anyspec] * n,
        scratch_shapes=[pltpu.SemaphoreType.DMA((3 * n,)), pltpu.SemaphoreType.DMA((3 * n,)), pltpu.SemaphoreType.DMA((n,))],
    )(*pieces)


def _swap_sibling(ts):
    n = len(ts)

    def body(*refs):
        ins, outs = refs[:n], refs[n:2 * n]
        send, recv = refs[2 * n:]
        x, y, c = lax.axis_index("x"), lax.axis_index("y"), lax.axis_index("c")
        cps = []
        for q in range(n):
            cp = pltpu.make_async_remote_copy(src_ref=ins[q], dst_ref=outs[q], send_sem=send.at[q], recv_sem=recv.at[q],
                                              device_id=(x, y, 1 - c), device_id_type=MESH)
            cp.start()
            cps.append(cp)
        for cp in cps:
            cp.wait_recv()
        for cp in cps:
            cp.wait_send()

    anyspec = pl.BlockSpec(memory_space=pl.ANY)
    return pl.pallas_call(
        body, name="swap_sibling",
        out_shape=[SDS(t.shape, t.dtype) for t in ts],
        in_specs=[anyspec] * n, out_specs=[anyspec] * n,
        scratch_shapes=[pltpu.SemaphoreType.DMA((n,)), pltpu.SemaphoreType.DMA((n,))],
    )(*ts)


HBM_SPEC = pl.BlockSpec(memory_space=pltpu.HBM)
SEM_SPEC = pl.BlockSpec(memory_space=pltpu.SEMAPHORE)
ANY_SPEC = pl.BlockSpec(memory_space=pl.ANY)
EFFECT = pltpu.SideEffectType.DATAFLOW_SIDE_EFFECTING
N_COPIES = 3 * NCLS


def _in_hbm(a):
    return pltpu.with_memory_space_constraint(a, pltpu.HBM)


def _chip_index():
    return 2 * lax.axis_index("x") + lax.axis_index("y")


def _place_own(shards):
    n = NCLS

    def body(*refs):
        ins, outs, lsem = refs[:n], refs[n:2 * n], refs[2 * n]
        chip = _chip_index()
        for mc in range(N_CHIPS):
            @pl.when(chip == mc)
            def _(mc=mc):
                cps = [pltpu.make_async_copy(ins[q], _shard_view(outs[q], kind, shp, mc), lsem.at[q])
                       for q, (_, kind, shp) in enumerate(WCLASSES)]
                for cp in cps:
                    cp.start()
                for cp in cps:
                    cp.wait()

    return pl.pallas_call(
        body, name="place_own",
        out_shape=[SDS(_full_shape(kind, shp), bf16) for _, kind, shp in WCLASSES],
        in_specs=[ANY_SPEC] * n, out_specs=[ANY_SPEC] * n,
        scratch_shapes=[pltpu.SemaphoreType.DMA((n,))],
    )(*shards)


def _take_own(pieces):
    n = NCLS

    def body(*refs):
        ins, outs, lsem = refs[:n], refs[n:2 * n], refs[2 * n]
        chip = _chip_index()
        for mc in range(N_CHIPS):
            @pl.when(chip == mc)
            def _(mc=mc):
                cps = [pltpu.make_async_copy(_shard_view(ins[q], kind, shp, mc), outs[q].at[0], lsem.at[q])
                       for q, (_, kind, shp) in enumerate(WCLASSES)]
                for cp in cps:
                    cp.start()
                for cp in cps:
                    cp.wait()

    return pl.pallas_call(
        body, name="take_own",
        out_shape=[SDS((N_CHIPS,) + shp, bf16) for _, _, shp in WCLASSES],
        in_specs=[ANY_SPEC] * n, out_specs=[ANY_SPEC] * n,
        scratch_shapes=[pltpu.SemaphoreType.DMA((n,))],
    )(*pieces)


def _split_start(name, srcs, dsts, after, src_view, dst_view):
    n = NCLS

    def body(*refs):
        src, dst = refs[:n], refs[n:2 * n]
        send, recv = refs[2 * n + 1], refs[2 * n + 2]
        token = refs[-1]
        c = lax.axis_index("c")
        chip = _chip_index()
        for mc in range(N_CHIPS):
            @pl.when(chip == mc)
            def _(mc=mc):
                for k in (1, 2, 3):
                    pj = mc ^ k
                    for q in range(n):
                        pltpu.make_async_remote_copy(
                            src_ref=src_view(src[q], q, mc, pj), dst_ref=dst_view(dst[q], q, mc, k),
                            send_sem=send.at[q * 3 + k - 1], recv_sem=recv.at[q * 3 + k - 1],
                            device_id=(pj >> 1, pj & 1, c), device_id_type=MESH).start()
        token[...] = jnp.zeros_like(token)

    return pl.pallas_call(
        body, name=name,
        out_shape=(pltpu.SemaphoreType.DMA((N_COPIES,)), pltpu.SemaphoreType.DMA((N_COPIES,)),
                   *[pltpu.HBM(a.shape, a.dtype) for a in srcs], *[pltpu.HBM(a.shape, a.dtype) for a in dsts], SDS((8, 128), f32)),
        in_specs=[HBM_SPEC] * (2 * n) + [ANY_SPEC],
        out_specs=(SEM_SPEC, SEM_SPEC, *([HBM_SPEC] * (2 * n)), pl.BlockSpec(memory_space=pltpu.VMEM)),
        input_output_aliases={i: 2 + i for i in range(2 * n)},
        compiler_params=pltpu.CompilerParams(has_side_effects=EFFECT),
    )(*[_in_hbm(a) for a in srcs], *[_in_hbm(a) for a in dsts], after)


def _split_wait(name, started, after, arrival_view):
    n = NCLS
    send, recv = started[0], started[1]
    srcs, dsts = started[2:2 + n], started[2 + n:2 + 2 * n]

    def body(*refs):
        src, dst = refs[:n], refs[n:2 * n]
        send_sem, recv_sem = refs[2 * n], refs[2 * n + 1]
        x, y, c = lax.axis_index("x"), lax.axis_index("y"), lax.axis_index("c")
        for k in (1, 2, 3):
            for q in range(n):
                arrival = arrival_view(dst[q], q, k)
                cp = pltpu.make_async_remote_copy(
                    src_ref=arrival, dst_ref=arrival, send_sem=send_sem.at[q * 3 + k - 1], recv_sem=recv_sem.at[q * 3 + k - 1],
                    device_id=(x, y, 1 - c), device_id_type=MESH)
                cp.wait_send()
                cp.wait_recv()

    out = pl.pallas_call(
        body, name=name,
        out_shape=(*[pltpu.HBM(a.shape, a.dtype) for a in srcs], *[pltpu.HBM(a.shape, a.dtype) for a in dsts]),
        in_specs=[HBM_SPEC] * (2 * n) + [SEM_SPEC, SEM_SPEC, ANY_SPEC],
        out_specs=tuple([HBM_SPEC] * (2 * n)),
        input_output_aliases={i: i for i in range(2 * n)},
        compiler_params=pltpu.CompilerParams(has_side_effects=EFFECT),
    )(*srcs, *dsts, send, recv, after)
    return out[n:]


def _cls(q):
    return WCLASSES[q][1], WCLASSES[q][2]


def _gather_start(shards, after):
    fulls = _place_own(shards)
    return _split_start("gather_start", shards, fulls, after,
                        lambda ref, q, mc, pj: ref,
                        lambda ref, q, mc, k: _shard_view(ref, *_cls(q), mc))


def _gather_wait(started, after):
    return _split_wait("gather_wait", started, after, lambda ref, q, k: _shard_view(ref, *_cls(q), 0))


def _scatter_start(pieces, after):
    lands = _take_own(pieces)
    return _split_start("scatter_start", pieces, lands, after,
                        lambda ref, q, mc, pj: _shard_view(ref, *_cls(q), pj),
                        lambda ref, q, mc, k: ref.at[k])


def _scatter_wait(started, after):
    return _split_wait("scatter_wait", started, after, lambda ref, q, k: ref.at[k])


GROUPS = {"A": (0, 1, 2), "B": (6,), "C": (7, 8, 9), "D": (3, 4, 5)}


def _own_spec(kind, shp, tr):
    R, C = shp
    if kind == "lead":
        return pl.BlockSpec((None, tr, C), lambda i, chip: (chip[0], i, 0))
    if kind == "row":
        return pl.BlockSpec((tr, C), lambda i, chip: (chip[0] * (R // tr) + i, 0))
    return pl.BlockSpec((tr, C), lambda i, chip: (i, chip[0]))


def _cast_place(shards, kind, shp, chip_arr):
    n = len(shards)
    R, C = shp
    tr = _row_tile(R, C)

    def body(chip_ref, *refs):
        for q in range(n):
            refs[n + q][...] = refs[q][...].astype(bf16)

    def in_spec(lead):
        return pl.BlockSpec((None,) * len(lead) + (tr, C), lambda i, chip: (*lead, i, 0))

    return pl.pallas_call(
        body, name="cast_place",
        grid_spec=pltpu.PrefetchScalarGridSpec(
            num_scalar_prefetch=1, grid=(R // tr,),
            in_specs=[in_spec(lead) for _, lead in shards],
            out_specs=[_own_spec(kind, shp, tr)] * n),
        out_shape=[SDS(_full_shape(kind, shp), bf16)] * n,
        compiler_params=_cp("arbitrary"),
    )(chip_arr, *[a for a, _ in shards])


def _sum_own_slots(piece, land, kind, shp, chip_arr):
    R, C = shp
    tr = _row_tile(R, C)

    def body(chip_ref, p_ref, l_ref, t_ref):
        t = p_ref[...].astype(f32)
        for k in range(N_CHIPS - 1):
            t = t + l_ref[k].astype(f32)
        t_ref[...] = t

    return pl.pallas_call(
        body, name="sum_own_slots",
        grid_spec=pltpu.PrefetchScalarGridSpec(
            num_scalar_prefetch=1, grid=(R // tr,),
            in_specs=[_own_spec(kind, shp, tr), pl.BlockSpec((N_CHIPS - 1, tr, C), lambda i, chip: (0, i, 0))],
            out_specs=pl.BlockSpec((tr, C), lambda i, chip: (i, 0))),
        out_shape=SDS((R, C), f32),
        compiler_params=_cp("arbitrary"),
    )(chip_arr, piece, land)


def _xfer_start(name, arrays, ng, after, src_view, dst_view):
    na = len(arrays)

    def body(*refs):
        arr = refs[:na]
        send, recv, token = refs[na + 1], refs[na + 2], refs[-1]
        c = lax.axis_index("c")
        chip = _chip_index()
        for mc in range(N_CHIPS):
            @pl.when(chip == mc)
            def _(mc=mc):
                for k in (1, 2, 3):
                    pj = mc ^ k
                    for i in range(ng):
                        pltpu.make_async_remote_copy(
                            src_ref=src_view(arr, i, mc, pj), dst_ref=dst_view(arr, i, mc, k),
                            send_sem=send.at[i * 3 + k - 1], recv_sem=recv.at[i * 3 + k - 1],
                            device_id=(pj >> 1, pj & 1, c), device_id_type=MESH).start()
        token[...] = jnp.zeros_like(token)

    return pl.pallas_call(
        body, name=name,
        out_shape=(pltpu.SemaphoreType.DMA((3 * ng,)), pltpu.SemaphoreType.DMA((3 * ng,)),
                   *[pltpu.HBM(a.shape, a.dtype) for a in arrays], SDS((8, 128), f32)),
        in_specs=[HBM_SPEC] * na + [ANY_SPEC],
        out_specs=(SEM_SPEC, SEM_SPEC, *([HBM_SPEC] * na), pl.BlockSpec(memory_space=pltpu.VMEM)),
        input_output_aliases={i: 2 + i for i in range(na)},
        compiler_params=pltpu.CompilerParams(has_side_effects=EFFECT),
    )(*[_in_hbm(a) for a in arrays], after)


def _xfer_wait(name, started, ng, after, arrival_view):
    send, recv = started[0], started[1]
    arrays = started[2:-1]
    na = len(arrays)

    def body(*refs):
        arr = refs[:na]
        send_sem, recv_sem = refs[na], refs[na + 1]
        x, y, c = lax.axis_index("x"), lax.axis_index("y"), lax.axis_index("c")
        for k in (1, 2, 3):
            for i in range(ng):
                arrival = arrival_view(arr, i)
                cp = pltpu.make_async_remote_copy(
                    src_ref=arrival, dst_ref=arrival, send_sem=send_sem.at[i * 3 + k - 1], recv_sem=recv_sem.at[i * 3 + k - 1],
                    device_id=(x, y, 1 - c), device_id_type=MESH)
                cp.wait_send()
                cp.wait_recv()

    return pl.pallas_call(
        body, name=name,
        out_shape=tuple(pltpu.HBM(a.shape, a.dtype) for a in arrays),
        in_specs=[HBM_SPEC] * na + [SEM_SPEC, SEM_SPEC, ANY_SPEC],
        out_specs=tuple([HBM_SPEC] * na),
        input_output_aliases={i: i for i in range(na)},
        compiler_params=pltpu.CompilerParams(has_side_effects=EFFECT),
    )(*arrays, send, recv, after)


def _gather_group_cast(group, shards_f32, chip_arr):
    fulls = [None] * len(group)
    by_shape = {}
    for i, q in enumerate(group):
        by_shape.setdefault(_cls(q), []).append(i)
    for (kind, shp), idx in by_shape.items():
        for i, f in zip(idx, _cast_place([shards_f32[i] for i in idx], kind, shp, chip_arr)):
            fulls[i] = f
    return fulls


def _gather_group_start(tag, group, fulls, after):
    view = lambda arr, i, mc, _: _shard_view(arr[i], *_cls(group[i]), mc)
    return _xfer_start("gather_start_" + tag, fulls, len(group), after, view, view)


def _gather_group_wait(tag, group, started, after):
    return _xfer_wait("gather_wait_" + tag, started, len(group), after, lambda arr, i: _shard_view(arr[i], *_cls(group[i]), 0))


def _scatter_group_start(tag, group, pieces, after):
    ng = len(group)
    lands = [lax.empty((N_CHIPS - 1,) + _cls(q)[1], bf16) for q in group]
    return _xfer_start("scatter_start_" + tag, list(pieces) + lands, ng, after,
                       lambda arr, i, mc, pj: _shard_view(arr[i], *_cls(group[i]), pj),
                       lambda arr, i, mc, k: arr[ng + i].at[k - 1])


def _scatter_group_wait(tag, group, started, after):
    ng = len(group)
    out = _xfer_wait("scatter_wait_" + tag, started, ng, after, lambda arr, i: arr[ng + i].at[0])
    return out[:ng], out[ng:]


def _mod_shards(c_all, ada_w, ada_b_sh):
    tn = ADA_SH // 3

    def body(c_ref, w_ref, b_ref, o_ref, cs_ref):
        cv = c_ref[...]
        cs = cv * _sigmoid(cv)
        cs_ref[...] = cs
        o_ref[...] = _dot(cs.astype(bf16), w_ref[...].astype(bf16)) + b_ref[...]

    return pl.pallas_call(
        body, name="mod_shards", grid=(DEPTH, 3),
        out_shape=[SDS((DEPTH, 8, ADA_SH), f32), SDS((8, D), f32)],
        in_specs=[pl.BlockSpec((8, D), lambda l, t: (0, 0)),
                  pl.BlockSpec((None, D, tn), lambda l, t: (l, 0, t)),
                  pl.BlockSpec((None, 1, tn), lambda l, t: (l, 0, t))],
        out_specs=[pl.BlockSpec((None, 8, tn), lambda l, t: (l, 0, t)), pl.BlockSpec((8, D), lambda l, t: (0, 0))],
        compiler_params=_cp("arbitrary", "arbitrary"),
    )(c_all, ada_w, ada_b_sh.reshape(DEPTH, 1, ADA_SH))


def _t5_bucket(dist):
    exact = NUM_BUCKETS // 2
    dd = np.maximum(dist, 1).astype(np.float32)
    large = exact + (np.log(dd / exact) / np.log(MAX_DISTANCE / exact) * (NUM_BUCKETS - exact)).astype(np.int32)
    large = np.minimum(large, NUM_BUCKETS - 1)
    return np.where(dist < exact, dist, large).astype(np.int32)


def _bucket_table():
    i = np.arange(BLK)[:, None]
    j = np.arange(2 * BLK)[None, :]
    rel = i - j + BLK
    return np.stack([_t5_bucket(np.maximum(rel, 0) * d) for d in DILATIONS]).astype(np.int32)


def _band():
    rel = lax.broadcasted_iota(jnp.int32, (BLK, 2 * BLK), 0) - lax.broadcasted_iota(jnp.int32, (BLK, 2 * BLK), 1) + BLK
    return (rel >= 0) & (rel <= BLK)


def _bias_blocks(rel_bias, buckets):
    def body(tab_ref, bk_ref, o_ref):
        h = pl.program_id(0)
        bk = bk_ref[...]
        acc = jnp.zeros((BLK, 2 * BLK), f32)
        for b in range(NUM_BUCKETS):
            acc = jnp.where(bk == b, tab_ref[b, h], acc)
        o_ref[...] = jnp.where(_band(), acc, NEG)

    return pl.pallas_call(
        body, name="bias_blocks", grid=(24,),
        out_shape=SDS((24, BLK, 2 * BLK), f32),
        in_specs=[pl.BlockSpec(memory_space=pltpu.SMEM), pl.BlockSpec((None, BLK, 2 * BLK), lambda h: (h // 8, 0, 0))],
        out_specs=pl.BlockSpec((None, BLK, 2 * BLK), lambda h: (h, 0, 0)),
        compiler_params=_cp("arbitrary"),
    )(rel_bias, buckets)


def _bias_grad(dsaccs, buckets):
    nl = len(dsaccs)

    def body(*refs):
        bk = refs[nl][...]
        tot = refs[0][...]
        for r in refs[1:nl]:
            tot = tot + r[...]
        lane = lax.broadcasted_iota(jnp.int32, (1, 128), 1)
        row = jnp.zeros((1, 128), f32)
        for b in range(NUM_BUCKETS):
            row = jnp.where(lane == b, jnp.sum(jnp.where(bk == b, tot, 0.0)), row)
        refs[nl + 1][...] = row

    return pl.pallas_call(
        body, name="bias_grad", grid=(24,),
        out_shape=SDS((24, 1, 128), f32),
        in_specs=[pl.BlockSpec((None, BLK, 2 * BLK), lambda h: (h, 0, 0))] * nl
                 + [pl.BlockSpec((None, BLK, 2 * BLK), lambda h: (h // 8, 0, 0))],
        out_specs=pl.BlockSpec((None, 1, 128), lambda h: (h, 0, 0)),
        compiler_params=_cp("arbitrary"),
    )(*dsaccs, buckets)


def _ffn_fwd(x, mod9, g3, wg, wu, wd, sub):
    S = x.shape[0]

    def body(x_ref, mod_ref, g_ref, wg_ref, wu_ref, wd_ref, xo_ref, h_ref, ga_ref, sa_ref, hid_ref, y_ref, acc):
        j = pl.program_id(1)

        @pl.when(j == 0)
        def _():
            h, _, _ = _norm_fwd(x_ref[...], g_ref[sub:sub + 1, :], mod_ref[3 * sub:3 * sub + 1, :], mod_ref[3 * sub + 1:3 * sub + 2, :])
            h_ref[...] = h.astype(bf16)
            acc[...] = jnp.zeros_like(acc)

        h = h_ref[...]
        a = _dot(h, wg_ref[...])
        u = _dot(h, wu_ref[...])
        sg = _sigmoid(a)
        sil = a * sg
        ga_ref[...] = (u * (sg * (1.0 + a * (1.0 - sg)))).astype(bf16)
        sa_ref[...] = sil.astype(bf16)
        hid_ref[...] = (sil * u).astype(bf16)
        acc[...] += _dot(hid_ref[...], wd_ref[...])

        @pl.when(j == N_CHIPS - 1)
        def _():
            y = acc[...]
            y_ref[...] = y.astype(bf16)
            xo_ref[...] = x_ref[...] + 0.5 * mod_ref[3 * sub + 2:3 * sub + 3, :] * y

    row = pl.BlockSpec((TMF, D), lambda i, j: (i, 0))
    hidb = pl.BlockSpec((None, TMF, FB), lambda i, j: (j, i, 0))
    hids = SDS((N_CHIPS, S, FB), bf16)
    return pl.pallas_call(
        body, name="ffn_fwd", grid=(S // TMF, N_CHIPS),
        out_shape=[SDS((S, D), f32), SDS((S, D), bf16), hids, hids, hids, SDS((S, D), bf16)],
        in_specs=[row, pl.BlockSpec((9, D), lambda i, j: (0, 0)), pl.BlockSpec((3, D), lambda i, j: (0, 0)),
                  pl.BlockSpec((None, D, FB), lambda i, j: (j, 0, 0)), pl.BlockSpec((None, D, FB), lambda i, j: (j, 0, 0)),
                  pl.BlockSpec((FB, D), lambda i, j: (j, 0))],
        out_specs=[row, row, hidb, hidb, hidb, row],
        scratch_shapes=[pltpu.VMEM((TMF, D), f32)],
        compiler_params=_cp("arbitrary", "arbitrary"),
    )(x, mod9, g3, wg, wu, wd)


def _ffn_bwd1(dxo, x, mod9, g3, y, ga, sa, wg, wu, wd, sub):
    S = x.shape[0]

    def body(dxo_ref, x_ref, mod_ref, g_ref, y_ref, ga_ref, sa_ref, wg_ref, wu_ref, wd_ref,
             dxi_ref, da_ref, du_ref, dy_ref, sm_ref, acc):
        i, j = pl.program_id(0), pl.program_id(1)
        gate = mod_ref[3 * sub + 2:3 * sub + 3, :]

        @pl.when((i == 0) & (j == 0))
        def _():
            sm_ref[...] = jnp.zeros_like(sm_ref)

        @pl.when(j == 0)
        def _():
            dxo_v = dxo_ref[...]
            dy_ref[...] = (0.5 * gate * dxo_v).astype(bf16)
            sm_ref[2:3, :] += jnp.sum(0.5 * y_ref[...].astype(f32) * dxo_v, axis=0, keepdims=True)
            acc[...] = jnp.zeros_like(acc)

        part = None
        for s in range(SH_STEP):
            dhid = _dot_nt(dy_ref[...], wd_ref[s * FB:(s + 1) * FB, :])
            da = (dhid * ga_ref[s].astype(f32)).astype(bf16)
            du = (dhid * sa_ref[s].astype(f32)).astype(bf16)
            da_ref[s] = da
            du_ref[s] = du
            t = _dot_nt(da, wg_ref[s]) + _dot_nt(du, wu_ref[s])
            part = t if part is None else part + t
        acc[...] += part

        @pl.when(j == N_CHIPS // SH_STEP - 1)
        def _():
            g = g_ref[sub:sub + 1, :]
            scale = mod_ref[3 * sub + 1:3 * sub + 2, :]
            _, xhat, rstd = _norm_fwd(x_ref[...], g, mod_ref[3 * sub:3 * sub + 1, :], scale)
            dx, dshift, dscale, dg = _norm_bwd(acc[...], xhat, rstd, g, scale)
            dxi_ref[...] = dxo_ref[...] + dx
            sm_ref[0:1, :] += dshift
            sm_ref[1:2, :] += dscale
            sm_ref[3:4, :] += dg

    row = pl.BlockSpec((TM, D), lambda i, j: (i, 0))
    hidb = pl.BlockSpec((SH_STEP, TM, FB), lambda i, j: (j, i, 0))
    wcol = pl.BlockSpec((SH_STEP, D, FB), lambda i, j: (j, 0, 0))
    return pl.pallas_call(
        body, name="ffn_bwd1", grid=(S // TM, N_CHIPS // SH_STEP),
        out_shape=[SDS((S, D), f32), SDS((N_CHIPS, S, FB), bf16), SDS((N_CHIPS, S, FB), bf16), SDS((S, D), bf16), SDS((8, D), f32)],
        in_specs=[row, row, pl.BlockSpec((9, D), lambda i, j: (0, 0)), pl.BlockSpec((3, D), lambda i, j: (0, 0)), row,
                  hidb, hidb, wcol, wcol, pl.BlockSpec((SH_STEP * FB, D), lambda i, j: (j, 0))],
        out_specs=[row, hidb, hidb, row, pl.BlockSpec((8, D), lambda i, j: (0, 0))],
        scratch_shapes=[pltpu.VMEM((TM, D), f32)],
        compiler_params=_cp("arbitrary", "arbitrary"),
    )(dxo, x, mod9, g3, y, ga, sa, wg, wu, wd)


def _ffn_bwd2(h, da, du, hid, dy):
    S = h.shape[0]
    ni = S // TMW

    def body(h_ref, da_ref, du_ref, hid_ref, dy_ref, dwg_ref, dwu_ref, dwd_ref, ag, au, ad):
        i = pl.program_id(1)

        @pl.when(i == 0)
        def _():
            ag[...] = jnp.zeros_like(ag)
            au[...] = jnp.zeros_like(au)
            ad[...] = jnp.zeros_like(ad)

        hv = h_ref[...]
        ag[...] += _dot_tn(hv, da_ref[...])
        au[...] += _dot_tn(hv, du_ref[...])
        ad[...] += _dot_tn(hid_ref[...], dy_ref[...])

        @pl.when(i == ni - 1)
        def _():
            dwg_ref[...] = ag[...].astype(bf16)
            dwu_ref[...] = au[...].astype(bf16)
            dwd_ref[...] = ad[...].astype(bf16)

    row = pl.BlockSpec((TMW, D), lambda j, i: (i, 0))
    hidb = pl.BlockSpec((None, TMW, FB), lambda j, i: (j, i, 0))
    wcol = pl.BlockSpec((None, D, FB), lambda j, i: (j, 0, 0))
    return pl.pallas_call(
        body, name="ffn_bwd2", grid=(N_CHIPS, ni),
        out_shape=[SDS((N_CHIPS, D, FB), bf16), SDS((N_CHIPS, D, FB), bf16), SDS((N_CHIPS * FB, D), bf16)],
        in_specs=[row, hidb, hidb, hidb, row],
        out_specs=[wcol, wcol, pl.BlockSpec((FB, D), lambda j, i: (j, 0))],
        scratch_shapes=[pltpu.VMEM((D, FB), f32), pltpu.VMEM((D, FB), f32), pltpu.VMEM((FB, D), f32)],
        compiler_params=_cp("arbitrary", "arbitrary"),
    )(h, da, du, hid, dy)


def _mix_qkv(x, mod9, g3, win):
    S = x.shape[0]

    def body(x_ref, mod_ref, g_ref, w_ref, h_ref, o_ref):
        @pl.when(pl.program_id(1) == 0)
        def _():
            h, _, _ = _norm_fwd(x_ref[...], g_ref[1:2, :], mod_ref[3:4, :], mod_ref[4:5, :])
            h_ref[...] = h.astype(bf16)

        o_ref[...] = _dot(h_ref[...], w_ref[...])

    row = pl.BlockSpec((TMP, D), lambda i, j: (i, 0))
    return pl.pallas_call(
        body, name="mix_qkv", grid=(S // TMP, QKV_W // CB),
        out_shape=[SDS((S, D), bf16), SDS((S, QKV_W), f32)],
        in_specs=[row, pl.BlockSpec((9, D), lambda i, j: (0, 0)), pl.BlockSpec((3, D), lambda i, j: (0, 0)),
                  pl.BlockSpec((D, CB), lambda i, j: (0, j))],
        out_specs=[row, pl.BlockSpec((TMP, CB), lambda i, j: (i, j))],
        compiler_params=_cp("arbitrary", "arbitrary"),
    )(x, mod9, g3, win)


def _mix_rest(h, win):
    S = h.shape[0]
    off = QKV_W // CB

    def body(h_ref, w_ref, o_ref):
        o_ref[...] = _dot(h_ref[...], w_ref[...]).astype(bf16)

    return pl.pallas_call(
        body, name="mix_rest", grid=(S // TMP, REST_W // CB),
        out_shape=SDS((S, REST_W), bf16),
        in_specs=[pl.BlockSpec((TMP, D), lambda i, j: (i, 0)), pl.BlockSpec((D, CB), lambda i, j: (0, off + j))],
        out_specs=pl.BlockSpec((TMP, CB), lambda i, j: (i, j)),
        compiler_params=_cp("arbitrary", "arbitrary"),
    )(h, win)


def _attn_fwd(qkv, bias, g):
    S = qkv.shape[0]
    d = DILATIONS[g]
    nq = Q_BLOCKS[g]
    Rb = BLK * d
    R = Rb * nq
    nb = S // R
    qb, kb, vb = 4 * g, 12 + 4 * g, 24 + 4 * g

    def body(q_ref, kc_ref, kp_ref, vc_ref, vp_ref, b_ref, o_ref, l_ref):
        n = pl.program_id(1)
        col = lax.broadcasted_iota(jnp.int32, (BLK, 2 * BLK), 1)
        first = jnp.where((col < BLK) & (n == 0), NEG, 0.0)
        head0 = lax.broadcasted_iota(jnp.int32, (1, 2 * HD), 1) < HD

        def one(b, r):
            sl = pl.ds(b * Rb + r, BLK, stride=d)
            q = q_ref[sl, :]
            if b == 0:
                kp, vp = kp_ref[pl.ds(r, BLK, stride=d), :], vp_ref[pl.ds(r, BLK, stride=d), :]
            else:
                before = pl.ds((b - 1) * Rb + r, BLK, stride=d)
                kp, vp = kc_ref[before, :], vc_ref[before, :]
            kk = jnp.concatenate([kp, kc_ref[sl, :]], axis=0).astype(bf16)
            vv = jnp.concatenate([vp, vc_ref[sl, :]], axis=0).astype(bf16)
            os, ls = [], []
            for hh in range(2):
                qm = jnp.where(head0 if hh == 0 else ~head0, q, 0.0).astype(bf16)
                s = _dot_nt(qm, kk) * SCALE + b_ref[hh]
                if b == 0:
                    s = s + first
                m = jnp.max(s, axis=-1, keepdims=True)
                p = jnp.exp(s - m)
                l = jnp.sum(p, axis=-1, keepdims=True)
                os.append(_dot(p.astype(bf16), vv) / l)
                ls.append(m + jnp.log(l))
            o_ref[sl, :] = jnp.where(head0, os[0], os[1])
            l_ref[sl, :] = jnp.where(head0, ls[0], ls[1])

        for b in range(nq):
            if d == 1:
                one(b, 0)
            else:
                lax.fori_loop(0, d, lambda r, carry, b=b: (one(b, r), carry)[1], 0, unroll=4)

    def blk(cb, prev):
        if prev:
            return pl.BlockSpec((Rb, 128), lambda hp, n: (jnp.maximum(n * nq - 1, 0), cb + hp))
        return pl.BlockSpec((R, 128), lambda hp, n: (n, cb + hp))

    outb = pl.BlockSpec((R, 128), lambda hp, n: (n, hp))
    return pl.pallas_call(
        body, name=f"attn_fwd_d{d}", grid=(4, nb),
        out_shape=[SDS((S, 512), f32), SDS((S, 512), f32)],
        in_specs=[blk(qb, False), blk(kb, False), blk(kb, True), blk(vb, False), blk(vb, True),
                  pl.BlockSpec((2, BLK, 2 * BLK), lambda hp, n: (4 * g + hp, 0, 0))],
        out_specs=[outb, outb],
        compiler_params=_cp("arbitrary", "arbitrary"),
    )(qkv, qkv, qkv, qkv, qkv, bias)


def _attn_bwd(qkv, do, o, lse, bias, dq_all, dk_all, dv_all, g):
    S = qkv.shape[0]
    d = DILATIONS[g]
    nq = Q_BLOCKS[g]
    Rb = BLK * d
    R = Rb * nq
    nb = S // R
    qb, kb, vb = 4 * g, 12 + 4 * g, 24 + 4 * g

    def body(q_ref, kc_ref, kp_ref, vc_ref, vp_ref, do_ref, o_ref, l_ref, b_ref, dqi, dki, dvi,
             dq_ref, dk_ref, dv_ref, ds_ref, ck, cv, tk, tv):
        n = pl.program_id(1)
        col = lax.broadcasted_iota(jnp.int32, (BLK, 2 * BLK), 1)
        first = jnp.where((col < BLK) & (n == 0), NEG, 0.0)

        @pl.when(n == 0)
        def _():
            ck[...] = jnp.zeros_like(ck)
            cv[...] = jnp.zeros_like(cv)
            ds_ref[...] = jnp.zeros_like(ds_ref)

        @pl.when(n < nb)
        def _():
            head0 = lax.broadcasted_iota(jnp.int32, (1, 2 * HD), 1) < HD

            def one(b, r):
                sl = pl.ds(b * Rb + r, BLK, stride=d)
                before = pl.ds((max(b, 1) - 1) * Rb + r, BLK, stride=d)
                q = q_ref[sl, :]
                if b == 0:
                    kp, vp = kp_ref[pl.ds(r, BLK, stride=d), :], vp_ref[pl.ds(r, BLK, stride=d), :]
                else:
                    kp, vp = kc_ref[before, :], vc_ref[before, :]
                kk = jnp.concatenate([kp, kc_ref[sl, :]], axis=0).astype(bf16)
                vv = jnp.concatenate([vp, vc_ref[sl, :]], axis=0).astype(bf16)
                dov, lv = do_ref[sl, :], l_ref[sl, :]
                prod = dov * o_ref[sl, :]
                qb, dob = q.astype(bf16), dov.astype(bf16)
                dqs, dks, dvs = [], [], []
                for hh in range(2):
                    msk = head0 if hh == 0 else ~head0
                    qm = jnp.where(msk, q, 0.0).astype(bf16)
                    dom = jnp.where(msk, dov, 0.0).astype(bf16)
                    dsum = jnp.sum(jnp.where(msk, prod, 0.0), axis=-1, keepdims=True)
                    s = _dot_nt(qm, kk) * SCALE + b_ref[hh]
                    if b == 0:
                        s = s + first
                    p = jnp.exp(s - lv[:, HD * hh:HD * hh + 1])
                    ds = p * (_dot_nt(dom, vv) - dsum)
                    ds_ref[hh] += ds
                    dsb = ds.astype(bf16)
                    dqs.append(_dot(dsb, kk) * SCALE)
                    dks.append(_dot_tn(dsb, qb) * SCALE)
                    dvs.append(_dot_tn(p.astype(bf16), dob))
                dq_ref[sl, :] = jnp.where(head0, dqs[0], dqs[1])
                dk = jnp.where(head0, dks[0], dks[1])
                dv = jnp.where(head0, dvs[0], dvs[1])
                tk[sl, :] = dk[BLK:]
                tv[sl, :] = dv[BLK:]
                if b == 0:
                    prev_rows = pl.ds((nq - 1) * Rb + r, BLK, stride=d)
                    ck[prev_rows, :] += dk[:BLK]
                    cv[prev_rows, :] += dv[:BLK]
                else:
                    tk[before, :] += dk[:BLK]
                    tv[before, :] += dv[:BLK]

            for b in range(nq):
                if d == 1:
                    one(b, 0)
                else:
                    lax.fori_loop(0, d, lambda r, carry, b=b: (one(b, r), carry)[1], 0, unroll=4)
            dk_ref[...] = ck[...]
            dv_ref[...] = cv[...]
            ck[...] = tk[...]
            cv[...] = tv[...]

        @pl.when(n == nb)
        def _():
            dk_ref[...] = ck[...]
            dv_ref[...] = cv[...]

    last = nb - 1

    def blk(cb, prev):
        if prev:
            return pl.BlockSpec((Rb, 128), lambda hp, n: (jnp.maximum(jnp.minimum(n, last) * nq - 1, 0), cb + hp))
        return pl.BlockSpec((R, 128), lambda hp, n: (jnp.minimum(n, last), cb + hp))

    cur = pl.BlockSpec((R, 128), lambda hp, n: (jnp.minimum(n, last), hp))
    anyspec = pl.BlockSpec(memory_space=pl.ANY)
    dqo = pl.BlockSpec((R, 128), lambda hp, n: (jnp.minimum(n, last), 4 * g + hp))
    dko = pl.BlockSpec((R, 128), lambda hp, n: (jnp.maximum(n - 1, 0), 4 * g + hp))
    return pl.pallas_call(
        body, name=f"attn_bwd_d{d}", grid=(4, nb + 1),
        out_shape=[SDS((S, 1536), f32), SDS((S, 1536), f32), SDS((S, 1536), f32), SDS((8, BLK, 2 * BLK), f32)],
        in_specs=[blk(qb, False), blk(kb, False), blk(kb, True), blk(vb, False), blk(vb, True), cur, cur, cur,
                  pl.BlockSpec((2, BLK, 2 * BLK), lambda hp, n: (4 * g + hp, 0, 0)), anyspec, anyspec, anyspec],
        out_specs=[dqo, dko, dko, pl.BlockSpec((2, BLK, 2 * BLK), lambda hp, n: (hp, 0, 0))],
        scratch_shapes=[pltpu.VMEM((R, 128), f32)] * 4,
        input_output_aliases={9: 0, 10: 1, 11: 2},
        compiler_params=_cp("arbitrary", "arbitrary"),
    )(qkv, qkv, qkv, qkv, qkv, do, o, lse, bias, dq_all, dk_all, dv_all)


def _conv_z(cc, ch, hc, hh, cw_ref, first):
    halo = jnp.where(first, 0.0, hc.astype(f32) * hh.astype(f32))
    T = jnp.concatenate([halo, cc * ch], axis=0)
    z = cw_ref[2:3, :] * T + cw_ref[1:2, :] * pltpu.roll(T, 1, 0) + cw_ref[0:1, :] * pltpu.roll(T, 2, 0)
    return T, z[HALO:]


def _rest_specs(tm, with_next):
    per = tm // HALO
    specs = [pl.BlockSpec((tm, D), functools.partial(lambda i, k: (i, k), k=k)) for k in range(5)]
    specs += [pl.BlockSpec((HALO, D), functools.partial(lambda i, k: (jnp.maximum(i * per - 1, 0), k), k=k)) for k in (1, 2)]
    return specs


def _mix_out_fwd(x, mod9, rest, ogs, lgs, cw, wco, wao, wo):
    S = x.shape[0]
    tm = TMX

    def body(x_ref, mod_ref, cb_ref, cc_ref, ch_ref, gc_ref, ga_ref, hc_ref, hh_ref,
             o0, o1, o2, l0, l1, l2, cw_ref, wco_ref, wao_ref, wo_ref,
             xo_ref, o_ref, lse_ref, yc_ref, ya_ref, out_ref):
        i = pl.program_id(0)
        lv = [l0[...], l1[...], l2[...]]
        mx = jnp.maximum(jnp.maximum(lv[0], lv[1]), lv[2])
        es = [jnp.exp(l - mx) for l in lv]
        den = es[0] + es[1] + es[2]
        o = (es[0] / den) * o0[...] + (es[1] / den) * o1[...] + (es[2] / den) * o2[...]
        o_ref[...] = o
        lse_ref[...] = mx + jnp.log(den)
        _, z = _conv_z(cc_ref[...].astype(f32), ch_ref[...].astype(f32), hc_ref[...], hh_ref[...], cw_ref, i == 0)
        p = (cb_ref[...].astype(f32) * z).astype(bf16)
        yc = _dot(p, wco_ref[...])
        ya = _dot(o.astype(bf16), wao_ref[...])
        yc_ref[...] = yc.astype(bf16)
        ya_ref[...] = ya.astype(bf16)
        merged = _sigmoid(gc_ref[...].astype(f32)) * yc + _sigmoid(ga_ref[...].astype(f32)) * ya
        out = _dot(merged.astype(bf16), wo_ref[...])
        out_ref[...] = out.astype(bf16)
        xo_ref[...] = x_ref[...] + mod_ref[5:6, :] * out

    row = pl.BlockSpec((tm, D), lambda i: (i, 0))
    att = pl.BlockSpec((tm, 512), lambda i: (i, 0))
    full = lambda shp: pl.BlockSpec(shp, lambda i: (0, 0))
    return pl.pallas_call(
        body, name="mix_out_fwd", grid=(S // tm,),
        out_shape=[SDS((S, D), f32), SDS((S, 512), f32), SDS((S, 512), f32), SDS((S, D), bf16), SDS((S, D), bf16), SDS((S, D), bf16)],
        in_specs=[row, full((9, D))] + _rest_specs(tm, False) + [att] * 6 + [full((3, D)), full((D, D)), full((512, D)), full((D, D))],
        out_specs=[row, att, att, row, row, row],
        compiler_params=_cp("arbitrary"),
    )(x, mod9, *([rest] * 7), *ogs, *lgs, cw, wco, wao, wo)


def _mix_out_bwd(dxo, mod9, outv, yc, ya, rest, o, cw, wco, wao, wo):
    S = dxo.shape[0]
    tm = TMX
    ni = S // tm

    def body(dxo_ref, mod_ref, out_ref, yc_ref, ya_ref, cb_ref, cc_ref, ch_ref, gc_ref, ga_ref, hc_ref, hh_ref,
             o_ref, cw_ref, wco_ref, wao_ref, wo_ref,
             dp_ref, dg2_ref, do_ref, dwco_ref, dwao_ref, dwo_ref, sm_ref, aco, aao, ao):
        i = pl.program_id(0)

        @pl.when(i == 0)
        def _():
            sm_ref[...] = jnp.zeros_like(sm_ref)
            aco[...] = jnp.zeros_like(aco)
            aao[...] = jnp.zeros_like(aao)
            ao[...] = jnp.zeros_like(ao)

        dxo_v = dxo_ref[...]
        sm_ref[2:3, :] += jnp.sum(out_ref[...].astype(f32) * dxo_v, axis=0, keepdims=True)
        dout = (mod_ref[5:6, :] * dxo_v).astype(bf16)
        dmerged = _dot_nt(dout, wo_ref[...])
        sc, sa = _sigmoid(gc_ref[...].astype(f32)), _sigmoid(ga_ref[...].astype(f32))
        ycv, yav = yc_ref[...].astype(f32), ya_ref[...].astype(f32)
        ao[...] += _dot_tn((sc * ycv + sa * yav).astype(bf16), dout)
        dyc = (dmerged * sc).astype(bf16)
        dya = (dmerged * sa).astype(bf16)
        dg2_ref[:, :D] = (dmerged * ycv * sc * (1.0 - sc)).astype(bf16)
        dg2_ref[:, D:] = (dmerged * yav * sa * (1.0 - sa)).astype(bf16)
        dp_ref[...] = _dot_nt(dyc, wco_ref[...]).astype(bf16)
        _, z = _conv_z(cc_ref[...].astype(f32), ch_ref[...].astype(f32), hc_ref[...], hh_ref[...], cw_ref, i == 0)
        aco[...] += _dot_tn((cb_ref[...].astype(f32) * z).astype(bf16), dyc)
        do_ref[...] = _dot_nt(dya, wao_ref[...])
        aao[...] += _dot_tn(o_ref[...].astype(bf16), dya)

        @pl.when(i == ni - 1)
        def _():
            dwco_ref[...] = aco[...].astype(bf16)
            dwao_ref[...] = aao[...].astype(bf16)
            dwo_ref[...] = ao[...].astype(bf16)

    row = pl.BlockSpec((tm, D), lambda i: (i, 0))
    att = pl.BlockSpec((tm, 512), lambda i: (i, 0))
    full = lambda shp: pl.BlockSpec(shp, lambda i: (0, 0))
    return pl.pallas_call(
        body, name="mix_out_bwd", grid=(ni,),
        out_shape=[SDS((S, D), bf16), SDS((S, 2 * D), bf16), SDS((S, 512), f32),
                   SDS((D, D), bf16), SDS((512, D), bf16), SDS((D, D), bf16), SDS((8, D), f32)],
        in_specs=[row, full((9, D)), row, row, row] + _rest_specs(tm, False) + [att, full((3, D)), full((D, D)), full((512, D)), full((D, D))],
        out_specs=[row, pl.BlockSpec((tm, 2 * D), lambda i: (i, 0)), att, full((D, D)), full((512, D)), full((D, D)), full((8, D))],
        scratch_shapes=[pltpu.VMEM((D, D), f32), pltpu.VMEM((512, D), f32), pltpu.VMEM((D, D), f32)],
        compiler_params=_cp("arbitrary"),
    )(dxo, mod9, outv, yc, ya, *([rest] * 7), o, cw, wco, wao, wo)


def _conv_bwd(dp, rest, cw):
    S = dp.shape[0]
    tm = TM
    per = tm // HALO
    nh = S // HALO
    ni = S // tm

    def body(dp_ref, dpn_ref, cb_ref, cbn_ref, cc_ref, ch_ref, hc_ref, hh_ref, cw_ref, d3_ref, sm_ref):
        i = pl.program_id(0)

        @pl.when(i == 0)
        def _():
            sm_ref[...] = jnp.zeros_like(sm_ref)

        cc, ch = cc_ref[...].astype(f32), ch_ref[...].astype(f32)
        T, z = _conv_z(cc, ch, hc_ref[...], hh_ref[...], cw_ref, i == 0)
        dpv = dp_ref[...].astype(f32)
        cbv = cb_ref[...].astype(f32)
        dz = dpv * cbv
        dzn = jnp.where(i == ni - 1, 0.0, dpn_ref[...].astype(f32) * cbn_ref[...].astype(f32))
        E = jnp.concatenate([dz, dzn], axis=0)
        ne = tm + HALO
        dT = cw_ref[2:3, :] * E + cw_ref[1:2, :] * pltpu.roll(E, ne - 1, 0) + cw_ref[0:1, :] * pltpu.roll(E, ne - 2, 0)
        dT = dT[:tm]
        d3_ref[:, :D] = (dpv * z).astype(bf16)
        d3_ref[:, D:2 * D] = (dT * ch).astype(bf16)
        d3_ref[:, 2 * D:] = (dT * cc).astype(bf16)
        sm_ref[2:3, :] += jnp.sum(dz * T[HALO:], axis=0, keepdims=True)
        sm_ref[1:2, :] += jnp.sum(dz * pltpu.roll(T, 1, 0)[HALO:], axis=0, keepdims=True)
        sm_ref[0:1, :] += jnp.sum(dz * pltpu.roll(T, 2, 0)[HALO:], axis=0, keepdims=True)

    row = pl.BlockSpec((tm, D), lambda i: (i, 0))
    nxt = pl.BlockSpec((HALO, D), lambda i: (jnp.minimum((i + 1) * per, nh - 1), 0))
    col = lambda k: pl.BlockSpec((tm, D), lambda i: (i, k))
    prv = lambda k: pl.BlockSpec((HALO, D), lambda i: (jnp.maximum(i * per - 1, 0), k))
    return pl.pallas_call(
        body, name="conv_bwd", grid=(ni,),
        out_shape=[SDS((S, 3 * D), bf16), SDS((8, D), f32)],
        in_specs=[row, nxt, col(0), nxt, col(1), col(2), prv(1), prv(2), pl.BlockSpec((3, D), lambda i: (0, 0))],
        out_specs=[pl.BlockSpec((tm, 3 * D), lambda i: (i, 0)), pl.BlockSpec((8, D), lambda i: (0, 0))],
        compiler_params=_cp("arbitrary"),
    )(dp, dp, rest, rest, rest, rest, rest, rest, cw)


_DU_RANGES = ((0, 3), (3, 6), (6, 9), (9, 15), (15, 19))
N_CBLK = IN_W // CB


def _mix_in_bwd_dh(dxo, x, mod9, g3, dus, win):
    S = x.shape[0]

    def body(dxo_ref, x_ref, mod_ref, g_ref, s0, s1, s2, s3, s4, w_ref, dxi_ref, sm_ref, acc):
        i, kb = pl.program_id(0), pl.program_id(1)

        @pl.when((i == 0) & (kb == 0))
        def _():
            sm_ref[...] = jnp.zeros_like(sm_ref)

        @pl.when(kb == 0)
        def _():
            acc[...] = jnp.zeros_like(acc)

        for src, (lo, hi) in zip((s0, s1, s2, s3, s4), _DU_RANGES):
            @pl.when((kb >= lo) & (kb < hi))
            def _(src=src):
                acc[...] += _dot_nt(src[...].astype(bf16), w_ref[...])

        @pl.when(kb == N_CBLK - 1)
        def _():
            g, scale = g_ref[1:2, :], mod_ref[4:5, :]
            _, xhat, rstd = _norm_fwd(x_ref[...], g, mod_ref[3:4, :], scale)
            dx, dshift, dscale, dg = _norm_bwd(acc[...], xhat, rstd, g, scale)
            dxi_ref[...] = dxo_ref[...] + dx
            sm_ref[0:1, :] += dshift
            sm_ref[1:2, :] += dscale
            sm_ref[3:4, :] += dg

    row = pl.BlockSpec((TMP, D), lambda i, kb: (i, 0))

    def src_spec(lo, hi):
        return pl.BlockSpec((TMP, CB), lambda i, kb: (i, jnp.clip(kb - lo, 0, hi - lo - 1)))

    return pl.pallas_call(
        body, name="mix_in_bwd_dh", grid=(S // TMP, N_CBLK),
        out_shape=[SDS((S, D), f32), SDS((8, D), f32)],
        in_specs=[row, row, pl.BlockSpec((9, D), lambda i, kb: (0, 0)), pl.BlockSpec((3, D), lambda i, kb: (0, 0))]
                 + [src_spec(lo, hi) for lo, hi in _DU_RANGES] + [pl.BlockSpec((D, CB), lambda i, kb: (0, kb))],
        out_specs=[row, pl.BlockSpec((8, D), lambda i, kb: (0, 0))],
        scratch_shapes=[pltpu.VMEM((TMP, D), f32)],
        compiler_params=_cp("arbitrary", "arbitrary"),
    )(dxo, x, mod9, g3, *dus, win)


def _mix_in_bwd_dw(h, dus):
    S = h.shape[0]
    ni = S // TMW

    def body(h_ref, s0, s1, s2, s3, s4, dw_ref, acc):
        kb, i = pl.program_id(0), pl.program_id(1)

        @pl.when(i == 0)
        def _():
            acc[...] = jnp.zeros_like(acc)

        for src, (lo, hi) in zip((s0, s1, s2, s3, s4), _DU_RANGES):
            @pl.when((kb >= lo) & (kb < hi))
            def _(src=src):
                rows = pl.ds(pl.multiple_of(i * TMW, TMW), TMW)
                acc[...] += _dot_tn(h_ref[rows, :], src[...].astype(bf16))

        @pl.when(i == ni - 1)
        def _():
            dw_ref[...] = acc[...].astype(bf16)

    def src_spec(lo, hi):
        def imap(kb, i):
            on = (kb >= lo) & (kb < hi)
            return (jnp.where(on, i, 0), jnp.clip(kb - lo, 0, hi - lo - 1))
        return pl.BlockSpec((TMW, CB), imap)

    return pl.pallas_call(
        body, name="mix_in_bwd_dw", grid=(N_CBLK, ni),
        out_shape=SDS((D, IN_W), bf16),
        in_specs=[pl.BlockSpec((S, D), lambda kb, i: (0, 0))] + [src_spec(lo, hi) for lo, hi in _DU_RANGES],
        out_specs=pl.BlockSpec((D, CB), lambda kb, i: (0, kb)),
        scratch_shapes=[pltpu.VMEM((D, CB), f32)],
        compiler_params=_cp("arbitrary", "arbitrary"),
    )(h, *dus)


def _loss_head(x, fg, tgt):
    S = x.shape[0]

    def body(x_ref, g_ref, t_ref, ls_ref, dx_ref, sm_ref):
        i = pl.program_id(0)

        @pl.when(i == 0)
        def _():
            ls_ref[...] = jnp.zeros_like(ls_ref)
            sm_ref[...] = jnp.zeros_like(sm_ref)

        xv, g = x_ref[...], g_ref[...]
        rstd = lax.rsqrt(jnp.mean(xv * xv, axis=-1, keepdims=True) + EPS)
        xhat = xv * rstd
        e = xhat * g - t_ref[...]
        ls_ref[...] += 0.5 * jnp.sum(jnp.mean(e * e, axis=-1, keepdims=True))
        dy = e * (1.0 / D)
        sm_ref[0:1, :] += jnp.sum(dy * xhat, axis=0, keepdims=True)
        dxh = dy * g
        dx_ref[...] = rstd * (dxh - xhat * jnp.mean(dxh * xhat, axis=-1, keepdims=True))

    row = pl.BlockSpec((TM, D), lambda i: (i, 0))
    return pl.pallas_call(
        body, name="loss_head", grid=(S // TM,),
        out_shape=[SDS((8, 128), f32), SDS((S, D), f32), SDS((8, D), f32)],
        in_specs=[row, pl.BlockSpec((1, D), lambda i: (0, 0)), row],
        out_specs=[pl.BlockSpec((8, 128), lambda i: (0, 0)), row, pl.BlockSpec((8, D), lambda i: (0, 0))],
        compiler_params=_cp("arbitrary"),
    )(x, fg, tgt)


def _adam(w, g, m, v):
    m2 = B1 * m + (1.0 - B1) * g
    v2 = B2 * v + (1.0 - B2) * (g * g)
    delta = -LR * ((m2 / BC1) / (jnp.sqrt(v2 / BC2) + AEPS) + WD * w)
    return delta, m2, v2


def _row_tile(rows, cols):
    for tr in (512, 352, 256, 128, 64):
        if rows % tr == 0 and tr * cols * 4 <= (5 << 18):
            return tr
    raise ValueError((rows, cols))


def _sum_slots(land):
    _, R, C = land.shape
    tr = _row_tile(R, C)

    def body(l_ref, t_ref):
        t = l_ref[0].astype(f32)
        for k in range(1, N_CHIPS):
            t = t + l_ref[k].astype(f32)
        t_ref[...] = t

    return pl.pallas_call(
        body, name="sum_slots", grid=(R // tr,),
        out_shape=SDS((R, C), f32),
        in_specs=[pl.BlockSpec((N_CHIPS, tr, C), lambda i: (0, i, 0))],
        out_specs=pl.BlockSpec((tr, C), lambda i: (i, 0)),
        compiler_params=_cp("arbitrary"),
    )(land)


def _adamw_pair(w2, m2, v2, ta, tb, outs, slot):
    R, C = ta.shape
    tr = _row_tile(R, C)
    nrt = R // tr

    def body(w_ref, m_ref, v_ref, ta_ref, tb_ref, g_in, d_in, m_in, v_in, g_ref, d_ref, mo_ref, vo_ref):
        g = ta_ref[...] + tb_ref[...]
        delta, mn, vn = _adam(w_ref[...], g, m_ref[...], v_ref[...])
        g_ref[...] = g
        d_ref[...] = delta
        mo_ref[...] = mn
        vo_ref[...] = vn

    big = pl.BlockSpec((tr, C), lambda i: (slot * nrt + i, 0))
    loc = pl.BlockSpec((tr, C), lambda i: (i, 0))
    anyspec = pl.BlockSpec(memory_space=pl.ANY)
    return pl.pallas_call(
        body, name="adamw_pair", grid=(nrt,),
        out_shape=[SDS(o.shape, f32) for o in outs],
        in_specs=[big, big, big, loc, loc] + [anyspec] * 4,
        out_specs=[big] * 4,
        input_output_aliases={5: 0, 6: 1, 7: 2, 8: 3},
        compiler_params=_cp("arbitrary"),
    )(w2, m2, v2, ta, tb, *outs)


def _adamw_small(w, g, m, v):
    def body(w_ref, g_ref, m_ref, v_ref, d_ref, mo_ref, vo_ref):
        delta, mn, vn = _adam(w_ref[...], g_ref[...], m_ref[...], v_ref[...])
        d_ref[...] = delta
        mo_ref[...] = mn
        vo_ref[...] = vn

    return pl.pallas_call(body, name="adamw_small", out_shape=[SDS(w.shape, f32)] * 3)(w, g, m, v)


def _ada_w_update(cs_all, dmod_sh, w, m, v):
    tr = 256

    def body(cs_ref, dm_ref, w_ref, m_ref, v_ref, g_ref, d_ref, mo_ref, vo_ref):
        g = _dot_tn(cs_ref[...].astype(bf16), dm_ref[...].astype(bf16))
        delta, mn, vn = _adam(w_ref[...], g, m_ref[...], v_ref[...])
        g_ref[...] = g
        d_ref[...] = delta
        mo_ref[...] = mn
        vo_ref[...] = vn

    blk = pl.BlockSpec((None, tr, ADA_SH), lambda l, i: (l, i, 0))
    return pl.pallas_call(
        body, name="ada_w_update", grid=(DEPTH, D // tr),
        out_shape=[SDS(w.shape, f32)] * 4,
        in_specs=[pl.BlockSpec((8, tr), lambda l, i: (0, i)), pl.BlockSpec((None, 8, ADA_SH), lambda l, i: (l, 0, 0)), blk, blk, blk],
        out_specs=[blk] * 4,
        compiler_params=_cp("arbitrary", "arbitrary"),
    )(cs_all, dmod_sh, w, m, v)


def _sum_devices(gathered):
    _, R, C = gathered.shape

    def body(g_ref, o_ref):
        t = g_ref[0]
        for k in range(1, 8):
            t = t + g_ref[k]
        o_ref[...] = t

    return pl.pallas_call(body, name="sum_devices", out_shape=SDS((R, C), f32))(gathered)


def _layer_fwd(x, mod9, g3, cw, getw, bias):
    W = {}

    def take(gname, after, mod9):
        w, tok = getw(gname, after)
        W.update(w)
        return mod9 if tok is None else mod9 + tok[0, 0]

    mod9 = take("A", x, mod9)
    x1, h1, a1, u1, hid1, y1 = _ffn_fwd(x, mod9, g3, W["wg0"], W["wu0"], W["wd0"], 0)
    mod9 = take("B", x1, mod9)
    hm, qkv = _mix_qkv(x1, mod9, g3, W["win"])
    rest = _mix_rest(hm, W["win"])
    ogs, lgs = [], []
    for g in range(3):
        og, lg = _attn_fwd(qkv, bias, g)
        ogs.append(og)
        lgs.append(lg)
    mod9 = take("C", ogs[2], mod9)
    x2, o, lse, yc, ya, outv = _mix_out_fwd(x1, mod9, rest, ogs, lgs, cw, W["wco"], W["wao"], W["wo"])
    mod9 = take("D", x2, mod9)
    x3, h3, a3, u3, hid3, y3 = _ffn_fwd(x2, mod9, g3, W["wg1"], W["wu1"], W["wd1"], 2)
    saved = dict(x0=x, x1=x1, x2=x2, h1=h1, a1=a1, u1=u1, hid1=hid1, y1=y1, hm=hm, qkv=qkv, rest=rest, o=o, lse=lse, yc=yc, ya=ya,
                 outv=outv, h3=h3, a3=a3, u3=u3, hid3=hid3, y3=y3)
    return x3, saved, W


def _layer_bwd(dx, sv, mod9, g3, cw, W, bias, emit):
    S = dx.shape[0]
    dw = {}

    def send(gname, mod9):
        tok = emit(gname, dw)
        return mod9 if tok is None else mod9 + tok[0, 0]

    dx2, da, du, dy, sm3 = _ffn_bwd1(dx, sv["x2"], mod9, g3, sv["y3"], sv["a3"], sv["u3"], W["wg1"], W["wu1"], W["wd1"], 2)
    dw["wg1"], dw["wu1"], dw["wd1"] = _ffn_bwd2(sv["h3"], da, du, sv["hid3"], dy)
    mod9 = send("D", mod9)
    dp, dg2, do, dw["wco"], dw["wao"], dw["wo"], smo = _mix_out_bwd(
        dx2, mod9, sv["outv"], sv["yc"], sv["ya"], sv["rest"], sv["o"], cw, W["wco"], W["wao"], W["wo"])
    mod9_c = send("C", mod9)
    cw = cw + (mod9_c - mod9)[0:1, :]
    mod9 = mod9_c
    d3, smc = _conv_bwd(dp, sv["rest"], cw)
    dq = lax.empty((S, 1536), f32)
    dk = lax.empty((S, 1536), f32)
    dv = lax.empty((S, 1536), f32)
    dsaccs = []
    for g in range(3):
        dq, dk, dv, dsg = _attn_bwd(sv["qkv"], do, sv["o"], sv["lse"], bias, dq, dk, dv, g)
        dsaccs.append(dsg)
    dus = (dq, dk, dv, d3, dg2)
    dx1, smm = _mix_in_bwd_dh(dx2, sv["x1"], mod9, g3, dus, W["win"])
    dw["win"] = _mix_in_bwd_dw(sv["hm"], dus)
    mod9 = send("B", mod9)
    dx0, da, du, dy, sm1 = _ffn_bwd1(dx1, sv["x0"], mod9, g3, sv["y1"], sv["a1"], sv["u1"], W["wg0"], W["wu0"], W["wd0"], 0)
    dw["wg0"], dw["wu0"], dw["wd0"] = _ffn_bwd2(sv["h1"], da, du, sv["hid1"], dy)
    send("A", mod9)
    dmod = jnp.concatenate([sm1[0:3], smm[0:2], smo[2:3], sm3[0:3]], axis=0)
    dng = jnp.concatenate([sm1[3:4], smm[3:4], sm3[3:4]], axis=0)
    return dx0, dmod, dng, smc[0:3], jnp.concatenate(dsaccs, axis=0)


def _chip_cols(a, chip, width):
    return lax.dynamic_slice_in_dim(a, chip * width, width, axis=a.ndim - 1)


def kernel(x, c, ada_w, ada_b, norm_g, ffn_w_gate, ffn_w_up, ffn_w_down, w_in, conv_w, w_conv_out, w_attn_out, w_o, rel_bias, final_g, loss_target, m_ada_w, m_ada_b, m_norm_g, m_ffn_w_gate, m_ffn_w_up, m_ffn_w_down, m_w_in, m_conv_w, m_w_conv_out, m_w_attn_out, m_w_o, m_rel_bias, m_final_g, v_ada_w, v_ada_b, v_norm_g, v_ffn_w_gate, v_ffn_w_up, v_ffn_w_down, v_w_in, v_conv_w, v_w_conv_out, v_w_attn_out, v_w_o, v_rel_bias, v_final_g):
    ix, iy, ic = lax.axis_index("x"), lax.axis_index("y"), lax.axis_index("c")
    chip = 2 * ix + iy
    dev = 4 * ix + 2 * iy + ic
    xs = x.reshape(x.shape[1:])
    S = xs.shape[0]
    qd = D // N_CHIPS

    chip_arr = jnp.reshape(chip, (1,)).astype(jnp.int32)
    names = [w[0] for w in WCLASSES]

    def layer_shards(l):
        return [(ffn_w_gate, (l, 0)), (ffn_w_up, (l, 0)), (ffn_w_down, (l, 0)), (ffn_w_gate, (l, 1)), (ffn_w_up, (l, 1)),
                (ffn_w_down, (l, 1)), (w_in, (l,)), (w_conv_out, (l,)), (w_attn_out, (l,)), (w_o, (l,))]

    started = {}
    extra_starts = {(0, "A"): [(0, "B")], (0, "B"): [(0, "C"), (0, "D"), (1, "A")]}

    casts = {}

    def cast_group(l, gname):
        shards = layer_shards(l)
        casts[(l, gname)] = _gather_group_cast(GROUPS[gname], [shards[q] for q in GROUPS[gname]], chip_arr)

    def start_gather(l, gname, after):
        started[(l, gname)] = _gather_group_start(f"l{l}{gname}", GROUPS[gname], casts[(l, gname)], after)
        return started[(l, gname)][-1]

    pad8 = lambda a: jnp.pad(a, ((0, -a.shape[0] % 8), (0, 0)))
    pack = jnp.concatenate([pad8(c), pad8(norm_g.reshape(3, D)), pad8(conv_w.reshape(3, D))], axis=0)
    g1 = _allgather_small(pack).reshape(8, 24, D)
    c_all = g1[:, 0]
    by_chip = g1[0::2]
    ng_full = jnp.concatenate([by_chip[j, 8:11].reshape(DEPTH, 3, qd) for j in range(N_CHIPS)], axis=-1)
    cw_full = jnp.concatenate([by_chip[j, 16:19].reshape(DEPTH, 3, qd) for j in range(N_CHIPS)], axis=-1)
    mod_sh, cs_all = _mod_shards(c_all, ada_w, _chip_cols(ada_b, chip, ADA_SH))
    g2 = _allgather_small(mod_sh.reshape(DEPTH * 8, ADA_SH)).reshape(8, DEPTH, 8, ADA_SH)
    mine = lax.dynamic_index_in_dim(g2[0::2], dev, axis=2, keepdims=False)
    mod = jnp.transpose(mine, (1, 0, 2)).reshape(DEPTH, 9, D)

    cast_group(0, "A")
    start_gather(0, "A", mod)
    for l in range(DEPTH):
        for gname in GROUPS:
            if (l, gname) not in casts:
                cast_group(l, gname)
    buckets = jnp.asarray(_bucket_table())
    bias = _bias_blocks(rel_bias, buckets)
    last_cast = casts[(DEPTH - 1, "D")][-1]

    def make_getw(l):
        def getw(gname, after):
            if (l, gname) == (0, "A"):
                after = last_cast
            full = _gather_group_wait(f"l{l}{gname}", GROUPS[gname], started[(l, gname)], after)
            tok = None
            for nl, ng in extra_starts.get((l, gname), []) + [(l + 1, gname)]:
                if nl < DEPTH and (nl, ng) not in started:
                    tok = start_gather(nl, ng, full[0] if tok is None else tok)
            return {names[q]: f for q, f in zip(GROUPS[gname], full)}, tok
        return getw

    Ws, saves = [], []
    xc = xs
    for l in range(DEPTH):
        xc, sv, W = _layer_fwd(xc, mod[l], ng_full[l], cw_full[l], make_getw(l), bias)
        Ws.append(W)
        saves.append(sv)

    ls, dx, smf = _loss_head(xc, final_g.reshape(1, D), loss_target.reshape(loss_target.shape[1:]))
    loss = lax.psum(ls[0, 0], ("x", "y", "c"))

    params = dict(wg=ffn_w_gate, wu=ffn_w_up, wd=ffn_w_down, win=w_in, wco=w_conv_out, wao=w_attn_out, wo=w_o)
    moms = dict(wg=m_ffn_w_gate, wu=m_ffn_w_up, wd=m_ffn_w_down, win=m_w_in, wco=m_w_conv_out, wao=m_w_attn_out, wo=m_w_o)
    vars_ = dict(wg=v_ffn_w_gate, wu=v_ffn_w_up, wd=v_ffn_w_down, win=v_w_in, wco=v_w_conv_out, wao=v_w_attn_out, wo=v_w_o)
    flat = lambda a: a.reshape(-1, a.shape[-1])
    big_out = {k: [lax.empty(flat(p).shape, f32) for _ in range(4)] for k, p in params.items()}
    dmods, dngs, dcws, dsaccs = [None] * DEPTH, [None] * DEPTH, [None] * DEPTH, [None] * DEPTH

    def finish(l, gname, started, after):
        group = GROUPS[gname]
        pieces, lands = _scatter_group_wait(f"l{l}{gname}", group, started, after)
        ts = [_sum_own_slots(pieces[i], lands[i], *_cls(q), chip_arr) for i, q in enumerate(group)]
        tsib = _swap_sibling(ts)
        for i, q in enumerate(group):
            name = names[q]
            key = name.rstrip("01")
            slot = 2 * l + int(name[-1]) if name[-1] in "01" else l
            big_out[key] = _adamw_pair(flat(params[key]), flat(moms[key]), flat(vars_[key]), ts[i], tsib[i], big_out[key], slot)

    pending, tok = [], None
    for l in reversed(range(DEPTH)):
        modl = mod[l] if tok is None else mod[l] + tok[0, 0]
        mine = []

        def emit(gname, dw, l=l, mine=mine):
            prev = mine[-1][2][-1] if mine else dx
            mine.append((l, gname, _scatter_group_start(f"l{l}{gname}", GROUPS[gname], [dw[names[q]] for q in GROUPS[gname]], prev)))
            return mine[-1][2][-1]

        dx, dmods[l], dngs[l], dcws[l], dsaccs[l] = _layer_bwd(dx, saves[l], modl, ng_full[l], cw_full[l], Ws[l], bias, emit)
        for pl_, pg, pst in pending:
            finish(pl_, pg, pst, dx)
        pending, tok = mine, mine[-1][2][-1]
    for pl_, pg, pst in pending[:-1]:
        finish(pl_, pg, pst, pending[-1][2][-1])

    drb = jnp.transpose(_bias_grad(dsaccs, buckets)[:, 0, :NUM_BUCKETS])
    drb_row = jnp.pad(drb.reshape(1, NUM_BUCKETS * 24), ((0, 0), (0, D - NUM_BUCKETS * 24)))
    pack2 = jnp.concatenate([pad8(a) for a in dmods] + [pad8(a) for a in dngs] + [pad8(a) for a in dcws] + [smf, pad8(drb_row)], axis=0)
    n_rows = pack2.shape[0]
    g3 = _allgather_small(pack2).reshape(8, n_rows, D)
    tot = _sum_devices(g3)
    o_ng, o_cw, o_fg, o_rb = 16 * DEPTH, 24 * DEPTH, 32 * DEPTH, 32 * DEPTH + 8
    g_ada_b = jnp.stack([tot[16 * l:16 * l + 9] for l in range(DEPTH)]).reshape(DEPTH, 9 * D)
    g_norm_g = _chip_cols(jnp.stack([tot[o_ng + 8 * l:o_ng + 8 * l + 3] for l in range(DEPTH)]), chip, qd)
    g_conv_w = _chip_cols(jnp.stack([tot[o_cw + 8 * l:o_cw + 8 * l + 3] for l in range(DEPTH)]), chip, qd)
    g_final_g = tot[o_fg]
    g_rel_bias = tot[o_rb, :NUM_BUCKETS * 24].reshape(NUM_BUCKETS, 24)
    dmod_all = jnp.stack([g3[:, 16 * l:16 * l + 9].reshape(8, 9 * D) for l in range(DEPTH)])
    dmod_sh = _chip_cols(dmod_all, chip, ADA_SH)
    g_ada_w, d_ada_w, nm_ada_w, nv_ada_w = _ada_w_update(cs_all, dmod_sh, ada_w, m_ada_w, v_ada_w)

    def small(w, g, m, v):
        shp = w.shape
        to2 = lambda a: a.reshape(-1, shp[-1])
        return [o.reshape(shp) for o in _adamw_small(to2(w), to2(g), to2(m), to2(v))]

    d_ada_b, nm_ada_b, nv_ada_b = small(ada_b, g_ada_b, m_ada_b, v_ada_b)
    d_norm_g, nm_norm_g, nv_norm_g = small(norm_g, g_norm_g, m_norm_g, v_norm_g)
    d_conv_w, nm_conv_w, nv_conv_w = small(conv_w, g_conv_w, m_conv_w, v_conv_w)
    d_rel_bias, nm_rel_bias, nv_rel_bias = small(rel_bias, g_rel_bias, m_rel_bias, v_rel_bias)
    d_final_g, nm_final_g, nv_final_g = small(final_g, g_final_g, m_final_g, v_final_g)

    behind = nv_ada_w[0, 0:8, 0:128]
    for key in big_out:
        behind = behind + big_out[key][3][0:8, 0:128]
    finish(*pending[-1], behind)

    def big(key, which):
        return big_out[key][which].reshape(params[key].shape)

    grads = [g_ada_w, g_ada_b, g_norm_g, big("wg", 0), big("wu", 0), big("wd", 0), big("win", 0), g_conv_w, big("wco", 0),
             big("wao", 0), big("wo", 0), g_rel_bias, g_final_g]
    deltas = [d_ada_w, d_ada_b, d_norm_g, big("wg", 1), big("wu", 1), big("wd", 1), big("win", 1), d_conv_w, big("wco", 1),
              big("wao", 1), big("wo", 1), d_rel_bias, d_final_g]
    new_m = [nm_ada_w, nm_ada_b, nm_norm_g, big("wg", 2), big("wu", 2), big("wd", 2), big("win", 2), nm_conv_w, big("wco", 2),
             big("wao", 2), big("wo", 2), nm_rel_bias, nm_final_g]
    new_v = [nv_ada_w, nv_ada_b, nv_norm_g, big("wg", 3), big("wu", 3), big("wd", 3), big("win", 3), nv_conv_w, big("wco", 3),
             big("wao", 3), big("wo", 3), nv_rel_bias, nv_final_g]
    return (loss, dx.reshape(x.shape), *grads, *deltas, *new_m, *new_v)
```

```python
import functools

import numpy as np
import jax
import jax.numpy as jnp
from jax import lax
from jax.experimental import pallas as pl
from jax.experimental.pallas import tpu as pltpu

f32, bf16 = jnp.float32, jnp.bfloat16
SDS = jax.ShapeDtypeStruct
MESH = pl.DeviceIdType.MESH

D = 1024
DEPTH = 4
N_CHIPS = 4
FB = 704
HD = 64
QKV_W = 4608
REST_W = 5120
IN_W = QKV_W + REST_W
WIN_SH = IN_W // N_CHIPS
ADA_SH = 9 * D // N_CHIPS
BLK = 128
DILATIONS = (1, 4, 16)
Q_BLOCKS = (4, 1, 1)
NUM_BUCKETS, MAX_DISTANCE = 32, 2048
EPS = 1e-6
NEG = -1e30
SCALE = HD ** -0.5
LR, B1, B2, AEPS, WD, STEP = 0.001, 0.9, 0.999, 1e-08, 0.01, 10
BC1 = 1.0 - B1 ** STEP
BC2 = 1.0 - B2 ** STEP
VMEM_LIMIT = 56 * 1024 * 1024
TM = 512
TMW = 1024
TMP = 1024
TMF = 1024
SH_STEP = 2
TMX = 256
HALO = 16
CB = 512


def _cp(*sem):
    return pltpu.CompilerParams(dimension_semantics=sem if sem else None, vmem_limit_bytes=VMEM_LIMIT)


def _dot(a, b):
    return jnp.dot(a, b, preferred_element_type=f32)


def _dot_nt(a, b):
    return lax.dot_general(a, b, (((1,), (1,)), ((), ())), preferred_element_type=f32)


def _dot_tn(a, b):
    return lax.dot_general(a, b, (((0,), (0,)), ((), ())), preferred_element_type=f32)


def _sigmoid(x):
    return 0.5 * jnp.tanh(0.5 * x) + 0.5


def _norm_fwd(x, g, shift, scale):
    rstd = lax.rsqrt(jnp.mean(x * x, axis=-1, keepdims=True) + EPS)
    xhat = x * rstd
    return xhat * g * (1.0 + scale) + shift, xhat, rstd


def _norm_bwd(dh, xhat, rstd, g, scale):
    dshift = jnp.sum(dh, axis=0, keepdims=True)
    dscale = jnp.sum(dh * xhat * g, axis=0, keepdims=True)
    dg = jnp.sum(dh * xhat * (1.0 + scale), axis=0, keepdims=True)
    dxh = dh * (g * (1.0 + scale))
    dx = rstd * (dxh - xhat * jnp.mean(dxh * xhat, axis=-1, keepdims=True))
    return dx, dshift, dscale, dg


def _allgather_small(xp):
    m_per, n = xp.shape

    def body(x_ref, out_ref, send_sems, recv_sems, local_sem):
        x, y, c = lax.axis_index("x"), lax.axis_index("y"), lax.axis_index("c")
        me, sibling = (x, y, c), (x, y, 1 - c)
        chips = [(1 - x, y), (x, 1 - y), (1 - x, 1 - y)]

        def rows(px, py, pc):
            return out_ref.at[pl.ds((4 * px + 2 * py + pc) * m_per, m_per), :]

        def copy(k, block, to, src=None):
            return pltpu.make_async_remote_copy(
                src_ref=rows(*block) if src is None else src, dst_ref=rows(*block),
                send_sem=send_sems.at[k], recv_sem=recv_sems.at[k], device_id=to, device_id_type=MESH)

        mine = pltpu.make_async_copy(x_ref, rows(*me), local_sem)
        mine.start()
        first = [copy(0, me, sibling, src=x_ref)]
        first += [copy(1 + j, me, (*chip, c), src=x_ref) for j, chip in enumerate(chips)]
        for cp in first:
            cp.start()
        passed = [copy(4 + j, (*chip, c), sibling) for j, chip in enumerate(chips)]
        for j, chip in enumerate(chips):
            copy(1 + j, (*chip, c), me).wait_recv()
            passed[j].start()
        copy(0, sibling, me).wait_recv()
        for j, chip in enumerate(chips):
            copy(4 + j, (*chip, 1 - c), me).wait_recv()
        for cp in first + passed:
            cp.wait_send()
        mine.wait()

    return pl.pallas_call(
        body, name="allgather_small",
        out_shape=SDS((8 * m_per, n), xp.dtype),
        in_specs=[pl.BlockSpec(memory_space=pltpu.VMEM)],
        out_specs=pl.BlockSpec(memory_space=pltpu.VMEM),
        scratch_shapes=[pltpu.SemaphoreType.DMA((7,)), pltpu.SemaphoreType.DMA((7,)), pltpu.SemaphoreType.DMA],
        compiler_params=pltpu.CompilerParams(vmem_limit_bytes=VMEM_LIMIT),
    )(xp)


WCLASSES = (
    ("wg0", "lead", (D, FB)), ("wu0", "lead", (D, FB)), ("wd0", "row", (FB, D)),
    ("wg1", "lead", (D, FB)), ("wu1", "lead", (D, FB)), ("wd1", "row", (FB, D)),
    ("win", "col", (D, WIN_SH)), ("wco", "row", (D // N_CHIPS, D)), ("wao", "col", (512, D // N_CHIPS)),
    ("wo", "row", (D // N_CHIPS, D)),
)
NCLS = len(WCLASSES)


def _full_shape(kind, shp):
    if kind == "lead":
        return (N_CHIPS,) + shp
    if kind == "row":
        return (N_CHIPS * shp[0], shp[1])
    return (shp[0], N_CHIPS * shp[1])


def _shard_view(ref, kind, shp, j):
    if kind == "lead":
        return ref.at[j]
    if kind == "row":
        return ref.at[pl.ds(j * shp[0], shp[0]), :]
    return ref.at[:, pl.ds(j * shp[1], shp[1])]


def _half(ref, shp, h):
    hr = shp[0] // 2
    return ref.at[pl.ds(pl.multiple_of(h * hr, 16), hr), :]


def _gather_weights(shards):
    n = NCLS

    def body(*refs):
        ins, outs = refs[:n], refs[n:2 * n]
        send1, recv1, send2, recv2, lsem = refs[2 * n:]
        x, y, c = lax.axis_index("x"), lax.axis_index("y"), lax.axis_index("c")
        chip = 2 * x + y
        sibling = (x, y, 1 - c)

        for mc in range(N_CHIPS):
            @pl.when(chip == mc)
            def _(mc=mc):
                local = []
                for q, (_, kind, shp) in enumerate(WCLASSES):
                    cp = pltpu.make_async_copy(ins[q], _shard_view(outs[q], kind, shp, mc), lsem.at[q])
                    cp.start()
                    local.append(cp)
                sends = []
                for k in (1, 2, 3):
                    pj = mc ^ k
                    for q, (_, kind, shp) in enumerate(WCLASSES):
                        cp = pltpu.make_async_remote_copy(
                            src_ref=_half(ins[q], shp, c), dst_ref=_half(_shard_view(outs[q], kind, shp, mc), shp, c),
                            send_sem=send1.at[q * 3 + k - 1], recv_sem=recv1.at[q * 3 + k - 1],
                            device_id=(pj >> 1, pj & 1, c), device_id_type=MESH)
                        cp.start()
                        sends.append(cp)
                for k in (1, 2, 3):
                    pj = mc ^ k
                    for q, (_, kind, shp) in enumerate(WCLASSES):
                        landed = _half(_shard_view(outs[q], kind, shp, pj), shp, c)
                        pltpu.make_async_remote_copy(
                            src_ref=landed, dst_ref=landed, send_sem=send1.at[q * 3 + k - 1], recv_sem=recv1.at[q * 3 + k - 1],
                            device_id=(pj >> 1, pj & 1, c), device_id_type=MESH).wait_recv()
                        cp = pltpu.make_async_remote_copy(
                            src_ref=landed, dst_ref=landed, send_sem=send2.at[q * 3 + k - 1], recv_sem=recv2.at[q * 3 + k - 1],
                            device_id=sibling, device_id_type=MESH)
                        cp.start()
                        sends.append(cp)
                for k in (1, 2, 3):
                    pj = mc ^ k
                    for q, (_, kind, shp) in enumerate(WCLASSES):
                        other = _half(_shard_view(outs[q], kind, shp, pj), shp, 1 - c)
                        pltpu.make_async_remote_copy(
                            src_ref=other, dst_ref=other, send_sem=send2.at[q * 3 + k - 1], recv_sem=recv2.at[q * 3 + k - 1],
                            device_id=sibling, device_id_type=MESH).wait_recv()
                for cp in sends:
                    cp.wait_send()
                for cp in local:
                    cp.wait()

    anyspec = pl.BlockSpec(memory_space=pl.ANY)
    return pl.pallas_call(
        body, name="gather_weights",
        out_shape=[SDS(_full_shape(kind, shp), bf16) for _, kind, shp in WCLASSES],
        in_specs=[anyspec] * n, out_specs=[anyspec] * n,
        scratch_shapes=[pltpu.SemaphoreType.DMA((3 * n,)), pltpu.SemaphoreType.DMA((3 * n,)),
                        pltpu.SemaphoreType.DMA((3 * n,)), pltpu.SemaphoreType.DMA((3 * n,)),
                        pltpu.SemaphoreType.DMA((n,))],
    )(*shards)


def _scatter_grads(pieces):
    n = NCLS

    def body(*refs):
        ins, outs = refs[:n], refs[n:2 * n]
        send1, recv1, lsem = refs[2 * n:]
        x, y, c = lax.axis_index("x"), lax.axis_index("y"), lax.axis_index("c")
        chip = 2 * x + y

        for mc in range(N_CHIPS):
            @pl.when(chip == mc)
            def _(mc=mc):
                local, sends = [], []
                for q, (_, kind, shp) in enumerate(WCLASSES):
                    cp = pltpu.make_async_copy(_shard_view(ins[q], kind, shp, mc), outs[q].at[0], lsem.at[q])
                    cp.start()
                    local.append(cp)
                for k in (1, 2, 3):
                    pj = mc ^ k
                    for q, (_, kind, shp) in enumerate(WCLASSES):
                        cp = pltpu.make_async_remote_copy(
                            src_ref=_shard_view(ins[q], kind, shp, pj), dst_ref=outs[q].at[k],
                            send_sem=send1.at[q * 3 + k - 1], recv_sem=recv1.at[q * 3 + k - 1],
                            device_id=(pj >> 1, pj & 1, c), device_id_type=MESH)
                        cp.start()
                        sends.append(cp)
                for cp in sends:
                    cp.wait_recv()
                for cp in sends:
                    cp.wait_send()
                for cp in local:
                    cp.wait()

    anyspec = pl.BlockSpec(memory_space=pl.ANY)
    return pl.pallas_call(
        body, name="scatter_grads",
        out_shape=[SDS((N_CHIPS,) + shp, bf16) for _, _, shp in WCLASSES],
        in_specs=[anyspec] * n, out_specs=[anyspec] * n,
        scratch_shapes=[pltpu.SemaphoreType.DMA((3 * n,)), pltpu.SemaphoreType.DMA((3 * n,)), pltpu.SemaphoreType.DMA((n,))],
    )(*pieces)


def _swap_sibling(ts):
    n = len(ts)

    def body(*refs):
        ins, outs = refs[:n], refs[n:2 * n]
        send, recv = refs[2 * n:]
        x, y, c = lax.axis_index("x"), lax.axis_index("y"), lax.axis_index("c")
        cps = []
        for q in range(n):
            cp = pltpu.make_async_remote_copy(src_ref=ins[q], dst_ref=outs[q], send_sem=send.at[q], recv_sem=recv.at[q],
                                              device_id=(x, y, 1 - c), device_id_type=MESH)
            cp.start()
            cps.append(cp)
        for cp in cps:
            cp.wait_recv()
        for cp in cps:
            cp.wait_send()

    anyspec = pl.BlockSpec(memory_space=pl.ANY)
    return pl.pallas_call(
        body, name="swap_sibling",
        out_shape=[SDS(t.shape, t.dtype) for t in ts],
        in_specs=[anyspec] * n, out_specs=[anyspec] * n,
        scratch_shapes=[pltpu.SemaphoreType.DMA((n,)), pltpu.SemaphoreType.DMA((n,))],
    )(*ts)


HBM_SPEC = pl.BlockSpec(memory_space=pltpu.HBM)
SEM_SPEC = pl.BlockSpec(memory_space=pltpu.SEMAPHORE)
ANY_SPEC = pl.BlockSpec(memory_space=pl.ANY)
EFFECT = pltpu.SideEffectType.DATAFLOW_SIDE_EFFECTING
N_COPIES = 3 * NCLS


def _in_hbm(a):
    return pltpu.with_memory_space_constraint(a, pltpu.HBM)


def _chip_index():
    return 2 * lax.axis_index("x") + lax.axis_index("y")


def _place_own(shards):
    n = NCLS

    def body(*refs):
        ins, outs, lsem = refs[:n], refs[n:2 * n], refs[2 * n]
        chip = _chip_index()
        for mc in range(N_CHIPS):
            @pl.when(chip == mc)
            def _(mc=mc):
                cps = [pltpu.make_async_copy(ins[q], _shard_view(outs[q], kind, shp, mc), lsem.at[q])
                       for q, (_, kind, shp) in enumerate(WCLASSES)]
                for cp in cps:
                    cp.start()
                for cp in cps:
                    cp.wait()

    return pl.pallas_call(
        body, name="place_own",
        out_shape=[SDS(_full_shape(kind, shp), bf16) for _, kind, shp in WCLASSES],
        in_specs=[ANY_SPEC] * n, out_specs=[ANY_SPEC] * n,
        scratch_shapes=[pltpu.SemaphoreType.DMA((n,))],
    )(*shards)


def _take_own(pieces):
    n = NCLS

    def body(*refs):
        ins, outs, lsem = refs[:n], refs[n:2 * n], refs[2 * n]
        chip = _chip_index()
        for mc in range(N_CHIPS):
            @pl.when(chip == mc)
            def _(mc=mc):
                cps = [pltpu.make_async_copy(_shard_view(ins[q], kind, shp, mc), outs[q].at[0], lsem.at[q])
                       for q, (_, kind, shp) in enumerate(WCLASSES)]
                for cp in cps:
                    cp.start()
                for cp in cps:
                    cp.wait()

    return pl.pallas_call(
        body, name="take_own",
        out_shape=[SDS((N_CHIPS,) + shp, bf16) for _, _, shp in WCLASSES],
        in_specs=[ANY_SPEC] * n, out_specs=[ANY_SPEC] * n,
        scratch_shapes=[pltpu.SemaphoreType.DMA((n,))],
    )(*pieces)


def _split_start(name, srcs, dsts, after, src_view, dst_view):
    n = NCLS

    def body(*refs):
        src, dst = refs[:n], refs[n:2 * n]
        send, recv = refs[2 * n + 1], refs[2 * n + 2]
        token = refs[-1]
        c = lax.axis_index("c")
        chip = _chip_index()
        for mc in range(N_CHIPS):
            @pl.when(chip == mc)
            def _(mc=mc):
                for k in (1, 2, 3):
                    pj = mc ^ k
                    for q in range(n):
                        pltpu.make_async_remote_copy(
                            src_ref=src_view(src[q], q, mc, pj), dst_ref=dst_view(dst[q], q, mc, k),
                            send_sem=send.at[q * 3 + k - 1], recv_sem=recv.at[q * 3 + k - 1],
                            device_id=(pj >> 1, pj & 1, c), device_id_type=MESH).start()
        token[...] = jnp.zeros_like(token)

    return pl.pallas_call(
        body, name=name,
        out_shape=(pltpu.SemaphoreType.DMA((N_COPIES,)), pltpu.SemaphoreType.DMA((N_COPIES,)),
                   *[pltpu.HBM(a.shape, a.dtype) for a in srcs], *[pltpu.HBM(a.shape, a.dtype) for a in dsts], SDS((8, 128), f32)),
        in_specs=[HBM_SPEC] * (2 * n) + [ANY_SPEC],
        out_specs=(SEM_SPEC, SEM_SPEC, *([HBM_SPEC] * (2 * n)), pl.BlockSpec(memory_space=pltpu.VMEM)),
        input_output_aliases={i: 2 + i for i in range(2 * n)},
        compiler_params=pltpu.CompilerParams(has_side_effects=EFFECT),
    )(*[_in_hbm(a) for a in srcs], *[_in_hbm(a) for a in dsts], after)


def _split_wait(name, started, after, arrival_view):
    n = NCLS
    send, recv = started[0], started[1]
    srcs, dsts = started[2:2 + n], started[2 + n:2 + 2 * n]

    def body(*refs):
        src, dst = refs[:n], refs[n:2 * n]
        send_sem, recv_sem = refs[2 * n], refs[2 * n + 1]
        x, y, c = lax.axis_index("x"), lax.axis_index("y"), lax.axis_index("c")
        for k in (1, 2, 3):
            for q in range(n):
                arrival = arrival_view(dst[q], q, k)
                cp = pltpu.make_async_remote_copy(
                    src_ref=arrival, dst_ref=arrival, send_sem=send_sem.at[q * 3 + k - 1], recv_sem=recv_sem.at[q * 3 + k - 1],
                    device_id=(x, y, 1 - c), device_id_type=MESH)
                cp.wait_send()
                cp.wait_recv()

    out = pl.pallas_call(
        body, name=name,
        out_shape=(*[pltpu.HBM(a.shape, a.dtype) for a in srcs], *[pltpu.HBM(a.shape, a.dtype) for a in dsts]),
        in_specs=[HBM_SPEC] * (2 * n) + [SEM_SPEC, SEM_SPEC, ANY_SPEC],
        out_specs=tuple([HBM_SPEC] * (2 * n)),
        input_output_aliases={i: i for i in range(2 * n)},
        compiler_params=pltpu.CompilerParams(has_side_effects=EFFECT),
    )(*srcs, *dsts, send, recv, after)
    return out[n:]


def _cls(q):
    return WCLASSES[q][1], WCLASSES[q][2]


def _gather_start(shards, after):
    fulls = _place_own(shards)
    return _split_start("gather_start", shards, fulls, after,
                        lambda ref, q, mc, pj: ref,
                        lambda ref, q, mc, k: _shard_view(ref, *_cls(q), mc))


def _gather_wait(started, after):
    return _split_wait("gather_wait", started, after, lambda ref, q, k: _shard_view(ref, *_cls(q), 0))


def _scatter_start(pieces, after):
    lands = _take_own(pieces)
    return _split_start("scatter_start", pieces, lands, after,
                        lambda ref, q, mc, pj: _shard_view(ref, *_cls(q), pj),
                        lambda ref, q, mc, k: ref.at[k])


def _scatter_wait(started, after):
    return _split_wait("scatter_wait", started, after, lambda ref, q, k: ref.at[k])


GROUPS = {"A": (0, 1, 2), "B": (6,), "C": (7, 8, 9), "D": (3, 4, 5)}


def _own_spec(kind, shp, tr):
    R, C = shp
    if kind == "lead":
        return pl.BlockSpec((None, tr, C), lambda i, chip: (chip[0], i, 0))
    if kind == "row":
        return pl.BlockSpec((tr, C), lambda i, chip: (chip[0] * (R // tr) + i, 0))
    return pl.BlockSpec((tr, C), lambda i, chip: (i, chip[0]))


def _cast_place(shards, kind, shp, chip_arr, after):
    n = len(shards)
    R, C = shp
    tr = _row_tile(R, C)

    def body(chip_ref, *refs):
        for q in range(n):
            refs[n + 1 + q][...] = refs[q][...].astype(bf16)

    def in_spec(lead):
        return pl.BlockSpec((None,) * len(lead) + (tr, C), lambda i, chip: (*lead, i, 0))

    return pl.pallas_call(
        body, name="cast_place",
        grid_spec=pltpu.PrefetchScalarGridSpec(
            num_scalar_prefetch=1, grid=(R // tr,),
            in_specs=[in_spec(lead) for _, lead in shards] + [ANY_SPEC],
            out_specs=[_own_spec(kind, shp, tr)] * n),
        out_shape=[SDS(_full_shape(kind, shp), bf16)] * n,
        compiler_params=_cp("arbitrary"),
    )(chip_arr, *[a for a, _ in shards], after)


def _sum_own_slots(piece, land, kind, shp, chip_arr):
    R, C = shp
    tr = _row_tile(R, C)

    def body(chip_ref, p_ref, l_ref, t_ref):
        t = p_ref[...].astype(f32)
        for k in range(N_CHIPS - 1):
            t = t + l_ref[k].astype(f32)
        t_ref[...] = t

    return pl.pallas_call(
        body, name="sum_own_slots",
        grid_spec=pltpu.PrefetchScalarGridSpec(
            num_scalar_prefetch=1, grid=(R // tr,),
            in_specs=[_own_spec(kind, shp, tr), pl.BlockSpec((N_CHIPS - 1, tr, C), lambda i, chip: (0, i, 0))],
            out_specs=pl.BlockSpec((tr, C), lambda i, chip: (i, 0))),
        out_shape=SDS((R, C), f32),
        compiler_params=_cp("arbitrary"),
    )(chip_arr, piece, land)


def _xfer_start(name, arrays, ng, after, src_view, dst_view):
    na = len(arrays)

    def body(*refs):
        arr = refs[:na]
        send, recv, token = refs[na + 1], refs[na + 2], refs[-1]
        c = lax.axis_index("c")
        chip = _chip_index()
        for mc in range(N_CHIPS):
            @pl.when(chip == mc)
            def _(mc=mc):
                for k in (1, 2, 3):
                    pj = mc ^ k
                    for i in range(ng):
                        pltpu.make_async_remote_copy(
                            src_ref=src_view(arr, i, mc, pj), dst_ref=dst_view(arr, i, mc, k),
                            send_sem=send.at[i * 3 + k - 1], recv_sem=recv.at[i * 3 + k - 1],
                            device_id=(pj >> 1, pj & 1, c), device_id_type=MESH).start()
        token[...] = jnp.zeros_like(token)

    return pl.pallas_call(
        body, name=name,
        out_shape=(pltpu.SemaphoreType.DMA((3 * ng,)), pltpu.SemaphoreType.DMA((3 * ng,)),
                   *[pltpu.HBM(a.shape, a.dtype) for a in arrays], SDS((8, 128), f32)),
        in_specs=[HBM_SPEC] * na + [ANY_SPEC],
        out_specs=(SEM_SPEC, SEM_SPEC, *([HBM_SPEC] * na), pl.BlockSpec(memory_space=pltpu.VMEM)),
        input_output_aliases={i: 2 + i for i in range(na)},
        compiler_params=pltpu.CompilerParams(has_side_effects=EFFECT),
    )(*[_in_hbm(a) for a in arrays], after)


def _xfer_wait(name, started, ng, after, arrival_view):
    send, recv = started[0], started[1]
    arrays = started[2:-1]
    na = len(arrays)

    def body(*refs):
        arr = refs[:na]
        send_sem, recv_sem = refs[na], refs[na + 1]
        x, y, c = lax.axis_index("x"), lax.axis_index("y"), lax.axis_index("c")
        for k in (1, 2, 3):
            for i in range(ng):
                arrival = arrival_view(arr, i)
                cp = pltpu.make_async_remote_copy(
                    src_ref=arrival, dst_ref=arrival, send_sem=send_sem.at[i * 3 + k - 1], recv_sem=recv_sem.at[i * 3 + k - 1],
                    device_id=(x, y, 1 - c), device_id_type=MESH)
                cp.wait_send()
                cp.wait_recv()

    return pl.pallas_call(
        body, name=name,
        out_shape=tuple(pltpu.HBM(a.shape, a.dtype) for a in arrays),
        in_specs=[HBM_SPEC] * na + [SEM_SPEC, SEM_SPEC, ANY_SPEC],
        out_specs=tuple([HBM_SPEC] * na),
        input_output_aliases={i: i for i in range(na)},
        compiler_params=pltpu.CompilerParams(has_side_effects=EFFECT),
    )(*arrays, send, recv, after)


def _gather_group_cast(group, shards_f32, chip_arr, after):
    fulls = [None] * len(group)
    by_shape = {}
    for i, q in enumerate(group):
        by_shape.setdefault(_cls(q), []).append(i)
    for (kind, shp), idx in by_shape.items():
        for i, f in zip(idx, _cast_place([shards_f32[i] for i in idx], kind, shp, chip_arr, after)):
            fulls[i] = f
    return fulls


def _gather_group_start(tag, group, fulls, after):
    view = lambda arr, i, mc, _: _shard_view(arr[i], *_cls(group[i]), mc)
    return _xfer_start("gather_start_" + tag, fulls, len(group), after, view, view)


def _gather_group_wait(tag, group, started, after):
    return _xfer_wait("gather_wait_" + tag, started, len(group), after, lambda arr, i: _shard_view(arr[i], *_cls(group[i]), 0))


def _scatter_group_start(tag, group, pieces, after):
    ng = len(group)
    lands = [lax.empty((N_CHIPS - 1,) + _cls(q)[1], bf16) for q in group]
    return _xfer_start("scatter_start_" + tag, list(pieces) + lands, ng, after,
                       lambda arr, i, mc, pj: _shard_view(arr[i], *_cls(group[i]), pj),
                       lambda arr, i, mc, k: arr[ng + i].at[k - 1])


def _scatter_group_wait(tag, group, started, after):
    ng = len(group)
    out = _xfer_wait("scatter_wait_" + tag, started, ng, after, lambda arr, i: arr[ng + i].at[0])
    return out[:ng], out[ng:]


def _mod_shards(c_all, ada_w, ada_b_sh):
    tn = ADA_SH // 3

    def body(c_ref, w_ref, b_ref, o_ref, cs_ref):
        cv = c_ref[...]
        cs = cv * _sigmoid(cv)
        cs_ref[...] = cs
        o_ref[...] = _dot(cs.astype(bf16), w_ref[...].astype(bf16)) + b_ref[...]

    return pl.pallas_call(
        body, name="mod_shards", grid=(DEPTH, 3),
        out_shape=[SDS((DEPTH, 8, ADA_SH), f32), SDS((8, D), f32)],
        in_specs=[pl.BlockSpec((8, D), lambda l, t: (0, 0)),
                  pl.BlockSpec((None, D, tn), lambda l, t: (l, 0, t)),
                  pl.BlockSpec((None, 1, tn), lambda l, t: (l, 0, t))],
        out_specs=[pl.BlockSpec((None, 8, tn), lambda l, t: (l, 0, t)), pl.BlockSpec((8, D), lambda l, t: (0, 0))],
        compiler_params=_cp("arbitrary", "arbitrary"),
    )(c_all, ada_w, ada_b_sh.reshape(DEPTH, 1, ADA_SH))


def _t5_bucket(dist):
    exact = NUM_BUCKETS // 2
    dd = np.maximum(dist, 1).astype(np.float32)
    large = exact + (np.log(dd / exact) / np.log(MAX_DISTANCE / exact) * (NUM_BUCKETS - exact)).astype(np.int32)
    large = np.minimum(large, NUM_BUCKETS - 1)
    return np.where(dist < exact, dist, large).astype(np.int32)


def _bucket_table():
    i = np.arange(BLK)[:, None]
    j = np.arange(2 * BLK)[None, :]
    rel = i - j + BLK
    return np.stack([_t5_bucket(np.maximum(rel, 0) * d) for d in DILATIONS]).astype(np.int32)


def _band():
    rel = lax.broadcasted_iota(jnp.int32, (BLK, 2 * BLK), 0) - lax.broadcasted_iota(jnp.int32, (BLK, 2 * BLK), 1) + BLK
    return (rel >= 0) & (rel <= BLK)


def _bias_blocks(rel_bias, buckets):
    def body(tab_ref, bk_ref, o_ref):
        h = pl.program_id(0)
        bk = bk_ref[...]
        acc = jnp.zeros((BLK, 2 * BLK), f32)
        for b in range(NUM_BUCKETS):
            acc = jnp.where(bk == b, tab_ref[b, h], acc)
        o_ref[...] = jnp.where(_band(), acc, NEG)

    return pl.pallas_call(
        body, name="bias_blocks", grid=(24,),
        out_shape=SDS((24, BLK, 2 * BLK), f32),
        in_specs=[pl.BlockSpec(memory_space=pltpu.SMEM), pl.BlockSpec((None, BLK, 2 * BLK), lambda h: (h // 8, 0, 0))],
        out_specs=pl.BlockSpec((None, BLK, 2 * BLK), lambda h: (h, 0, 0)),
        compiler_params=_cp("arbitrary"),
    )(rel_bias, buckets)


def _bias_grad(dsaccs, buckets):
    nl = len(dsaccs)

    def body(*refs):
        bk = refs[nl][...]
        tot = refs[0][...]
        for r in refs[1:nl]:
            tot = tot + r[...]
        lane = lax.broadcasted_iota(jnp.int32, (1, 128), 1)
        row = jnp.zeros((1, 128), f32)
        for b in range(NUM_BUCKETS):
            row = jnp.where(lane == b, jnp.sum(jnp.where(bk == b, tot, 0.0)), row)
        refs[nl + 1][...] = row

    return pl.pallas_call(
        body, name="bias_grad", grid=(24,),
        out_shape=SDS((24, 1, 128), f32),
        in_specs=[pl.BlockSpec((None, BLK, 2 * BLK), lambda h: (h, 0, 0))] * nl
                 + [pl.BlockSpec((None, BLK, 2 * BLK), lambda h: (h // 8, 0, 0))],
        out_specs=pl.BlockSpec((None, 1, 128), lambda h: (h, 0, 0)),
        compiler_params=_cp("arbitrary"),
    )(*dsaccs, buckets)


def _ffn_fwd(x, mod9, g3, wg, wu, wd, sub):
    S = x.shape[0]

    def body(x_ref, mod_ref, g_ref, wg_ref, wu_ref, wd_ref, xo_ref, h_ref, ga_ref, sa_ref, hid_ref, y_ref, acc):
        j = pl.program_id(1)

        @pl.when(j == 0)
        def _():
            h, _, _ = _norm_fwd(x_ref[...], g_ref[sub:sub + 1, :], mod_ref[3 * sub:3 * sub + 1, :], mod_ref[3 * sub + 1:3 * sub + 2, :])
            h_ref[...] = h.astype(bf16)
            acc[...] = jnp.zeros_like(acc)

        h = h_ref[...]
        a = _dot(h, wg_ref[...])
        u = _dot(h, wu_ref[...])
        sg = _sigmoid(a)
        sil = a * sg
        ga_ref[...] = (u * (sg * (1.0 + a * (1.0 - sg)))).astype(bf16)
        sa_ref[...] = sil.astype(bf16)
        hid_ref[...] = (sil * u).astype(bf16)
        acc[...] += _dot(hid_ref[...], wd_ref[...])

        @pl.when(j == N_CHIPS - 1)
        def _():
            y = acc[...]
            y_ref[...] = y.astype(bf16)
            xo_ref[...] = x_ref[...] + 0.5 * mod_ref[3 * sub + 2:3 * sub + 3, :] * y

    row = pl.BlockSpec((TMF, D), lambda i, j: (i, 0))
    hidb = pl.BlockSpec((None, TMF, FB), lambda i, j: (j, i, 0))
    hids = SDS((N_CHIPS, S, FB), bf16)
    return pl.pallas_call(
        body, name="ffn_fwd", grid=(S // TMF, N_CHIPS),
        out_shape=[SDS((S, D), f32), SDS((S, D), bf16), hids, hids, hids, SDS((S, D), bf16)],
        in_specs=[row, pl.BlockSpec((9, D), lambda i, j: (0, 0)), pl.BlockSpec((3, D), lambda i, j: (0, 0)),
                  pl.BlockSpec((None, D, FB), lambda i, j: (j, 0, 0)), pl.BlockSpec((None, D, FB), lambda i, j: (j, 0, 0)),
                  pl.BlockSpec((FB, D), lambda i, j: (j, 0))],
        out_specs=[row, row, hidb, hidb, hidb, row],
        scratch_shapes=[pltpu.VMEM((TMF, D), f32)],
        compiler_params=_cp("arbitrary", "arbitrary"),
    )(x, mod9, g3, wg, wu, wd)


def _ffn_bwd1(dxo, x, mod9, g3, y, ga, sa, wg, wu, wd, sub):
    S = x.shape[0]

    def body(dxo_ref, x_ref, mod_ref, g_ref, y_ref, ga_ref, sa_ref, wg_ref, wu_ref, wd_ref,
             dxi_ref, da_ref, du_ref, dy_ref, sm_ref, acc):
        i, j = pl.program_id(0), pl.program_id(1)
        gate = mod_ref[3 * sub + 2:3 * sub + 3, :]

        @pl.when((i == 0) & (j == 0))
        def _():
            sm_ref[...] = jnp.zeros_like(sm_ref)

        @pl.when(j == 0)
        def _():
            dxo_v = dxo_ref[...]
            dy_ref[...] = (0.5 * gate * dxo_v).astype(bf16)
            sm_ref[2:3, :] += jnp.sum(0.5 * y_ref[...].astype(f32) * dxo_v, axis=0, keepdims=True)
            acc[...] = jnp.zeros_like(acc)

        part = None
        for s in range(SH_STEP):
            dhid = _dot_nt(dy_ref[...], wd_ref[s * FB:(s + 1) * FB, :])
            da = (dhid * ga_ref[s].astype(f32)).astype(bf16)
            du = (dhid * sa_ref[s].astype(f32)).astype(bf16)
            da_ref[s] = da
            du_ref[s] = du
            t = _dot_nt(da, wg_ref[s]) + _dot_nt(du, wu_ref[s])
            part = t if part is None else part + t
        acc[...] += part

        @pl.when(j == N_CHIPS // SH_STEP - 1)
        def _():
            g = g_ref[sub:sub + 1, :]
            scale = mod_ref[3 * sub + 1:3 * sub + 2, :]
            _, xhat, rstd = _norm_fwd(x_ref[...], g, mod_ref[3 * sub:3 * sub + 1, :], scale)
            dx, dshift, dscale, dg = _norm_bwd(acc[...], xhat, rstd, g, scale)
            dxi_ref[...] = dxo_ref[...] + dx
            sm_ref[0:1, :] += dshift
            sm_ref[1:2, :] += dscale
            sm_ref[3:4, :] += dg

    row = pl.BlockSpec((TM, D), lambda i, j: (i, 0))
    hidb = pl.BlockSpec((SH_STEP, TM, FB), lambda i, j: (j, i, 0))
    wcol = pl.BlockSpec((SH_STEP, D, FB), lambda i, j: (j, 0, 0))
    return pl.pallas_call(
        body, name="ffn_bwd1", grid=(S // TM, N_CHIPS // SH_STEP),
        out_shape=[SDS((S, D), f32), SDS((N_CHIPS, S, FB), bf16), SDS((N_CHIPS, S, FB), bf16), SDS((S, D), bf16), SDS((8, D), f32)],
        in_specs=[row, row, pl.BlockSpec((9, D), lambda i, j: (0, 0)), pl.BlockSpec((3, D), lambda i, j: (0, 0)), row,
                  hidb, hidb, wcol, wcol, pl.BlockSpec((SH_STEP * FB, D), lambda i, j: (j, 0))],
        out_specs=[row, hidb, hidb, row, pl.BlockSpec((8, D), lambda i, j: (0, 0))],
        scratch_shapes=[pltpu.VMEM((TM, D), f32)],
        compiler_params=_cp("arbitrary", "arbitrary"),
    )(dxo, x, mod9, g3, y, ga, sa, wg, wu, wd)


def _ffn_bwd2(h, da, du, hid, dy):
    S = h.shape[0]
    ni = S // TMW

    def body(h_ref, da_ref, du_ref, hid_ref, dy_ref, dwg_ref, dwu_ref, dwd_ref, ag, au, ad):
        i = pl.program_id(1)

        @pl.when(i == 0)
        def _():
            ag[...] = jnp.zeros_like(ag)
            au[...] = jnp.zeros_like(au)
            ad[...] = jnp.zeros_like(ad)

        hv = h_ref[...]
        ag[...] += _dot_tn(hv, da_ref[...])
        au[...] += _dot_tn(hv, du_ref[...])
        ad[...] += _dot_tn(hid_ref[...], dy_ref[...])

        @pl.when(i == ni - 1)
        def _():
            dwg_ref[...] = ag[...].astype(bf16)
            dwu_ref[...] = au[...].astype(bf16)
            dwd_ref[...] = ad[...].astype(bf16)

    row = pl.BlockSpec((TMW, D), lambda j, i: (i, 0))
    hidb = pl.BlockSpec((None, TMW, FB), lambda j, i: (j, i, 0))
    wcol = pl.BlockSpec((None, D, FB), lambda j, i: (j, 0, 0))
    return pl.pallas_call(
        body, name="ffn_bwd2", grid=(N_CHIPS, ni),
        out_shape=[SDS((N_CHIPS, D, FB), bf16), SDS((N_CHIPS, D, FB), bf16), SDS((N_CHIPS * FB, D), bf16)],
        in_specs=[row, hidb, hidb, hidb, row],
        out_specs=[wcol, wcol, pl.BlockSpec((FB, D), lambda j, i: (j, 0))],
        scratch_shapes=[pltpu.VMEM((D, FB), f32), pltpu.VMEM((D, FB), f32), pltpu.VMEM((FB, D), f32)],
        compiler_params=_cp("arbitrary", "arbitrary"),
    )(h, da, du, hid, dy)


def _mix_qkv(x, mod9, g3, win):
    S = x.shape[0]

    def body(x_ref, mod_ref, g_ref, w_ref, h_ref, o_ref):
        @pl.when(pl.program_id(1) == 0)
        def _():
            h, _, _ = _norm_fwd(x_ref[...], g_ref[1:2, :], mod_ref[3:4, :], mod_ref[4:5, :])
            h_ref[...] = h.astype(bf16)

        o_ref[...] = _dot(h_ref[...], w_ref[...]).astype(bf16)

    row = pl.BlockSpec((TMP, D), lambda i, j: (i, 0))
    return pl.pallas_call(
        body, name="mix_qkv", grid=(S // TMP, QKV_W // CB),
        out_shape=[SDS((S, D), bf16), SDS((S, QKV_W), bf16)],
        in_specs=[row, pl.BlockSpec((9, D), lambda i, j: (0, 0)), pl.BlockSpec((3, D), lambda i, j: (0, 0)),
                  pl.BlockSpec((D, CB), lambda i, j: (0, j))],
        out_specs=[row, pl.BlockSpec((TMP, CB), lambda i, j: (i, j))],
        compiler_params=_cp("arbitrary", "arbitrary"),
    )(x, mod9, g3, win)


def _mix_rest(h, win):
    S = h.shape[0]
    off = QKV_W // CB

    def body(h_ref, w_ref, o_ref):
        o_ref[...] = _dot(h_ref[...], w_ref[...]).astype(bf16)

    return pl.pallas_call(
        body, name="mix_rest", grid=(S // TMP, REST_W // CB),
        out_shape=SDS((S, REST_W), bf16),
        in_specs=[pl.BlockSpec((TMP, D), lambda i, j: (i, 0)), pl.BlockSpec((D, CB), lambda i, j: (0, off + j))],
        out_specs=pl.BlockSpec((TMP, CB), lambda i, j: (i, j)),
        compiler_params=_cp("arbitrary", "arbitrary"),
    )(h, win)


def _widen(srcs, dsts):
    for src, dst in zip(srcs, dsts):
        dst[...] = src[...].astype(f32)


def _qkv_scratch(R, Rb):
    return [pltpu.VMEM((R, 128), f32), pltpu.VMEM((R, 128), f32), pltpu.VMEM((Rb, 128), f32),
            pltpu.VMEM((R, 128), f32), pltpu.VMEM((Rb, 128), f32)]


def _attn_fwd(qkv, bias, g):
    S = qkv.shape[0]
    d = DILATIONS[g]
    nq = Q_BLOCKS[g]
    Rb = BLK * d
    R = Rb * nq
    nb = S // R
    qb, kb, vb = 4 * g, 12 + 4 * g, 24 + 4 * g

    def body(q_in, kc_in, kp_in, vc_in, vp_in, b_ref, o_ref, l_ref, q_ref, kc_ref, kp_ref, vc_ref, vp_ref):
        n = pl.program_id(1)
        col = lax.broadcasted_iota(jnp.int32, (BLK, 2 * BLK), 1)
        first = jnp.where((col < BLK) & (n == 0), NEG, 0.0)
        head0 = lax.broadcasted_iota(jnp.int32, (1, 2 * HD), 1) < HD
        _widen((q_in, kc_in, kp_in, vc_in, vp_in), (q_ref, kc_ref, kp_ref, vc_ref, vp_ref))

        def one(b, r):
            sl = pl.ds(b * Rb + r, BLK, stride=d)
            q = q_ref[sl, :]
            if b == 0:
                kp, vp = kp_ref[pl.ds(r, BLK, stride=d), :], vp_ref[pl.ds(r, BLK, stride=d), :]
            else:
                before = pl.ds((b - 1) * Rb + r, BLK, stride=d)
                kp, vp = kc_ref[before, :], vc_ref[before, :]
            kk = jnp.concatenate([kp, kc_ref[sl, :]], axis=0).astype(bf16)
            vv = jnp.concatenate([vp, vc_ref[sl, :]], axis=0).astype(bf16)
            os, ls = [], []
            for hh in range(2):
                qm = jnp.where(head0 if hh == 0 else ~head0, q, 0.0).astype(bf16)
                s = _dot_nt(qm, kk) * SCALE + b_ref[hh]
                if b == 0:
                    s = s + first
                m = jnp.max(s, axis=-1, keepdims=True)
                p = jnp.exp(s - m)
                l = jnp.sum(p, axis=-1, keepdims=True)
                os.append(_dot(p.astype(bf16), vv) / l)
                ls.append(m + jnp.log(l))
            o_ref[sl, :] = jnp.where(head0, os[0], os[1])
            l_ref[sl, :] = jnp.where(head0, ls[0], ls[1])

        for b in range(nq):
            if d == 1:
                one(b, 0)
            else:
                lax.fori_loop(0, d, lambda r, carry, b=b: (one(b, r), carry)[1], 0, unroll=4)

    def blk(cb, prev):
        if prev:
            return pl.BlockSpec((Rb, 128), lambda hp, n: (jnp.maximum(n * nq - 1, 0), cb + hp))
        return pl.BlockSpec((R, 128), lambda hp, n: (n, cb + hp))

    outb = pl.BlockSpec((R, 128), lambda hp, n: (n, hp))
    return pl.pallas_call(
        body, name=f"attn_fwd_d{d}", grid=(4, nb),
        out_shape=[SDS((S, 512), f32), SDS((S, 512), f32)],
        in_specs=[blk(qb, False), blk(kb, False), blk(kb, True), blk(vb, False), blk(vb, True),
                  pl.BlockSpec((2, BLK, 2 * BLK), lambda hp, n: (4 * g + hp, 0, 0))],
        out_specs=[outb, outb],
        scratch_shapes=_qkv_scratch(R, Rb),
        compiler_params=_cp("arbitrary", "arbitrary"),
    )(qkv, qkv, qkv, qkv, qkv, bias)


def _attn_bwd(qkv, do, o, lse, bias, dq_all, dk_all, dv_all, g):
    S = qkv.shape[0]
    d = DILATIONS[g]
    nq = Q_BLOCKS[g]
    Rb = BLK * d
    R = Rb * nq
    nb = S // R
    qb, kb, vb = 4 * g, 12 + 4 * g, 24 + 4 * g

    def body(q_in, kc_in, kp_in, vc_in, vp_in, do_ref, o_ref, l_ref, b_ref, dqi, dki, dvi,
             dq_out, dk_out, dv_out, ds_ref, ck, cv, tk, tv, dq_ref, q_ref, kc_ref, kp_ref, vc_ref, vp_ref):
        n = pl.program_id(1)
        col = lax.broadcasted_iota(jnp.int32, (BLK, 2 * BLK), 1)
        first = jnp.where((col < BLK) & (n == 0), NEG, 0.0)

        @pl.when(n == 0)
        def _():
            ck[...] = jnp.zeros_like(ck)
            cv[...] = jnp.zeros_like(cv)
            ds_ref[...] = jnp.zeros_like(ds_ref)

        @pl.when(n < nb)
        def _():
            head0 = lax.broadcasted_iota(jnp.int32, (1, 2 * HD), 1) < HD
            _widen((q_in, kc_in, kp_in, vc_in, vp_in), (q_ref, kc_ref, kp_ref, vc_ref, vp_ref))

            def one(b, r):
                sl = pl.ds(b * Rb + r, BLK, stride=d)
                before = pl.ds((max(b, 1) - 1) * Rb + r, BLK, stride=d)
                q = q_ref[sl, :]
                if b == 0:
                    kp, vp = kp_ref[pl.ds(r, BLK, stride=d), :], vp_ref[pl.ds(r, BLK, stride=d), :]
                else:
                    kp, vp = kc_ref[before, :], vc_ref[before, :]
                kk = jnp.concatenate([kp, kc_ref[sl, :]], axis=0).astype(bf16)
                vv = jnp.concatenate([vp, vc_ref[sl, :]], axis=0).astype(bf16)
                dov, lv = do_ref[sl, :], l_ref[sl, :]
                prod = dov * o_ref[sl, :]
                qb, dob = q.astype(bf16), dov.astype(bf16)
                dqs, dks, dvs = [], [], []
                for hh in range(2):
                    msk = head0 if hh == 0 else ~head0
                    qm = jnp.where(msk, q, 0.0).astype(bf16)
                    dom = jnp.where(msk, dov, 0.0).astype(bf16)
                    dsum = jnp.sum(jnp.where(msk, prod, 0.0), axis=-1, keepdims=True)
                    s = _dot_nt(qm, kk) * SCALE + b_ref[hh]
                    if b == 0:
                        s = s + first
                    p = jnp.exp(s - lv[:, HD * hh:HD * hh + 1])
                    ds = p * (_dot_nt(dom, vv) - dsum)
                    ds_ref[hh] += ds
                    dsb = ds.astype(bf16)
                    dqs.append(_dot(dsb, kk) * SCALE)
                    dks.append(_dot_tn(dsb, qb) * SCALE)
                    dvs.append(_dot_tn(p.astype(bf16), dob))
                dq_ref[sl, :] = jnp.where(head0, dqs[0], dqs[1])
                dk = jnp.where(head0, dks[0], dks[1])
                dv = jnp.where(head0, dvs[0], dvs[1])
                tk[sl, :] = dk[BLK:]
                tv[sl, :] = dv[BLK:]
                if b == 0:
                    prev_rows = pl.ds((nq - 1) * Rb + r, BLK, stride=d)
                    ck[prev_rows, :] += dk[:BLK]
                    cv[prev_rows, :] += dv[:BLK]
                else:
                    tk[before, :] += dk[:BLK]
                    tv[before, :] += dv[:BLK]

            for b in range(nq):
                if d == 1:
                    one(b, 0)
                else:
                    lax.fori_loop(0, d, lambda r, carry, b=b: (one(b, r), carry)[1], 0, unroll=4)
            dq_out[...] = dq_ref[...].astype(bf16)
            dk_out[...] = ck[...].astype(bf16)
            dv_out[...] = cv[...].astype(bf16)
            ck[...] = tk[...]
            cv[...] = tv[...]

        @pl.when(n == nb)
        def _():
            dk_out[...] = ck[...].astype(bf16)
            dv_out[...] = cv[...].astype(bf16)

    last = nb - 1

    def blk(cb, prev):
        if prev:
            return pl.BlockSpec((Rb, 128), lambda hp, n: (jnp.maximum(jnp.minimum(n, last) * nq - 1, 0), cb + hp))
        return pl.BlockSpec((R, 128), lambda hp, n: (jnp.minimum(n, last), cb + hp))

    cur = pl.BlockSpec((R, 128), lambda hp, n: (jnp.minimum(n, last), hp))
    anyspec = pl.BlockSpec(memory_space=pl.ANY)
    dqo = pl.BlockSpec((R, 128), lambda hp, n: (jnp.minimum(n, last), 4 * g + hp))
    dko = pl.BlockSpec((R, 128), lambda hp, n: (jnp.maximum(n - 1, 0), 4 * g + hp))
    return pl.pallas_call(
        body, name=f"attn_bwd_d{d}", grid=(4, nb + 1),
        out_shape=[SDS((S, 1536), bf16), SDS((S, 1536), bf16), SDS((S, 1536), bf16), SDS((8, BLK, 2 * BLK), f32)],
        in_specs=[blk(qb, False), blk(kb, False), blk(kb, True), blk(vb, False), blk(vb, True), cur, cur, cur,
                  pl.BlockSpec((2, BLK, 2 * BLK), lambda hp, n: (4 * g + hp, 0, 0)), anyspec, anyspec, anyspec],
        out_specs=[dqo, dko, dko, pl.BlockSpec((2, BLK, 2 * BLK), lambda hp, n: (hp, 0, 0))],
        scratch_shapes=[pltpu.VMEM((R, 128), f32)] * 5 + _qkv_scratch(R, Rb),
        input_output_aliases={9: 0, 10: 1, 11: 2},
        compiler_params=_cp("arbitrary", "arbitrary"),
    )(qkv, qkv, qkv, qkv, qkv, do, o, lse, bias, dq_all, dk_all, dv_all)


def _conv_z(cc, ch, hc, hh, cw_ref, first):
    halo = jnp.where(first, 0.0, hc.astype(f32) * hh.astype(f32))
    T = jnp.concatenate([halo, cc * ch], axis=0)
    z = cw_ref[2:3, :] * T + cw_ref[1:2, :] * pltpu.roll(T, 1, 0) + cw_ref[0:1, :] * pltpu.roll(T, 2, 0)
    return T, z[HALO:]


def _rest_specs(tm, with_next):
    per = tm // HALO
    specs = [pl.BlockSpec((tm, D), functools.partial(lambda i, k: (i, k), k=k)) for k in range(5)]
    specs += [pl.BlockSpec((HALO, D), functools.partial(lambda i, k: (jnp.maximum(i * per - 1, 0), k), k=k)) for k in (1, 2)]
    return specs


def _mix_out_fwd(x, mod9, rest, ogs, lgs, cw, wco, wao, wo):
    S = x.shape[0]
    tm = TMX

    def body(x_ref, mod_ref, cb_ref, cc_ref, ch_ref, gc_ref, ga_ref, hc_ref, hh_ref,
             o0, o1, o2, l0, l1, l2, cw_ref, wco_ref, wao_ref, wo_ref,
             xo_ref, o_ref, lse_ref, yc_ref, ya_ref, out_ref):
        i = pl.program_id(0)
        lv = [l0[...], l1[...], l2[...]]
        mx = jnp.maximum(jnp.maximum(lv[0], lv[1]), lv[2])
        es = [jnp.exp(l - mx) for l in lv]
        den = es[0] + es[1] + es[2]
        o = (es[0] / den) * o0[...] + (es[1] / den) * o1[...] + (es[2] / den) * o2[...]
        o_ref[...] = o
        lse_ref[...] = mx + jnp.log(den)
        _, z = _conv_z(cc_ref[...].astype(f32), ch_ref[...].astype(f32), hc_ref[...], hh_ref[...], cw_ref, i == 0)
        p = (cb_ref[...].astype(f32) * z).astype(bf16)
        yc = _dot(p, wco_ref[...])
        ya = _dot(o.astype(bf16), wao_ref[...])
        yc_ref[...] = yc.astype(bf16)
        ya_ref[...] = ya.astype(bf16)
        merged = _sigmoid(gc_ref[...].astype(f32)) * yc + _sigmoid(ga_ref[...].astype(f32)) * ya
        out = _dot(merged.astype(bf16), wo_ref[...])
        out_ref[...] = out.astype(bf16)
        xo_ref[...] = x_ref[...] + mod_ref[5:6, :] * out

    row = pl.BlockSpec((tm, D), lambda i: (i, 0))
    att = pl.BlockSpec((tm, 512), lambda i: (i, 0))
    full = lambda shp: pl.BlockSpec(shp, lambda i: (0, 0))
    return pl.pallas_call(
        body, name="mix_out_fwd", grid=(S // tm,),
        out_shape=[SDS((S, D), f32), SDS((S, 512), f32), SDS((S, 512), f32), SDS((S, D), bf16), SDS((S, D), bf16), SDS((S, D), bf16)],
        in_specs=[row, full((9, D))] + _rest_specs(tm, False) + [att] * 6 + [full((3, D)), full((D, D)), full((512, D)), full((D, D))],
        out_specs=[row, att, att, row, row, row],
        compiler_params=_cp("arbitrary"),
    )(x, mod9, *([rest] * 7), *ogs, *lgs, cw, wco, wao, wo)


def _mix_out_bwd(dxo, mod9, outv, yc, ya, rest, o, cw, wco, wao, wo):
    S = dxo.shape[0]
    tm = TMX
    ni = S // tm

    def body(dxo_ref, mod_ref, out_ref, yc_ref, ya_ref, cb_ref, cc_ref, ch_ref, gc_ref, ga_ref, hc_ref, hh_ref,
             o_ref, cw_ref, wco_ref, wao_ref, wo_ref,
             dp_ref, dg2_ref, do_ref, dwco_ref, dwao_ref, dwo_ref, sm_ref, aco, aao, ao):
        i = pl.program_id(0)

        @pl.when(i == 0)
        def _():
            sm_ref[...] = jnp.zeros_like(sm_ref)
            aco[...] = jnp.zeros_like(aco)
            aao[...] = jnp.zeros_like(aao)
            ao[...] = jnp.zeros_like(ao)

        dxo_v = dxo_ref[...]
        sm_ref[2:3, :] += jnp.sum(out_ref[...].astype(f32) * dxo_v, axis=0, keepdims=True)
        dout = (mod_ref[5:6, :] * dxo_v).astype(bf16)
        dmerged = _dot_nt(dout, wo_ref[...])
        sc, sa = _sigmoid(gc_ref[...].astype(f32)), _sigmoid(ga_ref[...].astype(f32))
        ycv, yav = yc_ref[...].astype(f32), ya_ref[...].astype(f32)
        ao[...] += _dot_tn((sc * ycv + sa * yav).astype(bf16), dout)
        dyc = (dmerged * sc).astype(bf16)
        dya = (dmerged * sa).astype(bf16)
        dg2_ref[:, :D] = (dmerged * ycv * sc * (1.0 - sc)).astype(bf16)
        dg2_ref[:, D:] = (dmerged * yav * sa * (1.0 - sa)).astype(bf16)
        dp_ref[...] = _dot_nt(dyc, wco_ref[...]).astype(bf16)
        _, z = _conv_z(cc_ref[...].astype(f32), ch_ref[...].astype(f32), hc_ref[...], hh_ref[...], cw_ref, i == 0)
        aco[...] += _dot_tn((cb_ref[...].astype(f32) * z).astype(bf16), dyc)
        do_ref[...] = _dot_nt(dya, wao_ref[...])
        aao[...] += _dot_tn(o_ref[...].astype(bf16), dya)

        @pl.when(i == ni - 1)
        def _():
            dwco_ref[...] = aco[...].astype(bf16)
            dwao_ref[...] = aao[...].astype(bf16)
            dwo_ref[...] = ao[...].astype(bf16)

    row = pl.BlockSpec((tm, D), lambda i: (i, 0))
    att = pl.BlockSpec((tm, 512), lambda i: (i, 0))
    full = lambda shp: pl.BlockSpec(shp, lambda i: (0, 0))
    return pl.pallas_call(
        body, name="mix_out_bwd", grid=(ni,),
        out_shape=[SDS((S, D), bf16), SDS((S, 2 * D), bf16), SDS((S, 512), f32),
                   SDS((D, D), bf16), SDS((512, D), bf16), SDS((D, D), bf16), SDS((8, D), f32)],
        in_specs=[row, full((9, D)), row, row, row] + _rest_specs(tm, False) + [att, full((3, D)), full((D, D)), full((512, D)), full((D, D))],
        out_specs=[row, pl.BlockSpec((tm, 2 * D), lambda i: (i, 0)), att, full((D, D)), full((512, D)), full((D, D)), full((8, D))],
        scratch_shapes=[pltpu.VMEM((D, D), f32), pltpu.VMEM((512, D), f32), pltpu.VMEM((D, D), f32)],
        compiler_params=_cp("arbitrary"),
    )(dxo, mod9, outv, yc, ya, *([rest] * 7), o, cw, wco, wao, wo)


def _conv_bwd(dp, rest, cw):
    S = dp.shape[0]
    tm = TM
    per = tm // HALO
    nh = S // HALO
    ni = S // tm

    def body(dp_ref, dpn_ref, cb_ref, cbn_ref, cc_ref, ch_ref, hc_ref, hh_ref, cw_ref, d3_ref, sm_ref):
        i = pl.program_id(0)

        @pl.when(i == 0)
        def _():
            sm_ref[...] = jnp.zeros_like(sm_ref)

        cc, ch = cc_ref[...].astype(f32), ch_ref[...].astype(f32)
        T, z = _conv_z(cc, ch, hc_ref[...], hh_ref[...], cw_ref, i == 0)
        dpv = dp_ref[...].astype(f32)
        cbv = cb_ref[...].astype(f32)
        dz = dpv * cbv
        dzn = jnp.where(i == ni - 1, 0.0, dpn_ref[...].astype(f32) * cbn_ref[...].astype(f32))
        E = jnp.concatenate([dz, dzn], axis=0)
        ne = tm + HALO
        dT = cw_ref[2:3, :] * E + cw_ref[1:2, :] * pltpu.roll(E, ne - 1, 0) + cw_ref[0:1, :] * pltpu.roll(E, ne - 2, 0)
        dT = dT[:tm]
        d3_ref[:, :D] = (dpv * z).astype(bf16)
        d3_ref[:, D:2 * D] = (dT * ch).astype(bf16)
        d3_ref[:, 2 * D:] = (dT * cc).astype(bf16)
        sm_ref[2:3, :] += jnp.sum(dz * T[HALO:], axis=0, keepdims=True)
        sm_ref[1:2, :] += jnp.sum(dz * pltpu.roll(T, 1, 0)[HALO:], axis=0, keepdims=True)
        sm_ref[0:1, :] += jnp.sum(dz * pltpu.roll(T, 2, 0)[HALO:], axis=0, keepdims=True)

    row = pl.BlockSpec((tm, D), lambda i: (i, 0))
    nxt = pl.BlockSpec((HALO, D), lambda i: (jnp.minimum((i + 1) * per, nh - 1), 0))
    col = lambda k: pl.BlockSpec((tm, D), lambda i: (i, k))
    prv = lambda k: pl.BlockSpec((HALO, D), lambda i: (jnp.maximum(i * per - 1, 0), k))
    return pl.pallas_call(
        body, name="conv_bwd", grid=(ni,),
        out_shape=[SDS((S, 3 * D), bf16), SDS((8, D), f32)],
        in_specs=[row, nxt, col(0), nxt, col(1), col(2), prv(1), prv(2), pl.BlockSpec((3, D), lambda i: (0, 0))],
        out_specs=[pl.BlockSpec((tm, 3 * D), lambda i: (i, 0)), pl.BlockSpec((8, D), lambda i: (0, 0))],
        compiler_params=_cp("arbitrary"),
    )(dp, dp, rest, rest, rest, rest, rest, rest, cw)


_DU_RANGES = ((0, 3), (3, 6), (6, 9), (9, 15), (15, 19))
N_CBLK = IN_W // CB


def _mix_in_bwd_dh(dxo, x, mod9, g3, dus, win):
    S = x.shape[0]

    def body(dxo_ref, x_ref, mod_ref, g_ref, s0, s1, s2, s3, s4, w_ref, dxi_ref, sm_ref, acc):
        i, kb = pl.program_id(0), pl.program_id(1)

        @pl.when((i == 0) & (kb == 0))
        def _():
            sm_ref[...] = jnp.zeros_like(sm_ref)

        @pl.when(kb == 0)
        def _():
            acc[...] = jnp.zeros_like(acc)

        for src, (lo, hi) in zip((s0, s1, s2, s3, s4), _DU_RANGES):
            @pl.when((kb >= lo) & (kb < hi))
            def _(src=src):
                acc[...] += _dot_nt(src[...].astype(bf16), w_ref[...])

        @pl.when(kb == N_CBLK - 1)
        def _():
            g, scale = g_ref[1:2, :], mod_ref[4:5, :]
            _, xhat, rstd = _norm_fwd(x_ref[...], g, mod_ref[3:4, :], scale)
            dx, dshift, dscale, dg = _norm_bwd(acc[...], xhat, rstd, g, scale)
            dxi_ref[...] = dxo_ref[...] + dx
            sm_ref[0:1, :] += dshift
            sm_ref[1:2, :] += dscale
            sm_ref[3:4, :] += dg

    row = pl.BlockSpec((TMP, D), lambda i, kb: (i, 0))

    def src_spec(lo, hi):
        return pl.BlockSpec((TMP, CB), lambda i, kb: (i, jnp.clip(kb - lo, 0, hi - lo - 1)))

    return pl.pallas_call(
        body, name="mix_in_bwd_dh", grid=(S // TMP, N_CBLK),
        out_shape=[SDS((S, D), f32), SDS((8, D), f32)],
        in_specs=[row, row, pl.BlockSpec((9, D), lambda i, kb: (0, 0)), pl.BlockSpec((3, D), lambda i, kb: (0, 0))]
                 + [src_spec(lo, hi) for lo, hi in _DU_RANGES] + [pl.BlockSpec((D, CB), lambda i, kb: (0, kb))],
        out_specs=[row, pl.BlockSpec((8, D), lambda i, kb: (0, 0))],
        scratch_shapes=[pltpu.VMEM((TMP, D), f32)],
        compiler_params=_cp("arbitrary", "arbitrary"),
    )(dxo, x, mod9, g3, *dus, win)


def _mix_in_bwd_dw(h, dus):
    S = h.shape[0]
    ni = S // TMW

    def body(h_ref, s0, s1, s2, s3, s4, dw_ref, acc):
        kb, i = pl.program_id(0), pl.program_id(1)

        @pl.when(i == 0)
        def _():
            acc[...] = jnp.zeros_like(acc)

        for src, (lo, hi) in zip((s0, s1, s2, s3, s4), _DU_RANGES):
            @pl.when((kb >= lo) & (kb < hi))
            def _(src=src):
                rows = pl.ds(pl.multiple_of(i * TMW, TMW), TMW)
                acc[...] += _dot_tn(h_ref[rows, :], src[...].astype(bf16))

        @pl.when(i == ni - 1)
        def _():
            dw_ref[...] = acc[...].astype(bf16)

    def src_spec(lo, hi):
        def imap(kb, i):
            on = (kb >= lo) & (kb < hi)
            return (jnp.where(on, i, 0), jnp.clip(kb - lo, 0, hi - lo - 1))
        return pl.BlockSpec((TMW, CB), imap)

    return pl.pallas_call(
        body, name="mix_in_bwd_dw", grid=(N_CBLK, ni),
        out_shape=SDS((D, IN_W), bf16),
        in_specs=[pl.BlockSpec((S, D), lambda kb, i: (0, 0))] + [src_spec(lo, hi) for lo, hi in _DU_RANGES],
        out_specs=pl.BlockSpec((D, CB), lambda kb, i: (0, kb)),
        scratch_shapes=[pltpu.VMEM((D, CB), f32)],
        compiler_params=_cp("arbitrary", "arbitrary"),
    )(h, *dus)


def _loss_head(x, fg, tgt):
    S = x.shape[0]

    def body(x_ref, g_ref, t_ref, ls_ref, dx_ref, sm_ref):
        i = pl.program_id(0)

        @pl.when(i == 0)
        def _():
            ls_ref[...] = jnp.zeros_like(ls_ref)
            sm_ref[...] = jnp.zeros_like(sm_ref)

        xv, g = x_ref[...], g_ref[...]
        rstd = lax.rsqrt(jnp.mean(xv * xv, axis=-1, keepdims=True) + EPS)
        xhat = xv * rstd
        e = xhat * g - t_ref[...]
        ls_ref[...] += 0.5 * jnp.sum(jnp.mean(e * e, axis=-1, keepdims=True))
        dy = e * (1.0 / D)
        sm_ref[0:1, :] += jnp.sum(dy * xhat, axis=0, keepdims=True)
        dxh = dy * g
        dx_ref[...] = rstd * (dxh - xhat * jnp.mean(dxh * xhat, axis=-1, keepdims=True))

    row = pl.BlockSpec((TM, D), lambda i: (i, 0))
    return pl.pallas_call(
        body, name="loss_head", grid=(S // TM,),
        out_shape=[SDS((8, 128), f32), SDS((S, D), f32), SDS((8, D), f32)],
        in_specs=[row, pl.BlockSpec((1, D), lambda i: (0, 0)), row],
        out_specs=[pl.BlockSpec((8, 128), lambda i: (0, 0)), row, pl.BlockSpec((8, D), lambda i: (0, 0))],
        compiler_params=_cp("arbitrary"),
    )(x, fg, tgt)


def _adam(w, g, m, v):
    m2 = B1 * m + (1.0 - B1) * g
    v2 = B2 * v + (1.0 - B2) * (g * g)
    delta = -LR * ((m2 / BC1) / (jnp.sqrt(v2 / BC2) + AEPS) + WD * w)
    return delta, m2, v2


def _row_tile(rows, cols):
    for tr in (512, 352, 256, 128, 64):
        if rows % tr == 0 and tr * cols * 4 <= (5 << 18):
            return tr
    raise ValueError((rows, cols))


def _sum_slots(land):
    _, R, C = land.shape
    tr = _row_tile(R, C)

    def body(l_ref, t_ref):
        t = l_ref[0].astype(f32)
        for k in range(1, N_CHIPS):
            t = t + l_ref[k].astype(f32)
        t_ref[...] = t

    return pl.pallas_call(
        body, name="sum_slots", grid=(R // tr,),
        out_shape=SDS((R, C), f32),
        in_specs=[pl.BlockSpec((N_CHIPS, tr, C), lambda i: (0, i, 0))],
        out_specs=pl.BlockSpec((tr, C), lambda i: (i, 0)),
        compiler_params=_cp("arbitrary"),
    )(land)


def _adamw_pair(w2, m2, v2, ta, tb, outs, slot):
    R, C = ta.shape
    tr = _row_tile(R, C)
    nrt = R // tr

    def body(w_ref, m_ref, v_ref, ta_ref, tb_ref, g_in, d_in, m_in, v_in, g_ref, d_ref, mo_ref, vo_ref):
        g = ta_ref[...] + tb_ref[...]
        delta, mn, vn = _adam(w_ref[...], g, m_ref[...], v_ref[...])
        g_ref[...] = g
        d_ref[...] = delta
        mo_ref[...] = mn
        vo_ref[...] = vn

    big = pl.BlockSpec((tr, C), lambda i: (slot * nrt + i, 0))
    loc = pl.BlockSpec((tr, C), lambda i: (i, 0))
    anyspec = pl.BlockSpec(memory_space=pl.ANY)
    return pl.pallas_call(
        body, name="adamw_pair", grid=(nrt,),
        out_shape=[SDS(o.shape, f32) for o in outs],
        in_specs=[big, big, big, loc, loc] + [anyspec] * 4,
        out_specs=[big] * 4,
        input_output_aliases={5: 0, 6: 1, 7: 2, 8: 3},
        compiler_params=_cp("arbitrary"),
    )(w2, m2, v2, ta, tb, *outs)


def _adamw_small(w, g, m, v):
    def body(w_ref, g_ref, m_ref, v_ref, d_ref, mo_ref, vo_ref):
        delta, mn, vn = _adam(w_ref[...], g_ref[...], m_ref[...], v_ref[...])
        d_ref[...] = delta
        mo_ref[...] = mn
        vo_ref[...] = vn

    return pl.pallas_call(body, name="adamw_small", out_shape=[SDS(w.shape, f32)] * 3)(w, g, m, v)


def _ada_w_update(cs_all, dmod_sh, w, m, v):
    tr = 256

    def body(cs_ref, dm_ref, w_ref, m_ref, v_ref, g_ref, d_ref, mo_ref, vo_ref):
        g = _dot_tn(cs_ref[...].astype(bf16), dm_ref[...].astype(bf16))
        delta, mn, vn = _adam(w_ref[...], g, m_ref[...], v_ref[...])
        g_ref[...] = g
        d_ref[...] = delta
        mo_ref[...] = mn
        vo_ref[...] = vn

    blk = pl.BlockSpec((None, tr, ADA_SH), lambda l, i: (l, i, 0))
    return pl.pallas_call(
        body, name="ada_w_update", grid=(DEPTH, D // tr),
        out_shape=[SDS(w.shape, f32)] * 4,
        in_specs=[pl.BlockSpec((8, tr), lambda l, i: (0, i)), pl.BlockSpec((None, 8, ADA_SH), lambda l, i: (l, 0, 0)), blk, blk, blk],
        out_specs=[blk] * 4,
        compiler_params=_cp("arbitrary", "arbitrary"),
    )(cs_all, dmod_sh, w, m, v)


def _sum_devices(gathered):
    _, R, C = gathered.shape

    def body(g_ref, o_ref):
        t = g_ref[0]
        for k in range(1, 8):
            t = t + g_ref[k]
        o_ref[...] = t

    return pl.pallas_call(body, name="sum_devices", out_shape=SDS((R, C), f32))(gathered)


def _layer_fwd(x, mod9, g3, cw, getw, bias):
    W = {}

    def take(gname, after, mod9):
        w, tok = getw(gname, after)
        W.update(w)
        return mod9 if tok is None else mod9 + tok[0, 0]

    mod9 = take("A", x, mod9)
    x1, h1, a1, u1, hid1, y1 = _ffn_fwd(x, mod9, g3, W["wg0"], W["wu0"], W["wd0"], 0)
    mod9 = take("B", x1, mod9)
    hm, qkv = _mix_qkv(x1, mod9, g3, W["win"])
    rest = _mix_rest(hm, W["win"])
    ogs, lgs = [], []
    for g in range(3):
        og, lg = _attn_fwd(qkv, bias, g)
        ogs.append(og)
        lgs.append(lg)
    mod9 = take("C", ogs[2], mod9)
    x2, o, lse, yc, ya, outv = _mix_out_fwd(x1, mod9, rest, ogs, lgs, cw, W["wco"], W["wao"], W["wo"])
    mod9 = take("D", x2, mod9)
    x3, h3, a3, u3, hid3, y3 = _ffn_fwd(x2, mod9, g3, W["wg1"], W["wu1"], W["wd1"], 2)
    saved = dict(x0=x, x1=x1, x2=x2, h1=h1, a1=a1, u1=u1, hid1=hid1, y1=y1, hm=hm, qkv=qkv, rest=rest, o=o, lse=lse, yc=yc, ya=ya,
                 outv=outv, h3=h3, a3=a3, u3=u3, hid3=hid3, y3=y3)
    return x3, saved, W


def _layer_bwd(dx, sv, mod9, g3, cw, W, bias, emit):
    S = dx.shape[0]
    dw = {}

    def send(gname, mod9):
        tok = emit(gname, dw)
        return mod9 if tok is None else mod9 + tok[0, 0]

    dx2, da, du, dy, sm3 = _ffn_bwd1(dx, sv["x2"], mod9, g3, sv["y3"], sv["a3"], sv["u3"], W["wg1"], W["wu1"], W["wd1"], 2)
    dw["wg1"], dw["wu1"], dw["wd1"] = _ffn_bwd2(sv["h3"], da, du, sv["hid3"], dy)
    mod9 = send("D", mod9)
    dp, dg2, do, dw["wco"], dw["wao"], dw["wo"], smo = _mix_out_bwd(
        dx2, mod9, sv["outv"], sv["yc"], sv["ya"], sv["rest"], sv["o"], cw, W["wco"], W["wao"], W["wo"])
    mod9_c = send("C", mod9)
    cw = cw + (mod9_c - mod9)[0:1, :]
    mod9 = mod9_c
    d3, smc = _conv_bwd(dp, sv["rest"], cw)
    dq = lax.empty((S, 1536), bf16)
    dk = lax.empty((S, 1536), bf16)
    dv = lax.empty((S, 1536), bf16)
    dsaccs = []
    for g in range(3):
        dq, dk, dv, dsg = _attn_bwd(sv["qkv"], do, sv["o"], sv["lse"], bias, dq, dk, dv, g)
        dsaccs.append(dsg)
    dus = (dq, dk, dv, d3, dg2)
    dx1, smm = _mix_in_bwd_dh(dx2, sv["x1"], mod9, g3, dus, W["win"])
    dw["win"] = _mix_in_bwd_dw(sv["hm"], dus)
    mod9 = send("B", mod9)
    dx0, da, du, dy, sm1 = _ffn_bwd1(dx1, sv["x0"], mod9, g3, sv["y1"], sv["a1"], sv["u1"], W["wg0"], W["wu0"], W["wd0"], 0)
    dw["wg0"], dw["wu0"], dw["wd0"] = _ffn_bwd2(sv["h1"], da, du, sv["hid1"], dy)
    send("A", mod9)
    dmod = jnp.concatenate([sm1[0:3], smm[0:2], smo[2:3], sm3[0:3]], axis=0)
    dng = jnp.concatenate([sm1[3:4], smm[3:4], sm3[3:4]], axis=0)
    return dx0, dmod, dng, smc[0:3], jnp.concatenate(dsaccs, axis=0)


def _chip_cols(a, chip, width):
    return lax.dynamic_slice_in_dim(a, chip * width, width, axis=a.ndim - 1)


def kernel(x, c, ada_w, ada_b, norm_g, ffn_w_gate, ffn_w_up, ffn_w_down, w_in, conv_w, w_conv_out, w_attn_out, w_o, rel_bias, final_g, loss_target, m_ada_w, m_ada_b, m_norm_g, m_ffn_w_gate, m_ffn_w_up, m_ffn_w_down, m_w_in, m_conv_w, m_w_conv_out, m_w_attn_out, m_w_o, m_rel_bias, m_final_g, v_ada_w, v_ada_b, v_norm_g, v_ffn_w_gate, v_ffn_w_up, v_ffn_w_down, v_w_in, v_conv_w, v_w_conv_out, v_w_attn_out, v_w_o, v_rel_bias, v_final_g):
    ix, iy, ic = lax.axis_index("x"), lax.axis_index("y"), lax.axis_index("c")
    chip = 2 * ix + iy
    dev = 4 * ix + 2 * iy + ic
    xs = x.reshape(x.shape[1:])
    S = xs.shape[0]
    qd = D // N_CHIPS

    chip_arr = jnp.reshape(chip, (1,)).astype(jnp.int32)
    names = [w[0] for w in WCLASSES]

    def layer_shards(l):
        return [(ffn_w_gate, (l, 0)), (ffn_w_up, (l, 0)), (ffn_w_down, (l, 0)), (ffn_w_gate, (l, 1)), (ffn_w_up, (l, 1)),
                (ffn_w_down, (l, 1)), (w_in, (l,)), (w_conv_out, (l,)), (w_attn_out, (l,)), (w_o, (l,))]

    started = {}
    extra_starts = {(0, "A"): [(0, "B")], (0, "B"): [(0, "C"), (0, "D"), (1, "A")]}

    casts = {}

    def cast_group(l, gname, after):
        shards = layer_shards(l)
        casts[(l, gname)] = _gather_group_cast(GROUPS[gname], [shards[q] for q in GROUPS[gname]], chip_arr, after)

    def start_gather(l, gname, after):
        started[(l, gname)] = _gather_group_start(f"l{l}{gname}", GROUPS[gname], casts[(l, gname)], after)
        return started[(l, gname)][-1]

    pad8 = lambda a: jnp.pad(a, ((0, -a.shape[0] % 8), (0, 0)))
    pack = jnp.concatenate([pad8(c), pad8(norm_g.reshape(3, D)), pad8(conv_w.reshape(3, D))], axis=0)
    g1 = _allgather_small(pack).reshape(8, 24, D)
    c_all = g1[:, 0]
    by_chip = g1[0::2]
    ng_full = jnp.concatenate([by_chip[j, 8:11].reshape(DEPTH, 3, qd) for j in range(N_CHIPS)], axis=-1)
    cw_full = jnp.concatenate([by_chip[j, 16:19].reshape(DEPTH, 3, qd) for j in range(N_CHIPS)], axis=-1)
    mod_sh, cs_all = _mod_shards(c_all, ada_w, _chip_cols(ada_b, chip, ADA_SH))
    g2 = _allgather_small(mod_sh.reshape(DEPTH * 8, ADA_SH)).reshape(8, DEPTH, 8, ADA_SH)
    mine = lax.dynamic_index_in_dim(g2[0::2], dev, axis=2, keepdims=False)
    mod = jnp.transpose(mine, (1, 0, 2)).reshape(DEPTH, 9, D)

    cast_group(0, "A", c)
    tok0 = start_gather(0, "A", mod)
    for l in range(DEPTH):
        for gname in GROUPS:
            if (l, gname) not in casts:
                cast_group(l, gname, tok0)
    buckets = jnp.asarray(_bucket_table())
    bias = _bias_blocks(rel_bias, buckets)
    last_cast = casts[(DEPTH - 1, "D")][-1]

    def make_getw(l):
        def getw(gname, after):
            if (l, gname) == (0, "A"):
                after = last_cast
            full = _gather_group_wait(f"l{l}{gname}", GROUPS[gname], started[(l, gname)], after)
            tok = None
            for nl, ng in extra_starts.get((l, gname), []) + [(l + 1, gname)]:
                if nl < DEPTH and (nl, ng) not in started:
                    tok = start_gather(nl, ng, full[0] if tok is None else tok)
            return {names[q]: f for q, f in zip(GROUPS[gname], full)}, tok
        return getw

    Ws, saves = [], []
    xc = xs
    for l in range(DEPTH):
        xc, sv, W = _layer_fwd(xc, mod[l], ng_full[l], cw_full[l], make_getw(l), bias)
        Ws.append(W)
        saves.append(sv)

    ls, dx, smf = _loss_head(xc, final_g.reshape(1, D), loss_target.reshape(loss_target.shape[1:]))
    loss = lax.psum(ls[0, 0], ("x", "y", "c"))

    params = dict(wg=ffn_w_gate, wu=ffn_w_up, wd=ffn_w_down, win=w_in, wco=w_conv_out, wao=w_attn_out, wo=w_o)
    moms = dict(wg=m_ffn_w_gate, wu=m_ffn_w_up, wd=m_ffn_w_down, win=m_w_in, wco=m_w_conv_out, wao=m_w_attn_out, wo=m_w_o)
    vars_ = dict(wg=v_ffn_w_gate, wu=v_ffn_w_up, wd=v_ffn_w_down, win=v_w_in, wco=v_w_conv_out, wao=v_w_attn_out, wo=v_w_o)
    flat = lambda a: a.reshape(-1, a.shape[-1])
    big_out = {k: [lax.empty(flat(p).shape, f32) for _ in range(4)] for k, p in params.items()}
    dmods, dngs, dcws, dsaccs = [None] * DEPTH, [None] * DEPTH, [None] * DEPTH, [None] * DEPTH

    def finish(l, gname, started, after):
        group = GROUPS[gname]
        pieces, lands = _scatter_group_wait(f"l{l}{gname}", group, started, after)
        ts = [_sum_own_slots(pieces[i], lands[i], *_cls(q), chip_arr) for i, q in enumerate(group)]
        tsib = _swap_sibling(ts)
        for i, q in enumerate(group):
            name = names[q]
            key = name.rstrip("01")
            slot = 2 * l + int(name[-1]) if name[-1] in "01" else l
            big_out[key] = _adamw_pair(flat(params[key]), flat(moms[key]), flat(vars_[key]), ts[i], tsib[i], big_out[key], slot)

    pending, tok = [], None
    for l in reversed(range(DEPTH)):
        modl = mod[l] if tok is None else mod[l] + tok[0, 0]
        mine = []

        def emit(gname, dw, l=l, mine=mine):
            prev = mine[-1][2][-1] if mine else dx
            mine.append((l, gname, _scatter_group_start(f"l{l}{gname}", GROUPS[gname], [dw[names[q]] for q in GROUPS[gname]], prev)))
            return mine[-1][2][-1]

        dx, dmods[l], dngs[l], dcws[l], dsaccs[l] = _layer_bwd(dx, saves[l], modl, ng_full[l], cw_full[l], Ws[l], bias, emit)
        for pl_, pg, pst in pending:
            finish(pl_, pg, pst, dx)
        pending, tok = mine, mine[-1][2][-1]
    for pl_, pg, pst in pending[:-1]:
        finish(pl_, pg, pst, pending[-1][2][-1])

    drb = jnp.transpose(_bias_grad(dsaccs, buckets)[:, 0, :NUM_BUCKETS])
    drb_row = jnp.pad(drb.reshape(1, NUM_BUCKETS * 24), ((0, 0), (0, D - NUM_BUCKETS * 24)))
    pack2 = jnp.concatenate([pad8(a) for a in dmods] + [pad8(a) for a in dngs] + [pad8(a) for a in dcws] + [smf, pad8(drb_row)], axis=0)
    n_rows = pack2.shape[0]
    g3 = _allgather_small(pack2).reshape(8, n_rows, D)
    tot = _sum_devices(g3)
    o_ng, o_cw, o_fg, o_rb = 16 * DEPTH, 24 * DEPTH, 32 * DEPTH, 32 * DEPTH + 8
    g_ada_b = jnp.stack([tot[16 * l:16 * l + 9] for l in range(DEPTH)]).reshape(DEPTH, 9 * D)
    g_norm_g = _chip_cols(jnp.stack([tot[o_ng + 8 * l:o_ng + 8 * l + 3] for l in range(DEPTH)]), chip, qd)
    g_conv_w = _chip_cols(jnp.stack([tot[o_cw + 8 * l:o_cw + 8 * l + 3] for l in range(DEPTH)]), chip, qd)
    g_final_g = tot[o_fg]
    g_rel_bias = tot[o_rb, :NUM_BUCKETS * 24].reshape(NUM_BUCKETS, 24)
    dmod_all = jnp.stack([g3[:, 16 * l:16 * l + 9].reshape(8, 9 * D) for l in range(DEPTH)])
    dmod_sh = _chip_cols(dmod_all, chip, ADA_SH)
    g_ada_w, d_ada_w, nm_ada_w, nv_ada_w = _ada_w_update(cs_all, dmod_sh, ada_w, m_ada_w, v_ada_w)

    def small(w, g, m, v):
        shp = w.shape
        to2 = lambda a: a.reshape(-1, shp[-1])
        return [o.reshape(shp) for o in _adamw_small(to2(w), to2(g), to2(m), to2(v))]

    d_ada_b, nm_ada_b, nv_ada_b = small(ada_b, g_ada_b, m_ada_b, v_ada_b)
    d_norm_g, nm_norm_g, nv_norm_g = small(norm_g, g_norm_g, m_norm_g, v_norm_g)
    d_conv_w, nm_conv_w, nv_conv_w = small(conv_w, g_conv_w, m_conv_w, v_conv_w)
    d_rel_bias, nm_rel_bias, nv_rel_bias = small(rel_bias, g_rel_bias, m_rel_bias, v_rel_bias)
    d_final_g, nm_final_g, nv_final_g = small(final_g, g_final_g, m_final_g, v_final_g)

    behind = nv_ada_w[0, 0:8, 0:128]
    for key in big_out:
        behind = behind + big_out[key][3][0:8, 0:128]
    finish(*pending[-1], behind)

    def big(key, which):
        return big_out[key][which].reshape(params[key].shape)

    grads = [g_ada_w, g_ada_b, g_norm_g, big("wg", 0), big("wu", 0), big("wd", 0), big("win", 0), g_conv_w, big("wco", 0),
             big("wao", 0), big("wo", 0), g_rel_bias, g_final_g]
    deltas = [d_ada_w, d_ada_b, d_norm_g, big("wg", 1), big("wu", 1), big("wd", 1), big("win", 1), d_conv_w, big("wco", 1),
              big("wao", 1), big("wo", 1), d_rel_bias, d_final_g]
    new_m = [nm_ada_w, nm_ada_b, nm_norm_g, big("wg", 2), big("wu", 2), big("wd", 2), big("win", 2), nm_conv_w, big("wco", 2),
             big("wao", 2), big("wo", 2), nm_rel_bias, nm_final_g]
    new_v = [nv_ada_w, nv_ada_b, nv_norm_g, big("wg", 3), big("wu", 3), big("wd", 3), big("win", 3), nv_conv_w, big("wco", 3),
             big("wao", 3), big("wo", 3), nv_rel_bias, nv_final_g]
    return (loss, dx.reshape(x.shape), *grads, *deltas, *new_m, *new_v)
```

```python
import functools

import numpy as np
import jax
import jax.numpy as jnp
from jax import lax
from jax.experimental import pallas as pl
from jax.experimental.pallas import tpu as pltpu

f32, bf16 = jnp.float32, jnp.bfloat16
SDS = jax.ShapeDtypeStruct
MESH = pl.DeviceIdType.MESH

D = 1024
DEPTH = 4
N_CHIPS = 4
FB = 704
HD = 64
QKV_W = 4608
REST_W = 5120
IN_W = QKV_W + REST_W
WIN_SH = IN_W // N_CHIPS
ADA_SH = 9 * D // N_CHIPS
BLK = 128
DILATIONS = (1, 4, 16)
Q_BLOCKS = (4, 1, 1)
NUM_BUCKETS, MAX_DISTANCE = 32, 2048
EPS = 1e-6
NEG = -1e30
SCALE = HD ** -0.5
LR, B1, B2, AEPS, WD, STEP = 0.001, 0.9, 0.999, 1e-08, 0.01, 10
BC1 = 1.0 - B1 ** STEP
BC2 = 1.0 - B2 ** STEP
VMEM_LIMIT = 56 * 1024 * 1024
TM = 512
TMW = 1024
TMP = 1024
TMF = 1024
SH_STEP = 2
TMX = 256
HALO = 16
CB = 512


def _cp(*sem):
    return pltpu.CompilerParams(dimension_semantics=sem if sem else None, vmem_limit_bytes=VMEM_LIMIT)


def _dot(a, b):
    return jnp.dot(a, b, preferred_element_type=f32)


def _dot_nt(a, b):
    return lax.dot_general(a, b, (((1,), (1,)), ((), ())), preferred_element_type=f32)


def _dot_tn(a, b):
    return lax.dot_general(a, b, (((0,), (0,)), ((), ())), preferred_element_type=f32)


def _sigmoid(x):
    return 0.5 * jnp.tanh(0.5 * x) + 0.5


def _norm_fwd(x, g, shift, scale):
    rstd = lax.rsqrt(jnp.mean(x * x, axis=-1, keepdims=True) + EPS)
    xhat = x * rstd
    return xhat * g * (1.0 + scale) + shift, xhat, rstd


def _norm_bwd(dh, xhat, rstd, g, scale):
    dshift = jnp.sum(dh, axis=0, keepdims=True)
    dscale = jnp.sum(dh * xhat * g, axis=0, keepdims=True)
    dg = jnp.sum(dh * xhat * (1.0 + scale), axis=0, keepdims=True)
    dxh = dh * (g * (1.0 + scale))
    dx = rstd * (dxh - xhat * jnp.mean(dxh * xhat, axis=-1, keepdims=True))
    return dx, dshift, dscale, dg


def _allgather_small(xp):
    m_per, n = xp.shape

    def body(x_ref, out_ref, send_sems, recv_sems, local_sem):
        x, y, c = lax.axis_index("x"), lax.axis_index("y"), lax.axis_index("c")
        me, sibling = (x, y, c), (x, y, 1 - c)
        chips = [(1 - x, y), (x, 1 - y), (1 - x, 1 - y)]

        def rows(px, py, pc):
            return out_ref.at[pl.ds((4 * px + 2 * py + pc) * m_per, m_per), :]

        def copy(k, block, to, src=None):
            return pltpu.make_async_remote_copy(
                src_ref=rows(*block) if src is None else src, dst_ref=rows(*block),
                send_sem=send_sems.at[k], recv_sem=recv_sems.at[k], device_id=to, device_id_type=MESH)

        mine = pltpu.make_async_copy(x_ref, rows(*me), local_sem)
        mine.start()
        first = [copy(0, me, sibling, src=x_ref)]
        first += [copy(1 + j, me, (*chip, c), src=x_ref) for j, chip in enumerate(chips)]
        for cp in first:
            cp.start()
        passed = [copy(4 + j, (*chip, c), sibling) for j, chip in enumerate(chips)]
        for j, chip in enumerate(chips):
            copy(1 + j, (*chip, c), me).wait_recv()
            passed[j].start()
        copy(0, sibling, me).wait_recv()
        for j, chip in enumerate(chips):
            copy(4 + j, (*chip, 1 - c), me).wait_recv()
        for cp in first + passed:
            cp.wait_send()
        mine.wait()

    return pl.pallas_call(
        body, name="allgather_small",
        out_shape=SDS((8 * m_per, n), xp.dtype),
        in_specs=[pl.BlockSpec(memory_space=pltpu.VMEM)],
        out_specs=pl.BlockSpec(memory_space=pltpu.VMEM),
        scratch_shapes=[pltpu.SemaphoreType.DMA((7,)), pltpu.SemaphoreType.DMA((7,)), pltpu.SemaphoreType.DMA],
        compiler_params=pltpu.CompilerParams(vmem_limit_bytes=VMEM_LIMIT),
    )(xp)


WCLASSES = (
    ("wg0", "lead", (D, FB)), ("wu0", "lead", (D, FB)), ("wd0", "row", (FB, D)),
    ("wg1", "lead", (D, FB)), ("wu1", "lead", (D, FB)), ("wd1", "row", (FB, D)),
    ("win", "col", (D, WIN_SH)), ("wco", "row", (D // N_CHIPS, D)), ("wao", "col", (512, D // N_CHIPS)),
    ("wo", "row", (D // N_CHIPS, D)),
)
NCLS = len(WCLASSES)


def _full_shape(kind, shp):
    if kind == "lead":
        return (N_CHIPS,) + shp
    if kind == "row":
        return (N_CHIPS * shp[0], shp[1])
    return (shp[0], N_CHIPS * shp[1])


def _shard_view(ref, kind, shp, j):
    if kind == "lead":
        return ref.at[j]
    if kind == "row":
        return ref.at[pl.ds(j * shp[0], shp[0]), :]
    return ref.at[:, pl.ds(j * shp[1], shp[1])]


def _half(ref, shp, h):
    hr = shp[0] // 2
    return ref.at[pl.ds(pl.multiple_of(h * hr, 16), hr), :]


def _gather_weights(shards):
    n = NCLS

    def body(*refs):
        ins, outs = refs[:n], refs[n:2 * n]
        send1, recv1, send2, recv2, lsem = refs[2 * n:]
        x, y, c = lax.axis_index("x"), lax.axis_index("y"), lax.axis_index("c")
        chip = 2 * x + y
        sibling = (x, y, 1 - c)

        for mc in range(N_CHIPS):
            @pl.when(chip == mc)
            def _(mc=mc):
                local = []
                for q, (_, kind, shp) in enumerate(WCLASSES):
                    cp = pltpu.make_async_copy(ins[q], _shard_view(outs[q], kind, shp, mc), lsem.at[q])
                    cp.start()
                    local.append(cp)
                sends = []
                for k in (1, 2, 3):
                    pj = mc ^ k
                    for q, (_, kind, shp) in enumerate(WCLASSES):
                        cp = pltpu.make_async_remote_copy(
                            src_ref=_half(ins[q], shp, c), dst_ref=_half(_shard_view(outs[q], kind, shp, mc), shp, c),
                            send_sem=send1.at[q * 3 + k - 1], recv_sem=recv1.at[q * 3 + k - 1],
                            device_id=(pj >> 1, pj & 1, c), device_id_type=MESH)
                        cp.start()
                        sends.append(cp)
                for k in (1, 2, 3):
                    pj = mc ^ k
                    for q, (_, kind, shp) in enumerate(WCLASSES):
                        landed = _half(_shard_view(outs[q], kind, shp, pj), shp, c)
                        pltpu.make_async_remote_copy(
                            src_ref=landed, dst_ref=landed, send_sem=send1.at[q * 3 + k - 1], recv_sem=recv1.at[q * 3 + k - 1],
                            device_id=(pj >> 1, pj & 1, c), device_id_type=MESH).wait_recv()
                        cp = pltpu.make_async_remote_copy(
                            src_ref=landed, dst_ref=landed, send_sem=send2.at[q * 3 + k - 1], recv_sem=recv2.at[q * 3 + k - 1],
                            device_id=sibling, device_id_type=MESH)
                        cp.start()
                        sends.append(cp)
                for k in (1, 2, 3):
                    pj = mc ^ k
                    for q, (_, kind, shp) in enumerate(WCLASSES):
                        other = _half(_shard_view(outs[q], kind, shp, pj), shp, 1 - c)
                        pltpu.make_async_remote_copy(
                            src_ref=other, dst_ref=other, send_sem=send2.at[q * 3 + k - 1], recv_sem=recv2.at[q * 3 + k - 1],
                            device_id=sibling, device_id_type=MESH).wait_recv()
                for cp in sends:
                    cp.wait_send()
                for cp in local:
                    cp.wait()

    anyspec = pl.BlockSpec(memory_space=pl.ANY)
    return pl.pallas_call(
        body, name="gather_weights",
        out_shape=[SDS(_full_shape(kind, shp), bf16) for _, kind, shp in WCLASSES],
        in_specs=[anyspec] * n, out_specs=[anyspec] * n,
        scratch_shapes=[pltpu.SemaphoreType.DMA((3 * n,)), pltpu.SemaphoreType.DMA((3 * n,)),
                        pltpu.SemaphoreType.DMA((3 * n,)), pltpu.SemaphoreType.DMA((3 * n,)),
                        pltpu.SemaphoreType.DMA((n,))],
    )(*shards)


def _scatter_grads(pieces):
    n = NCLS

    def body(*refs):
        ins, outs = refs[:n], refs[n:2 * n]
        send1, recv1, lsem = refs[2 * n:]
        x, y, c = lax.axis_index("x"), lax.axis_index("y"), lax.axis_index("c")
        chip = 2 * x + y

        for mc in range(N_CHIPS):
            @pl.when(chip == mc)
            def _(mc=mc):
                local, sends = [], []
                for q, (_, kind, shp) in enumerate(WCLASSES):
                    cp = pltpu.make_async_copy(_shard_view(ins[q], kind, shp, mc), outs[q].at[0], lsem.at[q])
                    cp.start()
                    local.append(cp)
                for k in (1, 2, 3):
                    pj = mc ^ k
                    for q, (_, kind, shp) in enumerate(WCLASSES):
                        cp = pltpu.make_async_remote_copy(
                            src_ref=_shard_view(ins[q], kind, shp, pj), dst_ref=outs[q].at[k],
                            send_sem=send1.at[q * 3 + k - 1], recv_sem=recv1.at[q * 3 + k - 1],
                            device_id=(pj >> 1, pj & 1, c), device_id_type=MESH)
                        cp.start()
                        sends.append(cp)
                for cp in sends:
                    cp.wait_recv()
                for cp in sends:
                    cp.wait_send()
                for cp in local:
                    cp.wait()

    anyspec = pl.BlockSpec(memory_space=pl.ANY)
    return pl.pallas_call(
        body, name="scatter_grads",
        out_shape=[SDS((N_CHIPS,) + shp, bf16) for _, _, shp in WCLASSES],
        in_specs=[anyspec] * n, out_specs=[anyspec] * n,
        scratch_shapes=[pltpu.SemaphoreType.DMA((3 * n,)), pltpu.SemaphoreType.DMA((3 * n,)), pltpu.SemaphoreType.DMA((n,))],
    )(*pieces)


def _swap_sibling(ts):
    n = len(ts)

    def body(*refs):
        ins, outs = refs[:n], refs[n:2 * n]
        send, recv = refs[2 * n:]
        x, y, c = lax.axis_index("x"), lax.axis_index("y"), lax.axis_index("c")
        cps = []
        for q in range(n):
            cp = pltpu.make_async_remote_copy(src_ref=ins[q], dst_ref=outs[q], send_sem=send.at[q], recv_sem=recv.at[q],
                                              device_id=(x, y, 1 - c), device_id_type=MESH)
            cp.start()
            cps.append(cp)
        for cp in cps:
            cp.wait_recv()
        for cp in cps:
            cp.wait_send()

    anyspec = pl.BlockSpec(memory_space=pl.ANY)
    return pl.pallas_call(
        body, name="swap_sibling",
        out_shape=[SDS(t.shape, t.dtype) for t in ts],
        in_specs=[anyspec] * n, out_specs=[anyspec] * n,
        scratch_shapes=[pltpu.SemaphoreType.DMA((n,)), pltpu.SemaphoreType.DMA((n,))],
    )(*ts)


HBM_SPEC = pl.BlockSpec(memory_space=pltpu.HBM)
SEM_SPEC = pl.BlockSpec(memory_space=pltpu.SEMAPHORE)
ANY_SPEC = pl.BlockSpec(memory_space=pl.ANY)
EFFECT = pltpu.SideEffectType.DATAFLOW_SIDE_EFFECTING
N_COPIES = 3 * NCLS


def _in_hbm(a):
    return pltpu.with_memory_space_constraint(a, pltpu.HBM)


def _chip_index():
    return 2 * lax.axis_index("x") + lax.axis_index("y")


def _place_own(shards):
    n = NCLS

    def body(*refs):
        ins, outs, lsem = refs[:n], refs[n:2 * n], refs[2 * n]
        chip = _chip_index()
        for mc in range(N_CHIPS):
            @pl.when(chip == mc)
            def _(mc=mc):
                cps = [pltpu.make_async_copy(ins[q], _shard_view(outs[q], kind, shp, mc), lsem.at[q])
                       for q, (_, kind, shp) in enumerate(WCLASSES)]
                for cp in cps:
                    cp.start()
                for cp in cps:
                    cp.wait()

    return pl.pallas_call(
        body, name="place_own",
        out_shape=[SDS(_full_shape(kind, shp), bf16) for _, kind, shp in WCLASSES],
        in_specs=[ANY_SPEC] * n, out_specs=[ANY_SPEC] * n,
        scratch_shapes=[pltpu.SemaphoreType.DMA((n,))],
    )(*shards)


def _take_own(pieces):
    n = NCLS

    def body(*refs):
        ins, outs, lsem = refs[:n], refs[n:2 * n], refs[2 * n]
        chip = _chip_index()
        for mc in range(N_CHIPS):
            @pl.when(chip == mc)
            def _(mc=mc):
                cps = [pltpu.make_async_copy(_shard_view(ins[q], kind, shp, mc), outs[q].at[0], lsem.at[q])
                       for q, (_, kind, shp) in enumerate(WCLASSES)]
                for cp in cps:
                    cp.start()
                for cp in cps:
                    cp.wait()

    return pl.pallas_call(
        body, name="take_own",
        out_shape=[SDS((N_CHIPS,) + shp, bf16) for _, _, shp in WCLASSES],
        in_specs=[ANY_SPEC] * n, out_specs=[ANY_SPEC] * n,
        scratch_shapes=[pltpu.SemaphoreType.DMA((n,))],
    )(*pieces)


def _split_start(name, srcs, dsts, after, src_view, dst_view):
    n = NCLS

    def body(*refs):
        src, dst = refs[:n], refs[n:2 * n]
        send, recv = refs[2 * n + 1], refs[2 * n + 2]
        token = refs[-1]
        c = lax.axis_index("c")
        chip = _chip_index()
        for mc in range(N_CHIPS):
            @pl.when(chip == mc)
            def _(mc=mc):
                for k in (1, 2, 3):
                    pj = mc ^ k
                    for q in range(n):
                        pltpu.make_async_remote_copy(
                            src_ref=src_view(src[q], q, mc, pj), dst_ref=dst_view(dst[q], q, mc, k),
                            send_sem=send.at[q * 3 + k - 1], recv_sem=recv.at[q * 3 + k - 1],
                            device_id=(pj >> 1, pj & 1, c), device_id_type=MESH).start()
        token[...] = jnp.zeros_like(token)

    return pl.pallas_call(
        body, name=name,
        out_shape=(pltpu.SemaphoreType.DMA((N_COPIES,)), pltpu.SemaphoreType.DMA((N_COPIES,)),
                   *[pltpu.HBM(a.shape, a.dtype) for a in srcs], *[pltpu.HBM(a.shape, a.dtype) for a in dsts], SDS((8, 128), f32)),
        in_specs=[HBM_SPEC] * (2 * n) + [ANY_SPEC],
        out_specs=(SEM_SPEC, SEM_SPEC, *([HBM_SPEC] * (2 * n)), pl.BlockSpec(memory_space=pltpu.VMEM)),
        input_output_aliases={i: 2 + i for i in range(2 * n)},
        compiler_params=pltpu.CompilerParams(has_side_effects=EFFECT),
    )(*[_in_hbm(a) for a in srcs], *[_in_hbm(a) for a in dsts], after)


def _split_wait(name, started, after, arrival_view):
    n = NCLS
    send, recv = started[0], started[1]
    srcs, dsts = started[2:2 + n], started[2 + n:2 + 2 * n]

    def body(*refs):
        src, dst = refs[:n], refs[n:2 * n]
        send_sem, recv_sem = refs[2 * n], refs[2 * n + 1]
        x, y, c = lax.axis_index("x"), lax.axis_index("y"), lax.axis_index("c")
        for k in (1, 2, 3):
            for q in range(n):
                arrival = arrival_view(dst[q], q, k)
                cp = pltpu.make_async_remote_copy(
                    src_ref=arrival, dst_ref=arrival, send_sem=send_sem.at[q * 3 + k - 1], recv_sem=recv_sem.at[q * 3 + k - 1],
                    device_id=(x, y, 1 - c), device_id_type=MESH)
                cp.wait_send()
                cp.wait_recv()

    out = pl.pallas_call(
        body, name=name,
        out_shape=(*[pltpu.HBM(a.shape, a.dtype) for a in srcs], *[pltpu.HBM(a.shape, a.dtype) for a in dsts]),
        in_specs=[HBM_SPEC] * (2 * n) + [SEM_SPEC, SEM_SPEC, ANY_SPEC],
        out_specs=tuple([HBM_SPEC] * (2 * n)),
        input_output_aliases={i: i for i in range(2 * n)},
        compiler_params=pltpu.CompilerParams(has_side_effects=EFFECT),
    )(*srcs, *dsts, send, recv, after)
    return out[n:]


def _cls(q):
    return WCLASSES[q][1], WCLASSES[q][2]


def _gather_start(shards, after):
    fulls = _place_own(shards)
    return _split_start("gather_start", shards, fulls, after,
                        lambda ref, q, mc, pj: ref,
                        lambda ref, q, mc, k: _shard_view(ref, *_cls(q), mc))


def _gather_wait(started, after):
    return _split_wait("gather_wait", started, after, lambda ref, q, k: _shard_view(ref, *_cls(q), 0))


def _scatter_start(pieces, after):
    lands = _take_own(pieces)
    return _split_start("scatter_start", pieces, lands, after,
                        lambda ref, q, mc, pj: _shard_view(ref, *_cls(q), pj),
                        lambda ref, q, mc, k: ref.at[k])


def _scatter_wait(started, after):
    return _split_wait("scatter_wait", started, after, lambda ref, q, k: ref.at[k])


GROUPS = {"A": (0, 1, 2), "B": (6,), "C": (7, 8, 9), "D": (3, 4, 5)}


def _own_spec(kind, shp, tr):
    R, C = shp
    if kind == "lead":
        return pl.BlockSpec((None, tr, C), lambda i, chip: (chip[0], i, 0))
    if kind == "row":
        return pl.BlockSpec((tr, C), lambda i, chip: (chip[0] * (R // tr) + i, 0))
    return pl.BlockSpec((tr, C), lambda i, chip: (i, chip[0]))


def _cast_place(shards, kind, shp, chip_arr, after):
    n = len(shards)
    R, C = shp
    tr = _row_tile(R, C)

    def body(chip_ref, *refs):
        for q in range(n):
            refs[n + 1 + q][...] = refs[q][...].astype(bf16)

    def in_spec(lead):
        return pl.BlockSpec((None,) * len(lead) + (tr, C), lambda i, chip: (*lead, i, 0))

    return pl.pallas_call(
        body, name="cast_place",
        grid_spec=pltpu.PrefetchScalarGridSpec(
            num_scalar_prefetch=1, grid=(R // tr,),
            in_specs=[in_spec(lead) for _, lead in shards] + [ANY_SPEC],
            out_specs=[_own_spec(kind, shp, tr)] * n),
        out_shape=[SDS(_full_shape(kind, shp), bf16)] * n,
        compiler_params=_cp("arbitrary"),
    )(chip_arr, *[a for a, _ in shards], after)


def _sum_own_slots(piece, land, kind, shp, chip_arr):
    R, C = shp
    tr = _row_tile(R, C)

    def body(chip_ref, p_ref, l_ref, t_ref):
        t = p_ref[...].astype(f32)
        for k in range(N_CHIPS - 1):
            t = t + l_ref[k].astype(f32)
        t_ref[...] = t.astype(bf16)

    return pl.pallas_call(
        body, name="sum_own_slots",
        grid_spec=pltpu.PrefetchScalarGridSpec(
            num_scalar_prefetch=1, grid=(R // tr,),
            in_specs=[_own_spec(kind, shp, tr), pl.BlockSpec((N_CHIPS - 1, tr, C), lambda i, chip: (0, i, 0))],
            out_specs=pl.BlockSpec((tr, C), lambda i, chip: (i, 0))),
        out_shape=SDS((R, C), bf16),
        compiler_params=_cp("arbitrary"),
    )(chip_arr, piece, land)


def _xfer_start(name, arrays, ng, after, src_view, dst_view):
    na = len(arrays)

    def body(*refs):
        arr = refs[:na]
        send, recv, token = refs[na + 1], refs[na + 2], refs[-1]
        c = lax.axis_index("c")
        chip = _chip_index()
        for mc in range(N_CHIPS):
            @pl.when(chip == mc)
            def _(mc=mc):
                for k in (1, 2, 3):
                    pj = mc ^ k
                    for i in range(ng):
                        pltpu.make_async_remote_copy(
                            src_ref=src_view(arr, i, mc, pj), dst_ref=dst_view(arr, i, mc, k),
                            send_sem=send.at[i * 3 + k - 1], recv_sem=recv.at[i * 3 + k - 1],
                            device_id=(pj >> 1, pj & 1, c), device_id_type=MESH).start()
        token[...] = jnp.zeros_like(token)

    return pl.pallas_call(
        body, name=name,
        out_shape=(pltpu.SemaphoreType.DMA((3 * ng,)), pltpu.SemaphoreType.DMA((3 * ng,)),
                   *[pltpu.HBM(a.shape, a.dtype) for a in arrays], SDS((8, 128), f32)),
        in_specs=[HBM_SPEC] * na + [ANY_SPEC],
        out_specs=(SEM_SPEC, SEM_SPEC, *([HBM_SPEC] * na), pl.BlockSpec(memory_space=pltpu.VMEM)),
        input_output_aliases={i: 2 + i for i in range(na)},
        compiler_params=pltpu.CompilerParams(has_side_effects=EFFECT),
    )(*[_in_hbm(a) for a in arrays], after)


def _xfer_wait(name, started, ng, after, arrival_view):
    send, recv = started[0], started[1]
    arrays = started[2:-1]
    na = len(arrays)

    def body(*refs):
        arr = refs[:na]
        send_sem, recv_sem = refs[na], refs[na + 1]
        x, y, c = lax.axis_index("x"), lax.axis_index("y"), lax.axis_index("c")
        for k in (1, 2, 3):
            for i in range(ng):
                arrival = arrival_view(arr, i)
                cp = pltpu.make_async_remote_copy(
                    src_ref=arrival, dst_ref=arrival, send_sem=send_sem.at[i * 3 + k - 1], recv_sem=recv_sem.at[i * 3 + k - 1],
                    device_id=(x, y, 1 - c), device_id_type=MESH)
                cp.wait_send()
                cp.wait_recv()

    return pl.pallas_call(
        body, name=name,
        out_shape=tuple(pltpu.HBM(a.shape, a.dtype) for a in arrays),
        in_specs=[HBM_SPEC] * na + [SEM_SPEC, SEM_SPEC, ANY_SPEC],
        out_specs=tuple([HBM_SPEC] * na),
        input_output_aliases={i: i for i in range(na)},
        compiler_params=pltpu.CompilerParams(has_side_effects=EFFECT),
    )(*arrays, send, recv, after)


def _gather_group_cast(group, shards_f32, chip_arr, after):
    fulls = [None] * len(group)
    by_shape = {}
    for i, q in enumerate(group):
        by_shape.setdefault(_cls(q), []).append(i)
    for (kind, shp), idx in by_shape.items():
        for i, f in zip(idx, _cast_place([shards_f32[i] for i in idx], kind, shp, chip_arr, after)):
            fulls[i] = f
    return fulls


def _gather_group_start(tag, group, fulls, after):
    def view(arr, i, mc, _):
        kind, shp = _cls(group[i])
        return _half(_shard_view(arr[i], kind, shp, mc), shp, lax.axis_index("c"))
    return _xfer_start("gather_start_" + tag, fulls, len(group), after, view, view)


def _gather_group_forward(tag, group, started, after):
    ng = len(group)
    send1, recv1 = started[0], started[1]
    arrays = started[2:-1]
    na = len(arrays)

    def body(*refs):
        arr = refs[:na]
        send_in, recv_in = refs[na], refs[na + 1]
        send2, recv2, token = refs[na + 3], refs[na + 4], refs[-1]
        x, y, c = lax.axis_index("x"), lax.axis_index("y"), lax.axis_index("c")
        chip = _chip_index()
        for mc in range(N_CHIPS):
            @pl.when(chip == mc)
            def _(mc=mc):
                for k in (1, 2, 3):
                    pj = mc ^ k
                    for i in range(ng):
                        kind, shp = _cls(group[i])
                        landed = _half(_shard_view(arr[i], kind, shp, pj), shp, c)
                        pltpu.make_async_remote_copy(
                            src_ref=landed, dst_ref=landed, send_sem=send_in.at[i * 3 + k - 1], recv_sem=recv_in.at[i * 3 + k - 1],
                            device_id=(pj >> 1, pj & 1, c), device_id_type=MESH).wait_recv()
                        pltpu.make_async_remote_copy(
                            src_ref=landed, dst_ref=landed, send_sem=send2.at[i * 3 + k - 1], recv_sem=recv2.at[i * 3 + k - 1],
                            device_id=(x, y, 1 - c), device_id_type=MESH).start()
        token[...] = jnp.zeros_like(token)

    return pl.pallas_call(
        body, name="gather_forward_" + tag,
        out_shape=(pltpu.SemaphoreType.DMA((3 * ng,)), pltpu.SemaphoreType.DMA((3 * ng,)),
                   *[pltpu.HBM(a.shape, a.dtype) for a in arrays], SDS((8, 128), f32)),
        in_specs=[HBM_SPEC] * na + [SEM_SPEC, SEM_SPEC, ANY_SPEC],
        out_specs=(SEM_SPEC, SEM_SPEC, *([HBM_SPEC] * na), pl.BlockSpec(memory_space=pltpu.VMEM)),
        input_output_aliases={i: 2 + i for i in range(na)},
        compiler_params=pltpu.CompilerParams(has_side_effects=EFFECT),
    )(*arrays, send1, recv1, after)


def _gather_group_wait(tag, group, send1, forwarded, after):
    ng = len(group)
    send2, recv2 = forwarded[0], forwarded[1]
    arrays = forwarded[2:-1]
    na = len(arrays)

    def body(*refs):
        arr = refs[:na]
        s1, s2, r2 = refs[na], refs[na + 1], refs[na + 2]
        x, y, c = lax.axis_index("x"), lax.axis_index("y"), lax.axis_index("c")
        for k in (1, 2, 3):
            for i in range(ng):
                kind, shp = _cls(group[i])
                half = _half(_shard_view(arr[i], kind, shp, 0), shp, 0)
                pltpu.make_async_remote_copy(src_ref=half, dst_ref=half, send_sem=s1.at[i * 3 + k - 1], recv_sem=r2.at[i * 3 + k - 1],
                                             device_id=(x, y, 1 - c), device_id_type=MESH).wait_send()
                cp = pltpu.make_async_remote_copy(src_ref=half, dst_ref=half, send_sem=s2.at[i * 3 + k - 1], recv_sem=r2.at[i * 3 + k - 1],
                                                  device_id=(x, y, 1 - c), device_id_type=MESH)
                cp.wait_send()
                cp.wait_recv()

    return pl.pallas_call(
        body, name="gather_wait_" + tag,
        out_shape=tuple(pltpu.HBM(a.shape, a.dtype) for a in arrays),
        in_specs=[HBM_SPEC] * na + [SEM_SPEC, SEM_SPEC, SEM_SPEC, ANY_SPEC],
        out_specs=tuple([HBM_SPEC] * na),
        input_output_aliases={i: i for i in range(na)},
        compiler_params=pltpu.CompilerParams(has_side_effects=EFFECT),
    )(*arrays, send1, send2, recv2, after)


def _scatter_group_start(tag, group, pieces, after):
    ng = len(group)
    lands = [lax.empty((N_CHIPS - 1,) + _cls(q)[1], bf16) for q in group]
    return _xfer_start("scatter_start_" + tag, list(pieces) + lands, ng, after,
                       lambda arr, i, mc, pj: _shard_view(arr[i], *_cls(group[i]), pj),
                       lambda arr, i, mc, k: arr[ng + i].at[k - 1])


def _scatter_group_wait(tag, group, started, after):
    ng = len(group)
    out = _xfer_wait("scatter_wait_" + tag, started, ng, after, lambda arr, i: arr[ng + i].at[0])
    return out[:ng], out[ng:]


def _mod_shards(c_all, ada_w, ada_b_sh):
    tn = ADA_SH // 3

    def body(c_ref, w_ref, b_ref, o_ref, cs_ref):
        cv = c_ref[...]
        cs = cv * _sigmoid(cv)
        cs_ref[...] = cs
        o_ref[...] = _dot(cs.astype(bf16), w_ref[...].astype(bf16)) + b_ref[...]

    return pl.pallas_call(
        body, name="mod_shards", grid=(DEPTH, 3),
        out_shape=[SDS((DEPTH, 8, ADA_SH), f32), SDS((8, D), f32)],
        in_specs=[pl.BlockSpec((8, D), lambda l, t: (0, 0)),
                  pl.BlockSpec((None, D, tn), lambda l, t: (l, 0, t)),
                  pl.BlockSpec((None, 1, tn), lambda l, t: (l, 0, t))],
        out_specs=[pl.BlockSpec((None, 8, tn), lambda l, t: (l, 0, t)), pl.BlockSpec((8, D), lambda l, t: (0, 0))],
        compiler_params=_cp("arbitrary", "arbitrary"),
    )(c_all, ada_w, ada_b_sh.reshape(DEPTH, 1, ADA_SH))


def _t5_bucket(dist):
    exact = NUM_BUCKETS // 2
    dd = np.maximum(dist, 1).astype(np.float32)
    large = exact + (np.log(dd / exact) / np.log(MAX_DISTANCE / exact) * (NUM_BUCKETS - exact)).astype(np.int32)
    large = np.minimum(large, NUM_BUCKETS - 1)
    return np.where(dist < exact, dist, large).astype(np.int32)


def _bucket_table():
    i = np.arange(BLK)[:, None]
    j = np.arange(2 * BLK)[None, :]
    rel = i - j + BLK
    return np.stack([_t5_bucket(np.maximum(rel, 0) * d) for d in DILATIONS]).astype(np.int32)


def _band():
    rel = lax.broadcasted_iota(jnp.int32, (BLK, 2 * BLK), 0) - lax.broadcasted_iota(jnp.int32, (BLK, 2 * BLK), 1) + BLK
    return (rel >= 0) & (rel <= BLK)


def _bias_blocks(rel_bias, buckets):
    def body(tab_ref, bk_ref, o_ref):
        h = pl.program_id(0)
        bk = bk_ref[...]
        acc = jnp.zeros((BLK, 2 * BLK), f32)
        for b in range(NUM_BUCKETS):
            acc = jnp.where(bk == b, tab_ref[b, h], acc)
        o_ref[...] = jnp.where(_band(), acc, NEG)

    return pl.pallas_call(
        body, name="bias_blocks", grid=(24,),
        out_shape=SDS((24, BLK, 2 * BLK), f32),
        in_specs=[pl.BlockSpec(memory_space=pltpu.SMEM), pl.BlockSpec((None, BLK, 2 * BLK), lambda h: (h // 8, 0, 0))],
        out_specs=pl.BlockSpec((None, BLK, 2 * BLK), lambda h: (h, 0, 0)),
        compiler_params=_cp("arbitrary"),
    )(rel_bias, buckets)


def _bias_grad(dsaccs, buckets):
    nl = len(dsaccs)

    def body(*refs):
        bk = refs[nl][...]
        tot = refs[0][...]
        for r in refs[1:nl]:
            tot = tot + r[...]
        lane = lax.broadcasted_iota(jnp.int32, (1, 128), 1)
        row = jnp.zeros((1, 128), f32)
        for b in range(NUM_BUCKETS):
            row = jnp.where(lane == b, jnp.sum(jnp.where(bk == b, tot, 0.0)), row)
        refs[nl + 1][...] = row

    return pl.pallas_call(
        body, name="bias_grad", grid=(24,),
        out_shape=SDS((24, 1, 128), f32),
        in_specs=[pl.BlockSpec((None, BLK, 2 * BLK), lambda h: (h, 0, 0))] * nl
                 + [pl.BlockSpec((None, BLK, 2 * BLK), lambda h: (h // 8, 0, 0))],
        out_specs=pl.BlockSpec((None, 1, 128), lambda h: (h, 0, 0)),
        compiler_params=_cp("arbitrary"),
    )(*dsaccs, buckets)


def _ffn_fwd(x, mod9, g3, wg, wu, wd, sub):
    S = x.shape[0]

    def body(x_ref, mod_ref, g_ref, wg_ref, wu_ref, wd_ref, xo_ref, h_ref, ga_ref, sa_ref, hid_ref, y_ref, acc):
        j = pl.program_id(1)

        @pl.when(j == 0)
        def _():
            h, _, _ = _norm_fwd(x_ref[...], g_ref[sub:sub + 1, :], mod_ref[3 * sub:3 * sub + 1, :], mod_ref[3 * sub + 1:3 * sub + 2, :])
            h_ref[...] = h.astype(bf16)
            acc[...] = jnp.zeros_like(acc)

        h = h_ref[...]
        a = _dot(h, wg_ref[...])
        u = _dot(h, wu_ref[...])
        sg = _sigmoid(a)
        sil = a * sg
        ga_ref[...] = (u * (sg * (1.0 + a * (1.0 - sg)))).astype(bf16)
        sa_ref[...] = sil.astype(bf16)
        hid_ref[...] = (sil * u).astype(bf16)
        acc[...] += _dot(hid_ref[...], wd_ref[...])

        @pl.when(j == N_CHIPS - 1)
        def _():
            y = acc[...]
            y_ref[...] = y.astype(bf16)
            xo_ref[...] = x_ref[...] + 0.5 * mod_ref[3 * sub + 2:3 * sub + 3, :] * y

    row = pl.BlockSpec((TMF, D), lambda i, j: (i, 0))
    hidb = pl.BlockSpec((None, TMF, FB), lambda i, j: (j, i, 0))
    hids = SDS((N_CHIPS, S, FB), bf16)
    return pl.pallas_call(
        body, name="ffn_fwd", grid=(S // TMF, N_CHIPS),
        out_shape=[SDS((S, D), f32), SDS((S, D), bf16), hids, hids, hids, SDS((S, D), bf16)],
        in_specs=[row, pl.BlockSpec((9, D), lambda i, j: (0, 0)), pl.BlockSpec((3, D), lambda i, j: (0, 0)),
                  pl.BlockSpec((None, D, FB), lambda i, j: (j, 0, 0)), pl.BlockSpec((None, D, FB), lambda i, j: (j, 0, 0)),
                  pl.BlockSpec((FB, D), lambda i, j: (j, 0))],
        out_specs=[row, row, hidb, hidb, hidb, row],
        scratch_shapes=[pltpu.VMEM((TMF, D), f32)],
        compiler_params=_cp("arbitrary", "arbitrary"),
    )(x, mod9, g3, wg, wu, wd)


def _ffn_bwd1(dxo, x, mod9, g3, y, ga, sa, wg, wu, wd, sub):
    S = x.shape[0]

    def body(dxo_ref, x_ref, mod_ref, g_ref, y_ref, ga_ref, sa_ref, wg_ref, wu_ref, wd_ref,
             dxi_ref, da_ref, du_ref, dy_ref, sm_ref, acc):
        i, j = pl.program_id(0), pl.program_id(1)
        gate = mod_ref[3 * sub + 2:3 * sub + 3, :]

        @pl.when((i == 0) & (j == 0))
        def _():
            sm_ref[...] = jnp.zeros_like(sm_ref)

        @pl.when(j == 0)
        def _():
            dxo_v = dxo_ref[...]
            dy_ref[...] = (0.5 * gate * dxo_v).astype(bf16)
            sm_ref[2:3, :] += jnp.sum(0.5 * y_ref[...].astype(f32) * dxo_v, axis=0, keepdims=True)
            acc[...] = jnp.zeros_like(acc)

        part = None
        for s in range(SH_STEP):
            dhid = _dot_nt(dy_ref[...], wd_ref[s * FB:(s + 1) * FB, :])
            da = (dhid * ga_ref[s].astype(f32)).astype(bf16)
            du = (dhid * sa_ref[s].astype(f32)).astype(bf16)
            da_ref[s] = da
            du_ref[s] = du
            t = _dot_nt(da, wg_ref[s]) + _dot_nt(du, wu_ref[s])
            part = t if part is None else part + t
        acc[...] += part

        @pl.when(j == N_CHIPS // SH_STEP - 1)
        def _():
            g = g_ref[sub:sub + 1, :]
            scale = mod_ref[3 * sub + 1:3 * sub + 2, :]
            _, xhat, rstd = _norm_fwd(x_ref[...], g, mod_ref[3 * sub:3 * sub + 1, :], scale)
            dx, dshift, dscale, dg = _norm_bwd(acc[...], xhat, rstd, g, scale)
            dxi_ref[...] = dxo_ref[...] + dx
            sm_ref[0:1, :] += dshift
            sm_ref[1:2, :] += dscale
            sm_ref[3:4, :] += dg

    row = pl.BlockSpec((TM, D), lambda i, j: (i, 0))
    hidb = pl.BlockSpec((SH_STEP, TM, FB), lambda i, j: (j, i, 0))
    wcol = pl.BlockSpec((SH_STEP, D, FB), lambda i, j: (j, 0, 0))
    return pl.pallas_call(
        body, name="ffn_bwd1", grid=(S // TM, N_CHIPS // SH_STEP),
        out_shape=[SDS((S, D), f32), SDS((N_CHIPS, S, FB), bf16), SDS((N_CHIPS, S, FB), bf16), SDS((S, D), bf16), SDS((8, D), f32)],
        in_specs=[row, row, pl.BlockSpec((9, D), lambda i, j: (0, 0)), pl.BlockSpec((3, D), lambda i, j: (0, 0)), row,
                  hidb, hidb, wcol, wcol, pl.BlockSpec((SH_STEP * FB, D), lambda i, j: (j, 0))],
        out_specs=[row, hidb, hidb, row, pl.BlockSpec((8, D), lambda i, j: (0, 0))],
        scratch_shapes=[pltpu.VMEM((TM, D), f32)],
        compiler_params=_cp("arbitrary", "arbitrary"),
    )(dxo, x, mod9, g3, y, ga, sa, wg, wu, wd)


def _ffn_bwd2(h, da, du, hid, dy):
    S = h.shape[0]
    ni = S // TMW

    def body(h_ref, da_ref, du_ref, hid_ref, dy_ref, dwg_ref, dwu_ref, dwd_ref, ag, au, ad):
        i = pl.program_id(1)

        @pl.when(i == 0)
        def _():
            ag[...] = jnp.zeros_like(ag)
            au[...] = jnp.zeros_like(au)
            ad[...] = jnp.zeros_like(ad)

        hv = h_ref[...]
        ag[...] += _dot_tn(hv, da_ref[...])
        au[...] += _dot_tn(hv, du_ref[...])
        ad[...] += _dot_tn(hid_ref[...], dy_ref[...])

        @pl.when(i == ni - 1)
        def _():
            dwg_ref[...] = ag[...].astype(bf16)
            dwu_ref[...] = au[...].astype(bf16)
            dwd_ref[...] = ad[...].astype(bf16)

    row = pl.BlockSpec((TMW, D), lambda j, i: (i, 0))
    hidb = pl.BlockSpec((None, TMW, FB), lambda j, i: (j, i, 0))
    wcol = pl.BlockSpec((None, D, FB), lambda j, i: (j, 0, 0))
    return pl.pallas_call(
        body, name="ffn_bwd2", grid=(N_CHIPS, ni),
        out_shape=[SDS((N_CHIPS, D, FB), bf16), SDS((N_CHIPS, D, FB), bf16), SDS((N_CHIPS * FB, D), bf16)],
        in_specs=[row, hidb, hidb, hidb, row],
        out_specs=[wcol, wcol, pl.BlockSpec((FB, D), lambda j, i: (j, 0))],
        scratch_shapes=[pltpu.VMEM((D, FB), f32), pltpu.VMEM((D, FB), f32), pltpu.VMEM((FB, D), f32)],
        compiler_params=_cp("arbitrary", "arbitrary"),
    )(h, da, du, hid, dy)


def _mix_qkv(x, mod9, g3, win):
    S = x.shape[0]

    def body(x_ref, mod_ref, g_ref, w_ref, h_ref, o_ref):
        @pl.when(pl.program_id(1) == 0)
        def _():
            h, _, _ = _norm_fwd(x_ref[...], g_ref[1:2, :], mod_ref[3:4, :], mod_ref[4:5, :])
            h_ref[...] = h.astype(bf16)

        o_ref[...] = _dot(h_ref[...], w_ref[...]).astype(bf16)

    row = pl.BlockSpec((TMP, D), lambda i, j: (i, 0))
    return pl.pallas_call(
        body, name="mix_qkv", grid=(S // TMP, QKV_W // CB),
        out_shape=[SDS((S, D), bf16), SDS((S, QKV_W), bf16)],
        in_specs=[row, pl.BlockSpec((9, D), lambda i, j: (0, 0)), pl.BlockSpec((3, D), lambda i, j: (0, 0)),
                  pl.BlockSpec((D, CB), lambda i, j: (0, j))],
        out_specs=[row, pl.BlockSpec((TMP, CB), lambda i, j: (i, j))],
        compiler_params=_cp("arbitrary", "arbitrary"),
    )(x, mod9, g3, win)


def _mix_rest(h, win):
    S = h.shape[0]
    off = QKV_W // CB

    def body(h_ref, w_ref, o_ref):
        o_ref[...] = _dot(h_ref[...], w_ref[...]).astype(bf16)

    return pl.pallas_call(
        body, name="mix_rest", grid=(S // TMP, REST_W // CB),
        out_shape=SDS((S, REST_W), bf16),
        in_specs=[pl.BlockSpec((TMP, D), lambda i, j: (i, 0)), pl.BlockSpec((D, CB), lambda i, j: (0, off + j))],
        out_specs=pl.BlockSpec((TMP, CB), lambda i, j: (i, j)),
        compiler_params=_cp("arbitrary", "arbitrary"),
    )(h, win)


def _widen(srcs, dsts):
    for src, dst in zip(srcs, dsts):
        dst[...] = src[...].astype(f32)


def _qkv_scratch(R, Rb):
    return [pltpu.VMEM((R, 128), f32), pltpu.VMEM((R, 128), f32), pltpu.VMEM((Rb, 128), f32),
            pltpu.VMEM((R, 128), f32), pltpu.VMEM((Rb, 128), f32)]


def _attn_fwd(qkv, bias, g):
    S = qkv.shape[0]
    d = DILATIONS[g]
    nq = Q_BLOCKS[g]
    Rb = BLK * d
    R = Rb * nq
    nb = S // R
    qb, kb, vb = 4 * g, 12 + 4 * g, 24 + 4 * g

    def body(q_in, kc_in, kp_in, vc_in, vp_in, b_ref, o_ref, l_ref, q_ref, kc_ref, kp_ref, vc_ref, vp_ref):
        n = pl.program_id(1)
        col = lax.broadcasted_iota(jnp.int32, (BLK, 2 * BLK), 1)
        first = jnp.where((col < BLK) & (n == 0), NEG, 0.0)
        head0 = lax.broadcasted_iota(jnp.int32, (1, 2 * HD), 1) < HD
        _widen((q_in, kc_in, kp_in, vc_in, vp_in), (q_ref, kc_ref, kp_ref, vc_ref, vp_ref))

        def one(b, r):
            sl = pl.ds(b * Rb + r, BLK, stride=d)
            q = q_ref[sl, :]
            if b == 0:
                kp, vp = kp_ref[pl.ds(r, BLK, stride=d), :], vp_ref[pl.ds(r, BLK, stride=d), :]
            else:
                before = pl.ds((b - 1) * Rb + r, BLK, stride=d)
                kp, vp = kc_ref[before, :], vc_ref[before, :]
            kk = jnp.concatenate([kp, kc_ref[sl, :]], axis=0).astype(bf16)
            vv = jnp.concatenate([vp, vc_ref[sl, :]], axis=0).astype(bf16)
            os, ls = [], []
            for hh in range(2):
                qm = jnp.where(head0 if hh == 0 else ~head0, q, 0.0).astype(bf16)
                s = _dot_nt(qm, kk) * SCALE + b_ref[hh]
                if b == 0:
                    s = s + first
                m = jnp.max(s, axis=-1, keepdims=True)
                p = jnp.exp(s - m)
                l = jnp.sum(p, axis=-1, keepdims=True)
                os.append(_dot(p.astype(bf16), vv) / l)
                ls.append(m + jnp.log(l))
            o_ref[sl, :] = jnp.where(head0, os[0], os[1])
            l_ref[sl, :] = jnp.where(head0, ls[0], ls[1])

        for b in range(nq):
            if d == 1:
                one(b, 0)
            else:
                lax.fori_loop(0, d, lambda r, carry, b=b: (one(b, r), carry)[1], 0, unroll=4)

    def blk(cb, prev):
        if prev:
            return pl.BlockSpec((Rb, 128), lambda hp, n: (jnp.maximum(n * nq - 1, 0), cb + hp))
        return pl.BlockSpec((R, 128), lambda hp, n: (n, cb + hp))

    outb = pl.BlockSpec((R, 128), lambda hp, n: (n, hp))
    return pl.pallas_call(
        body, name=f"attn_fwd_d{d}", grid=(4, nb),
        out_shape=[SDS((S, 512), f32), SDS((S, 512), f32)],
        in_specs=[blk(qb, False), blk(kb, False), blk(kb, True), blk(vb, False), blk(vb, True),
                  pl.BlockSpec((2, BLK, 2 * BLK), lambda hp, n: (4 * g + hp, 0, 0))],
        out_specs=[outb, outb],
        scratch_shapes=_qkv_scratch(R, Rb),
        compiler_params=_cp("arbitrary", "arbitrary"),
    )(qkv, qkv, qkv, qkv, qkv, bias)


def _attn_bwd(qkv, do, o, lse, bias, dq_all, dk_all, dv_all, g):
    S = qkv.shape[0]
    d = DILATIONS[g]
    nq = Q_BLOCKS[g]
    Rb = BLK * d
    R = Rb * nq
    nb = S // R
    qb, kb, vb = 4 * g, 12 + 4 * g, 24 + 4 * g

    def body(q_in, kc_in, kp_in, vc_in, vp_in, do_ref, o_ref, l_ref, b_ref, dqi, dki, dvi,
             dq_out, dk_out, dv_out, ds_ref, ck, cv, tk, tv, dq_ref, q_ref, kc_ref, kp_ref, vc_ref, vp_ref):
        n = pl.program_id(1)
        col = lax.broadcasted_iota(jnp.int32, (BLK, 2 * BLK), 1)
        first = jnp.where((col < BLK) & (n == 0), NEG, 0.0)

        @pl.when(n == 0)
        def _():
            ck[...] = jnp.zeros_like(ck)
            cv[...] = jnp.zeros_like(cv)
            ds_ref[...] = jnp.zeros_like(ds_ref)

        @pl.when(n < nb)
        def _():
            head0 = lax.broadcasted_iota(jnp.int32, (1, 2 * HD), 1) < HD
            _widen((q_in, kc_in, kp_in, vc_in, vp_in), (q_ref, kc_ref, kp_ref, vc_ref, vp_ref))

            def one(b, r):
                sl = pl.ds(b * Rb + r, BLK, stride=d)
                before = pl.ds((max(b, 1) - 1) * Rb + r, BLK, stride=d)
                q = q_ref[sl, :]
                if b == 0:
                    kp, vp = kp_ref[pl.ds(r, BLK, stride=d), :], vp_ref[pl.ds(r, BLK, stride=d), :]
                else:
                    kp, vp = kc_ref[before, :], vc_ref[before, :]
                kk = jnp.concatenate([kp, kc_ref[sl, :]], axis=0).astype(bf16)
                vv = jnp.concatenate([vp, vc_ref[sl, :]], axis=0).astype(bf16)
                dov, lv = do_ref[sl, :], l_ref[sl, :]
                prod = dov * o_ref[sl, :]
                qb, dob = q.astype(bf16), dov.astype(bf16)
                dqs, dks, dvs = [], [], []
                for hh in range(2):
                    msk = head0 if hh == 0 else ~head0
                    qm = jnp.where(msk, q, 0.0).astype(bf16)
                    dom = jnp.where(msk, dov, 0.0).astype(bf16)
                    dsum = jnp.sum(jnp.where(msk, prod, 0.0), axis=-1, keepdims=True)
                    s = _dot_nt(qm, kk) * SCALE + b_ref[hh]
                    if b == 0:
                        s = s + first
                    p = jnp.exp(s - lv[:, HD * hh:HD * hh + 1])
                    ds = p * (_dot_nt(dom, vv) - dsum)
                    ds_ref[hh] += ds
                    dsb = ds.astype(bf16)
                    dqs.append(_dot(dsb, kk) * SCALE)
                    dks.append(_dot_tn(dsb, qb) * SCALE)
                    dvs.append(_dot_tn(p.astype(bf16), dob))
                dq_ref[sl, :] = jnp.where(head0, dqs[0], dqs[1])
                dk = jnp.where(head0, dks[0], dks[1])
                dv = jnp.where(head0, dvs[0], dvs[1])
                tk[sl, :] = dk[BLK:]
                tv[sl, :] = dv[BLK:]
                if b == 0:
                    prev_rows = pl.ds((nq - 1) * Rb + r, BLK, stride=d)
                    ck[prev_rows, :] += dk[:BLK]
                    cv[prev_rows, :] += dv[:BLK]
                else:
                    tk[before, :] += dk[:BLK]
                    tv[before, :] += dv[:BLK]

            for b in range(nq):
                if d == 1:
                    one(b, 0)
                else:
                    lax.fori_loop(0, d, lambda r, carry, b=b: (one(b, r), carry)[1], 0, unroll=4)
            dq_out[...] = dq_ref[...].astype(bf16)
            dk_out[...] = ck[...].astype(bf16)
            dv_out[...] = cv[...].astype(bf16)
            ck[...] = tk[...]
            cv[...] = tv[...]

        @pl.when(n == nb)
        def _():
            dk_out[...] = ck[...].astype(bf16)
            dv_out[...] = cv[...].astype(bf16)

    last = nb - 1

    def blk(cb, prev):
        if prev:
            return pl.BlockSpec((Rb, 128), lambda hp, n: (jnp.maximum(jnp.minimum(n, last) * nq - 1, 0), cb + hp))
        return pl.BlockSpec((R, 128), lambda hp, n: (jnp.minimum(n, last), cb + hp))

    cur = pl.BlockSpec((R, 128), lambda hp, n: (jnp.minimum(n, last), hp))
    anyspec = pl.BlockSpec(memory_space=pl.ANY)
    dqo = pl.BlockSpec((R, 128), lambda hp, n: (jnp.minimum(n, last), 4 * g + hp))
    dko = pl.BlockSpec((R, 128), lambda hp, n: (jnp.maximum(n - 1, 0), 4 * g + hp))
    return pl.pallas_call(
        body, name=f"attn_bwd_d{d}", grid=(4, nb + 1),
        out_shape=[SDS((S, 1536), bf16), SDS((S, 1536), bf16), SDS((S, 1536), bf16), SDS((8, BLK, 2 * BLK), f32)],
        in_specs=[blk(qb, False), blk(kb, False), blk(kb, True), blk(vb, False), blk(vb, True), cur, cur, cur,
                  pl.BlockSpec((2, BLK, 2 * BLK), lambda hp, n: (4 * g + hp, 0, 0)), anyspec, anyspec, anyspec],
        out_specs=[dqo, dko, dko, pl.BlockSpec((2, BLK, 2 * BLK), lambda hp, n: (hp, 0, 0))],
        scratch_shapes=[pltpu.VMEM((R, 128), f32)] * 5 + _qkv_scratch(R, Rb),
        input_output_aliases={9: 0, 10: 1, 11: 2},
        compiler_params=_cp("arbitrary", "arbitrary"),
    )(qkv, qkv, qkv, qkv, qkv, do, o, lse, bias, dq_all, dk_all, dv_all)


def _conv_z(cc, ch, hc, hh, cw_ref, first):
    halo = jnp.where(first, 0.0, hc.astype(f32) * hh.astype(f32))
    T = jnp.concatenate([halo, cc * ch], axis=0)
    z = cw_ref[2:3, :] * T + cw_ref[1:2, :] * pltpu.roll(T, 1, 0) + cw_ref[0:1, :] * pltpu.roll(T, 2, 0)
    return T, z[HALO:]


def _rest_specs(tm, with_next):
    per = tm // HALO
    specs = [pl.BlockSpec((tm, D), functools.partial(lambda i, k: (i, k), k=k)) for k in range(5)]
    specs += [pl.BlockSpec((HALO, D), functools.partial(lambda i, k: (jnp.maximum(i * per - 1, 0), k), k=k)) for k in (1, 2)]
    return specs


def _mix_out_fwd(x, mod9, rest, ogs, lgs, cw, wco, wao, wo):
    S = x.shape[0]
    tm = TMX

    def body(x_ref, mod_ref, cb_ref, cc_ref, ch_ref, gc_ref, ga_ref, hc_ref, hh_ref,
             o0, o1, o2, l0, l1, l2, cw_ref, wco_ref, wao_ref, wo_ref,
             xo_ref, o_ref, lse_ref, yc_ref, ya_ref, out_ref):
        i = pl.program_id(0)
        lv = [l0[...], l1[...], l2[...]]
        mx = jnp.maximum(jnp.maximum(lv[0], lv[1]), lv[2])
        es = [jnp.exp(l - mx) for l in lv]
        den = es[0] + es[1] + es[2]
        o = (es[0] / den) * o0[...] + (es[1] / den) * o1[...] + (es[2] / den) * o2[...]
        o_ref[...] = o
        lse_ref[...] = mx + jnp.log(den)
        _, z = _conv_z(cc_ref[...].astype(f32), ch_ref[...].astype(f32), hc_ref[...], hh_ref[...], cw_ref, i == 0)
        p = (cb_ref[...].astype(f32) * z).astype(bf16)
        yc = _dot(p, wco_ref[...])
        ya = _dot(o.astype(bf16), wao_ref[...])
        yc_ref[...] = yc.astype(bf16)
        ya_ref[...] = ya.astype(bf16)
        merged = _sigmoid(gc_ref[...].astype(f32)) * yc + _sigmoid(ga_ref[...].astype(f32)) * ya
        out = _dot(merged.astype(bf16), wo_ref[...])
        out_ref[...] = out.astype(bf16)
        xo_ref[...] = x_ref[...] + mod_ref[5:6, :] * out

    row = pl.BlockSpec((tm, D), lambda i: (i, 0))
    att = pl.BlockSpec((tm, 512), lambda i: (i, 0))
    full = lambda shp: pl.BlockSpec(shp, lambda i: (0, 0))
    return pl.pallas_call(
        body, name="mix_out_fwd", grid=(S // tm,),
        out_shape=[SDS((S, D), f32), SDS((S, 512), f32), SDS((S, 512), f32), SDS((S, D), bf16), SDS((S, D), bf16), SDS((S, D), bf16)],
        in_specs=[row, full((9, D))] + _rest_specs(tm, False) + [att] * 6 + [full((3, D)), full((D, D)), full((512, D)), full((D, D))],
        out_specs=[row, att, att, row, row, row],
        compiler_params=_cp("arbitrary"),
    )(x, mod9, *([rest] * 7), *ogs, *lgs, cw, wco, wao, wo)


def _mix_out_bwd(dxo, mod9, outv, yc, ya, rest, o, cw, wco, wao, wo):
    S = dxo.shape[0]
    tm = TMX
    ni = S // tm

    def body(dxo_ref, mod_ref, out_ref, yc_ref, ya_ref, cb_ref, cc_ref, ch_ref, gc_ref, ga_ref, hc_ref, hh_ref,
             o_ref, cw_ref, wco_ref, wao_ref, wo_ref,
             dp_ref, dg2_ref, do_ref, dwco_ref, dwao_ref, dwo_ref, sm_ref, aco, aao, ao):
        i = pl.program_id(0)

        @pl.when(i == 0)
        def _():
            sm_ref[...] = jnp.zeros_like(sm_ref)
            aco[...] = jnp.zeros_like(aco)
            aao[...] = jnp.zeros_like(aao)
            ao[...] = jnp.zeros_like(ao)

        dxo_v = dxo_ref[...]
        sm_ref[2:3, :] += jnp.sum(out_ref[...].astype(f32) * dxo_v, axis=0, keepdims=True)
        dout = (mod_ref[5:6, :] * dxo_v).astype(bf16)
        dmerged = _dot_nt(dout, wo_ref[...])
        sc, sa = _sigmoid(gc_ref[...].astype(f32)), _sigmoid(ga_ref[...].astype(f32))
        ycv, yav = yc_ref[...].astype(f32), ya_ref[...].astype(f32)
        ao[...] += _dot_tn((sc * ycv + sa * yav).astype(bf16), dout)
        dyc = (dmerged * sc).astype(bf16)
        dya = (dmerged * sa).astype(bf16)
        dg2_ref[:, :D] = (dmerged * ycv * sc * (1.0 - sc)).astype(bf16)
        dg2_ref[:, D:] = (dmerged * yav * sa * (1.0 - sa)).astype(bf16)
        dp_ref[...] = _dot_nt(dyc, wco_ref[...]).astype(bf16)
        _, z = _conv_z(cc_ref[...].astype(f32), ch_ref[...].astype(f32), hc_ref[...], hh_ref[...], cw_ref, i == 0)
        aco[...] += _dot_tn((cb_ref[...].astype(f32) * z).astype(bf16), dyc)
        do_ref[...] = _dot_nt(dya, wao_ref[...])
        aao[...] += _dot_tn(o_ref[...].astype(bf16), dya)

        @pl.when(i == ni - 1)
        def _():
            dwco_ref[...] = aco[...].astype(bf16)
            dwao_ref[...] = aao[...].astype(bf16)
            dwo_ref[...] = ao[...].astype(bf16)

    row = pl.BlockSpec((tm, D), lambda i: (i, 0))
    att = pl.BlockSpec((tm, 512), lambda i: (i, 0))
    full = lambda shp: pl.BlockSpec(shp, lambda i: (0, 0))
    return pl.pallas_call(
        body, name="mix_out_bwd", grid=(ni,),
        out_shape=[SDS((S, D), bf16), SDS((S, 2 * D), bf16), SDS((S, 512), f32),
                   SDS((D, D), bf16), SDS((512, D), bf16), SDS((D, D), bf16), SDS((8, D), f32)],
        in_specs=[row, full((9, D)), row, row, row] + _rest_specs(tm, False) + [att, full((3, D)), full((D, D)), full((512, D)), full((D, D))],
        out_specs=[row, pl.BlockSpec((tm, 2 * D), lambda i: (i, 0)), att, full((D, D)), full((512, D)), full((D, D)), full((8, D))],
        scratch_shapes=[pltpu.VMEM((D, D), f32), pltpu.VMEM((512, D), f32), pltpu.VMEM((D, D), f32)],
        compiler_params=_cp("arbitrary"),
    )(dxo, mod9, outv, yc, ya, *([rest] * 7), o, cw, wco, wao, wo)


def _conv_bwd(dp, rest, cw):
    S = dp.shape[0]
    tm = TM
    per = tm // HALO
    nh = S // HALO
    ni = S // tm

    def body(dp_ref, dpn_ref, cb_ref, cbn_ref, cc_ref, ch_ref, hc_ref, hh_ref, cw_ref, d3_ref, sm_ref):
        i = pl.program_id(0)

        @pl.when(i == 0)
        def _():
            sm_ref[...] = jnp.zeros_like(sm_ref)

        cc, ch = cc_ref[...].astype(f32), ch_ref[...].astype(f32)
        T, z = _conv_z(cc, ch, hc_ref[...], hh_ref[...], cw_ref, i == 0)
        dpv = dp_ref[...].astype(f32)
        cbv = cb_ref[...].astype(f32)
        dz = dpv * cbv
        dzn = jnp.where(i == ni - 1, 0.0, dpn_ref[...].astype(f32) * cbn_ref[...].astype(f32))
        E = jnp.concatenate([dz, dzn], axis=0)
        ne = tm + HALO
        dT = cw_ref[2:3, :] * E + cw_ref[1:2, :] * pltpu.roll(E, ne - 1, 0) + cw_ref[0:1, :] * pltpu.roll(E, ne - 2, 0)
        dT = dT[:tm]
        d3_ref[:, :D] = (dpv * z).astype(bf16)
        d3_ref[:, D:2 * D] = (dT * ch).astype(bf16)
        d3_ref[:, 2 * D:] = (dT * cc).astype(bf16)
        sm_ref[2:3, :] += jnp.sum(dz * T[HALO:], axis=0, keepdims=True)
        sm_ref[1:2, :] += jnp.sum(dz * pltpu.roll(T, 1, 0)[HALO:], axis=0, keepdims=True)
        sm_ref[0:1, :] += jnp.sum(dz * pltpu.roll(T, 2, 0)[HALO:], axis=0, keepdims=True)

    row = pl.BlockSpec((tm, D), lambda i: (i, 0))
    nxt = pl.BlockSpec((HALO, D), lambda i: (jnp.minimum((i + 1) * per, nh - 1), 0))
    col = lambda k: pl.BlockSpec((tm, D), lambda i: (i, k))
    prv = lambda k: pl.BlockSpec((HALO, D), lambda i: (jnp.maximum(i * per - 1, 0), k))
    return pl.pallas_call(
        body, name="conv_bwd", grid=(ni,),
        out_shape=[SDS((S, 3 * D), bf16), SDS((8, D), f32)],
        in_specs=[row, nxt, col(0), nxt, col(1), col(2), prv(1), prv(2), pl.BlockSpec((3, D), lambda i: (0, 0))],
        out_specs=[pl.BlockSpec((tm, 3 * D), lambda i: (i, 0)), pl.BlockSpec((8, D), lambda i: (0, 0))],
        compiler_params=_cp("arbitrary"),
    )(dp, dp, rest, rest, rest, rest, rest, rest, cw)


_DU_RANGES = ((0, 3), (3, 6), (6, 9), (9, 15), (15, 19))
N_CBLK = IN_W // CB


def _mix_in_bwd_dh(dxo, x, mod9, g3, dus, win):
    S = x.shape[0]

    def body(dxo_ref, x_ref, mod_ref, g_ref, s0, s1, s2, s3, s4, w_ref, dxi_ref, sm_ref, acc):
        i, kb = pl.program_id(0), pl.program_id(1)

        @pl.when((i == 0) & (kb == 0))
        def _():
            sm_ref[...] = jnp.zeros_like(sm_ref)

        @pl.when(kb == 0)
        def _():
            acc[...] = jnp.zeros_like(acc)

        for src, (lo, hi) in zip((s0, s1, s2, s3, s4), _DU_RANGES):
            @pl.when((kb >= lo) & (kb < hi))
            def _(src=src):
                acc[...] += _dot_nt(src[...].astype(bf16), w_ref[...])

        @pl.when(kb == N_CBLK - 1)
        def _():
            g, scale = g_ref[1:2, :], mod_ref[4:5, :]
            _, xhat, rstd = _norm_fwd(x_ref[...], g, mod_ref[3:4, :], scale)
            dx, dshift, dscale, dg = _norm_bwd(acc[...], xhat, rstd, g, scale)
            dxi_ref[...] = dxo_ref[...] + dx
            sm_ref[0:1, :] += dshift
            sm_ref[1:2, :] += dscale
            sm_ref[3:4, :] += dg

    row = pl.BlockSpec((TMP, D), lambda i, kb: (i, 0))

    def src_spec(lo, hi):
        return pl.BlockSpec((TMP, CB), lambda i, kb: (i, jnp.clip(kb - lo, 0, hi - lo - 1)))

    return pl.pallas_call(
        body, name="mix_in_bwd_dh", grid=(S // TMP, N_CBLK),
        out_shape=[SDS((S, D), f32), SDS((8, D), f32)],
        in_specs=[row, row, pl.BlockSpec((9, D), lambda i, kb: (0, 0)), pl.BlockSpec((3, D), lambda i, kb: (0, 0))]
                 + [src_spec(lo, hi) for lo, hi in _DU_RANGES] + [pl.BlockSpec((D, CB), lambda i, kb: (0, kb))],
        out_specs=[row, pl.BlockSpec((8, D), lambda i, kb: (0, 0))],
        scratch_shapes=[pltpu.VMEM((TMP, D), f32)],
        compiler_params=_cp("arbitrary", "arbitrary"),
    )(dxo, x, mod9, g3, *dus, win)


def _mix_in_bwd_dw(h, dus):
    S = h.shape[0]
    ni = S // TMW

    def body(h_ref, s0, s1, s2, s3, s4, dw_ref, acc):
        kb, i = pl.program_id(0), pl.program_id(1)

        @pl.when(i == 0)
        def _():
            acc[...] = jnp.zeros_like(acc)

        for src, (lo, hi) in zip((s0, s1, s2, s3, s4), _DU_RANGES):
            @pl.when((kb >= lo) & (kb < hi))
            def _(src=src):
                rows = pl.ds(pl.multiple_of(i * TMW, TMW), TMW)
                acc[...] += _dot_tn(h_ref[rows, :], src[...].astype(bf16))

        @pl.when(i == ni - 1)
        def _():
            dw_ref[...] = acc[...].astype(bf16)

    def src_spec(lo, hi):
        def imap(kb, i):
            on = (kb >= lo) & (kb < hi)
            return (jnp.where(on, i, 0), jnp.clip(kb - lo, 0, hi - lo - 1))
        return pl.BlockSpec((TMW, CB), imap)

    return pl.pallas_call(
        body, name="mix_in_bwd_dw", grid=(N_CBLK, ni),
        out_shape=SDS((D, IN_W), bf16),
        in_specs=[pl.BlockSpec((S, D), lambda kb, i: (0, 0))] + [src_spec(lo, hi) for lo, hi in _DU_RANGES],
        out_specs=pl.BlockSpec((D, CB), lambda kb, i: (0, kb)),
        scratch_shapes=[pltpu.VMEM((D, CB), f32)],
        compiler_params=_cp("arbitrary", "arbitrary"),
    )(h, *dus)


def _loss_head(x, fg, tgt):
    S = x.shape[0]

    def body(x_ref, g_ref, t_ref, ls_ref, dx_ref, sm_ref):
        i = pl.program_id(0)

        @pl.when(i == 0)
        def _():
            ls_ref[...] = jnp.zeros_like(ls_ref)
            sm_ref[...] = jnp.zeros_like(sm_ref)

        xv, g = x_ref[...], g_ref[...]
        rstd = lax.rsqrt(jnp.mean(xv * xv, axis=-1, keepdims=True) + EPS)
        xhat = xv * rstd
        e = xhat * g - t_ref[...]
        ls_ref[...] += 0.5 * jnp.sum(jnp.mean(e * e, axis=-1, keepdims=True))
        dy = e * (1.0 / D)
        sm_ref[0:1, :] += jnp.sum(dy * xhat, axis=0, keepdims=True)
        dxh = dy * g
        dx_ref[...] = rstd * (dxh - xhat * jnp.mean(dxh * xhat, axis=-1, keepdims=True))

    row = pl.BlockSpec((TM, D), lambda i: (i, 0))
    return pl.pallas_call(
        body, name="loss_head", grid=(S // TM,),
        out_shape=[SDS((8, 128), f32), SDS((S, D), f32), SDS((8, D), f32)],
        in_specs=[row, pl.BlockSpec((1, D), lambda i: (0, 0)), row],
        out_specs=[pl.BlockSpec((8, 128), lambda i: (0, 0)), row, pl.BlockSpec((8, D), lambda i: (0, 0))],
        compiler_params=_cp("arbitrary"),
    )(x, fg, tgt)


def _adam(w, g, m, v):
    m2 = B1 * m + (1.0 - B1) * g
    v2 = B2 * v + (1.0 - B2) * (g * g)
    delta = -LR * ((m2 / BC1) / (jnp.sqrt(v2 / BC2) + AEPS) + WD * w)
    return delta, m2, v2


def _row_tile(rows, cols):
    for tr in (512, 352, 256, 128, 64):
        if rows % tr == 0 and tr * cols * 4 <= (5 << 18):
            return tr
    raise ValueError((rows, cols))


def _sum_slots(land):
    _, R, C = land.shape
    tr = _row_tile(R, C)

    def body(l_ref, t_ref):
        t = l_ref[0].astype(f32)
        for k in range(1, N_CHIPS):
            t = t + l_ref[k].astype(f32)
        t_ref[...] = t

    return pl.pallas_call(
        body, name="sum_slots", grid=(R // tr,),
        out_shape=SDS((R, C), f32),
        in_specs=[pl.BlockSpec((N_CHIPS, tr, C), lambda i: (0, i, 0))],
        out_specs=pl.BlockSpec((tr, C), lambda i: (i, 0)),
        compiler_params=_cp("arbitrary"),
    )(land)


def _adamw_pair(w2, m2, v2, ta, tb, outs, slot):
    R, C = ta.shape
    tr = _row_tile(R, C)
    nrt = R // tr

    def body(w_ref, m_ref, v_ref, ta_ref, tb_ref, g_in, d_in, m_in, v_in, g_ref, d_ref, mo_ref, vo_ref):
        g = ta_ref[...].astype(f32) + tb_ref[...].astype(f32)
        delta, mn, vn = _adam(w_ref[...], g, m_ref[...], v_ref[...])
        g_ref[...] = g
        d_ref[...] = delta
        mo_ref[...] = mn
        vo_ref[...] = vn

    big = pl.BlockSpec((tr, C), lambda i: (slot * nrt + i, 0))
    loc = pl.BlockSpec((tr, C), lambda i: (i, 0))
    anyspec = pl.BlockSpec(memory_space=pl.ANY)
    return pl.pallas_call(
        body, name="adamw_pair", grid=(nrt,),
        out_shape=[SDS(o.shape, f32) for o in outs],
        in_specs=[big, big, big, loc, loc] + [anyspec] * 4,
        out_specs=[big] * 4,
        input_output_aliases={5: 0, 6: 1, 7: 2, 8: 3},
        compiler_params=_cp("arbitrary"),
    )(w2, m2, v2, ta, tb, *outs)


def _adamw_small(w, g, m, v):
    def body(w_ref, g_ref, m_ref, v_ref, d_ref, mo_ref, vo_ref):
        delta, mn, vn = _adam(w_ref[...], g_ref[...], m_ref[...], v_ref[...])
        d_ref[...] = delta
        mo_ref[...] = mn
        vo_ref[...] = vn

    return pl.pallas_call(body, name="adamw_small", out_shape=[SDS(w.shape, f32)] * 3)(w, g, m, v)


def _ada_w_update(cs_all, dmod_sh, w, m, v):
    tr = 256

    def body(cs_ref, dm_ref, w_ref, m_ref, v_ref, g_ref, d_ref, mo_ref, vo_ref):
        g = _dot_tn(cs_ref[...].astype(bf16), dm_ref[...].astype(bf16))
        delta, mn, vn = _adam(w_ref[...], g, m_ref[...], v_ref[...])
        g_ref[...] = g
        d_ref[...] = delta
        mo_ref[...] = mn
        vo_ref[...] = vn

    blk = pl.BlockSpec((None, tr, ADA_SH), lambda l, i: (l, i, 0))
    return pl.pallas_call(
        body, name="ada_w_update", grid=(DEPTH, D // tr),
        out_shape=[SDS(w.shape, f32)] * 4,
        in_specs=[pl.BlockSpec((8, tr), lambda l, i: (0, i)), pl.BlockSpec((None, 8, ADA_SH), lambda l, i: (l, 0, 0)), blk, blk, blk],
        out_specs=[blk] * 4,
        compiler_params=_cp("arbitrary", "arbitrary"),
    )(cs_all, dmod_sh, w, m, v)


def _sum_devices(gathered):
    _, R, C = gathered.shape

    def body(g_ref, o_ref):
        t = g_ref[0]
        for k in range(1, 8):
            t = t + g_ref[k]
        o_ref[...] = t

    return pl.pallas_call(body, name="sum_devices", out_shape=SDS((R, C), f32))(gathered)


def _layer_fwd(x, mod9, g3, cw, getw, bias):
    W = {}

    def take(gname, after, mod9):
        w, tok = getw(gname, after)
        W.update(w)
        return mod9 if tok is None else mod9 + tok[0, 0]

    mod9 = take("A", x, mod9)
    x1, h1, a1, u1, hid1, y1 = _ffn_fwd(x, mod9, g3, W["wg0"], W["wu0"], W["wd0"], 0)
    mod9 = take("B", x1, mod9)
    hm, qkv = _mix_qkv(x1, mod9, g3, W["win"])
    rest = _mix_rest(hm, W["win"])
    ogs, lgs = [], []
    for g in range(3):
        og, lg = _attn_fwd(qkv, bias, g)
        ogs.append(og)
        lgs.append(lg)
    mod9 = take("C", ogs[2], mod9)
    x2, o, lse, yc, ya, outv = _mix_out_fwd(x1, mod9, rest, ogs, lgs, cw, W["wco"], W["wao"], W["wo"])
    mod9 = take("D", x2, mod9)
    x3, h3, a3, u3, hid3, y3 = _ffn_fwd(x2, mod9, g3, W["wg1"], W["wu1"], W["wd1"], 2)
    saved = dict(x0=x, x1=x1, x2=x2, h1=h1, a1=a1, u1=u1, hid1=hid1, y1=y1, hm=hm, qkv=qkv, rest=rest, o=o, lse=lse, yc=yc, ya=ya,
                 outv=outv, h3=h3, a3=a3, u3=u3, hid3=hid3, y3=y3)
    return x3, saved, W


def _layer_bwd(dx, sv, mod9, g3, cw, W, bias, emit):
    S = dx.shape[0]
    dw = {}

    def send(gname, mod9):
        tok = emit(gname, dw)
        return mod9 if tok is None else mod9 + tok[0, 0]

    dx2, da, du, dy, sm3 = _ffn_bwd1(dx, sv["x2"], mod9, g3, sv["y3"], sv["a3"], sv["u3"], W["wg1"], W["wu1"], W["wd1"], 2)
    dw["wg1"], dw["wu1"], dw["wd1"] = _ffn_bwd2(sv["h3"], da, du, sv["hid3"], dy)
    mod9 = send("D", mod9)
    dp, dg2, do, dw["wco"], dw["wao"], dw["wo"], smo = _mix_out_bwd(
        dx2, mod9, sv["outv"], sv["yc"], sv["ya"], sv["rest"], sv["o"], cw, W["wco"], W["wao"], W["wo"])
    mod9_c = send("C", mod9)
    cw = cw + (mod9_c - mod9)[0:1, :]
    mod9 = mod9_c
    d3, smc = _conv_bwd(dp, sv["rest"], cw)
    dq = lax.empty((S, 1536), bf16)
    dk = lax.empty((S, 1536), bf16)
    dv = lax.empty((S, 1536), bf16)
    dsaccs = []
    for g in range(3):
        dq, dk, dv, dsg = _attn_bwd(sv["qkv"], do, sv["o"], sv["lse"], bias, dq, dk, dv, g)
        dsaccs.append(dsg)
    dus = (dq, dk, dv, d3, dg2)
    dx1, smm = _mix_in_bwd_dh(dx2, sv["x1"], mod9, g3, dus, W["win"])
    dw["win"] = _mix_in_bwd_dw(sv["hm"], dus)
    mod9 = send("B", mod9)
    dx0, da, du, dy, sm1 = _ffn_bwd1(dx1, sv["x0"], mod9, g3, sv["y1"], sv["a1"], sv["u1"], W["wg0"], W["wu0"], W["wd0"], 0)
    dw["wg0"], dw["wu0"], dw["wd0"] = _ffn_bwd2(sv["h1"], da, du, sv["hid1"], dy)
    send("A", mod9)
    dmod = jnp.concatenate([sm1[0:3], smm[0:2], smo[2:3], sm3[0:3]], axis=0)
    dng = jnp.concatenate([sm1[3:4], smm[3:4], sm3[3:4]], axis=0)
    return dx0, dmod, dng, smc[0:3], jnp.concatenate(dsaccs, axis=0)


def _chip_cols(a, chip, width):
    return lax.dynamic_slice_in_dim(a, chip * width, width, axis=a.ndim - 1)


def kernel(x, c, ada_w, ada_b, norm_g, ffn_w_gate, ffn_w_up, ffn_w_down, w_in, conv_w, w_conv_out, w_attn_out, w_o, rel_bias, final_g, loss_target, m_ada_w, m_ada_b, m_norm_g, m_ffn_w_gate, m_ffn_w_up, m_ffn_w_down, m_w_in, m_conv_w, m_w_conv_out, m_w_attn_out, m_w_o, m_rel_bias, m_final_g, v_ada_w, v_ada_b, v_norm_g, v_ffn_w_gate, v_ffn_w_up, v_ffn_w_down, v_w_in, v_conv_w, v_w_conv_out, v_w_attn_out, v_w_o, v_rel_bias, v_final_g):
    ix, iy, ic = lax.axis_index("x"), lax.axis_index("y"), lax.axis_index("c")
    chip = 2 * ix + iy
    dev = 4 * ix + 2 * iy + ic
    xs = x.reshape(x.shape[1:])
    S = xs.shape[0]
    qd = D // N_CHIPS

    chip_arr = jnp.reshape(chip, (1,)).astype(jnp.int32)
    names = [w[0] for w in WCLASSES]

    def layer_shards(l):
        return [(ffn_w_gate, (l, 0)), (ffn_w_up, (l, 0)), (ffn_w_down, (l, 0)), (ffn_w_gate, (l, 1)), (ffn_w_up, (l, 1)),
                (ffn_w_down, (l, 1)), (w_in, (l,)), (w_conv_out, (l,)), (w_attn_out, (l,)), (w_o, (l,))]

    started = {}
    extra_starts = {(0, "A"): [(0, "B")], (0, "B"): [(0, "C"), (0, "D"), (1, "A")]}

    casts = {}

    def cast_group(l, gname, after):
        shards = layer_shards(l)
        casts[(l, gname)] = _gather_group_cast(GROUPS[gname], [shards[q] for q in GROUPS[gname]], chip_arr, after)

    def start_gather(l, gname, after):
        started[(l, gname)] = _gather_group_start(f"l{l}{gname}", GROUPS[gname], casts[(l, gname)], after)
        return started[(l, gname)][-1]

    pad8 = lambda a: jnp.pad(a, ((0, -a.shape[0] % 8), (0, 0)))
    pack = jnp.concatenate([pad8(c), pad8(norm_g.reshape(3, D)), pad8(conv_w.reshape(3, D))], axis=0)
    g1 = _allgather_small(pack).reshape(8, 24, D)
    c_all = g1[:, 0]
    by_chip = g1[0::2]
    ng_full = jnp.concatenate([by_chip[j, 8:11].reshape(DEPTH, 3, qd) for j in range(N_CHIPS)], axis=-1)
    cw_full = jnp.concatenate([by_chip[j, 16:19].reshape(DEPTH, 3, qd) for j in range(N_CHIPS)], axis=-1)
    mod_sh, cs_all = _mod_shards(c_all, ada_w, _chip_cols(ada_b, chip, ADA_SH))
    g2 = _allgather_small(mod_sh.reshape(DEPTH * 8, ADA_SH)).reshape(8, DEPTH, 8, ADA_SH)
    mine = lax.dynamic_index_in_dim(g2[0::2], dev, axis=2, keepdims=False)
    mod = jnp.transpose(mine, (1, 0, 2)).reshape(DEPTH, 9, D)

    cast_group(0, "A", c)
    tok0 = start_gather(0, "A", mod)
    for l in range(DEPTH):
        for gname in GROUPS:
            if (l, gname) not in casts:
                cast_group(l, gname, tok0)
    buckets = jnp.asarray(_bucket_table())
    bias = _bias_blocks(rel_bias, buckets)
    last_cast = casts[(DEPTH - 1, "D")][-1]

    need_order = [(l, gname) for l in range(DEPTH) for gname in GROUPS]
    forwarded = {}

    def forward_gather(key, after):
        forwarded[key] = _gather_group_forward(f"l{key[0]}{key[1]}", GROUPS[key[1]], started[key], after)
        return forwarded[key][-1]

    def make_getw(l):
        def getw(gname, after):
            key = (l, gname)
            if key == (0, "A"):
                after = last_cast
            if key not in forwarded:
                after = forward_gather(key, after)
            full = _gather_group_wait(f"l{l}{gname}", GROUPS[gname], started[key][0], forwarded[key], after)
            tok = None
            before = set(started)
            for nl, ng in extra_starts.get(key, []) + [(l + 1, gname)]:
                if nl < DEPTH and (nl, ng) not in started:
                    tok = start_gather(nl, ng, full[0] if tok is None else tok)
            at = need_order.index(key) + 1
            if at < len(need_order) and need_order[at] in before and need_order[at] not in forwarded:
                tok = forward_gather(need_order[at], full[0] if tok is None else tok)
            return {names[q]: f for q, f in zip(GROUPS[gname], full)}, tok
        return getw

    Ws, saves = [], []
    xc = xs
    for l in range(DEPTH):
        xc, sv, W = _layer_fwd(xc, mod[l], ng_full[l], cw_full[l], make_getw(l), bias)
        Ws.append(W)
        saves.append(sv)

    ls, dx, smf = _loss_head(xc, final_g.reshape(1, D), loss_target.reshape(loss_target.shape[1:]))
    loss = lax.psum(ls[0, 0], ("x", "y", "c"))

    params = dict(wg=ffn_w_gate, wu=ffn_w_up, wd=ffn_w_down, win=w_in, wco=w_conv_out, wao=w_attn_out, wo=w_o)
    moms = dict(wg=m_ffn_w_gate, wu=m_ffn_w_up, wd=m_ffn_w_down, win=m_w_in, wco=m_w_conv_out, wao=m_w_attn_out, wo=m_w_o)
    vars_ = dict(wg=v_ffn_w_gate, wu=v_ffn_w_up, wd=v_ffn_w_down, win=v_w_in, wco=v_w_conv_out, wao=v_w_attn_out, wo=v_w_o)
    flat = lambda a: a.reshape(-1, a.shape[-1])
    big_out = {k: [lax.empty(flat(p).shape, f32) for _ in range(4)] for k, p in params.items()}
    dmods, dngs, dcws, dsaccs = [None] * DEPTH, [None] * DEPTH, [None] * DEPTH, [None] * DEPTH

    def finish(l, gname, started, after):
        group = GROUPS[gname]
        pieces, lands = _scatter_group_wait(f"l{l}{gname}", group, started, after)
        ts = [_sum_own_slots(pieces[i], lands[i], *_cls(q), chip_arr) for i, q in enumerate(group)]
        tsib = _swap_sibling(ts)
        for i, q in enumerate(group):
            name = names[q]
            key = name.rstrip("01")
            slot = 2 * l + int(name[-1]) if name[-1] in "01" else l
            big_out[key] = _adamw_pair(flat(params[key]), flat(moms[key]), flat(vars_[key]), ts[i], tsib[i], big_out[key], slot)

    pending, tok = [], None
    for l in reversed(range(DEPTH)):
        modl = mod[l] if tok is None else mod[l] + tok[0, 0]
        mine = []

        def emit(gname, dw, l=l, mine=mine):
            prev = mine[-1][2][-1] if mine else dx
            mine.append((l, gname, _scatter_group_start(f"l{l}{gname}", GROUPS[gname], [dw[names[q]] for q in GROUPS[gname]], prev)))
            return mine[-1][2][-1]

        dx, dmods[l], dngs[l], dcws[l], dsaccs[l] = _layer_bwd(dx, saves[l], modl, ng_full[l], cw_full[l], Ws[l], bias, emit)
        for pl_, pg, pst in pending:
            finish(pl_, pg, pst, dx)
        pending, tok = mine, mine[-1][2][-1]
    for pl_, pg, pst in pending[:-1]:
        finish(pl_, pg, pst, pending[-1][2][-1])

    drb = jnp.transpose(_bias_grad(dsaccs, buckets)[:, 0, :NUM_BUCKETS])
    drb_row = jnp.pad(drb.reshape(1, NUM_BUCKETS * 24), ((0, 0), (0, D - NUM_BUCKETS * 24)))
    pack2 = jnp.concatenate([pad8(a) for a in dmods] + [pad8(a) for a in dngs] + [pad8(a) for a in dcws] + [smf, pad8(drb_row)], axis=0)
    n_rows = pack2.shape[0]
    g3 = _allgather_small(pack2).reshape(8, n_rows, D)
    tot = _sum_devices(g3)
    o_ng, o_cw, o_fg, o_rb = 16 * DEPTH, 24 * DEPTH, 32 * DEPTH, 32 * DEPTH + 8
    g_ada_b = jnp.stack([tot[16 * l:16 * l + 9] for l in range(DEPTH)]).reshape(DEPTH, 9 * D)
    g_norm_g = _chip_cols(jnp.stack([tot[o_ng + 8 * l:o_ng + 8 * l + 3] for l in range(DEPTH)]), chip, qd)
    g_conv_w = _chip_cols(jnp.stack([tot[o_cw + 8 * l:o_cw + 8 * l + 3] for l in range(DEPTH)]), chip, qd)
    g_final_g = tot[o_fg]
    g_rel_bias = tot[o_rb, :NUM_BUCKETS * 24].reshape(NUM_BUCKETS, 24)
    dmod_all = jnp.stack([g3[:, 16 * l:16 * l + 9].reshape(8, 9 * D) for l in range(DEPTH)])
    dmod_sh = _chip_cols(dmod_all, chip, ADA_SH)
    g_ada_w, d_ada_w, nm_ada_w, nv_ada_w = _ada_w_update(cs_all, dmod_sh, ada_w, m_ada_w, v_ada_w)

    def small(w, g, m, v):
        shp = w.shape
        to2 = lambda a: a.reshape(-1, shp[-1])
        return [o.reshape(shp) for o in _adamw_small(to2(w), to2(g), to2(m), to2(v))]

    d_ada_b, nm_ada_b, nv_ada_b = small(ada_b, g_ada_b, m_ada_b, v_ada_b)
    d_norm_g, nm_norm_g, nv_norm_g = small(norm_g, g_norm_g, m_norm_g, v_norm_g)
    d_conv_w, nm_conv_w, nv_conv_w = small(conv_w, g_conv_w, m_conv_w, v_conv_w)
    d_rel_bias, nm_rel_bias, nv_rel_bias = small(rel_bias, g_rel_bias, m_rel_bias, v_rel_bias)
    d_final_g, nm_final_g, nv_final_g = small(final_g, g_final_g, m_final_g, v_final_g)

    behind = nv_ada_w[0, 0:8, 0:128]
    for key in big_out:
        behind = behind + big_out[key][3][0:8, 0:128]
    finish(*pending[-1], behind)

    def big(key, which):
        return big_out[key][which].reshape(params[key].shape)

    grads = [g_ada_w, g_ada_b, g_norm_g, big("wg", 0), big("wu", 0), big("wd", 0), big("win", 0), g_conv_w, big("wco", 0),
             big("wao", 0), big("wo", 0), g_rel_bias, g_final_g]
    deltas = [d_ada_w, d_ada_b, d_norm_g, big("wg", 1), big("wu", 1), big("wd", 1), big("win", 1), d_conv_w, big("wco", 1),
              big("wao", 1), big("wo", 1), d_rel_bias, d_final_g]
    new_m = [nm_ada_w, nm_ada_b, nm_norm_g, big("wg", 2), big("wu", 2), big("wd", 2), big("win", 2), nm_conv_w, big("wco", 2),
             big("wao", 2), big("wo", 2), nm_rel_bias, nm_final_g]
    new_v = [nv_ada_w, nv_ada_b, nv_norm_g, big("wg", 3), big("wu", 3), big("wd", 3), big("win", 3), nv_conv_w, big("wco", 3),
             big("wao", 3), big("wo", 3), nv_rel_bias, nv_final_g]
    return (loss, dx.reshape(x.shape), *grads, *deltas, *new_m, *new_v)
```

```python
import functools

import numpy as np
import jax
import jax.numpy as jnp
from jax import lax
from jax.experimental import pallas as pl
from jax.experimental.pallas import tpu as pltpu

f32, bf16 = jnp.float32, jnp.bfloat16
SDS = jax.ShapeDtypeStruct
MESH = pl.DeviceIdType.MESH

D = 1024
DEPTH = 4
N_CHIPS = 4
FB = 704
HD = 64
QKV_W = 4608
REST_W = 5120
IN_W = QKV_W + REST_W
WIN_SH = IN_W // N_CHIPS
ADA_SH = 9 * D // N_CHIPS
BLK = 128
DILATIONS = (1, 4, 16)
Q_BLOCKS = (4, 1, 1)
NUM_BUCKETS, MAX_DISTANCE = 32, 2048
EPS = 1e-6
NEG = -1e30
SCALE = HD ** -0.5
LR, B1, B2, AEPS, WD, STEP = 0.001, 0.9, 0.999, 1e-08, 0.01, 10
BC1 = 1.0 - B1 ** STEP
BC2 = 1.0 - B2 ** STEP
VMEM_LIMIT = 56 * 1024 * 1024
TM = 512
TMW = 1024
TMP = 1024
TMF = 1024
SH_STEP = 2
TMX = 256
HALO = 16
CB = 512


def _cp(*sem):
    return pltpu.CompilerParams(dimension_semantics=sem if sem else None, vmem_limit_bytes=VMEM_LIMIT)


def _dot(a, b):
    return jnp.dot(a, b, preferred_element_type=f32)


def _dot_nt(a, b):
    return lax.dot_general(a, b, (((1,), (1,)), ((), ())), preferred_element_type=f32)


def _dot_tn(a, b):
    return lax.dot_general(a, b, (((0,), (0,)), ((), ())), preferred_element_type=f32)


def _sigmoid(x):
    return 0.5 * jnp.tanh(0.5 * x) + 0.5


def _norm_fwd(x, g, shift, scale):
    rstd = lax.rsqrt(jnp.mean(x * x, axis=-1, keepdims=True) + EPS)
    xhat = x * rstd
    return xhat * g * (1.0 + scale) + shift, xhat, rstd


def _norm_bwd(dh, xhat, rstd, g, scale):
    dshift = jnp.sum(dh, axis=0, keepdims=True)
    dscale = jnp.sum(dh * xhat * g, axis=0, keepdims=True)
    dg = jnp.sum(dh * xhat * (1.0 + scale), axis=0, keepdims=True)
    dxh = dh * (g * (1.0 + scale))
    dx = rstd * (dxh - xhat * jnp.mean(dxh * xhat, axis=-1, keepdims=True))
    return dx, dshift, dscale, dg


def _allgather_small(xp):
    m_per, n = xp.shape

    def body(x_ref, out_ref, send_sems, recv_sems, local_sem):
        x, y, c = lax.axis_index("x"), lax.axis_index("y"), lax.axis_index("c")
        me, sibling = (x, y, c), (x, y, 1 - c)
        chips = [(1 - x, y), (x, 1 - y), (1 - x, 1 - y)]

        def rows(px, py, pc):
            return out_ref.at[pl.ds((4 * px + 2 * py + pc) * m_per, m_per), :]

        def copy(k, block, to, src=None):
            return pltpu.make_async_remote_copy(
                src_ref=rows(*block) if src is None else src, dst_ref=rows(*block),
                send_sem=send_sems.at[k], recv_sem=recv_sems.at[k], device_id=to, device_id_type=MESH)

        mine = pltpu.make_async_copy(x_ref, rows(*me), local_sem)
        mine.start()
        first = [copy(0, me, sibling, src=x_ref)]
        first += [copy(1 + j, me, (*chip, c), src=x_ref) for j, chip in enumerate(chips)]
        for cp in first:
            cp.start()
        passed = [copy(4 + j, (*chip, c), sibling) for j, chip in enumerate(chips)]
        for j, chip in enumerate(chips):
            copy(1 + j, (*chip, c), me).wait_recv()
            passed[j].start()
        copy(0, sibling, me).wait_recv()
        for j, chip in enumerate(chips):
            copy(4 + j, (*chip, 1 - c), me).wait_recv()
        for cp in first + passed:
            cp.wait_send()
        mine.wait()

    return pl.pallas_call(
        body, name="allgather_small",
        out_shape=SDS((8 * m_per, n), xp.dtype),
        in_specs=[pl.BlockSpec(memory_space=pltpu.VMEM)],
        out_specs=pl.BlockSpec(memory_space=pltpu.VMEM),
        scratch_shapes=[pltpu.SemaphoreType.DMA((7,)), pltpu.SemaphoreType.DMA((7,)), pltpu.SemaphoreType.DMA],
        compiler_params=pltpu.CompilerParams(vmem_limit_bytes=VMEM_LIMIT),
    )(xp)


WCLASSES = (
    ("wg0", "row", (FB, D)), ("wu0", "row", (FB, D)), ("wd0", "row", (FB, D)),
    ("wg1", "row", (FB, D)), ("wu1", "row", (FB, D)), ("wd1", "row", (FB, D)),
    ("win", "col", (D, WIN_SH)), ("wco", "row", (D // N_CHIPS, D)), ("wao", "col", (512, D // N_CHIPS)),
    ("wo", "row", (D // N_CHIPS, D)),
)
NCLS = len(WCLASSES)


def _full_shape(kind, shp):
    if kind == "lead":
        return (N_CHIPS,) + shp
    if kind == "row":
        return (N_CHIPS * shp[0], shp[1])
    return (shp[0], N_CHIPS * shp[1])


def _shard_view(ref, kind, shp, j):
    if kind == "lead":
        return ref.at[j]
    if kind == "row":
        return ref.at[pl.ds(j * shp[0], shp[0]), :]
    return ref.at[:, pl.ds(j * shp[1], shp[1])]


def _half(ref, shp, h):
    hr = shp[0] // 2
    return ref.at[pl.ds(pl.multiple_of(h * hr, 16), hr), :]


def _gather_weights(shards):
    n = NCLS

    def body(*refs):
        ins, outs = refs[:n], refs[n:2 * n]
        send1, recv1, send2, recv2, lsem = refs[2 * n:]
        x, y, c = lax.axis_index("x"), lax.axis_index("y"), lax.axis_index("c")
        chip = 2 * x + y
        sibling = (x, y, 1 - c)

        for mc in range(N_CHIPS):
            @pl.when(chip == mc)
            def _(mc=mc):
                local = []
                for q, (_, kind, shp) in enumerate(WCLASSES):
                    cp = pltpu.make_async_copy(ins[q], _shard_view(outs[q], kind, shp, mc), lsem.at[q])
                    cp.start()
                    local.append(cp)
                sends = []
                for k in (1, 2, 3):
                    pj = mc ^ k
                    for q, (_, kind, shp) in enumerate(WCLASSES):
                        cp = pltpu.make_async_remote_copy(
                            src_ref=_half(ins[q], shp, c), dst_ref=_half(_shard_view(outs[q], kind, shp, mc), shp, c),
                            send_sem=send1.at[q * 3 + k - 1], recv_sem=recv1.at[q * 3 + k - 1],
                            device_id=(pj >> 1, pj & 1, c), device_id_type=MESH)
                        cp.start()
                        sends.append(cp)
                for k in (1, 2, 3):
                    pj = mc ^ k
                    for q, (_, kind, shp) in enumerate(WCLASSES):
                        landed = _half(_shard_view(outs[q], kind, shp, pj), shp, c)
                        pltpu.make_async_remote_copy(
                            src_ref=landed, dst_ref=landed, send_sem=send1.at[q * 3 + k - 1], recv_sem=recv1.at[q * 3 + k - 1],
                            device_id=(pj >> 1, pj & 1, c), device_id_type=MESH).wait_recv()
                        cp = pltpu.make_async_remote_copy(
                            src_ref=landed, dst_ref=landed, send_sem=send2.at[q * 3 + k - 1], recv_sem=recv2.at[q * 3 + k - 1],
                            device_id=sibling, device_id_type=MESH)
                        cp.start()
                        sends.append(cp)
                for k in (1, 2, 3):
                    pj = mc ^ k
                    for q, (_, kind, shp) in enumerate(WCLASSES):
                        other = _half(_shard_view(outs[q], kind, shp, pj), shp, 1 - c)
                        pltpu.make_async_remote_copy(
                            src_ref=other, dst_ref=other, send_sem=send2.at[q * 3 + k - 1], recv_sem=recv2.at[q * 3 + k - 1],
                            device_id=sibling, device_id_type=MESH).wait_recv()
                for cp in sends:
                    cp.wait_send()
                for cp in local:
                    cp.wait()

    anyspec = pl.BlockSpec(memory_space=pl.ANY)
    return pl.pallas_call(
        body, name="gather_weights",
        out_shape=[SDS(_full_shape(kind, shp), bf16) for _, kind, shp in WCLASSES],
        in_specs=[anyspec] * n, out_specs=[anyspec] * n,
        scratch_shapes=[pltpu.SemaphoreType.DMA((3 * n,)), pltpu.SemaphoreType.DMA((3 * n,)),
                        pltpu.SemaphoreType.DMA((3 * n,)), pltpu.SemaphoreType.DMA((3 * n,)),
                        pltpu.SemaphoreType.DMA((n,))],
    )(*shards)


def _scatter_grads(pieces):
    n = NCLS

    def body(*refs):
        ins, outs = refs[:n], refs[n:2 * n]
        send1, recv1, lsem = refs[2 * n:]
        x, y, c = lax.axis_index("x"), lax.axis_index("y"), lax.axis_index("c")
        chip = 2 * x + y

        for mc in range(N_CHIPS):
            @pl.when(chip == mc)
            def _(mc=mc):
                local, sends = [], []
                for q, (_, kind, shp) in enumerate(WCLASSES):
                    cp = pltpu.make_async_copy(_shard_view(ins[q], kind, shp, mc), outs[q].at[0], lsem.at[q])
                    cp.start()
                    local.append(cp)
                for k in (1, 2, 3):
                    pj = mc ^ k
                    for q, (_, kind, shp) in enumerate(WCLASSES):
                        cp = pltpu.make_async_remote_copy(
                            src_ref=_shard_view(ins[q], kind, shp, pj), dst_ref=outs[q].at[k],
                            send_sem=send1.at[q * 3 + k - 1], recv_sem=recv1.at[q * 3 + k - 1],
                            device_id=(pj >> 1, pj & 1, c), device_id_type=MESH)
                        cp.start()
                        sends.append(cp)
                for cp in sends:
                    cp.wait_recv()
                for cp in sends:
                    cp.wait_send()
                for cp in local:
                    cp.wait()

    anyspec = pl.BlockSpec(memory_space=pl.ANY)
    return pl.pallas_call(
        body, name="scatter_grads",
        out_shape=[SDS((N_CHIPS,) + shp, bf16) for _, _, shp in WCLASSES],
        in_specs=[anyspec] * n, out_specs=[anyspec] * n,
        scratch_shapes=[pltpu.SemaphoreType.DMA((3 * n,)), pltpu.SemaphoreType.DMA((3 * n,)), pltpu.SemaphoreType.DMA((n,))],
    )(*pieces)


def _swap_sibling(ts):
    n = len(ts)

    def body(*refs):
        ins, outs = refs[:n], refs[n:2 * n]
        send, recv = refs[2 * n:]
        x, y, c = lax.axis_index("x"), lax.axis_index("y"), lax.axis_index("c")
        cps = []
        for q in range(n):
            cp = pltpu.make_async_remote_copy(src_ref=ins[q], dst_ref=outs[q], send_sem=send.at[q], recv_sem=recv.at[q],
                                              device_id=(x, y, 1 - c), device_id_type=MESH)
            cp.start()
            cps.append(cp)
        for cp in cps:
            cp.wait_recv()
        for cp in cps:
            cp.wait_send()

    anyspec = pl.BlockSpec(memory_space=pl.ANY)
    return pl.pallas_call(
        body, name="swap_sibling",
        out_shape=[SDS(t.shape, t.dtype) for t in ts],
        in_specs=[anyspec] * n, out_specs=[anyspec] * n,
        scratch_shapes=[pltpu.SemaphoreType.DMA((n,)), pltpu.SemaphoreType.DMA((n,))],
    )(*ts)


HBM_SPEC = pl.BlockSpec(memory_space=pltpu.HBM)
SEM_SPEC = pl.BlockSpec(memory_space=pltpu.SEMAPHORE)
ANY_SPEC = pl.BlockSpec(memory_space=pl.ANY)
EFFECT = pltpu.SideEffectType.DATAFLOW_SIDE_EFFECTING
N_COPIES = 3 * NCLS


def _in_hbm(a):
    return pltpu.with_memory_space_constraint(a, pltpu.HBM)


def _chip_index():
    return 2 * lax.axis_index("x") + lax.axis_index("y")


def _place_own(shards):
    n = NCLS

    def body(*refs):
        ins, outs, lsem = refs[:n], refs[n:2 * n], refs[2 * n]
        chip = _chip_index()
        for mc in range(N_CHIPS):
            @pl.when(chip == mc)
            def _(mc=mc):
                cps = [pltpu.make_async_copy(ins[q], _shard_view(outs[q], kind, shp, mc), lsem.at[q])
                       for q, (_, kind, shp) in enumerate(WCLASSES)]
                for cp in cps:
                    cp.start()
                for cp in cps:
                    cp.wait()

    return pl.pallas_call(
        body, name="place_own",
        out_shape=[SDS(_full_shape(kind, shp), bf16) for _, kind, shp in WCLASSES],
        in_specs=[ANY_SPEC] * n, out_specs=[ANY_SPEC] * n,
        scratch_shapes=[pltpu.SemaphoreType.DMA((n,))],
    )(*shards)


def _take_own(pieces):
    n = NCLS

    def body(*refs):
        ins, outs, lsem = refs[:n], refs[n:2 * n], refs[2 * n]
        chip = _chip_index()
        for mc in range(N_CHIPS):
            @pl.when(chip == mc)
            def _(mc=mc):
                cps = [pltpu.make_async_copy(_shard_view(ins[q], kind, shp, mc), outs[q].at[0], lsem.at[q])
                       for q, (_, kind, shp) in enumerate(WCLASSES)]
                for cp in cps:
                    cp.start()
                for cp in cps:
                    cp.wait()

    return pl.pallas_call(
        body, name="take_own",
        out_shape=[SDS((N_CHIPS,) + shp, bf16) for _, _, shp in WCLASSES],
        in_specs=[ANY_SPEC] * n, out_specs=[ANY_SPEC] * n,
        scratch_shapes=[pltpu.SemaphoreType.DMA((n,))],
    )(*pieces)


def _split_start(name, srcs, dsts, after, src_view, dst_view):
    n = NCLS

    def body(*refs):
        src, dst = refs[:n], refs[n:2 * n]
        send, recv = refs[2 * n + 1], refs[2 * n + 2]
        token = refs[-1]
        c = lax.axis_index("c")
        chip = _chip_index()
        for mc in range(N_CHIPS):
            @pl.when(chip == mc)
            def _(mc=mc):
                for k in (1, 2, 3):
                    pj = mc ^ k
                    for q in range(n):
                        pltpu.make_async_remote_copy(
                            src_ref=src_view(src[q], q, mc, pj), dst_ref=dst_view(dst[q], q, mc, k),
                            send_sem=send.at[q * 3 + k - 1], recv_sem=recv.at[q * 3 + k - 1],
                            device_id=(pj >> 1, pj & 1, c), device_id_type=MESH).start()
        token[...] = jnp.zeros_like(token)

    return pl.pallas_call(
        body, name=name,
        out_shape=(pltpu.SemaphoreType.DMA((N_COPIES,)), pltpu.SemaphoreType.DMA((N_COPIES,)),
                   *[pltpu.HBM(a.shape, a.dtype) for a in srcs], *[pltpu.HBM(a.shape, a.dtype) for a in dsts], SDS((8, 128), f32)),
        in_specs=[HBM_SPEC] * (2 * n) + [ANY_SPEC],
        out_specs=(SEM_SPEC, SEM_SPEC, *([HBM_SPEC] * (2 * n)), pl.BlockSpec(memory_space=pltpu.VMEM)),
        input_output_aliases={i: 2 + i for i in range(2 * n)},
        compiler_params=pltpu.CompilerParams(has_side_effects=EFFECT),
    )(*[_in_hbm(a) for a in srcs], *[_in_hbm(a) for a in dsts], after)


def _split_wait(name, started, after, arrival_view):
    n = NCLS
    send, recv = started[0], started[1]
    srcs, dsts = started[2:2 + n], started[2 + n:2 + 2 * n]

    def body(*refs):
        src, dst = refs[:n], refs[n:2 * n]
        send_sem, recv_sem = refs[2 * n], refs[2 * n + 1]
        x, y, c = lax.axis_index("x"), lax.axis_index("y"), lax.axis_index("c")
        for k in (1, 2, 3):
            for q in range(n):
                arrival = arrival_view(dst[q], q, k)
                cp = pltpu.make_async_remote_copy(
                    src_ref=arrival, dst_ref=arrival, send_sem=send_sem.at[q * 3 + k - 1], recv_sem=recv_sem.at[q * 3 + k - 1],
                    device_id=(x, y, 1 - c), device_id_type=MESH)
                cp.wait_send()
                cp.wait_recv()

    out = pl.pallas_call(
        body, name=name,
        out_shape=(*[pltpu.HBM(a.shape, a.dtype) for a in srcs], *[pltpu.HBM(a.shape, a.dtype) for a in dsts]),
        in_specs=[HBM_SPEC] * (2 * n) + [SEM_SPEC, SEM_SPEC, ANY_SPEC],
        out_specs=tuple([HBM_SPEC] * (2 * n)),
        input_output_aliases={i: i for i in range(2 * n)},
        compiler_params=pltpu.CompilerParams(has_side_effects=EFFECT),
    )(*srcs, *dsts, send, recv, after)
    return out[n:]


def _cls(q):
    return WCLASSES[q][1], WCLASSES[q][2]


def _gather_start(shards, after):
    fulls = _place_own(shards)
    return _split_start("gather_start", shards, fulls, after,
                        lambda ref, q, mc, pj: ref,
                        lambda ref, q, mc, k: _shard_view(ref, *_cls(q), mc))


def _gather_wait(started, after):
    return _split_wait("gather_wait", started, after, lambda ref, q, k: _shard_view(ref, *_cls(q), 0))


def _scatter_start(pieces, after):
    lands = _take_own(pieces)
    return _split_start("scatter_start", pieces, lands, after,
                        lambda ref, q, mc, pj: _shard_view(ref, *_cls(q), pj),
                        lambda ref, q, mc, k: ref.at[k])


def _scatter_wait(started, after):
    return _split_wait("scatter_wait", started, after, lambda ref, q, k: ref.at[k])


GROUPS = {"A": (0, 1, 2), "B": (6,), "C": (7, 8, 9), "D": (3, 4, 5)}


def _own_spec(kind, shp, tr):
    R, C = shp
    if kind == "lead":
        return pl.BlockSpec((None, tr, C), lambda i, chip: (chip[0], i, 0))
    if kind == "row":
        return pl.BlockSpec((tr, C), lambda i, chip: (chip[0] * (R // tr) + i, 0))
    return pl.BlockSpec((tr, C), lambda i, chip: (i, chip[0]))


def _cast_place(shards, kind, shp, chip_arr, after):
    n = len(shards)
    R, C = shp
    tr = _row_tile(R, C)

    def body(chip_ref, *refs):
        for q in range(n):
            refs[n + 1 + q][...] = refs[q][...].astype(bf16)

    def in_spec(lead):
        return pl.BlockSpec((None,) * len(lead) + (tr, C), lambda i, chip: (*lead, i, 0))

    return pl.pallas_call(
        body, name="cast_place",
        grid_spec=pltpu.PrefetchScalarGridSpec(
            num_scalar_prefetch=1, grid=(R // tr,),
            in_specs=[in_spec(lead) for _, lead in shards] + [ANY_SPEC],
            out_specs=[_own_spec(kind, shp, tr)] * n),
        out_shape=[SDS(_full_shape(kind, shp), bf16)] * n,
        compiler_params=_cp("arbitrary"),
    )(chip_arr, *[a for a, _ in shards], after)


def _sum_own_slots(piece, land, kind, shp, chip_arr):
    R, C = shp
    tr = _row_tile(R, C)

    def body(chip_ref, p_ref, l_ref, t_ref):
        t = p_ref[...].astype(f32)
        for k in range(N_CHIPS - 1):
            t = t + l_ref[k].astype(f32)
        t_ref[...] = t.astype(bf16)

    return pl.pallas_call(
        body, name="sum_own_slots",
        grid_spec=pltpu.PrefetchScalarGridSpec(
            num_scalar_prefetch=1, grid=(R // tr,),
            in_specs=[_own_spec(kind, shp, tr), pl.BlockSpec((N_CHIPS - 1, tr, C), lambda i, chip: (0, i, 0))],
            out_specs=pl.BlockSpec((tr, C), lambda i, chip: (i, 0))),
        out_shape=SDS((R, C), bf16),
        compiler_params=_cp("arbitrary"),
    )(chip_arr, piece, land)


def _xfer_start(name, arrays, ng, after, src_view, dst_view):
    na = len(arrays)

    def body(*refs):
        arr = refs[:na]
        send, recv, token = refs[na + 1], refs[na + 2], refs[-1]
        c = lax.axis_index("c")
        chip = _chip_index()
        for mc in range(N_CHIPS):
            @pl.when(chip == mc)
            def _(mc=mc):
                for k in (1, 2, 3):
                    pj = mc ^ k
                    for i in range(ng):
                        pltpu.make_async_remote_copy(
                            src_ref=src_view(arr, i, mc, pj), dst_ref=dst_view(arr, i, mc, k),
                            send_sem=send.at[i * 3 + k - 1], recv_sem=recv.at[i * 3 + k - 1],
                            device_id=(pj >> 1, pj & 1, c), device_id_type=MESH).start()
        token[...] = jnp.zeros_like(token)

    return pl.pallas_call(
        body, name=name,
        out_shape=(pltpu.SemaphoreType.DMA((3 * ng,)), pltpu.SemaphoreType.DMA((3 * ng,)),
                   *[pltpu.HBM(a.shape, a.dtype) for a in arrays], SDS((8, 128), f32)),
        in_specs=[HBM_SPEC] * na + [ANY_SPEC],
        out_specs=(SEM_SPEC, SEM_SPEC, *([HBM_SPEC] * na), pl.BlockSpec(memory_space=pltpu.VMEM)),
        input_output_aliases={i: 2 + i for i in range(na)},
        compiler_params=pltpu.CompilerParams(has_side_effects=EFFECT),
    )(*[_in_hbm(a) for a in arrays], after)


def _xfer_wait(name, started, ng, after, arrival_view):
    send, recv = started[0], started[1]
    arrays = started[2:-1]
    na = len(arrays)

    def body(*refs):
        arr = refs[:na]
        send_sem, recv_sem = refs[na], refs[na + 1]
        x, y, c = lax.axis_index("x"), lax.axis_index("y"), lax.axis_index("c")
        for k in (1, 2, 3):
            for i in range(ng):
                arrival = arrival_view(arr, i)
                cp = pltpu.make_async_remote_copy(
                    src_ref=arrival, dst_ref=arrival, send_sem=send_sem.at[i * 3 + k - 1], recv_sem=recv_sem.at[i * 3 + k - 1],
                    device_id=(x, y, 1 - c), device_id_type=MESH)
                cp.wait_send()
                cp.wait_recv()

    return pl.pallas_call(
        body, name=name,
        out_shape=tuple(pltpu.HBM(a.shape, a.dtype) for a in arrays),
        in_specs=[HBM_SPEC] * na + [SEM_SPEC, SEM_SPEC, ANY_SPEC],
        out_specs=tuple([HBM_SPEC] * na),
        input_output_aliases={i: i for i in range(na)},
        compiler_params=pltpu.CompilerParams(has_side_effects=EFFECT),
    )(*arrays, send, recv, after)


def _gather_group_cast(group, shards_f32, chip_arr, after):
    fulls = [None] * len(group)
    by_shape = {}
    for i, q in enumerate(group):
        by_shape.setdefault(_cls(q), []).append(i)
    for (kind, shp), idx in by_shape.items():
        for i, f in zip(idx, _cast_place([shards_f32[i] for i in idx], kind, shp, chip_arr, after)):
            fulls[i] = f
    return fulls


def _gather_group_start(tag, group, fulls, after):
    def view(arr, i, mc, _):
        kind, shp = _cls(group[i])
        return _half(_shard_view(arr[i], kind, shp, mc), shp, lax.axis_index("c"))
    return _xfer_start("gather_start_" + tag, fulls, len(group), after, view, view)


def _gather_group_forward(tag, group, started, after):
    ng = len(group)
    send1, recv1 = started[0], started[1]
    arrays = started[2:-1]
    na = len(arrays)

    def body(*refs):
        arr = refs[:na]
        send_in, recv_in = refs[na], refs[na + 1]
        send2, recv2, token = refs[na + 3], refs[na + 4], refs[-1]
        x, y, c = lax.axis_index("x"), lax.axis_index("y"), lax.axis_index("c")
        chip = _chip_index()
        for mc in range(N_CHIPS):
            @pl.when(chip == mc)
            def _(mc=mc):
                for k in (1, 2, 3):
                    pj = mc ^ k
                    for i in range(ng):
                        kind, shp = _cls(group[i])
                        landed = _half(_shard_view(arr[i], kind, shp, pj), shp, c)
                        pltpu.make_async_remote_copy(
                            src_ref=landed, dst_ref=landed, send_sem=send_in.at[i * 3 + k - 1], recv_sem=recv_in.at[i * 3 + k - 1],
                            device_id=(pj >> 1, pj & 1, c), device_id_type=MESH).wait_recv()
                        pltpu.make_async_remote_copy(
                            src_ref=landed, dst_ref=landed, send_sem=send2.at[i * 3 + k - 1], recv_sem=recv2.at[i * 3 + k - 1],
                            device_id=(x, y, 1 - c), device_id_type=MESH).start()
        token[...] = jnp.zeros_like(token)

    return pl.pallas_call(
        body, name="gather_forward_" + tag,
        out_shape=(pltpu.SemaphoreType.DMA((3 * ng,)), pltpu.SemaphoreType.DMA((3 * ng,)),
                   *[pltpu.HBM(a.shape, a.dtype) for a in arrays], SDS((8, 128), f32)),
        in_specs=[HBM_SPEC] * na + [SEM_SPEC, SEM_SPEC, ANY_SPEC],
        out_specs=(SEM_SPEC, SEM_SPEC, *([HBM_SPEC] * na), pl.BlockSpec(memory_space=pltpu.VMEM)),
        input_output_aliases={i: 2 + i for i in range(na)},
        compiler_params=pltpu.CompilerParams(has_side_effects=EFFECT),
    )(*arrays, send1, recv1, after)


def _gather_group_wait(tag, group, send1, forwarded, after):
    ng = len(group)
    send2, recv2 = forwarded[0], forwarded[1]
    arrays = forwarded[2:-1]
    na = len(arrays)

    def body(*refs):
        arr = refs[:na]
        s1, s2, r2 = refs[na], refs[na + 1], refs[na + 2]
        x, y, c = lax.axis_index("x"), lax.axis_index("y"), lax.axis_index("c")
        for k in (1, 2, 3):
            for i in range(ng):
                kind, shp = _cls(group[i])
                half = _half(_shard_view(arr[i], kind, shp, 0), shp, 0)
                pltpu.make_async_remote_copy(src_ref=half, dst_ref=half, send_sem=s1.at[i * 3 + k - 1], recv_sem=r2.at[i * 3 + k - 1],
                                             device_id=(x, y, 1 - c), device_id_type=MESH).wait_send()
                cp = pltpu.make_async_remote_copy(src_ref=half, dst_ref=half, send_sem=s2.at[i * 3 + k - 1], recv_sem=r2.at[i * 3 + k - 1],
                                                  device_id=(x, y, 1 - c), device_id_type=MESH)
                cp.wait_send()
                cp.wait_recv()

    return pl.pallas_call(
        body, name="gather_wait_" + tag,
        out_shape=tuple(pltpu.HBM(a.shape, a.dtype) for a in arrays),
        in_specs=[HBM_SPEC] * na + [SEM_SPEC, SEM_SPEC, SEM_SPEC, ANY_SPEC],
        out_specs=tuple([HBM_SPEC] * na),
        input_output_aliases={i: i for i in range(na)},
        compiler_params=pltpu.CompilerParams(has_side_effects=EFFECT),
    )(*arrays, send1, send2, recv2, after)


def _scatter_group_start(tag, group, pieces, after):
    ng = len(group)
    lands = [lax.empty((N_CHIPS - 1,) + _cls(q)[1], bf16) for q in group]
    return _xfer_start("scatter_start_" + tag, list(pieces) + lands, ng, after,
                       lambda arr, i, mc, pj: _shard_view(arr[i], *_cls(group[i]), pj),
                       lambda arr, i, mc, k: arr[ng + i].at[k - 1])


def _scatter_group_wait(tag, group, started, after):
    ng = len(group)
    out = _xfer_wait("scatter_wait_" + tag, started, ng, after, lambda arr, i: arr[ng + i].at[0])
    return out[:ng], out[ng:]


def _mod_shards(c_all, ada_w, ada_b_sh):
    tn = ADA_SH // 3

    def body(c_ref, w_ref, b_ref, o_ref, cs_ref):
        cv = c_ref[...]
        cs = cv * _sigmoid(cv)
        cs_ref[...] = cs
        o_ref[...] = _dot(cs.astype(bf16), w_ref[...].astype(bf16)) + b_ref[...]

    return pl.pallas_call(
        body, name="mod_shards", grid=(DEPTH, 3),
        out_shape=[SDS((DEPTH, 8, ADA_SH), f32), SDS((8, D), f32)],
        in_specs=[pl.BlockSpec((8, D), lambda l, t: (0, 0)),
                  pl.BlockSpec((None, D, tn), lambda l, t: (l, 0, t)),
                  pl.BlockSpec((None, 1, tn), lambda l, t: (l, 0, t))],
        out_specs=[pl.BlockSpec((None, 8, tn), lambda l, t: (l, 0, t)), pl.BlockSpec((8, D), lambda l, t: (0, 0))],
        compiler_params=_cp("arbitrary", "arbitrary"),
    )(c_all, ada_w, ada_b_sh.reshape(DEPTH, 1, ADA_SH))


def _t5_bucket(dist):
    exact = NUM_BUCKETS // 2
    dd = np.maximum(dist, 1).astype(np.float32)
    large = exact + (np.log(dd / exact) / np.log(MAX_DISTANCE / exact) * (NUM_BUCKETS - exact)).astype(np.int32)
    large = np.minimum(large, NUM_BUCKETS - 1)
    return np.where(dist < exact, dist, large).astype(np.int32)


def _bucket_table():
    i = np.arange(BLK)[:, None]
    j = np.arange(2 * BLK)[None, :]
    rel = i - j + BLK
    return np.stack([_t5_bucket(np.maximum(rel, 0) * d) for d in DILATIONS]).astype(np.int32)


def _band():
    rel = lax.broadcasted_iota(jnp.int32, (BLK, 2 * BLK), 0) - lax.broadcasted_iota(jnp.int32, (BLK, 2 * BLK), 1) + BLK
    return (rel >= 0) & (rel <= BLK)


def _bias_blocks(rel_bias, buckets):
    def body(tab_ref, bk_ref, o_ref):
        h = pl.program_id(0)
        bk = bk_ref[...]
        acc = jnp.zeros((BLK, 2 * BLK), f32)
        for b in range(NUM_BUCKETS):
            acc = jnp.where(bk == b, tab_ref[b, h], acc)
        o_ref[...] = jnp.where(_band(), acc, NEG)

    return pl.pallas_call(
        body, name="bias_blocks", grid=(24,),
        out_shape=SDS((24, BLK, 2 * BLK), f32),
        in_specs=[pl.BlockSpec(memory_space=pltpu.SMEM), pl.BlockSpec((None, BLK, 2 * BLK), lambda h: (h // 8, 0, 0))],
        out_specs=pl.BlockSpec((None, BLK, 2 * BLK), lambda h: (h, 0, 0)),
        compiler_params=_cp("arbitrary"),
    )(rel_bias, buckets)


def _bias_grad(dsaccs, buckets):
    nl = len(dsaccs)

    def body(*refs):
        bk = refs[nl][...]
        tot = refs[0][...]
        for r in refs[1:nl]:
            tot = tot + r[...]
        lane = lax.broadcasted_iota(jnp.int32, (1, 128), 1)
        row = jnp.zeros((1, 128), f32)
        for b in range(NUM_BUCKETS):
            row = jnp.where(lane == b, jnp.sum(jnp.where(bk == b, tot, 0.0)), row)
        refs[nl + 1][...] = row

    return pl.pallas_call(
        body, name="bias_grad", grid=(24,),
        out_shape=SDS((24, 1, 128), f32),
        in_specs=[pl.BlockSpec((None, BLK, 2 * BLK), lambda h: (h, 0, 0))] * nl
                 + [pl.BlockSpec((None, BLK, 2 * BLK), lambda h: (h // 8, 0, 0))],
        out_specs=pl.BlockSpec((None, 1, 128), lambda h: (h, 0, 0)),
        compiler_params=_cp("arbitrary"),
    )(*dsaccs, buckets)


def _ffn_fwd(x, mod9, g3, wg, wu, wd, sub):
    S = x.shape[0]

    def body(x_ref, mod_ref, g_ref, wg_ref, wu_ref, wd_ref, xo_ref, h_ref, ga_ref, sa_ref, hid_ref, y_ref, acc):
        j = pl.program_id(1)

        @pl.when(j == 0)
        def _():
            h, _, _ = _norm_fwd(x_ref[...], g_ref[sub:sub + 1, :], mod_ref[3 * sub:3 * sub + 1, :], mod_ref[3 * sub + 1:3 * sub + 2, :])
            h_ref[...] = h.astype(bf16)
            acc[...] = jnp.zeros_like(acc)

        h = h_ref[...]
        a = _dot_nt(h, wg_ref[...])
        u = _dot_nt(h, wu_ref[...])
        sg = _sigmoid(a)
        sil = a * sg
        ga_ref[...] = (u * (sg * (1.0 + a * (1.0 - sg)))).astype(bf16)
        sa_ref[...] = sil.astype(bf16)
        hid_ref[...] = (sil * u).astype(bf16)
        acc[...] += _dot(hid_ref[...], wd_ref[...])

        @pl.when(j == N_CHIPS - 1)
        def _():
            y = acc[...]
            y_ref[...] = y.astype(bf16)
            xo_ref[...] = x_ref[...] + 0.5 * mod_ref[3 * sub + 2:3 * sub + 3, :] * y

    row = pl.BlockSpec((TMF, D), lambda i, j: (i, 0))
    hidb = pl.BlockSpec((None, TMF, FB), lambda i, j: (j, i, 0))
    hids = SDS((N_CHIPS, S, FB), bf16)
    return pl.pallas_call(
        body, name="ffn_fwd", grid=(S // TMF, N_CHIPS),
        out_shape=[SDS((S, D), f32), SDS((S, D), bf16), hids, hids, hids, SDS((S, D), bf16)],
        in_specs=[row, pl.BlockSpec((9, D), lambda i, j: (0, 0)), pl.BlockSpec((3, D), lambda i, j: (0, 0)),
                  pl.BlockSpec((FB, D), lambda i, j: (j, 0)), pl.BlockSpec((FB, D), lambda i, j: (j, 0)),
                  pl.BlockSpec((FB, D), lambda i, j: (j, 0))],
        out_specs=[row, row, hidb, hidb, hidb, row],
        scratch_shapes=[pltpu.VMEM((TMF, D), f32)],
        compiler_params=_cp("arbitrary", "arbitrary"),
    )(x, mod9, g3, wg, wu, wd)


def _ffn_bwd1(dxo, x, mod9, g3, y, ga, sa, wg, wu, wd, sub):
    S = x.shape[0]

    def body(dxo_ref, x_ref, mod_ref, g_ref, y_ref, ga_ref, sa_ref, wg_ref, wu_ref, wd_ref,
             dxi_ref, da_ref, du_ref, dy_ref, sm_ref, acc):
        i, j = pl.program_id(0), pl.program_id(1)
        gate = mod_ref[3 * sub + 2:3 * sub + 3, :]

        @pl.when((i == 0) & (j == 0))
        def _():
            sm_ref[...] = jnp.zeros_like(sm_ref)

        @pl.when(j == 0)
        def _():
            dxo_v = dxo_ref[...]
            dy_ref[...] = (0.5 * gate * dxo_v).astype(bf16)
            sm_ref[2:3, :] += jnp.sum(0.5 * y_ref[...].astype(f32) * dxo_v, axis=0, keepdims=True)
            acc[...] = jnp.zeros_like(acc)

        part = None
        for s in range(SH_STEP):
            dhid = _dot_nt(dy_ref[...], wd_ref[s * FB:(s + 1) * FB, :])
            da = (dhid * ga_ref[s].astype(f32)).astype(bf16)
            du = (dhid * sa_ref[s].astype(f32)).astype(bf16)
            da_ref[s] = da
            du_ref[s] = du
            t = _dot(da, wg_ref[s * FB:(s + 1) * FB, :]) + _dot(du, wu_ref[s * FB:(s + 1) * FB, :])
            part = t if part is None else part + t
        acc[...] += part

        @pl.when(j == N_CHIPS // SH_STEP - 1)
        def _():
            g = g_ref[sub:sub + 1, :]
            scale = mod_ref[3 * sub + 1:3 * sub + 2, :]
            _, xhat, rstd = _norm_fwd(x_ref[...], g, mod_ref[3 * sub:3 * sub + 1, :], scale)
            dx, dshift, dscale, dg = _norm_bwd(acc[...], xhat, rstd, g, scale)
            dxi_ref[...] = dxo_ref[...] + dx
            sm_ref[0:1, :] += dshift
            sm_ref[1:2, :] += dscale
            sm_ref[3:4, :] += dg

    row = pl.BlockSpec((TM, D), lambda i, j: (i, 0))
    hidb = pl.BlockSpec((SH_STEP, TM, FB), lambda i, j: (j, i, 0))
    wcol = pl.BlockSpec((SH_STEP * FB, D), lambda i, j: (j, 0))
    return pl.pallas_call(
        body, name="ffn_bwd1", grid=(S // TM, N_CHIPS // SH_STEP),
        out_shape=[SDS((S, D), f32), SDS((N_CHIPS, S, FB), bf16), SDS((N_CHIPS, S, FB), bf16), SDS((S, D), bf16), SDS((8, D), f32)],
        in_specs=[row, row, pl.BlockSpec((9, D), lambda i, j: (0, 0)), pl.BlockSpec((3, D), lambda i, j: (0, 0)), row,
                  hidb, hidb, wcol, wcol, pl.BlockSpec((SH_STEP * FB, D), lambda i, j: (j, 0))],
        out_specs=[row, hidb, hidb, row, pl.BlockSpec((8, D), lambda i, j: (0, 0))],
        scratch_shapes=[pltpu.VMEM((TM, D), f32)],
        compiler_params=_cp("arbitrary", "arbitrary"),
    )(dxo, x, mod9, g3, y, ga, sa, wg, wu, wd)


def _ffn_bwd2(h, da, du, hid, dy):
    S = h.shape[0]
    ni = S // TMW

    def body(h_ref, da_ref, du_ref, hid_ref, dy_ref, dwg_ref, dwu_ref, dwd_ref, ag, au, ad):
        i = pl.program_id(1)

        @pl.when(i == 0)
        def _():
            ag[...] = jnp.zeros_like(ag)
            au[...] = jnp.zeros_like(au)
            ad[...] = jnp.zeros_like(ad)

        hv = h_ref[...]
        ag[...] += _dot_tn(da_ref[...], hv)
        au[...] += _dot_tn(du_ref[...], hv)
        ad[...] += _dot_tn(hid_ref[...], dy_ref[...])

        @pl.when(i == ni - 1)
        def _():
            dwg_ref[...] = ag[...].astype(bf16)
            dwu_ref[...] = au[...].astype(bf16)
            dwd_ref[...] = ad[...].astype(bf16)

    row = pl.BlockSpec((TMW, D), lambda j, i: (i, 0))
    hidb = pl.BlockSpec((None, TMW, FB), lambda j, i: (j, i, 0))
    wrow = pl.BlockSpec((FB, D), lambda j, i: (j, 0))
    return pl.pallas_call(
        body, name="ffn_bwd2", grid=(N_CHIPS, ni),
        out_shape=[SDS((N_CHIPS * FB, D), bf16)] * 3,
        in_specs=[row, hidb, hidb, hidb, row],
        out_specs=[wrow, wrow, wrow],
        scratch_shapes=[pltpu.VMEM((FB, D), f32)] * 3,
        compiler_params=_cp("arbitrary", "arbitrary"),
    )(h, da, du, hid, dy)


def _mix_qkv(x, mod9, g3, win):
    S = x.shape[0]

    def body(x_ref, mod_ref, g_ref, w_ref, h_ref, o_ref):
        @pl.when(pl.program_id(1) == 0)
        def _():
            h, _, _ = _norm_fwd(x_ref[...], g_ref[1:2, :], mod_ref[3:4, :], mod_ref[4:5, :])
            h_ref[...] = h.astype(bf16)

        o_ref[...] = _dot(h_ref[...], w_ref[...]).astype(bf16)

    row = pl.BlockSpec((TMP, D), lambda i, j: (i, 0))
    return pl.pallas_call(
        body, name="mix_qkv", grid=(S // TMP, QKV_W // CB),
        out_shape=[SDS((S, D), bf16), SDS((S, QKV_W), bf16)],
        in_specs=[row, pl.BlockSpec((9, D), lambda i, j: (0, 0)), pl.BlockSpec((3, D), lambda i, j: (0, 0)),
                  pl.BlockSpec((D, CB), lambda i, j: (0, j))],
        out_specs=[row, pl.BlockSpec((TMP, CB), lambda i, j: (i, j))],
        compiler_params=_cp("arbitrary", "arbitrary"),
    )(x, mod9, g3, win)


def _mix_rest(h, win):
    S = h.shape[0]
    off = QKV_W // CB

    def body(h_ref, w_ref, o_ref):
        o_ref[...] = _dot(h_ref[...], w_ref[...]).astype(bf16)

    return pl.pallas_call(
        body, name="mix_rest", grid=(S // TMP, REST_W // CB),
        out_shape=SDS((S, REST_W), bf16),
        in_specs=[pl.BlockSpec((TMP, D), lambda i, j: (i, 0)), pl.BlockSpec((D, CB), lambda i, j: (0, off + j))],
        out_specs=pl.BlockSpec((TMP, CB), lambda i, j: (i, j)),
        compiler_params=_cp("arbitrary", "arbitrary"),
    )(h, win)


def _widen(srcs, dsts):
    for src, dst in zip(srcs, dsts):
        dst[...] = src[...].astype(f32)


def _qkv_scratch(R, Rb):
    return [pltpu.VMEM((R, 128), f32), pltpu.VMEM((R, 128), f32), pltpu.VMEM((Rb, 128), f32),
            pltpu.VMEM((R, 128), f32), pltpu.VMEM((Rb, 128), f32)]


def _attn_fwd(qkv, bias, g):
    S = qkv.shape[0]
    d = DILATIONS[g]
    nq = Q_BLOCKS[g]
    Rb = BLK * d
    R = Rb * nq
    nb = S // R
    qb, kb, vb = 4 * g, 12 + 4 * g, 24 + 4 * g

    def body(q_in, kc_in, kp_in, vc_in, vp_in, b_ref, o_ref, l_ref, q_ref, kc_ref, kp_ref, vc_ref, vp_ref):
        n = pl.program_id(1)
        col = lax.broadcasted_iota(jnp.int32, (BLK, 2 * BLK), 1)
        first = jnp.where((col < BLK) & (n == 0), NEG, 0.0)
        head0 = lax.broadcasted_iota(jnp.int32, (1, 2 * HD), 1) < HD
        _widen((q_in, kc_in, kp_in, vc_in, vp_in), (q_ref, kc_ref, kp_ref, vc_ref, vp_ref))

        def one(b, r):
            sl = pl.ds(b * Rb + r, BLK, stride=d)
            q = q_ref[sl, :]
            if b == 0:
                kp, vp = kp_ref[pl.ds(r, BLK, stride=d), :], vp_ref[pl.ds(r, BLK, stride=d), :]
            else:
                before = pl.ds((b - 1) * Rb + r, BLK, stride=d)
                kp, vp = kc_ref[before, :], vc_ref[before, :]
            kk = jnp.concatenate([kp, kc_ref[sl, :]], axis=0).astype(bf16)
            vv = jnp.concatenate([vp, vc_ref[sl, :]], axis=0).astype(bf16)
            os, ls = [], []
            for hh in range(2):
                qm = jnp.where(head0 if hh == 0 else ~head0, q, 0.0).astype(bf16)
                s = _dot_nt(qm, kk) * SCALE + b_ref[hh]
                if b == 0:
                    s = s + first
                m = jnp.max(s, axis=-1, keepdims=True)
                p = jnp.exp(s - m)
                l = jnp.sum(p, axis=-1, keepdims=True)
                os.append(_dot(p.astype(bf16), vv) / l)
                ls.append(m + jnp.log(l))
            o_ref[sl, :] = jnp.where(head0, os[0], os[1])
            l_ref[sl, :] = jnp.where(head0, ls[0], ls[1])

        for b in range(nq):
            if d == 1:
                one(b, 0)
            else:
                lax.fori_loop(0, d, lambda r, carry, b=b: (one(b, r), carry)[1], 0, unroll=4)

    def blk(cb, prev):
        if prev:
            return pl.BlockSpec((Rb, 128), lambda hp, n: (jnp.maximum(n * nq - 1, 0), cb + hp))
        return pl.BlockSpec((R, 128), lambda hp, n: (n, cb + hp))

    outb = pl.BlockSpec((R, 128), lambda hp, n: (n, hp))
    return pl.pallas_call(
        body, name=f"attn_fwd_d{d}", grid=(4, nb),
        out_shape=[SDS((S, 512), f32), SDS((S, 512), f32)],
        in_specs=[blk(qb, False), blk(kb, False), blk(kb, True), blk(vb, False), blk(vb, True),
                  pl.BlockSpec((2, BLK, 2 * BLK), lambda hp, n: (4 * g + hp, 0, 0))],
        out_specs=[outb, outb],
        scratch_shapes=_qkv_scratch(R, Rb),
        compiler_params=_cp("arbitrary", "arbitrary"),
    )(qkv, qkv, qkv, qkv, qkv, bias)


def _attn_bwd(qkv, do, o, lse, bias, dq_all, dk_all, dv_all, g):
    S = qkv.shape[0]
    d = DILATIONS[g]
    nq = Q_BLOCKS[g]
    Rb = BLK * d
    R = Rb * nq
    nb = S // R
    qb, kb, vb = 4 * g, 12 + 4 * g, 24 + 4 * g

    def body(q_in, kc_in, kp_in, vc_in, vp_in, do_ref, o_ref, l_ref, b_ref, dqi, dki, dvi,
             dq_out, dk_out, dv_out, ds_ref, ck, cv, tk, tv, dq_ref, q_ref, kc_ref, kp_ref, vc_ref, vp_ref):
        n = pl.program_id(1)
        col = lax.broadcasted_iota(jnp.int32, (BLK, 2 * BLK), 1)
        first = jnp.where((col < BLK) & (n == 0), NEG, 0.0)

        @pl.when(n == 0)
        def _():
            ck[...] = jnp.zeros_like(ck)
            cv[...] = jnp.zeros_like(cv)
            ds_ref[...] = jnp.zeros_like(ds_ref)

        @pl.when(n < nb)
        def _():
            head0 = lax.broadcasted_iota(jnp.int32, (1, 2 * HD), 1) < HD
            _widen((q_in, kc_in, kp_in, vc_in, vp_in), (q_ref, kc_ref, kp_ref, vc_ref, vp_ref))

            def one(b, r):
                sl = pl.ds(b * Rb + r, BLK, stride=d)
                before = pl.ds((max(b, 1) - 1) * Rb + r, BLK, stride=d)
                q = q_ref[sl, :]
                if b == 0:
                    kp, vp = kp_ref[pl.ds(r, BLK, stride=d), :], vp_ref[pl.ds(r, BLK, stride=d), :]
                else:
                    kp, vp = kc_ref[before, :], vc_ref[before, :]
                kk = jnp.concatenate([kp, kc_ref[sl, :]], axis=0).astype(bf16)
                vv = jnp.concatenate([vp, vc_ref[sl, :]], axis=0).astype(bf16)
                dov, lv = do_ref[sl, :], l_ref[sl, :]
                prod = dov * o_ref[sl, :]
                qb, dob = q.astype(bf16), dov.astype(bf16)
                dqs, dks, dvs = [], [], []
                for hh in range(2):
                    msk = head0 if hh == 0 else ~head0
                    qm = jnp.where(msk, q, 0.0).astype(bf16)
                    dom = jnp.where(msk, dov, 0.0).astype(bf16)
                    dsum = jnp.sum(jnp.where(msk, prod, 0.0), axis=-1, keepdims=True)
                    s = _dot_nt(qm, kk) * SCALE + b_ref[hh]
                    if b == 0:
                        s = s + first
                    p = jnp.exp(s - lv[:, HD * hh:HD * hh + 1])
                    ds = p * (_dot_nt(dom, vv) - dsum)
                    ds_ref[hh] += ds
                    dsb = ds.astype(bf16)
                    dqs.append(_dot(dsb, kk) * SCALE)
                    dks.append(_dot_tn(dsb, qb) * SCALE)
                    dvs.append(_dot_tn(p.astype(bf16), dob))
                dq_ref[sl, :] = jnp.where(head0, dqs[0], dqs[1])
                dk = jnp.where(head0, dks[0], dks[1])
                dv = jnp.where(head0, dvs[0], dvs[1])
                tk[sl, :] = dk[BLK:]
                tv[sl, :] = dv[BLK:]
                if b == 0:
                    prev_rows = pl.ds((nq - 1) * Rb + r, BLK, stride=d)
                    ck[prev_rows, :] += dk[:BLK]
                    cv[prev_rows, :] += dv[:BLK]
                else:
                    tk[before, :] += dk[:BLK]
                    tv[before, :] += dv[:BLK]

            for b in range(nq):
                if d == 1:
                    one(b, 0)
                else:
                    lax.fori_loop(0, d, lambda r, carry, b=b: (one(b, r), carry)[1], 0, unroll=4)
            dq_out[...] = dq_ref[...].astype(bf16)
            dk_out[...] = ck[...].astype(bf16)
            dv_out[...] = cv[...].astype(bf16)
            ck[...] = tk[...]
            cv[...] = tv[...]

        @pl.when(n == nb)
        def _():
            dk_out[...] = ck[...].astype(bf16)
            dv_out[...] = cv[...].astype(bf16)

    last = nb - 1

    def blk(cb, prev):
        if prev:
            return pl.BlockSpec((Rb, 128), lambda hp, n: (jnp.maximum(jnp.minimum(n, last) * nq - 1, 0), cb + hp))
        return pl.BlockSpec((R, 128), lambda hp, n: (jnp.minimum(n, last), cb + hp))

    cur = pl.BlockSpec((R, 128), lambda hp, n: (jnp.minimum(n, last), hp))
    anyspec = pl.BlockSpec(memory_space=pl.ANY)
    dqo = pl.BlockSpec((R, 128), lambda hp, n: (jnp.minimum(n, last), 4 * g + hp))
    dko = pl.BlockSpec((R, 128), lambda hp, n: (jnp.maximum(n - 1, 0), 4 * g + hp))
    return pl.pallas_call(
        body, name=f"attn_bwd_d{d}", grid=(4, nb + 1),
        out_shape=[SDS((S, 1536), bf16), SDS((S, 1536), bf16), SDS((S, 1536), bf16), SDS((8, BLK, 2 * BLK), f32)],
        in_specs=[blk(qb, False), blk(kb, False), blk(kb, True), blk(vb, False), blk(vb, True), cur, cur, cur,
                  pl.BlockSpec((2, BLK, 2 * BLK), lambda hp, n: (4 * g + hp, 0, 0)), anyspec, anyspec, anyspec],
        out_specs=[dqo, dko, dko, pl.BlockSpec((2, BLK, 2 * BLK), lambda hp, n: (hp, 0, 0))],
        scratch_shapes=[pltpu.VMEM((R, 128), f32)] * 5 + _qkv_scratch(R, Rb),
        input_output_aliases={9: 0, 10: 1, 11: 2},
        compiler_params=_cp("arbitrary", "arbitrary"),
    )(qkv, qkv, qkv, qkv, qkv, do, o, lse, bias, dq_all, dk_all, dv_all)


def _conv_z(cc, ch, hc, hh, cw_ref, first):
    halo = jnp.where(first, 0.0, hc.astype(f32) * hh.astype(f32))
    T = jnp.concatenate([halo, cc * ch], axis=0)
    z = cw_ref[2:3, :] * T + cw_ref[1:2, :] * pltpu.roll(T, 1, 0) + cw_ref[0:1, :] * pltpu.roll(T, 2, 0)
    return T, z[HALO:]


def _rest_specs(tm, with_next):
    per = tm // HALO
    specs = [pl.BlockSpec((tm, D), functools.partial(lambda i, k: (i, k), k=k)) for k in range(5)]
    specs += [pl.BlockSpec((HALO, D), functools.partial(lambda i, k: (jnp.maximum(i * per - 1, 0), k), k=k)) for k in (1, 2)]
    return specs


def _mix_out_fwd(x, mod9, rest, ogs, lgs, cw, wco, wao, wo):
    S = x.shape[0]
    tm = TMX

    def body(x_ref, mod_ref, cb_ref, cc_ref, ch_ref, gc_ref, ga_ref, hc_ref, hh_ref,
             o0, o1, o2, l0, l1, l2, cw_ref, wco_ref, wao_ref, wo_ref,
             xo_ref, o_ref, lse_ref, yc_ref, ya_ref, out_ref):
        i = pl.program_id(0)
        lv = [l0[...], l1[...], l2[...]]
        mx = jnp.maximum(jnp.maximum(lv[0], lv[1]), lv[2])
        es = [jnp.exp(l - mx) for l in lv]
        den = es[0] + es[1] + es[2]
        o = (es[0] / den) * o0[...] + (es[1] / den) * o1[...] + (es[2] / den) * o2[...]
        o_ref[...] = o
        lse_ref[...] = mx + jnp.log(den)
        _, z = _conv_z(cc_ref[...].astype(f32), ch_ref[...].astype(f32), hc_ref[...], hh_ref[...], cw_ref, i == 0)
        p = (cb_ref[...].astype(f32) * z).astype(bf16)
        yc = _dot(p, wco_ref[...])
        ya = _dot(o.astype(bf16), wao_ref[...])
        yc_ref[...] = yc.astype(bf16)
        ya_ref[...] = ya.astype(bf16)
        merged = _sigmoid(gc_ref[...].astype(f32)) * yc + _sigmoid(ga_ref[...].astype(f32)) * ya
        out = _dot(merged.astype(bf16), wo_ref[...])
        out_ref[...] = out.astype(bf16)
        xo_ref[...] = x_ref[...] + mod_ref[5:6, :] * out

    row = pl.BlockSpec((tm, D), lambda i: (i, 0))
    att = pl.BlockSpec((tm, 512), lambda i: (i, 0))
    full = lambda shp: pl.BlockSpec(shp, lambda i: (0, 0))
    return pl.pallas_call(
        body, name="mix_out_fwd", grid=(S // tm,),
        out_shape=[SDS((S, D), f32), SDS((S, 512), f32), SDS((S, 512), f32), SDS((S, D), bf16), SDS((S, D), bf16), SDS((S, D), bf16)],
        in_specs=[row, full((9, D))] + _rest_specs(tm, False) + [att] * 6 + [full((3, D)), full((D, D)), full((512, D)), full((D, D))],
        out_specs=[row, att, att, row, row, row],
        compiler_params=_cp("arbitrary"),
    )(x, mod9, *([rest] * 7), *ogs, *lgs, cw, wco, wao, wo)


def _mix_out_bwd(dxo, mod9, outv, yc, ya, rest, o, cw, wco, wao, wo):
    S = dxo.shape[0]
    tm = TMX
    ni = S // tm

    def body(dxo_ref, mod_ref, out_ref, yc_ref, ya_ref, cb_ref, cc_ref, ch_ref, gc_ref, ga_ref, hc_ref, hh_ref,
             o_ref, cw_ref, wco_ref, wao_ref, wo_ref,
             dp_ref, dg2_ref, do_ref, dwco_ref, dwao_ref, dwo_ref, sm_ref, aco, aao, ao):
        i = pl.program_id(0)

        @pl.when(i == 0)
        def _():
            sm_ref[...] = jnp.zeros_like(sm_ref)
            aco[...] = jnp.zeros_like(aco)
            aao[...] = jnp.zeros_like(aao)
            ao[...] = jnp.zeros_like(ao)

        dxo_v = dxo_ref[...]
        sm_ref[2:3, :] += jnp.sum(out_ref[...].astype(f32) * dxo_v, axis=0, keepdims=True)
        dout = (mod_ref[5:6, :] * dxo_v).astype(bf16)
        dmerged = _dot_nt(dout, wo_ref[...])
        sc, sa = _sigmoid(gc_ref[...].astype(f32)), _sigmoid(ga_ref[...].astype(f32))
        ycv, yav = yc_ref[...].astype(f32), ya_ref[...].astype(f32)
        ao[...] += _dot_tn((sc * ycv + sa * yav).astype(bf16), dout)
        dyc = (dmerged * sc).astype(bf16)
        dya = (dmerged * sa).astype(bf16)
        dg2_ref[:, :D] = (dmerged * ycv * sc * (1.0 - sc)).astype(bf16)
        dg2_ref[:, D:] = (dmerged * yav * sa * (1.0 - sa)).astype(bf16)
        dp_ref[...] = _dot_nt(dyc, wco_ref[...]).astype(bf16)
        _, z = _conv_z(cc_ref[...].astype(f32), ch_ref[...].astype(f32), hc_ref[...], hh_ref[...], cw_ref, i == 0)
        aco[...] += _dot_tn((cb_ref[...].astype(f32) * z).astype(bf16), dyc)
        do_ref[...] = _dot_nt(dya, wao_ref[...])
        aao[...] += _dot_tn(o_ref[...].astype(bf16), dya)

        @pl.when(i == ni - 1)
        def _():
            dwco_ref[...] = aco[...].astype(bf16)
            dwao_ref[...] = aao[...].astype(bf16)
            dwo_ref[...] = ao[...].astype(bf16)

    row = pl.BlockSpec((tm, D), lambda i: (i, 0))
    att = pl.BlockSpec((tm, 512), lambda i: (i, 0))
    full = lambda shp: pl.BlockSpec(shp, lambda i: (0, 0))
    return pl.pallas_call(
        body, name="mix_out_bwd", grid=(ni,),
        out_shape=[SDS((S, D), bf16), SDS((S, 2 * D), bf16), SDS((S, 512), f32),
                   SDS((D, D), bf16), SDS((512, D), bf16), SDS((D, D), bf16), SDS((8, D), f32)],
        in_specs=[row, full((9, D)), row, row, row] + _rest_specs(tm, False) + [att, full((3, D)), full((D, D)), full((512, D)), full((D, D))],
        out_specs=[row, pl.BlockSpec((tm, 2 * D), lambda i: (i, 0)), att, full((D, D)), full((512, D)), full((D, D)), full((8, D))],
        scratch_shapes=[pltpu.VMEM((D, D), f32), pltpu.VMEM((512, D), f32), pltpu.VMEM((D, D), f32)],
        compiler_params=_cp("arbitrary"),
    )(dxo, mod9, outv, yc, ya, *([rest] * 7), o, cw, wco, wao, wo)


def _conv_bwd(dp, rest, cw):
    S = dp.shape[0]
    tm = TM
    per = tm // HALO
    nh = S // HALO
    ni = S // tm

    def body(dp_ref, dpn_ref, cb_ref, cbn_ref, cc_ref, ch_ref, hc_ref, hh_ref, cw_ref, d3_ref, sm_ref):
        i = pl.program_id(0)

        @pl.when(i == 0)
        def _():
            sm_ref[...] = jnp.zeros_like(sm_ref)

        cc, ch = cc_ref[...].astype(f32), ch_ref[...].astype(f32)
        T, z = _conv_z(cc, ch, hc_ref[...], hh_ref[...], cw_ref, i == 0)
        dpv = dp_ref[...].astype(f32)
        cbv = cb_ref[...].astype(f32)
        dz = dpv * cbv
        dzn = jnp.where(i == ni - 1, 0.0, dpn_ref[...].astype(f32) * cbn_ref[...].astype(f32))
        E = jnp.concatenate([dz, dzn], axis=0)
        ne = tm + HALO
        dT = cw_ref[2:3, :] * E + cw_ref[1:2, :] * pltpu.roll(E, ne - 1, 0) + cw_ref[0:1, :] * pltpu.roll(E, ne - 2, 0)
        dT = dT[:tm]
        d3_ref[:, :D] = (dpv * z).astype(bf16)
        d3_ref[:, D:2 * D] = (dT * ch).astype(bf16)
        d3_ref[:, 2 * D:] = (dT * cc).astype(bf16)
        sm_ref[2:3, :] += jnp.sum(dz * T[HALO:], axis=0, keepdims=True)
        sm_ref[1:2, :] += jnp.sum(dz * pltpu.roll(T, 1, 0)[HALO:], axis=0, keepdims=True)
        sm_ref[0:1, :] += jnp.sum(dz * pltpu.roll(T, 2, 0)[HALO:], axis=0, keepdims=True)

    row = pl.BlockSpec((tm, D), lambda i: (i, 0))
    nxt = pl.BlockSpec((HALO, D), lambda i: (jnp.minimum((i + 1) * per, nh - 1), 0))
    col = lambda k: pl.BlockSpec((tm, D), lambda i: (i, k))
    prv = lambda k: pl.BlockSpec((HALO, D), lambda i: (jnp.maximum(i * per - 1, 0), k))
    return pl.pallas_call(
        body, name="conv_bwd", grid=(ni,),
        out_shape=[SDS((S, 3 * D), bf16), SDS((8, D), f32)],
        in_specs=[row, nxt, col(0), nxt, col(1), col(2), prv(1), prv(2), pl.BlockSpec((3, D), lambda i: (0, 0))],
        out_specs=[pl.BlockSpec((tm, 3 * D), lambda i: (i, 0)), pl.BlockSpec((8, D), lambda i: (0, 0))],
        compiler_params=_cp("arbitrary"),
    )(dp, dp, rest, rest, rest, rest, rest, rest, cw)


_DU_RANGES = ((0, 3), (3, 6), (6, 9), (9, 15), (15, 19))
N_CBLK = IN_W // CB


def _mix_in_bwd_dh(dxo, x, mod9, g3, dus, win):
    S = x.shape[0]

    def body(dxo_ref, x_ref, mod_ref, g_ref, s0, s1, s2, s3, s4, w_ref, dxi_ref, sm_ref, acc):
        i, kb = pl.program_id(0), pl.program_id(1)

        @pl.when((i == 0) & (kb == 0))
        def _():
            sm_ref[...] = jnp.zeros_like(sm_ref)

        @pl.when(kb == 0)
        def _():
            acc[...] = jnp.zeros_like(acc)

        for src, (lo, hi) in zip((s0, s1, s2, s3, s4), _DU_RANGES):
            @pl.when((kb >= lo) & (kb < hi))
            def _(src=src):
                acc[...] += _dot_nt(src[...].astype(bf16), w_ref[...])

        @pl.when(kb == N_CBLK - 1)
        def _():
            g, scale = g_ref[1:2, :], mod_ref[4:5, :]
            _, xhat, rstd = _norm_fwd(x_ref[...], g, mod_ref[3:4, :], scale)
            dx, dshift, dscale, dg = _norm_bwd(acc[...], xhat, rstd, g, scale)
            dxi_ref[...] = dxo_ref[...] + dx
            sm_ref[0:1, :] += dshift
            sm_ref[1:2, :] += dscale
            sm_ref[3:4, :] += dg

    row = pl.BlockSpec((TMP, D), lambda i, kb: (i, 0))

    def src_spec(lo, hi):
        return pl.BlockSpec((TMP, CB), lambda i, kb: (i, jnp.clip(kb - lo, 0, hi - lo - 1)))

    return pl.pallas_call(
        body, name="mix_in_bwd_dh", grid=(S // TMP, N_CBLK),
        out_shape=[SDS((S, D), f32), SDS((8, D), f32)],
        in_specs=[row, row, pl.BlockSpec((9, D), lambda i, kb: (0, 0)), pl.BlockSpec((3, D), lambda i, kb: (0, 0))]
                 + [src_spec(lo, hi) for lo, hi in _DU_RANGES] + [pl.BlockSpec((D, CB), lambda i, kb: (0, kb))],
        out_specs=[row, pl.BlockSpec((8, D), lambda i, kb: (0, 0))],
        scratch_shapes=[pltpu.VMEM((TMP, D), f32)],
        compiler_params=_cp("arbitrary", "arbitrary"),
    )(dxo, x, mod9, g3, *dus, win)


def _mix_in_bwd_dw(h, dus):
    S = h.shape[0]
    ni = S // TMW

    def body(h_ref, s0, s1, s2, s3, s4, dw_ref, acc):
        kb, i = pl.program_id(0), pl.program_id(1)

        @pl.when(i == 0)
        def _():
            acc[...] = jnp.zeros_like(acc)

        for src, (lo, hi) in zip((s0, s1, s2, s3, s4), _DU_RANGES):
            @pl.when((kb >= lo) & (kb < hi))
            def _(src=src):
                rows = pl.ds(pl.multiple_of(i * TMW, TMW), TMW)
                acc[...] += _dot_tn(h_ref[rows, :], src[...].astype(bf16))

        @pl.when(i == ni - 1)
        def _():
            dw_ref[...] = acc[...].astype(bf16)

    def src_spec(lo, hi):
        def imap(kb, i):
            on = (kb >= lo) & (kb < hi)
            return (jnp.where(on, i, 0), jnp.clip(kb - lo, 0, hi - lo - 1))
        return pl.BlockSpec((TMW, CB), imap)

    return pl.pallas_call(
        body, name="mix_in_bwd_dw", grid=(N_CBLK, ni),
        out_shape=SDS((D, IN_W), bf16),
        in_specs=[pl.BlockSpec((S, D), lambda kb, i: (0, 0))] + [src_spec(lo, hi) for lo, hi in _DU_RANGES],
        out_specs=pl.BlockSpec((D, CB), lambda kb, i: (0, kb)),
        scratch_shapes=[pltpu.VMEM((D, CB), f32)],
        compiler_params=_cp("arbitrary", "arbitrary"),
    )(h, *dus)


def _loss_head(x, fg, tgt):
    S = x.shape[0]

    def body(x_ref, g_ref, t_ref, ls_ref, dx_ref, sm_ref):
        i = pl.program_id(0)

        @pl.when(i == 0)
        def _():
            ls_ref[...] = jnp.zeros_like(ls_ref)
            sm_ref[...] = jnp.zeros_like(sm_ref)

        xv, g = x_ref[...], g_ref[...]
        rstd = lax.rsqrt(jnp.mean(xv * xv, axis=-1, keepdims=True) + EPS)
        xhat = xv * rstd
        e = xhat * g - t_ref[...]
        ls_ref[...] += 0.5 * jnp.sum(jnp.mean(e * e, axis=-1, keepdims=True))
        dy = e * (1.0 / D)
        sm_ref[0:1, :] += jnp.sum(dy * xhat, axis=0, keepdims=True)
        dxh = dy * g
        dx_ref[...] = rstd * (dxh - xhat * jnp.mean(dxh * xhat, axis=-1, keepdims=True))

    row = pl.BlockSpec((TM, D), lambda i: (i, 0))
    return pl.pallas_call(
        body, name="loss_head", grid=(S // TM,),
        out_shape=[SDS((8, 128), f32), SDS((S, D), f32), SDS((8, D), f32)],
        in_specs=[row, pl.BlockSpec((1, D), lambda i: (0, 0)), row],
        out_specs=[pl.BlockSpec((8, 128), lambda i: (0, 0)), row, pl.BlockSpec((8, D), lambda i: (0, 0))],
        compiler_params=_cp("arbitrary"),
    )(x, fg, tgt)


def _adam(w, g, m, v):
    m2 = B1 * m + (1.0 - B1) * g
    v2 = B2 * v + (1.0 - B2) * (g * g)
    delta = -LR * ((m2 / BC1) / (jnp.sqrt(v2 / BC2) + AEPS) + WD * w)
    return delta, m2, v2


def _row_tile(rows, cols):
    for tr in (512, 352, 256, 128, 64):
        if rows % tr == 0 and tr * cols * 4 <= (5 << 18):
            return tr
    raise ValueError((rows, cols))


def _sum_slots(land):
    _, R, C = land.shape
    tr = _row_tile(R, C)

    def body(l_ref, t_ref):
        t = l_ref[0].astype(f32)
        for k in range(1, N_CHIPS):
            t = t + l_ref[k].astype(f32)
        t_ref[...] = t

    return pl.pallas_call(
        body, name="sum_slots", grid=(R // tr,),
        out_shape=SDS((R, C), f32),
        in_specs=[pl.BlockSpec((N_CHIPS, tr, C), lambda i: (0, i, 0))],
        out_specs=pl.BlockSpec((tr, C), lambda i: (i, 0)),
        compiler_params=_cp("arbitrary"),
    )(land)


def _adamw_pair(w2, m2, v2, ta, tb, outs, slot):
    R, C = ta.shape
    tr = _row_tile(R, C)
    nrt = R // tr

    def body(w_ref, m_ref, v_ref, ta_ref, tb_ref, g_in, d_in, m_in, v_in, g_ref, d_ref, mo_ref, vo_ref):
        g = ta_ref[...].astype(f32) + tb_ref[...].astype(f32)
        delta, mn, vn = _adam(w_ref[...], g, m_ref[...], v_ref[...])
        g_ref[...] = g
        d_ref[...] = delta
        mo_ref[...] = mn
        vo_ref[...] = vn

    big = pl.BlockSpec((tr, C), lambda i: (slot * nrt + i, 0))
    loc = pl.BlockSpec((tr, C), lambda i: (i, 0))
    anyspec = pl.BlockSpec(memory_space=pl.ANY)
    return pl.pallas_call(
        body, name="adamw_pair", grid=(nrt,),
        out_shape=[SDS(o.shape, f32) for o in outs],
        in_specs=[big, big, big, loc, loc] + [anyspec] * 4,
        out_specs=[big] * 4,
        input_output_aliases={5: 0, 6: 1, 7: 2, 8: 3},
        compiler_params=_cp("arbitrary"),
    )(w2, m2, v2, ta, tb, *outs)


def _adamw_small(w, g, m, v):
    def body(w_ref, g_ref, m_ref, v_ref, d_ref, mo_ref, vo_ref):
        delta, mn, vn = _adam(w_ref[...], g_ref[...], m_ref[...], v_ref[...])
        d_ref[...] = delta
        mo_ref[...] = mn
        vo_ref[...] = vn

    return pl.pallas_call(body, name="adamw_small", out_shape=[SDS(w.shape, f32)] * 3)(w, g, m, v)


def _ada_w_update(cs_all, dmod_sh, w, m, v):
    tr = 256

    def body(cs_ref, dm_ref, w_ref, m_ref, v_ref, g_ref, d_ref, mo_ref, vo_ref):
        g = _dot_tn(cs_ref[...].astype(bf16), dm_ref[...].astype(bf16))
        delta, mn, vn = _adam(w_ref[...], g, m_ref[...], v_ref[...])
        g_ref[...] = g
        d_ref[...] = delta
        mo_ref[...] = mn
        vo_ref[...] = vn

    blk = pl.BlockSpec((None, tr, ADA_SH), lambda l, i: (l, i, 0))
    return pl.pallas_call(
        body, name="ada_w_update", grid=(DEPTH, D // tr),
        out_shape=[SDS(w.shape, f32)] * 4,
        in_specs=[pl.BlockSpec((8, tr), lambda l, i: (0, i)), pl.BlockSpec((None, 8, ADA_SH), lambda l, i: (l, 0, 0)), blk, blk, blk],
        out_specs=[blk] * 4,
        compiler_params=_cp("arbitrary", "arbitrary"),
    )(cs_all, dmod_sh, w, m, v)


def _sum_devices(gathered):
    _, R, C = gathered.shape

    def body(g_ref, o_ref):
        t = g_ref[0]
        for k in range(1, 8):
            t = t + g_ref[k]
        o_ref[...] = t

    return pl.pallas_call(body, name="sum_devices", out_shape=SDS((R, C), f32))(gathered)


def _layer_fwd(x, mod9, g3, cw, getw, bias):
    W = {}

    def take(gname, after, mod9):
        w, tok = getw(gname, after)
        W.update(w)
        return mod9 if tok is None else mod9 + tok[0, 0]

    mod9 = take("A", x, mod9)
    x1, h1, a1, u1, hid1, y1 = _ffn_fwd(x, mod9, g3, W["wg0"], W["wu0"], W["wd0"], 0)
    mod9 = take("B", x1, mod9)
    hm, qkv = _mix_qkv(x1, mod9, g3, W["win"])
    rest = _mix_rest(hm, W["win"])
    ogs, lgs = [], []
    for g in range(3):
        og, lg = _attn_fwd(qkv, bias, g)
        ogs.append(og)
        lgs.append(lg)
    mod9 = take("C", ogs[2], mod9)
    x2, o, lse, yc, ya, outv = _mix_out_fwd(x1, mod9, rest, ogs, lgs, cw, W["wco"], W["wao"], W["wo"])
    mod9 = take("D", x2, mod9)
    x3, h3, a3, u3, hid3, y3 = _ffn_fwd(x2, mod9, g3, W["wg1"], W["wu1"], W["wd1"], 2)
    saved = dict(x0=x, x1=x1, x2=x2, h1=h1, a1=a1, u1=u1, hid1=hid1, y1=y1, hm=hm, qkv=qkv, rest=rest, o=o, lse=lse, yc=yc, ya=ya,
                 outv=outv, h3=h3, a3=a3, u3=u3, hid3=hid3, y3=y3)
    return x3, saved, W


def _layer_bwd(dx, sv, mod9, g3, cw, W, bias, emit):
    S = dx.shape[0]
    dw = {}

    def send(gname, mod9):
        tok = emit(gname, dw)
        return mod9 if tok is None else mod9 + tok[0, 0]

    dx2, da, du, dy, sm3 = _ffn_bwd1(dx, sv["x2"], mod9, g3, sv["y3"], sv["a3"], sv["u3"], W["wg1"], W["wu1"], W["wd1"], 2)
    dw["wg1"], dw["wu1"], dw["wd1"] = _ffn_bwd2(sv["h3"], da, du, sv["hid3"], dy)
    mod9 = send("D", mod9)
    dp, dg2, do, dw["wco"], dw["wao"], dw["wo"], smo = _mix_out_bwd(
        dx2, mod9, sv["outv"], sv["yc"], sv["ya"], sv["rest"], sv["o"], cw, W["wco"], W["wao"], W["wo"])
    mod9_c = send("C", mod9)
    cw = cw + (mod9_c - mod9)[0:1, :]
    mod9 = mod9_c
    d3, smc = _conv_bwd(dp, sv["rest"], cw)
    dq = lax.empty((S, 1536), bf16)
    dk = lax.empty((S, 1536), bf16)
    dv = lax.empty((S, 1536), bf16)
    dsaccs = []
    for g in range(3):
        dq, dk, dv, dsg = _attn_bwd(sv["qkv"], do, sv["o"], sv["lse"], bias, dq, dk, dv, g)
        dsaccs.append(dsg)
    dus = (dq, dk, dv, d3, dg2)
    dx1, smm = _mix_in_bwd_dh(dx2, sv["x1"], mod9, g3, dus, W["win"])
    dw["win"] = _mix_in_bwd_dw(sv["hm"], dus)
    mod9 = send("B", mod9)
    dx0, da, du, dy, sm1 = _ffn_bwd1(dx1, sv["x0"], mod9, g3, sv["y1"], sv["a1"], sv["u1"], W["wg0"], W["wu0"], W["wd0"], 0)
    dw["wg0"], dw["wu0"], dw["wd0"] = _ffn_bwd2(sv["h1"], da, du, sv["hid1"], dy)
    send("A", mod9)
    dmod = jnp.concatenate([sm1[0:3], smm[0:2], smo[2:3], sm3[0:3]], axis=0)
    dng = jnp.concatenate([sm1[3:4], smm[3:4], sm3[3:4]], axis=0)
    return dx0, dmod, dng, smc[0:3], jnp.concatenate(dsaccs, axis=0)


def _chip_cols(a, chip, width):
    return lax.dynamic_slice_in_dim(a, chip * width, width, axis=a.ndim - 1)


def kernel(x, c, ada_w, ada_b, norm_g, ffn_w_gate, ffn_w_up, ffn_w_down, w_in, conv_w, w_conv_out, w_attn_out, w_o, rel_bias, final_g, loss_target, m_ada_w, m_ada_b, m_norm_g, m_ffn_w_gate, m_ffn_w_up, m_ffn_w_down, m_w_in, m_conv_w, m_w_conv_out, m_w_attn_out, m_w_o, m_rel_bias, m_final_g, v_ada_w, v_ada_b, v_norm_g, v_ffn_w_gate, v_ffn_w_up, v_ffn_w_down, v_w_in, v_conv_w, v_w_conv_out, v_w_attn_out, v_w_o, v_rel_bias, v_final_g):
    ix, iy, ic = lax.axis_index("x"), lax.axis_index("y"), lax.axis_index("c")
    chip = 2 * ix + iy
    dev = 4 * ix + 2 * iy + ic
    xs = x.reshape(x.shape[1:])
    S = xs.shape[0]
    qd = D // N_CHIPS

    chip_arr = jnp.reshape(chip, (1,)).astype(jnp.int32)
    names = [w[0] for w in WCLASSES]

    tr2 = lambda a: jnp.swapaxes(a, -1, -2)
    wg_t, wu_t = tr2(ffn_w_gate), tr2(ffn_w_up)

    def layer_shards(l):
        return [(wg_t, (l, 0)), (wu_t, (l, 0)), (ffn_w_down, (l, 0)), (wg_t, (l, 1)), (wu_t, (l, 1)),
                (ffn_w_down, (l, 1)), (w_in, (l,)), (w_conv_out, (l,)), (w_attn_out, (l,)), (w_o, (l,))]

    started = {}
    extra_starts = {(0, "A"): [(0, "B")], (0, "B"): [(0, "C"), (0, "D"), (1, "A")]}

    casts = {}

    def cast_group(l, gname, after):
        shards = layer_shards(l)
        casts[(l, gname)] = _gather_group_cast(GROUPS[gname], [shards[q] for q in GROUPS[gname]], chip_arr, after)

    def start_gather(l, gname, after):
        started[(l, gname)] = _gather_group_start(f"l{l}{gname}", GROUPS[gname], casts[(l, gname)], after)
        return started[(l, gname)][-1]

    pad8 = lambda a: jnp.pad(a, ((0, -a.shape[0] % 8), (0, 0)))
    pack = jnp.concatenate([pad8(c), pad8(norm_g.reshape(3, D)), pad8(conv_w.reshape(3, D))], axis=0)
    g1 = _allgather_small(pack).reshape(8, 24, D)
    c_all = g1[:, 0]
    by_chip = g1[0::2]
    ng_full = jnp.concatenate([by_chip[j, 8:11].reshape(DEPTH, 3, qd) for j in range(N_CHIPS)], axis=-1)
    cw_full = jnp.concatenate([by_chip[j, 16:19].reshape(DEPTH, 3, qd) for j in range(N_CHIPS)], axis=-1)
    mod_sh, cs_all = _mod_shards(c_all, ada_w, _chip_cols(ada_b, chip, ADA_SH))
    g2 = _allgather_small(mod_sh.reshape(DEPTH * 8, ADA_SH)).reshape(8, DEPTH, 8, ADA_SH)
    mine = lax.dynamic_index_in_dim(g2[0::2], dev, axis=2, keepdims=False)
    mod = jnp.transpose(mine, (1, 0, 2)).reshape(DEPTH, 9, D)

    cast_group(0, "A", c)
    tok0 = start_gather(0, "A", mod)
    for l in range(DEPTH):
        for gname in GROUPS:
            if (l, gname) not in casts:
                cast_group(l, gname, tok0)
    buckets = jnp.asarray(_bucket_table())
    bias = _bias_blocks(rel_bias, buckets)
    last_cast = casts[(DEPTH - 1, "D")][-1]

    need_order = [(l, gname) for l in range(DEPTH) for gname in GROUPS]
    forwarded = {}

    def forward_gather(key, after):
        forwarded[key] = _gather_group_forward(f"l{key[0]}{key[1]}", GROUPS[key[1]], started[key], after)
        return forwarded[key][-1]

    def make_getw(l):
        def getw(gname, after):
            key = (l, gname)
            if key == (0, "A"):
                after = last_cast
            if key not in forwarded:
                after = forward_gather(key, after)
            full = _gather_group_wait(f"l{l}{gname}", GROUPS[gname], started[key][0], forwarded[key], after)
            tok = None
            before = set(started)
            for nl, ng in extra_starts.get(key, []) + [(l + 1, gname)]:
                if nl < DEPTH and (nl, ng) not in started:
                    tok = start_gather(nl, ng, full[0] if tok is None else tok)
            at = need_order.index(key) + 1
            if at < len(need_order) and need_order[at] in before and need_order[at] not in forwarded:
                tok = forward_gather(need_order[at], full[0] if tok is None else tok)
            return {names[q]: f for q, f in zip(GROUPS[gname], full)}, tok
        return getw

    Ws, saves = [], []
    xc = xs
    for l in range(DEPTH):
        xc, sv, W = _layer_fwd(xc, mod[l], ng_full[l], cw_full[l], make_getw(l), bias)
        Ws.append(W)
        saves.append(sv)

    ls, dx, smf = _loss_head(xc, final_g.reshape(1, D), loss_target.reshape(loss_target.shape[1:]))
    loss = lax.psum(ls[0, 0], ("x", "y", "c"))

    params = dict(wg=wg_t, wu=wu_t, wd=ffn_w_down, win=w_in, wco=w_conv_out, wao=w_attn_out, wo=w_o)
    moms = dict(wg=tr2(m_ffn_w_gate), wu=tr2(m_ffn_w_up), wd=m_ffn_w_down, win=m_w_in, wco=m_w_conv_out, wao=m_w_attn_out, wo=m_w_o)
    vars_ = dict(wg=tr2(v_ffn_w_gate), wu=tr2(v_ffn_w_up), wd=v_ffn_w_down, win=v_w_in, wco=v_w_conv_out, wao=v_w_attn_out, wo=v_w_o)
    flat = lambda a: a.reshape(-1, a.shape[-1])
    big_out = {k: [lax.empty(flat(p).shape, f32) for _ in range(4)] for k, p in params.items()}
    dmods, dngs, dcws, dsaccs = [None] * DEPTH, [None] * DEPTH, [None] * DEPTH, [None] * DEPTH

    def finish(l, gname, started, after):
        group = GROUPS[gname]
        pieces, lands = _scatter_group_wait(f"l{l}{gname}", group, started, after)
        ts = [_sum_own_slots(pieces[i], lands[i], *_cls(q), chip_arr) for i, q in enumerate(group)]
        tsib = _swap_sibling(ts)
        for i, q in enumerate(group):
            name = names[q]
            key = name.rstrip("01")
            slot = 2 * l + int(name[-1]) if name[-1] in "01" else l
            big_out[key] = _adamw_pair(flat(params[key]), flat(moms[key]), flat(vars_[key]), ts[i], tsib[i], big_out[key], slot)

    pending, tok = [], None
    for l in reversed(range(DEPTH)):
        modl = mod[l] if tok is None else mod[l] + tok[0, 0]
        mine = []

        def emit(gname, dw, l=l, mine=mine):
            prev = mine[-1][2][-1] if mine else dx
            mine.append((l, gname, _scatter_group_start(f"l{l}{gname}", GROUPS[gname], [dw[names[q]] for q in GROUPS[gname]], prev)))
            return mine[-1][2][-1]

        dx, dmods[l], dngs[l], dcws[l], dsaccs[l] = _layer_bwd(dx, saves[l], modl, ng_full[l], cw_full[l], Ws[l], bias, emit)
        for pl_, pg, pst in pending:
            finish(pl_, pg, pst, dx)
        pending, tok = mine, mine[-1][2][-1]
    for pl_, pg, pst in pending[:-1]:
        finish(pl_, pg, pst, pending[-1][2][-1])

    drb = jnp.transpose(_bias_grad(dsaccs, buckets)[:, 0, :NUM_BUCKETS])
    drb_row = jnp.pad(drb.reshape(1, NUM_BUCKETS * 24), ((0, 0), (0, D - NUM_BUCKETS * 24)))
    pack2 = jnp.concatenate([pad8(a) for a in dmods] + [pad8(a) for a in dngs] + [pad8(a) for a in dcws] + [smf, pad8(drb_row)], axis=0)
    n_rows = pack2.shape[0]
    g3 = _allgather_small(pack2).reshape(8, n_rows, D)
    tot = _sum_devices(g3)
    o_ng, o_cw, o_fg, o_rb = 16 * DEPTH, 24 * DEPTH, 32 * DEPTH, 32 * DEPTH + 8
    g_ada_b = jnp.stack([tot[16 * l:16 * l + 9] for l in range(DEPTH)]).reshape(DEPTH, 9 * D)
    g_norm_g = _chip_cols(jnp.stack([tot[o_ng + 8 * l:o_ng + 8 * l + 3] for l in range(DEPTH)]), chip, qd)
    g_conv_w = _chip_cols(jnp.stack([tot[o_cw + 8 * l:o_cw + 8 * l + 3] for l in range(DEPTH)]), chip, qd)
    g_final_g = tot[o_fg]
    g_rel_bias = tot[o_rb, :NUM_BUCKETS * 24].reshape(NUM_BUCKETS, 24)
    dmod_all = jnp.stack([g3[:, 16 * l:16 * l + 9].reshape(8, 9 * D) for l in range(DEPTH)])
    dmod_sh = _chip_cols(dmod_all, chip, ADA_SH)
    g_ada_w, d_ada_w, nm_ada_w, nv_ada_w = _ada_w_update(cs_all, dmod_sh, ada_w, m_ada_w, v_ada_w)

    def small(w, g, m, v):
        shp = w.shape
        to2 = lambda a: a.reshape(-1, shp[-1])
        return [o.reshape(shp) for o in _adamw_small(to2(w), to2(g), to2(m), to2(v))]

    d_ada_b, nm_ada_b, nv_ada_b = small(ada_b, g_ada_b, m_ada_b, v_ada_b)
    d_norm_g, nm_norm_g, nv_norm_g = small(norm_g, g_norm_g, m_norm_g, v_norm_g)
    d_conv_w, nm_conv_w, nv_conv_w = small(conv_w, g_conv_w, m_conv_w, v_conv_w)
    d_rel_bias, nm_rel_bias, nv_rel_bias = small(rel_bias, g_rel_bias, m_rel_bias, v_rel_bias)
    d_final_g, nm_final_g, nv_final_g = small(final_g, g_final_g, m_final_g, v_final_g)

    behind = nv_ada_w[0, 0:8, 0:128]
    for key in big_out:
        behind = behind + big_out[key][3][0:8, 0:128]
    finish(*pending[-1], behind)

    def big(key, which):
        out = big_out[key][which].reshape(params[key].shape)
        return tr2(out) if key in ("wg", "wu") else out

    grads = [g_ada_w, g_ada_b, g_norm_g, big("wg", 0), big("wu", 0), big("wd", 0), big("win", 0), g_conv_w, big("wco", 0),
             big("wao", 0), big("wo", 0), g_rel_bias, g_final_g]
    deltas = [d_ada_w, d_ada_b, d_norm_g, big("wg", 1), big("wu", 1), big("wd", 1), big("win", 1), d_conv_w, big("wco", 1),
              big("wao", 1), big("wo", 1), d_rel_bias, d_final_g]
    new_m = [nm_ada_w, nm_ada_b, nm_norm_g, big("wg", 2), big("wu", 2), big("wd", 2), big("win", 2), nm_conv_w, big("wco", 2),
             big("wao", 2), big("wo", 2), nm_rel_bias, nm_final_g]
    new_v = [nv_ada_w, nv_ada_b, nv_norm_g, big("wg", 3), big("wu", 3), big("wd", 3), big("win", 3), nv_conv_w, big("wco", 3),
             big("wao", 3), big("wo", 3), nv_rel_bias, nv_final_g]
    return (loss, dx.reshape(x.shape), *grads, *deltas, *new_m, *new_v)
```

```python
import functools

import numpy as np
import jax
import jax.numpy as jnp
from jax import lax
from jax.experimental import pallas as pl
from jax.experimental.pallas import tpu as pltpu

f32, bf16 = jnp.float32, jnp.bfloat16
SDS = jax.ShapeDtypeStruct
MESH = pl.DeviceIdType.MESH

D = 1024
DEPTH = 4
N_CHIPS = 4
FB = 704
HD = 64
QKV_W = 4608
REST_W = 5120
IN_W = QKV_W + REST_W
WIN_SH = IN_W // N_CHIPS
ADA_SH = 9 * D // N_CHIPS
BLK = 128
DILATIONS = (1, 4, 16)
Q_BLOCKS = (4, 1, 1)
NUM_BUCKETS, MAX_DISTANCE = 32, 2048
EPS = 1e-6
NEG = -1e30
SCALE = HD ** -0.5
LR, B1, B2, AEPS, WD, STEP = 0.001, 0.9, 0.999, 1e-08, 0.01, 10
BC1 = 1.0 - B1 ** STEP
BC2 = 1.0 - B2 ** STEP
VMEM_LIMIT = 56 * 1024 * 1024
TM = 512
TMW = 1024
TMP = 1024
TMF = 1024
SH_STEP = 2
TMX = 256
TMXF = 512
HALO = 16
CB = 512


def _cp(*sem):
    return pltpu.CompilerParams(dimension_semantics=sem if sem else None, vmem_limit_bytes=VMEM_LIMIT)


def _dot(a, b):
    return jnp.dot(a, b, preferred_element_type=f32)


def _dot_nt(a, b):
    return lax.dot_general(a, b, (((1,), (1,)), ((), ())), preferred_element_type=f32)


def _dot_tn(a, b):
    return lax.dot_general(a, b, (((0,), (0,)), ((), ())), preferred_element_type=f32)


def _sigmoid(x):
    return 0.5 * jnp.tanh(0.5 * x) + 0.5


def _norm_fwd(x, g, shift, scale):
    rstd = lax.rsqrt(jnp.mean(x * x, axis=-1, keepdims=True) + EPS)
    xhat = x * rstd
    return xhat * g * (1.0 + scale) + shift, xhat, rstd


def _norm_bwd(dh, xhat, rstd, g, scale):
    dshift = jnp.sum(dh, axis=0, keepdims=True)
    dscale = jnp.sum(dh * xhat * g, axis=0, keepdims=True)
    dg = jnp.sum(dh * xhat * (1.0 + scale), axis=0, keepdims=True)
    dxh = dh * (g * (1.0 + scale))
    dx = rstd * (dxh - xhat * jnp.mean(dxh * xhat, axis=-1, keepdims=True))
    return dx, dshift, dscale, dg


def _allgather_small(xp):
    m_per, n = xp.shape

    def body(x_ref, out_ref, send_sems, recv_sems, local_sem):
        x, y, c = lax.axis_index("x"), lax.axis_index("y"), lax.axis_index("c")
        me, sibling = (x, y, c), (x, y, 1 - c)
        chips = [(1 - x, y), (x, 1 - y), (1 - x, 1 - y)]

        def rows(px, py, pc):
            return out_ref.at[pl.ds((4 * px + 2 * py + pc) * m_per, m_per), :]

        def copy(k, block, to, src=None):
            return pltpu.make_async_remote_copy(
                src_ref=rows(*block) if src is None else src, dst_ref=rows(*block),
                send_sem=send_sems.at[k], recv_sem=recv_sems.at[k], device_id=to, device_id_type=MESH)

        mine = pltpu.make_async_copy(x_ref, rows(*me), local_sem)
        mine.start()
        first = [copy(0, me, sibling, src=x_ref)]
        first += [copy(1 + j, me, (*chip, c), src=x_ref) for j, chip in enumerate(chips)]
        for cp in first:
            cp.start()
        passed = [copy(4 + j, (*chip, c), sibling) for j, chip in enumerate(chips)]
        for j, chip in enumerate(chips):
            copy(1 + j, (*chip, c), me).wait_recv()
            passed[j].start()
        copy(0, sibling, me).wait_recv()
        for j, chip in enumerate(chips):
            copy(4 + j, (*chip, 1 - c), me).wait_recv()
        for cp in first + passed:
            cp.wait_send()
        mine.wait()

    return pl.pallas_call(
        body, name="allgather_small",
        out_shape=SDS((8 * m_per, n), xp.dtype),
        in_specs=[pl.BlockSpec(memory_space=pltpu.VMEM)],
        out_specs=pl.BlockSpec(memory_space=pltpu.VMEM),
        scratch_shapes=[pltpu.SemaphoreType.DMA((7,)), pltpu.SemaphoreType.DMA((7,)), pltpu.SemaphoreType.DMA],
        compiler_params=pltpu.CompilerParams(vmem_limit_bytes=VMEM_LIMIT),
    )(xp)


WCLASSES = (
    ("wg0", "row", (FB, D)), ("wu0", "row", (FB, D)), ("wd0", "row", (FB, D)),
    ("wg1", "row", (FB, D)), ("wu1", "row", (FB, D)), ("wd1", "row", (FB, D)),
    ("win", "col", (D, WIN_SH)), ("wco", "row", (D // N_CHIPS, D)), ("wao", "col", (512, D // N_CHIPS)),
    ("wo", "row", (D // N_CHIPS, D)),
)
NCLS = len(WCLASSES)


def _full_shape(kind, shp):
    if kind == "lead":
        return (N_CHIPS,) + shp
    if kind == "row":
        return (N_CHIPS * shp[0], shp[1])
    return (shp[0], N_CHIPS * shp[1])


def _shard_view(ref, kind, shp, j):
    if kind == "lead":
        return ref.at[j]
    if kind == "row":
        return ref.at[pl.ds(j * shp[0], shp[0]), :]
    return ref.at[:, pl.ds(j * shp[1], shp[1])]


def _half(ref, shp, h):
    hr = shp[0] // 2
    return ref.at[pl.ds(pl.multiple_of(h * hr, 16), hr), :]


def _gather_weights(shards):
    n = NCLS

    def body(*refs):
        ins, outs = refs[:n], refs[n:2 * n]
        send1, recv1, send2, recv2, lsem = refs[2 * n:]
        x, y, c = lax.axis_index("x"), lax.axis_index("y"), lax.axis_index("c")
        chip = 2 * x + y
        sibling = (x, y, 1 - c)

        for mc in range(N_CHIPS):
            @pl.when(chip == mc)
            def _(mc=mc):
                local = []
                for q, (_, kind, shp) in enumerate(WCLASSES):
                    cp = pltpu.make_async_copy(ins[q], _shard_view(outs[q], kind, shp, mc), lsem.at[q])
                    cp.start()
                    local.append(cp)
                sends = []
                for k in (1, 2, 3):
                    pj = mc ^ k
                    for q, (_, kind, shp) in enumerate(WCLASSES):
                        cp = pltpu.make_async_remote_copy(
                            src_ref=_half(ins[q], shp, c), dst_ref=_half(_shard_view(outs[q], kind, shp, mc), shp, c),
                            send_sem=send1.at[q * 3 + k - 1], recv_sem=recv1.at[q * 3 + k - 1],
                            device_id=(pj >> 1, pj & 1, c), device_id_type=MESH)
                        cp.start()
                        sends.append(cp)
                for k in (1, 2, 3):
                    pj = mc ^ k
                    for q, (_, kind, shp) in enumerate(WCLASSES):
                        landed = _half(_shard_view(outs[q], kind, shp, pj), shp, c)
                        pltpu.make_async_remote_copy(
                            src_ref=landed, dst_ref=landed, send_sem=send1.at[q * 3 + k - 1], recv_sem=recv1.at[q * 3 + k - 1],
                            device_id=(pj >> 1, pj & 1, c), device_id_type=MESH).wait_recv()
                        cp = pltpu.make_async_remote_copy(
                            src_ref=landed, dst_ref=landed, send_sem=send2.at[q * 3 + k - 1], recv_sem=recv2.at[q * 3 + k - 1],
                            device_id=sibling, device_id_type=MESH)
                        cp.start()
                        sends.append(cp)
                for k in (1, 2, 3):
                    pj = mc ^ k
                    for q, (_, kind, shp) in enumerate(WCLASSES):
                        other = _half(_shard_view(outs[q], kind, shp, pj), shp, 1 - c)
                        pltpu.make_async_remote_copy(
                            src_ref=other, dst_ref=other, send_sem=send2.at[q * 3 + k - 1], recv_sem=recv2.at[q * 3 + k - 1],
                            device_id=sibling, device_id_type=MESH).wait_recv()
                for cp in sends:
                    cp.wait_send()
                for cp in local:
                    cp.wait()

    anyspec = pl.BlockSpec(memory_space=pl.ANY)
    return pl.pallas_call(
        body, name="gather_weights",
        out_shape=[SDS(_full_shape(kind, shp), bf16) for _, kind, shp in WCLASSES],
        in_specs=[anyspec] * n, out_specs=[anyspec] * n,
        scratch_shapes=[pltpu.SemaphoreType.DMA((3 * n,)), pltpu.SemaphoreType.DMA((3 * n,)),
                        pltpu.SemaphoreType.DMA((3 * n,)), pltpu.SemaphoreType.DMA((3 * n,)),
                        pltpu.SemaphoreType.DMA((n,))],
    )(*shards)


def _scatter_grads(pieces):
    n = NCLS

    def body(*refs):
        ins, outs = refs[:n], refs[n:2 * n]
        send1, recv1, lsem = refs[2 * n:]
        x, y, c = lax.axis_index("x"), lax.axis_index("y"), lax.axis_index("c")
        chip = 2 * x + y

        for mc in range(N_CHIPS):
            @pl.when(chip == mc)
            def _(mc=mc):
                local, sends = [], []
                for q, (_, kind, shp) in enumerate(WCLASSES):
                    cp = pltpu.make_async_copy(_shard_view(ins[q], kind, shp, mc), outs[q].at[0], lsem.at[q])
                    cp.start()
                    local.append(cp)
                for k in (1, 2, 3):
                    pj = mc ^ k
                    for q, (_, kind, shp) in enumerate(WCLASSES):
                        cp = pltpu.make_async_remote_copy(
                            src_ref=_shard_view(ins[q], kind, shp, pj), dst_ref=outs[q].at[k],
                            send_sem=send1.at[q * 3 + k - 1], recv_sem=recv1.at[q * 3 + k - 1],
                            device_id=(pj >> 1, pj & 1, c), device_id_type=MESH)
                        cp.start()
                        sends.append(cp)
                for cp in sends:
                    cp.wait_recv()
                for cp in sends:
                    cp.wait_send()
                for cp in local:
                    cp.wait()

    anyspec = pl.BlockSpec(memory_space=pl.ANY)
    return pl.pallas_call(
        body, name="scatter_grads",
        out_shape=[SDS((N_CHIPS,) + shp, bf16) for _, _, shp in WCLASSES],
        in_specs=[anyspec] * n, out_specs=[anyspec] * n,
        scratch_shapes=[pltpu.SemaphoreType.DMA((3 * n,)), pltpu.SemaphoreType.DMA((3 * n,)), pltpu.SemaphoreType.DMA((n,))],
    )(*pieces)


def _swap_sibling(ts):
    n = len(ts)

    def body(*refs):
        ins, outs = refs[:n], refs[n:2 * n]
        send, recv = refs[2 * n:]
        x, y, c = lax.axis_index("x"), lax.axis_index("y"), lax.axis_index("c")
        cps = []
        for q in range(n):
            cp = pltpu.make_async_remote_copy(src_ref=ins[q], dst_ref=outs[q], send_sem=send.at[q], recv_sem=recv.at[q],
                                              device_id=(x, y, 1 - c), device_id_type=MESH)
            cp.start()
            cps.append(cp)
        for cp in cps:
            cp.wait_recv()
        for cp in cps:
            cp.wait_send()

    anyspec = pl.BlockSpec(memory_space=pl.ANY)
    return pl.pallas_call(
        body, name="swap_sibling",
        out_shape=[SDS(t.shape, t.dtype) for t in ts],
        in_specs=[anyspec] * n, out_specs=[anyspec] * n,
        scratch_shapes=[pltpu.SemaphoreType.DMA((n,)), pltpu.SemaphoreType.DMA((n,))],
    )(*ts)


HBM_SPEC = pl.BlockSpec(memory_space=pltpu.HBM)
SEM_SPEC = pl.BlockSpec(memory_space=pltpu.SEMAPHORE)
ANY_SPEC = pl.BlockSpec(memory_space=pl.ANY)
EFFECT = pltpu.SideEffectType.DATAFLOW_SIDE_EFFECTING
N_COPIES = 3 * NCLS


def _in_hbm(a):
    return pltpu.with_memory_space_constraint(a, pltpu.HBM)


def _chip_index():
    return 2 * lax.axis_index("x") + lax.axis_index("y")


def _place_own(shards):
    n = NCLS

    def body(*refs):
        ins, outs, lsem = refs[:n], refs[n:2 * n], refs[2 * n]
        chip = _chip_index()
        for mc in range(N_CHIPS):
            @pl.when(chip == mc)
            def _(mc=mc):
                cps = [pltpu.make_async_copy(ins[q], _shard_view(outs[q], kind, shp, mc), lsem.at[q])
                       for q, (_, kind, shp) in enumerate(WCLASSES)]
                for cp in cps:
                    cp.start()
                for cp in cps:
                    cp.wait()

    return pl.pallas_call(
        body, name="place_own",
        out_shape=[SDS(_full_shape(kind, shp), bf16) for _, kind, shp in WCLASSES],
        in_specs=[ANY_SPEC] * n, out_specs=[ANY_SPEC] * n,
        scratch_shapes=[pltpu.SemaphoreType.DMA((n,))],
    )(*shards)


def _take_own(pieces):
    n = NCLS

    def body(*refs):
        ins, outs, lsem = refs[:n], refs[n:2 * n], refs[2 * n]
        chip = _chip_index()
        for mc in range(N_CHIPS):
            @pl.when(chip == mc)
            def _(mc=mc):
                cps = [pltpu.make_async_copy(_shard_view(ins[q], kind, shp, mc), outs[q].at[0], lsem.at[q])
                       for q, (_, kind, shp) in enumerate(WCLASSES)]
                for cp in cps:
                    cp.start()
                for cp in cps:
                    cp.wait()

    return pl.pallas_call(
        body, name="take_own",
        out_shape=[SDS((N_CHIPS,) + shp, bf16) for _, _, shp in WCLASSES],
        in_specs=[ANY_SPEC] * n, out_specs=[ANY_SPEC] * n,
        scratch_shapes=[pltpu.SemaphoreType.DMA((n,))],
    )(*pieces)


def _split_start(name, srcs, dsts, after, src_view, dst_view):
    n = NCLS

    def body(*refs):
        src, dst = refs[:n], refs[n:2 * n]
        send, recv = refs[2 * n + 1], refs[2 * n + 2]
        token = refs[-1]
        c = lax.axis_index("c")
        chip = _chip_index()
        for mc in range(N_CHIPS):
            @pl.when(chip == mc)
            def _(mc=mc):
                for k in (1, 2, 3):
                    pj = mc ^ k
                    for q in range(n):
                        pltpu.make_async_remote_copy(
                            src_ref=src_view(src[q], q, mc, pj), dst_ref=dst_view(dst[q], q, mc, k),
                            send_sem=send.at[q * 3 + k - 1], recv_sem=recv.at[q * 3 + k - 1],
                            device_id=(pj >> 1, pj & 1, c), device_id_type=MESH).start()
        token[...] = jnp.zeros_like(token)

    return pl.pallas_call(
        body, name=name,
        out_shape=(pltpu.SemaphoreType.DMA((N_COPIES,)), pltpu.SemaphoreType.DMA((N_COPIES,)),
                   *[pltpu.HBM(a.shape, a.dtype) for a in srcs], *[pltpu.HBM(a.shape, a.dtype) for a in dsts], SDS((8, 128), f32)),
        in_specs=[HBM_SPEC] * (2 * n) + [ANY_SPEC],
        out_specs=(SEM_SPEC, SEM_SPEC, *([HBM_SPEC] * (2 * n)), pl.BlockSpec(memory_space=pltpu.VMEM)),
        input_output_aliases={i: 2 + i for i in range(2 * n)},
        compiler_params=pltpu.CompilerParams(has_side_effects=EFFECT),
    )(*[_in_hbm(a) for a in srcs], *[_in_hbm(a) for a in dsts], after)


def _split_wait(name, started, after, arrival_view):
    n = NCLS
    send, recv = started[0], started[1]
    srcs, dsts = started[2:2 + n], started[2 + n:2 + 2 * n]

    def body(*refs):
        src, dst = refs[:n], refs[n:2 * n]
        send_sem, recv_sem = refs[2 * n], refs[2 * n + 1]
        x, y, c = lax.axis_index("x"), lax.axis_index("y"), lax.axis_index("c")
        for k in (1, 2, 3):
            for q in range(n):
                arrival = arrival_view(dst[q], q, k)
                cp = pltpu.make_async_remote_copy(
                    src_ref=arrival, dst_ref=arrival, send_sem=send_sem.at[q * 3 + k - 1], recv_sem=recv_sem.at[q * 3 + k - 1],
                    device_id=(x, y, 1 - c), device_id_type=MESH)
                cp.wait_send()
                cp.wait_recv()

    out = pl.pallas_call(
        body, name=name,
        out_shape=(*[pltpu.HBM(a.shape, a.dtype) for a in srcs], *[pltpu.HBM(a.shape, a.dtype) for a in dsts]),
        in_specs=[HBM_SPEC] * (2 * n) + [SEM_SPEC, SEM_SPEC, ANY_SPEC],
        out_specs=tuple([HBM_SPEC] * (2 * n)),
        input_output_aliases={i: i for i in range(2 * n)},
        compiler_params=pltpu.CompilerParams(has_side_effects=EFFECT),
    )(*srcs, *dsts, send, recv, after)
    return out[n:]


def _cls(q):
    return WCLASSES[q][1], WCLASSES[q][2]


def _gather_start(shards, after):
    fulls = _place_own(shards)
    return _split_start("gather_start", shards, fulls, after,
                        lambda ref, q, mc, pj: ref,
                        lambda ref, q, mc, k: _shard_view(ref, *_cls(q), mc))


def _gather_wait(started, after):
    return _split_wait("gather_wait", started, after, lambda ref, q, k: _shard_view(ref, *_cls(q), 0))


def _scatter_start(pieces, after):
    lands = _take_own(pieces)
    return _split_start("scatter_start", pieces, lands, after,
                        lambda ref, q, mc, pj: _shard_view(ref, *_cls(q), pj),
                        lambda ref, q, mc, k: ref.at[k])


def _scatter_wait(started, after):
    return _split_wait("scatter_wait", started, after, lambda ref, q, k: ref.at[k])


GROUPS = {"A": (0, 1, 2), "B": (6,), "C": (7, 8, 9), "D": (3, 4, 5)}


def _own_spec(kind, shp, tr):
    R, C = shp
    if kind == "lead":
        return pl.BlockSpec((None, tr, C), lambda i, chip: (chip[0], i, 0))
    if kind == "row":
        return pl.BlockSpec((tr, C), lambda i, chip: (chip[0] * (R // tr) + i, 0))
    return pl.BlockSpec((tr, C), lambda i, chip: (i, chip[0]))


def _cast_place(shards, kind, shp, chip_arr, after):
    n = len(shards)
    R, C = shp
    tr = _row_tile(R, C)

    def body(chip_ref, *refs):
        for q in range(n):
            refs[n + 1 + q][...] = refs[q][...].astype(bf16)

    def in_spec(lead):
        return pl.BlockSpec((None,) * len(lead) + (tr, C), lambda i, chip: (*lead, i, 0))

    return pl.pallas_call(
        body, name="cast_place",
        grid_spec=pltpu.PrefetchScalarGridSpec(
            num_scalar_prefetch=1, grid=(R // tr,),
            in_specs=[in_spec(lead) for _, lead in shards] + [ANY_SPEC],
            out_specs=[_own_spec(kind, shp, tr)] * n),
        out_shape=[SDS(_full_shape(kind, shp), bf16)] * n,
        compiler_params=_cp("arbitrary"),
    )(chip_arr, *[a for a, _ in shards], after)


def _sum_own_slots(piece, land, kind, shp, chip_arr):
    R, C = shp
    tr = _row_tile(R, C)

    def body(chip_ref, p_ref, l_ref, t_ref):
        t = p_ref[...].astype(f32)
        for k in range(N_CHIPS - 1):
            t = t + l_ref[k].astype(f32)
        t_ref[...] = t.astype(bf16)

    return pl.pallas_call(
        body, name="sum_own_slots",
        grid_spec=pltpu.PrefetchScalarGridSpec(
            num_scalar_prefetch=1, grid=(R // tr,),
            in_specs=[_own_spec(kind, shp, tr), pl.BlockSpec((N_CHIPS - 1, tr, C), lambda i, chip: (0, i, 0))],
            out_specs=pl.BlockSpec((tr, C), lambda i, chip: (i, 0))),
        out_shape=SDS((R, C), bf16),
        compiler_params=_cp("arbitrary"),
    )(chip_arr, piece, land)


def _xfer_start(name, arrays, ng, after, src_view, dst_view):
    na = len(arrays)

    def body(*refs):
        arr = refs[:na]
        send, recv, token = refs[na + 1], refs[na + 2], refs[-1]
        c = lax.axis_index("c")
        chip = _chip_index()
        for mc in range(N_CHIPS):
            @pl.when(chip == mc)
            def _(mc=mc):
                for k in (1, 2, 3):
                    pj = mc ^ k
                    for i in range(ng):
                        pltpu.make_async_remote_copy(
                            src_ref=src_view(arr, i, mc, pj), dst_ref=dst_view(arr, i, mc, k),
                            send_sem=send.at[i * 3 + k - 1], recv_sem=recv.at[i * 3 + k - 1],
                            device_id=(pj >> 1, pj & 1, c), device_id_type=MESH).start()
        token[...] = jnp.zeros_like(token)

    return pl.pallas_call(
        body, name=name,
        out_shape=(pltpu.SemaphoreType.DMA((3 * ng,)), pltpu.SemaphoreType.DMA((3 * ng,)),
                   *[pltpu.HBM(a.shape, a.dtype) for a in arrays], SDS((8, 128), f32)),
        in_specs=[HBM_SPEC] * na + [ANY_SPEC],
        out_specs=(SEM_SPEC, SEM_SPEC, *([HBM_SPEC] * na), pl.BlockSpec(memory_space=pltpu.VMEM)),
        input_output_aliases={i: 2 + i for i in range(na)},
        compiler_params=pltpu.CompilerParams(has_side_effects=EFFECT),
    )(*[_in_hbm(a) for a in arrays], after)


def _xfer_wait(name, started, ng, after, arrival_view):
    send, recv = started[0], started[1]
    arrays = started[2:-1]
    na = len(arrays)

    def body(*refs):
        arr = refs[:na]
        send_sem, recv_sem = refs[na], refs[na + 1]
        x, y, c = lax.axis_index("x"), lax.axis_index("y"), lax.axis_index("c")
        for k in (1, 2, 3):
            for i in range(ng):
                arrival = arrival_view(arr, i)
                cp = pltpu.make_async_remote_copy(
                    src_ref=arrival, dst_ref=arrival, send_sem=send_sem.at[i * 3 + k - 1], recv_sem=recv_sem.at[i * 3 + k - 1],
                    device_id=(x, y, 1 - c), device_id_type=MESH)
                cp.wait_send()
                cp.wait_recv()

    return pl.pallas_call(
        body, name=name,
        out_shape=tuple(pltpu.HBM(a.shape, a.dtype) for a in arrays),
        in_specs=[HBM_SPEC] * na + [SEM_SPEC, SEM_SPEC, ANY_SPEC],
        out_specs=tuple([HBM_SPEC] * na),
        input_output_aliases={i: i for i in range(na)},
        compiler_params=pltpu.CompilerParams(has_side_effects=EFFECT),
    )(*arrays, send, recv, after)


def _gather_group_cast(group, shards_f32, chip_arr, after):
    fulls = [None] * len(group)
    by_shape = {}
    for i, q in enumerate(group):
        by_shape.setdefault(_cls(q), []).append(i)
    for (kind, shp), idx in by_shape.items():
        for i, f in zip(idx, _cast_place([shards_f32[i] for i in idx], kind, shp, chip_arr, after)):
            fulls[i] = f
    return fulls


def _gather_group_start(tag, group, fulls, after):
    def view(arr, i, mc, _):
        kind, shp = _cls(group[i])
        return _half(_shard_view(arr[i], kind, shp, mc), shp, lax.axis_index("c"))
    return _xfer_start("gather_start_" + tag, fulls, len(group), after, view, view)


def _gather_group_forward(tag, group, started, after):
    ng = len(group)
    send1, recv1 = started[0], started[1]
    arrays = started[2:-1]
    na = len(arrays)

    def body(*refs):
        arr = refs[:na]
        send_in, recv_in = refs[na], refs[na + 1]
        send2, recv2, token = refs[na + 3], refs[na + 4], refs[-1]
        x, y, c = lax.axis_index("x"), lax.axis_index("y"), lax.axis_index("c")
        chip = _chip_index()
        for mc in range(N_CHIPS):
            @pl.when(chip == mc)
            def _(mc=mc):
                for k in (1, 2, 3):
                    pj = mc ^ k
                    for i in range(ng):
                        kind, shp = _cls(group[i])
                        landed = _half(_shard_view(arr[i], kind, shp, pj), shp, c)
                        pltpu.make_async_remote_copy(
                            src_ref=landed, dst_ref=landed, send_sem=send_in.at[i * 3 + k - 1], recv_sem=recv_in.at[i * 3 + k - 1],
                            device_id=(pj >> 1, pj & 1, c), device_id_type=MESH).wait_recv()
                        pltpu.make_async_remote_copy(
                            src_ref=landed, dst_ref=landed, send_sem=send2.at[i * 3 + k - 1], recv_sem=recv2.at[i * 3 + k - 1],
                            device_id=(x, y, 1 - c), device_id_type=MESH).start()
        token[...] = jnp.zeros_like(token)

    return pl.pallas_call(
        body, name="gather_forward_" + tag,
        out_shape=(pltpu.SemaphoreType.DMA((3 * ng,)), pltpu.SemaphoreType.DMA((3 * ng,)),
                   *[pltpu.HBM(a.shape, a.dtype) for a in arrays], SDS((8, 128), f32)),
        in_specs=[HBM_SPEC] * na + [SEM_SPEC, SEM_SPEC, ANY_SPEC],
        out_specs=(SEM_SPEC, SEM_SPEC, *([HBM_SPEC] * na), pl.BlockSpec(memory_space=pltpu.VMEM)),
        input_output_aliases={i: 2 + i for i in range(na)},
        compiler_params=pltpu.CompilerParams(has_side_effects=EFFECT),
    )(*arrays, send1, recv1, after)


def _gather_group_wait(tag, group, send1, forwarded, after):
    ng = len(group)
    send2, recv2 = forwarded[0], forwarded[1]
    arrays = forwarded[2:-1]
    na = len(arrays)

    def body(*refs):
        arr = refs[:na]
        s1, s2, r2 = refs[na], refs[na + 1], refs[na + 2]
        x, y, c = lax.axis_index("x"), lax.axis_index("y"), lax.axis_index("c")
        for k in (1, 2, 3):
            for i in range(ng):
                kind, shp = _cls(group[i])
                half = _half(_shard_view(arr[i], kind, shp, 0), shp, 0)
                pltpu.make_async_remote_copy(src_ref=half, dst_ref=half, send_sem=s1.at[i * 3 + k - 1], recv_sem=r2.at[i * 3 + k - 1],
                                             device_id=(x, y, 1 - c), device_id_type=MESH).wait_send()
                cp = pltpu.make_async_remote_copy(src_ref=half, dst_ref=half, send_sem=s2.at[i * 3 + k - 1], recv_sem=r2.at[i * 3 + k - 1],
                                                  device_id=(x, y, 1 - c), device_id_type=MESH)
                cp.wait_send()
                cp.wait_recv()

    return pl.pallas_call(
        body, name="gather_wait_" + tag,
        out_shape=tuple(pltpu.HBM(a.shape, a.dtype) for a in arrays),
        in_specs=[HBM_SPEC] * na + [SEM_SPEC, SEM_SPEC, SEM_SPEC, ANY_SPEC],
        out_specs=tuple([HBM_SPEC] * na),
        input_output_aliases={i: i for i in range(na)},
        compiler_params=pltpu.CompilerParams(has_side_effects=EFFECT),
    )(*arrays, send1, send2, recv2, after)


def _scatter_group_start(tag, group, pieces, after):
    ng = len(group)
    lands = [lax.empty((N_CHIPS - 1,) + _cls(q)[1], bf16) for q in group]
    return _xfer_start("scatter_start_" + tag, list(pieces) + lands, ng, after,
                       lambda arr, i, mc, pj: _shard_view(arr[i], *_cls(group[i]), pj),
                       lambda arr, i, mc, k: arr[ng + i].at[k - 1])


def _scatter_group_wait(tag, group, started, after):
    ng = len(group)
    out = _xfer_wait("scatter_wait_" + tag, started, ng, after, lambda arr, i: arr[ng + i].at[0])
    return out[:ng], out[ng:]


def _mod_shards(c_all, ada_w, ada_b_sh):
    tn = ADA_SH // 3

    def body(c_ref, w_ref, b_ref, o_ref, cs_ref):
        cv = c_ref[...]
        cs = cv * _sigmoid(cv)
        cs_ref[...] = cs
        o_ref[...] = _dot(cs.astype(bf16), w_ref[...].astype(bf16)) + b_ref[...]

    return pl.pallas_call(
        body, name="mod_shards", grid=(DEPTH, 3),
        out_shape=[SDS((DEPTH, 8, ADA_SH), f32), SDS((8, D), f32)],
        in_specs=[pl.BlockSpec((8, D), lambda l, t: (0, 0)),
                  pl.BlockSpec((None, D, tn), lambda l, t: (l, 0, t)),
                  pl.BlockSpec((None, 1, tn), lambda l, t: (l, 0, t))],
        out_specs=[pl.BlockSpec((None, 8, tn), lambda l, t: (l, 0, t)), pl.BlockSpec((8, D), lambda l, t: (0, 0))],
        compiler_params=_cp("arbitrary", "arbitrary"),
    )(c_all, ada_w, ada_b_sh.reshape(DEPTH, 1, ADA_SH))


def _t5_bucket(dist):
    exact = NUM_BUCKETS // 2
    dd = np.maximum(dist, 1).astype(np.float32)
    large = exact + (np.log(dd / exact) / np.log(MAX_DISTANCE / exact) * (NUM_BUCKETS - exact)).astype(np.int32)
    large = np.minimum(large, NUM_BUCKETS - 1)
    return np.where(dist < exact, dist, large).astype(np.int32)


def _bucket_table():
    i = np.arange(BLK)[:, None]
    j = np.arange(2 * BLK)[None, :]
    rel = i - j + BLK
    return np.stack([_t5_bucket(np.maximum(rel, 0) * d) for d in DILATIONS]).astype(np.int32)


def _band():
    rel = lax.broadcasted_iota(jnp.int32, (BLK, 2 * BLK), 0) - lax.broadcasted_iota(jnp.int32, (BLK, 2 * BLK), 1) + BLK
    return (rel >= 0) & (rel <= BLK)


def _bias_blocks(rel_bias, buckets):
    def body(tab_ref, bk_ref, o_ref):
        h = pl.program_id(0)
        bk = bk_ref[...]
        acc = jnp.zeros((BLK, 2 * BLK), f32)
        for b in range(NUM_BUCKETS):
            acc = jnp.where(bk == b, tab_ref[b, h], acc)
        o_ref[...] = jnp.where(_band(), acc, NEG)

    return pl.pallas_call(
        body, name="bias_blocks", grid=(24,),
        out_shape=SDS((24, BLK, 2 * BLK), f32),
        in_specs=[pl.BlockSpec(memory_space=pltpu.SMEM), pl.BlockSpec((None, BLK, 2 * BLK), lambda h: (h // 8, 0, 0))],
        out_specs=pl.BlockSpec((None, BLK, 2 * BLK), lambda h: (h, 0, 0)),
        compiler_params=_cp("arbitrary"),
    )(rel_bias, buckets)


def _bias_grad(dsaccs, buckets):
    nl = len(dsaccs)

    def body(*refs):
        bk = refs[nl][...]
        tot = refs[0][...]
        for r in refs[1:nl]:
            tot = tot + r[...]
        lane = lax.broadcasted_iota(jnp.int32, (1, 128), 1)
        row = jnp.zeros((1, 128), f32)
        for b in range(NUM_BUCKETS):
            row = jnp.where(lane == b, jnp.sum(jnp.where(bk == b, tot, 0.0)), row)
        refs[nl + 1][...] = row

    return pl.pallas_call(
        body, name="bias_grad", grid=(24,),
        out_shape=SDS((24, 1, 128), f32),
        in_specs=[pl.BlockSpec((None, BLK, 2 * BLK), lambda h: (h, 0, 0))] * nl
                 + [pl.BlockSpec((None, BLK, 2 * BLK), lambda h: (h // 8, 0, 0))],
        out_specs=pl.BlockSpec((None, 1, 128), lambda h: (h, 0, 0)),
        compiler_params=_cp("arbitrary"),
    )(*dsaccs, buckets)


def _ffn_fwd(x, mod9, g3, wg, wu, wd, sub):
    S = x.shape[0]

    def body(x_ref, mod_ref, g_ref, wg_ref, wu_ref, wd_ref, xo_ref, h_ref, ga_ref, sa_ref, hid_ref, y_ref, acc):
        j = pl.program_id(1)

        @pl.when(j == 0)
        def _():
            h, _, _ = _norm_fwd(x_ref[...], g_ref[sub:sub + 1, :], mod_ref[3 * sub:3 * sub + 1, :], mod_ref[3 * sub + 1:3 * sub + 2, :])
            h_ref[...] = h.astype(bf16)
            acc[...] = jnp.zeros_like(acc)

        h = h_ref[...]
        a = _dot_nt(h, wg_ref[...])
        u = _dot_nt(h, wu_ref[...])
        sg = _sigmoid(a)
        sil = a * sg
        ga_ref[...] = (u * (sg * (1.0 + a * (1.0 - sg)))).astype(bf16)
        sa_ref[...] = sil.astype(bf16)
        hid_ref[...] = (sil * u).astype(bf16)
        acc[...] += _dot(hid_ref[...], wd_ref[...])

        @pl.when(j == N_CHIPS - 1)
        def _():
            y = acc[...]
            y_ref[...] = y.astype(bf16)
            xo_ref[...] = x_ref[...] + 0.5 * mod_ref[3 * sub + 2:3 * sub + 3, :] * y

    row = pl.BlockSpec((TMF, D), lambda i, j: (i, 0))
    hidb = pl.BlockSpec((None, TMF, FB), lambda i, j: (j, i, 0))
    hids = SDS((N_CHIPS, S, FB), bf16)
    return pl.pallas_call(
        body, name="ffn_fwd", grid=(S // TMF, N_CHIPS),
        out_shape=[SDS((S, D), f32), SDS((S, D), bf16), hids, hids, hids, SDS((S, D), bf16)],
        in_specs=[row, pl.BlockSpec((9, D), lambda i, j: (0, 0)), pl.BlockSpec((3, D), lambda i, j: (0, 0)),
                  pl.BlockSpec((FB, D), lambda i, j: (j, 0)), pl.BlockSpec((FB, D), lambda i, j: (j, 0)),
                  pl.BlockSpec((FB, D), lambda i, j: (j, 0))],
        out_specs=[row, row, hidb, hidb, hidb, row],
        scratch_shapes=[pltpu.VMEM((TMF, D), f32)],
        compiler_params=_cp("arbitrary", "arbitrary"),
    )(x, mod9, g3, wg, wu, wd)


def _ffn_bwd1(dxo, x, mod9, g3, y, ga, sa, wg, wu, wd, sub):
    S = x.shape[0]

    def body(dxo_ref, x_ref, mod_ref, g_ref, y_ref, ga_ref, sa_ref, wg_ref, wu_ref, wd_ref,
             dxi_ref, da_ref, du_ref, dy_ref, sm_ref, acc):
        i, j = pl.program_id(0), pl.program_id(1)
        gate = mod_ref[3 * sub + 2:3 * sub + 3, :]

        @pl.when((i == 0) & (j == 0))
        def _():
            sm_ref[...] = jnp.zeros_like(sm_ref)

        @pl.when(j == 0)
        def _():
            dxo_v = dxo_ref[...]
            dy_ref[...] = (0.5 * gate * dxo_v).astype(bf16)
            sm_ref[2:3, :] += jnp.sum(0.5 * y_ref[...].astype(f32) * dxo_v, axis=0, keepdims=True)
            acc[...] = jnp.zeros_like(acc)

        part = None
        for s in range(SH_STEP):
            dhid = _dot_nt(dy_ref[...], wd_ref[s * FB:(s + 1) * FB, :])
            da = (dhid * ga_ref[s].astype(f32)).astype(bf16)
            du = (dhid * sa_ref[s].astype(f32)).astype(bf16)
            da_ref[s] = da
            du_ref[s] = du
            t = _dot(da, wg_ref[s * FB:(s + 1) * FB, :]) + _dot(du, wu_ref[s * FB:(s + 1) * FB, :])
            part = t if part is None else part + t
        acc[...] += part

        @pl.when(j == N_CHIPS // SH_STEP - 1)
        def _():
            g = g_ref[sub:sub + 1, :]
            scale = mod_ref[3 * sub + 1:3 * sub + 2, :]
            _, xhat, rstd = _norm_fwd(x_ref[...], g, mod_ref[3 * sub:3 * sub + 1, :], scale)
            dx, dshift, dscale, dg = _norm_bwd(acc[...], xhat, rstd, g, scale)
            dxi_ref[...] = dxo_ref[...] + dx
            sm_ref[0:1, :] += dshift
            sm_ref[1:2, :] += dscale
            sm_ref[3:4, :] += dg

    row = pl.BlockSpec((TM, D), lambda i, j: (i, 0))
    hidb = pl.BlockSpec((SH_STEP, TM, FB), lambda i, j: (j, i, 0))
    wcol = pl.BlockSpec((SH_STEP * FB, D), lambda i, j: (j, 0))
    return pl.pallas_call(
        body, name="ffn_bwd1", grid=(S // TM, N_CHIPS // SH_STEP),
        out_shape=[SDS((S, D), f32), SDS((N_CHIPS, S, FB), bf16), SDS((N_CHIPS, S, FB), bf16), SDS((S, D), bf16), SDS((8, D), f32)],
        in_specs=[row, row, pl.BlockSpec((9, D), lambda i, j: (0, 0)), pl.BlockSpec((3, D), lambda i, j: (0, 0)), row,
                  hidb, hidb, wcol, wcol, pl.BlockSpec((SH_STEP * FB, D), lambda i, j: (j, 0))],
        out_specs=[row, hidb, hidb, row, pl.BlockSpec((8, D), lambda i, j: (0, 0))],
        scratch_shapes=[pltpu.VMEM((TM, D), f32)],
        compiler_params=_cp("arbitrary", "arbitrary"),
    )(dxo, x, mod9, g3, y, ga, sa, wg, wu, wd)


def _ffn_bwd2(h, da, du, hid, dy):
    S = h.shape[0]
    ni = S // TMW

    def body(h_ref, da_ref, du_ref, hid_ref, dy_ref, dwg_ref, dwu_ref, dwd_ref, ag, au, ad):
        i = pl.program_id(1)

        @pl.when(i == 0)
        def _():
            ag[...] = jnp.zeros_like(ag)
            au[...] = jnp.zeros_like(au)
            ad[...] = jnp.zeros_like(ad)

        hv = h_ref[...]
        ag[...] += _dot_tn(da_ref[...], hv)
        au[...] += _dot_tn(du_ref[...], hv)
        ad[...] += _dot_tn(hid_ref[...], dy_ref[...])

        @pl.when(i == ni - 1)
        def _():
            dwg_ref[...] = ag[...].astype(bf16)
            dwu_ref[...] = au[...].astype(bf16)
            dwd_ref[...] = ad[...].astype(bf16)

    row = pl.BlockSpec((TMW, D), lambda j, i: (i, 0))
    hidb = pl.BlockSpec((None, TMW, FB), lambda j, i: (j, i, 0))
    wrow = pl.BlockSpec((FB, D), lambda j, i: (j, 0))
    return pl.pallas_call(
        body, name="ffn_bwd2", grid=(N_CHIPS, ni),
        out_shape=[SDS((N_CHIPS * FB, D), bf16)] * 3,
        in_specs=[row, hidb, hidb, hidb, row],
        out_specs=[wrow, wrow, wrow],
        scratch_shapes=[pltpu.VMEM((FB, D), f32)] * 3,
        compiler_params=_cp("arbitrary", "arbitrary"),
    )(h, da, du, hid, dy)


def _mix_qkv(x, mod9, g3, win):
    S = x.shape[0]

    def body(x_ref, mod_ref, g_ref, w_ref, h_ref, o_ref):
        @pl.when(pl.program_id(1) == 0)
        def _():
            h, _, _ = _norm_fwd(x_ref[...], g_ref[1:2, :], mod_ref[3:4, :], mod_ref[4:5, :])
            h_ref[...] = h.astype(bf16)

        o_ref[...] = _dot(h_ref[...], w_ref[...]).astype(bf16)

    row = pl.BlockSpec((TMP, D), lambda i, j: (i, 0))
    return pl.pallas_call(
        body, name="mix_qkv", grid=(S // TMP, QKV_W // CB),
        out_shape=[SDS((S, D), bf16), SDS((S, QKV_W), bf16)],
        in_specs=[row, pl.BlockSpec((9, D), lambda i, j: (0, 0)), pl.BlockSpec((3, D), lambda i, j: (0, 0)),
                  pl.BlockSpec((D, CB), lambda i, j: (0, j))],
        out_specs=[row, pl.BlockSpec((TMP, CB), lambda i, j: (i, j))],
        compiler_params=_cp("arbitrary", "arbitrary"),
    )(x, mod9, g3, win)


def _mix_rest(h, win):
    S = h.shape[0]
    off = QKV_W // CB

    def body(h_ref, w_ref, o_ref):
        o_ref[...] = _dot(h_ref[...], w_ref[...]).astype(bf16)

    return pl.pallas_call(
        body, name="mix_rest", grid=(S // TMP, REST_W // CB),
        out_shape=SDS((S, REST_W), bf16),
        in_specs=[pl.BlockSpec((TMP, D), lambda i, j: (i, 0)), pl.BlockSpec((D, CB), lambda i, j: (0, off + j))],
        out_specs=pl.BlockSpec((TMP, CB), lambda i, j: (i, j)),
        compiler_params=_cp("arbitrary", "arbitrary"),
    )(h, win)


def _widen(srcs, dsts):
    for src, dst in zip(srcs, dsts):
        dst[...] = src[...].astype(f32)


def _qkv_scratch(R, Rb):
    return [pltpu.VMEM((R, 128), f32), pltpu.VMEM((R, 128), f32), pltpu.VMEM((Rb, 128), f32),
            pltpu.VMEM((R, 128), f32), pltpu.VMEM((Rb, 128), f32)]


def _attn_fwd(qkv, bias, g):
    S = qkv.shape[0]
    d = DILATIONS[g]
    nq = Q_BLOCKS[g]
    Rb = BLK * d
    R = Rb * nq
    nb = S // R
    qb, kb, vb = 4 * g, 12 + 4 * g, 24 + 4 * g

    def body(q_in, kc_in, kp_in, vc_in, vp_in, b_ref, o_ref, l_ref, q_ref, kc_ref, kp_ref, vc_ref, vp_ref):
        n = pl.program_id(1)
        col = lax.broadcasted_iota(jnp.int32, (BLK, 2 * BLK), 1)
        first = jnp.where((col < BLK) & (n == 0), NEG, 0.0)
        head0 = lax.broadcasted_iota(jnp.int32, (1, 2 * HD), 1) < HD
        _widen((q_in, kc_in, kp_in, vc_in, vp_in), (q_ref, kc_ref, kp_ref, vc_ref, vp_ref))

        def one(b, r):
            sl = pl.ds(b * Rb + r, BLK, stride=d)
            q = q_ref[sl, :]
            if b == 0:
                kp, vp = kp_ref[pl.ds(r, BLK, stride=d), :], vp_ref[pl.ds(r, BLK, stride=d), :]
            else:
                before = pl.ds((b - 1) * Rb + r, BLK, stride=d)
                kp, vp = kc_ref[before, :], vc_ref[before, :]
            kk = jnp.concatenate([kp, kc_ref[sl, :]], axis=0).astype(bf16)
            vv = jnp.concatenate([vp, vc_ref[sl, :]], axis=0).astype(bf16)
            os, ls = [], []
            for hh in range(2):
                qm = jnp.where(head0 if hh == 0 else ~head0, q, 0.0).astype(bf16)
                s = _dot_nt(qm, kk) * SCALE + b_ref[hh]
                if b == 0:
                    s = s + first
                m = jnp.max(s, axis=-1, keepdims=True)
                p = jnp.exp(s - m)
                l = jnp.sum(p, axis=-1, keepdims=True)
                os.append(_dot(p.astype(bf16), vv) / l)
                ls.append(m + jnp.log(l))
            o_ref[sl, :] = jnp.where(head0, os[0], os[1])
            l_ref[sl, :] = jnp.where(head0, ls[0], ls[1])

        for b in range(nq):
            if d == 1:
                one(b, 0)
            else:
                lax.fori_loop(0, d, lambda r, carry, b=b: (one(b, r), carry)[1], 0, unroll=4)

    def blk(cb, prev):
        if prev:
            return pl.BlockSpec((Rb, 128), lambda hp, n: (jnp.maximum(n * nq - 1, 0), cb + hp))
        return pl.BlockSpec((R, 128), lambda hp, n: (n, cb + hp))

    outb = pl.BlockSpec((R, 128), lambda hp, n: (n, hp))
    return pl.pallas_call(
        body, name=f"attn_fwd_d{d}", grid=(4, nb),
        out_shape=[SDS((S, 512), f32), SDS((S, 512), f32)],
        in_specs=[blk(qb, False), blk(kb, False), blk(kb, True), blk(vb, False), blk(vb, True),
                  pl.BlockSpec((2, BLK, 2 * BLK), lambda hp, n: (4 * g + hp, 0, 0))],
        out_specs=[outb, outb],
        scratch_shapes=_qkv_scratch(R, Rb),
        compiler_params=_cp("arbitrary", "arbitrary"),
    )(qkv, qkv, qkv, qkv, qkv, bias)


def _attn_bwd(qkv, do, o, lse, bias, dq_all, dk_all, dv_all, g):
    S = qkv.shape[0]
    d = DILATIONS[g]
    nq = Q_BLOCKS[g]
    Rb = BLK * d
    R = Rb * nq
    nb = S // R
    qb, kb, vb = 4 * g, 12 + 4 * g, 24 + 4 * g

    def body(q_in, kc_in, kp_in, vc_in, vp_in, do_ref, o_ref, l_ref, b_ref, dqi, dki, dvi,
             dq_out, dk_out, dv_out, ds_ref, ck, cv, tk, tv, dq_ref, q_ref, kc_ref, kp_ref, vc_ref, vp_ref):
        n = pl.program_id(1)
        col = lax.broadcasted_iota(jnp.int32, (BLK, 2 * BLK), 1)
        first = jnp.where((col < BLK) & (n == 0), NEG, 0.0)

        @pl.when(n == 0)
        def _():
            ck[...] = jnp.zeros_like(ck)
            cv[...] = jnp.zeros_like(cv)
            ds_ref[...] = jnp.zeros_like(ds_ref)

        @pl.when(n < nb)
        def _():
            head0 = lax.broadcasted_iota(jnp.int32, (1, 2 * HD), 1) < HD
            _widen((q_in, kc_in, kp_in, vc_in, vp_in), (q_ref, kc_ref, kp_ref, vc_ref, vp_ref))

            def one(b, r):
                sl = pl.ds(b * Rb + r, BLK, stride=d)
                before = pl.ds((max(b, 1) - 1) * Rb + r, BLK, stride=d)
                q = q_ref[sl, :]
                if b == 0:
                    kp, vp = kp_ref[pl.ds(r, BLK, stride=d), :], vp_ref[pl.ds(r, BLK, stride=d), :]
                else:
                    kp, vp = kc_ref[before, :], vc_ref[before, :]
                kk = jnp.concatenate([kp, kc_ref[sl, :]], axis=0).astype(bf16)
                vv = jnp.concatenate([vp, vc_ref[sl, :]], axis=0).astype(bf16)
                dov, lv = do_ref[sl, :], l_ref[sl, :]
                prod = dov * o_ref[sl, :]
                qb, dob = q.astype(bf16), dov.astype(bf16)
                dqs, dks, dvs = [], [], []
                for hh in range(2):
                    msk = head0 if hh == 0 else ~head0
                    qm = jnp.where(msk, q, 0.0).astype(bf16)
                    dom = jnp.where(msk, dov, 0.0).astype(bf16)
                    dsum = jnp.sum(jnp.where(msk, prod, 0.0), axis=-1, keepdims=True)
                    s = _dot_nt(qm, kk) * SCALE + b_ref[hh]
                    if b == 0:
                        s = s + first
                    p = jnp.exp(s - lv[:, HD * hh:HD * hh + 1])
                    ds = p * (_dot_nt(dom, vv) - dsum)
                    ds_ref[hh] += ds
                    dsb = ds.astype(bf16)
                    dqs.append(_dot(dsb, kk) * SCALE)
                    dks.append(_dot_tn(dsb, qb) * SCALE)
                    dvs.append(_dot_tn(p.astype(bf16), dob))
                dq_ref[sl, :] = jnp.where(head0, dqs[0], dqs[1])
                dk = jnp.where(head0, dks[0], dks[1])
                dv = jnp.where(head0, dvs[0], dvs[1])
                tk[sl, :] = dk[BLK:]
                tv[sl, :] = dv[BLK:]
                if b == 0:
                    prev_rows = pl.ds((nq - 1) * Rb + r, BLK, stride=d)
                    ck[prev_rows, :] += dk[:BLK]
                    cv[prev_rows, :] += dv[:BLK]
                else:
                    tk[before, :] += dk[:BLK]
                    tv[before, :] += dv[:BLK]

            for b in range(nq):
                if d == 1:
                    one(b, 0)
                else:
                    lax.fori_loop(0, d, lambda r, carry, b=b: (one(b, r), carry)[1], 0, unroll=4)
            dq_out[...] = dq_ref[...].astype(bf16)
            dk_out[...] = ck[...].astype(bf16)
            dv_out[...] = cv[...].astype(bf16)
            ck[...] = tk[...]
            cv[...] = tv[...]

        @pl.when(n == nb)
        def _():
            dk_out[...] = ck[...].astype(bf16)
            dv_out[...] = cv[...].astype(bf16)

    last = nb - 1

    def blk(cb, prev):
        if prev:
            return pl.BlockSpec((Rb, 128), lambda hp, n: (jnp.maximum(jnp.minimum(n, last) * nq - 1, 0), cb + hp))
        return pl.BlockSpec((R, 128), lambda hp, n: (jnp.minimum(n, last), cb + hp))

    cur = pl.BlockSpec((R, 128), lambda hp, n: (jnp.minimum(n, last), hp))
    anyspec = pl.BlockSpec(memory_space=pl.ANY)
    dqo = pl.BlockSpec((R, 128), lambda hp, n: (jnp.minimum(n, last), 4 * g + hp))
    dko = pl.BlockSpec((R, 128), lambda hp, n: (jnp.maximum(n - 1, 0), 4 * g + hp))
    return pl.pallas_call(
        body, name=f"attn_bwd_d{d}", grid=(4, nb + 1),
        out_shape=[SDS((S, 1536), bf16), SDS((S, 1536), bf16), SDS((S, 1536), bf16), SDS((8, BLK, 2 * BLK), f32)],
        in_specs=[blk(qb, False), blk(kb, False), blk(kb, True), blk(vb, False), blk(vb, True), cur, cur, cur,
                  pl.BlockSpec((2, BLK, 2 * BLK), lambda hp, n: (4 * g + hp, 0, 0)), anyspec, anyspec, anyspec],
        out_specs=[dqo, dko, dko, pl.BlockSpec((2, BLK, 2 * BLK), lambda hp, n: (hp, 0, 0))],
        scratch_shapes=[pltpu.VMEM((R, 128), f32)] * 5 + _qkv_scratch(R, Rb),
        input_output_aliases={9: 0, 10: 1, 11: 2},
        compiler_params=_cp("arbitrary", "arbitrary"),
    )(qkv, qkv, qkv, qkv, qkv, do, o, lse, bias, dq_all, dk_all, dv_all)


def _conv_z(cc, ch, hc, hh, cw_ref, first):
    halo = jnp.where(first, 0.0, hc.astype(f32) * hh.astype(f32))
    T = jnp.concatenate([halo, cc * ch], axis=0)
    z = cw_ref[2:3, :] * T + cw_ref[1:2, :] * pltpu.roll(T, 1, 0) + cw_ref[0:1, :] * pltpu.roll(T, 2, 0)
    return T, z[HALO:]


def _rest_specs(tm, with_next):
    per = tm // HALO
    specs = [pl.BlockSpec((tm, D), functools.partial(lambda i, k: (i, k), k=k)) for k in range(5)]
    specs += [pl.BlockSpec((HALO, D), functools.partial(lambda i, k: (jnp.maximum(i * per - 1, 0), k), k=k)) for k in (1, 2)]
    return specs


def _mix_out_fwd(x, mod9, rest, ogs, lgs, cw, wco, wao, wo):
    S = x.shape[0]
    tm = TMXF

    def body(x_ref, mod_ref, cb_ref, cc_ref, ch_ref, gc_ref, ga_ref, hc_ref, hh_ref,
             o0, o1, o2, l0, l1, l2, cw_ref, wco_ref, wao_ref, wo_ref,
             xo_ref, o_ref, lse_ref, yc_ref, ya_ref, out_ref):
        i = pl.program_id(0)
        lv = [l0[...], l1[...], l2[...]]
        mx = jnp.maximum(jnp.maximum(lv[0], lv[1]), lv[2])
        es = [jnp.exp(l - mx) for l in lv]
        den = es[0] + es[1] + es[2]
        o = (es[0] / den) * o0[...] + (es[1] / den) * o1[...] + (es[2] / den) * o2[...]
        o_ref[...] = o
        lse_ref[...] = mx + jnp.log(den)
        _, z = _conv_z(cc_ref[...].astype(f32), ch_ref[...].astype(f32), hc_ref[...], hh_ref[...], cw_ref, i == 0)
        p = (cb_ref[...].astype(f32) * z).astype(bf16)
        yc = _dot(p, wco_ref[...])
        ya = _dot(o.astype(bf16), wao_ref[...])
        yc_ref[...] = yc.astype(bf16)
        ya_ref[...] = ya.astype(bf16)
        merged = _sigmoid(gc_ref[...].astype(f32)) * yc + _sigmoid(ga_ref[...].astype(f32)) * ya
        out = _dot(merged.astype(bf16), wo_ref[...])
        out_ref[...] = out.astype(bf16)
        xo_ref[...] = x_ref[...] + mod_ref[5:6, :] * out

    row = pl.BlockSpec((tm, D), lambda i: (i, 0))
    att = pl.BlockSpec((tm, 512), lambda i: (i, 0))
    full = lambda shp: pl.BlockSpec(shp, lambda i: (0, 0))
    return pl.pallas_call(
        body, name="mix_out_fwd", grid=(S // tm,),
        out_shape=[SDS((S, D), f32), SDS((S, 512), f32), SDS((S, 512), f32), SDS((S, D), bf16), SDS((S, D), bf16), SDS((S, D), bf16)],
        in_specs=[row, full((9, D))] + _rest_specs(tm, False) + [att] * 6 + [full((3, D)), full((D, D)), full((512, D)), full((D, D))],
        out_specs=[row, att, att, row, row, row],
        compiler_params=_cp("arbitrary"),
    )(x, mod9, *([rest] * 7), *ogs, *lgs, cw, wco, wao, wo)


def _mix_out_bwd(dxo, mod9, outv, yc, ya, rest, o, cw, wco, wao, wo):
    S = dxo.shape[0]
    tm = TMX
    ni = S // tm

    def body(dxo_ref, mod_ref, out_ref, yc_ref, ya_ref, cb_ref, cc_ref, ch_ref, gc_ref, ga_ref, hc_ref, hh_ref,
             o_ref, cw_ref, wco_ref, wao_ref, wo_ref,
             dp_ref, dg2_ref, do_ref, dwco_ref, dwao_ref, dwo_ref, sm_ref, aco, aao, ao):
        i = pl.program_id(0)

        @pl.when(i == 0)
        def _():
            sm_ref[...] = jnp.zeros_like(sm_ref)
            aco[...] = jnp.zeros_like(aco)
            aao[...] = jnp.zeros_like(aao)
            ao[...] = jnp.zeros_like(ao)

        dxo_v = dxo_ref[...]
        sm_ref[2:3, :] += jnp.sum(out_ref[...].astype(f32) * dxo_v, axis=0, keepdims=True)
        dout = (mod_ref[5:6, :] * dxo_v).astype(bf16)
        dmerged = _dot_nt(dout, wo_ref[...])
        sc, sa = _sigmoid(gc_ref[...].astype(f32)), _sigmoid(ga_ref[...].astype(f32))
        ycv, yav = yc_ref[...].astype(f32), ya_ref[...].astype(f32)
        ao[...] += _dot_tn((sc * ycv + sa * yav).astype(bf16), dout)
        dyc = (dmerged * sc).astype(bf16)
        dya = (dmerged * sa).astype(bf16)
        dg2_ref[:, :D] = (dmerged * ycv * sc * (1.0 - sc)).astype(bf16)
        dg2_ref[:, D:] = (dmerged * yav * sa * (1.0 - sa)).astype(bf16)
        dp_ref[...] = _dot_nt(dyc, wco_ref[...]).astype(bf16)
        _, z = _conv_z(cc_ref[...].astype(f32), ch_ref[...].astype(f32), hc_ref[...], hh_ref[...], cw_ref, i == 0)
        aco[...] += _dot_tn((cb_ref[...].astype(f32) * z).astype(bf16), dyc)
        do_ref[...] = _dot_nt(dya, wao_ref[...])
        aao[...] += _dot_tn(o_ref[...].astype(bf16), dya)

        @pl.when(i == ni - 1)
        def _():
            dwco_ref[...] = aco[...].astype(bf16)
            dwao_ref[...] = aao[...].astype(bf16)
            dwo_ref[...] = ao[...].astype(bf16)

    row = pl.BlockSpec((tm, D), lambda i: (i, 0))
    att = pl.BlockSpec((tm, 512), lambda i: (i, 0))
    full = lambda shp: pl.BlockSpec(shp, lambda i: (0, 0))
    return pl.pallas_call(
        body, name="mix_out_bwd", grid=(ni,),
        out_shape=[SDS((S, D), bf16), SDS((S, 2 * D), bf16), SDS((S, 512), f32),
                   SDS((D, D), bf16), SDS((512, D), bf16), SDS((D, D), bf16), SDS((8, D), f32)],
        in_specs=[row, full((9, D)), row, row, row] + _rest_specs(tm, False) + [att, full((3, D)), full((D, D)), full((512, D)), full((D, D))],
        out_specs=[row, pl.BlockSpec((tm, 2 * D), lambda i: (i, 0)), att, full((D, D)), full((512, D)), full((D, D)), full((8, D))],
        scratch_shapes=[pltpu.VMEM((D, D), f32), pltpu.VMEM((512, D), f32), pltpu.VMEM((D, D), f32)],
        compiler_params=_cp("arbitrary"),
    )(dxo, mod9, outv, yc, ya, *([rest] * 7), o, cw, wco, wao, wo)


def _conv_bwd(dp, rest, cw):
    S = dp.shape[0]
    tm = TM
    per = tm // HALO
    nh = S // HALO
    ni = S // tm

    def body(dp_ref, dpn_ref, cb_ref, cbn_ref, cc_ref, ch_ref, hc_ref, hh_ref, cw_ref, d3_ref, sm_ref):
        i = pl.program_id(0)

        @pl.when(i == 0)
        def _():
            sm_ref[...] = jnp.zeros_like(sm_ref)

        cc, ch = cc_ref[...].astype(f32), ch_ref[...].astype(f32)
        T, z = _conv_z(cc, ch, hc_ref[...], hh_ref[...], cw_ref, i == 0)
        dpv = dp_ref[...].astype(f32)
        cbv = cb_ref[...].astype(f32)
        dz = dpv * cbv
        dzn = jnp.where(i == ni - 1, 0.0, dpn_ref[...].astype(f32) * cbn_ref[...].astype(f32))
        E = jnp.concatenate([dz, dzn], axis=0)
        ne = tm + HALO
        dT = cw_ref[2:3, :] * E + cw_ref[1:2, :] * pltpu.roll(E, ne - 1, 0) + cw_ref[0:1, :] * pltpu.roll(E, ne - 2, 0)
        dT = dT[:tm]
        d3_ref[:, :D] = (dpv * z).astype(bf16)
        d3_ref[:, D:2 * D] = (dT * ch).astype(bf16)
        d3_ref[:, 2 * D:] = (dT * cc).astype(bf16)
        sm_ref[2:3, :] += jnp.sum(dz * T[HALO:], axis=0, keepdims=True)
        sm_ref[1:2, :] += jnp.sum(dz * pltpu.roll(T, 1, 0)[HALO:], axis=0, keepdims=True)
        sm_ref[0:1, :] += jnp.sum(dz * pltpu.roll(T, 2, 0)[HALO:], axis=0, keepdims=True)

    row = pl.BlockSpec((tm, D), lambda i: (i, 0))
    nxt = pl.BlockSpec((HALO, D), lambda i: (jnp.minimum((i + 1) * per, nh - 1), 0))
    col = lambda k: pl.BlockSpec((tm, D), lambda i: (i, k))
    prv = lambda k: pl.BlockSpec((HALO, D), lambda i: (jnp.maximum(i * per - 1, 0), k))
    return pl.pallas_call(
        body, name="conv_bwd", grid=(ni,),
        out_shape=[SDS((S, 3 * D), bf16), SDS((8, D), f32)],
        in_specs=[row, nxt, col(0), nxt, col(1), col(2), prv(1), prv(2), pl.BlockSpec((3, D), lambda i: (0, 0))],
        out_specs=[pl.BlockSpec((tm, 3 * D), lambda i: (i, 0)), pl.BlockSpec((8, D), lambda i: (0, 0))],
        compiler_params=_cp("arbitrary"),
    )(dp, dp, rest, rest, rest, rest, rest, rest, cw)


_DU_RANGES = ((0, 3), (3, 6), (6, 9), (9, 15), (15, 19))
N_CBLK = IN_W // CB


def _mix_in_bwd_dh(dxo, x, mod9, g3, dus, win):
    S = x.shape[0]

    def body(dxo_ref, x_ref, mod_ref, g_ref, s0, s1, s2, s3, s4, w_ref, dxi_ref, sm_ref, acc):
        i, kb = pl.program_id(0), pl.program_id(1)

        @pl.when((i == 0) & (kb == 0))
        def _():
            sm_ref[...] = jnp.zeros_like(sm_ref)

        @pl.when(kb == 0)
        def _():
            acc[...] = jnp.zeros_like(acc)

        for src, (lo, hi) in zip((s0, s1, s2, s3, s4), _DU_RANGES):
            @pl.when((kb >= lo) & (kb < hi))
            def _(src=src):
                acc[...] += _dot_nt(src[...].astype(bf16), w_ref[...])

        @pl.when(kb == N_CBLK - 1)
        def _():
            g, scale = g_ref[1:2, :], mod_ref[4:5, :]
            _, xhat, rstd = _norm_fwd(x_ref[...], g, mod_ref[3:4, :], scale)
            dx, dshift, dscale, dg = _norm_bwd(acc[...], xhat, rstd, g, scale)
            dxi_ref[...] = dxo_ref[...] + dx
            sm_ref[0:1, :] += dshift
            sm_ref[1:2, :] += dscale
            sm_ref[3:4, :] += dg

    row = pl.BlockSpec((TMP, D), lambda i, kb: (i, 0))

    def src_spec(lo, hi):
        return pl.BlockSpec((TMP, CB), lambda i, kb: (i, jnp.clip(kb - lo, 0, hi - lo - 1)))

    return pl.pallas_call(
        body, name="mix_in_bwd_dh", grid=(S // TMP, N_CBLK),
        out_shape=[SDS((S, D), f32), SDS((8, D), f32)],
        in_specs=[row, row, pl.BlockSpec((9, D), lambda i, kb: (0, 0)), pl.BlockSpec((3, D), lambda i, kb: (0, 0))]
                 + [src_spec(lo, hi) for lo, hi in _DU_RANGES] + [pl.BlockSpec((D, CB), lambda i, kb: (0, kb))],
        out_specs=[row, pl.BlockSpec((8, D), lambda i, kb: (0, 0))],
        scratch_shapes=[pltpu.VMEM((TMP, D), f32)],
        compiler_params=_cp("arbitrary", "arbitrary"),
    )(dxo, x, mod9, g3, *dus, win)


def _mix_in_bwd_dw(h, dus):
    S = h.shape[0]
    ni = S // TMW

    def body(h_ref, s0, s1, s2, s3, s4, dw_ref, acc):
        kb, i = pl.program_id(0), pl.program_id(1)

        @pl.when(i == 0)
        def _():
            acc[...] = jnp.zeros_like(acc)

        for src, (lo, hi) in zip((s0, s1, s2, s3, s4), _DU_RANGES):
            @pl.when((kb >= lo) & (kb < hi))
            def _(src=src):
                rows = pl.ds(pl.multiple_of(i * TMW, TMW), TMW)
                acc[...] += _dot_tn(h_ref[rows, :], src[...].astype(bf16))

        @pl.when(i == ni - 1)
        def _():
            dw_ref[...] = acc[...].astype(bf16)

    def src_spec(lo, hi):
        def imap(kb, i):
            on = (kb >= lo) & (kb < hi)
            return (jnp.where(on, i, 0), jnp.clip(kb - lo, 0, hi - lo - 1))
        return pl.BlockSpec((TMW, CB), imap)

    return pl.pallas_call(
        body, name="mix_in_bwd_dw", grid=(N_CBLK, ni),
        out_shape=SDS((D, IN_W), bf16),
        in_specs=[pl.BlockSpec((S, D), lambda kb, i: (0, 0))] + [src_spec(lo, hi) for lo, hi in _DU_RANGES],
        out_specs=pl.BlockSpec((D, CB), lambda kb, i: (0, kb)),
        scratch_shapes=[pltpu.VMEM((D, CB), f32)],
        compiler_params=_cp("arbitrary", "arbitrary"),
    )(h, *dus)


def _loss_head(x, fg, tgt):
    S = x.shape[0]

    def body(x_ref, g_ref, t_ref, ls_ref, dx_ref, sm_ref):
        i = pl.program_id(0)

        @pl.when(i == 0)
        def _():
            ls_ref[...] = jnp.zeros_like(ls_ref)
            sm_ref[...] = jnp.zeros_like(sm_ref)

        xv, g = x_ref[...], g_ref[...]
        rstd = lax.rsqrt(jnp.mean(xv * xv, axis=-1, keepdims=True) + EPS)
        xhat = xv * rstd
        e = xhat * g - t_ref[...]
        ls_ref[...] += 0.5 * jnp.sum(jnp.mean(e * e, axis=-1, keepdims=True))
        dy = e * (1.0 / D)
        sm_ref[0:1, :] += jnp.sum(dy * xhat, axis=0, keepdims=True)
        dxh = dy * g
        dx_ref[...] = rstd * (dxh - xhat * jnp.mean(dxh * xhat, axis=-1, keepdims=True))

    row = pl.BlockSpec((TM, D), lambda i: (i, 0))
    return pl.pallas_call(
        body, name="loss_head", grid=(S // TM,),
        out_shape=[SDS((8, 128), f32), SDS((S, D), f32), SDS((8, D), f32)],
        in_specs=[row, pl.BlockSpec((1, D), lambda i: (0, 0)), row],
        out_specs=[pl.BlockSpec((8, 128), lambda i: (0, 0)), row, pl.BlockSpec((8, D), lambda i: (0, 0))],
        compiler_params=_cp("arbitrary"),
    )(x, fg, tgt)


def _adam(w, g, m, v):
    m2 = B1 * m + (1.0 - B1) * g
    v2 = B2 * v + (1.0 - B2) * (g * g)
    delta = -LR * ((m2 / BC1) / (jnp.sqrt(v2 / BC2) + AEPS) + WD * w)
    return delta, m2, v2


def _row_tile(rows, cols):
    for tr in (512, 352, 256, 128, 64):
        if rows % tr == 0 and tr * cols * 4 <= (3 << 19):
            return tr
    raise ValueError((rows, cols))


def _sum_slots(land):
    _, R, C = land.shape
    tr = _row_tile(R, C)

    def body(l_ref, t_ref):
        t = l_ref[0].astype(f32)
        for k in range(1, N_CHIPS):
            t = t + l_ref[k].astype(f32)
        t_ref[...] = t

    return pl.pallas_call(
        body, name="sum_slots", grid=(R // tr,),
        out_shape=SDS((R, C), f32),
        in_specs=[pl.BlockSpec((N_CHIPS, tr, C), lambda i: (0, i, 0))],
        out_specs=pl.BlockSpec((tr, C), lambda i: (i, 0)),
        compiler_params=_cp("arbitrary"),
    )(land)


def _adamw_pair(w2, m2, v2, ta, tb, outs, slot):
    R, C = ta.shape
    tr = _row_tile(R, C)
    nrt = R // tr

    def body(w_ref, m_ref, v_ref, ta_ref, tb_ref, g_in, d_in, m_in, v_in, g_ref, d_ref, mo_ref, vo_ref):
        g = ta_ref[...].astype(f32) + tb_ref[...].astype(f32)
        delta, mn, vn = _adam(w_ref[...], g, m_ref[...], v_ref[...])
        g_ref[...] = g
        d_ref[...] = delta
        mo_ref[...] = mn
        vo_ref[...] = vn

    big = pl.BlockSpec((tr, C), lambda i: (slot * nrt + i, 0))
    loc = pl.BlockSpec((tr, C), lambda i: (i, 0))
    anyspec = pl.BlockSpec(memory_space=pl.ANY)
    return pl.pallas_call(
        body, name="adamw_pair", grid=(nrt,),
        out_shape=[SDS(o.shape, f32) for o in outs],
        in_specs=[big, big, big, loc, loc] + [anyspec] * 4,
        out_specs=[big] * 4,
        input_output_aliases={5: 0, 6: 1, 7: 2, 8: 3},
        compiler_params=_cp("arbitrary"),
    )(w2, m2, v2, ta, tb, *outs)


def _adamw_small(w, g, m, v):
    def body(w_ref, g_ref, m_ref, v_ref, d_ref, mo_ref, vo_ref):
        delta, mn, vn = _adam(w_ref[...], g_ref[...], m_ref[...], v_ref[...])
        d_ref[...] = delta
        mo_ref[...] = mn
        vo_ref[...] = vn

    return pl.pallas_call(body, name="adamw_small", out_shape=[SDS(w.shape, f32)] * 3)(w, g, m, v)


def _ada_w_update(cs_all, dmod_sh, w, m, v):
    tr = 256

    def body(cs_ref, dm_ref, w_ref, m_ref, v_ref, g_ref, d_ref, mo_ref, vo_ref):
        g = _dot_tn(cs_ref[...].astype(bf16), dm_ref[...].astype(bf16))
        delta, mn, vn = _adam(w_ref[...], g, m_ref[...], v_ref[...])
        g_ref[...] = g
        d_ref[...] = delta
        mo_ref[...] = mn
        vo_ref[...] = vn

    blk = pl.BlockSpec((None, tr, ADA_SH), lambda l, i: (l, i, 0))
    return pl.pallas_call(
        body, name="ada_w_update", grid=(DEPTH, D // tr),
        out_shape=[SDS(w.shape, f32)] * 4,
        in_specs=[pl.BlockSpec((8, tr), lambda l, i: (0, i)), pl.BlockSpec((None, 8, ADA_SH), lambda l, i: (l, 0, 0)), blk, blk, blk],
        out_specs=[blk] * 4,
        compiler_params=_cp("arbitrary", "arbitrary"),
    )(cs_all, dmod_sh, w, m, v)


def _sum_devices(gathered):
    _, R, C = gathered.shape

    def body(g_ref, o_ref):
        t = g_ref[0]
        for k in range(1, 8):
            t = t + g_ref[k]
        o_ref[...] = t

    return pl.pallas_call(body, name="sum_devices", out_shape=SDS((R, C), f32))(gathered)


def _layer_fwd(x, mod9, g3, cw, getw, bias):
    W = {}

    def take(gname, after, mod9):
        w, tok = getw(gname, after)
        W.update(w)
        return mod9 if tok is None else mod9 + tok[0, 0]

    mod9 = take("A", x, mod9)
    x1, h1, a1, u1, hid1, y1 = _ffn_fwd(x, mod9, g3, W["wg0"], W["wu0"], W["wd0"], 0)
    mod9 = take("B", x1, mod9)
    hm, qkv = _mix_qkv(x1, mod9, g3, W["win"])
    rest = _mix_rest(hm, W["win"])
    ogs, lgs = [], []
    for g in range(3):
        og, lg = _attn_fwd(qkv, bias, g)
        ogs.append(og)
        lgs.append(lg)
    mod9 = take("C", ogs[2], mod9)
    x2, o, lse, yc, ya, outv = _mix_out_fwd(x1, mod9, rest, ogs, lgs, cw, W["wco"], W["wao"], W["wo"])
    mod9 = take("D", x2, mod9)
    x3, h3, a3, u3, hid3, y3 = _ffn_fwd(x2, mod9, g3, W["wg1"], W["wu1"], W["wd1"], 2)
    saved = dict(x0=x, x1=x1, x2=x2, h1=h1, a1=a1, u1=u1, hid1=hid1, y1=y1, hm=hm, qkv=qkv, rest=rest, o=o, lse=lse, yc=yc, ya=ya,
                 outv=outv, h3=h3, a3=a3, u3=u3, hid3=hid3, y3=y3)
    return x3, saved, W


def _layer_bwd(dx, sv, mod9, g3, cw, W, bias, emit):
    S = dx.shape[0]
    dw = {}

    def send(gname, mod9):
        tok = emit(gname, dw)
        return mod9 if tok is None else mod9 + tok[0, 0]

    dx2, da, du, dy, sm3 = _ffn_bwd1(dx, sv["x2"], mod9, g3, sv["y3"], sv["a3"], sv["u3"], W["wg1"], W["wu1"], W["wd1"], 2)
    dw["wg1"], dw["wu1"], dw["wd1"] = _ffn_bwd2(sv["h3"], da, du, sv["hid3"], dy)
    mod9 = send("D", mod9)
    dp, dg2, do, dw["wco"], dw["wao"], dw["wo"], smo = _mix_out_bwd(
        dx2, mod9, sv["outv"], sv["yc"], sv["ya"], sv["rest"], sv["o"], cw, W["wco"], W["wao"], W["wo"])
    mod9_c = send("C", mod9)
    cw = cw + (mod9_c - mod9)[0:1, :]
    mod9 = mod9_c
    d3, smc = _conv_bwd(dp, sv["rest"], cw)
    dq = lax.empty((S, 1536), bf16)
    dk = lax.empty((S, 1536), bf16)
    dv = lax.empty((S, 1536), bf16)
    dsaccs = []
    for g in range(3):
        dq, dk, dv, dsg = _attn_bwd(sv["qkv"], do, sv["o"], sv["lse"], bias, dq, dk, dv, g)
        dsaccs.append(dsg)
    dus = (dq, dk, dv, d3, dg2)
    dx1, smm = _mix_in_bwd_dh(dx2, sv["x1"], mod9, g3, dus, W["win"])
    dw["win"] = _mix_in_bwd_dw(sv["hm"], dus)
    mod9 = send("B", mod9)
    dx0, da, du, dy, sm1 = _ffn_bwd1(dx1, sv["x0"], mod9, g3, sv["y1"], sv["a1"], sv["u1"], W["wg0"], W["wu0"], W["wd0"], 0)
    dw["wg0"], dw["wu0"], dw["wd0"] = _ffn_bwd2(sv["h1"], da, du, sv["hid1"], dy)
    send("A", mod9)
    dmod = jnp.concatenate([sm1[0:3], smm[0:2], smo[2:3], sm3[0:3]], axis=0)
    dng = jnp.concatenate([sm1[3:4], smm[3:4], sm3[3:4]], axis=0)
    return dx0, dmod, dng, smc[0:3], jnp.concatenate(dsaccs, axis=0)


def _chip_cols(a, chip, width):
    return lax.dynamic_slice_in_dim(a, chip * width, width, axis=a.ndim - 1)


def kernel(x, c, ada_w, ada_b, norm_g, ffn_w_gate, ffn_w_up, ffn_w_down, w_in, conv_w, w_conv_out, w_attn_out, w_o, rel_bias, final_g, loss_target, m_ada_w, m_ada_b, m_norm_g, m_ffn_w_gate, m_ffn_w_up, m_ffn_w_down, m_w_in, m_conv_w, m_w_conv_out, m_w_attn_out, m_w_o, m_rel_bias, m_final_g, v_ada_w, v_ada_b, v_norm_g, v_ffn_w_gate, v_ffn_w_up, v_ffn_w_down, v_w_in, v_conv_w, v_w_conv_out, v_w_attn_out, v_w_o, v_rel_bias, v_final_g):
    ix, iy, ic = lax.axis_index("x"), lax.axis_index("y"), lax.axis_index("c")
    chip = 2 * ix + iy
    dev = 4 * ix + 2 * iy + ic
    xs = x.reshape(x.shape[1:])
    S = xs.shape[0]
    qd = D // N_CHIPS

    chip_arr = jnp.reshape(chip, (1,)).astype(jnp.int32)
    names = [w[0] for w in WCLASSES]

    tr2 = lambda a: jnp.swapaxes(a, -1, -2)
    wg_t, wu_t = tr2(ffn_w_gate), tr2(ffn_w_up)

    def layer_shards(l):
        return [(wg_t, (l, 0)), (wu_t, (l, 0)), (ffn_w_down, (l, 0)), (wg_t, (l, 1)), (wu_t, (l, 1)),
                (ffn_w_down, (l, 1)), (w_in, (l,)), (w_conv_out, (l,)), (w_attn_out, (l,)), (w_o, (l,))]

    started = {}
    extra_starts = {(0, "A"): [(0, "B")], (0, "B"): [(0, "C"), (0, "D"), (1, "A")]}

    casts = {}

    def cast_group(l, gname, after):
        shards = layer_shards(l)
        casts[(l, gname)] = _gather_group_cast(GROUPS[gname], [shards[q] for q in GROUPS[gname]], chip_arr, after)

    def start_gather(l, gname, after):
        started[(l, gname)] = _gather_group_start(f"l{l}{gname}", GROUPS[gname], casts[(l, gname)], after)
        return started[(l, gname)][-1]

    pad8 = lambda a: jnp.pad(a, ((0, -a.shape[0] % 8), (0, 0)))
    pack = jnp.concatenate([pad8(c), pad8(norm_g.reshape(3, D)), pad8(conv_w.reshape(3, D))], axis=0)
    g1 = _allgather_small(pack).reshape(8, 24, D)
    c_all = g1[:, 0]
    by_chip = g1[0::2]
    ng_full = jnp.concatenate([by_chip[j, 8:11].reshape(DEPTH, 3, qd) for j in range(N_CHIPS)], axis=-1)
    cw_full = jnp.concatenate([by_chip[j, 16:19].reshape(DEPTH, 3, qd) for j in range(N_CHIPS)], axis=-1)
    mod_sh, cs_all = _mod_shards(c_all, ada_w, _chip_cols(ada_b, chip, ADA_SH))
    g2 = _allgather_small(mod_sh.reshape(DEPTH * 8, ADA_SH)).reshape(8, DEPTH, 8, ADA_SH)
    mine = lax.dynamic_index_in_dim(g2[0::2], dev, axis=2, keepdims=False)
    mod = jnp.transpose(mine, (1, 0, 2)).reshape(DEPTH, 9, D)

    cast_group(0, "A", c)
    tok0 = start_gather(0, "A", mod)
    for l in range(DEPTH):
        for gname in GROUPS:
            if (l, gname) not in casts:
                cast_group(l, gname, tok0)
    buckets = jnp.asarray(_bucket_table())
    bias = _bias_blocks(rel_bias, buckets)
    last_cast = casts[(DEPTH - 1, "D")][-1]

    need_order = [(l, gname) for l in range(DEPTH) for gname in GROUPS]
    forwarded = {}

    def forward_gather(key, after):
        forwarded[key] = _gather_group_forward(f"l{key[0]}{key[1]}", GROUPS[key[1]], started[key], after)
        return forwarded[key][-1]

    def make_getw(l):
        def getw(gname, after):
            key = (l, gname)
            if key == (0, "A"):
                after = last_cast
            if key not in forwarded:
                after = forward_gather(key, after)
            full = _gather_group_wait(f"l{l}{gname}", GROUPS[gname], started[key][0], forwarded[key], after)
            tok = None
            before = set(started)
            for nl, ng in extra_starts.get(key, []) + [(l + 1, gname)]:
                if nl < DEPTH and (nl, ng) not in started:
                    tok = start_gather(nl, ng, full[0] if tok is None else tok)
            at = need_order.index(key) + 1
            if at < len(need_order) and need_order[at] in before and need_order[at] not in forwarded:
                tok = forward_gather(need_order[at], full[0] if tok is None else tok)
            return {names[q]: f for q, f in zip(GROUPS[gname], full)}, tok
        return getw

    Ws, saves = [], []
    xc = xs
    for l in range(DEPTH):
        xc, sv, W = _layer_fwd(xc, mod[l], ng_full[l], cw_full[l], make_getw(l), bias)
        Ws.append(W)
        saves.append(sv)

    ls, dx, smf = _loss_head(xc, final_g.reshape(1, D), loss_target.reshape(loss_target.shape[1:]))
    loss = lax.psum(ls[0, 0], ("x", "y", "c"))

    params = dict(wg=wg_t, wu=wu_t, wd=ffn_w_down, win=w_in, wco=w_conv_out, wao=w_attn_out, wo=w_o)
    moms = dict(wg=tr2(m_ffn_w_gate), wu=tr2(m_ffn_w_up), wd=m_ffn_w_down, win=m_w_in, wco=m_w_conv_out, wao=m_w_attn_out, wo=m_w_o)
    vars_ = dict(wg=tr2(v_ffn_w_gate), wu=tr2(v_ffn_w_up), wd=v_ffn_w_down, win=v_w_in, wco=v_w_conv_out, wao=v_w_attn_out, wo=v_w_o)
    flat = lambda a: a.reshape(-1, a.shape[-1])
    big_out = {k: [lax.empty(flat(p).shape, f32) for _ in range(4)] for k, p in params.items()}
    dmods, dngs, dcws, dsaccs = [None] * DEPTH, [None] * DEPTH, [None] * DEPTH, [None] * DEPTH

    def finish(l, gname, started, after):
        group = GROUPS[gname]
        pieces, lands = _scatter_group_wait(f"l{l}{gname}", group, started, after)
        ts = [_sum_own_slots(pieces[i], lands[i], *_cls(q), chip_arr) for i, q in enumerate(group)]
        tsib = _swap_sibling(ts)
        for i, q in enumerate(group):
            name = names[q]
            key = name.rstrip("01")
            slot = 2 * l + int(name[-1]) if name[-1] in "01" else l
            big_out[key] = _adamw_pair(flat(params[key]), flat(moms[key]), flat(vars_[key]), ts[i], tsib[i], big_out[key], slot)

    pending, tok = [], None
    for l in reversed(range(DEPTH)):
        modl = mod[l] if tok is None else mod[l] + tok[0, 0]
        mine = []

        def emit(gname, dw, l=l, mine=mine):
            prev = mine[-1][2][-1] if mine else dx
            mine.append((l, gname, _scatter_group_start(f"l{l}{gname}", GROUPS[gname], [dw[names[q]] for q in GROUPS[gname]], prev)))
            return mine[-1][2][-1]

        dx, dmods[l], dngs[l], dcws[l], dsaccs[l] = _layer_bwd(dx, saves[l], modl, ng_full[l], cw_full[l], Ws[l], bias, emit)
        for pl_, pg, pst in pending:
            finish(pl_, pg, pst, dx)
        pending, tok = mine, mine[-1][2][-1]
    for pl_, pg, pst in pending[:-1]:
        finish(pl_, pg, pst, pending[-1][2][-1])

    drb = jnp.transpose(_bias_grad(dsaccs, buckets)[:, 0, :NUM_BUCKETS])
    drb_row = jnp.pad(drb.reshape(1, NUM_BUCKETS * 24), ((0, 0), (0, D - NUM_BUCKETS * 24)))
    pack2 = jnp.concatenate([pad8(a) for a in dmods] + [pad8(a) for a in dngs] + [pad8(a) for a in dcws] + [smf, pad8(drb_row)], axis=0)
    n_rows = pack2.shape[0]
    g3 = _allgather_small(pack2).reshape(8, n_rows, D)
    tot = _sum_devices(g3)
    o_ng, o_cw, o_fg, o_rb = 16 * DEPTH, 24 * DEPTH, 32 * DEPTH, 32 * DEPTH + 8
    g_ada_b = jnp.stack([tot[16 * l:16 * l + 9] for l in range(DEPTH)]).reshape(DEPTH, 9 * D)
    g_norm_g = _chip_cols(jnp.stack([tot[o_ng + 8 * l:o_ng + 8 * l + 3] for l in range(DEPTH)]), chip, qd)
    g_conv_w = _chip_cols(jnp.stack([tot[o_cw + 8 * l:o_cw + 8 * l + 3] for l in range(DEPTH)]), chip, qd)
    g_final_g = tot[o_fg]
    g_rel_bias = tot[o_rb, :NUM_BUCKETS * 24].reshape(NUM_BUCKETS, 24)
    dmod_all = jnp.stack([g3[:, 16 * l:16 * l + 9].reshape(8, 9 * D) for l in range(DEPTH)])
    dmod_sh = _chip_cols(dmod_all, chip, ADA_SH)
    g_ada_w, d_ada_w, nm_ada_w, nv_ada_w = _ada_w_update(cs_all, dmod_sh, ada_w, m_ada_w, v_ada_w)

    def small(w, g, m, v):
        shp = w.shape
        to2 = lambda a: a.reshape(-1, shp[-1])
        return [o.reshape(shp) for o in _adamw_small(to2(w), to2(g), to2(m), to2(v))]

    d_ada_b, nm_ada_b, nv_ada_b = small(ada_b, g_ada_b, m_ada_b, v_ada_b)
    d_norm_g, nm_norm_g, nv_norm_g = small(norm_g, g_norm_g, m_norm_g, v_norm_g)
    d_conv_w, nm_conv_w, nv_conv_w = small(conv_w, g_conv_w, m_conv_w, v_conv_w)
    d_rel_bias, nm_rel_bias, nv_rel_bias = small(rel_bias, g_rel_bias, m_rel_bias, v_rel_bias)
    d_final_g, nm_final_g, nv_final_g = small(final_g, g_final_g, m_final_g, v_final_g)

    behind = nv_ada_w[0, 0:8, 0:128]
    for key in big_out:
        behind = behind + big_out[key][3][0:8, 0:128]
    finish(*pending[-1], behind)

    def big(key, which):
        out = big_out[key][which].reshape(params[key].shape)
        return tr2(out) if key in ("wg", "wu") else out

    grads = [g_ada_w, g_ada_b, g_norm_g, big("wg", 0), big("wu", 0), big("wd", 0), big("win", 0), g_conv_w, big("wco", 0),
             big("wao", 0), big("wo", 0), g_rel_bias, g_final_g]
    deltas = [d_ada_w, d_ada_b, d_norm_g, big("wg", 1), big("wu", 1), big("wd", 1), big("win", 1), d_conv_w, big("wco", 1),
              big("wao", 1), big("wo", 1), d_rel_bias, d_final_g]
    new_m = [nm_ada_w, nm_ada_b, nm_norm_g, big("wg", 2), big("wu", 2), big("wd", 2), big("win", 2), nm_conv_w, big("wco", 2),
             big("wao", 2), big("wo", 2), nm_rel_bias, nm_final_g]
    new_v = [nv_ada_w, nv_ada_b, nv_norm_g, big("wg", 3), big("wu", 3), big("wd", 3), big("win", 3), nv_conv_w, big("wco", 3),
             big("wao", 3), big("wo", 3), nv_rel_bias, nv_final_g]
    return (loss, dx.reshape(x.shape), *grads, *deltas, *new_m, *new_v)
```

```python
import functools

import numpy as np
import jax
import jax.numpy as jnp
from jax import lax
from jax.experimental import pallas as pl
from jax.experimental.pallas import tpu as pltpu

f32, bf16 = jnp.float32, jnp.bfloat16
SDS = jax.ShapeDtypeStruct
MESH = pl.DeviceIdType.MESH

D = 1024
DEPTH = 4
N_CHIPS = 4
FB = 704
HD = 64
QKV_W = 4608
REST_W = 5120
IN_W = QKV_W + REST_W
WIN_SH = IN_W // N_CHIPS
ADA_SH = 9 * D // N_CHIPS
BLK = 128
DILATIONS = (1, 4, 16)
Q_BLOCKS = (4, 2, 1)
NUM_BUCKETS, MAX_DISTANCE = 32, 2048
EPS = 1e-6
NEG = -1e30
SCALE = HD ** -0.5
LR, B1, B2, AEPS, WD, STEP = 0.001, 0.9, 0.999, 1e-08, 0.01, 10
BC1 = 1.0 - B1 ** STEP
BC2 = 1.0 - B2 ** STEP
VMEM_LIMIT = 56 * 1024 * 1024
TM = 512
TMW = 1024
TMP = 1024
TMF = 1024
SH_STEP = 2
TMX = 256
TMXF = 512
HALO = 16
CB = 512
CBQ = 1536


def _cp(*sem):
    return pltpu.CompilerParams(dimension_semantics=sem if sem else None, vmem_limit_bytes=VMEM_LIMIT)


def _dot(a, b):
    return jnp.dot(a, b, preferred_element_type=f32)


def _dot_nt(a, b):
    return lax.dot_general(a, b, (((1,), (1,)), ((), ())), preferred_element_type=f32)


def _dot_tn(a, b):
    return lax.dot_general(a, b, (((0,), (0,)), ((), ())), preferred_element_type=f32)


def _sigmoid(x):
    return 0.5 * jnp.tanh(0.5 * x) + 0.5


def _norm_fwd(x, g, shift, scale):
    rstd = lax.rsqrt(jnp.mean(x * x, axis=-1, keepdims=True) + EPS)
    xhat = x * rstd
    return xhat * g * (1.0 + scale) + shift, xhat, rstd


def _norm_bwd(dh, xhat, rstd, g, scale):
    dshift = jnp.sum(dh, axis=0, keepdims=True)
    dscale = jnp.sum(dh * xhat * g, axis=0, keepdims=True)
    dg = jnp.sum(dh * xhat * (1.0 + scale), axis=0, keepdims=True)
    dxh = dh * (g * (1.0 + scale))
    dx = rstd * (dxh - xhat * jnp.mean(dxh * xhat, axis=-1, keepdims=True))
    return dx, dshift, dscale, dg


def _allgather_small(xp):
    m_per, n = xp.shape

    def body(x_ref, out_ref, send_sems, recv_sems, local_sem):
        x, y, c = lax.axis_index("x"), lax.axis_index("y"), lax.axis_index("c")
        me, sibling = (x, y, c), (x, y, 1 - c)
        chips = [(1 - x, y), (x, 1 - y), (1 - x, 1 - y)]

        def rows(px, py, pc):
            return out_ref.at[pl.ds((4 * px + 2 * py + pc) * m_per, m_per), :]

        def copy(k, block, to, src=None):
            return pltpu.make_async_remote_copy(
                src_ref=rows(*block) if src is None else src, dst_ref=rows(*block),
                send_sem=send_sems.at[k], recv_sem=recv_sems.at[k], device_id=to, device_id_type=MESH)

        mine = pltpu.make_async_copy(x_ref, rows(*me), local_sem)
        mine.start()
        first = [copy(0, me, sibling, src=x_ref)]
        first += [copy(1 + j, me, (*chip, c), src=x_ref) for j, chip in enumerate(chips)]
        for cp in first:
            cp.start()
        passed = [copy(4 + j, (*chip, c), sibling) for j, chip in enumerate(chips)]
        for j, chip in enumerate(chips):
            copy(1 + j, (*chip, c), me).wait_recv()
            passed[j].start()
        copy(0, sibling, me).wait_recv()
        for j, chip in enumerate(chips):
            copy(4 + j, (*chip, 1 - c), me).wait_recv()
        for cp in first + passed:
            cp.wait_send()
        mine.wait()

    return pl.pallas_call(
        body, name="allgather_small",
        out_shape=SDS((8 * m_per, n), xp.dtype),
        in_specs=[pl.BlockSpec(memory_space=pltpu.VMEM)],
        out_specs=pl.BlockSpec(memory_space=pltpu.VMEM),
        scratch_shapes=[pltpu.SemaphoreType.DMA((7,)), pltpu.SemaphoreType.DMA((7,)), pltpu.SemaphoreType.DMA],
        compiler_params=pltpu.CompilerParams(vmem_limit_bytes=VMEM_LIMIT),
    )(xp)


WCLASSES = (
    ("wg0", "row", (FB, D)), ("wu0", "row", (FB, D)), ("wd0", "row", (FB, D)),
    ("wg1", "row", (FB, D)), ("wu1", "row", (FB, D)), ("wd1", "row", (FB, D)),
    ("win", "col", (D, WIN_SH)), ("wco", "row", (D // N_CHIPS, D)), ("wao", "col", (512, D // N_CHIPS)),
    ("wo", "row", (D // N_CHIPS, D)),
)
NCLS = len(WCLASSES)


def _full_shape(kind, shp):
    if kind == "lead":
        return (N_CHIPS,) + shp
    if kind == "row":
        return (N_CHIPS * shp[0], shp[1])
    return (shp[0], N_CHIPS * shp[1])


def _shard_view(ref, kind, shp, j):
    if kind == "lead":
        return ref.at[j]
    if kind == "row":
        return ref.at[pl.ds(j * shp[0], shp[0]), :]
    return ref.at[:, pl.ds(j * shp[1], shp[1])]


def _half(ref, shp, h):
    hr = shp[0] // 2
    return ref.at[pl.ds(pl.multiple_of(h * hr, 16), hr), :]


def _gather_weights(shards):
    n = NCLS

    def body(*refs):
        ins, outs = refs[:n], refs[n:2 * n]
        send1, recv1, send2, recv2, lsem = refs[2 * n:]
        x, y, c = lax.axis_index("x"), lax.axis_index("y"), lax.axis_index("c")
        chip = 2 * x + y
        sibling = (x, y, 1 - c)

        for mc in range(N_CHIPS):
            @pl.when(chip == mc)
            def _(mc=mc):
                local = []
                for q, (_, kind, shp) in enumerate(WCLASSES):
                    cp = pltpu.make_async_copy(ins[q], _shard_view(outs[q], kind, shp, mc), lsem.at[q])
                    cp.start()
                    local.append(cp)
                sends = []
                for k in (1, 2, 3):
                    pj = mc ^ k
                    for q, (_, kind, shp) in enumerate(WCLASSES):
                        cp = pltpu.make_async_remote_copy(
                            src_ref=_half(ins[q], shp, c), dst_ref=_half(_shard_view(outs[q], kind, shp, mc), shp, c),
                            send_sem=send1.at[q * 3 + k - 1], recv_sem=recv1.at[q * 3 + k - 1],
                            device_id=(pj >> 1, pj & 1, c), device_id_type=MESH)
                        cp.start()
                        sends.append(cp)
                for k in (1, 2, 3):
                    pj = mc ^ k
                    for q, (_, kind, shp) in enumerate(WCLASSES):
                        landed = _half(_shard_view(outs[q], kind, shp, pj), shp, c)
                        pltpu.make_async_remote_copy(
                            src_ref=landed, dst_ref=landed, send_sem=send1.at[q * 3 + k - 1], recv_sem=recv1.at[q * 3 + k - 1],
                            device_id=(pj >> 1, pj & 1, c), device_id_type=MESH).wait_recv()
                        cp = pltpu.make_async_remote_copy(
                            src_ref=landed, dst_ref=landed, send_sem=send2.at[q * 3 + k - 1], recv_sem=recv2.at[q * 3 + k - 1],
                            device_id=sibling, device_id_type=MESH)
                        cp.start()
                        sends.append(cp)
                for k in (1, 2, 3):
                    pj = mc ^ k
                    for q, (_, kind, shp) in enumerate(WCLASSES):
                        other = _half(_shard_view(outs[q], kind, shp, pj), shp, 1 - c)
                        pltpu.make_async_remote_copy(
                            src_ref=other, dst_ref=other, send_sem=send2.at[q * 3 + k - 1], recv_sem=recv2.at[q * 3 + k - 1],
                            device_id=sibling, device_id_type=MESH).wait_recv()
                for cp in sends:
                    cp.wait_send()
                for cp in local:
                    cp.wait()

    anyspec = pl.BlockSpec(memory_space=pl.ANY)
    return pl.pallas_call(
        body, name="gather_weights",
        out_shape=[SDS(_full_shape(kind, shp), bf16) for _, kind, shp in WCLASSES],
        in_specs=[anyspec] * n, out_specs=[anyspec] * n,
        scratch_shapes=[pltpu.SemaphoreType.DMA((3 * n,)), pltpu.SemaphoreType.DMA((3 * n,)),
                        pltpu.SemaphoreType.DMA((3 * n,)), pltpu.SemaphoreType.DMA((3 * n,)),
                        pltpu.SemaphoreType.DMA((n,))],
    )(*shards)


def _scatter_grads(pieces):
    n = NCLS

    def body(*refs):
        ins, outs = refs[:n], refs[n:2 * n]
        send1, recv1, lsem = refs[2 * n:]
        x, y, c = lax.axis_index("x"), lax.axis_index("y"), lax.axis_index("c")
        chip = 2 * x + y

        for mc in range(N_CHIPS):
            @pl.when(chip == mc)
            def _(mc=mc):
                local, sends = [], []
                for q, (_, kind, shp) in enumerate(WCLASSES):
                    cp = pltpu.make_async_copy(_shard_view(ins[q], kind, shp, mc), outs[q].at[0], lsem.at[q])
                    cp.start()
                    local.append(cp)
                for k in (1, 2, 3):
                    pj = mc ^ k
                    for q, (_, kind, shp) in enumerate(WCLASSES):
                        cp = pltpu.make_async_remote_copy(
                            src_ref=_shard_view(ins[q], kind, shp, pj), dst_ref=outs[q].at[k],
                            send_sem=send1.at[q * 3 + k - 1], recv_sem=recv1.at[q * 3 + k - 1],
                            device_id=(pj >> 1, pj & 1, c), device_id_type=MESH)
                        cp.start()
                        sends.append(cp)
                for cp in sends:
                    cp.wait_recv()
                for cp in sends:
                    cp.wait_send()
                for cp in local:
                    cp.wait()

    anyspec = pl.BlockSpec(memory_space=pl.ANY)
    return pl.pallas_call(
        body, name="scatter_grads",
        out_shape=[SDS((N_CHIPS,) + shp, bf16) for _, _, shp in WCLASSES],
        in_specs=[anyspec] * n, out_specs=[anyspec] * n,
        scratch_shapes=[pltpu.SemaphoreType.DMA((3 * n,)), pltpu.SemaphoreType.DMA((3 * n,)), pltpu.SemaphoreType.DMA((n,))],
    )(*pieces)


def _swap_sibling(ts):
    n = len(ts)

    def body(*refs):
        ins, outs = refs[:n], refs[n:2 * n]
        send, recv = refs[2 * n:]
        x, y, c = lax.axis_index("x"), lax.axis_index("y"), lax.axis_index("c")
        cps = []
        for q in range(n):
            cp = pltpu.make_async_remote_copy(src_ref=ins[q], dst_ref=outs[q], send_sem=send.at[q], recv_sem=recv.at[q],
                                              device_id=(x, y, 1 - c), device_id_type=MESH)
            cp.start()
            cps.append(cp)
        for cp in cps:
            cp.wait_recv()
        for cp in cps:
            cp.wait_send()

    anyspec = pl.BlockSpec(memory_space=pl.ANY)
    return pl.pallas_call(
        body, name="swap_sibling",
        out_shape=[SDS(t.shape, t.dtype) for t in ts],
        in_specs=[anyspec] * n, out_specs=[anyspec] * n,
        scratch_shapes=[pltpu.SemaphoreType.DMA((n,)), pltpu.SemaphoreType.DMA((n,))],
    )(*ts)


HBM_SPEC = pl.BlockSpec(memory_space=pltpu.HBM)
SEM_SPEC = pl.BlockSpec(memory_space=pltpu.SEMAPHORE)
ANY_SPEC = pl.BlockSpec(memory_space=pl.ANY)
EFFECT = pltpu.SideEffectType.DATAFLOW_SIDE_EFFECTING
N_COPIES = 3 * NCLS


def _in_hbm(a):
    return pltpu.with_memory_space_constraint(a, pltpu.HBM)


def _chip_index():
    return 2 * lax.axis_index("x") + lax.axis_index("y")


def _place_own(shards):
    n = NCLS

    def body(*refs):
        ins, outs, lsem = refs[:n], refs[n:2 * n], refs[2 * n]
        chip = _chip_index()
        for mc in range(N_CHIPS):
            @pl.when(chip == mc)
            def _(mc=mc):
                cps = [pltpu.make_async_copy(ins[q], _shard_view(outs[q], kind, shp, mc), lsem.at[q])
                       for q, (_, kind, shp) in enumerate(WCLASSES)]
                for cp in cps:
                    cp.start()
                for cp in cps:
                    cp.wait()

    return pl.pallas_call(
        body, name="place_own",
        out_shape=[SDS(_full_shape(kind, shp), bf16) for _, kind, shp in WCLASSES],
        in_specs=[ANY_SPEC] * n, out_specs=[ANY_SPEC] * n,
        scratch_shapes=[pltpu.SemaphoreType.DMA((n,))],
    )(*shards)


def _take_own(pieces):
    n = NCLS

    def body(*refs):
        ins, outs, lsem = refs[:n], refs[n:2 * n], refs[2 * n]
        chip = _chip_index()
        for mc in range(N_CHIPS):
            @pl.when(chip == mc)
            def _(mc=mc):
                cps = [pltpu.make_async_copy(_shard_view(ins[q], kind, shp, mc), outs[q].at[0], lsem.at[q])
                       for q, (_, kind, shp) in enumerate(WCLASSES)]
                for cp in cps:
                    cp.start()
                for cp in cps:
                    cp.wait()

    return pl.pallas_call(
        body, name="take_own",
        out_shape=[SDS((N_CHIPS,) + shp, bf16) for _, _, shp in WCLASSES],
        in_specs=[ANY_SPEC] * n, out_specs=[ANY_SPEC] * n,
        scratch_shapes=[pltpu.SemaphoreType.DMA((n,))],
    )(*pieces)


def _split_start(name, srcs, dsts, after, src_view, dst_view):
    n = NCLS

    def body(*refs):
        src, dst = refs[:n], refs[n:2 * n]
        send, recv = refs[2 * n + 1], refs[2 * n + 2]
        token = refs[-1]
        c = lax.axis_index("c")
        chip = _chip_index()
        for mc in range(N_CHIPS):
            @pl.when(chip == mc)
            def _(mc=mc):
                for k in (1, 2, 3):
                    pj = mc ^ k
                    for q in range(n):
                        pltpu.make_async_remote_copy(
                            src_ref=src_view(src[q], q, mc, pj), dst_ref=dst_view(dst[q], q, mc, k),
                            send_sem=send.at[q * 3 + k - 1], recv_sem=recv.at[q * 3 + k - 1],
                            device_id=(pj >> 1, pj & 1, c), device_id_type=MESH).start()
        token[...] = jnp.zeros_like(token)

    return pl.pallas_call(
        body, name=name,
        out_shape=(pltpu.SemaphoreType.DMA((N_COPIES,)), pltpu.SemaphoreType.DMA((N_COPIES,)),
                   *[pltpu.HBM(a.shape, a.dtype) for a in srcs], *[pltpu.HBM(a.shape, a.dtype) for a in dsts], SDS((8, 128), f32)),
        in_specs=[HBM_SPEC] * (2 * n) + [ANY_SPEC],
        out_specs=(SEM_SPEC, SEM_SPEC, *([HBM_SPEC] * (2 * n)), pl.BlockSpec(memory_space=pltpu.VMEM)),
        input_output_aliases={i: 2 + i for i in range(2 * n)},
        compiler_params=pltpu.CompilerParams(has_side_effects=EFFECT),
    )(*[_in_hbm(a) for a in srcs], *[_in_hbm(a) for a in dsts], after)


def _split_wait(name, started, after, arrival_view):
    n = NCLS
    send, recv = started[0], started[1]
    srcs, dsts = started[2:2 + n], started[2 + n:2 + 2 * n]

    def body(*refs):
        src, dst = refs[:n], refs[n:2 * n]
        send_sem, recv_sem = refs[2 * n], refs[2 * n + 1]
        x, y, c = lax.axis_index("x"), lax.axis_index("y"), lax.axis_index("c")
        for k in (1, 2, 3):
            for q in range(n):
                arrival = arrival_view(dst[q], q, k)
                cp = pltpu.make_async_remote_copy(
                    src_ref=arrival, dst_ref=arrival, send_sem=send_sem.at[q * 3 + k - 1], recv_sem=recv_sem.at[q * 3 + k - 1],
                    device_id=(x, y, 1 - c), device_id_type=MESH)
                cp.wait_send()
                cp.wait_recv()

    out = pl.pallas_call(
        body, name=name,
        out_shape=(*[pltpu.HBM(a.shape, a.dtype) for a in srcs], *[pltpu.HBM(a.shape, a.dtype) for a in dsts]),
        in_specs=[HBM_SPEC] * (2 * n) + [SEM_SPEC, SEM_SPEC, ANY_SPEC],
        out_specs=tuple([HBM_SPEC] * (2 * n)),
        input_output_aliases={i: i for i in range(2 * n)},
        compiler_params=pltpu.CompilerParams(has_side_effects=EFFECT),
    )(*srcs, *dsts, send, recv, after)
    return out[n:]


def _cls(q):
    return WCLASSES[q][1], WCLASSES[q][2]


def _gather_start(shards, after):
    fulls = _place_own(shards)
    return _split_start("gather_start", shards, fulls, after,
                        lambda ref, q, mc, pj: ref,
                        lambda ref, q, mc, k: _shard_view(ref, *_cls(q), mc))


def _gather_wait(started, after):
    return _split_wait("gather_wait", started, after, lambda ref, q, k: _shard_view(ref, *_cls(q), 0))


def _scatter_start(pieces, after):
    lands = _take_own(pieces)
    return _split_start("scatter_start", pieces, lands, after,
                        lambda ref, q, mc, pj: _shard_view(ref, *_cls(q), pj),
                        lambda ref, q, mc, k: ref.at[k])


def _scatter_wait(started, after):
    return _split_wait("scatter_wait", started, after, lambda ref, q, k: ref.at[k])


GROUPS = {"A": (0, 1, 2), "B": (6,), "C": (7, 8, 9), "D": (3, 4, 5)}


def _own_spec(kind, shp, tr):
    R, C = shp
    if kind == "lead":
        return pl.BlockSpec((None, tr, C), lambda i, chip: (chip[0], i, 0))
    if kind == "row":
        return pl.BlockSpec((tr, C), lambda i, chip: (chip[0] * (R // tr) + i, 0))
    return pl.BlockSpec((tr, C), lambda i, chip: (i, chip[0]))


def _cast_place(shards, kind, shp, chip_arr, after):
    n = len(shards)
    R, C = shp
    tr = _row_tile(R, C)

    def body(chip_ref, *refs):
        for q in range(n):
            refs[n + 1 + q][...] = refs[q][...].astype(bf16)

    def in_spec(lead):
        return pl.BlockSpec((None,) * len(lead) + (tr, C), lambda i, chip: (*lead, i, 0))

    return pl.pallas_call(
        body, name="cast_place",
        grid_spec=pltpu.PrefetchScalarGridSpec(
            num_scalar_prefetch=1, grid=(R // tr,),
            in_specs=[in_spec(lead) for _, lead in shards] + [ANY_SPEC],
            out_specs=[_own_spec(kind, shp, tr)] * n),
        out_shape=[SDS(_full_shape(kind, shp), bf16)] * n,
        compiler_params=_cp("arbitrary"),
    )(chip_arr, *[a for a, _ in shards], after)


def _sum_own_slots(piece, land, kind, shp, chip_arr):
    R, C = shp
    tr = _row_tile(R, C)

    def body(chip_ref, p_ref, l_ref, t_ref):
        t = p_ref[...].astype(f32)
        for k in range(N_CHIPS - 1):
            t = t + l_ref[k].astype(f32)
        t_ref[...] = t.astype(bf16)

    return pl.pallas_call(
        body, name="sum_own_slots",
        grid_spec=pltpu.PrefetchScalarGridSpec(
            num_scalar_prefetch=1, grid=(R // tr,),
            in_specs=[_own_spec(kind, shp, tr), pl.BlockSpec((N_CHIPS - 1, tr, C), lambda i, chip: (0, i, 0))],
            out_specs=pl.BlockSpec((tr, C), lambda i, chip: (i, 0))),
        out_shape=SDS((R, C), bf16),
        compiler_params=_cp("arbitrary"),
    )(chip_arr, piece, land)


def _xfer_start(name, arrays, ng, after, src_view, dst_view):
    na = len(arrays)

    def body(*refs):
        arr = refs[:na]
        send, recv, token = refs[na + 1], refs[na + 2], refs[-1]
        c = lax.axis_index("c")
        chip = _chip_index()
        for mc in range(N_CHIPS):
            @pl.when(chip == mc)
            def _(mc=mc):
                for k in (1, 2, 3):
                    pj = mc ^ k
                    for i in range(ng):
                        pltpu.make_async_remote_copy(
                            src_ref=src_view(arr, i, mc, pj), dst_ref=dst_view(arr, i, mc, k),
                            send_sem=send.at[i * 3 + k - 1], recv_sem=recv.at[i * 3 + k - 1],
                            device_id=(pj >> 1, pj & 1, c), device_id_type=MESH).start()
        token[...] = jnp.zeros_like(token)

    return pl.pallas_call(
        body, name=name,
        out_shape=(pltpu.SemaphoreType.DMA((3 * ng,)), pltpu.SemaphoreType.DMA((3 * ng,)),
                   *[pltpu.HBM(a.shape, a.dtype) for a in arrays], SDS((8, 128), f32)),
        in_specs=[HBM_SPEC] * na + [ANY_SPEC],
        out_specs=(SEM_SPEC, SEM_SPEC, *([HBM_SPEC] * na), pl.BlockSpec(memory_space=pltpu.VMEM)),
        input_output_aliases={i: 2 + i for i in range(na)},
        compiler_params=pltpu.CompilerParams(has_side_effects=EFFECT),
    )(*[_in_hbm(a) for a in arrays], after)


def _xfer_wait(name, started, ng, after, arrival_view):
    send, recv = started[0], started[1]
    arrays = started[2:-1]
    na = len(arrays)

    def body(*refs):
        arr = refs[:na]
        send_sem, recv_sem = refs[na], refs[na + 1]
        x, y, c = lax.axis_index("x"), lax.axis_index("y"), lax.axis_index("c")
        for k in (1, 2, 3):
            for i in range(ng):
                arrival = arrival_view(arr, i)
                cp = pltpu.make_async_remote_copy(
                    src_ref=arrival, dst_ref=arrival, send_sem=send_sem.at[i * 3 + k - 1], recv_sem=recv_sem.at[i * 3 + k - 1],
                    device_id=(x, y, 1 - c), device_id_type=MESH)
                cp.wait_send()
                cp.wait_recv()

    return pl.pallas_call(
        body, name=name,
        out_shape=tuple(pltpu.HBM(a.shape, a.dtype) for a in arrays),
        in_specs=[HBM_SPEC] * na + [SEM_SPEC, SEM_SPEC, ANY_SPEC],
        out_specs=tuple([HBM_SPEC] * na),
        input_output_aliases={i: i for i in range(na)},
        compiler_params=pltpu.CompilerParams(has_side_effects=EFFECT),
    )(*arrays, send, recv, after)


def _gather_group_cast(group, shards_f32, chip_arr, after):
    fulls = [None] * len(group)
    by_shape = {}
    for i, q in enumerate(group):
        by_shape.setdefault(_cls(q), []).append(i)
    for (kind, shp), idx in by_shape.items():
        for i, f in zip(idx, _cast_place([shards_f32[i] for i in idx], kind, shp, chip_arr, after)):
            fulls[i] = f
    return fulls


def _gather_group_start(tag, group, fulls, after):
    def view(arr, i, mc, _):
        kind, shp = _cls(group[i])
        return _half(_shard_view(arr[i], kind, shp, mc), shp, lax.axis_index("c"))
    return _xfer_start("gather_start_" + tag, fulls, len(group), after, view, view)


def _gather_group_forward(tag, group, started, after):
    ng = len(group)
    send1, recv1 = started[0], started[1]
    arrays = started[2:-1]
    na = len(arrays)

    def body(*refs):
        arr = refs[:na]
        send_in, recv_in = refs[na], refs[na + 1]
        send2, recv2, token = refs[na + 3], refs[na + 4], refs[-1]
        x, y, c = lax.axis_index("x"), lax.axis_index("y"), lax.axis_index("c")
        chip = _chip_index()
        for mc in range(N_CHIPS):
            @pl.when(chip == mc)
            def _(mc=mc):
                for k in (1, 2, 3):
                    pj = mc ^ k
                    for i in range(ng):
                        kind, shp = _cls(group[i])
                        landed = _half(_shard_view(arr[i], kind, shp, pj), shp, c)
                        pltpu.make_async_remote_copy(
                            src_ref=landed, dst_ref=landed, send_sem=send_in.at[i * 3 + k - 1], recv_sem=recv_in.at[i * 3 + k - 1],
                            device_id=(pj >> 1, pj & 1, c), device_id_type=MESH).wait_recv()
                        pltpu.make_async_remote_copy(
                            src_ref=landed, dst_ref=landed, send_sem=send2.at[i * 3 + k - 1], recv_sem=recv2.at[i * 3 + k - 1],
                            device_id=(x, y, 1 - c), device_id_type=MESH).start()
        token[...] = jnp.zeros_like(token)

    return pl.pallas_call(
        body, name="gather_forward_" + tag,
        out_shape=(pltpu.SemaphoreType.DMA((3 * ng,)), pltpu.SemaphoreType.DMA((3 * ng,)),
                   *[pltpu.HBM(a.shape, a.dtype) for a in arrays], SDS((8, 128), f32)),
        in_specs=[HBM_SPEC] * na + [SEM_SPEC, SEM_SPEC, ANY_SPEC],
        out_specs=(SEM_SPEC, SEM_SPEC, *([HBM_SPEC] * na), pl.BlockSpec(memory_space=pltpu.VMEM)),
        input_output_aliases={i: 2 + i for i in range(na)},
        compiler_params=pltpu.CompilerParams(has_side_effects=EFFECT),
    )(*arrays, send1, recv1, after)


def _gather_group_wait(tag, group, send1, forwarded, after):
    ng = len(group)
    send2, recv2 = forwarded[0], forwarded[1]
    arrays = forwarded[2:-1]
    na = len(arrays)

    def body(*refs):
        arr = refs[:na]
        s1, s2, r2 = refs[na], refs[na + 1], refs[na + 2]
        x, y, c = lax.axis_index("x"), lax.axis_index("y"), lax.axis_index("c")
        for k in (1, 2, 3):
            for i in range(ng):
                kind, shp = _cls(group[i])
                half = _half(_shard_view(arr[i], kind, shp, 0), shp, 0)
                pltpu.make_async_remote_copy(src_ref=half, dst_ref=half, send_sem=s1.at[i * 3 + k - 1], recv_sem=r2.at[i * 3 + k - 1],
                                             device_id=(x, y, 1 - c), device_id_type=MESH).wait_send()
                cp = pltpu.make_async_remote_copy(src_ref=half, dst_ref=half, send_sem=s2.at[i * 3 + k - 1], recv_sem=r2.at[i * 3 + k - 1],
                                                  device_id=(x, y, 1 - c), device_id_type=MESH)
                cp.wait_send()
                cp.wait_recv()

    return pl.pallas_call(
        body, name="gather_wait_" + tag,
        out_shape=tuple(pltpu.HBM(a.shape, a.dtype) for a in arrays),
        in_specs=[HBM_SPEC] * na + [SEM_SPEC, SEM_SPEC, SEM_SPEC, ANY_SPEC],
        out_specs=tuple([HBM_SPEC] * na),
        input_output_aliases={i: i for i in range(na)},
        compiler_params=pltpu.CompilerParams(has_side_effects=EFFECT),
    )(*arrays, send1, send2, recv2, after)


def _scatter_group_start(tag, group, pieces, after):
    ng = len(group)
    lands = [lax.empty((N_CHIPS - 1,) + _cls(q)[1], bf16) for q in group]
    return _xfer_start("scatter_start_" + tag, list(pieces) + lands, ng, after,
                       lambda arr, i, mc, pj: _shard_view(arr[i], *_cls(group[i]), pj),
                       lambda arr, i, mc, k: arr[ng + i].at[k - 1])


def _scatter_group_wait(tag, group, started, after):
    ng = len(group)
    out = _xfer_wait("scatter_wait_" + tag, started, ng, after, lambda arr, i: arr[ng + i].at[0])
    return out[:ng], out[ng:]


def _mod_shards(c_all, ada_w, ada_b_sh):
    tn = ADA_SH // 3

    def body(c_ref, w_ref, b_ref, o_ref, cs_ref):
        cv = c_ref[...]
        cs = cv * _sigmoid(cv)
        cs_ref[...] = cs
        o_ref[...] = _dot(cs.astype(bf16), w_ref[...].astype(bf16)) + b_ref[...]

    return pl.pallas_call(
        body, name="mod_shards", grid=(DEPTH, 3),
        out_shape=[SDS((DEPTH, 8, ADA_SH), f32), SDS((8, D), f32)],
        in_specs=[pl.BlockSpec((8, D), lambda l, t: (0, 0)),
                  pl.BlockSpec((None, D, tn), lambda l, t: (l, 0, t)),
                  pl.BlockSpec((None, 1, tn), lambda l, t: (l, 0, t))],
        out_specs=[pl.BlockSpec((None, 8, tn), lambda l, t: (l, 0, t)), pl.BlockSpec((8, D), lambda l, t: (0, 0))],
        compiler_params=_cp("arbitrary", "arbitrary"),
    )(c_all, ada_w, ada_b_sh.reshape(DEPTH, 1, ADA_SH))


def _t5_bucket(dist):
    exact = NUM_BUCKETS // 2
    dd = np.maximum(dist, 1).astype(np.float32)
    large = exact + (np.log(dd / exact) / np.log(MAX_DISTANCE / exact) * (NUM_BUCKETS - exact)).astype(np.int32)
    large = np.minimum(large, NUM_BUCKETS - 1)
    return np.where(dist < exact, dist, large).astype(np.int32)


def _bucket_table():
    i = np.arange(BLK)[:, None]
    j = np.arange(2 * BLK)[None, :]
    rel = i - j + BLK
    return np.stack([_t5_bucket(np.maximum(rel, 0) * d) for d in DILATIONS]).astype(np.int32)


def _band():
    rel = lax.broadcasted_iota(jnp.int32, (BLK, 2 * BLK), 0) - lax.broadcasted_iota(jnp.int32, (BLK, 2 * BLK), 1) + BLK
    return (rel >= 0) & (rel <= BLK)


def _bias_blocks(rel_bias, buckets):
    def body(tab_ref, bk_ref, o_ref):
        h = pl.program_id(0)
        bk = bk_ref[...]
        acc = jnp.zeros((BLK, 2 * BLK), f32)
        for b in range(NUM_BUCKETS):
            acc = jnp.where(bk == b, tab_ref[b, h], acc)
        o_ref[...] = jnp.where(_band(), acc, NEG)

    return pl.pallas_call(
        body, name="bias_blocks", grid=(24,),
        out_shape=SDS((24, BLK, 2 * BLK), f32),
        in_specs=[pl.BlockSpec(memory_space=pltpu.SMEM), pl.BlockSpec((None, BLK, 2 * BLK), lambda h: (h // 8, 0, 0))],
        out_specs=pl.BlockSpec((None, BLK, 2 * BLK), lambda h: (h, 0, 0)),
        compiler_params=_cp("arbitrary"),
    )(rel_bias, buckets)


def _bias_grad(dsaccs, buckets):
    nl = len(dsaccs)

    def body(*refs):
        bk = refs[nl][...]
        tot = refs[0][...]
        for r in refs[1:nl]:
            tot = tot + r[...]
        lane = lax.broadcasted_iota(jnp.int32, (1, 128), 1)
        row = jnp.zeros((1, 128), f32)
        for b in range(NUM_BUCKETS):
            row = jnp.where(lane == b, jnp.sum(jnp.where(bk == b, tot, 0.0)), row)
        refs[nl + 1][...] = row

    return pl.pallas_call(
        body, name="bias_grad", grid=(24,),
        out_shape=SDS((24, 1, 128), f32),
        in_specs=[pl.BlockSpec((None, BLK, 2 * BLK), lambda h: (h, 0, 0))] * nl
                 + [pl.BlockSpec((None, BLK, 2 * BLK), lambda h: (h // 8, 0, 0))],
        out_specs=pl.BlockSpec((None, 1, 128), lambda h: (h, 0, 0)),
        compiler_params=_cp("arbitrary"),
    )(*dsaccs, buckets)


def _ffn_fwd(x, mod9, g3, wg, wu, wd, sub):
    S = x.shape[0]

    def body(x_ref, mod_ref, g_ref, wg_ref, wu_ref, wd_ref, xo_ref, h_ref, ga_ref, sa_ref, hid_ref, y_ref, acc):
        j = pl.program_id(1)

        @pl.when(j == 0)
        def _():
            h, _, _ = _norm_fwd(x_ref[...], g_ref[sub:sub + 1, :], mod_ref[3 * sub:3 * sub + 1, :], mod_ref[3 * sub + 1:3 * sub + 2, :])
            h_ref[...] = h.astype(bf16)
            acc[...] = jnp.zeros_like(acc)

        h = h_ref[...]
        a = _dot_nt(h, wg_ref[...])
        u = _dot_nt(h, wu_ref[...])
        sg = _sigmoid(a)
        sil = a * sg
        ga_ref[...] = (u * (sg * (1.0 + a * (1.0 - sg)))).astype(bf16)
        sa_ref[...] = sil.astype(bf16)
        hid_ref[...] = (sil * u).astype(bf16)
        acc[...] += _dot(hid_ref[...], wd_ref[...])

        @pl.when(j == N_CHIPS - 1)
        def _():
            y = acc[...]
            y_ref[...] = y.astype(bf16)
            xo_ref[...] = x_ref[...] + 0.5 * mod_ref[3 * sub + 2:3 * sub + 3, :] * y

    row = pl.BlockSpec((TMF, D), lambda i, j: (i, 0))
    hidb = pl.BlockSpec((None, TMF, FB), lambda i, j: (j, i, 0))
    hids = SDS((N_CHIPS, S, FB), bf16)
    return pl.pallas_call(
        body, name="ffn_fwd", grid=(S // TMF, N_CHIPS),
        out_shape=[SDS((S, D), f32), SDS((S, D), bf16), hids, hids, hids, SDS((S, D), bf16)],
        in_specs=[row, pl.BlockSpec((9, D), lambda i, j: (0, 0)), pl.BlockSpec((3, D), lambda i, j: (0, 0)),
                  pl.BlockSpec((FB, D), lambda i, j: (j, 0)), pl.BlockSpec((FB, D), lambda i, j: (j, 0)),
                  pl.BlockSpec((FB, D), lambda i, j: (j, 0))],
        out_specs=[row, row, hidb, hidb, hidb, row],
        scratch_shapes=[pltpu.VMEM((TMF, D), f32)],
        compiler_params=_cp("arbitrary", "arbitrary"),
    )(x, mod9, g3, wg, wu, wd)


def _ffn_bwd1(dxo, x, mod9, g3, y, ga, sa, wg, wu, wd, sub):
    S = x.shape[0]

    def body(dxo_ref, x_ref, mod_ref, g_ref, y_ref, ga_ref, sa_ref, wg_ref, wu_ref, wd_ref,
             dxi_ref, da_ref, du_ref, dy_ref, sm_ref, acc):
        i, j = pl.program_id(0), pl.program_id(1)
        gate = mod_ref[3 * sub + 2:3 * sub + 3, :]

        @pl.when((i == 0) & (j == 0))
        def _():
            sm_ref[...] = jnp.zeros_like(sm_ref)

        @pl.when(j == 0)
        def _():
            dxo_v = dxo_ref[...]
            dy_ref[...] = (0.5 * gate * dxo_v).astype(bf16)
            sm_ref[2:3, :] += jnp.sum(0.5 * y_ref[...].astype(f32) * dxo_v, axis=0, keepdims=True)
            acc[...] = jnp.zeros_like(acc)

        part = None
        for s in range(SH_STEP):
            dhid = _dot_nt(dy_ref[...], wd_ref[s * FB:(s + 1) * FB, :])
            da = (dhid * ga_ref[s].astype(f32)).astype(bf16)
            du = (dhid * sa_ref[s].astype(f32)).astype(bf16)
            da_ref[s] = da
            du_ref[s] = du
            t = _dot(da, wg_ref[s * FB:(s + 1) * FB, :]) + _dot(du, wu_ref[s * FB:(s + 1) * FB, :])
            part = t if part is None else part + t
        acc[...] += part

        @pl.when(j == N_CHIPS // SH_STEP - 1)
        def _():
            g = g_ref[sub:sub + 1, :]
            scale = mod_ref[3 * sub + 1:3 * sub + 2, :]
            _, xhat, rstd = _norm_fwd(x_ref[...], g, mod_ref[3 * sub:3 * sub + 1, :], scale)
            dx, dshift, dscale, dg = _norm_bwd(acc[...], xhat, rstd, g, scale)
            dxi_ref[...] = dxo_ref[...] + dx
            sm_ref[0:1, :] += dshift
            sm_ref[1:2, :] += dscale
            sm_ref[3:4, :] += dg

    row = pl.BlockSpec((TM, D), lambda i, j: (i, 0))
    hidb = pl.BlockSpec((SH_STEP, TM, FB), lambda i, j: (j, i, 0))
    wcol = pl.BlockSpec((SH_STEP * FB, D), lambda i, j: (j, 0))
    return pl.pallas_call(
        body, name="ffn_bwd1", grid=(S // TM, N_CHIPS // SH_STEP),
        out_shape=[SDS((S, D), f32), SDS((N_CHIPS, S, FB), bf16), SDS((N_CHIPS, S, FB), bf16), SDS((S, D), bf16), SDS((8, D), f32)],
        in_specs=[row, row, pl.BlockSpec((9, D), lambda i, j: (0, 0)), pl.BlockSpec((3, D), lambda i, j: (0, 0)), row,
                  hidb, hidb, wcol, wcol, pl.BlockSpec((SH_STEP * FB, D), lambda i, j: (j, 0))],
        out_specs=[row, hidb, hidb, row, pl.BlockSpec((8, D), lambda i, j: (0, 0))],
        scratch_shapes=[pltpu.VMEM((TM, D), f32)],
        compiler_params=_cp("arbitrary", "arbitrary"),
    )(dxo, x, mod9, g3, y, ga, sa, wg, wu, wd)


def _ffn_bwd2(h, da, du, hid, dy):
    S = h.shape[0]
    ni = S // TMW

    def body(h_ref, da_ref, du_ref, hid_ref, dy_ref, dwg_ref, dwu_ref, dwd_ref, ag, au, ad):
        i = pl.program_id(1)

        @pl.when(i == 0)
        def _():
            ag[...] = jnp.zeros_like(ag)
            au[...] = jnp.zeros_like(au)
            ad[...] = jnp.zeros_like(ad)

        hv = h_ref[...]
        ag[...] += _dot_tn(da_ref[...], hv)
        au[...] += _dot_tn(du_ref[...], hv)
        ad[...] += _dot_tn(hid_ref[...], dy_ref[...])

        @pl.when(i == ni - 1)
        def _():
            dwg_ref[...] = ag[...].astype(bf16)
            dwu_ref[...] = au[...].astype(bf16)
            dwd_ref[...] = ad[...].astype(bf16)

    row = pl.BlockSpec((TMW, D), lambda j, i: (i, 0))
    hidb = pl.BlockSpec((None, TMW, FB), lambda j, i: (j, i, 0))
    wrow = pl.BlockSpec((FB, D), lambda j, i: (j, 0))
    return pl.pallas_call(
        body, name="ffn_bwd2", grid=(N_CHIPS, ni),
        out_shape=[SDS((N_CHIPS * FB, D), bf16)] * 3,
        in_specs=[row, hidb, hidb, hidb, row],
        out_specs=[wrow, wrow, wrow],
        scratch_shapes=[pltpu.VMEM((FB, D), f32)] * 3,
        compiler_params=_cp("arbitrary", "arbitrary"),
    )(h, da, du, hid, dy)


def _mix_qkv(x, mod9, g3, win):
    S = x.shape[0]

    def body(x_ref, mod_ref, g_ref, w_ref, h_ref, o_ref):
        @pl.when(pl.program_id(1) == 0)
        def _():
            h, _, _ = _norm_fwd(x_ref[...], g_ref[1:2, :], mod_ref[3:4, :], mod_ref[4:5, :])
            h_ref[...] = h.astype(bf16)

        o_ref[...] = _dot(h_ref[...], w_ref[...]).astype(bf16)

    row = pl.BlockSpec((TMP, D), lambda i, j: (i, 0))
    return pl.pallas_call(
        body, name="mix_qkv", grid=(S // TMP, QKV_W // CBQ),
        out_shape=[SDS((S, D), bf16), SDS((S, QKV_W), bf16)],
        in_specs=[row, pl.BlockSpec((9, D), lambda i, j: (0, 0)), pl.BlockSpec((3, D), lambda i, j: (0, 0)),
                  pl.BlockSpec((D, CBQ), lambda i, j: (0, j))],
        out_specs=[row, pl.BlockSpec((TMP, CBQ), lambda i, j: (i, j))],
        compiler_params=_cp("arbitrary", "arbitrary"),
    )(x, mod9, g3, win)


def _mix_rest(h, win):
    S = h.shape[0]
    off = QKV_W // CB

    def body(h_ref, w_ref, o_ref):
        o_ref[...] = _dot(h_ref[...], w_ref[...]).astype(bf16)

    return pl.pallas_call(
        body, name="mix_rest", grid=(S // TMP, REST_W // CB),
        out_shape=SDS((S, REST_W), bf16),
        in_specs=[pl.BlockSpec((TMP, D), lambda i, j: (i, 0)), pl.BlockSpec((D, CB), lambda i, j: (0, off + j))],
        out_specs=pl.BlockSpec((TMP, CB), lambda i, j: (i, j)),
        compiler_params=_cp("arbitrary", "arbitrary"),
    )(h, win)


def _widen(srcs, dsts):
    for src, dst in zip(srcs, dsts):
        dst[...] = src[...].astype(f32)


def _qkv_scratch(R, Rb):
    return [pltpu.VMEM((R, 128), f32), pltpu.VMEM((R, 128), f32), pltpu.VMEM((Rb, 128), f32),
            pltpu.VMEM((R, 128), f32), pltpu.VMEM((Rb, 128), f32)]


def _attn_fwd(qkv, bias, g):
    S = qkv.shape[0]
    d = DILATIONS[g]
    nq = Q_BLOCKS[g]
    Rb = BLK * d
    R = Rb * nq
    nb = S // R
    qb, kb, vb = 4 * g, 12 + 4 * g, 24 + 4 * g

    def body(q_in, kc_in, kp_in, vc_in, vp_in, b_ref, o_ref, l_ref, q_ref, kc_ref, kp_ref, vc_ref, vp_ref):
        n = pl.program_id(1)
        col = lax.broadcasted_iota(jnp.int32, (BLK, 2 * BLK), 1)
        first = jnp.where((col < BLK) & (n == 0), NEG, 0.0)
        head0 = lax.broadcasted_iota(jnp.int32, (1, 2 * HD), 1) < HD
        _widen((q_in, kc_in, kp_in, vc_in, vp_in), (q_ref, kc_ref, kp_ref, vc_ref, vp_ref))

        def one(b, r):
            sl = pl.ds(b * Rb + r, BLK, stride=d)
            q = q_ref[sl, :]
            if b == 0:
                kp, vp = kp_ref[pl.ds(r, BLK, stride=d), :], vp_ref[pl.ds(r, BLK, stride=d), :]
            else:
                before = pl.ds((b - 1) * Rb + r, BLK, stride=d)
                kp, vp = kc_ref[before, :], vc_ref[before, :]
            kk = jnp.concatenate([kp, kc_ref[sl, :]], axis=0).astype(bf16)
            vv = jnp.concatenate([vp, vc_ref[sl, :]], axis=0).astype(bf16)
            os, ls = [], []
            for hh in range(2):
                qm = jnp.where(head0 if hh == 0 else ~head0, q, 0.0).astype(bf16)
                s = _dot_nt(qm, kk) * SCALE + b_ref[hh]
                if b == 0:
                    s = s + first
                m = jnp.max(s, axis=-1, keepdims=True)
                p = jnp.exp(s - m)
                l = jnp.sum(p, axis=-1, keepdims=True)
                os.append(_dot(p.astype(bf16), vv) / l)
                ls.append(m + jnp.log(l))
            o_ref[sl, :] = jnp.where(head0, os[0], os[1])
            l_ref[sl, :] = jnp.where(head0, ls[0], ls[1])

        for b in range(nq):
            if d == 1:
                one(b, 0)
            else:
                lax.fori_loop(0, d, lambda r, carry, b=b: (one(b, r), carry)[1], 0, unroll=4)

    def blk(cb, prev):
        if prev:
            return pl.BlockSpec((Rb, 128), lambda hp, n: (jnp.maximum(n * nq - 1, 0), cb + hp))
        return pl.BlockSpec((R, 128), lambda hp, n: (n, cb + hp))

    outb = pl.BlockSpec((R, 128), lambda hp, n: (n, hp))
    return pl.pallas_call(
        body, name=f"attn_fwd_d{d}", grid=(4, nb),
        out_shape=[SDS((S, 512), f32), SDS((S, 512), f32)],
        in_specs=[blk(qb, False), blk(kb, False), blk(kb, True), blk(vb, False), blk(vb, True),
                  pl.BlockSpec((2, BLK, 2 * BLK), lambda hp, n: (4 * g + hp, 0, 0))],
        out_specs=[outb, outb],
        scratch_shapes=_qkv_scratch(R, Rb),
        compiler_params=_cp("arbitrary", "arbitrary"),
    )(qkv, qkv, qkv, qkv, qkv, bias)


def _attn_bwd(qkv, do, o, lse, bias, dq_all, dk_all, dv_all, g):
    S = qkv.shape[0]
    d = DILATIONS[g]
    nq = Q_BLOCKS[g]
    Rb = BLK * d
    R = Rb * nq
    nb = S // R
    qb, kb, vb = 4 * g, 12 + 4 * g, 24 + 4 * g

    def body(q_in, kc_in, kp_in, vc_in, vp_in, do_ref, o_ref, l_ref, b_ref, dqi, dki, dvi,
             dq_out, dk_out, dv_out, ds_ref, ck, cv, tk, tv, dq_ref, q_ref, kc_ref, kp_ref, vc_ref, vp_ref):
        n = pl.program_id(1)
        col = lax.broadcasted_iota(jnp.int32, (BLK, 2 * BLK), 1)
        first = jnp.where((col < BLK) & (n == 0), NEG, 0.0)

        @pl.when(n == 0)
        def _():
            ck[...] = jnp.zeros_like(ck)
            cv[...] = jnp.zeros_like(cv)
            ds_ref[...] = jnp.zeros_like(ds_ref)

        @pl.when(n < nb)
        def _():
            head0 = lax.broadcasted_iota(jnp.int32, (1, 2 * HD), 1) < HD
            _widen((q_in, kc_in, kp_in, vc_in, vp_in), (q_ref, kc_ref, kp_ref, vc_ref, vp_ref))

            def one(b, r):
                sl = pl.ds(b * Rb + r, BLK, stride=d)
                before = pl.ds((max(b, 1) - 1) * Rb + r, BLK, stride=d)
                q = q_ref[sl, :]
                if b == 0:
                    kp, vp = kp_ref[pl.ds(r, BLK, stride=d), :], vp_ref[pl.ds(r, BLK, stride=d), :]
                else:
                    kp, vp = kc_ref[before, :], vc_ref[before, :]
                kk = jnp.concatenate([kp, kc_ref[sl, :]], axis=0).astype(bf16)
                vv = jnp.concatenate([vp, vc_ref[sl, :]], axis=0).astype(bf16)
                dov, lv = do_ref[sl, :], l_ref[sl, :]
                prod = dov * o_ref[sl, :]
                qb, dob = q.astype(bf16), dov.astype(bf16)
                dqs, dks, dvs = [], [], []
                for hh in range(2):
                    msk = head0 if hh == 0 else ~head0
                    qm = jnp.where(msk, q, 0.0).astype(bf16)
                    dom = jnp.where(msk, dov, 0.0).astype(bf16)
                    dsum = jnp.sum(jnp.where(msk, prod, 0.0), axis=-1, keepdims=True)
                    s = _dot_nt(qm, kk) * SCALE + b_ref[hh]
                    if b == 0:
                        s = s + first
                    p = jnp.exp(s - lv[:, HD * hh:HD * hh + 1])
                    ds = p * (_dot_nt(dom, vv) - dsum)
                    ds_ref[hh] += ds
                    dsb = ds.astype(bf16)
                    dqs.append(_dot(dsb, kk) * SCALE)
                    dks.append(_dot_tn(dsb, qb) * SCALE)
                    dvs.append(_dot_tn(p.astype(bf16), dob))
                dq_ref[sl, :] = jnp.where(head0, dqs[0], dqs[1])
                dk = jnp.where(head0, dks[0], dks[1])
                dv = jnp.where(head0, dvs[0], dvs[1])
                tk[sl, :] = dk[BLK:]
                tv[sl, :] = dv[BLK:]
                if b == 0:
                    prev_rows = pl.ds((nq - 1) * Rb + r, BLK, stride=d)
                    ck[prev_rows, :] += dk[:BLK]
                    cv[prev_rows, :] += dv[:BLK]
                else:
                    tk[before, :] += dk[:BLK]
                    tv[before, :] += dv[:BLK]

            for b in range(nq):
                if d == 1:
                    one(b, 0)
                else:
                    lax.fori_loop(0, d, lambda r, carry, b=b: (one(b, r), carry)[1], 0, unroll=4)
            dq_out[...] = dq_ref[...].astype(bf16)
            dk_out[...] = ck[...].astype(bf16)
            dv_out[...] = cv[...].astype(bf16)
            ck[...] = tk[...]
            cv[...] = tv[...]

        @pl.when(n == nb)
        def _():
            dk_out[...] = ck[...].astype(bf16)
            dv_out[...] = cv[...].astype(bf16)

    last = nb - 1

    def blk(cb, prev):
        if prev:
            return pl.BlockSpec((Rb, 128), lambda hp, n: (jnp.maximum(jnp.minimum(n, last) * nq - 1, 0), cb + hp))
        return pl.BlockSpec((R, 128), lambda hp, n: (jnp.minimum(n, last), cb + hp))

    cur = pl.BlockSpec((R, 128), lambda hp, n: (jnp.minimum(n, last), hp))
    anyspec = pl.BlockSpec(memory_space=pl.ANY)
    dqo = pl.BlockSpec((R, 128), lambda hp, n: (jnp.minimum(n, last), 4 * g + hp))
    dko = pl.BlockSpec((R, 128), lambda hp, n: (jnp.maximum(n - 1, 0), 4 * g + hp))
    return pl.pallas_call(
        body, name=f"attn_bwd_d{d}", grid=(4, nb + 1),
        out_shape=[SDS((S, 1536), bf16), SDS((S, 1536), bf16), SDS((S, 1536), bf16), SDS((8, BLK, 2 * BLK), f32)],
        in_specs=[blk(qb, False), blk(kb, False), blk(kb, True), blk(vb, False), blk(vb, True), cur, cur, cur,
                  pl.BlockSpec((2, BLK, 2 * BLK), lambda hp, n: (4 * g + hp, 0, 0)), anyspec, anyspec, anyspec],
        out_specs=[dqo, dko, dko, pl.BlockSpec((2, BLK, 2 * BLK), lambda hp, n: (hp, 0, 0))],
        scratch_shapes=[pltpu.VMEM((R, 128), f32)] * 5 + _qkv_scratch(R, Rb),
        input_output_aliases={9: 0, 10: 1, 11: 2},
        compiler_params=_cp("arbitrary", "arbitrary"),
    )(qkv, qkv, qkv, qkv, qkv, do, o, lse, bias, dq_all, dk_all, dv_all)


def _conv_z(cc, ch, hc, hh, cw_ref, first):
    halo = jnp.where(first, 0.0, hc.astype(f32) * hh.astype(f32))
    T = jnp.concatenate([halo, cc * ch], axis=0)
    z = cw_ref[2:3, :] * T + cw_ref[1:2, :] * pltpu.roll(T, 1, 0) + cw_ref[0:1, :] * pltpu.roll(T, 2, 0)
    return T, z[HALO:]


def _rest_specs(tm, with_next):
    per = tm // HALO
    specs = [pl.BlockSpec((tm, D), functools.partial(lambda i, k: (i, k), k=k)) for k in range(5)]
    specs += [pl.BlockSpec((HALO, D), functools.partial(lambda i, k: (jnp.maximum(i * per - 1, 0), k), k=k)) for k in (1, 2)]
    return specs


def _mix_out_fwd(x, mod9, rest, ogs, lgs, cw, wco, wao, wo):
    S = x.shape[0]
    tm = TMXF

    def body(x_ref, mod_ref, cb_ref, cc_ref, ch_ref, gc_ref, ga_ref, hc_ref, hh_ref,
             o0, o1, o2, l0, l1, l2, cw_ref, wco_ref, wao_ref, wo_ref,
             xo_ref, o_ref, lse_ref, yc_ref, ya_ref, out_ref):
        i = pl.program_id(0)
        lv = [l0[...], l1[...], l2[...]]
        mx = jnp.maximum(jnp.maximum(lv[0], lv[1]), lv[2])
        es = [jnp.exp(l - mx) for l in lv]
        den = es[0] + es[1] + es[2]
        o = (es[0] / den) * o0[...] + (es[1] / den) * o1[...] + (es[2] / den) * o2[...]
        o_ref[...] = o
        lse_ref[...] = mx + jnp.log(den)
        _, z = _conv_z(cc_ref[...].astype(f32), ch_ref[...].astype(f32), hc_ref[...], hh_ref[...], cw_ref, i == 0)
        p = (cb_ref[...].astype(f32) * z).astype(bf16)
        yc = _dot(p, wco_ref[...])
        ya = _dot(o.astype(bf16), wao_ref[...])
        yc_ref[...] = yc.astype(bf16)
        ya_ref[...] = ya.astype(bf16)
        merged = _sigmoid(gc_ref[...].astype(f32)) * yc + _sigmoid(ga_ref[...].astype(f32)) * ya
        out = _dot(merged.astype(bf16), wo_ref[...])
        out_ref[...] = out.astype(bf16)
        xo_ref[...] = x_ref[...] + mod_ref[5:6, :] * out

    row = pl.BlockSpec((tm, D), lambda i: (i, 0))
    att = pl.BlockSpec((tm, 512), lambda i: (i, 0))
    full = lambda shp: pl.BlockSpec(shp, lambda i: (0, 0))
    return pl.pallas_call(
        body, name="mix_out_fwd", grid=(S // tm,),
        out_shape=[SDS((S, D), f32), SDS((S, 512), f32), SDS((S, 512), f32), SDS((S, D), bf16), SDS((S, D), bf16), SDS((S, D), bf16)],
        in_specs=[row, full((9, D))] + _rest_specs(tm, False) + [att] * 6 + [full((3, D)), full((D, D)), full((512, D)), full((D, D))],
        out_specs=[row, att, att, row, row, row],
        compiler_params=_cp("arbitrary"),
    )(x, mod9, *([rest] * 7), *ogs, *lgs, cw, wco, wao, wo)


def _mix_out_bwd(dxo, mod9, outv, yc, ya, rest, o, cw, wco, wao, wo):
    S = dxo.shape[0]
    tm = TMX
    ni = S // tm

    def body(dxo_ref, mod_ref, out_ref, yc_ref, ya_ref, cb_ref, cc_ref, ch_ref, gc_ref, ga_ref, hc_ref, hh_ref,
             o_ref, cw_ref, wco_ref, wao_ref, wo_ref,
             dp_ref, dg2_ref, do_ref, dwco_ref, dwao_ref, dwo_ref, sm_ref, aco, aao, ao):
        i = pl.program_id(0)

        @pl.when(i == 0)
        def _():
            sm_ref[...] = jnp.zeros_like(sm_ref)
            aco[...] = jnp.zeros_like(aco)
            aao[...] = jnp.zeros_like(aao)
            ao[...] = jnp.zeros_like(ao)

        dxo_v = dxo_ref[...]
        sm_ref[2:3, :] += jnp.sum(out_ref[...].astype(f32) * dxo_v, axis=0, keepdims=True)
        dout = (mod_ref[5:6, :] * dxo_v).astype(bf16)
        dmerged = _dot_nt(dout, wo_ref[...])
        sc, sa = _sigmoid(gc_ref[...].astype(f32)), _sigmoid(ga_ref[...].astype(f32))
        ycv, yav = yc_ref[...].astype(f32), ya_ref[...].astype(f32)
        ao[...] += _dot_tn((sc * ycv + sa * yav).astype(bf16), dout)
        dyc = (dmerged * sc).astype(bf16)
        dya = (dmerged * sa).astype(bf16)
        dg2_ref[:, :D] = (dmerged * ycv * sc * (1.0 - sc)).astype(bf16)
        dg2_ref[:, D:] = (dmerged * yav * sa * (1.0 - sa)).astype(bf16)
        dp_ref[...] = _dot_nt(dyc, wco_ref[...]).astype(bf16)
        _, z = _conv_z(cc_ref[...].astype(f32), ch_ref[...].astype(f32), hc_ref[...], hh_ref[...], cw_ref, i == 0)
        aco[...] += _dot_tn((cb_ref[...].astype(f32) * z).astype(bf16), dyc)
        do_ref[...] = _dot_nt(dya, wao_ref[...])
        aao[...] += _dot_tn(o_ref[...].astype(bf16), dya)

        @pl.when(i == ni - 1)
        def _():
            dwco_ref[...] = aco[...].astype(bf16)
            dwao_ref[...] = aao[...].astype(bf16)
            dwo_ref[...] = ao[...].astype(bf16)

    row = pl.BlockSpec((tm, D), lambda i: (i, 0))
    att = pl.BlockSpec((tm, 512), lambda i: (i, 0))
    full = lambda shp: pl.BlockSpec(shp, lambda i: (0, 0))
    return pl.pallas_call(
        body, name="mix_out_bwd", grid=(ni,),
        out_shape=[SDS((S, D), bf16), SDS((S, 2 * D), bf16), SDS((S, 512), f32),
                   SDS((D, D), bf16), SDS((512, D), bf16), SDS((D, D), bf16), SDS((8, D), f32)],
        in_specs=[row, full((9, D)), row, row, row] + _rest_specs(tm, False) + [att, full((3, D)), full((D, D)), full((512, D)), full((D, D))],
        out_specs=[row, pl.BlockSpec((tm, 2 * D), lambda i: (i, 0)), att, full((D, D)), full((512, D)), full((D, D)), full((8, D))],
        scratch_shapes=[pltpu.VMEM((D, D), f32), pltpu.VMEM((512, D), f32), pltpu.VMEM((D, D), f32)],
        compiler_params=_cp("arbitrary"),
    )(dxo, mod9, outv, yc, ya, *([rest] * 7), o, cw, wco, wao, wo)


def _conv_bwd(dp, rest, cw):
    S = dp.shape[0]
    tm = TM
    per = tm // HALO
    nh = S // HALO
    ni = S // tm

    def body(dp_ref, dpn_ref, cb_ref, cbn_ref, cc_ref, ch_ref, hc_ref, hh_ref, cw_ref, d3_ref, sm_ref):
        i = pl.program_id(0)

        @pl.when(i == 0)
        def _():
            sm_ref[...] = jnp.zeros_like(sm_ref)

        cc, ch = cc_ref[...].astype(f32), ch_ref[...].astype(f32)
        T, z = _conv_z(cc, ch, hc_ref[...], hh_ref[...], cw_ref, i == 0)
        dpv = dp_ref[...].astype(f32)
        cbv = cb_ref[...].astype(f32)
        dz = dpv * cbv
        dzn = jnp.where(i == ni - 1, 0.0, dpn_ref[...].astype(f32) * cbn_ref[...].astype(f32))
        E = jnp.concatenate([dz, dzn], axis=0)
        ne = tm + HALO
        dT = cw_ref[2:3, :] * E + cw_ref[1:2, :] * pltpu.roll(E, ne - 1, 0) + cw_ref[0:1, :] * pltpu.roll(E, ne - 2, 0)
        dT = dT[:tm]
        d3_ref[:, :D] = (dpv * z).astype(bf16)
        d3_ref[:, D:2 * D] = (dT * ch).astype(bf16)
        d3_ref[:, 2 * D:] = (dT * cc).astype(bf16)
        sm_ref[2:3, :] += jnp.sum(dz * T[HALO:], axis=0, keepdims=True)
        sm_ref[1:2, :] += jnp.sum(dz * pltpu.roll(T, 1, 0)[HALO:], axis=0, keepdims=True)
        sm_ref[0:1, :] += jnp.sum(dz * pltpu.roll(T, 2, 0)[HALO:], axis=0, keepdims=True)

    row = pl.BlockSpec((tm, D), lambda i: (i, 0))
    nxt = pl.BlockSpec((HALO, D), lambda i: (jnp.minimum((i + 1) * per, nh - 1), 0))
    col = lambda k: pl.BlockSpec((tm, D), lambda i: (i, k))
    prv = lambda k: pl.BlockSpec((HALO, D), lambda i: (jnp.maximum(i * per - 1, 0), k))
    return pl.pallas_call(
        body, name="conv_bwd", grid=(ni,),
        out_shape=[SDS((S, 3 * D), bf16), SDS((8, D), f32)],
        in_specs=[row, nxt, col(0), nxt, col(1), col(2), prv(1), prv(2), pl.BlockSpec((3, D), lambda i: (0, 0))],
        out_specs=[pl.BlockSpec((tm, 3 * D), lambda i: (i, 0)), pl.BlockSpec((8, D), lambda i: (0, 0))],
        compiler_params=_cp("arbitrary"),
    )(dp, dp, rest, rest, rest, rest, rest, rest, cw)


_DU_RANGES = ((0, 3), (3, 6), (6, 9), (9, 15), (15, 19))
N_CBLK = IN_W // CB


def _mix_in_bwd_dh(dxo, x, mod9, g3, dus, win):
    S = x.shape[0]

    def body(dxo_ref, x_ref, mod_ref, g_ref, s0, s1, s2, s3, s4, w_ref, dxi_ref, sm_ref, acc):
        i, kb = pl.program_id(0), pl.program_id(1)

        @pl.when((i == 0) & (kb == 0))
        def _():
            sm_ref[...] = jnp.zeros_like(sm_ref)

        @pl.when(kb == 0)
        def _():
            acc[...] = jnp.zeros_like(acc)

        for src, (lo, hi) in zip((s0, s1, s2, s3, s4), _DU_RANGES):
            @pl.when((kb >= lo) & (kb < hi))
            def _(src=src):
                acc[...] += _dot_nt(src[...].astype(bf16), w_ref[...])

        @pl.when(kb == N_CBLK - 1)
        def _():
            g, scale = g_ref[1:2, :], mod_ref[4:5, :]
            _, xhat, rstd = _norm_fwd(x_ref[...], g, mod_ref[3:4, :], scale)
            dx, dshift, dscale, dg = _norm_bwd(acc[...], xhat, rstd, g, scale)
            dxi_ref[...] = dxo_ref[...] + dx
            sm_ref[0:1, :] += dshift
            sm_ref[1:2, :] += dscale
            sm_ref[3:4, :] += dg

    row = pl.BlockSpec((TMP, D), lambda i, kb: (i, 0))

    def src_spec(lo, hi):
        return pl.BlockSpec((TMP, CB), lambda i, kb: (i, jnp.clip(kb - lo, 0, hi - lo - 1)))

    return pl.pallas_call(
        body, name="mix_in_bwd_dh", grid=(S // TMP, N_CBLK),
        out_shape=[SDS((S, D), f32), SDS((8, D), f32)],
        in_specs=[row, row, pl.BlockSpec((9, D), lambda i, kb: (0, 0)), pl.BlockSpec((3, D), lambda i, kb: (0, 0))]
                 + [src_spec(lo, hi) for lo, hi in _DU_RANGES] + [pl.BlockSpec((D, CB), lambda i, kb: (0, kb))],
        out_specs=[row, pl.BlockSpec((8, D), lambda i, kb: (0, 0))],
        scratch_shapes=[pltpu.VMEM((TMP, D), f32)],
        compiler_params=_cp("arbitrary", "arbitrary"),
    )(dxo, x, mod9, g3, *dus, win)


def _mix_in_bwd_dw(h, dus):
    S = h.shape[0]
    ni = S // TMW

    def body(h_ref, s0, s1, s2, s3, s4, dw_ref, acc):
        kb, i = pl.program_id(0), pl.program_id(1)

        @pl.when(i == 0)
        def _():
            acc[...] = jnp.zeros_like(acc)

        for src, (lo, hi) in zip((s0, s1, s2, s3, s4), _DU_RANGES):
            @pl.when((kb >= lo) & (kb < hi))
            def _(src=src):
                rows = pl.ds(pl.multiple_of(i * TMW, TMW), TMW)
                acc[...] += _dot_tn(h_ref[rows, :], src[...].astype(bf16))

        @pl.when(i == ni - 1)
        def _():
            dw_ref[...] = acc[...].astype(bf16)

    def src_spec(lo, hi):
        def imap(kb, i):
            on = (kb >= lo) & (kb < hi)
            return (jnp.where(on, i, 0), jnp.clip(kb - lo, 0, hi - lo - 1))
        return pl.BlockSpec((TMW, CB), imap)

    return pl.pallas_call(
        body, name="mix_in_bwd_dw", grid=(N_CBLK, ni),
        out_shape=SDS((D, IN_W), bf16),
        in_specs=[pl.BlockSpec((S, D), lambda kb, i: (0, 0))] + [src_spec(lo, hi) for lo, hi in _DU_RANGES],
        out_specs=pl.BlockSpec((D, CB), lambda kb, i: (0, kb)),
        scratch_shapes=[pltpu.VMEM((D, CB), f32)],
        compiler_params=_cp("arbitrary", "arbitrary"),
    )(h, *dus)


def _loss_head(x, fg, tgt):
    S = x.shape[0]

    def body(x_ref, g_ref, t_ref, ls_ref, dx_ref, sm_ref):
        i = pl.program_id(0)

        @pl.when(i == 0)
        def _():
            ls_ref[...] = jnp.zeros_like(ls_ref)
            sm_ref[...] = jnp.zeros_like(sm_ref)

        xv, g = x_ref[...], g_ref[...]
        rstd = lax.rsqrt(jnp.mean(xv * xv, axis=-1, keepdims=True) + EPS)
        xhat = xv * rstd
        e = xhat * g - t_ref[...]
        ls_ref[...] += 0.5 * jnp.sum(jnp.mean(e * e, axis=-1, keepdims=True))
        dy = e * (1.0 / D)
        sm_ref[0:1, :] += jnp.sum(dy * xhat, axis=0, keepdims=True)
        dxh = dy * g
        dx_ref[...] = rstd * (dxh - xhat * jnp.mean(dxh * xhat, axis=-1, keepdims=True))

    row = pl.BlockSpec((TM, D), lambda i: (i, 0))
    return pl.pallas_call(
        body, name="loss_head", grid=(S // TM,),
        out_shape=[SDS((8, 128), f32), SDS((S, D), f32), SDS((8, D), f32)],
        in_specs=[row, pl.BlockSpec((1, D), lambda i: (0, 0)), row],
        out_specs=[pl.BlockSpec((8, 128), lambda i: (0, 0)), row, pl.BlockSpec((8, D), lambda i: (0, 0))],
        compiler_params=_cp("arbitrary"),
    )(x, fg, tgt)


def _adam(w, g, m, v):
    m2 = B1 * m + (1.0 - B1) * g
    v2 = B2 * v + (1.0 - B2) * (g * g)
    delta = -LR * ((m2 / BC1) / (jnp.sqrt(v2 / BC2) + AEPS) + WD * w)
    return delta, m2, v2


def _row_tile(rows, cols):
    for tr in (512, 352, 256, 128, 64):
        if rows % tr == 0 and tr * cols * 4 <= (3 << 19):
            return tr
    raise ValueError((rows, cols))


def _sum_slots(land):
    _, R, C = land.shape
    tr = _row_tile(R, C)

    def body(l_ref, t_ref):
        t = l_ref[0].astype(f32)
        for k in range(1, N_CHIPS):
            t = t + l_ref[k].astype(f32)
        t_ref[...] = t

    return pl.pallas_call(
        body, name="sum_slots", grid=(R // tr,),
        out_shape=SDS((R, C), f32),
        in_specs=[pl.BlockSpec((N_CHIPS, tr, C), lambda i: (0, i, 0))],
        out_specs=pl.BlockSpec((tr, C), lambda i: (i, 0)),
        compiler_params=_cp("arbitrary"),
    )(land)


def _adamw_pair(w2, m2, v2, ta, tb, outs, slot):
    R, C = ta.shape
    tr = _row_tile(R, C)
    nrt = R // tr

    def body(w_ref, m_ref, v_ref, ta_ref, tb_ref, g_in, d_in, m_in, v_in, g_ref, d_ref, mo_ref, vo_ref):
        g = ta_ref[...].astype(f32) + tb_ref[...].astype(f32)
        delta, mn, vn = _adam(w_ref[...], g, m_ref[...], v_ref[...])
        g_ref[...] = g
        d_ref[...] = delta
        mo_ref[...] = mn
        vo_ref[...] = vn

    big = pl.BlockSpec((tr, C), lambda i: (slot * nrt + i, 0))
    loc = pl.BlockSpec((tr, C), lambda i: (i, 0))
    anyspec = pl.BlockSpec(memory_space=pl.ANY)
    return pl.pallas_call(
        body, name="adamw_pair", grid=(nrt,),
        out_shape=[SDS(o.shape, f32) for o in outs],
        in_specs=[big, big, big, loc, loc] + [anyspec] * 4,
        out_specs=[big] * 4,
        input_output_aliases={5: 0, 6: 1, 7: 2, 8: 3},
        compiler_params=_cp("arbitrary"),
    )(w2, m2, v2, ta, tb, *outs)


def _adamw_small(w, g, m, v):
    def body(w_ref, g_ref, m_ref, v_ref, d_ref, mo_ref, vo_ref):
        delta, mn, vn = _adam(w_ref[...], g_ref[...], m_ref[...], v_ref[...])
        d_ref[...] = delta
        mo_ref[...] = mn
        vo_ref[...] = vn

    return pl.pallas_call(body, name="adamw_small", out_shape=[SDS(w.shape, f32)] * 3)(w, g, m, v)


def _ada_w_update(cs_all, dmod_sh, w, m, v):
    tr = 256

    def body(cs_ref, dm_ref, w_ref, m_ref, v_ref, g_ref, d_ref, mo_ref, vo_ref):
        g = _dot_tn(cs_ref[...].astype(bf16), dm_ref[...].astype(bf16))
        delta, mn, vn = _adam(w_ref[...], g, m_ref[...], v_ref[...])
        g_ref[...] = g
        d_ref[...] = delta
        mo_ref[...] = mn
        vo_ref[...] = vn

    blk = pl.BlockSpec((None, tr, ADA_SH), lambda l, i: (l, i, 0))
    return pl.pallas_call(
        body, name="ada_w_update", grid=(DEPTH, D // tr),
        out_shape=[SDS(w.shape, f32)] * 4,
        in_specs=[pl.BlockSpec((8, tr), lambda l, i: (0, i)), pl.BlockSpec((None, 8, ADA_SH), lambda l, i: (l, 0, 0)), blk, blk, blk],
        out_specs=[blk] * 4,
        compiler_params=_cp("arbitrary", "arbitrary"),
    )(cs_all, dmod_sh, w, m, v)


def _sum_devices(gathered):
    _, R, C = gathered.shape

    def body(g_ref, o_ref):
        t = g_ref[0]
        for k in range(1, 8):
            t = t + g_ref[k]
        o_ref[...] = t

    return pl.pallas_call(body, name="sum_devices", out_shape=SDS((R, C), f32))(gathered)


def _layer_fwd(x, mod9, g3, cw, getw, bias):
    W = {}

    def take(gname, after, mod9):
        w, tok = getw(gname, after)
        W.update(w)
        return mod9 if tok is None else mod9 + tok[0, 0]

    mod9 = take("A", x, mod9)
    x1, h1, a1, u1, hid1, y1 = _ffn_fwd(x, mod9, g3, W["wg0"], W["wu0"], W["wd0"], 0)
    mod9 = take("B", x1, mod9)
    hm, qkv = _mix_qkv(x1, mod9, g3, W["win"])
    rest = _mix_rest(hm, W["win"])
    ogs, lgs = [], []
    for g in range(3):
        og, lg = _attn_fwd(qkv, bias, g)
        ogs.append(og)
        lgs.append(lg)
    mod9 = take("C", ogs[2], mod9)
    x2, o, lse, yc, ya, outv = _mix_out_fwd(x1, mod9, rest, ogs, lgs, cw, W["wco"], W["wao"], W["wo"])
    mod9 = take("D", x2, mod9)
    x3, h3, a3, u3, hid3, y3 = _ffn_fwd(x2, mod9, g3, W["wg1"], W["wu1"], W["wd1"], 2)
    saved = dict(x0=x, x1=x1, x2=x2, h1=h1, a1=a1, u1=u1, hid1=hid1, y1=y1, hm=hm, qkv=qkv, rest=rest, o=o, lse=lse, yc=yc, ya=ya,
                 outv=outv, h3=h3, a3=a3, u3=u3, hid3=hid3, y3=y3)
    return x3, saved, W


def _layer_bwd(dx, sv, mod9, g3, cw, W, bias, emit):
    S = dx.shape[0]
    dw = {}

    def send(gname, mod9):
        tok = emit(gname, dw)
        return mod9 if tok is None else mod9 + tok[0, 0]

    dx2, da, du, dy, sm3 = _ffn_bwd1(dx, sv["x2"], mod9, g3, sv["y3"], sv["a3"], sv["u3"], W["wg1"], W["wu1"], W["wd1"], 2)
    dw["wg1"], dw["wu1"], dw["wd1"] = _ffn_bwd2(sv["h3"], da, du, sv["hid3"], dy)
    mod9 = send("D", mod9)
    dp, dg2, do, dw["wco"], dw["wao"], dw["wo"], smo = _mix_out_bwd(
        dx2, mod9, sv["outv"], sv["yc"], sv["ya"], sv["rest"], sv["o"], cw, W["wco"], W["wao"], W["wo"])
    mod9_c = send("C", mod9)
    cw = cw + (mod9_c - mod9)[0:1, :]
    mod9 = mod9_c
    d3, smc = _conv_bwd(dp, sv["rest"], cw)
    dq = lax.empty((S, 1536), bf16)
    dk = lax.empty((S, 1536), bf16)
    dv = lax.empty((S, 1536), bf16)
    dsaccs = []
    for g in range(3):
        dq, dk, dv, dsg = _attn_bwd(sv["qkv"], do, sv["o"], sv["lse"], bias, dq, dk, dv, g)
        dsaccs.append(dsg)
    dus = (dq, dk, dv, d3, dg2)
    dx1, smm = _mix_in_bwd_dh(dx2, sv["x1"], mod9, g3, dus, W["win"])
    dw["win"] = _mix_in_bwd_dw(sv["hm"], dus)
    mod9 = send("B", mod9)
    dx0, da, du, dy, sm1 = _ffn_bwd1(dx1, sv["x0"], mod9, g3, sv["y1"], sv["a1"], sv["u1"], W["wg0"], W["wu0"], W["wd0"], 0)
    dw["wg0"], dw["wu0"], dw["wd0"] = _ffn_bwd2(sv["h1"], da, du, sv["hid1"], dy)
    send("A", mod9)
    dmod = jnp.concatenate([sm1[0:3], smm[0:2], smo[2:3], sm3[0:3]], axis=0)
    dng = jnp.concatenate([sm1[3:4], smm[3:4], sm3[3:4]], axis=0)
    return dx0, dmod, dng, smc[0:3], jnp.concatenate(dsaccs, axis=0)


def _chip_cols(a, chip, width):
    return lax.dynamic_slice_in_dim(a, chip * width, width, axis=a.ndim - 1)


def kernel(x, c, ada_w, ada_b, norm_g, ffn_w_gate, ffn_w_up, ffn_w_down, w_in, conv_w, w_conv_out, w_attn_out, w_o, rel_bias, final_g, loss_target, m_ada_w, m_ada_b, m_norm_g, m_ffn_w_gate, m_ffn_w_up, m_ffn_w_down, m_w_in, m_conv_w, m_w_conv_out, m_w_attn_out, m_w_o, m_rel_bias, m_final_g, v_ada_w, v_ada_b, v_norm_g, v_ffn_w_gate, v_ffn_w_up, v_ffn_w_down, v_w_in, v_conv_w, v_w_conv_out, v_w_attn_out, v_w_o, v_rel_bias, v_final_g):
    ix, iy, ic = lax.axis_index("x"), lax.axis_index("y"), lax.axis_index("c")
    chip = 2 * ix + iy
    dev = 4 * ix + 2 * iy + ic
    xs = x.reshape(x.shape[1:])
    S = xs.shape[0]
    qd = D // N_CHIPS

    chip_arr = jnp.reshape(chip, (1,)).astype(jnp.int32)
    names = [w[0] for w in WCLASSES]

    tr2 = lambda a: jnp.swapaxes(a, -1, -2)
    wg_t, wu_t = tr2(ffn_w_gate), tr2(ffn_w_up)

    def layer_shards(l):
        return [(wg_t, (l, 0)), (wu_t, (l, 0)), (ffn_w_down, (l, 0)), (wg_t, (l, 1)), (wu_t, (l, 1)),
                (ffn_w_down, (l, 1)), (w_in, (l,)), (w_conv_out, (l,)), (w_attn_out, (l,)), (w_o, (l,))]

    started = {}
    extra_starts = {(0, "A"): [(0, "B")], (0, "B"): [(0, "C"), (0, "D"), (1, "A")]}

    casts = {}

    def cast_group(l, gname, after):
        shards = layer_shards(l)
        casts[(l, gname)] = _gather_group_cast(GROUPS[gname], [shards[q] for q in GROUPS[gname]], chip_arr, after)

    def start_gather(l, gname, after):
        started[(l, gname)] = _gather_group_start(f"l{l}{gname}", GROUPS[gname], casts[(l, gname)], after)
        return started[(l, gname)][-1]

    pad8 = lambda a: jnp.pad(a, ((0, -a.shape[0] % 8), (0, 0)))
    pack = jnp.concatenate([pad8(c), pad8(norm_g.reshape(3, D)), pad8(conv_w.reshape(3, D))], axis=0)
    g1 = _allgather_small(pack).reshape(8, 24, D)
    c_all = g1[:, 0]
    by_chip = g1[0::2]
    ng_full = jnp.concatenate([by_chip[j, 8:11].reshape(DEPTH, 3, qd) for j in range(N_CHIPS)], axis=-1)
    cw_full = jnp.concatenate([by_chip[j, 16:19].reshape(DEPTH, 3, qd) for j in range(N_CHIPS)], axis=-1)
    mod_sh, cs_all = _mod_shards(c_all, ada_w, _chip_cols(ada_b, chip, ADA_SH))
    g2 = _allgather_small(mod_sh.reshape(DEPTH * 8, ADA_SH)).reshape(8, DEPTH, 8, ADA_SH)
    mine = lax.dynamic_index_in_dim(g2[0::2], dev, axis=2, keepdims=False)
    mod = jnp.transpose(mine, (1, 0, 2)).reshape(DEPTH, 9, D)

    cast_group(0, "A", c)
    tok0 = start_gather(0, "A", mod)
    for l in range(DEPTH):
        for gname in GROUPS:
            if (l, gname) not in casts:
                cast_group(l, gname, tok0)
    buckets = jnp.asarray(_bucket_table())
    bias = _bias_blocks(rel_bias, buckets)
    last_cast = casts[(DEPTH - 1, "D")][-1]

    need_order = [(l, gname) for l in range(DEPTH) for gname in GROUPS]
    forwarded = {}

    def forward_gather(key, after):
        forwarded[key] = _gather_group_forward(f"l{key[0]}{key[1]}", GROUPS[key[1]], started[key], after)
        return forwarded[key][-1]

    def make_getw(l):
        def getw(gname, after):
            key = (l, gname)
            if key == (0, "A"):
                after = last_cast
            if key not in forwarded:
                after = forward_gather(key, after)
            full = _gather_group_wait(f"l{l}{gname}", GROUPS[gname], started[key][0], forwarded[key], after)
            tok = None
            before = set(started)
            for nl, ng in extra_starts.get(key, []) + [(l + 1, gname)]:
                if nl < DEPTH and (nl, ng) not in started:
                    tok = start_gather(nl, ng, full[0] if tok is None else tok)
            at = need_order.index(key) + 1
            if at < len(need_order) and need_order[at] in before and need_order[at] not in forwarded:
                tok = forward_gather(need_order[at], full[0] if tok is None else tok)
            return {names[q]: f for q, f in zip(GROUPS[gname], full)}, tok
        return getw

    Ws, saves = [], []
    xc = xs
    for l in range(DEPTH):
        xc, sv, W = _layer_fwd(xc, mod[l], ng_full[l], cw_full[l], make_getw(l), bias)
        Ws.append(W)
        saves.append(sv)

    ls, dx, smf = _loss_head(xc, final_g.reshape(1, D), loss_target.reshape(loss_target.shape[1:]))
    loss = lax.psum(ls[0, 0], ("x", "y", "c"))

    params = dict(wg=wg_t, wu=wu_t, wd=ffn_w_down, win=w_in, wco=w_conv_out, wao=w_attn_out, wo=w_o)
    moms = dict(wg=tr2(m_ffn_w_gate), wu=tr2(m_ffn_w_up), wd=m_ffn_w_down, win=m_w_in, wco=m_w_conv_out, wao=m_w_attn_out, wo=m_w_o)
    vars_ = dict(wg=tr2(v_ffn_w_gate), wu=tr2(v_ffn_w_up), wd=v_ffn_w_down, win=v_w_in, wco=v_w_conv_out, wao=v_w_attn_out, wo=v_w_o)
    flat = lambda a: a.reshape(-1, a.shape[-1])
    big_out = {k: [lax.empty(flat(p).shape, f32) for _ in range(4)] for k, p in params.items()}
    dmods, dngs, dcws, dsaccs = [None] * DEPTH, [None] * DEPTH, [None] * DEPTH, [None] * DEPTH

    def finish(l, gname, started, after):
        group = GROUPS[gname]
        pieces, lands = _scatter_group_wait(f"l{l}{gname}", group, started, after)
        ts = [_sum_own_slots(pieces[i], lands[i], *_cls(q), chip_arr) for i, q in enumerate(group)]
        tsib = _swap_sibling(ts)
        for i, q in enumerate(group):
            name = names[q]
            key = name.rstrip("01")
            slot = 2 * l + int(name[-1]) if name[-1] in "01" else l
            big_out[key] = _adamw_pair(flat(params[key]), flat(moms[key]), flat(vars_[key]), ts[i], tsib[i], big_out[key], slot)

    pending, tok = [], None
    for l in reversed(range(DEPTH)):
        modl = mod[l] if tok is None else mod[l] + tok[0, 0]
        mine = []

        def emit(gname, dw, l=l, mine=mine):
            prev = mine[-1][2][-1] if mine else dx
            mine.append((l, gname, _scatter_group_start(f"l{l}{gname}", GROUPS[gname], [dw[names[q]] for q in GROUPS[gname]], prev)))
            return mine[-1][2][-1]

        dx, dmods[l], dngs[l], dcws[l], dsaccs[l] = _layer_bwd(dx, saves[l], modl, ng_full[l], cw_full[l], Ws[l], bias, emit)
        for pl_, pg, pst in pending:
            finish(pl_, pg, pst, dx)
        pending, tok = mine, mine[-1][2][-1]
    for pl_, pg, pst in pending[:-1]:
        finish(pl_, pg, pst, pending[-1][2][-1])

    drb = jnp.transpose(_bias_grad(dsaccs, buckets)[:, 0, :NUM_BUCKETS])
    drb_row = jnp.pad(drb.reshape(1, NUM_BUCKETS * 24), ((0, 0), (0, D - NUM_BUCKETS * 24)))
    pack2 = jnp.concatenate([pad8(a) for a in dmods] + [pad8(a) for a in dngs] + [pad8(a) for a in dcws] + [smf, pad8(drb_row)], axis=0)
    n_rows = pack2.shape[0]
    g3 = _allgather_small(pack2).reshape(8, n_rows, D)
    tot = _sum_devices(g3)
    o_ng, o_cw, o_fg, o_rb = 16 * DEPTH, 24 * DEPTH, 32 * DEPTH, 32 * DEPTH + 8
    g_ada_b = jnp.stack([tot[16 * l:16 * l + 9] for l in range(DEPTH)]).reshape(DEPTH, 9 * D)
    g_norm_g = _chip_cols(jnp.stack([tot[o_ng + 8 * l:o_ng + 8 * l + 3] for l in range(DEPTH)]), chip, qd)
    g_conv_w = _chip_cols(jnp.stack([tot[o_cw + 8 * l:o_cw + 8 * l + 3] for l in range(DEPTH)]), chip, qd)
    g_final_g = tot[o_fg]
    g_rel_bias = tot[o_rb, :NUM_BUCKETS * 24].reshape(NUM_BUCKETS, 24)
    dmod_all = jnp.stack([g3[:, 16 * l:16 * l + 9].reshape(8, 9 * D) for l in range(DEPTH)])
    dmod_sh = _chip_cols(dmod_all, chip, ADA_SH)
    g_ada_w, d_ada_w, nm_ada_w, nv_ada_w = _ada_w_update(cs_all, dmod_sh, ada_w, m_ada_w, v_ada_w)

    def small(w, g, m, v):
        shp = w.shape
        to2 = lambda a: a.reshape(-1, shp[-1])
        return [o.reshape(shp) for o in _adamw_small(to2(w), to2(g), to2(m), to2(v))]

    d_ada_b, nm_ada_b, nv_ada_b = small(ada_b, g_ada_b, m_ada_b, v_ada_b)
    d_norm_g, nm_norm_g, nv_norm_g = small(norm_g, g_norm_g, m_norm_g, v_norm_g)
    d_conv_w, nm_conv_w, nv_conv_w = small(conv_w, g_conv_w, m_conv_w, v_conv_w)
    d_rel_bias, nm_rel_bias, nv_rel_bias = small(rel_bias, g_rel_bias, m_rel_bias, v_rel_bias)
    d_final_g, nm_final_g, nv_final_g = small(final_g, g_final_g, m_final_g, v_final_g)

    behind = nv_ada_w[0, 0:8, 0:128]
    for key in big_out:
        behind = behind + big_out[key][3][0:8, 0:128]
    finish(*pending[-1], behind)

    def big(key, which):
        out = big_out[key][which].reshape(params[key].shape)
        return tr2(out) if key in ("wg", "wu") else out

    grads = [g_ada_w, g_ada_b, g_norm_g, big("wg", 0), big("wu", 0), big("wd", 0), big("win", 0), g_conv_w, big("wco", 0),
             big("wao", 0), big("wo", 0), g_rel_bias, g_final_g]
    deltas = [d_ada_w, d_ada_b, d_norm_g, big("wg", 1), big("wu", 1), big("wd", 1), big("win", 1), d_conv_w, big("wco", 1),
              big("wao", 1), big("wo", 1), d_rel_bias, d_final_g]
    new_m = [nm_ada_w, nm_ada_b, nm_norm_g, big("wg", 2), big("wu", 2), big("wd", 2), big("win", 2), nm_conv_w, big("wco", 2),
             big("wao", 2), big("wo", 2), nm_rel_bias, nm_final_g]
    new_v = [nv_ada_w, nv_ada_b, nv_norm_g, big("wg", 3), big("wu", 3), big("wd", 3), big("win", 3), nv_conv_w, big("wco", 3),
             big("wao", 3), big("wo", 3), nv_rel_bias, nv_final_g]
    return (loss, dx.reshape(x.shape), *grads, *deltas, *new_m, *new_v)
```

```python
import functools

import numpy as np
import jax
import jax.numpy as jnp
from jax import lax
from jax.experimental import pallas as pl
from jax.experimental.pallas import tpu as pltpu

f32, bf16 = jnp.float32, jnp.bfloat16
SDS = jax.ShapeDtypeStruct
MESH = pl.DeviceIdType.MESH

D = 1024
DEPTH = 4
N_CHIPS = 4
FB = 704
HD = 64
QKV_W = 4608
REST_W = 5120
IN_W = QKV_W + REST_W
WIN_SH = IN_W // N_CHIPS
ADA_SH = 9 * D // N_CHIPS
BLK = 128
DILATIONS = (1, 4, 16)
Q_BLOCKS = (4, 2, 1)
NUM_BUCKETS, MAX_DISTANCE = 32, 2048
EPS = 1e-6
NEG = -1e30
SCALE = HD ** -0.5
LR, B1, B2, AEPS, WD, STEP = 0.001, 0.9, 0.999, 1e-08, 0.01, 10
BC1 = 1.0 - B1 ** STEP
BC2 = 1.0 - B2 ** STEP
VMEM_LIMIT = 56 * 1024 * 1024
TM = 512
TMW = 2048
TMP = 1024
TMF = 1024
SH_STEP = 2
TMX = 256
TMXF = 512
HALO = 16
CB = 512
CBQ = 1536


def _cp(*sem):
    return pltpu.CompilerParams(dimension_semantics=sem if sem else None, vmem_limit_bytes=VMEM_LIMIT)


def _dot(a, b):
    return jnp.dot(a, b, preferred_element_type=f32)


def _dot_nt(a, b):
    return lax.dot_general(a, b, (((1,), (1,)), ((), ())), preferred_element_type=f32)


def _dot_tn(a, b):
    return lax.dot_general(a, b, (((0,), (0,)), ((), ())), preferred_element_type=f32)


def _sigmoid(x):
    return 0.5 * jnp.tanh(0.5 * x) + 0.5


def _norm_fwd(x, g, shift, scale):
    rstd = lax.rsqrt(jnp.mean(x * x, axis=-1, keepdims=True) + EPS)
    xhat = x * rstd
    return xhat * g * (1.0 + scale) + shift, xhat, rstd


def _norm_bwd(dh, xhat, rstd, g, scale):
    dshift = jnp.sum(dh, axis=0, keepdims=True)
    dscale = jnp.sum(dh * xhat * g, axis=0, keepdims=True)
    dg = jnp.sum(dh * xhat * (1.0 + scale), axis=0, keepdims=True)
    dxh = dh * (g * (1.0 + scale))
    dx = rstd * (dxh - xhat * jnp.mean(dxh * xhat, axis=-1, keepdims=True))
    return dx, dshift, dscale, dg


def _allgather_small(xp):
    m_per, n = xp.shape

    def body(x_ref, out_ref, send_sems, recv_sems, local_sem):
        x, y, c = lax.axis_index("x"), lax.axis_index("y"), lax.axis_index("c")
        me, sibling = (x, y, c), (x, y, 1 - c)
        chips = [(1 - x, y), (x, 1 - y), (1 - x, 1 - y)]

        def rows(px, py, pc):
            return out_ref.at[pl.ds((4 * px + 2 * py + pc) * m_per, m_per), :]

        def copy(k, block, to, src=None):
            return pltpu.make_async_remote_copy(
                src_ref=rows(*block) if src is None else src, dst_ref=rows(*block),
                send_sem=send_sems.at[k], recv_sem=recv_sems.at[k], device_id=to, device_id_type=MESH)

        mine = pltpu.make_async_copy(x_ref, rows(*me), local_sem)
        mine.start()
        first = [copy(0, me, sibling, src=x_ref)]
        first += [copy(1 + j, me, (*chip, c), src=x_ref) for j, chip in enumerate(chips)]
        for cp in first:
            cp.start()
        passed = [copy(4 + j, (*chip, c), sibling) for j, chip in enumerate(chips)]
        for j, chip in enumerate(chips):
            copy(1 + j, (*chip, c), me).wait_recv()
            passed[j].start()
        copy(0, sibling, me).wait_recv()
        for j, chip in enumerate(chips):
            copy(4 + j, (*chip, 1 - c), me).wait_recv()
        for cp in first + passed:
            cp.wait_send()
        mine.wait()

    return pl.pallas_call(
        body, name="allgather_small",
        out_shape=SDS((8 * m_per, n), xp.dtype),
        in_specs=[pl.BlockSpec(memory_space=pltpu.VMEM)],
        out_specs=pl.BlockSpec(memory_space=pltpu.VMEM),
        scratch_shapes=[pltpu.SemaphoreType.DMA((7,)), pltpu.SemaphoreType.DMA((7,)), pltpu.SemaphoreType.DMA],
        compiler_params=pltpu.CompilerParams(vmem_limit_bytes=VMEM_LIMIT),
    )(xp)


WCLASSES = (
    ("wg0", "row", (FB, D)), ("wu0", "row", (FB, D)), ("wd0", "row", (FB, D)),
    ("wg1", "row", (FB, D)), ("wu1", "row", (FB, D)), ("wd1", "row", (FB, D)),
    ("win", "col", (D, WIN_SH)), ("wco", "row", (D // N_CHIPS, D)), ("wao", "col", (512, D // N_CHIPS)),
    ("wo", "row", (D // N_CHIPS, D)),
)
NCLS = len(WCLASSES)


def _full_shape(kind, shp):
    if kind == "lead":
        return (N_CHIPS,) + shp
    if kind == "row":
        return (N_CHIPS * shp[0], shp[1])
    return (shp[0], N_CHIPS * shp[1])


def _shard_view(ref, kind, shp, j):
    if kind == "lead":
        return ref.at[j]
    if kind == "row":
        return ref.at[pl.ds(j * shp[0], shp[0]), :]
    return ref.at[:, pl.ds(j * shp[1], shp[1])]


def _half(ref, shp, h):
    hr = shp[0] // 2
    return ref.at[pl.ds(pl.multiple_of(h * hr, 16), hr), :]


def _gather_weights(shards):
    n = NCLS

    def body(*refs):
        ins, outs = refs[:n], refs[n:2 * n]
        send1, recv1, send2, recv2, lsem = refs[2 * n:]
        x, y, c = lax.axis_index("x"), lax.axis_index("y"), lax.axis_index("c")
        chip = 2 * x + y
        sibling = (x, y, 1 - c)

        for mc in range(N_CHIPS):
            @pl.when(chip == mc)
            def _(mc=mc):
                local = []
                for q, (_, kind, shp) in enumerate(WCLASSES):
                    cp = pltpu.make_async_copy(ins[q], _shard_view(outs[q], kind, shp, mc), lsem.at[q])
                    cp.start()
                    local.append(cp)
                sends = []
                for k in (1, 2, 3):
                    pj = mc ^ k
                    for q, (_, kind, shp) in enumerate(WCLASSES):
                        cp = pltpu.make_async_remote_copy(
                            src_ref=_half(ins[q], shp, c), dst_ref=_half(_shard_view(outs[q], kind, shp, mc), shp, c),
                            send_sem=send1.at[q * 3 + k - 1], recv_sem=recv1.at[q * 3 + k - 1],
                            device_id=(pj >> 1, pj & 1, c), device_id_type=MESH)
                        cp.start()
                        sends.append(cp)
                for k in (1, 2, 3):
                    pj = mc ^ k
                    for q, (_, kind, shp) in enumerate(WCLASSES):
                        landed = _half(_shard_view(outs[q], kind, shp, pj), shp, c)
                        pltpu.make_async_remote_copy(
                            src_ref=landed, dst_ref=landed, send_sem=send1.at[q * 3 + k - 1], recv_sem=recv1.at[q * 3 + k - 1],
                            device_id=(pj >> 1, pj & 1, c), device_id_type=MESH).wait_recv()
                        cp = pltpu.make_async_remote_copy(
                            src_ref=landed, dst_ref=landed, send_sem=send2.at[q * 3 + k - 1], recv_sem=recv2.at[q * 3 + k - 1],
                            device_id=sibling, device_id_type=MESH)
                        cp.start()
                        sends.append(cp)
                for k in (1, 2, 3):
                    pj = mc ^ k
                    for q, (_, kind, shp) in enumerate(WCLASSES):
                        other = _half(_shard_view(outs[q], kind, shp, pj), shp, 1 - c)
                        pltpu.make_async_remote_copy(
                            src_ref=other, dst_ref=other, send_sem=send2.at[q * 3 + k - 1], recv_sem=recv2.at[q * 3 + k - 1],
                            device_id=sibling, device_id_type=MESH).wait_recv()
                for cp in sends:
                    cp.wait_send()
                for cp in local:
                    cp.wait()

    anyspec = pl.BlockSpec(memory_space=pl.ANY)
    return pl.pallas_call(
        body, name="gather_weights",
        out_shape=[SDS(_full_shape(kind, shp), bf16) for _, kind, shp in WCLASSES],
        in_specs=[anyspec] * n, out_specs=[anyspec] * n,
        scratch_shapes=[pltpu.SemaphoreType.DMA((3 * n,)), pltpu.SemaphoreType.DMA((3 * n,)),
                        pltpu.SemaphoreType.DMA((3 * n,)), pltpu.SemaphoreType.DMA((3 * n,)),
                        pltpu.SemaphoreType.DMA((n,))],
    )(*shards)


def _scatter_grads(pieces):
    n = NCLS

    def body(*refs):
        ins, outs = refs[:n], refs[n:2 * n]
        send1, recv1, lsem = refs[2 * n:]
        x, y, c = lax.axis_index("x"), lax.axis_index("y"), lax.axis_index("c")
        chip = 2 * x + y

        for mc in range(N_CHIPS):
            @pl.when(chip == mc)
            def _(mc=mc):
                local, sends = [], []
                for q, (_, kind, shp) in enumerate(WCLASSES):
                    cp = pltpu.make_async_copy(_shard_view(ins[q], kind, shp, mc), outs[q].at[0], lsem.at[q])
                    cp.start()
                    local.append(cp)
                for k in (1, 2, 3):
                    pj = mc ^ k
                    for q, (_, kind, shp) in enumerate(WCLASSES):
                        cp = pltpu.make_async_remote_copy(
                            src_ref=_shard_view(ins[q], kind, shp, pj), dst_ref=outs[q].at[k],
                            send_sem=send1.at[q * 3 + k - 1], recv_sem=recv1.at[q * 3 + k - 1],
                            device_id=(pj >> 1, pj & 1, c), device_id_type=MESH)
                        cp.start()
                        sends.append(cp)
                for cp in sends:
                    cp.wait_recv()
                for cp in sends:
                    cp.wait_send()
                for cp in local:
                    cp.wait()

    anyspec = pl.BlockSpec(memory_space=pl.ANY)
    return pl.pallas_call(
        body, name="scatter_grads",
        out_shape=[SDS((N_CHIPS,) + shp, bf16) for _, _, shp in WCLASSES],
        in_specs=[anyspec] * n, out_specs=[anyspec] * n,
        scratch_shapes=[pltpu.SemaphoreType.DMA((3 * n,)), pltpu.SemaphoreType.DMA((3 * n,)), pltpu.SemaphoreType.DMA((n,))],
    )(*pieces)


def _swap_sibling(ts):
    n = len(ts)

    def body(*refs):
        ins, outs = refs[:n], refs[n:2 * n]
        send, recv = refs[2 * n:]
        x, y, c = lax.axis_index("x"), lax.axis_index("y"), lax.axis_index("c")
        cps = []
        for q in range(n):
            cp = pltpu.make_async_remote_copy(src_ref=ins[q], dst_ref=outs[q], send_sem=send.at[q], recv_sem=recv.at[q],
                                              device_id=(x, y, 1 - c), device_id_type=MESH)
            cp.start()
            cps.append(cp)
        for cp in cps:
            cp.wait_recv()
        for cp in cps:
            cp.wait_send()

    anyspec = pl.BlockSpec(memory_space=pl.ANY)
    return pl.pallas_call(
        body, name="swap_sibling",
        out_shape=[SDS(t.shape, t.dtype) for t in ts],
        in_specs=[anyspec] * n, out_specs=[anyspec] * n,
        scratch_shapes=[pltpu.SemaphoreType.DMA((n,)), pltpu.SemaphoreType.DMA((n,))],
    )(*ts)


HBM_SPEC = pl.BlockSpec(memory_space=pltpu.HBM)
SEM_SPEC = pl.BlockSpec(memory_space=pltpu.SEMAPHORE)
ANY_SPEC = pl.BlockSpec(memory_space=pl.ANY)
EFFECT = pltpu.SideEffectType.DATAFLOW_SIDE_EFFECTING
N_COPIES = 3 * NCLS


def _in_hbm(a):
    return pltpu.with_memory_space_constraint(a, pltpu.HBM)


def _chip_index():
    return 2 * lax.axis_index("x") + lax.axis_index("y")


def _place_own(shards):
    n = NCLS

    def body(*refs):
        ins, outs, lsem = refs[:n], refs[n:2 * n], refs[2 * n]
        chip = _chip_index()
        for mc in range(N_CHIPS):
            @pl.when(chip == mc)
            def _(mc=mc):
                cps = [pltpu.make_async_copy(ins[q], _shard_view(outs[q], kind, shp, mc), lsem.at[q])
                       for q, (_, kind, shp) in enumerate(WCLASSES)]
                for cp in cps:
                    cp.start()
                for cp in cps:
                    cp.wait()

    return pl.pallas_call(
        body, name="place_own",
        out_shape=[SDS(_full_shape(kind, shp), bf16) for _, kind, shp in WCLASSES],
        in_specs=[ANY_SPEC] * n, out_specs=[ANY_SPEC] * n,
        scratch_shapes=[pltpu.SemaphoreType.DMA((n,))],
    )(*shards)


def _take_own(pieces):
    n = NCLS

    def body(*refs):
        ins, outs, lsem = refs[:n], refs[n:2 * n], refs[2 * n]
        chip = _chip_index()
        for mc in range(N_CHIPS):
            @pl.when(chip == mc)
            def _(mc=mc):
                cps = [pltpu.make_async_copy(_shard_view(ins[q], kind, shp, mc), outs[q].at[0], lsem.at[q])
                       for q, (_, kind, shp) in enumerate(WCLASSES)]
                for cp in cps:
                    cp.start()
                for cp in cps:
                    cp.wait()

    return pl.pallas_call(
        body, name="take_own",
        out_shape=[SDS((N_CHIPS,) + shp, bf16) for _, _, shp in WCLASSES],
        in_specs=[ANY_SPEC] * n, out_specs=[ANY_SPEC] * n,
        scratch_shapes=[pltpu.SemaphoreType.DMA((n,))],
    )(*pieces)


def _split_start(name, srcs, dsts, after, src_view, dst_view):
    n = NCLS

    def body(*refs):
        src, dst = refs[:n], refs[n:2 * n]
        send, recv = refs[2 * n + 1], refs[2 * n + 2]
        token = refs[-1]
        c = lax.axis_index("c")
        chip = _chip_index()
        for mc in range(N_CHIPS):
            @pl.when(chip == mc)
            def _(mc=mc):
                for k in (1, 2, 3):
                    pj = mc ^ k
                    for q in range(n):
                        pltpu.make_async_remote_copy(
                            src_ref=src_view(src[q], q, mc, pj), dst_ref=dst_view(dst[q], q, mc, k),
                            send_sem=send.at[q * 3 + k - 1], recv_sem=recv.at[q * 3 + k - 1],
                            device_id=(pj >> 1, pj & 1, c), device_id_type=MESH).start()
        token[...] = jnp.zeros_like(token)

    return pl.pallas_call(
        body, name=name,
        out_shape=(pltpu.SemaphoreType.DMA((N_COPIES,)), pltpu.SemaphoreType.DMA((N_COPIES,)),
                   *[pltpu.HBM(a.shape, a.dtype) for a in srcs], *[pltpu.HBM(a.shape, a.dtype) for a in dsts], SDS((8, 128), f32)),
        in_specs=[HBM_SPEC] * (2 * n) + [ANY_SPEC],
        out_specs=(SEM_SPEC, SEM_SPEC, *([HBM_SPEC] * (2 * n)), pl.BlockSpec(memory_space=pltpu.VMEM)),
        input_output_aliases={i: 2 + i for i in range(2 * n)},
        compiler_params=pltpu.CompilerParams(has_side_effects=EFFECT),
    )(*[_in_hbm(a) for a in srcs], *[_in_hbm(a) for a in dsts], after)


def _split_wait(name, started, after, arrival_view):
    n = NCLS
    send, recv = started[0], started[1]
    srcs, dsts = started[2:2 + n], started[2 + n:2 + 2 * n]

    def body(*refs):
        src, dst = refs[:n], refs[n:2 * n]
        send_sem, recv_sem = refs[2 * n], refs[2 * n + 1]
        x, y, c = lax.axis_index("x"), lax.axis_index("y"), lax.axis_index("c")
        for k in (1, 2, 3):
            for q in range(n):
                arrival = arrival_view(dst[q], q, k)
                cp = pltpu.make_async_remote_copy(
                    src_ref=arrival, dst_ref=arrival, send_sem=send_sem.at[q * 3 + k - 1], recv_sem=recv_sem.at[q * 3 + k - 1],
                    device_id=(x, y, 1 - c), device_id_type=MESH)
                cp.wait_send()
                cp.wait_recv()

    out = pl.pallas_call(
        body, name=name,
        out_shape=(*[pltpu.HBM(a.shape, a.dtype) for a in srcs], *[pltpu.HBM(a.shape, a.dtype) for a in dsts]),
        in_specs=[HBM_SPEC] * (2 * n) + [SEM_SPEC, SEM_SPEC, ANY_SPEC],
        out_specs=tuple([HBM_SPEC] * (2 * n)),
        input_output_aliases={i: i for i in range(2 * n)},
        compiler_params=pltpu.CompilerParams(has_side_effects=EFFECT),
    )(*srcs, *dsts, send, recv, after)
    return out[n:]


def _cls(q):
    return WCLASSES[q][1], WCLASSES[q][2]


def _gather_start(shards, after):
    fulls = _place_own(shards)
    return _split_start("gather_start", shards, fulls, after,
                        lambda ref, q, mc, pj: ref,
                        lambda ref, q, mc, k: _shard_view(ref, *_cls(q), mc))


def _gather_wait(started, after):
    return _split_wait("gather_wait", started, after, lambda ref, q, k: _shard_view(ref, *_cls(q), 0))


def _scatter_start(pieces, after):
    lands = _take_own(pieces)
    return _split_start("scatter_start", pieces, lands, after,
                        lambda ref, q, mc, pj: _shard_view(ref, *_cls(q), pj),
                        lambda ref, q, mc, k: ref.at[k])


def _scatter_wait(started, after):
    return _split_wait("scatter_wait", started, after, lambda ref, q, k: ref.at[k])


GROUPS = {"A": (0, 1, 2), "B": (6,), "C": (7, 8, 9), "D": (3, 4, 5)}


def _own_spec(kind, shp, tr):
    R, C = shp
    if kind == "lead":
        return pl.BlockSpec((None, tr, C), lambda i, chip: (chip[0], i, 0))
    if kind == "row":
        return pl.BlockSpec((tr, C), lambda i, chip: (chip[0] * (R // tr) + i, 0))
    return pl.BlockSpec((tr, C), lambda i, chip: (i, chip[0]))


def _cast_place(shards, kind, shp, chip_arr, after):
    n = len(shards)
    R, C = shp
    tr = _row_tile(R, C)

    def body(chip_ref, *refs):
        for q in range(n):
            refs[n + 1 + q][...] = refs[q][...].astype(bf16)

    def in_spec(lead):
        return pl.BlockSpec((None,) * len(lead) + (tr, C), lambda i, chip: (*lead, i, 0))

    return pl.pallas_call(
        body, name="cast_place",
        grid_spec=pltpu.PrefetchScalarGridSpec(
            num_scalar_prefetch=1, grid=(R // tr,),
            in_specs=[in_spec(lead) for _, lead in shards] + [ANY_SPEC],
            out_specs=[_own_spec(kind, shp, tr)] * n),
        out_shape=[SDS(_full_shape(kind, shp), bf16)] * n,
        compiler_params=_cp("arbitrary"),
    )(chip_arr, *[a for a, _ in shards], after)


def _sum_own_slots(piece, land, kind, shp, chip_arr):
    R, C = shp
    tr = _row_tile(R, C)

    def body(chip_ref, p_ref, l_ref, t_ref):
        t = p_ref[...].astype(f32)
        for k in range(N_CHIPS - 1):
            t = t + l_ref[k].astype(f32)
        t_ref[...] = t.astype(bf16)

    return pl.pallas_call(
        body, name="sum_own_slots",
        grid_spec=pltpu.PrefetchScalarGridSpec(
            num_scalar_prefetch=1, grid=(R // tr,),
            in_specs=[_own_spec(kind, shp, tr), pl.BlockSpec((N_CHIPS - 1, tr, C), lambda i, chip: (0, i, 0))],
            out_specs=pl.BlockSpec((tr, C), lambda i, chip: (i, 0))),
        out_shape=SDS((R, C), bf16),
        compiler_params=_cp("arbitrary"),
    )(chip_arr, piece, land)


def _xfer_start(name, arrays, ng, after, src_view, dst_view):
    na = len(arrays)

    def body(*refs):
        arr = refs[:na]
        send, recv, token = refs[na + 1], refs[na + 2], refs[-1]
        c = lax.axis_index("c")
        chip = _chip_index()
        for mc in range(N_CHIPS):
            @pl.when(chip == mc)
            def _(mc=mc):
                for k in (1, 2, 3):
                    pj = mc ^ k
                    for i in range(ng):
                        pltpu.make_async_remote_copy(
                            src_ref=src_view(arr, i, mc, pj), dst_ref=dst_view(arr, i, mc, k),
                            send_sem=send.at[i * 3 + k - 1], recv_sem=recv.at[i * 3 + k - 1],
                            device_id=(pj >> 1, pj & 1, c), device_id_type=MESH).start()
        token[...] = jnp.zeros_like(token)

    return pl.pallas_call(
        body, name=name,
        out_shape=(pltpu.SemaphoreType.DMA((3 * ng,)), pltpu.SemaphoreType.DMA((3 * ng,)),
                   *[pltpu.HBM(a.shape, a.dtype) for a in arrays], SDS((8, 128), f32)),
        in_specs=[HBM_SPEC] * na + [ANY_SPEC],
        out_specs=(SEM_SPEC, SEM_SPEC, *([HBM_SPEC] * na), pl.BlockSpec(memory_space=pltpu.VMEM)),
        input_output_aliases={i: 2 + i for i in range(na)},
        compiler_params=pltpu.CompilerParams(has_side_effects=EFFECT),
    )(*[_in_hbm(a) for a in arrays], after)


def _xfer_wait(name, started, ng, after, arrival_view):
    send, recv = started[0], started[1]
    arrays = started[2:-1]
    na = len(arrays)

    def body(*refs):
        arr = refs[:na]
        send_sem, recv_sem = refs[na], refs[na + 1]
        x, y, c = lax.axis_index("x"), lax.axis_index("y"), lax.axis_index("c")
        for k in (1, 2, 3):
            for i in range(ng):
                arrival = arrival_view(arr, i)
                cp = pltpu.make_async_remote_copy(
                    src_ref=arrival, dst_ref=arrival, send_sem=send_sem.at[i * 3 + k - 1], recv_sem=recv_sem.at[i * 3 + k - 1],
                    device_id=(x, y, 1 - c), device_id_type=MESH)
                cp.wait_send()
                cp.wait_recv()

    return pl.pallas_call(
        body, name=name,
        out_shape=tuple(pltpu.HBM(a.shape, a.dtype) for a in arrays),
        in_specs=[HBM_SPEC] * na + [SEM_SPEC, SEM_SPEC, ANY_SPEC],
        out_specs=tuple([HBM_SPEC] * na),
        input_output_aliases={i: i for i in range(na)},
        compiler_params=pltpu.CompilerParams(has_side_effects=EFFECT),
    )(*arrays, send, recv, after)


def _gather_group_cast(group, shards_f32, chip_arr, after):
    fulls = [None] * len(group)
    by_shape = {}
    for i, q in enumerate(group):
        by_shape.setdefault(_cls(q), []).append(i)
    for (kind, shp), idx in by_shape.items():
        for i, f in zip(idx, _cast_place([shards_f32[i] for i in idx], kind, shp, chip_arr, after)):
            fulls[i] = f
    return fulls


def _gather_group_start(tag, group, fulls, after):
    def view(arr, i, mc, _):
        kind, shp = _cls(group[i])
        return _half(_shard_view(arr[i], kind, shp, mc), shp, lax.axis_index("c"))
    return _xfer_start("gather_start_" + tag, fulls, len(group), after, view, view)


def _gather_group_forward(tag, group, started, after):
    ng = len(group)
    send1, recv1 = started[0], started[1]
    arrays = started[2:-1]
    na = len(arrays)

    def body(*refs):
        arr = refs[:na]
        send_in, recv_in = refs[na], refs[na + 1]
        send2, recv2, token = refs[na + 3], refs[na + 4], refs[-1]
        x, y, c = lax.axis_index("x"), lax.axis_index("y"), lax.axis_index("c")
        chip = _chip_index()
        for mc in range(N_CHIPS):
            @pl.when(chip == mc)
            def _(mc=mc):
                for k in (1, 2, 3):
                    pj = mc ^ k
                    for i in range(ng):
                        kind, shp = _cls(group[i])
                        landed = _half(_shard_view(arr[i], kind, shp, pj), shp, c)
                        pltpu.make_async_remote_copy(
                            src_ref=landed, dst_ref=landed, send_sem=send_in.at[i * 3 + k - 1], recv_sem=recv_in.at[i * 3 + k - 1],
                            device_id=(pj >> 1, pj & 1, c), device_id_type=MESH).wait_recv()
                        pltpu.make_async_remote_copy(
                            src_ref=landed, dst_ref=landed, send_sem=send2.at[i * 3 + k - 1], recv_sem=recv2.at[i * 3 + k - 1],
                            device_id=(x, y, 1 - c), device_id_type=MESH).start()
        token[...] = jnp.zeros_like(token)

    return pl.pallas_call(
        body, name="gather_forward_" + tag,
        out_shape=(pltpu.SemaphoreType.DMA((3 * ng,)), pltpu.SemaphoreType.DMA((3 * ng,)),
                   *[pltpu.HBM(a.shape, a.dtype) for a in arrays], SDS((8, 128), f32)),
        in_specs=[HBM_SPEC] * na + [SEM_SPEC, SEM_SPEC, ANY_SPEC],
        out_specs=(SEM_SPEC, SEM_SPEC, *([HBM_SPEC] * na), pl.BlockSpec(memory_space=pltpu.VMEM)),
        input_output_aliases={i: 2 + i for i in range(na)},
        compiler_params=pltpu.CompilerParams(has_side_effects=EFFECT),
    )(*arrays, send1, recv1, after)


def _gather_group_wait(tag, group, send1, forwarded, after):
    ng = len(group)
    send2, recv2 = forwarded[0], forwarded[1]
    arrays = forwarded[2:-1]
    na = len(arrays)

    def body(*refs):
        arr = refs[:na]
        s1, s2, r2 = refs[na], refs[na + 1], refs[na + 2]
        x, y, c = lax.axis_index("x"), lax.axis_index("y"), lax.axis_index("c")
        for k in (1, 2, 3):
            for i in range(ng):
                kind, shp = _cls(group[i])
                half = _half(_shard_view(arr[i], kind, shp, 0), shp, 0)
                pltpu.make_async_remote_copy(src_ref=half, dst_ref=half, send_sem=s1.at[i * 3 + k - 1], recv_sem=r2.at[i * 3 + k - 1],
                                             device_id=(x, y, 1 - c), device_id_type=MESH).wait_send()
                cp = pltpu.make_async_remote_copy(src_ref=half, dst_ref=half, send_sem=s2.at[i * 3 + k - 1], recv_sem=r2.at[i * 3 + k - 1],
                                                  device_id=(x, y, 1 - c), device_id_type=MESH)
                cp.wait_send()
                cp.wait_recv()

    return pl.pallas_call(
        body, name="gather_wait_" + tag,
        out_shape=tuple(pltpu.HBM(a.shape, a.dtype) for a in arrays),
        in_specs=[HBM_SPEC] * na + [SEM_SPEC, SEM_SPEC, SEM_SPEC, ANY_SPEC],
        out_specs=tuple([HBM_SPEC] * na),
        input_output_aliases={i: i for i in range(na)},
        compiler_params=pltpu.CompilerParams(has_side_effects=EFFECT),
    )(*arrays, send1, send2, recv2, after)


def _scatter_group_start(tag, group, pieces, after):
    ng = len(group)
    lands = [lax.empty((N_CHIPS - 1,) + _cls(q)[1], bf16) for q in group]
    return _xfer_start("scatter_start_" + tag, list(pieces) + lands, ng, after,
                       lambda arr, i, mc, pj: _shard_view(arr[i], *_cls(group[i]), pj),
                       lambda arr, i, mc, k: arr[ng + i].at[k - 1])


def _scatter_group_wait(tag, group, started, after):
    ng = len(group)
    out = _xfer_wait("scatter_wait_" + tag, started, ng, after, lambda arr, i: arr[ng + i].at[0])
    return out[:ng], out[ng:]


def _mod_shards(c_all, ada_w, ada_b_sh):
    tn = ADA_SH // 3

    def body(c_ref, w_ref, b_ref, o_ref, cs_ref):
        cv = c_ref[...]
        cs = cv * _sigmoid(cv)
        cs_ref[...] = cs
        o_ref[...] = _dot(cs.astype(bf16), w_ref[...].astype(bf16)) + b_ref[...]

    return pl.pallas_call(
        body, name="mod_shards", grid=(DEPTH, 3),
        out_shape=[SDS((DEPTH, 8, ADA_SH), f32), SDS((8, D), f32)],
        in_specs=[pl.BlockSpec((8, D), lambda l, t: (0, 0)),
                  pl.BlockSpec((None, D, tn), lambda l, t: (l, 0, t)),
                  pl.BlockSpec((None, 1, tn), lambda l, t: (l, 0, t))],
        out_specs=[pl.BlockSpec((None, 8, tn), lambda l, t: (l, 0, t)), pl.BlockSpec((8, D), lambda l, t: (0, 0))],
        compiler_params=_cp("arbitrary", "arbitrary"),
    )(c_all, ada_w, ada_b_sh.reshape(DEPTH, 1, ADA_SH))


def _t5_bucket(dist):
    exact = NUM_BUCKETS // 2
    dd = np.maximum(dist, 1).astype(np.float32)
    large = exact + (np.log(dd / exact) / np.log(MAX_DISTANCE / exact) * (NUM_BUCKETS - exact)).astype(np.int32)
    large = np.minimum(large, NUM_BUCKETS - 1)
    return np.where(dist < exact, dist, large).astype(np.int32)


def _bucket_table():
    i = np.arange(BLK)[:, None]
    j = np.arange(2 * BLK)[None, :]
    rel = i - j + BLK
    return np.stack([_t5_bucket(np.maximum(rel, 0) * d) for d in DILATIONS]).astype(np.int32)


def _band():
    rel = lax.broadcasted_iota(jnp.int32, (BLK, 2 * BLK), 0) - lax.broadcasted_iota(jnp.int32, (BLK, 2 * BLK), 1) + BLK
    return (rel >= 0) & (rel <= BLK)


def _bias_blocks(rel_bias, buckets):
    def body(tab_ref, bk_ref, o_ref):
        h = pl.program_id(0)
        bk = bk_ref[...]
        acc = jnp.zeros((BLK, 2 * BLK), f32)
        for b in range(NUM_BUCKETS):
            acc = jnp.where(bk == b, tab_ref[b, h], acc)
        o_ref[...] = jnp.where(_band(), acc, NEG)

    return pl.pallas_call(
        body, name="bias_blocks", grid=(24,),
        out_shape=SDS((24, BLK, 2 * BLK), f32),
        in_specs=[pl.BlockSpec(memory_space=pltpu.SMEM), pl.BlockSpec((None, BLK, 2 * BLK), lambda h: (h // 8, 0, 0))],
        out_specs=pl.BlockSpec((None, BLK, 2 * BLK), lambda h: (h, 0, 0)),
        compiler_params=_cp("arbitrary"),
    )(rel_bias, buckets)


def _bias_grad(dsaccs, buckets):
    nl = len(dsaccs)

    def body(*refs):
        bk = refs[nl][...]
        tot = refs[0][...]
        for r in refs[1:nl]:
            tot = tot + r[...]
        lane = lax.broadcasted_iota(jnp.int32, (1, 128), 1)
        row = jnp.zeros((1, 128), f32)
        for b in range(NUM_BUCKETS):
            row = jnp.where(lane == b, jnp.sum(jnp.where(bk == b, tot, 0.0)), row)
        refs[nl + 1][...] = row

    return pl.pallas_call(
        body, name="bias_grad", grid=(24,),
        out_shape=SDS((24, 1, 128), f32),
        in_specs=[pl.BlockSpec((None, BLK, 2 * BLK), lambda h: (h, 0, 0))] * nl
                 + [pl.BlockSpec((None, BLK, 2 * BLK), lambda h: (h // 8, 0, 0))],
        out_specs=pl.BlockSpec((None, 1, 128), lambda h: (h, 0, 0)),
        compiler_params=_cp("arbitrary"),
    )(*dsaccs, buckets)


def _ffn_fwd(x, mod9, g3, wg, wu, wd, sub):
    S = x.shape[0]

    def body(x_ref, mod_ref, g_ref, wg_ref, wu_ref, wd_ref, xo_ref, h_ref, ga_ref, sa_ref, hid_ref, y_ref, acc):
        j = pl.program_id(1)

        @pl.when(j == 0)
        def _():
            h, _, _ = _norm_fwd(x_ref[...], g_ref[sub:sub + 1, :], mod_ref[3 * sub:3 * sub + 1, :], mod_ref[3 * sub + 1:3 * sub + 2, :])
            h_ref[...] = h.astype(bf16)
            acc[...] = jnp.zeros_like(acc)

        h = h_ref[...]
        a = _dot_nt(h, wg_ref[...])
        u = _dot_nt(h, wu_ref[...])
        sg = _sigmoid(a)
        sil = a * sg
        ga_ref[...] = (u * (sg * (1.0 + a * (1.0 - sg)))).astype(bf16)
        sa_ref[...] = sil.astype(bf16)
        hid_ref[...] = (sil * u).astype(bf16)
        acc[...] += _dot(hid_ref[...], wd_ref[...])

        @pl.when(j == N_CHIPS - 1)
        def _():
            y = acc[...]
            y_ref[...] = y.astype(bf16)
            xo_ref[...] = x_ref[...] + 0.5 * mod_ref[3 * sub + 2:3 * sub + 3, :] * y

    row = pl.BlockSpec((TMF, D), lambda i, j: (i, 0))
    hidb = pl.BlockSpec((None, TMF, FB), lambda i, j: (j, i, 0))
    hids = SDS((N_CHIPS, S, FB), bf16)
    return pl.pallas_call(
        body, name="ffn_fwd", grid=(S // TMF, N_CHIPS),
        out_shape=[SDS((S, D), f32), SDS((S, D), bf16), hids, hids, hids, SDS((S, D), bf16)],
        in_specs=[row, pl.BlockSpec((9, D), lambda i, j: (0, 0)), pl.BlockSpec((3, D), lambda i, j: (0, 0)),
                  pl.BlockSpec((FB, D), lambda i, j: (j, 0)), pl.BlockSpec((FB, D), lambda i, j: (j, 0)),
                  pl.BlockSpec((FB, D), lambda i, j: (j, 0))],
        out_specs=[row, row, hidb, hidb, hidb, row],
        scratch_shapes=[pltpu.VMEM((TMF, D), f32)],
        compiler_params=_cp("arbitrary", "arbitrary"),
    )(x, mod9, g3, wg, wu, wd)


def _ffn_bwd1(dxo, x, mod9, g3, y, ga, sa, wg, wu, wd, sub):
    S = x.shape[0]

    def body(dxo_ref, x_ref, mod_ref, g_ref, y_ref, ga_ref, sa_ref, wg_ref, wu_ref, wd_ref,
             dxi_ref, da_ref, du_ref, dy_ref, sm_ref, acc):
        i, j = pl.program_id(0), pl.program_id(1)
        gate = mod_ref[3 * sub + 2:3 * sub + 3, :]

        @pl.when((i == 0) & (j == 0))
        def _():
            sm_ref[...] = jnp.zeros_like(sm_ref)

        @pl.when(j == 0)
        def _():
            dxo_v = dxo_ref[...]
            dy_ref[...] = (0.5 * gate * dxo_v).astype(bf16)
            sm_ref[2:3, :] += jnp.sum(0.5 * y_ref[...].astype(f32) * dxo_v, axis=0, keepdims=True)
            acc[...] = jnp.zeros_like(acc)

        part = None
        for s in range(SH_STEP):
            dhid = _dot_nt(dy_ref[...], wd_ref[s * FB:(s + 1) * FB, :])
            da = (dhid * ga_ref[s].astype(f32)).astype(bf16)
            du = (dhid * sa_ref[s].astype(f32)).astype(bf16)
            da_ref[s] = da
            du_ref[s] = du
            t = _dot(da, wg_ref[s * FB:(s + 1) * FB, :]) + _dot(du, wu_ref[s * FB:(s + 1) * FB, :])
            part = t if part is None else part + t
        acc[...] += part

        @pl.when(j == N_CHIPS // SH_STEP - 1)
        def _():
            g = g_ref[sub:sub + 1, :]
            scale = mod_ref[3 * sub + 1:3 * sub + 2, :]
            _, xhat, rstd = _norm_fwd(x_ref[...], g, mod_ref[3 * sub:3 * sub + 1, :], scale)
            dx, dshift, dscale, dg = _norm_bwd(acc[...], xhat, rstd, g, scale)
            dxi_ref[...] = dxo_ref[...] + dx
            sm_ref[0:1, :] += dshift
            sm_ref[1:2, :] += dscale
            sm_ref[3:4, :] += dg

    row = pl.BlockSpec((TM, D), lambda i, j: (i, 0))
    hidb = pl.BlockSpec((SH_STEP, TM, FB), lambda i, j: (j, i, 0))
    wcol = pl.BlockSpec((SH_STEP * FB, D), lambda i, j: (j, 0))
    return pl.pallas_call(
        body, name="ffn_bwd1", grid=(S // TM, N_CHIPS // SH_STEP),
        out_shape=[SDS((S, D), f32), SDS((N_CHIPS, S, FB), bf16), SDS((N_CHIPS, S, FB), bf16), SDS((S, D), bf16), SDS((8, D), f32)],
        in_specs=[row, row, pl.BlockSpec((9, D), lambda i, j: (0, 0)), pl.BlockSpec((3, D), lambda i, j: (0, 0)), row,
                  hidb, hidb, wcol, wcol, pl.BlockSpec((SH_STEP * FB, D), lambda i, j: (j, 0))],
        out_specs=[row, hidb, hidb, row, pl.BlockSpec((8, D), lambda i, j: (0, 0))],
        scratch_shapes=[pltpu.VMEM((TM, D), f32)],
        compiler_params=_cp("arbitrary", "arbitrary"),
    )(dxo, x, mod9, g3, y, ga, sa, wg, wu, wd)


def _ffn_bwd2(h, da, du, hid, dy):
    S = h.shape[0]
    ni = S // TMW

    def body(h_ref, da_ref, du_ref, hid_ref, dy_ref, dwg_ref, dwu_ref, dwd_ref, ag, au, ad):
        i = pl.program_id(1)

        @pl.when(i == 0)
        def _():
            ag[...] = jnp.zeros_like(ag)
            au[...] = jnp.zeros_like(au)
            ad[...] = jnp.zeros_like(ad)

        hv = h_ref[...]
        ag[...] += _dot_tn(da_ref[...], hv)
        au[...] += _dot_tn(du_ref[...], hv)
        ad[...] += _dot_tn(hid_ref[...], dy_ref[...])

        @pl.when(i == ni - 1)
        def _():
            dwg_ref[...] = ag[...].astype(bf16)
            dwu_ref[...] = au[...].astype(bf16)
            dwd_ref[...] = ad[...].astype(bf16)

    row = pl.BlockSpec((TMW, D), lambda j, i: (i, 0))
    hidb = pl.BlockSpec((None, TMW, FB), lambda j, i: (j, i, 0))
    wrow = pl.BlockSpec((FB, D), lambda j, i: (j, 0))
    return pl.pallas_call(
        body, name="ffn_bwd2", grid=(N_CHIPS, ni),
        out_shape=[SDS((N_CHIPS * FB, D), bf16)] * 3,
        in_specs=[row, hidb, hidb, hidb, row],
        out_specs=[wrow, wrow, wrow],
        scratch_shapes=[pltpu.VMEM((FB, D), f32)] * 3,
        compiler_params=_cp("arbitrary", "arbitrary"),
    )(h, da, du, hid, dy)


def _mix_qkv(x, mod9, g3, win):
    S = x.shape[0]

    def body(x_ref, mod_ref, g_ref, w_ref, h_ref, o_ref):
        @pl.when(pl.program_id(1) == 0)
        def _():
            h, _, _ = _norm_fwd(x_ref[...], g_ref[1:2, :], mod_ref[3:4, :], mod_ref[4:5, :])
            h_ref[...] = h.astype(bf16)

        o_ref[...] = _dot(h_ref[...], w_ref[...]).astype(bf16)

    row = pl.BlockSpec((TMP, D), lambda i, j: (i, 0))
    return pl.pallas_call(
        body, name="mix_qkv", grid=(S // TMP, QKV_W // CBQ),
        out_shape=[SDS((S, D), bf16), SDS((S, QKV_W), bf16)],
        in_specs=[row, pl.BlockSpec((9, D), lambda i, j: (0, 0)), pl.BlockSpec((3, D), lambda i, j: (0, 0)),
                  pl.BlockSpec((D, CBQ), lambda i, j: (0, j))],
        out_specs=[row, pl.BlockSpec((TMP, CBQ), lambda i, j: (i, j))],
        compiler_params=_cp("arbitrary", "arbitrary"),
    )(x, mod9, g3, win)


def _mix_rest(h, win):
    S = h.shape[0]
    off = QKV_W // CB

    def body(h_ref, w_ref, o_ref):
        o_ref[...] = _dot(h_ref[...], w_ref[...]).astype(bf16)

    return pl.pallas_call(
        body, name="mix_rest", grid=(S // TMP, REST_W // CB),
        out_shape=SDS((S, REST_W), bf16),
        in_specs=[pl.BlockSpec((TMP, D), lambda i, j: (i, 0)), pl.BlockSpec((D, CB), lambda i, j: (0, off + j))],
        out_specs=pl.BlockSpec((TMP, CB), lambda i, j: (i, j)),
        compiler_params=_cp("arbitrary", "arbitrary"),
    )(h, win)


def _widen(srcs, dsts):
    for src, dst in zip(srcs, dsts):
        dst[...] = src[...].astype(f32)


def _qkv_scratch(R, Rb):
    return [pltpu.VMEM((R, 128), f32), pltpu.VMEM((R, 128), f32), pltpu.VMEM((Rb, 128), f32),
            pltpu.VMEM((R, 128), f32), pltpu.VMEM((Rb, 128), f32)]


def _attn_fwd(qkv, bias, g):
    S = qkv.shape[0]
    d = DILATIONS[g]
    nq = Q_BLOCKS[g]
    Rb = BLK * d
    R = Rb * nq
    nb = S // R
    qb, kb, vb = 4 * g, 12 + 4 * g, 24 + 4 * g

    def body(q_in, kc_in, kp_in, vc_in, vp_in, b_ref, o_ref, l_ref, q_ref, kc_ref, kp_ref, vc_ref, vp_ref):
        n = pl.program_id(1)
        col = lax.broadcasted_iota(jnp.int32, (BLK, 2 * BLK), 1)
        first = jnp.where((col < BLK) & (n == 0), NEG, 0.0)
        head0 = lax.broadcasted_iota(jnp.int32, (1, 2 * HD), 1) < HD
        _widen((q_in, kc_in, kp_in, vc_in, vp_in), (q_ref, kc_ref, kp_ref, vc_ref, vp_ref))

        def one(b, r):
            sl = pl.ds(b * Rb + r, BLK, stride=d)
            q = q_ref[sl, :]
            if b == 0:
                kp, vp = kp_ref[pl.ds(r, BLK, stride=d), :], vp_ref[pl.ds(r, BLK, stride=d), :]
            else:
                before = pl.ds((b - 1) * Rb + r, BLK, stride=d)
                kp, vp = kc_ref[before, :], vc_ref[before, :]
            kk = jnp.concatenate([kp, kc_ref[sl, :]], axis=0).astype(bf16)
            vv = jnp.concatenate([vp, vc_ref[sl, :]], axis=0).astype(bf16)
            os, ls = [], []
            for hh in range(2):
                qm = jnp.where(head0 if hh == 0 else ~head0, q, 0.0).astype(bf16)
                s = _dot_nt(qm, kk) * SCALE + b_ref[hh]
                if b == 0:
                    s = s + first
                m = jnp.max(s, axis=-1, keepdims=True)
                p = jnp.exp(s - m)
                l = jnp.sum(p, axis=-1, keepdims=True)
                os.append(_dot(p.astype(bf16), vv) / l)
                ls.append(m + jnp.log(l))
            o_ref[sl, :] = jnp.where(head0, os[0], os[1])
            l_ref[sl, :] = jnp.where(head0, ls[0], ls[1])

        for b in range(nq):
            if d == 1:
                one(b, 0)
            else:
                lax.fori_loop(0, d, lambda r, carry, b=b: (one(b, r), carry)[1], 0, unroll=4)

    def blk(cb, prev):
        if prev:
            return pl.BlockSpec((Rb, 128), lambda hp, n: (jnp.maximum(n * nq - 1, 0), cb + hp))
        return pl.BlockSpec((R, 128), lambda hp, n: (n, cb + hp))

    outb = pl.BlockSpec((R, 128), lambda hp, n: (n, hp))
    return pl.pallas_call(
        body, name=f"attn_fwd_d{d}", grid=(4, nb),
        out_shape=[SDS((S, 512), f32), SDS((S, 512), f32)],
        in_specs=[blk(qb, False), blk(kb, False), blk(kb, True), blk(vb, False), blk(vb, True),
                  pl.BlockSpec((2, BLK, 2 * BLK), lambda hp, n: (4 * g + hp, 0, 0))],
        out_specs=[outb, outb],
        scratch_shapes=_qkv_scratch(R, Rb),
        compiler_params=_cp("arbitrary", "arbitrary"),
    )(qkv, qkv, qkv, qkv, qkv, bias)


def _attn_bwd(qkv, do, o, lse, bias, dq_all, dk_all, dv_all, g):
    S = qkv.shape[0]
    d = DILATIONS[g]
    nq = Q_BLOCKS[g]
    Rb = BLK * d
    R = Rb * nq
    nb = S // R
    qb, kb, vb = 4 * g, 12 + 4 * g, 24 + 4 * g

    def body(q_in, kc_in, kp_in, vc_in, vp_in, do_ref, o_ref, l_ref, b_ref, dqi, dki, dvi,
             dq_out, dk_out, dv_out, ds_ref, ck, cv, tk, tv, dq_ref, q_ref, kc_ref, kp_ref, vc_ref, vp_ref):
        n = pl.program_id(1)
        col = lax.broadcasted_iota(jnp.int32, (BLK, 2 * BLK), 1)
        first = jnp.where((col < BLK) & (n == 0), NEG, 0.0)

        @pl.when(n == 0)
        def _():
            ck[...] = jnp.zeros_like(ck)
            cv[...] = jnp.zeros_like(cv)
            ds_ref[...] = jnp.zeros_like(ds_ref)

        @pl.when(n < nb)
        def _():
            head0 = lax.broadcasted_iota(jnp.int32, (1, 2 * HD), 1) < HD
            _widen((q_in, kc_in, kp_in, vc_in, vp_in), (q_ref, kc_ref, kp_ref, vc_ref, vp_ref))

            def one(b, r):
                sl = pl.ds(b * Rb + r, BLK, stride=d)
                before = pl.ds((max(b, 1) - 1) * Rb + r, BLK, stride=d)
                q = q_ref[sl, :]
                if b == 0:
                    kp, vp = kp_ref[pl.ds(r, BLK, stride=d), :], vp_ref[pl.ds(r, BLK, stride=d), :]
                else:
                    kp, vp = kc_ref[before, :], vc_ref[before, :]
                kk = jnp.concatenate([kp, kc_ref[sl, :]], axis=0).astype(bf16)
                vv = jnp.concatenate([vp, vc_ref[sl, :]], axis=0).astype(bf16)
                dov, lv = do_ref[sl, :], l_ref[sl, :]
                prod = dov * o_ref[sl, :]
                qb, dob = q.astype(bf16), dov.astype(bf16)
                dqs, dks, dvs = [], [], []
                for hh in range(2):
                    msk = head0 if hh == 0 else ~head0
                    qm = jnp.where(msk, q, 0.0).astype(bf16)
                    dom = jnp.where(msk, dov, 0.0).astype(bf16)
                    dsum = jnp.sum(jnp.where(msk, prod, 0.0), axis=-1, keepdims=True)
                    s = _dot_nt(qm, kk) * SCALE + b_ref[hh]
                    if b == 0:
                        s = s + first
                    p = jnp.exp(s - lv[:, HD * hh:HD * hh + 1])
                    ds = p * (_dot_nt(dom, vv) - dsum)
                    ds_ref[hh] += ds
                    dsb = ds.astype(bf16)
                    dqs.append(_dot(dsb, kk) * SCALE)
                    dks.append(_dot_tn(dsb, qb) * SCALE)
                    dvs.append(_dot_tn(p.astype(bf16), dob))
                dq_ref[sl, :] = jnp.where(head0, dqs[0], dqs[1])
                dk = jnp.where(head0, dks[0], dks[1])
                dv = jnp.where(head0, dvs[0], dvs[1])
                tk[sl, :] = dk[BLK:]
                tv[sl, :] = dv[BLK:]
                if b == 0:
                    prev_rows = pl.ds((nq - 1) * Rb + r, BLK, stride=d)
                    ck[prev_rows, :] += dk[:BLK]
                    cv[prev_rows, :] += dv[:BLK]
                else:
                    tk[before, :] += dk[:BLK]
                    tv[before, :] += dv[:BLK]

            for b in range(nq):
                if d == 1:
                    one(b, 0)
                else:
                    lax.fori_loop(0, d, lambda r, carry, b=b: (one(b, r), carry)[1], 0, unroll=4)
            dq_out[...] = dq_ref[...].astype(bf16)
            dk_out[...] = ck[...].astype(bf16)
            dv_out[...] = cv[...].astype(bf16)
            ck[...] = tk[...]
            cv[...] = tv[...]

        @pl.when(n == nb)
        def _():
            dk_out[...] = ck[...].astype(bf16)
            dv_out[...] = cv[...].astype(bf16)

    last = nb - 1

    def blk(cb, prev):
        if prev:
            return pl.BlockSpec((Rb, 128), lambda hp, n: (jnp.maximum(jnp.minimum(n, last) * nq - 1, 0), cb + hp))
        return pl.BlockSpec((R, 128), lambda hp, n: (jnp.minimum(n, last), cb + hp))

    cur = pl.BlockSpec((R, 128), lambda hp, n: (jnp.minimum(n, last), hp))
    anyspec = pl.BlockSpec(memory_space=pl.ANY)
    dqo = pl.BlockSpec((R, 128), lambda hp, n: (jnp.minimum(n, last), 4 * g + hp))
    dko = pl.BlockSpec((R, 128), lambda hp, n: (jnp.maximum(n - 1, 0), 4 * g + hp))
    return pl.pallas_call(
        body, name=f"attn_bwd_d{d}", grid=(4, nb + 1),
        out_shape=[SDS((S, 1536), bf16), SDS((S, 1536), bf16), SDS((S, 1536), bf16), SDS((8, BLK, 2 * BLK), f32)],
        in_specs=[blk(qb, False), blk(kb, False), blk(kb, True), blk(vb, False), blk(vb, True), cur, cur, cur,
                  pl.BlockSpec((2, BLK, 2 * BLK), lambda hp, n: (4 * g + hp, 0, 0)), anyspec, anyspec, anyspec],
        out_specs=[dqo, dko, dko, pl.BlockSpec((2, BLK, 2 * BLK), lambda hp, n: (hp, 0, 0))],
        scratch_shapes=[pltpu.VMEM((R, 128), f32)] * 5 + _qkv_scratch(R, Rb),
        input_output_aliases={9: 0, 10: 1, 11: 2},
        compiler_params=_cp("arbitrary", "arbitrary"),
    )(qkv, qkv, qkv, qkv, qkv, do, o, lse, bias, dq_all, dk_all, dv_all)


def _conv_z(cc, ch, hc, hh, cw_ref, first):
    halo = jnp.where(first, 0.0, hc.astype(f32) * hh.astype(f32))
    T = jnp.concatenate([halo, cc * ch], axis=0)
    z = cw_ref[2:3, :] * T + cw_ref[1:2, :] * pltpu.roll(T, 1, 0) + cw_ref[0:1, :] * pltpu.roll(T, 2, 0)
    return T, z[HALO:]


def _rest_specs(tm, with_next):
    per = tm // HALO
    specs = [pl.BlockSpec((tm, D), functools.partial(lambda i, k: (i, k), k=k)) for k in range(5)]
    specs += [pl.BlockSpec((HALO, D), functools.partial(lambda i, k: (jnp.maximum(i * per - 1, 0), k), k=k)) for k in (1, 2)]
    return specs


def _mix_out_fwd(x, mod9, rest, ogs, lgs, cw, wco, wao, wo):
    S = x.shape[0]
    tm = TMXF

    def body(x_ref, mod_ref, cb_ref, cc_ref, ch_ref, gc_ref, ga_ref, hc_ref, hh_ref,
             o0, o1, o2, l0, l1, l2, cw_ref, wco_ref, wao_ref, wo_ref,
             xo_ref, o_ref, lse_ref, yc_ref, ya_ref, out_ref):
        i = pl.program_id(0)
        lv = [l0[...], l1[...], l2[...]]
        mx = jnp.maximum(jnp.maximum(lv[0], lv[1]), lv[2])
        es = [jnp.exp(l - mx) for l in lv]
        den = es[0] + es[1] + es[2]
        o = (es[0] / den) * o0[...] + (es[1] / den) * o1[...] + (es[2] / den) * o2[...]
        o_ref[...] = o
        lse_ref[...] = mx + jnp.log(den)
        _, z = _conv_z(cc_ref[...].astype(f32), ch_ref[...].astype(f32), hc_ref[...], hh_ref[...], cw_ref, i == 0)
        p = (cb_ref[...].astype(f32) * z).astype(bf16)
        yc = _dot(p, wco_ref[...])
        ya = _dot(o.astype(bf16), wao_ref[...])
        yc_ref[...] = yc.astype(bf16)
        ya_ref[...] = ya.astype(bf16)
        merged = _sigmoid(gc_ref[...].astype(f32)) * yc + _sigmoid(ga_ref[...].astype(f32)) * ya
        out = _dot(merged.astype(bf16), wo_ref[...])
        out_ref[...] = out.astype(bf16)
        xo_ref[...] = x_ref[...] + mod_ref[5:6, :] * out

    row = pl.BlockSpec((tm, D), lambda i: (i, 0))
    att = pl.BlockSpec((tm, 512), lambda i: (i, 0))
    full = lambda shp: pl.BlockSpec(shp, lambda i: (0, 0))
    return pl.pallas_call(
        body, name="mix_out_fwd", grid=(S // tm,),
        out_shape=[SDS((S, D), f32), SDS((S, 512), f32), SDS((S, 512), f32), SDS((S, D), bf16), SDS((S, D), bf16), SDS((S, D), bf16)],
        in_specs=[row, full((9, D))] + _rest_specs(tm, False) + [att] * 6 + [full((3, D)), full((D, D)), full((512, D)), full((D, D))],
        out_specs=[row, att, att, row, row, row],
        compiler_params=_cp("arbitrary"),
    )(x, mod9, *([rest] * 7), *ogs, *lgs, cw, wco, wao, wo)


def _mix_out_bwd(dxo, mod9, outv, yc, ya, rest, o, cw, wco, wao, wo):
    S = dxo.shape[0]
    tm = TMX
    ni = S // tm

    def body(dxo_ref, mod_ref, out_ref, yc_ref, ya_ref, cb_ref, cc_ref, ch_ref, gc_ref, ga_ref, hc_ref, hh_ref,
             o_ref, cw_ref, wco_ref, wao_ref, wo_ref,
             dp_ref, dg2_ref, do_ref, dwco_ref, dwao_ref, dwo_ref, sm_ref, aco, aao, ao):
        i = pl.program_id(0)

        @pl.when(i == 0)
        def _():
            sm_ref[...] = jnp.zeros_like(sm_ref)
            aco[...] = jnp.zeros_like(aco)
            aao[...] = jnp.zeros_like(aao)
            ao[...] = jnp.zeros_like(ao)

        dxo_v = dxo_ref[...]
        sm_ref[2:3, :] += jnp.sum(out_ref[...].astype(f32) * dxo_v, axis=0, keepdims=True)
        dout = (mod_ref[5:6, :] * dxo_v).astype(bf16)
        dmerged = _dot_nt(dout, wo_ref[...])
        sc, sa = _sigmoid(gc_ref[...].astype(f32)), _sigmoid(ga_ref[...].astype(f32))
        ycv, yav = yc_ref[...].astype(f32), ya_ref[...].astype(f32)
        ao[...] += _dot_tn((sc * ycv + sa * yav).astype(bf16), dout)
        dyc = (dmerged * sc).astype(bf16)
        dya = (dmerged * sa).astype(bf16)
        dg2_ref[:, :D] = (dmerged * ycv * sc * (1.0 - sc)).astype(bf16)
        dg2_ref[:, D:] = (dmerged * yav * sa * (1.0 - sa)).astype(bf16)
        dp_ref[...] = _dot_nt(dyc, wco_ref[...]).astype(bf16)
        _, z = _conv_z(cc_ref[...].astype(f32), ch_ref[...].astype(f32), hc_ref[...], hh_ref[...], cw_ref, i == 0)
        aco[...] += _dot_tn((cb_ref[...].astype(f32) * z).astype(bf16), dyc)
        do_ref[...] = _dot_nt(dya, wao_ref[...])
        aao[...] += _dot_tn(o_ref[...].astype(bf16), dya)

        @pl.when(i == ni - 1)
        def _():
            dwco_ref[...] = aco[...].astype(bf16)
            dwao_ref[...] = aao[...].astype(bf16)
            dwo_ref[...] = ao[...].astype(bf16)

    row = pl.BlockSpec((tm, D), lambda i: (i, 0))
    att = pl.BlockSpec((tm, 512), lambda i: (i, 0))
    full = lambda shp: pl.BlockSpec(shp, lambda i: (0, 0))
    return pl.pallas_call(
        body, name="mix_out_bwd", grid=(ni,),
        out_shape=[SDS((S, D), bf16), SDS((S, 2 * D), bf16), SDS((S, 512), f32),
                   SDS((D, D), bf16), SDS((512, D), bf16), SDS((D, D), bf16), SDS((8, D), f32)],
        in_specs=[row, full((9, D)), row, row, row] + _rest_specs(tm, False) + [att, full((3, D)), full((D, D)), full((512, D)), full((D, D))],
        out_specs=[row, pl.BlockSpec((tm, 2 * D), lambda i: (i, 0)), att, full((D, D)), full((512, D)), full((D, D)), full((8, D))],
        scratch_shapes=[pltpu.VMEM((D, D), f32), pltpu.VMEM((512, D), f32), pltpu.VMEM((D, D), f32)],
        compiler_params=_cp("arbitrary"),
    )(dxo, mod9, outv, yc, ya, *([rest] * 7), o, cw, wco, wao, wo)


def _conv_bwd(dp, rest, cw):
    S = dp.shape[0]
    tm = TM
    per = tm // HALO
    nh = S // HALO
    ni = S // tm

    def body(dp_ref, dpn_ref, cb_ref, cbn_ref, cc_ref, ch_ref, hc_ref, hh_ref, cw_ref, d3_ref, sm_ref):
        i = pl.program_id(0)

        @pl.when(i == 0)
        def _():
            sm_ref[...] = jnp.zeros_like(sm_ref)

        cc, ch = cc_ref[...].astype(f32), ch_ref[...].astype(f32)
        T, z = _conv_z(cc, ch, hc_ref[...], hh_ref[...], cw_ref, i == 0)
        dpv = dp_ref[...].astype(f32)
        cbv = cb_ref[...].astype(f32)
        dz = dpv * cbv
        dzn = jnp.where(i == ni - 1, 0.0, dpn_ref[...].astype(f32) * cbn_ref[...].astype(f32))
        E = jnp.concatenate([dz, dzn], axis=0)
        ne = tm + HALO
        dT = cw_ref[2:3, :] * E + cw_ref[1:2, :] * pltpu.roll(E, ne - 1, 0) + cw_ref[0:1, :] * pltpu.roll(E, ne - 2, 0)
        dT = dT[:tm]
        d3_ref[:, :D] = (dpv * z).astype(bf16)
        d3_ref[:, D:2 * D] = (dT * ch).astype(bf16)
        d3_ref[:, 2 * D:] = (dT * cc).astype(bf16)
        sm_ref[2:3, :] += jnp.sum(dz * T[HALO:], axis=0, keepdims=True)
        sm_ref[1:2, :] += jnp.sum(dz * pltpu.roll(T, 1, 0)[HALO:], axis=0, keepdims=True)
        sm_ref[0:1, :] += jnp.sum(dz * pltpu.roll(T, 2, 0)[HALO:], axis=0, keepdims=True)

    row = pl.BlockSpec((tm, D), lambda i: (i, 0))
    nxt = pl.BlockSpec((HALO, D), lambda i: (jnp.minimum((i + 1) * per, nh - 1), 0))
    col = lambda k: pl.BlockSpec((tm, D), lambda i: (i, k))
    prv = lambda k: pl.BlockSpec((HALO, D), lambda i: (jnp.maximum(i * per - 1, 0), k))
    return pl.pallas_call(
        body, name="conv_bwd", grid=(ni,),
        out_shape=[SDS((S, 3 * D), bf16), SDS((8, D), f32)],
        in_specs=[row, nxt, col(0), nxt, col(1), col(2), prv(1), prv(2), pl.BlockSpec((3, D), lambda i: (0, 0))],
        out_specs=[pl.BlockSpec((tm, 3 * D), lambda i: (i, 0)), pl.BlockSpec((8, D), lambda i: (0, 0))],
        compiler_params=_cp("arbitrary"),
    )(dp, dp, rest, rest, rest, rest, rest, rest, cw)


_DU_RANGES = ((0, 3), (3, 6), (6, 9), (9, 15), (15, 19))
N_CBLK = IN_W // CB


def _mix_in_bwd_dh(dxo, x, mod9, g3, dus, win):
    S = x.shape[0]

    def body(dxo_ref, x_ref, mod_ref, g_ref, s0, s1, s2, s3, s4, w_ref, dxi_ref, sm_ref, acc):
        i, kb = pl.program_id(0), pl.program_id(1)

        @pl.when((i == 0) & (kb == 0))
        def _():
            sm_ref[...] = jnp.zeros_like(sm_ref)

        @pl.when(kb == 0)
        def _():
            acc[...] = jnp.zeros_like(acc)

        for src, (lo, hi) in zip((s0, s1, s2, s3, s4), _DU_RANGES):
            @pl.when((kb >= lo) & (kb < hi))
            def _(src=src):
                acc[...] += _dot_nt(src[...].astype(bf16), w_ref[...])

        @pl.when(kb == N_CBLK - 1)
        def _():
            g, scale = g_ref[1:2, :], mod_ref[4:5, :]
            _, xhat, rstd = _norm_fwd(x_ref[...], g, mod_ref[3:4, :], scale)
            dx, dshift, dscale, dg = _norm_bwd(acc[...], xhat, rstd, g, scale)
            dxi_ref[...] = dxo_ref[...] + dx
            sm_ref[0:1, :] += dshift
            sm_ref[1:2, :] += dscale
            sm_ref[3:4, :] += dg

    row = pl.BlockSpec((TMP, D), lambda i, kb: (i, 0))

    def src_spec(lo, hi):
        return pl.BlockSpec((TMP, CB), lambda i, kb: (i, jnp.clip(kb - lo, 0, hi - lo - 1)))

    return pl.pallas_call(
        body, name="mix_in_bwd_dh", grid=(S // TMP, N_CBLK),
        out_shape=[SDS((S, D), f32), SDS((8, D), f32)],
        in_specs=[row, row, pl.BlockSpec((9, D), lambda i, kb: (0, 0)), pl.BlockSpec((3, D), lambda i, kb: (0, 0))]
                 + [src_spec(lo, hi) for lo, hi in _DU_RANGES] + [pl.BlockSpec((D, CB), lambda i, kb: (0, kb))],
        out_specs=[row, pl.BlockSpec((8, D), lambda i, kb: (0, 0))],
        scratch_shapes=[pltpu.VMEM((TMP, D), f32)],
        compiler_params=_cp("arbitrary", "arbitrary"),
    )(dxo, x, mod9, g3, *dus, win)


def _mix_in_bwd_dw(h, dus):
    S = h.shape[0]
    ni = S // TMW

    def body(h_ref, s0, s1, s2, s3, s4, dw_ref, acc):
        kb, i = pl.program_id(0), pl.program_id(1)

        @pl.when(i == 0)
        def _():
            acc[...] = jnp.zeros_like(acc)

        for src, (lo, hi) in zip((s0, s1, s2, s3, s4), _DU_RANGES):
            @pl.when((kb >= lo) & (kb < hi))
            def _(src=src):
                rows = pl.ds(pl.multiple_of(i * TMW, TMW), TMW)
                acc[...] += _dot_tn(h_ref[rows, :], src[...].astype(bf16))

        @pl.when(i == ni - 1)
        def _():
            dw_ref[...] = acc[...].astype(bf16)

    def src_spec(lo, hi):
        def imap(kb, i):
            on = (kb >= lo) & (kb < hi)
            return (jnp.where(on, i, 0), jnp.clip(kb - lo, 0, hi - lo - 1))
        return pl.BlockSpec((TMW, CB), imap)

    return pl.pallas_call(
        body, name="mix_in_bwd_dw", grid=(N_CBLK, ni),
        out_shape=SDS((D, IN_W), bf16),
        in_specs=[pl.BlockSpec((S, D), lambda kb, i: (0, 0))] + [src_spec(lo, hi) for lo, hi in _DU_RANGES],
        out_specs=pl.BlockSpec((D, CB), lambda kb, i: (0, kb)),
        scratch_shapes=[pltpu.VMEM((D, CB), f32)],
        compiler_params=_cp("arbitrary", "arbitrary"),
    )(h, *dus)


def _loss_head(x, fg, tgt):
    S = x.shape[0]

    def body(x_ref, g_ref, t_ref, ls_ref, dx_ref, sm_ref):
        i = pl.program_id(0)

        @pl.when(i == 0)
        def _():
            ls_ref[...] = jnp.zeros_like(ls_ref)
            sm_ref[...] = jnp.zeros_like(sm_ref)

        xv, g = x_ref[...], g_ref[...]
        rstd = lax.rsqrt(jnp.mean(xv * xv, axis=-1, keepdims=True) + EPS)
        xhat = xv * rstd
        e = xhat * g - t_ref[...]
        ls_ref[...] += 0.5 * jnp.sum(jnp.mean(e * e, axis=-1, keepdims=True))
        dy = e * (1.0 / D)
        sm_ref[0:1, :] += jnp.sum(dy * xhat, axis=0, keepdims=True)
        dxh = dy * g
        dx_ref[...] = rstd * (dxh - xhat * jnp.mean(dxh * xhat, axis=-1, keepdims=True))

    row = pl.BlockSpec((TM, D), lambda i: (i, 0))
    return pl.pallas_call(
        body, name="loss_head", grid=(S // TM,),
        out_shape=[SDS((8, 128), f32), SDS((S, D), f32), SDS((8, D), f32)],
        in_specs=[row, pl.BlockSpec((1, D), lambda i: (0, 0)), row],
        out_specs=[pl.BlockSpec((8, 128), lambda i: (0, 0)), row, pl.BlockSpec((8, D), lambda i: (0, 0))],
        compiler_params=_cp("arbitrary"),
    )(x, fg, tgt)


def _adam(w, g, m, v):
    m2 = B1 * m + (1.0 - B1) * g
    v2 = B2 * v + (1.0 - B2) * (g * g)
    delta = -LR * ((m2 / BC1) / (jnp.sqrt(v2 / BC2) + AEPS) + WD * w)
    return delta, m2, v2


def _row_tile(rows, cols):
    for tr in (512, 352, 256, 128, 64):
        if rows % tr == 0 and tr * cols * 4 <= (3 << 19):
            return tr
    raise ValueError((rows, cols))


def _sum_slots(land):
    _, R, C = land.shape
    tr = _row_tile(R, C)

    def body(l_ref, t_ref):
        t = l_ref[0].astype(f32)
        for k in range(1, N_CHIPS):
            t = t + l_ref[k].astype(f32)
        t_ref[...] = t

    return pl.pallas_call(
        body, name="sum_slots", grid=(R // tr,),
        out_shape=SDS((R, C), f32),
        in_specs=[pl.BlockSpec((N_CHIPS, tr, C), lambda i: (0, i, 0))],
        out_specs=pl.BlockSpec((tr, C), lambda i: (i, 0)),
        compiler_params=_cp("arbitrary"),
    )(land)


def _adamw_pair(w2, m2, v2, ta, tb, outs, slot):
    R, C = ta.shape
    tr = _row_tile(R, C)
    nrt = R // tr

    def body(w_ref, m_ref, v_ref, ta_ref, tb_ref, g_in, d_in, m_in, v_in, g_ref, d_ref, mo_ref, vo_ref):
        g = ta_ref[...].astype(f32) + tb_ref[...].astype(f32)
        delta, mn, vn = _adam(w_ref[...], g, m_ref[...], v_ref[...])
        g_ref[...] = g
        d_ref[...] = delta
        mo_ref[...] = mn
        vo_ref[...] = vn

    big = pl.BlockSpec((tr, C), lambda i: (slot * nrt + i, 0))
    loc = pl.BlockSpec((tr, C), lambda i: (i, 0))
    anyspec = pl.BlockSpec(memory_space=pl.ANY)
    return pl.pallas_call(
        body, name="adamw_pair", grid=(nrt,),
        out_shape=[SDS(o.shape, f32) for o in outs],
        in_specs=[big, big, big, loc, loc] + [anyspec] * 4,
        out_specs=[big] * 4,
        input_output_aliases={5: 0, 6: 1, 7: 2, 8: 3},
        compiler_params=_cp("arbitrary"),
    )(w2, m2, v2, ta, tb, *outs)


def _adamw_small(w, g, m, v):
    def body(w_ref, g_ref, m_ref, v_ref, d_ref, mo_ref, vo_ref):
        delta, mn, vn = _adam(w_ref[...], g_ref[...], m_ref[...], v_ref[...])
        d_ref[...] = delta
        mo_ref[...] = mn
        vo_ref[...] = vn

    return pl.pallas_call(body, name="adamw_small", out_shape=[SDS(w.shape, f32)] * 3)(w, g, m, v)


def _ada_w_update(cs_all, dmod_sh, w, m, v):
    tr = 256

    def body(cs_ref, dm_ref, w_ref, m_ref, v_ref, g_ref, d_ref, mo_ref, vo_ref):
        g = _dot_tn(cs_ref[...].astype(bf16), dm_ref[...].astype(bf16))
        delta, mn, vn = _adam(w_ref[...], g, m_ref[...], v_ref[...])
        g_ref[...] = g
        d_ref[...] = delta
        mo_ref[...] = mn
        vo_ref[...] = vn

    blk = pl.BlockSpec((None, tr, ADA_SH), lambda l, i: (l, i, 0))
    return pl.pallas_call(
        body, name="ada_w_update", grid=(DEPTH, D // tr),
        out_shape=[SDS(w.shape, f32)] * 4,
        in_specs=[pl.BlockSpec((8, tr), lambda l, i: (0, i)), pl.BlockSpec((None, 8, ADA_SH), lambda l, i: (l, 0, 0)), blk, blk, blk],
        out_specs=[blk] * 4,
        compiler_params=_cp("arbitrary", "arbitrary"),
    )(cs_all, dmod_sh, w, m, v)


def _sum_devices(gathered):
    _, R, C = gathered.shape

    def body(g_ref, o_ref):
        t = g_ref[0]
        for k in range(1, 8):
            t = t + g_ref[k]
        o_ref[...] = t

    return pl.pallas_call(body, name="sum_devices", out_shape=SDS((R, C), f32))(gathered)


def _layer_fwd(x, mod9, g3, cw, getw, bias):
    W = {}

    def take(gname, after, mod9):
        w, tok = getw(gname, after)
        W.update(w)
        return mod9 if tok is None else mod9 + tok[0, 0]

    mod9 = take("A", x, mod9)
    x1, h1, a1, u1, hid1, y1 = _ffn_fwd(x, mod9, g3, W["wg0"], W["wu0"], W["wd0"], 0)
    mod9 = take("B", x1, mod9)
    hm, qkv = _mix_qkv(x1, mod9, g3, W["win"])
    rest = _mix_rest(hm, W["win"])
    ogs, lgs = [], []
    for g in range(3):
        og, lg = _attn_fwd(qkv, bias, g)
        ogs.append(og)
        lgs.append(lg)
    mod9 = take("C", ogs[2], mod9)
    x2, o, lse, yc, ya, outv = _mix_out_fwd(x1, mod9, rest, ogs, lgs, cw, W["wco"], W["wao"], W["wo"])
    mod9 = take("D", x2, mod9)
    x3, h3, a3, u3, hid3, y3 = _ffn_fwd(x2, mod9, g3, W["wg1"], W["wu1"], W["wd1"], 2)
    saved = dict(x0=x, x1=x1, x2=x2, h1=h1, a1=a1, u1=u1, hid1=hid1, y1=y1, hm=hm, qkv=qkv, rest=rest, o=o, lse=lse, yc=yc, ya=ya,
                 outv=outv, h3=h3, a3=a3, u3=u3, hid3=hid3, y3=y3)
    return x3, saved, W


def _layer_bwd(dx, sv, mod9, g3, cw, W, bias, emit):
    S = dx.shape[0]
    dw = {}

    def send(gname, mod9):
        tok = emit(gname, dw)
        return mod9 if tok is None else mod9 + tok[0, 0]

    dx2, da, du, dy, sm3 = _ffn_bwd1(dx, sv["x2"], mod9, g3, sv["y3"], sv["a3"], sv["u3"], W["wg1"], W["wu1"], W["wd1"], 2)
    dw["wg1"], dw["wu1"], dw["wd1"] = _ffn_bwd2(sv["h3"], da, du, sv["hid3"], dy)
    mod9 = send("D", mod9)
    dp, dg2, do, dw["wco"], dw["wao"], dw["wo"], smo = _mix_out_bwd(
        dx2, mod9, sv["outv"], sv["yc"], sv["ya"], sv["rest"], sv["o"], cw, W["wco"], W["wao"], W["wo"])
    mod9_c = send("C", mod9)
    cw = cw + (mod9_c - mod9)[0:1, :]
    mod9 = mod9_c
    d3, smc = _conv_bwd(dp, sv["rest"], cw)
    dq = lax.empty((S, 1536), bf16)
    dk = lax.empty((S, 1536), bf16)
    dv = lax.empty((S, 1536), bf16)
    dsaccs = []
    for g in range(3):
        dq, dk, dv, dsg = _attn_bwd(sv["qkv"], do, sv["o"], sv["lse"], bias, dq, dk, dv, g)
        dsaccs.append(dsg)
    dus = (dq, dk, dv, d3, dg2)
    dx1, smm = _mix_in_bwd_dh(dx2, sv["x1"], mod9, g3, dus, W["win"])
    dw["win"] = _mix_in_bwd_dw(sv["hm"], dus)
    mod9 = send("B", mod9)
    dx0, da, du, dy, sm1 = _ffn_bwd1(dx1, sv["x0"], mod9, g3, sv["y1"], sv["a1"], sv["u1"], W["wg0"], W["wu0"], W["wd0"], 0)
    dw["wg0"], dw["wu0"], dw["wd0"] = _ffn_bwd2(sv["h1"], da, du, sv["hid1"], dy)
    send("A", mod9)
    dmod = jnp.concatenate([sm1[0:3], smm[0:2], smo[2:3], sm3[0:3]], axis=0)
    dng = jnp.concatenate([sm1[3:4], smm[3:4], sm3[3:4]], axis=0)
    return dx0, dmod, dng, smc[0:3], jnp.concatenate(dsaccs, axis=0)


def _chip_cols(a, chip, width):
    return lax.dynamic_slice_in_dim(a, chip * width, width, axis=a.ndim - 1)


def kernel(x, c, ada_w, ada_b, norm_g, ffn_w_gate, ffn_w_up, ffn_w_down, w_in, conv_w, w_conv_out, w_attn_out, w_o, rel_bias, final_g, loss_target, m_ada_w, m_ada_b, m_norm_g, m_ffn_w_gate, m_ffn_w_up, m_ffn_w_down, m_w_in, m_conv_w, m_w_conv_out, m_w_attn_out, m_w_o, m_rel_bias, m_final_g, v_ada_w, v_ada_b, v_norm_g, v_ffn_w_gate, v_ffn_w_up, v_ffn_w_down, v_w_in, v_conv_w, v_w_conv_out, v_w_attn_out, v_w_o, v_rel_bias, v_final_g):
    ix, iy, ic = lax.axis_index("x"), lax.axis_index("y"), lax.axis_index("c")
    chip = 2 * ix + iy
    dev = 4 * ix + 2 * iy + ic
    xs = x.reshape(x.shape[1:])
    S = xs.shape[0]
    qd = D // N_CHIPS

    chip_arr = jnp.reshape(chip, (1,)).astype(jnp.int32)
    names = [w[0] for w in WCLASSES]

    tr2 = lambda a: jnp.swapaxes(a, -1, -2)
    wg_t, wu_t = tr2(ffn_w_gate), tr2(ffn_w_up)

    def layer_shards(l):
        return [(wg_t, (l, 0)), (wu_t, (l, 0)), (ffn_w_down, (l, 0)), (wg_t, (l, 1)), (wu_t, (l, 1)),
                (ffn_w_down, (l, 1)), (w_in, (l,)), (w_conv_out, (l,)), (w_attn_out, (l,)), (w_o, (l,))]

    started = {}
    extra_starts = {(0, "A"): [(0, "B")], (0, "B"): [(0, "C"), (0, "D"), (1, "A")]}

    casts = {}

    def cast_group(l, gname, after):
        shards = layer_shards(l)
        casts[(l, gname)] = _gather_group_cast(GROUPS[gname], [shards[q] for q in GROUPS[gname]], chip_arr, after)

    def start_gather(l, gname, after):
        started[(l, gname)] = _gather_group_start(f"l{l}{gname}", GROUPS[gname], casts[(l, gname)], after)
        return started[(l, gname)][-1]

    pad8 = lambda a: jnp.pad(a, ((0, -a.shape[0] % 8), (0, 0)))
    pack = jnp.concatenate([pad8(c), pad8(norm_g.reshape(3, D)), pad8(conv_w.reshape(3, D))], axis=0)
    g1 = _allgather_small(pack).reshape(8, 24, D)
    c_all = g1[:, 0]
    by_chip = g1[0::2]
    ng_full = jnp.concatenate([by_chip[j, 8:11].reshape(DEPTH, 3, qd) for j in range(N_CHIPS)], axis=-1)
    cw_full = jnp.concatenate([by_chip[j, 16:19].reshape(DEPTH, 3, qd) for j in range(N_CHIPS)], axis=-1)
    mod_sh, cs_all = _mod_shards(c_all, ada_w, _chip_cols(ada_b, chip, ADA_SH))
    g2 = _allgather_small(mod_sh.reshape(DEPTH * 8, ADA_SH)).reshape(8, DEPTH, 8, ADA_SH)
    mine = lax.dynamic_index_in_dim(g2[0::2], dev, axis=2, keepdims=False)
    mod = jnp.transpose(mine, (1, 0, 2)).reshape(DEPTH, 9, D)

    cast_group(0, "A", c)
    tok0 = start_gather(0, "A", mod)
    for l in range(DEPTH):
        for gname in GROUPS:
            if (l, gname) not in casts:
                cast_group(l, gname, tok0)
    buckets = jnp.asarray(_bucket_table())
    bias = _bias_blocks(rel_bias, buckets)
    last_cast = casts[(DEPTH - 1, "D")][-1]

    need_order = [(l, gname) for l in range(DEPTH) for gname in GROUPS]
    forwarded = {}

    def forward_gather(key, after):
        forwarded[key] = _gather_group_forward(f"l{key[0]}{key[1]}", GROUPS[key[1]], started[key], after)
        return forwarded[key][-1]

    def make_getw(l):
        def getw(gname, after):
            key = (l, gname)
            if key == (0, "A"):
                after = last_cast
            if key not in forwarded:
                after = forward_gather(key, after)
            full = _gather_group_wait(f"l{l}{gname}", GROUPS[gname], started[key][0], forwarded[key], after)
            tok = None
            before = set(started)
            for nl, ng in extra_starts.get(key, []) + [(l + 1, gname)]:
                if nl < DEPTH and (nl, ng) not in started:
                    tok = start_gather(nl, ng, full[0] if tok is None else tok)
            at = need_order.index(key) + 1
            if at < len(need_order) and need_order[at] in before and need_order[at] not in forwarded:
                tok = forward_gather(need_order[at], full[0] if tok is None else tok)
            return {names[q]: f for q, f in zip(GROUPS[gname], full)}, tok
        return getw

    Ws, saves = [], []
    xc = xs
    for l in range(DEPTH):
        xc, sv, W = _layer_fwd(xc, mod[l], ng_full[l], cw_full[l], make_getw(l), bias)
        Ws.append(W)
        saves.append(sv)

    ls, dx, smf = _loss_head(xc, final_g.reshape(1, D), loss_target.reshape(loss_target.shape[1:]))
    loss = lax.psum(ls[0, 0], ("x", "y", "c"))

    params = dict(wg=wg_t, wu=wu_t, wd=ffn_w_down, win=w_in, wco=w_conv_out, wao=w_attn_out, wo=w_o)
    moms = dict(wg=tr2(m_ffn_w_gate), wu=tr2(m_ffn_w_up), wd=m_ffn_w_down, win=m_w_in, wco=m_w_conv_out, wao=m_w_attn_out, wo=m_w_o)
    vars_ = dict(wg=tr2(v_ffn_w_gate), wu=tr2(v_ffn_w_up), wd=v_ffn_w_down, win=v_w_in, wco=v_w_conv_out, wao=v_w_attn_out, wo=v_w_o)
    flat = lambda a: a.reshape(-1, a.shape[-1])
    big_out = {k: [lax.empty(flat(p).shape, f32) for _ in range(4)] for k, p in params.items()}
    dmods, dngs, dcws, dsaccs = [None] * DEPTH, [None] * DEPTH, [None] * DEPTH, [None] * DEPTH

    def finish(l, gname, started, after):
        group = GROUPS[gname]
        pieces, lands = _scatter_group_wait(f"l{l}{gname}", group, started, after)
        ts = [_sum_own_slots(pieces[i], lands[i], *_cls(q), chip_arr) for i, q in enumerate(group)]
        tsib = _swap_sibling(ts)
        for i, q in enumerate(group):
            name = names[q]
            key = name.rstrip("01")
            slot = 2 * l + int(name[-1]) if name[-1] in "01" else l
            big_out[key] = _adamw_pair(flat(params[key]), flat(moms[key]), flat(vars_[key]), ts[i], tsib[i], big_out[key], slot)

    pending, tok = [], None
    for l in reversed(range(DEPTH)):
        modl = mod[l] if tok is None else mod[l] + tok[0, 0]
        mine = []

        def emit(gname, dw, l=l, mine=mine):
            prev = mine[-1][2][-1] if mine else dx
            mine.append((l, gname, _scatter_group_start(f"l{l}{gname}", GROUPS[gname], [dw[names[q]] for q in GROUPS[gname]], prev)))
            return mine[-1][2][-1]

        dx, dmods[l], dngs[l], dcws[l], dsaccs[l] = _layer_bwd(dx, saves[l], modl, ng_full[l], cw_full[l], Ws[l], bias, emit)
        for pl_, pg, pst in pending:
            finish(pl_, pg, pst, dx)
        pending, tok = mine, mine[-1][2][-1]
    for pl_, pg, pst in pending[:-1]:
        finish(pl_, pg, pst, pending[-1][2][-1])

    drb = jnp.transpose(_bias_grad(dsaccs, buckets)[:, 0, :NUM_BUCKETS])
    drb_row = jnp.pad(drb.reshape(1, NUM_BUCKETS * 24), ((0, 0), (0, D - NUM_BUCKETS * 24)))
    pack2 = jnp.concatenate([pad8(a) for a in dmods] + [pad8(a) for a in dngs] + [pad8(a) for a in dcws] + [smf, pad8(drb_row)], axis=0)
    n_rows = pack2.shape[0]
    g3 = _allgather_small(pack2).reshape(8, n_rows, D)
    tot = _sum_devices(g3)
    o_ng, o_cw, o_fg, o_rb = 16 * DEPTH, 24 * DEPTH, 32 * DEPTH, 32 * DEPTH + 8
    g_ada_b = jnp.stack([tot[16 * l:16 * l + 9] for l in range(DEPTH)]).reshape(DEPTH, 9 * D)
    g_norm_g = _chip_cols(jnp.stack([tot[o_ng + 8 * l:o_ng + 8 * l + 3] for l in range(DEPTH)]), chip, qd)
    g_conv_w = _chip_cols(jnp.stack([tot[o_cw + 8 * l:o_cw + 8 * l + 3] for l in range(DEPTH)]), chip, qd)
    g_final_g = tot[o_fg]
    g_rel_bias = tot[o_rb, :NUM_BUCKETS * 24].reshape(NUM_BUCKETS, 24)
    dmod_all = jnp.stack([g3[:, 16 * l:16 * l + 9].reshape(8, 9 * D) for l in range(DEPTH)])
    dmod_sh = _chip_cols(dmod_all, chip, ADA_SH)
    g_ada_w, d_ada_w, nm_ada_w, nv_ada_w = _ada_w_update(cs_all, dmod_sh, ada_w, m_ada_w, v_ada_w)

    def small(w, g, m, v):
        shp = w.shape
        to2 = lambda a: a.reshape(-1, shp[-1])
        return [o.reshape(shp) for o in _adamw_small(to2(w), to2(g), to2(m), to2(v))]

    d_ada_b, nm_ada_b, nv_ada_b = small(ada_b, g_ada_b, m_ada_b, v_ada_b)
    d_norm_g, nm_norm_g, nv_norm_g = small(norm_g, g_norm_g, m_norm_g, v_norm_g)
    d_conv_w, nm_conv_w, nv_conv_w = small(conv_w, g_conv_w, m_conv_w, v_conv_w)
    d_rel_bias, nm_rel_bias, nv_rel_bias = small(rel_bias, g_rel_bias, m_rel_bias, v_rel_bias)
    d_final_g, nm_final_g, nv_final_g = small(final_g, g_final_g, m_final_g, v_final_g)

    behind = nv_ada_w[0, 0:8, 0:128]
    for key in big_out:
        behind = behind + big_out[key][3][0:8, 0:128]
    finish(*pending[-1], behind)

    def big(key, which):
        out = big_out[key][which].reshape(params[key].shape)
        return tr2(out) if key in ("wg", "wu") else out

    grads = [g_ada_w, g_ada_b, g_norm_g, big("wg", 0), big("wu", 0), big("wd", 0), big("win", 0), g_conv_w, big("wco", 0),
             big("wao", 0), big("wo", 0), g_rel_bias, g_final_g]
    deltas = [d_ada_w, d_ada_b, d_norm_g, big("wg", 1), big("wu", 1), big("wd", 1), big("win", 1), d_conv_w, big("wco", 1),
              big("wao", 1), big("wo", 1), d_rel_bias, d_final_g]
    new_m = [nm_ada_w, nm_ada_b, nm_norm_g, big("wg", 2), big("wu", 2), big("wd", 2), big("win", 2), nm_conv_w, big("wco", 2),
             big("wao", 2), big("wo", 2), nm_rel_bias, nm_final_g]
    new_v = [nv_ada_w, nv_ada_b, nv_norm_g, big("wg", 3), big("wu", 3), big("wd", 3), big("win", 3), nv_conv_w, big("wco", 3),
             big("wao", 3), big("wo", 3), nv_rel_bias, nv_final_g]
    return (loss, dx.reshape(x.shape), *grads, *deltas, *new_m, *new_v)
```

```python
import functools

import numpy as np
import jax
import jax.numpy as jnp
from jax import lax
from jax.experimental import pallas as pl
from jax.experimental.pallas import tpu as pltpu

f32, bf16 = jnp.float32, jnp.bfloat16
SDS = jax.ShapeDtypeStruct
MESH = pl.DeviceIdType.MESH

D = 1024
DEPTH = 4
N_CHIPS = 4
FB = 704
HD = 64
QKV_W = 4608
REST_W = 5120
IN_W = QKV_W + REST_W
WIN_SH = IN_W // N_CHIPS
ADA_SH = 9 * D // N_CHIPS
BLK = 128
DILATIONS = (1, 4, 16)
Q_BLOCKS = (4, 2, 1)
NUM_BUCKETS, MAX_DISTANCE = 32, 2048
EPS = 1e-6
NEG = -1e30
SCALE = HD ** -0.5
LR, B1, B2, AEPS, WD, STEP = 0.001, 0.9, 0.999, 1e-08, 0.01, 10
BC1 = 1.0 - B1 ** STEP
BC2 = 1.0 - B2 ** STEP
VMEM_LIMIT = 56 * 1024 * 1024
TM = 512
TMW = 2048
TMP = 1024
TMF = 1024
SH_STEP = 2
TMX = 256
TMXF = 512
HALO = 16
CB = 512
CBQ = 1536


def _cp(*sem):
    return pltpu.CompilerParams(dimension_semantics=sem if sem else None, vmem_limit_bytes=VMEM_LIMIT)


def _dot(a, b):
    return jnp.dot(a, b, preferred_element_type=f32)


def _dot_nt(a, b):
    return lax.dot_general(a, b, (((1,), (1,)), ((), ())), preferred_element_type=f32)


def _dot_tn(a, b):
    return lax.dot_general(a, b, (((0,), (0,)), ((), ())), preferred_element_type=f32)


def _sigmoid(x):
    return 0.5 * jnp.tanh(0.5 * x) + 0.5


def _norm_fwd(x, g, shift, scale):
    rstd = lax.rsqrt(jnp.mean(x * x, axis=-1, keepdims=True) + EPS)
    xhat = x * rstd
    return xhat * g * (1.0 + scale) + shift, xhat, rstd


def _norm_bwd(dh, xhat, rstd, g, scale):
    dshift = jnp.sum(dh, axis=0, keepdims=True)
    dscale = jnp.sum(dh * xhat * g, axis=0, keepdims=True)
    dg = jnp.sum(dh * xhat * (1.0 + scale), axis=0, keepdims=True)
    dxh = dh * (g * (1.0 + scale))
    dx = rstd * (dxh - xhat * jnp.mean(dxh * xhat, axis=-1, keepdims=True))
    return dx, dshift, dscale, dg


def _allgather_small(xp):
    m_per, n = xp.shape

    def body(x_ref, out_ref, send_sems, recv_sems, local_sem):
        x, y, c = lax.axis_index("x"), lax.axis_index("y"), lax.axis_index("c")
        me, sibling = (x, y, c), (x, y, 1 - c)
        chips = [(1 - x, y), (x, 1 - y), (1 - x, 1 - y)]

        def rows(px, py, pc):
            return out_ref.at[pl.ds((4 * px + 2 * py + pc) * m_per, m_per), :]

        def copy(k, block, to, src=None):
            return pltpu.make_async_remote_copy(
                src_ref=rows(*block) if src is None else src, dst_ref=rows(*block),
                send_sem=send_sems.at[k], recv_sem=recv_sems.at[k], device_id=to, device_id_type=MESH)

        mine = pltpu.make_async_copy(x_ref, rows(*me), local_sem)
        mine.start()
        first = [copy(0, me, sibling, src=x_ref)]
        first += [copy(1 + j, me, (*chip, c), src=x_ref) for j, chip in enumerate(chips)]
        for cp in first:
            cp.start()
        passed = [copy(4 + j, (*chip, c), sibling) for j, chip in enumerate(chips)]
        for j, chip in enumerate(chips):
            copy(1 + j, (*chip, c), me).wait_recv()
            passed[j].start()
        copy(0, sibling, me).wait_recv()
        for j, chip in enumerate(chips):
            copy(4 + j, (*chip, 1 - c), me).wait_recv()
        for cp in first + passed:
            cp.wait_send()
        mine.wait()

    return pl.pallas_call(
        body, name="allgather_small",
        out_shape=SDS((8 * m_per, n), xp.dtype),
        in_specs=[pl.BlockSpec(memory_space=pltpu.VMEM)],
        out_specs=pl.BlockSpec(memory_space=pltpu.VMEM),
        scratch_shapes=[pltpu.SemaphoreType.DMA((7,)), pltpu.SemaphoreType.DMA((7,)), pltpu.SemaphoreType.DMA],
        compiler_params=pltpu.CompilerParams(vmem_limit_bytes=VMEM_LIMIT),
    )(xp)


WCLASSES = (
    ("wg0", "row", (FB, D)), ("wu0", "row", (FB, D)), ("wd0", "row", (FB, D)),
    ("wg1", "row", (FB, D)), ("wu1", "row", (FB, D)), ("wd1", "row", (FB, D)),
    ("win", "col", (D, WIN_SH)), ("wco", "row", (D // N_CHIPS, D)), ("wao", "col", (512, D // N_CHIPS)),
    ("wo", "row", (D // N_CHIPS, D)),
)
NCLS = len(WCLASSES)


def _full_shape(kind, shp):
    if kind == "lead":
        return (N_CHIPS,) + shp
    if kind == "row":
        return (N_CHIPS * shp[0], shp[1])
    return (shp[0], N_CHIPS * shp[1])


def _shard_view(ref, kind, shp, j):
    if kind == "lead":
        return ref.at[j]
    if kind == "row":
        return ref.at[pl.ds(j * shp[0], shp[0]), :]
    return ref.at[:, pl.ds(j * shp[1], shp[1])]


def _half(ref, shp, h):
    hr = shp[0] // 2
    return ref.at[pl.ds(pl.multiple_of(h * hr, 16), hr), :]


def _gather_weights(shards):
    n = NCLS

    def body(*refs):
        ins, outs = refs[:n], refs[n:2 * n]
        send1, recv1, send2, recv2, lsem = refs[2 * n:]
        x, y, c = lax.axis_index("x"), lax.axis_index("y"), lax.axis_index("c")
        chip = 2 * x + y
        sibling = (x, y, 1 - c)

        for mc in range(N_CHIPS):
            @pl.when(chip == mc)
            def _(mc=mc):
                local = []
                for q, (_, kind, shp) in enumerate(WCLASSES):
                    cp = pltpu.make_async_copy(ins[q], _shard_view(outs[q], kind, shp, mc), lsem.at[q])
                    cp.start()
                    local.append(cp)
                sends = []
                for k in (1, 2, 3):
                    pj = mc ^ k
                    for q, (_, kind, shp) in enumerate(WCLASSES):
                        cp = pltpu.make_async_remote_copy(
                            src_ref=_half(ins[q], shp, c), dst_ref=_half(_shard_view(outs[q], kind, shp, mc), shp, c),
                            send_sem=send1.at[q * 3 + k - 1], recv_sem=recv1.at[q * 3 + k - 1],
                            device_id=(pj >> 1, pj & 1, c), device_id_type=MESH)
                        cp.start()
                        sends.append(cp)
                for k in (1, 2, 3):
                    pj = mc ^ k
                    for q, (_, kind, shp) in enumerate(WCLASSES):
                        landed = _half(_shard_view(outs[q], kind, shp, pj), shp, c)
                        pltpu.make_async_remote_copy(
                            src_ref=landed, dst_ref=landed, send_sem=send1.at[q * 3 + k - 1], recv_sem=recv1.at[q * 3 + k - 1],
                            device_id=(pj >> 1, pj & 1, c), device_id_type=MESH).wait_recv()
                        cp = pltpu.make_async_remote_copy(
                            src_ref=landed, dst_ref=landed, send_sem=send2.at[q * 3 + k - 1], recv_sem=recv2.at[q * 3 + k - 1],
                            device_id=sibling, device_id_type=MESH)
                        cp.start()
                        sends.append(cp)
                for k in (1, 2, 3):
                    pj = mc ^ k
                    for q, (_, kind, shp) in enumerate(WCLASSES):
                        other = _half(_shard_view(outs[q], kind, shp, pj), shp, 1 - c)
                        pltpu.make_async_remote_copy(
                            src_ref=other, dst_ref=other, send_sem=send2.at[q * 3 + k - 1], recv_sem=recv2.at[q * 3 + k - 1],
                            device_id=sibling, device_id_type=MESH).wait_recv()
                for cp in sends:
                    cp.wait_send()
                for cp in local:
                    cp.wait()

    anyspec = pl.BlockSpec(memory_space=pl.ANY)
    return pl.pallas_call(
        body, name="gather_weights",
        out_shape=[SDS(_full_shape(kind, shp), bf16) for _, kind, shp in WCLASSES],
        in_specs=[anyspec] * n, out_specs=[anyspec] * n,
        scratch_shapes=[pltpu.SemaphoreType.DMA((3 * n,)), pltpu.SemaphoreType.DMA((3 * n,)),
                        pltpu.SemaphoreType.DMA((3 * n,)), pltpu.SemaphoreType.DMA((3 * n,)),
                        pltpu.SemaphoreType.DMA((n,))],
    )(*shards)


def _scatter_grads(pieces):
    n = NCLS

    def body(*refs):
        ins, outs = refs[:n], refs[n:2 * n]
        send1, recv1, lsem = refs[2 * n:]
        x, y, c = lax.axis_index("x"), lax.axis_index("y"), lax.axis_index("c")
        chip = 2 * x + y

        for mc in range(N_CHIPS):
            @pl.when(chip == mc)
            def _(mc=mc):
                local, sends = [], []
                for q, (_, kind, shp) in enumerate(WCLASSES):
                    cp = pltpu.make_async_copy(_shard_view(ins[q], kind, shp, mc), outs[q].at[0], lsem.at[q])
                    cp.start()
                    local.append(cp)
                for k in (1, 2, 3):
                    pj = mc ^ k
                    for q, (_, kind, shp) in enumerate(WCLASSES):
                        cp = pltpu.make_async_remote_copy(
                            src_ref=_shard_view(ins[q], kind, shp, pj), dst_ref=outs[q].at[k],
                            send_sem=send1.at[q * 3 + k - 1], recv_sem=recv1.at[q * 3 + k - 1],
                            device_id=(pj >> 1, pj & 1, c), device_id_type=MESH)
                        cp.start()
                        sends.append(cp)
                for cp in sends:
                    cp.wait_recv()
                for cp in sends:
                    cp.wait_send()
                for cp in local:
                    cp.wait()

    anyspec = pl.BlockSpec(memory_space=pl.ANY)
    return pl.pallas_call(
        body, name="scatter_grads",
        out_shape=[SDS((N_CHIPS,) + shp, bf16) for _, _, shp in WCLASSES],
        in_specs=[anyspec] * n, out_specs=[anyspec] * n,
        scratch_shapes=[pltpu.SemaphoreType.DMA((3 * n,)), pltpu.SemaphoreType.DMA((3 * n,)), pltpu.SemaphoreType.DMA((n,))],
    )(*pieces)


def _swap_sibling(ts):
    n = len(ts)

    def body(*refs):
        ins, outs = refs[:n], refs[n:2 * n]
        send, recv = refs[2 * n:]
        x, y, c = lax.axis_index("x"), lax.axis_index("y"), lax.axis_index("c")
        cps = []
        for q in range(n):
            cp = pltpu.make_async_remote_copy(src_ref=ins[q], dst_ref=outs[q], send_sem=send.at[q], recv_sem=recv.at[q],
                                              device_id=(x, y, 1 - c), device_id_type=MESH)
            cp.start()
            cps.append(cp)
        for cp in cps:
            cp.wait_recv()
        for cp in cps:
            cp.wait_send()

    anyspec = pl.BlockSpec(memory_space=pl.ANY)
    return pl.pallas_call(
        body, name="swap_sibling",
        out_shape=[SDS(t.shape, t.dtype) for t in ts],
        in_specs=[anyspec] * n, out_specs=[anyspec] * n,
        scratch_shapes=[pltpu.SemaphoreType.DMA((n,)), pltpu.SemaphoreType.DMA((n,))],
    )(*ts)


HBM_SPEC = pl.BlockSpec(memory_space=pltpu.HBM)
SEM_SPEC = pl.BlockSpec(memory_space=pltpu.SEMAPHORE)
ANY_SPEC = pl.BlockSpec(memory_space=pl.ANY)
EFFECT = pltpu.SideEffectType.DATAFLOW_SIDE_EFFECTING
N_COPIES = 3 * NCLS


def _in_hbm(a):
    return pltpu.with_memory_space_constraint(a, pltpu.HBM)


def _chip_index():
    return 2 * lax.axis_index("x") + lax.axis_index("y")


def _place_own(shards):
    n = NCLS

    def body(*refs):
        ins, outs, lsem = refs[:n], refs[n:2 * n], refs[2 * n]
        chip = _chip_index()
        for mc in range(N_CHIPS):
            @pl.when(chip == mc)
            def _(mc=mc):
                cps = [pltpu.make_async_copy(ins[q], _shard_view(outs[q], kind, shp, mc), lsem.at[q])
                       for q, (_, kind, shp) in enumerate(WCLASSES)]
                for cp in cps:
                    cp.start()
                for cp in cps:
                    cp.wait()

    return pl.pallas_call(
        body, name="place_own",
        out_shape=[SDS(_full_shape(kind, shp), bf16) for _, kind, shp in WCLASSES],
        in_specs=[ANY_SPEC] * n, out_specs=[ANY_SPEC] * n,
        scratch_shapes=[pltpu.SemaphoreType.DMA((n,))],
    )(*shards)


def _take_own(pieces):
    n = NCLS

    def body(*refs):
        ins, outs, lsem = refs[:n], refs[n:2 * n], refs[2 * n]
        chip = _chip_index()
        for mc in range(N_CHIPS):
            @pl.when(chip == mc)
            def _(mc=mc):
                cps = [pltpu.make_async_copy(_shard_view(ins[q], kind, shp, mc), outs[q].at[0], lsem.at[q])
                       for q, (_, kind, shp) in enumerate(WCLASSES)]
                for cp in cps:
                    cp.start()
                for cp in cps:
                    cp.wait()

    return pl.pallas_call(
        body, name="take_own",
        out_shape=[SDS((N_CHIPS,) + shp, bf16) for _, _, shp in WCLASSES],
        in_specs=[ANY_SPEC] * n, out_specs=[ANY_SPEC] * n,
        scratch_shapes=[pltpu.SemaphoreType.DMA((n,))],
    )(*pieces)


def _split_start(name, srcs, dsts, after, src_view, dst_view):
    n = NCLS

    def body(*refs):
        src, dst = refs[:n], refs[n:2 * n]
        send, recv = refs[2 * n + 1], refs[2 * n + 2]
        token = refs[-1]
        c = lax.axis_index("c")
        chip = _chip_index()
        for mc in range(N_CHIPS):
            @pl.when(chip == mc)
            def _(mc=mc):
                for k in (1, 2, 3):
                    pj = mc ^ k
                    for q in range(n):
                        pltpu.make_async_remote_copy(
                            src_ref=src_view(src[q], q, mc, pj), dst_ref=dst_view(dst[q], q, mc, k),
                            send_sem=send.at[q * 3 + k - 1], recv_sem=recv.at[q * 3 + k - 1],
                            device_id=(pj >> 1, pj & 1, c), device_id_type=MESH).start()
        token[...] = jnp.zeros_like(token)

    return pl.pallas_call(
        body, name=name,
        out_shape=(pltpu.SemaphoreType.DMA((N_COPIES,)), pltpu.SemaphoreType.DMA((N_COPIES,)),
                   *[pltpu.HBM(a.shape, a.dtype) for a in srcs], *[pltpu.HBM(a.shape, a.dtype) for a in dsts], SDS((8, 128), f32)),
        in_specs=[HBM_SPEC] * (2 * n) + [ANY_SPEC],
        out_specs=(SEM_SPEC, SEM_SPEC, *([HBM_SPEC] * (2 * n)), pl.BlockSpec(memory_space=pltpu.VMEM)),
        input_output_aliases={i: 2 + i for i in range(2 * n)},
        compiler_params=pltpu.CompilerParams(has_side_effects=EFFECT),
    )(*[_in_hbm(a) for a in srcs], *[_in_hbm(a) for a in dsts], after)


def _split_wait(name, started, after, arrival_view):
    n = NCLS
    send, recv = started[0], started[1]
    srcs, dsts = started[2:2 + n], started[2 + n:2 + 2 * n]

    def body(*refs):
        src, dst = refs[:n], refs[n:2 * n]
        send_sem, recv_sem = refs[2 * n], refs[2 * n + 1]
        x, y, c = lax.axis_index("x"), lax.axis_index("y"), lax.axis_index("c")
        for k in (1, 2, 3):
            for q in range(n):
                arrival = arrival_view(dst[q], q, k)
                cp = pltpu.make_async_remote_copy(
                    src_ref=arrival, dst_ref=arrival, send_sem=send_sem.at[q * 3 + k - 1], recv_sem=recv_sem.at[q * 3 + k - 1],
                    device_id=(x, y, 1 - c), device_id_type=MESH)
                cp.wait_send()
                cp.wait_recv()

    out = pl.pallas_call(
        body, name=name,
        out_shape=(*[pltpu.HBM(a.shape, a.dtype) for a in srcs], *[pltpu.HBM(a.shape, a.dtype) for a in dsts]),
        in_specs=[HBM_SPEC] * (2 * n) + [SEM_SPEC, SEM_SPEC, ANY_SPEC],
        out_specs=tuple([HBM_SPEC] * (2 * n)),
        input_output_aliases={i: i for i in range(2 * n)},
        compiler_params=pltpu.CompilerParams(has_side_effects=EFFECT),
    )(*srcs, *dsts, send, recv, after)
    return out[n:]


def _cls(q):
    return WCLASSES[q][1], WCLASSES[q][2]


def _gather_start(shards, after):
    fulls = _place_own(shards)
    return _split_start("gather_start", shards, fulls, after,
                        lambda ref, q, mc, pj: ref,
                        lambda ref, q, mc, k: _shard_view(ref, *_cls(q), mc))


def _gather_wait(started, after):
    return _split_wait("gather_wait", started, after, lambda ref, q, k: _shard_view(ref, *_cls(q), 0))


def _scatter_start(pieces, after):
    lands = _take_own(pieces)
    return _split_start("scatter_start", pieces, lands, after,
                        lambda ref, q, mc, pj: _shard_view(ref, *_cls(q), pj),
                        lambda ref, q, mc, k: ref.at[k])


def _scatter_wait(started, after):
    return _split_wait("scatter_wait", started, after, lambda ref, q, k: ref.at[k])


GROUPS = {"A": (0, 1, 2), "B": (6,), "C": (7, 8, 9), "D": (3, 4, 5)}


def _own_spec(kind, shp, tr):
    R, C = shp
    if kind == "lead":
        return pl.BlockSpec((None, tr, C), lambda i, chip: (chip[0], i, 0))
    if kind == "row":
        return pl.BlockSpec((tr, C), lambda i, chip: (chip[0] * (R // tr) + i, 0))
    return pl.BlockSpec((tr, C), lambda i, chip: (i, chip[0]))


def _cast_place(shards, kind, shp, chip_arr, after):
    n = len(shards)
    R, C = shp
    tr = _row_tile(R, C)

    def body(chip_ref, *refs):
        for q in range(n):
            refs[n + 1 + q][...] = refs[q][...].astype(bf16)

    def in_spec(lead):
        return pl.BlockSpec((None,) * len(lead) + (tr, C), lambda i, chip: (*lead, i, 0))

    return pl.pallas_call(
        body, name="cast_place",
        grid_spec=pltpu.PrefetchScalarGridSpec(
            num_scalar_prefetch=1, grid=(R // tr,),
            in_specs=[in_spec(lead) for _, lead in shards] + [ANY_SPEC],
            out_specs=[_own_spec(kind, shp, tr)] * n),
        out_shape=[SDS(_full_shape(kind, shp), bf16)] * n,
        compiler_params=_cp("arbitrary"),
    )(chip_arr, *[a for a, _ in shards], after)


def _sum_own_slots(piece, land, kind, shp, chip_arr):
    R, C = shp
    tr = _row_tile(R, C)

    def body(chip_ref, p_ref, l_ref, t_ref):
        t = p_ref[...].astype(f32)
        for k in range(N_CHIPS - 1):
            t = t + l_ref[k].astype(f32)
        t_ref[...] = t.astype(bf16)

    return pl.pallas_call(
        body, name="sum_own_slots",
        grid_spec=pltpu.PrefetchScalarGridSpec(
            num_scalar_prefetch=1, grid=(R // tr,),
            in_specs=[_own_spec(kind, shp, tr), pl.BlockSpec((N_CHIPS - 1, tr, C), lambda i, chip: (0, i, 0))],
            out_specs=pl.BlockSpec((tr, C), lambda i, chip: (i, 0))),
        out_shape=SDS((R, C), bf16),
        compiler_params=_cp("arbitrary"),
    )(chip_arr, piece, land)


def _xfer_start(name, arrays, ng, after, src_view, dst_view):
    na = len(arrays)

    def body(*refs):
        arr = refs[:na]
        send, recv, token = refs[na + 1], refs[na + 2], refs[-1]
        c = lax.axis_index("c")
        chip = _chip_index()
        for mc in range(N_CHIPS):
            @pl.when(chip == mc)
            def _(mc=mc):
                for k in (1, 2, 3):
                    pj = mc ^ k
                    for i in range(ng):
                        pltpu.make_async_remote_copy(
                            src_ref=src_view(arr, i, mc, pj), dst_ref=dst_view(arr, i, mc, k),
                            send_sem=send.at[i * 3 + k - 1], recv_sem=recv.at[i * 3 + k - 1],
                            device_id=(pj >> 1, pj & 1, c), device_id_type=MESH).start()
        token[...] = jnp.zeros_like(token)

    return pl.pallas_call(
        body, name=name,
        out_shape=(pltpu.SemaphoreType.DMA((3 * ng,)), pltpu.SemaphoreType.DMA((3 * ng,)),
                   *[pltpu.HBM(a.shape, a.dtype) for a in arrays], SDS((8, 128), f32)),
        in_specs=[HBM_SPEC] * na + [ANY_SPEC],
        out_specs=(SEM_SPEC, SEM_SPEC, *([HBM_SPEC] * na), pl.BlockSpec(memory_space=pltpu.VMEM)),
        input_output_aliases={i: 2 + i for i in range(na)},
        compiler_params=pltpu.CompilerParams(has_side_effects=EFFECT),
    )(*[_in_hbm(a) for a in arrays], after)


def _xfer_wait(name, started, ng, after, arrival_view):
    send, recv = started[0], started[1]
    arrays = started[2:-1]
    na = len(arrays)

    def body(*refs):
        arr = refs[:na]
        send_sem, recv_sem = refs[na], refs[na + 1]
        x, y, c = lax.axis_index("x"), lax.axis_index("y"), lax.axis_index("c")
        for k in (1, 2, 3):
            for i in range(ng):
                arrival = arrival_view(arr, i)
                cp = pltpu.make_async_remote_copy(
                    src_ref=arrival, dst_ref=arrival, send_sem=send_sem.at[i * 3 + k - 1], recv_sem=recv_sem.at[i * 3 + k - 1],
                    device_id=(x, y, 1 - c), device_id_type=MESH)
                cp.wait_send()
                cp.wait_recv()

    return pl.pallas_call(
        body, name=name,
        out_shape=tuple(pltpu.HBM(a.shape, a.dtype) for a in arrays),
        in_specs=[HBM_SPEC] * na + [SEM_SPEC, SEM_SPEC, ANY_SPEC],
        out_specs=tuple([HBM_SPEC] * na),
        input_output_aliases={i: i for i in range(na)},
        compiler_params=pltpu.CompilerParams(has_side_effects=EFFECT),
    )(*arrays, send, recv, after)


def _gather_group_cast(group, shards_f32, chip_arr, after):
    fulls = [None] * len(group)
    by_shape = {}
    for i, q in enumerate(group):
        by_shape.setdefault(_cls(q), []).append(i)
    for (kind, shp), idx in by_shape.items():
        for i, f in zip(idx, _cast_place([shards_f32[i] for i in idx], kind, shp, chip_arr, after)):
            fulls[i] = f
    return fulls


def _gather_group_start(tag, group, fulls, after):
    def view(arr, i, mc, _):
        kind, shp = _cls(group[i])
        return _half(_shard_view(arr[i], kind, shp, mc), shp, lax.axis_index("c"))
    return _xfer_start("gather_start_" + tag, fulls, len(group), after, view, view)


def _gather_group_forward(tag, group, started, after):
    ng = len(group)
    send1, recv1 = started[0], started[1]
    arrays = started[2:-1]
    na = len(arrays)

    def body(*refs):
        arr = refs[:na]
        send_in, recv_in = refs[na], refs[na + 1]
        send2, recv2, token = refs[na + 3], refs[na + 4], refs[-1]
        x, y, c = lax.axis_index("x"), lax.axis_index("y"), lax.axis_index("c")
        chip = _chip_index()
        for mc in range(N_CHIPS):
            @pl.when(chip == mc)
            def _(mc=mc):
                for k in (1, 2, 3):
                    pj = mc ^ k
                    for i in range(ng):
                        kind, shp = _cls(group[i])
                        landed = _half(_shard_view(arr[i], kind, shp, pj), shp, c)
                        pltpu.make_async_remote_copy(
                            src_ref=landed, dst_ref=landed, send_sem=send_in.at[i * 3 + k - 1], recv_sem=recv_in.at[i * 3 + k - 1],
                            device_id=(pj >> 1, pj & 1, c), device_id_type=MESH).wait_recv()
                        pltpu.make_async_remote_copy(
                            src_ref=landed, dst_ref=landed, send_sem=send2.at[i * 3 + k - 1], recv_sem=recv2.at[i * 3 + k - 1],
                            device_id=(x, y, 1 - c), device_id_type=MESH).start()
        token[...] = jnp.zeros_like(token)

    return pl.pallas_call(
        body, name="gather_forward_" + tag,
        out_shape=(pltpu.SemaphoreType.DMA((3 * ng,)), pltpu.SemaphoreType.DMA((3 * ng,)),
                   *[pltpu.HBM(a.shape, a.dtype) for a in arrays], SDS((8, 128), f32)),
        in_specs=[HBM_SPEC] * na + [SEM_SPEC, SEM_SPEC, ANY_SPEC],
        out_specs=(SEM_SPEC, SEM_SPEC, *([HBM_SPEC] * na), pl.BlockSpec(memory_space=pltpu.VMEM)),
        input_output_aliases={i: 2 + i for i in range(na)},
        compiler_params=pltpu.CompilerParams(has_side_effects=EFFECT),
    )(*arrays, send1, recv1, after)


def _gather_group_wait(tag, group, send1, forwarded, after):
    ng = len(group)
    send2, recv2 = forwarded[0], forwarded[1]
    arrays = forwarded[2:-1]
    na = len(arrays)

    def body(*refs):
        arr = refs[:na]
        s1, s2, r2 = refs[na], refs[na + 1], refs[na + 2]
        x, y, c = lax.axis_index("x"), lax.axis_index("y"), lax.axis_index("c")
        for k in (1, 2, 3):
            for i in range(ng):
                kind, shp = _cls(group[i])
                half = _half(_shard_view(arr[i], kind, shp, 0), shp, 0)
                pltpu.make_async_remote_copy(src_ref=half, dst_ref=half, send_sem=s1.at[i * 3 + k - 1], recv_sem=r2.at[i * 3 + k - 1],
                                             device_id=(x, y, 1 - c), device_id_type=MESH).wait_send()
                cp = pltpu.make_async_remote_copy(src_ref=half, dst_ref=half, send_sem=s2.at[i * 3 + k - 1], recv_sem=r2.at[i * 3 + k - 1],
                                                  device_id=(x, y, 1 - c), device_id_type=MESH)
                cp.wait_send()
                cp.wait_recv()

    return pl.pallas_call(
        body, name="gather_wait_" + tag,
        out_shape=tuple(pltpu.HBM(a.shape, a.dtype) for a in arrays),
        in_specs=[HBM_SPEC] * na + [SEM_SPEC, SEM_SPEC, SEM_SPEC, ANY_SPEC],
        out_specs=tuple([HBM_SPEC] * na),
        input_output_aliases={i: i for i in range(na)},
        compiler_params=pltpu.CompilerParams(has_side_effects=EFFECT),
    )(*arrays, send1, send2, recv2, after)


def _scatter_group_start(tag, group, pieces, after):
    ng = len(group)
    lands = [lax.empty((N_CHIPS - 1,) + _cls(q)[1], bf16) for q in group]
    return _xfer_start("scatter_start_" + tag, list(pieces) + lands, ng, after,
                       lambda arr, i, mc, pj: _shard_view(arr[i], *_cls(group[i]), pj),
                       lambda arr, i, mc, k: arr[ng + i].at[k - 1])


def _scatter_group_wait(tag, group, started, after):
    ng = len(group)
    out = _xfer_wait("scatter_wait_" + tag, started, ng, after, lambda arr, i: arr[ng + i].at[0])
    return out[:ng], out[ng:]


def _mod_shards(c_all, ada_w, ada_b_sh):
    tn = ADA_SH // 3

    def body(c_ref, w_ref, b_ref, o_ref, cs_ref):
        cv = c_ref[...]
        cs = cv * _sigmoid(cv)
        cs_ref[...] = cs
        o_ref[...] = _dot(cs.astype(bf16), w_ref[...].astype(bf16)) + b_ref[...]

    return pl.pallas_call(
        body, name="mod_shards", grid=(DEPTH, 3),
        out_shape=[SDS((DEPTH, 8, ADA_SH), f32), SDS((8, D), f32)],
        in_specs=[pl.BlockSpec((8, D), lambda l, t: (0, 0)),
                  pl.BlockSpec((None, D, tn), lambda l, t: (l, 0, t)),
                  pl.BlockSpec((None, 1, tn), lambda l, t: (l, 0, t))],
        out_specs=[pl.BlockSpec((None, 8, tn), lambda l, t: (l, 0, t)), pl.BlockSpec((8, D), lambda l, t: (0, 0))],
        compiler_params=_cp("arbitrary", "arbitrary"),
    )(c_all, ada_w, ada_b_sh.reshape(DEPTH, 1, ADA_SH))


def _t5_bucket(dist):
    exact = NUM_BUCKETS // 2
    dd = np.maximum(dist, 1).astype(np.float32)
    large = exact + (np.log(dd / exact) / np.log(MAX_DISTANCE / exact) * (NUM_BUCKETS - exact)).astype(np.int32)
    large = np.minimum(large, NUM_BUCKETS - 1)
    return np.where(dist < exact, dist, large).astype(np.int32)


def _bucket_table():
    i = np.arange(BLK)[:, None]
    j = np.arange(2 * BLK)[None, :]
    rel = i - j + BLK
    return np.stack([_t5_bucket(np.maximum(rel, 0) * d) for d in DILATIONS]).astype(np.int32)


def _band():
    rel = lax.broadcasted_iota(jnp.int32, (BLK, 2 * BLK), 0) - lax.broadcasted_iota(jnp.int32, (BLK, 2 * BLK), 1) + BLK
    return (rel >= 0) & (rel <= BLK)


def _bias_blocks(rel_bias, buckets):
    def body(tab_ref, bk_ref, o_ref):
        h = pl.program_id(0)
        bk = bk_ref[...]
        acc = jnp.zeros((BLK, 2 * BLK), f32)
        for b in range(NUM_BUCKETS):
            acc = jnp.where(bk == b, tab_ref[b, h], acc)
        o_ref[...] = jnp.where(_band(), acc, NEG)

    return pl.pallas_call(
        body, name="bias_blocks", grid=(24,),
        out_shape=SDS((24, BLK, 2 * BLK), f32),
        in_specs=[pl.BlockSpec(memory_space=pltpu.SMEM), pl.BlockSpec((None, BLK, 2 * BLK), lambda h: (h // 8, 0, 0))],
        out_specs=pl.BlockSpec((None, BLK, 2 * BLK), lambda h: (h, 0, 0)),
        compiler_params=_cp("arbitrary"),
    )(rel_bias, buckets)


def _bias_grad(dsaccs, buckets):
    nl = len(dsaccs)

    def body(*refs):
        bk = refs[nl][...]
        tot = refs[0][...]
        for r in refs[1:nl]:
            tot = tot + r[...]
        lane = lax.broadcasted_iota(jnp.int32, (1, 128), 1)
        row = jnp.zeros((1, 128), f32)
        for b in range(NUM_BUCKETS):
            row = jnp.where(lane == b, jnp.sum(jnp.where(bk == b, tot, 0.0)), row)
        refs[nl + 1][...] = row

    return pl.pallas_call(
        body, name="bias_grad", grid=(24,),
        out_shape=SDS((24, 1, 128), f32),
        in_specs=[pl.BlockSpec((None, BLK, 2 * BLK), lambda h: (h, 0, 0))] * nl
                 + [pl.BlockSpec((None, BLK, 2 * BLK), lambda h: (h // 8, 0, 0))],
        out_specs=pl.BlockSpec((None, 1, 128), lambda h: (h, 0, 0)),
        compiler_params=_cp("arbitrary"),
    )(*dsaccs, buckets)


def _ffn_fwd(x, mod9, g3, wg, wu, wd, sub):
    S = x.shape[0]

    def body(x_ref, mod_ref, g_ref, wg_ref, wu_ref, wd_ref, xo_ref, h_ref, ga_ref, sa_ref, hid_ref, y_ref, acc):
        j = pl.program_id(1)

        @pl.when(j == 0)
        def _():
            h, _, _ = _norm_fwd(x_ref[...], g_ref[sub:sub + 1, :], mod_ref[3 * sub:3 * sub + 1, :], mod_ref[3 * sub + 1:3 * sub + 2, :])
            h_ref[...] = h.astype(bf16)
            acc[...] = jnp.zeros_like(acc)

        h = h_ref[...]
        a = _dot_nt(h, wg_ref[...])
        u = _dot_nt(h, wu_ref[...])
        sg = _sigmoid(a)
        sil = a * sg
        ga_ref[...] = (u * (sg * (1.0 + a * (1.0 - sg)))).astype(bf16)
        sa_ref[...] = sil.astype(bf16)
        hid_ref[...] = (sil * u).astype(bf16)
        acc[...] += _dot(hid_ref[...], wd_ref[...])

        @pl.when(j == N_CHIPS - 1)
        def _():
            y = acc[...]
            y_ref[...] = y.astype(bf16)
            xo_ref[...] = x_ref[...] + 0.5 * mod_ref[3 * sub + 2:3 * sub + 3, :] * y

    row = pl.BlockSpec((TMF, D), lambda i, j: (i, 0))
    hidb = pl.BlockSpec((None, TMF, FB), lambda i, j: (j, i, 0))
    hids = SDS((N_CHIPS, S, FB), bf16)
    return pl.pallas_call(
        body, name="ffn_fwd", grid=(S // TMF, N_CHIPS),
        out_shape=[SDS((S, D), f32), SDS((S, D), bf16), hids, hids, hids, SDS((S, D), bf16)],
        in_specs=[row, pl.BlockSpec((9, D), lambda i, j: (0, 0)), pl.BlockSpec((3, D), lambda i, j: (0, 0)),
                  pl.BlockSpec((FB, D), lambda i, j: (j, 0)), pl.BlockSpec((FB, D), lambda i, j: (j, 0)),
                  pl.BlockSpec((FB, D), lambda i, j: (j, 0))],
        out_specs=[row, row, hidb, hidb, hidb, row],
        scratch_shapes=[pltpu.VMEM((TMF, D), f32)],
        compiler_params=_cp("arbitrary", "arbitrary"),
    )(x, mod9, g3, wg, wu, wd)


def _ffn_bwd1(dxo, x, mod9, g3, y, ga, sa, wg, wu, wd, sub):
    S = x.shape[0]

    def body(dxo_ref, x_ref, mod_ref, g_ref, y_ref, ga_ref, sa_ref, wg_ref, wu_ref, wd_ref,
             dxi_ref, da_ref, du_ref, dy_ref, sm_ref, acc):
        i, j = pl.program_id(0), pl.program_id(1)
        gate = mod_ref[3 * sub + 2:3 * sub + 3, :]

        @pl.when((i == 0) & (j == 0))
        def _():
            sm_ref[...] = jnp.zeros_like(sm_ref)

        @pl.when(j == 0)
        def _():
            dxo_v = dxo_ref[...]
            dy_ref[...] = (0.5 * gate * dxo_v).astype(bf16)
            sm_ref[2:3, :] += jnp.sum(0.5 * y_ref[...].astype(f32) * dxo_v, axis=0, keepdims=True)
            acc[...] = jnp.zeros_like(acc)

        part = None
        for s in range(SH_STEP):
            dhid = _dot_nt(dy_ref[...], wd_ref[s * FB:(s + 1) * FB, :])
            da = (dhid * ga_ref[s].astype(f32)).astype(bf16)
            du = (dhid * sa_ref[s].astype(f32)).astype(bf16)
            da_ref[s] = da
            du_ref[s] = du
            t = _dot(da, wg_ref[s * FB:(s + 1) * FB, :]) + _dot(du, wu_ref[s * FB:(s + 1) * FB, :])
            part = t if part is None else part + t
        acc[...] += part

        @pl.when(j == N_CHIPS // SH_STEP - 1)
        def _():
            g = g_ref[sub:sub + 1, :]
            scale = mod_ref[3 * sub + 1:3 * sub + 2, :]
            _, xhat, rstd = _norm_fwd(x_ref[...], g, mod_ref[3 * sub:3 * sub + 1, :], scale)
            dx, dshift, dscale, dg = _norm_bwd(acc[...], xhat, rstd, g, scale)
            dxi_ref[...] = dxo_ref[...] + dx
            sm_ref[0:1, :] += dshift
            sm_ref[1:2, :] += dscale
            sm_ref[3:4, :] += dg

    row = pl.BlockSpec((TM, D), lambda i, j: (i, 0))
    hidb = pl.BlockSpec((SH_STEP, TM, FB), lambda i, j: (j, i, 0))
    wcol = pl.BlockSpec((SH_STEP * FB, D), lambda i, j: (j, 0))
    return pl.pallas_call(
        body, name="ffn_bwd1", grid=(S // TM, N_CHIPS // SH_STEP),
        out_shape=[SDS((S, D), f32), SDS((N_CHIPS, S, FB), bf16), SDS((N_CHIPS, S, FB), bf16), SDS((S, D), bf16), SDS((8, D), f32)],
        in_specs=[row, row, pl.BlockSpec((9, D), lambda i, j: (0, 0)), pl.BlockSpec((3, D), lambda i, j: (0, 0)), row,
                  hidb, hidb, wcol, wcol, pl.BlockSpec((SH_STEP * FB, D), lambda i, j: (j, 0))],
        out_specs=[row, hidb, hidb, row, pl.BlockSpec((8, D), lambda i, j: (0, 0))],
        scratch_shapes=[pltpu.VMEM((TM, D), f32)],
        compiler_params=_cp("arbitrary", "arbitrary"),
    )(dxo, x, mod9, g3, y, ga, sa, wg, wu, wd)


def _ffn_bwd2(h, da, du, hid, dy):
    S = h.shape[0]
    ni = S // TMW

    def body(h_ref, da_ref, du_ref, hid_ref, dy_ref, dwg_ref, dwu_ref, dwd_ref, ag, au, ad):
        i = pl.program_id(1)

        @pl.when(i == 0)
        def _():
            ag[...] = jnp.zeros_like(ag)
            au[...] = jnp.zeros_like(au)
            ad[...] = jnp.zeros_like(ad)

        hv = h_ref[...]
        ag[...] += _dot_tn(da_ref[...], hv)
        au[...] += _dot_tn(du_ref[...], hv)
        ad[...] += _dot_tn(hid_ref[...], dy_ref[...])

        @pl.when(i == ni - 1)
        def _():
            dwg_ref[...] = ag[...].astype(bf16)
            dwu_ref[...] = au[...].astype(bf16)
            dwd_ref[...] = ad[...].astype(bf16)

    row = pl.BlockSpec((TMW, D), lambda j, i: (i, 0))
    hidb = pl.BlockSpec((None, TMW, FB), lambda j, i: (j, i, 0))
    wrow = pl.BlockSpec((FB, D), lambda j, i: (j, 0))
    return pl.pallas_call(
        body, name="ffn_bwd2", grid=(N_CHIPS, ni),
        out_shape=[SDS((N_CHIPS * FB, D), bf16)] * 3,
        in_specs=[row, hidb, hidb, hidb, row],
        out_specs=[wrow, wrow, wrow],
        scratch_shapes=[pltpu.VMEM((FB, D), f32)] * 3,
        compiler_params=_cp("arbitrary", "arbitrary"),
    )(h, da, du, hid, dy)


def _mix_qkv(x, mod9, g3, win):
    S = x.shape[0]

    def body(x_ref, mod_ref, g_ref, w_ref, h_ref, o_ref):
        @pl.when(pl.program_id(1) == 0)
        def _():
            h, _, _ = _norm_fwd(x_ref[...], g_ref[1:2, :], mod_ref[3:4, :], mod_ref[4:5, :])
            h_ref[...] = h.astype(bf16)

        o_ref[...] = _dot(h_ref[...], w_ref[...]).astype(bf16)

    row = pl.BlockSpec((TMP, D), lambda i, j: (i, 0))
    return pl.pallas_call(
        body, name="mix_qkv", grid=(S // TMP, QKV_W // CBQ),
        out_shape=[SDS((S, D), bf16), SDS((S, QKV_W), bf16)],
        in_specs=[row, pl.BlockSpec((9, D), lambda i, j: (0, 0)), pl.BlockSpec((3, D), lambda i, j: (0, 0)),
                  pl.BlockSpec((D, CBQ), lambda i, j: (0, j))],
        out_specs=[row, pl.BlockSpec((TMP, CBQ), lambda i, j: (i, j))],
        compiler_params=_cp("arbitrary", "arbitrary"),
    )(x, mod9, g3, win)


def _mix_rest(h, win):
    S = h.shape[0]
    off = QKV_W // CB

    def body(h_ref, w_ref, o_ref):
        o_ref[...] = _dot(h_ref[...], w_ref[...]).astype(bf16)

    return pl.pallas_call(
        body, name="mix_rest", grid=(S // TMP, REST_W // CB),
        out_shape=SDS((S, REST_W), bf16),
        in_specs=[pl.BlockSpec((TMP, D), lambda i, j: (i, 0)), pl.BlockSpec((D, CB), lambda i, j: (0, off + j))],
        out_specs=pl.BlockSpec((TMP, CB), lambda i, j: (i, j)),
        compiler_params=_cp("arbitrary", "arbitrary"),
    )(h, win)


def _widen(srcs, dsts):
    for src, dst in zip(srcs, dsts):
        dst[...] = src[...].astype(f32)


def _qkv_scratch(R, Rb):
    return [pltpu.VMEM((R, 128), f32), pltpu.VMEM((R, 128), f32), pltpu.VMEM((Rb, 128), f32),
            pltpu.VMEM((R, 128), f32), pltpu.VMEM((Rb, 128), f32)]


def _attn_fwd(qkv, bias, g):
    S = qkv.shape[0]
    d = DILATIONS[g]
    nq = Q_BLOCKS[g]
    Rb = BLK * d
    R = Rb * nq
    nb = S // R
    qb, kb, vb = 4 * g, 12 + 4 * g, 24 + 4 * g

    def body(q_in, kc_in, kp_in, vc_in, vp_in, b_ref, o_ref, l_ref, q_ref, kc_ref, kp_ref, vc_ref, vp_ref):
        n = pl.program_id(1)
        col = lax.broadcasted_iota(jnp.int32, (BLK, 2 * BLK), 1)
        first = jnp.where((col < BLK) & (n == 0), NEG, 0.0)
        head0 = lax.broadcasted_iota(jnp.int32, (1, 2 * HD), 1) < HD
        _widen((q_in, kc_in, kp_in, vc_in, vp_in), (q_ref, kc_ref, kp_ref, vc_ref, vp_ref))

        def one(b, r):
            sl = pl.ds(b * Rb + r, BLK, stride=d)
            q = q_ref[sl, :]
            if b == 0:
                kp, vp = kp_ref[pl.ds(r, BLK, stride=d), :], vp_ref[pl.ds(r, BLK, stride=d), :]
            else:
                before = pl.ds((b - 1) * Rb + r, BLK, stride=d)
                kp, vp = kc_ref[before, :], vc_ref[before, :]
            kk = jnp.concatenate([kp, kc_ref[sl, :]], axis=0).astype(bf16)
            vv = jnp.concatenate([vp, vc_ref[sl, :]], axis=0).astype(bf16)
            os, ls = [], []
            for hh in range(2):
                qm = jnp.where(head0 if hh == 0 else ~head0, q, 0.0).astype(bf16)
                s = _dot_nt(qm, kk) * SCALE + b_ref[hh]
                if b == 0:
                    s = s + first
                m = jnp.max(s, axis=-1, keepdims=True)
                p = jnp.exp(s - m)
                l = jnp.sum(p, axis=-1, keepdims=True)
                os.append(_dot(p.astype(bf16), vv) / l)
                ls.append(m + jnp.log(l))
            o_ref[sl, :] = jnp.where(head0, os[0], os[1])
            l_ref[sl, :] = jnp.where(head0, ls[0], ls[1])

        for b in range(nq):
            if d == 1:
                one(b, 0)
            else:
                lax.fori_loop(0, d, lambda r, carry, b=b: (one(b, r), carry)[1], 0, unroll=4)

    def blk(cb, prev):
        if prev:
            return pl.BlockSpec((Rb, 128), lambda hp, n: (jnp.maximum(n * nq - 1, 0), cb + hp))
        return pl.BlockSpec((R, 128), lambda hp, n: (n, cb + hp))

    outb = pl.BlockSpec((R, 128), lambda hp, n: (n, hp))
    return pl.pallas_call(
        body, name=f"attn_fwd_d{d}", grid=(4, nb),
        out_shape=[SDS((S, 512), f32), SDS((S, 512), f32)],
        in_specs=[blk(qb, False), blk(kb, False), blk(kb, True), blk(vb, False), blk(vb, True),
                  pl.BlockSpec((2, BLK, 2 * BLK), lambda hp, n: (4 * g + hp, 0, 0))],
        out_specs=[outb, outb],
        scratch_shapes=_qkv_scratch(R, Rb),
        compiler_params=_cp("arbitrary", "arbitrary"),
    )(qkv, qkv, qkv, qkv, qkv, bias)


def _attn_bwd(qkv, do, o, lse, bias, dq_all, dk_all, dv_all, g):
    S = qkv.shape[0]
    d = DILATIONS[g]
    nq = Q_BLOCKS[g]
    Rb = BLK * d
    R = Rb * nq
    nb = S // R
    qb, kb, vb = 4 * g, 12 + 4 * g, 24 + 4 * g

    def body(q_in, kc_in, kp_in, vc_in, vp_in, do_ref, o_ref, l_ref, b_ref, dqi, dki, dvi,
             dq_out, dk_out, dv_out, ds_ref, ck, cv, tk, tv, dq_ref, q_ref, kc_ref, kp_ref, vc_ref, vp_ref):
        n = pl.program_id(1)
        col = lax.broadcasted_iota(jnp.int32, (BLK, 2 * BLK), 1)
        first = jnp.where((col < BLK) & (n == 0), NEG, 0.0)

        @pl.when(n == 0)
        def _():
            ck[...] = jnp.zeros_like(ck)
            cv[...] = jnp.zeros_like(cv)
            ds_ref[...] = jnp.zeros_like(ds_ref)

        @pl.when(n < nb)
        def _():
            head0 = lax.broadcasted_iota(jnp.int32, (1, 2 * HD), 1) < HD
            _widen((q_in, kc_in, kp_in, vc_in, vp_in), (q_ref, kc_ref, kp_ref, vc_ref, vp_ref))

            def one(b, r):
                sl = pl.ds(b * Rb + r, BLK, stride=d)
                before = pl.ds((max(b, 1) - 1) * Rb + r, BLK, stride=d)
                q = q_ref[sl, :]
                if b == 0:
                    kp, vp = kp_ref[pl.ds(r, BLK, stride=d), :], vp_ref[pl.ds(r, BLK, stride=d), :]
                else:
                    kp, vp = kc_ref[before, :], vc_ref[before, :]
                kk = jnp.concatenate([kp, kc_ref[sl, :]], axis=0).astype(bf16)
                vv = jnp.concatenate([vp, vc_ref[sl, :]], axis=0).astype(bf16)
                dov, lv = do_ref[sl, :], l_ref[sl, :]
                prod = dov * o_ref[sl, :]
                qb, dob = q.astype(bf16), dov.astype(bf16)
                dqs, dks, dvs = [], [], []
                for hh in range(2):
                    msk = head0 if hh == 0 else ~head0
                    qm = jnp.where(msk, q, 0.0).astype(bf16)
                    dom = jnp.where(msk, dov, 0.0).astype(bf16)
                    dsum = jnp.sum(jnp.where(msk, prod, 0.0), axis=-1, keepdims=True)
                    s = _dot_nt(qm, kk) * SCALE + b_ref[hh]
                    if b == 0:
                        s = s + first
                    p = jnp.exp(s - lv[:, HD * hh:HD * hh + 1])
                    ds = p * (_dot_nt(dom, vv) - dsum)
                    ds_ref[hh] += ds
                    dsb = ds.astype(bf16)
                    dqs.append(_dot(dsb, kk) * SCALE)
                    dks.append(_dot_tn(dsb, qb) * SCALE)
                    dvs.append(_dot_tn(p.astype(bf16), dob))
                dq_ref[sl, :] = jnp.where(head0, dqs[0], dqs[1])
                dk = jnp.where(head0, dks[0], dks[1])
                dv = jnp.where(head0, dvs[0], dvs[1])
                tk[sl, :] = dk[BLK:]
                tv[sl, :] = dv[BLK:]
                if b == 0:
                    prev_rows = pl.ds((nq - 1) * Rb + r, BLK, stride=d)
                    ck[prev_rows, :] += dk[:BLK]
                    cv[prev_rows, :] += dv[:BLK]
                else:
                    tk[before, :] += dk[:BLK]
                    tv[before, :] += dv[:BLK]

            for b in range(nq):
                if d == 1:
                    one(b, 0)
                else:
                    lax.fori_loop(0, d, lambda r, carry, b=b: (one(b, r), carry)[1], 0, unroll=4)
            dq_out[...] = dq_ref[...].astype(bf16)
            dk_out[...] = ck[...].astype(bf16)
            dv_out[...] = cv[...].astype(bf16)
            ck[...] = tk[...]
            cv[...] = tv[...]

        @pl.when(n == nb)
        def _():
            dk_out[...] = ck[...].astype(bf16)
            dv_out[...] = cv[...].astype(bf16)

    last = nb - 1

    def blk(cb, prev):
        if prev:
            return pl.BlockSpec((Rb, 128), lambda hp, n: (jnp.maximum(jnp.minimum(n, last) * nq - 1, 0), cb + hp))
        return pl.BlockSpec((R, 128), lambda hp, n: (jnp.minimum(n, last), cb + hp))

    cur = pl.BlockSpec((R, 128), lambda hp, n: (jnp.minimum(n, last), hp))
    anyspec = pl.BlockSpec(memory_space=pl.ANY)
    dqo = pl.BlockSpec((R, 128), lambda hp, n: (jnp.minimum(n, last), 4 * g + hp))
    dko = pl.BlockSpec((R, 128), lambda hp, n: (jnp.maximum(n - 1, 0), 4 * g + hp))
    return pl.pallas_call(
        body, name=f"attn_bwd_d{d}", grid=(4, nb + 1),
        out_shape=[SDS((S, 1536), bf16), SDS((S, 1536), bf16), SDS((S, 1536), bf16), SDS((8, BLK, 2 * BLK), f32)],
        in_specs=[blk(qb, False), blk(kb, False), blk(kb, True), blk(vb, False), blk(vb, True), cur, cur, cur,
                  pl.BlockSpec((2, BLK, 2 * BLK), lambda hp, n: (4 * g + hp, 0, 0)), anyspec, anyspec, anyspec],
        out_specs=[dqo, dko, dko, pl.BlockSpec((2, BLK, 2 * BLK), lambda hp, n: (hp, 0, 0))],
        scratch_shapes=[pltpu.VMEM((R, 128), f32)] * 5 + _qkv_scratch(R, Rb),
        input_output_aliases={9: 0, 10: 1, 11: 2},
        compiler_params=_cp("arbitrary", "arbitrary"),
    )(qkv, qkv, qkv, qkv, qkv, do, o, lse, bias, dq_all, dk_all, dv_all)


def _conv_z(cc, ch, hc, hh, cw_ref, first):
    halo = jnp.where(first, 0.0, hc.astype(f32) * hh.astype(f32))
    T = jnp.concatenate([halo, cc * ch], axis=0)
    z = cw_ref[2:3, :] * T + cw_ref[1:2, :] * pltpu.roll(T, 1, 0) + cw_ref[0:1, :] * pltpu.roll(T, 2, 0)
    return T, z[HALO:]


def _rest_specs(tm, with_next):
    per = tm // HALO
    specs = [pl.BlockSpec((tm, D), functools.partial(lambda i, k: (i, k), k=k)) for k in range(5)]
    specs += [pl.BlockSpec((HALO, D), functools.partial(lambda i, k: (jnp.maximum(i * per - 1, 0), k), k=k)) for k in (1, 2)]
    return specs


def _mix_out_fwd(x, mod9, rest, ogs, lgs, cw, wco, wao, wo):
    S = x.shape[0]
    tm = TMXF

    def body(x_ref, mod_ref, cb_ref, cc_ref, ch_ref, gc_ref, ga_ref, hc_ref, hh_ref,
             o0, o1, o2, l0, l1, l2, cw_ref, wco_ref, wao_ref, wo_ref,
             xo_ref, o_ref, lse_ref, yc_ref, ya_ref, out_ref):
        i = pl.program_id(0)
        lv = [l0[...], l1[...], l2[...]]
        mx = jnp.maximum(jnp.maximum(lv[0], lv[1]), lv[2])
        es = [jnp.exp(l - mx) for l in lv]
        den = es[0] + es[1] + es[2]
        o = (es[0] / den) * o0[...] + (es[1] / den) * o1[...] + (es[2] / den) * o2[...]
        o_ref[...] = o
        lse_ref[...] = mx + jnp.log(den)
        _, z = _conv_z(cc_ref[...].astype(f32), ch_ref[...].astype(f32), hc_ref[...], hh_ref[...], cw_ref, i == 0)
        p = (cb_ref[...].astype(f32) * z).astype(bf16)
        yc = _dot(p, wco_ref[...])
        ya = _dot(o.astype(bf16), wao_ref[...])
        yc_ref[...] = yc.astype(bf16)
        ya_ref[...] = ya.astype(bf16)
        merged = _sigmoid(gc_ref[...].astype(f32)) * yc + _sigmoid(ga_ref[...].astype(f32)) * ya
        out = _dot(merged.astype(bf16), wo_ref[...])
        out_ref[...] = out.astype(bf16)
        xo_ref[...] = x_ref[...] + mod_ref[5:6, :] * out

    row = pl.BlockSpec((tm, D), lambda i: (i, 0))
    att = pl.BlockSpec((tm, 512), lambda i: (i, 0))
    full = lambda shp: pl.BlockSpec(shp, lambda i: (0, 0))
    return pl.pallas_call(
        body, name="mix_out_fwd", grid=(S // tm,),
        out_shape=[SDS((S, D), f32), SDS((S, 512), f32), SDS((S, 512), f32), SDS((S, D), bf16), SDS((S, D), bf16), SDS((S, D), bf16)],
        in_specs=[row, full((9, D))] + _rest_specs(tm, False) + [att] * 6 + [full((3, D)), full((D, D)), full((512, D)), full((D, D))],
        out_specs=[row, att, att, row, row, row],
        compiler_params=_cp("arbitrary"),
    )(x, mod9, *([rest] * 7), *ogs, *lgs, cw, wco, wao, wo)


def _mix_out_bwd(dxo, mod9, outv, yc, ya, rest, o, cw, wco, wao, wo):
    S = dxo.shape[0]
    tm = TMXF
    ni = S // tm

    def body(dxo_ref, mod_ref, out_ref, yc_ref, ya_ref, cb_ref, cc_ref, ch_ref, gc_ref, ga_ref, hc_ref, hh_ref,
             o_ref, cw_ref, wco_ref, wao_ref, wo_ref,
             dp_ref, dg2_ref, do_ref, dwco_ref, dwao_ref, dwo_ref, sm_ref, aco, aao, ao):
        i = pl.program_id(0)

        @pl.when(i == 0)
        def _():
            sm_ref[...] = jnp.zeros_like(sm_ref)
            aco[...] = jnp.zeros_like(aco)
            aao[...] = jnp.zeros_like(aao)
            ao[...] = jnp.zeros_like(ao)

        dxo_v = dxo_ref[...]
        sm_ref[2:3, :] += jnp.sum(out_ref[...].astype(f32) * dxo_v, axis=0, keepdims=True)
        dout = (mod_ref[5:6, :] * dxo_v).astype(bf16)
        dmerged = _dot_nt(dout, wo_ref[...])
        sc, sa = _sigmoid(gc_ref[...].astype(f32)), _sigmoid(ga_ref[...].astype(f32))
        ycv, yav = yc_ref[...].astype(f32), ya_ref[...].astype(f32)
        ao[...] += _dot_tn((sc * ycv + sa * yav).astype(bf16), dout)
        dyc = (dmerged * sc).astype(bf16)
        dya = (dmerged * sa).astype(bf16)
        dg2_ref[:, :D] = (dmerged * ycv * sc * (1.0 - sc)).astype(bf16)
        dg2_ref[:, D:] = (dmerged * yav * sa * (1.0 - sa)).astype(bf16)
        dp_ref[...] = _dot_nt(dyc, wco_ref[...]).astype(bf16)
        _, z = _conv_z(cc_ref[...].astype(f32), ch_ref[...].astype(f32), hc_ref[...], hh_ref[...], cw_ref, i == 0)
        aco[...] += _dot_tn((cb_ref[...].astype(f32) * z).astype(bf16), dyc)
        do_ref[...] = _dot_nt(dya, wao_ref[...])
        aao[...] += _dot_tn(o_ref[...].astype(bf16), dya)

        @pl.when(i == ni - 1)
        def _():
            dwco_ref[...] = aco[...].astype(bf16)
            dwao_ref[...] = aao[...].astype(bf16)
            dwo_ref[...] = ao[...].astype(bf16)

    row = pl.BlockSpec((tm, D), lambda i: (i, 0))
    att = pl.BlockSpec((tm, 512), lambda i: (i, 0))
    full = lambda shp: pl.BlockSpec(shp, lambda i: (0, 0))
    once = lambda shp: pl.BlockSpec(shp, lambda i: (0, 0), pipeline_mode=pl.Buffered(1))
    return pl.pallas_call(
        body, name="mix_out_bwd", grid=(ni,),
        out_shape=[SDS((S, D), bf16), SDS((S, 2 * D), bf16), SDS((S, 512), f32),
                   SDS((D, D), bf16), SDS((512, D), bf16), SDS((D, D), bf16), SDS((8, D), f32)],
        in_specs=[row, full((9, D)), row, row, row] + _rest_specs(tm, False)
                 + [att, full((3, D)), once((D, D)), once((512, D)), once((D, D))],
        out_specs=[row, pl.BlockSpec((tm, 2 * D), lambda i: (i, 0)), att, full((D, D)), full((512, D)), full((D, D)), full((8, D))],
        scratch_shapes=[pltpu.VMEM((D, D), f32), pltpu.VMEM((512, D), f32), pltpu.VMEM((D, D), f32)],
        compiler_params=_cp("arbitrary"),
    )(dxo, mod9, outv, yc, ya, *([rest] * 7), o, cw, wco, wao, wo)


def _conv_bwd(dp, rest, cw):
    S = dp.shape[0]
    tm = TM
    per = tm // HALO
    nh = S // HALO
    ni = S // tm

    def body(dp_ref, dpn_ref, cb_ref, cbn_ref, cc_ref, ch_ref, hc_ref, hh_ref, cw_ref, d3_ref, sm_ref):
        i = pl.program_id(0)

        @pl.when(i == 0)
        def _():
            sm_ref[...] = jnp.zeros_like(sm_ref)

        cc, ch = cc_ref[...].astype(f32), ch_ref[...].astype(f32)
        T, z = _conv_z(cc, ch, hc_ref[...], hh_ref[...], cw_ref, i == 0)
        dpv = dp_ref[...].astype(f32)
        cbv = cb_ref[...].astype(f32)
        dz = dpv * cbv
        dzn = jnp.where(i == ni - 1, 0.0, dpn_ref[...].astype(f32) * cbn_ref[...].astype(f32))
        E = jnp.concatenate([dz, dzn], axis=0)
        ne = tm + HALO
        dT = cw_ref[2:3, :] * E + cw_ref[1:2, :] * pltpu.roll(E, ne - 1, 0) + cw_ref[0:1, :] * pltpu.roll(E, ne - 2, 0)
        dT = dT[:tm]
        d3_ref[:, :D] = (dpv * z).astype(bf16)
        d3_ref[:, D:2 * D] = (dT * ch).astype(bf16)
        d3_ref[:, 2 * D:] = (dT * cc).astype(bf16)
        sm_ref[2:3, :] += jnp.sum(dz * T[HALO:], axis=0, keepdims=True)
        sm_ref[1:2, :] += jnp.sum(dz * pltpu.roll(T, 1, 0)[HALO:], axis=0, keepdims=True)
        sm_ref[0:1, :] += jnp.sum(dz * pltpu.roll(T, 2, 0)[HALO:], axis=0, keepdims=True)

    row = pl.BlockSpec((tm, D), lambda i: (i, 0))
    nxt = pl.BlockSpec((HALO, D), lambda i: (jnp.minimum((i + 1) * per, nh - 1), 0))
    col = lambda k: pl.BlockSpec((tm, D), lambda i: (i, k))
    prv = lambda k: pl.BlockSpec((HALO, D), lambda i: (jnp.maximum(i * per - 1, 0), k))
    return pl.pallas_call(
        body, name="conv_bwd", grid=(ni,),
        out_shape=[SDS((S, 3 * D), bf16), SDS((8, D), f32)],
        in_specs=[row, nxt, col(0), nxt, col(1), col(2), prv(1), prv(2), pl.BlockSpec((3, D), lambda i: (0, 0))],
        out_specs=[pl.BlockSpec((tm, 3 * D), lambda i: (i, 0)), pl.BlockSpec((8, D), lambda i: (0, 0))],
        compiler_params=_cp("arbitrary"),
    )(dp, dp, rest, rest, rest, rest, rest, rest, cw)


_DU_RANGES = ((0, 3), (3, 6), (6, 9), (9, 15), (15, 19))
N_CBLK = IN_W // CB


def _mix_in_bwd_dh(dxo, x, mod9, g3, dus, win):
    S = x.shape[0]

    def body(dxo_ref, x_ref, mod_ref, g_ref, s0, s1, s2, s3, s4, w_ref, dxi_ref, sm_ref, acc):
        i, kb = pl.program_id(0), pl.program_id(1)

        @pl.when((i == 0) & (kb == 0))
        def _():
            sm_ref[...] = jnp.zeros_like(sm_ref)

        @pl.when(kb == 0)
        def _():
            acc[...] = jnp.zeros_like(acc)

        for src, (lo, hi) in zip((s0, s1, s2, s3, s4), _DU_RANGES):
            @pl.when((kb >= lo) & (kb < hi))
            def _(src=src):
                acc[...] += _dot_nt(src[...].astype(bf16), w_ref[...])

        @pl.when(kb == N_CBLK - 1)
        def _():
            g, scale = g_ref[1:2, :], mod_ref[4:5, :]
            _, xhat, rstd = _norm_fwd(x_ref[...], g, mod_ref[3:4, :], scale)
            dx, dshift, dscale, dg = _norm_bwd(acc[...], xhat, rstd, g, scale)
            dxi_ref[...] = dxo_ref[...] + dx
            sm_ref[0:1, :] += dshift
            sm_ref[1:2, :] += dscale
            sm_ref[3:4, :] += dg

    row = pl.BlockSpec((TMP, D), lambda i, kb: (i, 0))

    def src_spec(lo, hi):
        return pl.BlockSpec((TMP, CB), lambda i, kb: (i, jnp.clip(kb - lo, 0, hi - lo - 1)))

    return pl.pallas_call(
        body, name="mix_in_bwd_dh", grid=(S // TMP, N_CBLK),
        out_shape=[SDS((S, D), f32), SDS((8, D), f32)],
        in_specs=[row, row, pl.BlockSpec((9, D), lambda i, kb: (0, 0)), pl.BlockSpec((3, D), lambda i, kb: (0, 0))]
                 + [src_spec(lo, hi) for lo, hi in _DU_RANGES] + [pl.BlockSpec((D, CB), lambda i, kb: (0, kb))],
        out_specs=[row, pl.BlockSpec((8, D), lambda i, kb: (0, 0))],
        scratch_shapes=[pltpu.VMEM((TMP, D), f32)],
        compiler_params=_cp("arbitrary", "arbitrary"),
    )(dxo, x, mod9, g3, *dus, win)


def _mix_in_bwd_dw(h, dus):
    S = h.shape[0]
    ni = S // TMW

    def body(h_ref, s0, s1, s2, s3, s4, dw_ref, acc):
        kb, i = pl.program_id(0), pl.program_id(1)

        @pl.when(i == 0)
        def _():
            acc[...] = jnp.zeros_like(acc)

        for src, (lo, hi) in zip((s0, s1, s2, s3, s4), _DU_RANGES):
            @pl.when((kb >= lo) & (kb < hi))
            def _(src=src):
                rows = pl.ds(pl.multiple_of(i * TMW, TMW), TMW)
                acc[...] += _dot_tn(h_ref[rows, :], src[...].astype(bf16))

        @pl.when(i == ni - 1)
        def _():
            dw_ref[...] = acc[...].astype(bf16)

    def src_spec(lo, hi):
        def imap(kb, i):
            on = (kb >= lo) & (kb < hi)
            return (jnp.where(on, i, 0), jnp.clip(kb - lo, 0, hi - lo - 1))
        return pl.BlockSpec((TMW, CB), imap)

    return pl.pallas_call(
        body, name="mix_in_bwd_dw", grid=(N_CBLK, ni),
        out_shape=SDS((D, IN_W), bf16),
        in_specs=[pl.BlockSpec((S, D), lambda kb, i: (0, 0))] + [src_spec(lo, hi) for lo, hi in _DU_RANGES],
        out_specs=pl.BlockSpec((D, CB), lambda kb, i: (0, kb)),
        scratch_shapes=[pltpu.VMEM((D, CB), f32)],
        compiler_params=_cp("arbitrary", "arbitrary"),
    )(h, *dus)


def _loss_head(x, fg, tgt):
    S = x.shape[0]

    def body(x_ref, g_ref, t_ref, ls_ref, dx_ref, sm_ref):
        i = pl.program_id(0)

        @pl.when(i == 0)
        def _():
            ls_ref[...] = jnp.zeros_like(ls_ref)
            sm_ref[...] = jnp.zeros_like(sm_ref)

        xv, g = x_ref[...], g_ref[...]
        rstd = lax.rsqrt(jnp.mean(xv * xv, axis=-1, keepdims=True) + EPS)
        xhat = xv * rstd
        e = xhat * g - t_ref[...]
        ls_ref[...] += 0.5 * jnp.sum(jnp.mean(e * e, axis=-1, keepdims=True))
        dy = e * (1.0 / D)
        sm_ref[0:1, :] += jnp.sum(dy * xhat, axis=0, keepdims=True)
        dxh = dy * g
        dx_ref[...] = rstd * (dxh - xhat * jnp.mean(dxh * xhat, axis=-1, keepdims=True))

    row = pl.BlockSpec((TM, D), lambda i: (i, 0))
    return pl.pallas_call(
        body, name="loss_head", grid=(S // TM,),
        out_shape=[SDS((8, 128), f32), SDS((S, D), f32), SDS((8, D), f32)],
        in_specs=[row, pl.BlockSpec((1, D), lambda i: (0, 0)), row],
        out_specs=[pl.BlockSpec((8, 128), lambda i: (0, 0)), row, pl.BlockSpec((8, D), lambda i: (0, 0))],
        compiler_params=_cp("arbitrary"),
    )(x, fg, tgt)


def _adam(w, g, m, v):
    m2 = B1 * m + (1.0 - B1) * g
    v2 = B2 * v + (1.0 - B2) * (g * g)
    delta = -LR * ((m2 / BC1) / (jnp.sqrt(v2 / BC2) + AEPS) + WD * w)
    return delta, m2, v2


def _row_tile(rows, cols):
    for tr in (512, 352, 256, 128, 64):
        if rows % tr == 0 and tr * cols * 4 <= (3 << 19):
            return tr
    raise ValueError((rows, cols))


def _sum_slots(land):
    _, R, C = land.shape
    tr = _row_tile(R, C)

    def body(l_ref, t_ref):
        t = l_ref[0].astype(f32)
        for k in range(1, N_CHIPS):
            t = t + l_ref[k].astype(f32)
        t_ref[...] = t

    return pl.pallas_call(
        body, name="sum_slots", grid=(R // tr,),
        out_shape=SDS((R, C), f32),
        in_specs=[pl.BlockSpec((N_CHIPS, tr, C), lambda i: (0, i, 0))],
        out_specs=pl.BlockSpec((tr, C), lambda i: (i, 0)),
        compiler_params=_cp("arbitrary"),
    )(land)


def _adamw_pair(w2, m2, v2, ta, tb, outs, slot):
    R, C = ta.shape
    tr = _row_tile(R, C)
    nrt = R // tr

    def body(w_ref, m_ref, v_ref, ta_ref, tb_ref, g_in, d_in, m_in, v_in, g_ref, d_ref, mo_ref, vo_ref):
        g = ta_ref[...].astype(f32) + tb_ref[...].astype(f32)
        delta, mn, vn = _adam(w_ref[...], g, m_ref[...], v_ref[...])
        g_ref[...] = g
        d_ref[...] = delta
        mo_ref[...] = mn
        vo_ref[...] = vn

    big = pl.BlockSpec((tr, C), lambda i: (slot * nrt + i, 0))
    loc = pl.BlockSpec((tr, C), lambda i: (i, 0))
    anyspec = pl.BlockSpec(memory_space=pl.ANY)
    return pl.pallas_call(
        body, name="adamw_pair", grid=(nrt,),
        out_shape=[SDS(o.shape, f32) for o in outs],
        in_specs=[big, big, big, loc, loc] + [anyspec] * 4,
        out_specs=[big] * 4,
        input_output_aliases={5: 0, 6: 1, 7: 2, 8: 3},
        compiler_params=_cp("arbitrary"),
    )(w2, m2, v2, ta, tb, *outs)


def _adamw_small(w, g, m, v):
    def body(w_ref, g_ref, m_ref, v_ref, d_ref, mo_ref, vo_ref):
        delta, mn, vn = _adam(w_ref[...], g_ref[...], m_ref[...], v_ref[...])
        d_ref[...] = delta
        mo_ref[...] = mn
        vo_ref[...] = vn

    return pl.pallas_call(body, name="adamw_small", out_shape=[SDS(w.shape, f32)] * 3)(w, g, m, v)


def _ada_w_update(cs_all, dmod_sh, w, m, v):
    tr = 256

    def body(cs_ref, dm_ref, w_ref, m_ref, v_ref, g_ref, d_ref, mo_ref, vo_ref):
        g = _dot_tn(cs_ref[...].astype(bf16), dm_ref[...].astype(bf16))
        delta, mn, vn = _adam(w_ref[...], g, m_ref[...], v_ref[...])
        g_ref[...] = g
        d_ref[...] = delta
        mo_ref[...] = mn
        vo_ref[...] = vn

    blk = pl.BlockSpec((None, tr, ADA_SH), lambda l, i: (l, i, 0))
    return pl.pallas_call(
        body, name="ada_w_update", grid=(DEPTH, D // tr),
        out_shape=[SDS(w.shape, f32)] * 4,
        in_specs=[pl.BlockSpec((8, tr), lambda l, i: (0, i)), pl.BlockSpec((None, 8, ADA_SH), lambda l, i: (l, 0, 0)), blk, blk, blk],
        out_specs=[blk] * 4,
        compiler_params=_cp("arbitrary", "arbitrary"),
    )(cs_all, dmod_sh, w, m, v)


def _sum_devices(gathered):
    _, R, C = gathered.shape

    def body(g_ref, o_ref):
        t = g_ref[0]
        for k in range(1, 8):
            t = t + g_ref[k]
        o_ref[...] = t

    return pl.pallas_call(body, name="sum_devices", out_shape=SDS((R, C), f32))(gathered)


def _layer_fwd(x, mod9, g3, cw, getw, bias):
    W = {}

    def take(gname, after, mod9):
        w, tok = getw(gname, after)
        W.update(w)
        return mod9 if tok is None else mod9 + tok[0, 0]

    mod9 = take("A", x, mod9)
    x1, h1, a1, u1, hid1, y1 = _ffn_fwd(x, mod9, g3, W["wg0"], W["wu0"], W["wd0"], 0)
    mod9 = take("B", x1, mod9)
    hm, qkv = _mix_qkv(x1, mod9, g3, W["win"])
    rest = _mix_rest(hm, W["win"])
    ogs, lgs = [], []
    for g in range(3):
        og, lg = _attn_fwd(qkv, bias, g)
        ogs.append(og)
        lgs.append(lg)
    mod9 = take("C", ogs[2], mod9)
    x2, o, lse, yc, ya, outv = _mix_out_fwd(x1, mod9, rest, ogs, lgs, cw, W["wco"], W["wao"], W["wo"])
    mod9 = take("D", x2, mod9)
    x3, h3, a3, u3, hid3, y3 = _ffn_fwd(x2, mod9, g3, W["wg1"], W["wu1"], W["wd1"], 2)
    saved = dict(x0=x, x1=x1, x2=x2, h1=h1, a1=a1, u1=u1, hid1=hid1, y1=y1, hm=hm, qkv=qkv, rest=rest, o=o, lse=lse, yc=yc, ya=ya,
                 outv=outv, h3=h3, a3=a3, u3=u3, hid3=hid3, y3=y3)
    return x3, saved, W


def _layer_bwd(dx, sv, mod9, g3, cw, W, bias, emit):
    S = dx.shape[0]
    dw = {}

    def send(gname, mod9):
        tok = emit(gname, dw)
        return mod9 if tok is None else mod9 + tok[0, 0]

    dx2, da, du, dy, sm3 = _ffn_bwd1(dx, sv["x2"], mod9, g3, sv["y3"], sv["a3"], sv["u3"], W["wg1"], W["wu1"], W["wd1"], 2)
    dw["wg1"], dw["wu1"], dw["wd1"] = _ffn_bwd2(sv["h3"], da, du, sv["hid3"], dy)
    mod9 = send("D", mod9)
    dp, dg2, do, dw["wco"], dw["wao"], dw["wo"], smo = _mix_out_bwd(
        dx2, mod9, sv["outv"], sv["yc"], sv["ya"], sv["rest"], sv["o"], cw, W["wco"], W["wao"], W["wo"])
    mod9_c = send("C", mod9)
    cw = cw + (mod9_c - mod9)[0:1, :]
    mod9 = mod9_c
    d3, smc = _conv_bwd(dp, sv["rest"], cw)
    dq = lax.empty((S, 1536), bf16)
    dk = lax.empty((S, 1536), bf16)
    dv = lax.empty((S, 1536), bf16)
    dsaccs = []
    for g in range(3):
        dq, dk, dv, dsg = _attn_bwd(sv["qkv"], do, sv["o"], sv["lse"], bias, dq, dk, dv, g)
        dsaccs.append(dsg)
    dus = (dq, dk, dv, d3, dg2)
    dx1, smm = _mix_in_bwd_dh(dx2, sv["x1"], mod9, g3, dus, W["win"])
    dw["win"] = _mix_in_bwd_dw(sv["hm"], dus)
    mod9 = send("B", mod9)
    dx0, da, du, dy, sm1 = _ffn_bwd1(dx1, sv["x0"], mod9, g3, sv["y1"], sv["a1"], sv["u1"], W["wg0"], W["wu0"], W["wd0"], 0)
    dw["wg0"], dw["wu0"], dw["wd0"] = _ffn_bwd2(sv["h1"], da, du, sv["hid1"], dy)
    send("A", mod9)
    dmod = jnp.concatenate([sm1[0:3], smm[0:2], smo[2:3], sm3[0:3]], axis=0)
    dng = jnp.concatenate([sm1[3:4], smm[3:4], sm3[3:4]], axis=0)
    return dx0, dmod, dng, smc[0:3], jnp.concatenate(dsaccs, axis=0)


def _chip_cols(a, chip, width):
    return lax.dynamic_slice_in_dim(a, chip * width, width, axis=a.ndim - 1)


def kernel(x, c, ada_w, ada_b, norm_g, ffn_w_gate, ffn_w_up, ffn_w_down, w_in, conv_w, w_conv_out, w_attn_out, w_o, rel_bias, final_g, loss_target, m_ada_w, m_ada_b, m_norm_g, m_ffn_w_gate, m_ffn_w_up, m_ffn_w_down, m_w_in, m_conv_w, m_w_conv_out, m_w_attn_out, m_w_o, m_rel_bias, m_final_g, v_ada_w, v_ada_b, v_norm_g, v_ffn_w_gate, v_ffn_w_up, v_ffn_w_down, v_w_in, v_conv_w, v_w_conv_out, v_w_attn_out, v_w_o, v_rel_bias, v_final_g):
    ix, iy, ic = lax.axis_index("x"), lax.axis_index("y"), lax.axis_index("c")
    chip = 2 * ix + iy
    dev = 4 * ix + 2 * iy + ic
    xs = x.reshape(x.shape[1:])
    S = xs.shape[0]
    qd = D // N_CHIPS

    chip_arr = jnp.reshape(chip, (1,)).astype(jnp.int32)
    names = [w[0] for w in WCLASSES]

    tr2 = lambda a: jnp.swapaxes(a, -1, -2)
    wg_t, wu_t = tr2(ffn_w_gate), tr2(ffn_w_up)

    def layer_shards(l):
        return [(wg_t, (l, 0)), (wu_t, (l, 0)), (ffn_w_down, (l, 0)), (wg_t, (l, 1)), (wu_t, (l, 1)),
                (ffn_w_down, (l, 1)), (w_in, (l,)), (w_conv_out, (l,)), (w_attn_out, (l,)), (w_o, (l,))]

    started = {}
    extra_starts = {(0, "A"): [(0, "B")], (0, "B"): [(0, "C"), (0, "D"), (1, "A")]}

    casts = {}

    def cast_group(l, gname, after):
        shards = layer_shards(l)
        casts[(l, gname)] = _gather_group_cast(GROUPS[gname], [shards[q] for q in GROUPS[gname]], chip_arr, after)

    def start_gather(l, gname, after):
        started[(l, gname)] = _gather_group_start(f"l{l}{gname}", GROUPS[gname], casts[(l, gname)], after)
        return started[(l, gname)][-1]

    pad8 = lambda a: jnp.pad(a, ((0, -a.shape[0] % 8), (0, 0)))
    pack = jnp.concatenate([pad8(c), pad8(norm_g.reshape(3, D)), pad8(conv_w.reshape(3, D))], axis=0)
    g1 = _allgather_small(pack).reshape(8, 24, D)
    c_all = g1[:, 0]
    by_chip = g1[0::2]
    ng_full = jnp.concatenate([by_chip[j, 8:11].reshape(DEPTH, 3, qd) for j in range(N_CHIPS)], axis=-1)
    cw_full = jnp.concatenate([by_chip[j, 16:19].reshape(DEPTH, 3, qd) for j in range(N_CHIPS)], axis=-1)
    mod_sh, cs_all = _mod_shards(c_all, ada_w, _chip_cols(ada_b, chip, ADA_SH))
    g2 = _allgather_small(mod_sh.reshape(DEPTH * 8, ADA_SH)).reshape(8, DEPTH, 8, ADA_SH)
    mine = lax.dynamic_index_in_dim(g2[0::2], dev, axis=2, keepdims=False)
    mod = jnp.transpose(mine, (1, 0, 2)).reshape(DEPTH, 9, D)

    cast_group(0, "A", c)
    tok0 = start_gather(0, "A", mod)
    for l in range(DEPTH):
        for gname in GROUPS:
            if (l, gname) not in casts:
                cast_group(l, gname, tok0)
    buckets = jnp.asarray(_bucket_table())
    bias = _bias_blocks(rel_bias, buckets)
    last_cast = casts[(DEPTH - 1, "D")][-1]

    need_order = [(l, gname) for l in range(DEPTH) for gname in GROUPS]
    forwarded = {}

    def forward_gather(key, after):
        forwarded[key] = _gather_group_forward(f"l{key[0]}{key[1]}", GROUPS[key[1]], started[key], after)
        return forwarded[key][-1]

    def make_getw(l):
        def getw(gname, after):
            key = (l, gname)
            if key == (0, "A"):
                after = last_cast
            if key not in forwarded:
                after = forward_gather(key, after)
            full = _gather_group_wait(f"l{l}{gname}", GROUPS[gname], started[key][0], forwarded[key], after)
            tok = None
            before = set(started)
            for nl, ng in extra_starts.get(key, []) + [(l + 1, gname)]:
                if nl < DEPTH and (nl, ng) not in started:
                    tok = start_gather(nl, ng, full[0] if tok is None else tok)
            at = need_order.index(key) + 1
            if at < len(need_order) and need_order[at] in before and need_order[at] not in forwarded:
                tok = forward_gather(need_order[at], full[0] if tok is None else tok)
            return {names[q]: f for q, f in zip(GROUPS[gname], full)}, tok
        return getw

    Ws, saves = [], []
    xc = xs
    for l in range(DEPTH):
        xc, sv, W = _layer_fwd(xc, mod[l], ng_full[l], cw_full[l], make_getw(l), bias)
        Ws.append(W)
        saves.append(sv)

    ls, dx, smf = _loss_head(xc, final_g.reshape(1, D), loss_target.reshape(loss_target.shape[1:]))
    loss = lax.psum(ls[0, 0], ("x", "y", "c"))

    params = dict(wg=wg_t, wu=wu_t, wd=ffn_w_down, win=w_in, wco=w_conv_out, wao=w_attn_out, wo=w_o)
    moms = dict(wg=tr2(m_ffn_w_gate), wu=tr2(m_ffn_w_up), wd=m_ffn_w_down, win=m_w_in, wco=m_w_conv_out, wao=m_w_attn_out, wo=m_w_o)
    vars_ = dict(wg=tr2(v_ffn_w_gate), wu=tr2(v_ffn_w_up), wd=v_ffn_w_down, win=v_w_in, wco=v_w_conv_out, wao=v_w_attn_out, wo=v_w_o)
    flat = lambda a: a.reshape(-1, a.shape[-1])
    big_out = {k: [lax.empty(flat(p).shape, f32) for _ in range(4)] for k, p in params.items()}
    dmods, dngs, dcws, dsaccs = [None] * DEPTH, [None] * DEPTH, [None] * DEPTH, [None] * DEPTH

    def finish(l, gname, started, after):
        group = GROUPS[gname]
        pieces, lands = _scatter_group_wait(f"l{l}{gname}", group, started, after)
        ts = [_sum_own_slots(pieces[i], lands[i], *_cls(q), chip_arr) for i, q in enumerate(group)]
        tsib = _swap_sibling(ts)
        for i, q in enumerate(group):
            name = names[q]
            key = name.rstrip("01")
            slot = 2 * l + int(name[-1]) if name[-1] in "01" else l
            big_out[key] = _adamw_pair(flat(params[key]), flat(moms[key]), flat(vars_[key]), ts[i], tsib[i], big_out[key], slot)

    pending, tok = [], None
    for l in reversed(range(DEPTH)):
        modl = mod[l] if tok is None else mod[l] + tok[0, 0]
        mine = []

        def emit(gname, dw, l=l, mine=mine):
            prev = mine[-1][2][-1] if mine else dx
            mine.append((l, gname, _scatter_group_start(f"l{l}{gname}", GROUPS[gname], [dw[names[q]] for q in GROUPS[gname]], prev)))
            return mine[-1][2][-1]

        dx, dmods[l], dngs[l], dcws[l], dsaccs[l] = _layer_bwd(dx, saves[l], modl, ng_full[l], cw_full[l], Ws[l], bias, emit)
        for pl_, pg, pst in pending:
            finish(pl_, pg, pst, dx)
        pending, tok = mine, mine[-1][2][-1]
    for pl_, pg, pst in pending[:-1]:
        finish(pl_, pg, pst, pending[-1][2][-1])

    drb = jnp.transpose(_bias_grad(dsaccs, buckets)[:, 0, :NUM_BUCKETS])
    drb_row = jnp.pad(drb.reshape(1, NUM_BUCKETS * 24), ((0, 0), (0, D - NUM_BUCKETS * 24)))
    pack2 = jnp.concatenate([pad8(a) for a in dmods] + [pad8(a) for a in dngs] + [pad8(a) for a in dcws] + [smf, pad8(drb_row)], axis=0)
    n_rows = pack2.shape[0]
    g3 = _allgather_small(pack2).reshape(8, n_rows, D)
    tot = _sum_devices(g3)
    o_ng, o_cw, o_fg, o_rb = 16 * DEPTH, 24 * DEPTH, 32 * DEPTH, 32 * DEPTH + 8
    g_ada_b = jnp.stack([tot[16 * l:16 * l + 9] for l in range(DEPTH)]).reshape(DEPTH, 9 * D)
    g_norm_g = _chip_cols(jnp.stack([tot[o_ng + 8 * l:o_ng + 8 * l + 3] for l in range(DEPTH)]), chip, qd)
    g_conv_w = _chip_cols(jnp.stack([tot[o_cw + 8 * l:o_cw + 8 * l + 3] for l in range(DEPTH)]), chip, qd)
    g_final_g = tot[o_fg]
    g_rel_bias = tot[o_rb, :NUM_BUCKETS * 24].reshape(NUM_BUCKETS, 24)
    dmod_all = jnp.stack([g3[:, 16 * l:16 * l + 9].reshape(8, 9 * D) for l in range(DEPTH)])
    dmod_sh = _chip_cols(dmod_all, chip, ADA_SH)
    g_ada_w, d_ada_w, nm_ada_w, nv_ada_w = _ada_w_update(cs_all, dmod_sh, ada_w, m_ada_w, v_ada_w)

    def small(w, g, m, v):
        shp = w.shape
        to2 = lambda a: a.reshape(-1, shp[-1])
        return [o.reshape(shp) for o in _adamw_small(to2(w), to2(g), to2(m), to2(v))]

    d_ada_b, nm_ada_b, nv_ada_b = small(ada_b, g_ada_b, m_ada_b, v_ada_b)
    d_norm_g, nm_norm_g, nv_norm_g = small(norm_g, g_norm_g, m_norm_g, v_norm_g)
    d_conv_w, nm_conv_w, nv_conv_w = small(conv_w, g_conv_w, m_conv_w, v_conv_w)
    d_rel_bias, nm_rel_bias, nv_rel_bias = small(rel_bias, g_rel_bias, m_rel_bias, v_rel_bias)
    d_final_g, nm_final_g, nv_final_g = small(final_g, g_final_g, m_final_g, v_final_g)

    behind = nv_ada_w[0, 0:8, 0:128]
    for key in big_out:
        behind = behind + big_out[key][3][0:8, 0:128]
    finish(*pending[-1], behind)

    def big(key, which):
        out = big_out[key][which].reshape(params[key].shape)
        return tr2(out) if key in ("wg", "wu") else out

    grads = [g_ada_w, g_ada_b, g_norm_g, big("wg", 0), big("wu", 0), big("wd", 0), big("win", 0), g_conv_w, big("wco", 0),
             big("wao", 0), big("wo", 0), g_rel_bias, g_final_g]
    deltas = [d_ada_w, d_ada_b, d_norm_g, big("wg", 1), big("wu", 1), big("wd", 1), big("win", 1), d_conv_w, big("wco", 1),
              big("wao", 1), big("wo", 1), d_rel_bias, d_final_g]
    new_m = [nm_ada_w, nm_ada_b, nm_norm_g, big("wg", 2), big("wu", 2), big("wd", 2), big("win", 2), nm_conv_w, big("wco", 2),
             big("wao", 2), big("wo", 2), nm_rel_bias, nm_final_g]
    new_v = [nv_ada_w, nv_ada_b, nv_norm_g, big("wg", 3), big("wu", 3), big("wd", 3), big("win", 3), nv_conv_w, big("wco", 3),
             big("wao", 3), big("wo", 3), nv_rel_bias, nv_final_g]
    return (loss, dx.reshape(x.shape), *grads, *deltas, *new_m, *new_v)
```
